```python
import math
import jax
import jax.numpy as jnp
from jax import lax
import numpy as np

D_MODEL = 2048
BATCH = 8
SEQ = 2048
DEPTH = 1

GRID_W = 64
CTX_LEN = 256
HEAD_DIM = 128
N_Q_HEADS = D_MODEL // HEAD_DIM
N_KV_HEADS = N_Q_HEADS // 4
Q_PER_KV = N_Q_HEADS // N_KV_HEADS
ATTN_W = N_Q_HEADS * HEAD_DIM
KV_W = N_KV_HEADS * HEAD_DIM
ROPE_AXIS_DIM = HEAD_DIM // 2
ROPE_THETA = 10000.0
Q_BLOCK = 128
ATTN_SCALE = HEAD_DIM ** -0.5
SSM_W = D_MODEL // 2
SSM_GROUP = 16
SSM_GROUPS = SSM_W // SSM_GROUP
SSM_STATE = 64
DT_MIN = 1e-3
DT_MAX = 1e-1
D_FF = ((8 * D_MODEL // 3 + 255) // 256) * 256
N_MOD = 9
N_MOD_CTX_LAST = 5
NORM_EPS = 1e-6
CTX_IN_W = 2 * KV_W + SSM_W
IN_W = CTX_IN_W + ATTN_W + 2 * D_MODEL
SPLITS = [KV_W, 2 * KV_W, CTX_IN_W, CTX_IN_W + ATTN_W]

kernel_name = 'hybrid_s5_gqa_macaron_dit_layer'


def _rms_norm(x, g):
    xf = x.astype(jnp.float32)
    xf = xf * lax.rsqrt(jnp.mean(xf * xf, axis=-1, keepdims=True) + NORM_EPS)
    return xf.astype(x.dtype) * g


def _modulate(h, shift, scale):
    return h * (1 + scale) + shift


def _swiglu(h, w_gate, w_up, w_down):
    return (jax.nn.silu(h @ w_gate) * (h @ w_up)) @ w_down


def _axial_rope_tables(L):
    rows = L // GRID_W
    row_ids = jnp.broadcast_to(jnp.arange(rows)[:, None], (rows, GRID_W)).reshape(-1)
    col_ids = jnp.broadcast_to(jnp.arange(GRID_W)[None, :], (rows, GRID_W)).reshape(-1)
    half = ROPE_AXIS_DIM // 2
    inv_freq = ROPE_THETA ** (-jnp.arange(half, dtype=jnp.float32) / half)
    ang_r = row_ids.astype(jnp.float32)[:, None, None] * inv_freq
    ang_c = col_ids.astype(jnp.float32)[:, None, None] * inv_freq
    return (jnp.cos(ang_r), jnp.sin(ang_r), jnp.cos(ang_c), jnp.sin(ang_c))


def _rope_half(x, cos, sin):
    cos = cos.astype(x.dtype)
    sin = sin.astype(x.dtype)
    x1, x2 = jnp.split(x, 2, axis=-1)
    return jnp.concatenate([x1 * cos - x2 * sin, x2 * cos + x1 * sin], axis=-1)


def _axial_rope(x, tables):
    cos_r, sin_r, cos_c, sin_c = tables
    return jnp.concatenate([_rope_half(x[..., :ROPE_AXIS_DIM], cos_r, sin_r),
                            _rope_half(x[..., ROPE_AXIS_DIM:], cos_c, sin_c)], axis=-1)


def _attend_block(qb, k, v):
    B, T = qb.shape[0], qb.shape[1]
    qg = qb.reshape(B, T, N_KV_HEADS, Q_PER_KV, HEAD_DIM)
    s = jnp.einsum('bqkrd,bskd->bkrqs', qg, k).astype(jnp.float32) * ATTN_SCALE
    p = jax.nn.softmax(s, axis=-1).astype(v.dtype)
    o = jnp.einsum('bkrqs,bskd->bqkrd', p, v)
    return o.reshape(B, T, ATTN_W)


def _blocked_attention(q, k, v):
    B, L = q.shape[0], q.shape[1]
    nb = L // Q_BLOCK
    qb = q.reshape(B, nb, Q_BLOCK, N_Q_HEADS, HEAD_DIM).swapaxes(0, 1)
    o = lax.map(lambda qi: _attend_block(qi, k, v), qb)
    return o.swapaxes(0, 1).reshape(B, L, ATTN_W)


def _zoh(a_re, a_im, log_dt):
    a_re = a_re.astype(jnp.float32)
    a_im = a_im.astype(jnp.float32)
    dt = jnp.exp(log_dt.astype(jnp.float32))[:, None]
    mag = jnp.exp(a_re * dt)
    lb_re = mag * jnp.cos(a_im * dt)
    lb_im = mag * jnp.sin(a_im * dt)
    den = a_re * a_re + a_im * a_im
    coef_re = ((lb_re - 1.0) * a_re + lb_im * a_im) / den
    coef_im = (lb_im * a_re - (lb_re - 1.0) * a_im) / den
    return lb_re, lb_im, coef_re, coef_im


def _drive(u, b_re, b_im, coef_re, coef_im):
    bu_re = jnp.einsum('blgc,gpc->blgp', u, b_re)
    bu_im = jnp.einsum('blgc,gpc->blgp', u, b_im)
    return coef_re * bu_re - coef_im * bu_im, coef_re * bu_im + coef_im * bu_re


def _combine(e1, e2):
    a1r, a1i, b1r, b1i = e1
    a2r, a2i, b2r, b2i = e2
    return (a2r * a1r - a2i * a1i,
            a2r * a1i + a2i * a1r,
            a2r * b1r - a2i * b1i + b2r,
            a2r * b1i + a2i * b1r + b2i)


def _scan(lb_re, lb_im, bu_re, bu_im, reverse, h0=None):
    L = bu_re.shape[1]
    a_re = jnp.broadcast_to(lb_re, (1, L) + lb_re.shape)
    a_im = jnp.broadcast_to(lb_im, (1, L) + lb_im.shape)
    A_re, A_im, s_re, s_im = lax.associative_scan(_combine, (a_re, a_im, bu_re, bu_im),
                                                  reverse=reverse, axis=1)
    if h0 is None:
        return s_re, s_im
    h0_re, h0_im = h0
    return (s_re + A_re * h0_re - A_im * h0_im, s_im + A_re * h0_im + A_im * h0_re)


def _readout(h_re, h_im, c_re, c_im):
    return (jnp.einsum('blgp,gcp->blgc', h_re, c_re)
            - jnp.einsum('blgp,gcp->blgc', h_im, c_im))


def _s5_mixer(u, uc, a_re, a_im, log_dt, b_re, b_im, c_re, c_im, d, ctx_out):
    B, L = u.shape[0], u.shape[1]
    Lc = uc.shape[1]
    uf = u.astype(jnp.float32).reshape(B, L, SSM_GROUPS, SSM_GROUP)
    ucf = uc.astype(jnp.float32).reshape(B, Lc, SSM_GROUPS, SSM_GROUP)
    d_g = d.astype(jnp.float32).reshape(SSM_GROUPS, SSM_GROUP)
    y = d_g * uf
    yc = d_g * ucf if ctx_out else None
    for direction, reverse in ((0, False), (1, True)):
        lb_re, lb_im, coef_re, coef_im = _zoh(a_re[direction], a_im[direction], log_dt[direction])
        br = b_re[direction].astype(jnp.float32)
        bi = b_im[direction].astype(jnp.float32)
        cr = c_re[direction].astype(jnp.float32)
        ci = c_im[direction].astype(jnp.float32)
        dc_re, dc_im = _drive(ucf, br, bi, coef_re, coef_im)
        hc_re, hc_im = _scan(lb_re, lb_im, dc_re, dc_im, reverse)
        edge = slice(0, 1) if reverse else slice(Lc - 1, Lc)
        h0 = (hc_re[:, edge], hc_im[:, edge])
        dl_re, dl_im = _drive(uf, br, bi, coef_re, coef_im)
        h_re, h_im = _scan(lb_re, lb_im, dl_re, dl_im, reverse, h0)
        y = y + _readout(h_re, h_im, cr, ci)
        if ctx_out:
            yc = yc + _readout(hc_re, hc_im, cr, ci)
    y = y.reshape(B, L, SSM_W).astype(u.dtype)
    if ctx_out:
        yc = yc.reshape(B, Lc, SSM_W).astype(u.dtype)
    return y, yc


def _merge(attn, ssm, gate, w_glu, b_glu, w_br_attn, w_br_ssm, w_out):
    y = jax.nn.gelu(ssm)
    y = y * jax.nn.sigmoid(y @ w_glu + b_glu)
    g_attn, g_ssm = jnp.split(jax.nn.sigmoid(gate), 2, axis=-1)
    merged = g_attn * (attn @ w_br_attn) + g_ssm * (y @ w_br_ssm)
    return merged @ w_out


def _token_mixer(h, hc, rope, w_in, q_g, k_g, a_re, a_im, log_dt, b_re, b_im, c_re, c_im, d,
                 w_glu, b_glu, w_br_attn, w_br_ssm, w_out, ctx_out):
    B, L = h.shape[0], h.shape[1]
    Lc = hc.shape[1]
    k, v, u, q, gate = jnp.split(h @ w_in, SPLITS, axis=-1)
    pc = hc @ (w_in if ctx_out else w_in[:, :CTX_IN_W])
    kc, vc, uc = pc[..., :KV_W], pc[..., KV_W:2 * KV_W], pc[..., 2 * KV_W:CTX_IN_W]
    q = _axial_rope(_rms_norm(q.reshape(B, L, N_Q_HEADS, HEAD_DIM), q_g), rope)
    k = _axial_rope(_rms_norm(k.reshape(B, L, N_KV_HEADS, HEAD_DIM), k_g), rope)
    kc = _rms_norm(kc.reshape(B, Lc, N_KV_HEADS, HEAD_DIM), k_g)
    vc = vc.reshape(B, Lc, N_KV_HEADS, HEAD_DIM)
    v = v.reshape(B, L, N_KV_HEADS, HEAD_DIM)
    k_all = jnp.concatenate([kc, k], axis=1)
    v_all = jnp.concatenate([vc, v], axis=1)
    attn = _blocked_attention(q, k_all, v_all)
    ssm, ssm_c = _s5_mixer(u, uc, a_re, a_im, log_dt, b_re, b_im, c_re, c_im, d, ctx_out)
    out = _merge(attn, ssm, gate, w_glu, b_glu, w_br_attn, w_br_ssm, w_out)
    out_c = None
    if ctx_out:
        qc = _rms_norm(pc[..., CTX_IN_W:CTX_IN_W + ATTN_W].reshape(B, Lc, N_Q_HEADS, HEAD_DIM), q_g)
        attn_c = _attend_block(qc, kc, vc)
        out_c = _merge(attn_c, ssm_c, pc[..., CTX_IN_W + ATTN_W:], w_glu, b_glu,
                       w_br_attn, w_br_ssm, w_out)
    return out, out_c


def _fwd_setup_inputs(seed: int = 0) -> dict:
    key = jax.random.key(seed)
    ks = jax.random.split(key, 32)
    f32 = jnp.float32

    def nrm(k, shape, scale):
        return jax.random.normal(k, shape, f32) * scale

    G, P, E = SSM_GROUPS, SSM_STATE, SSM_GROUP
    n_idx = jnp.arange(P, dtype=f32)
    return {
        'x': nrm(ks[0], (BATCH, SEQ, D_MODEL), 1.0),
        'c': nrm(ks[1], (BATCH, D_MODEL), 1.0),
        'ctx': nrm(ks[2], (BATCH, CTX_LEN, D_MODEL), 1.0),
        'c_ctx': nrm(ks[3], (D_MODEL,), 1.0),
        'w_mod': nrm(ks[4], (DEPTH, D_MODEL, N_MOD * D_MODEL), 0.5 * D_MODEL ** -0.5),
        'b_mod': nrm(ks[5], (DEPTH, N_MOD * D_MODEL), 0.01),
        'norm_g': 1.0 + nrm(ks[6], (DEPTH, 3, D_MODEL), 0.02),
        'w_ffn1_gate': nrm(ks[7], (DEPTH, D_MODEL, D_FF), D_MODEL ** -0.5),
        'w_ffn1_up': nrm(ks[8], (DEPTH, D_MODEL, D_FF), D_MODEL ** -0.5),
        'w_ffn1_down': nrm(ks[9], (DEPTH, D_FF, D_MODEL), D_FF ** -0.5),
        'w_in': nrm(ks[10], (DEPTH, D_MODEL, IN_W), D_MODEL ** -0.5),
        'q_norm_g': 1.0 + nrm(ks[11], (DEPTH, HEAD_DIM), 0.02),
        'k_norm_g': 1.0 + nrm(ks[12], (DEPTH, HEAD_DIM), 0.02),
        'ssm_a_re': -0.5 + nrm(ks[13], (DEPTH, 2, G, P), 0.01),
        'ssm_a_im': math.pi * n_idx + nrm(ks[14], (DEPTH, 2, G, P), 0.01),
        'ssm_log_dt': jax.random.uniform(ks[15], (DEPTH, 2, G), f32,
                                         math.log(DT_MIN), math.log(DT_MAX)),
        'ssm_b_re': nrm(ks[16], (DEPTH, 2, G, P, E), (2 * E) ** -0.5),
        'ssm_b_im': nrm(ks[17], (DEPTH, 2, G, P, E), (2 * E) ** -0.5),
        'ssm_c_re': nrm(ks[18], (DEPTH, 2, G, E, P), P ** -0.5),
        'ssm_c_im': nrm(ks[19], (DEPTH, 2, G, E, P), P ** -0.5),
        'ssm_d': nrm(ks[20], (DEPTH, SSM_W), 1.0),
        'w_glu': nrm(ks[21], (DEPTH, SSM_W, SSM_W), SSM_W ** -0.5),
        'b_glu': nrm(ks[22], (DEPTH, SSM_W), 0.01),
        'w_br_attn': nrm(ks[23], (DEPTH, ATTN_W, D_MODEL), ATTN_W ** -0.5),
        'w_br_ssm': nrm(ks[24], (DEPTH, SSM_W, D_MODEL), SSM_W ** -0.5),
        'w_out': nrm(ks[25], (DEPTH, D_MODEL, D_MODEL), D_MODEL ** -0.5),
        'w_ffn2_gate': nrm(ks[26], (DEPTH, D_MODEL, D_FF), D_MODEL ** -0.5),
        'w_ffn2_up': nrm(ks[27], (DEPTH, D_MODEL, D_FF), D_MODEL ** -0.5),
        'w_ffn2_down': nrm(ks[28], (DEPTH, D_FF, D_MODEL), D_FF ** -0.5),
    }


def _fwd_reference(x, c, ctx, c_ctx, w_mod, b_mod, norm_g, w_ffn1_gate, w_ffn1_up, w_ffn1_down,
              w_in, q_norm_g, k_norm_g, ssm_a_re, ssm_a_im, ssm_log_dt, ssm_b_re, ssm_b_im,
              ssm_c_re, ssm_c_im, ssm_d, w_glu, b_glu, w_br_attn, w_br_ssm, w_out,
              w_ffn2_gate, w_ffn2_up, w_ffn2_down):
    L = x.shape[1]
    rope = _axial_rope_tables(L)
    silu_c = jax.nn.silu(c)
    silu_cc = jax.nn.silu(c_ctx)
    for l in range(DEPTH):
        last = l == DEPTH - 1
        n_ctx_mod = N_MOD_CTX_LAST if last else N_MOD
        mod = (silu_c @ w_mod[l] + b_mod[l])[:, None, :]
        sh1, sc1, g1, sh2, sc2, g2, sh3, sc3, g3 = jnp.split(mod, N_MOD, axis=-1)
        mod_c = silu_cc @ w_mod[l][:, :n_ctx_mod * D_MODEL] + b_mod[l][:n_ctx_mod * D_MODEL]
        mc = jnp.split(mod_c, n_ctx_mod, axis=-1)
        ffn1 = (w_ffn1_gate[l], w_ffn1_up[l], w_ffn1_down[l])
        ffn2 = (w_ffn2_gate[l], w_ffn2_up[l], w_ffn2_down[l])
        x = x + 0.5 * g1 * _swiglu(_modulate(_rms_norm(x, norm_g[l, 0]), sh1, sc1), *ffn1)
        ctx = ctx + 0.5 * mc[2] * _swiglu(_modulate(_rms_norm(ctx, norm_g[l, 0]), mc[0], mc[1]), *ffn1)
        h = _modulate(_rms_norm(x, norm_g[l, 1]), sh2, sc2)
        hc = _modulate(_rms_norm(ctx, norm_g[l, 1]), mc[3], mc[4])
        mix, mix_c = _token_mixer(h, hc, rope, w_in[l], q_norm_g[l], k_norm_g[l],
                                  ssm_a_re[l], ssm_a_im[l], ssm_log_dt[l], ssm_b_re[l], ssm_b_im[l],
                                  ssm_c_re[l], ssm_c_im[l], ssm_d[l], w_glu[l], b_glu[l],
                                  w_br_attn[l], w_br_ssm[l], w_out[l], not last)
        x = x + g2 * mix
        x = x + 0.5 * g3 * _swiglu(_modulate(_rms_norm(x, norm_g[l, 2]), sh3, sc3), *ffn2)
        if not last:
            ctx = ctx + mc[5] * mix_c
            ctx = ctx + 0.5 * mc[8] * _swiglu(_modulate(_rms_norm(ctx, norm_g[l, 2]), mc[6], mc[7]), *ffn2)
    return x


import jax as _jax
import jax.numpy as _jnp

TWIN_FORMAT = 'train_step'
FWD_PARAMS = ['x', 'c', 'ctx', 'c_ctx', 'w_mod', 'b_mod', 'norm_g', 'w_ffn1_gate', 'w_ffn1_up', 'w_ffn1_down', 'w_in', 'q_norm_g', 'k_norm_g', 'ssm_a_re', 'ssm_a_im', 'ssm_log_dt', 'ssm_b_re', 'ssm_b_im', 'ssm_c_re', 'ssm_c_im', 'ssm_d', 'w_glu', 'b_glu', 'w_br_attn', 'w_br_ssm', 'w_out', 'w_ffn2_gate', 'w_ffn2_up', 'w_ffn2_down']
TWIN_WEIGHTS = ['c_ctx', 'w_mod', 'b_mod', 'norm_g', 'w_ffn1_gate', 'w_ffn1_up', 'w_ffn1_down', 'w_in', 'q_norm_g', 'k_norm_g', 'ssm_a_re', 'ssm_a_im', 'ssm_log_dt', 'ssm_b_re', 'ssm_b_im', 'ssm_c_re', 'ssm_c_im', 'ssm_d', 'w_glu', 'b_glu', 'w_br_attn', 'w_br_ssm', 'w_out', 'w_ffn2_gate', 'w_ffn2_up', 'w_ffn2_down']
TWIN_DIFF_INPUT = 'x'
TWIN_INPUTS = ['x', 'c', 'ctx', 'c_ctx', 'w_mod', 'b_mod', 'norm_g', 'w_ffn1_gate', 'w_ffn1_up', 'w_ffn1_down', 'w_in', 'q_norm_g', 'k_norm_g', 'ssm_a_re', 'ssm_a_im', 'ssm_log_dt', 'ssm_b_re', 'ssm_b_im', 'ssm_c_re', 'ssm_c_im', 'ssm_d', 'w_glu', 'b_glu', 'w_br_attn', 'w_br_ssm', 'w_out', 'w_ffn2_gate', 'w_ffn2_up', 'w_ffn2_down', 'loss_target', 'm_c_ctx', 'm_w_mod', 'm_b_mod', 'm_norm_g', 'm_w_ffn1_gate', 'm_w_ffn1_up', 'm_w_ffn1_down', 'm_w_in', 'm_q_norm_g', 'm_k_norm_g', 'm_ssm_a_re', 'm_ssm_a_im', 'm_ssm_log_dt', 'm_ssm_b_re', 'm_ssm_b_im', 'm_ssm_c_re', 'm_ssm_c_im', 'm_ssm_d', 'm_w_glu', 'm_b_glu', 'm_w_br_attn', 'm_w_br_ssm', 'm_w_out', 'm_w_ffn2_gate', 'm_w_ffn2_up', 'm_w_ffn2_down', 'v_c_ctx', 'v_w_mod', 'v_b_mod', 'v_norm_g', 'v_w_ffn1_gate', 'v_w_ffn1_up', 'v_w_ffn1_down', 'v_w_in', 'v_q_norm_g', 'v_k_norm_g', 'v_ssm_a_re', 'v_ssm_a_im', 'v_ssm_log_dt', 'v_ssm_b_re', 'v_ssm_b_im', 'v_ssm_c_re', 'v_ssm_c_im', 'v_ssm_d', 'v_w_glu', 'v_b_glu', 'v_w_br_attn', 'v_w_br_ssm', 'v_w_out', 'v_w_ffn2_gate', 'v_w_ffn2_up', 'v_w_ffn2_down']
TWIN_OUTPUTS = ['loss', 'grad_x', 'grad_c_ctx', 'grad_w_mod', 'grad_b_mod', 'grad_norm_g', 'grad_w_ffn1_gate', 'grad_w_ffn1_up', 'grad_w_ffn1_down', 'grad_w_in', 'grad_q_norm_g', 'grad_k_norm_g', 'grad_ssm_a_re', 'grad_ssm_a_im', 'grad_ssm_log_dt', 'grad_ssm_b_re', 'grad_ssm_b_im', 'grad_ssm_c_re', 'grad_ssm_c_im', 'grad_ssm_d', 'grad_w_glu', 'grad_b_glu', 'grad_w_br_attn', 'grad_w_br_ssm', 'grad_w_out', 'grad_w_ffn2_gate', 'grad_w_ffn2_up', 'grad_w_ffn2_down', 'delta_c_ctx', 'delta_w_mod', 'delta_b_mod', 'delta_norm_g', 'delta_w_ffn1_gate', 'delta_w_ffn1_up', 'delta_w_ffn1_down', 'delta_w_in', 'delta_q_norm_g', 'delta_k_norm_g', 'delta_ssm_a_re', 'delta_ssm_a_im', 'delta_ssm_log_dt', 'delta_ssm_b_re', 'delta_ssm_b_im', 'delta_ssm_c_re', 'delta_ssm_c_im', 'delta_ssm_d', 'delta_w_glu', 'delta_b_glu', 'delta_w_br_attn', 'delta_w_br_ssm', 'delta_w_out', 'delta_w_ffn2_gate', 'delta_w_ffn2_up', 'delta_w_ffn2_down', 'new_m_c_ctx', 'new_m_w_mod', 'new_m_b_mod', 'new_m_norm_g', 'new_m_w_ffn1_gate', 'new_m_w_ffn1_up', 'new_m_w_ffn1_down', 'new_m_w_in', 'new_m_q_norm_g', 'new_m_k_norm_g', 'new_m_ssm_a_re', 'new_m_ssm_a_im', 'new_m_ssm_log_dt', 'new_m_ssm_b_re', 'new_m_ssm_b_im', 'new_m_ssm_c_re', 'new_m_ssm_c_im', 'new_m_ssm_d', 'new_m_w_glu', 'new_m_b_glu', 'new_m_w_br_attn', 'new_m_w_br_ssm', 'new_m_w_out', 'new_m_w_ffn2_gate', 'new_m_w_ffn2_up', 'new_m_w_ffn2_down', 'new_v_c_ctx', 'new_v_w_mod', 'new_v_b_mod', 'new_v_norm_g', 'new_v_w_ffn1_gate', 'new_v_w_ffn1_up', 'new_v_w_ffn1_down', 'new_v_w_in', 'new_v_q_norm_g', 'new_v_k_norm_g', 'new_v_ssm_a_re', 'new_v_ssm_a_im', 'new_v_ssm_log_dt', 'new_v_ssm_b_re', 'new_v_ssm_b_im', 'new_v_ssm_c_re', 'new_v_ssm_c_im', 'new_v_ssm_d', 'new_v_w_glu', 'new_v_b_glu', 'new_v_w_br_attn', 'new_v_w_br_ssm', 'new_v_w_out', 'new_v_w_ffn2_gate', 'new_v_w_ffn2_up', 'new_v_w_ffn2_down']
TWIN_LEAF_KINDS = {'loss': 'loss', 'grad_x': 'grad_x', 'grad_c_ctx': 'grad_w', 'grad_w_mod': 'grad_w', 'grad_b_mod': 'grad_w', 'grad_norm_g': 'grad_w', 'grad_w_ffn1_gate': 'grad_w', 'grad_w_ffn1_up': 'grad_w', 'grad_w_ffn1_down': 'grad_w', 'grad_w_in': 'grad_w', 'grad_q_norm_g': 'grad_w', 'grad_k_norm_g': 'grad_w', 'grad_ssm_a_re': 'grad_w', 'grad_ssm_a_im': 'grad_w', 'grad_ssm_log_dt': 'grad_w', 'grad_ssm_b_re': 'grad_w', 'grad_ssm_b_im': 'grad_w', 'grad_ssm_c_re': 'grad_w', 'grad_ssm_c_im': 'grad_w', 'grad_ssm_d': 'grad_w', 'grad_w_glu': 'grad_w', 'grad_b_glu': 'grad_w', 'grad_w_br_attn': 'grad_w', 'grad_w_br_ssm': 'grad_w', 'grad_w_out': 'grad_w', 'grad_w_ffn2_gate': 'grad_w', 'grad_w_ffn2_up': 'grad_w', 'grad_w_ffn2_down': 'grad_w', 'delta_c_ctx': 'delta_w', 'delta_w_mod': 'delta_w', 'delta_b_mod': 'delta_w', 'delta_norm_g': 'delta_w', 'delta_w_ffn1_gate': 'delta_w', 'delta_w_ffn1_up': 'delta_w', 'delta_w_ffn1_down': 'delta_w', 'delta_w_in': 'delta_w', 'delta_q_norm_g': 'delta_w', 'delta_k_norm_g': 'delta_w', 'delta_ssm_a_re': 'delta_w', 'delta_ssm_a_im': 'delta_w', 'delta_ssm_log_dt': 'delta_w', 'delta_ssm_b_re': 'delta_w', 'delta_ssm_b_im': 'delta_w', 'delta_ssm_c_re': 'delta_w', 'delta_ssm_c_im': 'delta_w', 'delta_ssm_d': 'delta_w', 'delta_w_glu': 'delta_w', 'delta_b_glu': 'delta_w', 'delta_w_br_attn': 'delta_w', 'delta_w_br_ssm': 'delta_w', 'delta_w_out': 'delta_w', 'delta_w_ffn2_gate': 'delta_w', 'delta_w_ffn2_up': 'delta_w', 'delta_w_ffn2_down': 'delta_w', 'new_m_c_ctx': 'new_m', 'new_m_w_mod': 'new_m', 'new_m_b_mod': 'new_m', 'new_m_norm_g': 'new_m', 'new_m_w_ffn1_gate': 'new_m', 'new_m_w_ffn1_up': 'new_m', 'new_m_w_ffn1_down': 'new_m', 'new_m_w_in': 'new_m', 'new_m_q_norm_g': 'new_m', 'new_m_k_norm_g': 'new_m', 'new_m_ssm_a_re': 'new_m', 'new_m_ssm_a_im': 'new_m', 'new_m_ssm_log_dt': 'new_m', 'new_m_ssm_b_re': 'new_m', 'new_m_ssm_b_im': 'new_m', 'new_m_ssm_c_re': 'new_m', 'new_m_ssm_c_im': 'new_m', 'new_m_ssm_d': 'new_m', 'new_m_w_glu': 'new_m', 'new_m_b_glu': 'new_m', 'new_m_w_br_attn': 'new_m', 'new_m_w_br_ssm': 'new_m', 'new_m_w_out': 'new_m', 'new_m_w_ffn2_gate': 'new_m', 'new_m_w_ffn2_up': 'new_m', 'new_m_w_ffn2_down': 'new_m', 'new_v_c_ctx': 'new_v', 'new_v_w_mod': 'new_v', 'new_v_b_mod': 'new_v', 'new_v_norm_g': 'new_v', 'new_v_w_ffn1_gate': 'new_v', 'new_v_w_ffn1_up': 'new_v', 'new_v_w_ffn1_down': 'new_v', 'new_v_w_in': 'new_v', 'new_v_q_norm_g': 'new_v', 'new_v_k_norm_g': 'new_v', 'new_v_ssm_a_re': 'new_v', 'new_v_ssm_a_im': 'new_v', 'new_v_ssm_log_dt': 'new_v', 'new_v_ssm_b_re': 'new_v', 'new_v_ssm_b_im': 'new_v', 'new_v_ssm_c_re': 'new_v', 'new_v_ssm_c_im': 'new_v', 'new_v_ssm_d': 'new_v', 'new_v_w_glu': 'new_v', 'new_v_b_glu': 'new_v', 'new_v_w_br_attn': 'new_v', 'new_v_w_br_ssm': 'new_v', 'new_v_w_out': 'new_v', 'new_v_w_ffn2_gate': 'new_v', 'new_v_w_ffn2_up': 'new_v', 'new_v_w_ffn2_down': 'new_v'}


def _forward(args):
    return _fwd_reference(*[args[k] for k in FWD_PARAMS])


def _output_shape():
    out = _jax.eval_shape(lambda: _forward(_fwd_setup_inputs(0)))
    return out.shape, out.dtype

N_MICROBATCH = 1
ADAM_LR = 0.001
ADAM_B1 = 0.9
ADAM_B2 = 0.999
ADAM_EPS = 1e-08
ADAM_WD = 0.01
ADAM_STEP = 10
PER_EXAMPLE_BATCH_AXIS = {'x': 0, 'c': 0, 'ctx': 0, 'loss_target': 0}
SHARED_INPUTS = []
_WEIGHT_DTYPES = {'c_ctx': _jnp.float32, 'w_mod': _jnp.float32, 'b_mod': _jnp.float32, 'norm_g': _jnp.float32, 'w_ffn1_gate': _jnp.float32, 'w_ffn1_up': _jnp.float32, 'w_ffn1_down': _jnp.float32, 'w_in': _jnp.float32, 'q_norm_g': _jnp.float32, 'k_norm_g': _jnp.float32, 'ssm_a_re': _jnp.float32, 'ssm_a_im': _jnp.float32, 'ssm_log_dt': _jnp.float32, 'ssm_b_re': _jnp.float32, 'ssm_b_im': _jnp.float32, 'ssm_c_re': _jnp.float32, 'ssm_c_im': _jnp.float32, 'ssm_d': _jnp.float32, 'w_glu': _jnp.float32, 'b_glu': _jnp.float32, 'w_br_attn': _jnp.float32, 'w_br_ssm': _jnp.float32, 'w_out': _jnp.float32, 'w_ffn2_gate': _jnp.float32, 'w_ffn2_up': _jnp.float32, 'w_ffn2_down': _jnp.float32}
MOMENT_SCALE = {'c_ctx': 5.476764e-03, 'w_mod': 5.751832e-02, 'b_mod': 1.283509e-01, 'norm_g': 1.619169e-01, 'w_ffn1_gate': 7.192950e-03, 'w_ffn1_up': 6.589574e-03, 'w_ffn1_down': 1.073737e-02, 'w_in': 1.286561e-02, 'q_norm_g': 7.523533e-03, 'k_norm_g': 7.404608e-03, 'ssm_a_re': 2.926013e-03, 'ssm_a_im': 1.742963e-03, 'ssm_log_dt': 3.077803e-01, 'ssm_b_re': 1.180361e-03, 'ssm_b_im': 1.471652e-03, 'ssm_c_re': 1.963006e-03, 'ssm_c_im': 1.696714e-03, 'ssm_d': 7.990879e-02, 'w_glu': 1.687987e-02, 'b_glu': 4.670305e-02, 'w_br_attn': 1.776386e-02, 'w_br_ssm': 1.384758e-02, 'w_out': 1.813784e-02, 'w_ffn2_gate': 7.176553e-03, 'w_ffn2_up': 6.548371e-03, 'w_ffn2_down': 1.062282e-02}


def _to_microbatches(a, axis):
    t = _jnp.moveaxis(a, axis, 0)
    t = t.reshape((N_MICROBATCH, t.shape[0] // N_MICROBATCH) + t.shape[1:])
    return _jnp.moveaxis(t, 1, axis + 1)


def setup_inputs(seed: int = 0) -> dict:
    inp = _fwd_setup_inputs(seed)
    key = _jax.random.fold_in(_jax.random.key(seed), 7919)
    shape, _ = _output_shape()
    out = dict(inp)
    out["loss_target"] = _jax.random.normal(_jax.random.fold_in(key, 0), shape, _jnp.float32)
    for i, name in enumerate(TWIN_WEIGHTS):
        w = inp[name].astype(_jnp.float32)
        if MOMENT_SCALE is None:
            s = _jnp.sqrt(_jnp.mean(_jnp.square(w)) + 1e-30)
        else:
            s = MOMENT_SCALE[name]
        km, kv = _jax.random.split(_jax.random.fold_in(key, i + 1))
        out[name] = w
        out["m_" + name] = s * _jax.random.normal(km, w.shape, _jnp.float32)
        out["v_" + name] = (s * s) * _jax.random.uniform(kv, w.shape, _jnp.float32, 0.5, 1.5)
    if N_MICROBATCH > 1:
        for name, axis in PER_EXAMPLE_BATCH_AXIS.items():
            out[name] = _to_microbatches(out[name], axis)
    return {'x': out['x'], 'c': out['c'], 'ctx': out['ctx'], 'c_ctx': out['c_ctx'], 'w_mod': out['w_mod'], 'b_mod': out['b_mod'], 'norm_g': out['norm_g'], 'w_ffn1_gate': out['w_ffn1_gate'], 'w_ffn1_up': out['w_ffn1_up'], 'w_ffn1_down': out['w_ffn1_down'], 'w_in': out['w_in'], 'q_norm_g': out['q_norm_g'], 'k_norm_g': out['k_norm_g'], 'ssm_a_re': out['ssm_a_re'], 'ssm_a_im': out['ssm_a_im'], 'ssm_log_dt': out['ssm_log_dt'], 'ssm_b_re': out['ssm_b_re'], 'ssm_b_im': out['ssm_b_im'], 'ssm_c_re': out['ssm_c_re'], 'ssm_c_im': out['ssm_c_im'], 'ssm_d': out['ssm_d'], 'w_glu': out['w_glu'], 'b_glu': out['b_glu'], 'w_br_attn': out['w_br_attn'], 'w_br_ssm': out['w_br_ssm'], 'w_out': out['w_out'], 'w_ffn2_gate': out['w_ffn2_gate'], 'w_ffn2_up': out['w_ffn2_up'], 'w_ffn2_down': out['w_ffn2_down'], 'loss_target': out['loss_target'], 'm_c_ctx': out['m_c_ctx'], 'm_w_mod': out['m_w_mod'], 'm_b_mod': out['m_b_mod'], 'm_norm_g': out['m_norm_g'], 'm_w_ffn1_gate': out['m_w_ffn1_gate'], 'm_w_ffn1_up': out['m_w_ffn1_up'], 'm_w_ffn1_down': out['m_w_ffn1_down'], 'm_w_in': out['m_w_in'], 'm_q_norm_g': out['m_q_norm_g'], 'm_k_norm_g': out['m_k_norm_g'], 'm_ssm_a_re': out['m_ssm_a_re'], 'm_ssm_a_im': out['m_ssm_a_im'], 'm_ssm_log_dt': out['m_ssm_log_dt'], 'm_ssm_b_re': out['m_ssm_b_re'], 'm_ssm_b_im': out['m_ssm_b_im'], 'm_ssm_c_re': out['m_ssm_c_re'], 'm_ssm_c_im': out['m_ssm_c_im'], 'm_ssm_d': out['m_ssm_d'], 'm_w_glu': out['m_w_glu'], 'm_b_glu': out['m_b_glu'], 'm_w_br_attn': out['m_w_br_attn'], 'm_w_br_ssm': out['m_w_br_ssm'], 'm_w_out': out['m_w_out'], 'm_w_ffn2_gate': out['m_w_ffn2_gate'], 'm_w_ffn2_up': out['m_w_ffn2_up'], 'm_w_ffn2_down': out['m_w_ffn2_down'], 'v_c_ctx': out['v_c_ctx'], 'v_w_mod': out['v_w_mod'], 'v_b_mod': out['v_b_mod'], 'v_norm_g': out['v_norm_g'], 'v_w_ffn1_gate': out['v_w_ffn1_gate'], 'v_w_ffn1_up': out['v_w_ffn1_up'], 'v_w_ffn1_down': out['v_w_ffn1_down'], 'v_w_in': out['v_w_in'], 'v_q_norm_g': out['v_q_norm_g'], 'v_k_norm_g': out['v_k_norm_g'], 'v_ssm_a_re': out['v_ssm_a_re'], 'v_ssm_a_im': out['v_ssm_a_im'], 'v_ssm_log_dt': out['v_ssm_log_dt'], 'v_ssm_b_re': out['v_ssm_b_re'], 'v_ssm_b_im': out['v_ssm_b_im'], 'v_ssm_c_re': out['v_ssm_c_re'], 'v_ssm_c_im': out['v_ssm_c_im'], 'v_ssm_d': out['v_ssm_d'], 'v_w_glu': out['v_w_glu'], 'v_b_glu': out['v_b_glu'], 'v_w_br_attn': out['v_w_br_attn'], 'v_w_br_ssm': out['v_w_br_ssm'], 'v_w_out': out['v_w_out'], 'v_w_ffn2_gate': out['v_w_ffn2_gate'], 'v_w_ffn2_up': out['v_w_ffn2_up'], 'v_w_ffn2_down': out['v_w_ffn2_down']}


def _loss(weights, diff, rest, loss_target):
    with _jax.named_scope("forward"):
        args = {**rest, TWIN_DIFF_INPUT: diff, **{k: w.astype(_WEIGHT_DTYPES[k]) for k, w in weights.items()}}
        y = _forward(args)
    with _jax.named_scope("loss_head"):
        err = _jnp.square(y.astype(_jnp.float32) - loss_target)
        return 0.5 * _jnp.sum(_jnp.mean(err, axis=-1)) if err.ndim else 0.5 * err


def _adamw(w, g, m, v):
    m = ADAM_B1 * m + (1.0 - ADAM_B1) * g
    v = ADAM_B2 * v + (1.0 - ADAM_B2) * _jnp.square(g)
    m_hat = m / (1.0 - ADAM_B1 ** ADAM_STEP)
    v_hat = v / (1.0 - ADAM_B2 ** ADAM_STEP)
    delta = -ADAM_LR * (m_hat / (_jnp.sqrt(v_hat) + ADAM_EPS) + ADAM_WD * w)
    return delta, m, v


def reference(x, c, ctx, c_ctx, w_mod, b_mod, norm_g, w_ffn1_gate, w_ffn1_up, w_ffn1_down, w_in, q_norm_g, k_norm_g, ssm_a_re, ssm_a_im, ssm_log_dt, ssm_b_re, ssm_b_im, ssm_c_re, ssm_c_im, ssm_d, w_glu, b_glu, w_br_attn, w_br_ssm, w_out, w_ffn2_gate, w_ffn2_up, w_ffn2_down, loss_target, m_c_ctx, m_w_mod, m_b_mod, m_norm_g, m_w_ffn1_gate, m_w_ffn1_up, m_w_ffn1_down, m_w_in, m_q_norm_g, m_k_norm_g, m_ssm_a_re, m_ssm_a_im, m_ssm_log_dt, m_ssm_b_re, m_ssm_b_im, m_ssm_c_re, m_ssm_c_im, m_ssm_d, m_w_glu, m_b_glu, m_w_br_attn, m_w_br_ssm, m_w_out, m_w_ffn2_gate, m_w_ffn2_up, m_w_ffn2_down, v_c_ctx, v_w_mod, v_b_mod, v_norm_g, v_w_ffn1_gate, v_w_ffn1_up, v_w_ffn1_down, v_w_in, v_q_norm_g, v_k_norm_g, v_ssm_a_re, v_ssm_a_im, v_ssm_log_dt, v_ssm_b_re, v_ssm_b_im, v_ssm_c_re, v_ssm_c_im, v_ssm_d, v_w_glu, v_b_glu, v_w_br_attn, v_w_br_ssm, v_w_out, v_w_ffn2_gate, v_w_ffn2_up, v_w_ffn2_down):
    given = dict(x=x, c=c, ctx=ctx, c_ctx=c_ctx, w_mod=w_mod, b_mod=b_mod, norm_g=norm_g, w_ffn1_gate=w_ffn1_gate, w_ffn1_up=w_ffn1_up, w_ffn1_down=w_ffn1_down, w_in=w_in, q_norm_g=q_norm_g, k_norm_g=k_norm_g, ssm_a_re=ssm_a_re, ssm_a_im=ssm_a_im, ssm_log_dt=ssm_log_dt, ssm_b_re=ssm_b_re, ssm_b_im=ssm_b_im, ssm_c_re=ssm_c_re, ssm_c_im=ssm_c_im, ssm_d=ssm_d, w_glu=w_glu, b_glu=b_glu, w_br_attn=w_br_attn, w_br_ssm=w_br_ssm, w_out=w_out, w_ffn2_gate=w_ffn2_gate, w_ffn2_up=w_ffn2_up, w_ffn2_down=w_ffn2_down, loss_target=loss_target, m_c_ctx=m_c_ctx, m_w_mod=m_w_mod, m_b_mod=m_b_mod, m_norm_g=m_norm_g, m_w_ffn1_gate=m_w_ffn1_gate, m_w_ffn1_up=m_w_ffn1_up, m_w_ffn1_down=m_w_ffn1_down, m_w_in=m_w_in, m_q_norm_g=m_q_norm_g, m_k_norm_g=m_k_norm_g, m_ssm_a_re=m_ssm_a_re, m_ssm_a_im=m_ssm_a_im, m_ssm_log_dt=m_ssm_log_dt, m_ssm_b_re=m_ssm_b_re, m_ssm_b_im=m_ssm_b_im, m_ssm_c_re=m_ssm_c_re, m_ssm_c_im=m_ssm_c_im, m_ssm_d=m_ssm_d, m_w_glu=m_w_glu, m_b_glu=m_b_glu, m_w_br_attn=m_w_br_attn, m_w_br_ssm=m_w_br_ssm, m_w_out=m_w_out, m_w_ffn2_gate=m_w_ffn2_gate, m_w_ffn2_up=m_w_ffn2_up, m_w_ffn2_down=m_w_ffn2_down, v_c_ctx=v_c_ctx, v_w_mod=v_w_mod, v_b_mod=v_b_mod, v_norm_g=v_norm_g, v_w_ffn1_gate=v_w_ffn1_gate, v_w_ffn1_up=v_w_ffn1_up, v_w_ffn1_down=v_w_ffn1_down, v_w_in=v_w_in, v_q_norm_g=v_q_norm_g, v_k_norm_g=v_k_norm_g, v_ssm_a_re=v_ssm_a_re, v_ssm_a_im=v_ssm_a_im, v_ssm_log_dt=v_ssm_log_dt, v_ssm_b_re=v_ssm_b_re, v_ssm_b_im=v_ssm_b_im, v_ssm_c_re=v_ssm_c_re, v_ssm_c_im=v_ssm_c_im, v_ssm_d=v_ssm_d, v_w_glu=v_w_glu, v_b_glu=v_b_glu, v_w_br_attn=v_w_br_attn, v_w_br_ssm=v_w_br_ssm, v_w_out=v_w_out, v_w_ffn2_gate=v_w_ffn2_gate, v_w_ffn2_up=v_w_ffn2_up, v_w_ffn2_down=v_w_ffn2_down)
    weights = {n: given[n] for n in TWIN_WEIGHTS}
    shared = {n: given[n] for n in SHARED_INPUTS}
    per_example = {n: given[n] for n in ['x', 'c', 'ctx']}
    grad_fn = _jax.value_and_grad(_loss, argnums=(0, 1))

    def one_microbatch(ex, loss_target):
        ex = dict(ex)
        diff = ex.pop(TWIN_DIFF_INPUT)
        return grad_fn(weights, diff, {**shared, **ex}, loss_target)

    if N_MICROBATCH == 1:
        loss, (grad_w, grad_x) = one_microbatch(per_example, given["loss_target"])
    else:
        def body(carry, xs):
            loss_sum, grad_sum = carry
            l_k, (gw_k, gx_k) = one_microbatch(xs[0], xs[1])
            with _jax.named_scope("update"):
                return (loss_sum + l_k, _jax.tree.map(_jnp.add, grad_sum, gw_k)), gx_k

        init = (_jnp.zeros((), _jnp.float32), _jax.tree.map(_jnp.zeros_like, weights))
        (loss, grad_w), grad_x = _jax.lax.scan(body, init, (per_example, given["loss_target"]))
    with _jax.named_scope("update"):
        delta_w, new_m, new_v = {}, {}, {}
        for n in TWIN_WEIGHTS:
            delta_w[n], new_m[n], new_v[n] = _adamw(weights[n], grad_w[n], given["m_" + n], given["v_" + n])
    return (loss, grad_x, *[grad_w[n] for n in TWIN_WEIGHTS], *[delta_w[n] for n in TWIN_WEIGHTS],
            *[new_m[n] for n in TWIN_WEIGHTS], *[new_v[n] for n in TWIN_WEIGHTS])
```

```python
import functools
import math

import jax
import jax.numpy as jnp
from jax import lax
from jax.experimental import pallas as pl
from jax.experimental.pallas import tpu as pltpu

F32 = jnp.float32
BF16 = jnp.bfloat16

N_DEV = 8
LANES = 128
SUBLANES = 8
VMEM_LIMIT = 56 * 1024 * 1024

NORM_EPS = 1e-6
GRID_W = 64
ROPE_THETA = 10000.0
SCAN_TAPS = SUBLANES
SLAB_GROUPS = 8

ADAM_LR = 0.001
ADAM_B1 = 0.9
ADAM_B2 = 0.999
ADAM_EPS = 1e-08
ADAM_WD = 0.01
ADAM_STEP = 10

NN = (((1,), (0,)), ((), ()))
NT = (((1,), (1,)), ((), ()))
TN = (((0,), (0,)), ((), ()))

MESH = pl.DeviceIdType.MESH
ANY = pl.BlockSpec(memory_space=pl.ANY)


def _tile(n, cap, align):
    best = None
    for t in range(align, min(n, cap) + 1, align):
        if n % t == 0:
            best = t
    return n if best is None else best


def _params(n_grid):
    return pltpu.CompilerParams(dimension_semantics=("arbitrary",) * n_grid, vmem_limit_bytes=VMEM_LIMIT)


def _sigmoid(x):
    return 1.0 / (1.0 + jnp.exp(-x))


GELU_K = math.sqrt(2.0 / math.pi)
GELU_C = 0.044715


def _gelu(x):
    return 0.5 * x * (1.0 + jnp.tanh(GELU_K * (x + GELU_C * x * x * x)))


def _gelu_grad(x):
    t = jnp.tanh(GELU_K * (x + GELU_C * x * x * x))
    return 0.5 * (1.0 + t) + 0.5 * x * (1.0 - t * t) * GELU_K * (1.0 + 3.0 * GELU_C * x * x)


def _adamw(w, g, m, v):
    m2 = ADAM_B1 * m + (1.0 - ADAM_B1) * g
    v2 = ADAM_B2 * v + (1.0 - ADAM_B2) * (g * g)
    m_hat = m2 / (1.0 - ADAM_B1 ** ADAM_STEP)
    v_hat = v2 / (1.0 - ADAM_B2 ** ADAM_STEP)
    delta = -ADAM_LR * (m_hat / (jnp.sqrt(v_hat) + ADAM_EPS) + ADAM_WD * w)
    return delta, m2, v2


def _position():
    return lax.axis_index("x"), lax.axis_index("y"), lax.axis_index("c")


def _all_gather(name, arrays):
    n = len(arrays)

    def body(*refs):
        ins, outs = refs[:n], refs[n:2 * n]
        send, recv, local = refs[2 * n:]
        x, y, c = _position()
        me, sibling = (x, y, c), (x, y, 1 - c)
        chips = [(1 - x, y), (x, 1 - y), (1 - x, 1 - y)]

        def slot(a, p):
            return outs[a].at[4 * p[0] + 2 * p[1] + p[2]]

        def copy(a, k, block, to, src=None):
            dst = slot(a, block)
            return pltpu.make_async_remote_copy(
                src_ref=dst if src is None else src, dst_ref=dst,
                send_sem=send.at[a, k], recv_sem=recv.at[a, k], device_id=to, device_id_type=MESH)

        mine = [pltpu.make_async_copy(ins[a], slot(a, me), local.at[a]) for a in range(n)]
        for cp in mine:
            cp.start()
        first = []
        for a in range(n):
            first.append(copy(a, 0, me, sibling, src=ins[a]))
            first += [copy(a, 1 + j, me, (*chip, c), src=ins[a]) for j, chip in enumerate(chips)]
        for cp in first:
            cp.start()
        passed = []
        for j, chip in enumerate(chips):
            for a in range(n):
                copy(a, 1 + j, (*chip, c), me).wait_recv()
                cp = copy(a, 4 + j, (*chip, c), sibling)
                cp.start()
                passed.append(cp)
        for a in range(n):
            copy(a, 0, sibling, me).wait_recv()
        for j, chip in enumerate(chips):
            for a in range(n):
                copy(a, 4 + j, (*chip, 1 - c), me).wait_recv()
        for cp in first + passed:
            cp.wait_send()
        for cp in mine:
            cp.wait()

    outs = pl.pallas_call(
        body, name=name,
        out_shape=[jax.ShapeDtypeStruct((N_DEV,) + a.shape, a.dtype) for a in arrays],
        in_specs=[ANY] * n, out_specs=[ANY] * n,
        scratch_shapes=[pltpu.SemaphoreType.DMA((n, 7)), pltpu.SemaphoreType.DMA((n, 7)),
                        pltpu.SemaphoreType.DMA((n,))],
    )(*arrays)
    return list(outs)


def _exchange_blocks(name, arrays):
    n = len(arrays)

    def body(*refs):
        ins, outs = refs[:n], refs[n:2 * n]
        send, recv, local = refs[2 * n:]
        x, y, c = _position()
        me = 4 * x + 2 * y + c
        copies = []
        for a in range(n):
            cp = pltpu.make_async_copy(ins[a].at[me], outs[a].at[me], local.at[a])
            cp.start()
            copies.append(cp)
        sends = []
        for r in range(1, N_DEV):
            px, py, pc = x ^ (r >> 2), y ^ ((r >> 1) & 1), c ^ (r & 1)
            peer = 4 * px + 2 * py + pc
            for a in range(n):
                cp = pltpu.make_async_remote_copy(
                    src_ref=ins[a].at[peer], dst_ref=outs[a].at[me],
                    send_sem=send.at[a, r - 1], recv_sem=recv.at[a, r - 1],
                    device_id=(px, py, pc), device_id_type=MESH)
                cp.start()
                sends.append(cp)
        for r in range(1, N_DEV):
            px, py, pc = x ^ (r >> 2), y ^ ((r >> 1) & 1), c ^ (r & 1)
            peer = 4 * px + 2 * py + pc
            for a in range(n):
                pltpu.make_async_remote_copy(
                    src_ref=ins[a].at[peer], dst_ref=outs[a].at[peer],
                    send_sem=send.at[a, r - 1], recv_sem=recv.at[a, r - 1],
                    device_id=(px, py, pc), device_id_type=MESH).wait_recv()
        for cp in sends:
            cp.wait_send()
        for cp in copies:
            cp.wait()

    outs = pl.pallas_call(
        body, name=name,
        out_shape=[jax.ShapeDtypeStruct(a.shape, a.dtype) for a in arrays],
        in_specs=[ANY] * n, out_specs=[ANY] * n,
        scratch_shapes=[pltpu.SemaphoreType.DMA((n, 7)), pltpu.SemaphoreType.DMA((n, 7)),
                        pltpu.SemaphoreType.DMA((n,))],
    )(*arrays)
    return list(outs)


def _matmul(name, grid, operands, in_specs, pairs, out_shapes, out_specs, epilogue, acc_shapes=(), nk=1,
            prologue=None):
    n_in, n_out = len(operands), len(out_shapes)
    prologue = prologue or {}

    def body(*refs):
        ins, outs, accs = refs[:n_in], refs[n_in:n_in + n_out], refs[n_in + n_out:]
        pids = [pl.program_id(ax) for ax in range(len(grid))]

        def operand(i):
            v = ins[i][...]
            if i in prologue:
                v = prologue[i](v)
            return v.astype(BF16)

        def products():
            vals = {}
            for ai, bi, ci, dn in pairs:
                p = lax.dot_general(operand(ai), operand(bi), dn, preferred_element_type=F32)
                vals[ci] = p if ci not in vals else vals[ci] + p
            return [vals[ci] for ci in sorted(vals)]

        if nk == 1:
            epilogue(products(), ins, outs, pids)
        else:
            k = pids[-1]
            prods = products()

            @pl.when(k == 0)
            def _():
                for acc, p in zip(accs, prods):
                    acc[...] = p

            @pl.when(k > 0)
            def _():
                for acc, p in zip(accs, prods):
                    acc[...] += p

            @pl.when(k == nk - 1)
            def _():
                epilogue([acc[...] for acc in accs], ins, outs, pids)

    return pl.pallas_call(
        body, name=name, grid=grid, in_specs=in_specs, out_specs=out_specs, out_shape=out_shapes,
        scratch_shapes=[pltpu.VMEM(s, F32) for s in acc_shapes] if nk > 1 else [],
        compiler_params=_params(len(grid)),
    )(*operands)


def _rowwise(name, n_tiles, operands, in_specs, out_shapes, out_specs, red_widths, fn):
    n_in, n_out, n_red = len(operands), len(out_shapes), len(red_widths)

    def body(*refs):
        ins, outs, reds = refs[:n_in], refs[n_in:n_in + n_out], refs[n_in + n_out:]
        i = pl.program_id(0)
        vals, sums = fn(i, *[r[...] for r in ins])
        for o, v in zip(outs, vals):
            o[...] = v.astype(o.dtype)
        if n_red:
            @pl.when(i == 0)
            def _():
                for r, s in zip(reds, sums):
                    r[...] = s

            @pl.when(i > 0)
            def _():
                for r, s in zip(reds, sums):
                    r[...] += s

    red_shapes = [jax.ShapeDtypeStruct((1, w), F32) for w in red_widths]
    red_specs = [pl.BlockSpec((1, w), lambda i: (0, 0)) for w in red_widths]
    res = pl.pallas_call(
        body, name=name, grid=(n_tiles,), in_specs=in_specs,
        out_specs=list(out_specs) + red_specs, out_shape=list(out_shapes) + red_shapes,
        compiler_params=_params(1),
    )(*operands)
    return list(res[:n_out]), list(res[n_out:])


def _colsum(v):
    return jnp.sum(v, axis=0, keepdims=True)


def _row_tile(rows_a, rows_b):
    return _tile(math.gcd(rows_a, rows_b) if rows_b else rows_a, 256, SUBLANES)


def _norm_mod_fwd(name, xs, gamma, shift2, scale2, n_lat, n_ctx):
    rows, d = xs.shape
    tm = _row_tile(n_lat, n_ctx)
    nlt = n_lat // tm
    grp = pl.BlockSpec((None, 1, d), lambda i: (i // nlt, 0, 0))

    def fn(i, x, g, sh, sc):
        xh = x * lax.rsqrt(jnp.mean(x * x, axis=-1, keepdims=True) + NORM_EPS)
        return [(xh * g) * (1.0 + sc) + sh], []

    (h,), _ = _rowwise(
        name, rows // tm, [xs, gamma, shift2, scale2],
        [pl.BlockSpec((tm, d), lambda i: (i, 0)), pl.BlockSpec((1, d), lambda i: (0, 0)), grp, grp],
        [jax.ShapeDtypeStruct((rows, d), BF16)], [pl.BlockSpec((tm, d), lambda i: (i, 0))], [], fn)
    return h


def _norm_mod_bwd(name, xs, dh, gamma, scale2, n_lat, n_ctx, dres=None):
    rows, d = xs.shape
    tm = _row_tile(n_lat, n_ctx)
    nlt = n_lat // tm
    grp = pl.BlockSpec((None, 1, d), lambda i: (i // nlt, 0, 0))
    row = pl.BlockSpec((tm, d), lambda i: (i, 0))

    def fn(i, x, dy, g, sc, *res):
        rstd = lax.rsqrt(jnp.mean(x * x, axis=-1, keepdims=True) + NORM_EPS)
        xh = x * rstd
        dsh = _colsum(dy)
        dsc = _colsum(dy * (xh * g))
        dn = dy * (1.0 + sc)
        dgam = _colsum(dn * xh)
        dxh = dn * g
        dx = rstd * (dxh - xh * jnp.mean(dxh * xh, axis=-1, keepdims=True))
        if res:
            dx = dx + jnp.where(i < nlt, res[0], 0.0)
        lat = (i < nlt).astype(F32)
        return [dx], [dsh * lat, dsc * lat, dsh * (1.0 - lat), dsc * (1.0 - lat), dgam]

    operands = [xs, dh, gamma, scale2]
    specs = [row, row, pl.BlockSpec((1, d), lambda i: (0, 0)), grp]
    if dres is not None:
        operands.append(dres)
        specs.append(pl.BlockSpec((tm, d), lambda i: (jnp.minimum(i, nlt - 1), 0)))
    (dx,), sums = _rowwise(name, rows // tm, operands, specs,
                           [jax.ShapeDtypeStruct((rows, d), F32)], [row], [d] * 5, fn)
    return dx, sums


def _gate_bwd(name, dx, f, gate2, coef, n_lat, n_ctx):
    rows, d = dx.shape
    tm = _row_tile(n_lat, n_ctx)
    nlt = n_lat // tm
    row = pl.BlockSpec((tm, d), lambda i: (i, 0))

    def fn(i, dxv, fv, gv):
        dg = _colsum(dxv * fv) * coef
        lat = (i < nlt).astype(F32)
        return [(coef * gv) * dxv], [dg * lat, dg * (1.0 - lat)]

    (df,), sums = _rowwise(
        name, rows // tm, [dx, f, gate2],
        [row, row, pl.BlockSpec((None, 1, d), lambda i: (i // nlt, 0, 0))],
        [jax.ShapeDtypeStruct((rows, d), BF16)], [row], [d, d], fn)
    return df, sums


def _select_rows(i, tm, n_lat, vec2):
    rows = i * tm + lax.broadcasted_iota(jnp.int32, (tm, 1), 0)
    return jnp.where(rows < n_lat, vec2[0:1, :], vec2[1:2, :])


def _ffn_fwd(tag, xs, h, wg, wu, wd, gate2, n_lat):
    rows, d = xs.shape
    nb, _, fs = wg.shape
    tm = _tile(rows, 512, 128)
    blk = pl.BlockSpec((None, tm, fs), lambda j, i: (j, i, 0))
    wspec = pl.BlockSpec((None, d, fs), lambda j, i: (j, 0, 0))

    def up_epilogue(accs, ins, outs, pids):
        a, b = accs
        outs[0][...] = a.astype(BF16)
        outs[1][...] = b.astype(BF16)
        outs[2][...] = (a * _sigmoid(a) * b).astype(BF16)

    hid = jax.ShapeDtypeStruct((nb, rows, fs), BF16)
    a, b, s = _matmul(
        tag + "_up", (nb, rows // tm), [h, wg, wu],
        [pl.BlockSpec((tm, d), lambda j, i: (i, 0)), wspec, wspec],
        [(0, 1, 0, NN), (0, 2, 1, NN)], [hid, hid, hid], [blk, blk, blk], up_epilogue)

    tn = _tile(d, 512, 128)

    def down_epilogue(accs, ins, outs, pids):
        i = pids[0]
        f = accs[0]
        outs[0][...] = f
        outs[1][...] = ins[2][...] + 0.5 * _select_rows(i, tm, n_lat, ins[3][...]) * f

    out = jax.ShapeDtypeStruct((rows, d), F32)
    ospec = pl.BlockSpec((tm, tn), lambda i, n, k: (i, n))
    f, xo = _matmul(
        tag + "_down", (rows // tm, d // tn, nb), [s, wd, xs, gate2],
        [pl.BlockSpec((None, tm, fs), lambda i, n, k: (k, i, 0)),
         pl.BlockSpec((None, fs, tn), lambda i, n, k: (k, 0, n)),
         ospec, pl.BlockSpec((2, tn), lambda i, n, k: (0, n))],
        [(0, 1, 0, NN)], [out, out], [ospec, ospec], down_epilogue, acc_shapes=[(tm, tn)], nk=nb)
    return a, b, s, f, xo


def _ffn_bwd(tag, df, h, a, b, s, wg, wu, wd):
    rows, d = df.shape
    nb, _, fs = wg.shape
    tm = _tile(rows, 512, 128)
    blk = pl.BlockSpec((None, tm, fs), lambda j, i: (j, i, 0))

    def ds_epilogue(accs, ins, outs, pids):
        ds = accs[0]
        av = ins[2][...].astype(F32)
        bv = ins[3][...].astype(F32)
        sg = _sigmoid(av)
        outs[0][...] = (ds * bv * (sg * (1.0 + av * (1.0 - sg)))).astype(BF16)
        outs[1][...] = (ds * (av * sg)).astype(BF16)

    hid = jax.ShapeDtypeStruct((nb, rows, fs), BF16)
    da, db = _matmul(
        tag + "_ds", (nb, rows // tm), [df, wd, a, b],
        [pl.BlockSpec((tm, d), lambda j, i: (i, 0)), pl.BlockSpec((None, fs, d), lambda j, i: (j, 0, 0)), blk, blk],
        [(0, 1, 0, NT)], [hid, hid], [blk, blk], ds_epilogue)

    def store_all(accs, ins, outs, pids):
        for o, v in zip(outs, accs):
            o[...] = v.astype(o.dtype)

    tn = _tile(d, 512, 128)
    dwd = _matmul(
        tag + "_dwd", (nb, d // tn), [s, df],
        [pl.BlockSpec((None, rows, fs), lambda j, n: (j, 0, 0)), pl.BlockSpec((rows, tn), lambda j, n: (0, n))],
        [(0, 1, 0, TN)], [jax.ShapeDtypeStruct((nb, fs, d), BF16)],
        [pl.BlockSpec((None, fs, tn), lambda j, n: (j, 0, n))], store_all)[0]

    tmo = _tile(d, 512, 128)
    full = pl.BlockSpec((None, rows, fs), lambda j, m: (j, 0, 0))
    wshape = jax.ShapeDtypeStruct((nb, d, fs), BF16)
    wblk = pl.BlockSpec((None, tmo, fs), lambda j, m: (j, m, 0))
    dwg, dwu = _matmul(
        tag + "_dwgu", (nb, d // tmo), [h, da, db],
        [pl.BlockSpec((rows, tmo), lambda j, m: (0, m)), full, full],
        [(0, 1, 0, TN), (0, 2, 1, TN)], [wshape, wshape], [wblk, wblk], store_all)

    ospec = pl.BlockSpec((tm, tn), lambda i, n, k: (i, n))
    aspec = pl.BlockSpec((None, tm, fs), lambda i, n, k: (k, i, 0))
    wspec = pl.BlockSpec((None, tn, fs), lambda i, n, k: (k, n, 0))
    dh = _matmul(
        tag + "_dh", (rows // tm, d // tn, nb), [da, wg, db, wu], [aspec, wspec, aspec, wspec],
        [(0, 1, 0, NT), (2, 3, 0, NT)], [jax.ShapeDtypeStruct((rows, d), F32)], [ospec], store_all,
        acc_shapes=[(tm, tn)], nk=nb)[0]
    return dh, dwg, dwu, dwd


def _swap_halves(x):
    lane = lax.broadcasted_iota(jnp.int32, x.shape, 1)
    return jnp.where((lane % 64) < 32, pltpu.roll(x, 96, 1), pltpu.roll(x, 32, 1))


def _head_spec(tq, blocks_per_shard, first_block):
    def index(h, i):
        blk = first_block + h
        return blk // blocks_per_shard, i, blk % blocks_per_shard
    return pl.BlockSpec((None, tq, LANES), index)


def _qk_prep(name, src, first_block, n_heads, rows, g, cos_t, sin_t):
    tq = _tile(rows, 256, SUBLANES)
    bps = src.shape[-1] // LANES
    tab = pl.BlockSpec((tq, LANES), lambda h, i: (i, 0))

    def body(x_ref, g_ref, c_ref, s_ref, o_ref):
        x = x_ref[...]
        n = x * lax.rsqrt(jnp.mean(x * x, axis=-1, keepdims=True) + NORM_EPS) * g_ref[...]
        o_ref[...] = (n * c_ref[...] + _swap_halves(n) * s_ref[...]).astype(BF16)

    return pl.pallas_call(
        body, name=name, grid=(n_heads, rows // tq),
        in_specs=[_head_spec(tq, bps, first_block), pl.BlockSpec((1, LANES), lambda h, i: (0, 0)), tab, tab],
        out_specs=pl.BlockSpec((None, tq, LANES), lambda h, i: (h, i, 0)),
        out_shape=jax.ShapeDtypeStruct((n_heads, rows, LANES), BF16), compiler_params=_params(2),
    )(src, g, cos_t, sin_t)


def _qk_prep_bwd(name, dy, src, first_block, n_heads, rows, g, cos_t, sin_t):
    tq = _tile(rows, 256, SUBLANES)
    bps = src.shape[-1] // LANES
    tab = pl.BlockSpec((tq, LANES), lambda h, i: (i, 0))

    def body(dy_ref, x_ref, g_ref, c_ref, s_ref, dx_ref, dg_ref):
        x = x_ref[...]
        g = g_ref[...]
        dyv = dy_ref[...]
        rstd = lax.rsqrt(jnp.mean(x * x, axis=-1, keepdims=True) + NORM_EPS)
        xh = x * rstd
        dn = dyv * c_ref[...] + _swap_halves(dyv * s_ref[...])
        dxh = dn * g
        dx_ref[...] = (rstd * (dxh - xh * jnp.mean(dxh * xh, axis=-1, keepdims=True))).astype(BF16)
        first = jnp.logical_and(pl.program_id(0) == 0, pl.program_id(1) == 0)

        @pl.when(first)
        def _():
            dg_ref[...] = _colsum(dn * xh)

        @pl.when(jnp.logical_not(first))
        def _():
            dg_ref[...] += _colsum(dn * xh)

    return pl.pallas_call(
        body, name=name, grid=(n_heads, rows // tq),
        in_specs=[pl.BlockSpec((None, tq, LANES), lambda h, i: (h, i, 0)), _head_spec(tq, bps, first_block),
                  pl.BlockSpec((1, LANES), lambda h, i: (0, 0)), tab, tab],
        out_specs=[pl.BlockSpec((tq, LANES), lambda h, i: (i, h)), pl.BlockSpec((1, LANES), lambda h, i: (0, 0))],
        out_shape=[jax.ShapeDtypeStruct((rows, n_heads * LANES), BF16), jax.ShapeDtypeStruct((1, LANES), F32)],
        compiler_params=_params(2),
    )(dy, src, g, cos_t, sin_t)


def _heads_cast(name, src, first_block, n_heads, rows):
    tq = _tile(rows, 512, SUBLANES)
    bps = src.shape[-1] // LANES

    def body(x_ref, o_ref):
        o_ref[...] = x_ref[...].astype(BF16)

    return pl.pallas_call(
        body, name=name, grid=(n_heads, rows // tq), in_specs=[_head_spec(tq, bps, first_block)],
        out_specs=pl.BlockSpec((None, tq, LANES), lambda h, i: (h, i, 0)),
        out_shape=jax.ShapeDtypeStruct((n_heads, rows, LANES), BF16), compiler_params=_params(2),
    )(src)


def _heads_merge(name, src):
    n_heads, rows, _ = src.shape
    tq = _tile(rows, 512, SUBLANES)

    def body(x_ref, o_ref):
        o_ref[...] = x_ref[...].astype(BF16)

    return pl.pallas_call(
        body, name=name, grid=(n_heads, rows // tq),
        in_specs=[pl.BlockSpec((None, tq, LANES), lambda h, i: (h, i, 0))],
        out_specs=pl.BlockSpec((tq, LANES), lambda h, i: (i, h)),
        out_shape=jax.ShapeDtypeStruct((rows, n_heads * LANES), BF16), compiler_params=_params(2),
    )(src)


def _attn_fwd(q, k, v, q_per_kv):
    nq, l, _ = q.shape
    s_len = k.shape[1]
    tq = _tile(l, 256, SUBLANES)
    scale = LANES ** -0.5
    kv = pl.BlockSpec((None, s_len, LANES), lambda h, i: (h // q_per_kv, 0, 0))

    def body(q_ref, k_ref, v_ref, o_ref):
        s = lax.dot_general(q_ref[...], k_ref[...], NT, preferred_element_type=F32) * scale
        p = jnp.exp(s - jnp.max(s, axis=-1, keepdims=True))
        den = jnp.sum(p, axis=-1, keepdims=True)
        o = jnp.dot(p.astype(BF16), v_ref[...], preferred_element_type=F32)
        o_ref[...] = (o / den).astype(BF16)

    return pl.pallas_call(
        body, name="attn_fwd", grid=(nq, l // tq),
        in_specs=[pl.BlockSpec((None, tq, LANES), lambda h, i: (h, i, 0)), kv, kv],
        out_specs=pl.BlockSpec((tq, LANES), lambda h, i: (i, h)),
        out_shape=jax.ShapeDtypeStruct((l, nq * LANES), BF16), compiler_params=_params(2),
    )(q, k, v)


def _attn_bwd(q, k, v, do, q_per_kv):
    nq, l, _ = q.shape
    nkv, s_len, _ = k.shape
    tq = _tile(l, 256, SUBLANES)
    scale = LANES ** -0.5
    kv = pl.BlockSpec((None, s_len, LANES), lambda g, r, i: (g, 0, 0))
    qs = pl.BlockSpec((None, tq, LANES), lambda g, r, i: (g * q_per_kv + r, i, 0))

    def body(q_ref, k_ref, v_ref, do_ref, dq_ref, dk_ref, dv_ref):
        qv, kvv, vv, dov = q_ref[...], k_ref[...], v_ref[...], do_ref[...]
        st = lax.dot_general(kvv, qv, NT, preferred_element_type=F32) * scale
        e = jnp.exp(st - jnp.max(st, axis=0, keepdims=True))
        pt = e / jnp.sum(e, axis=0, keepdims=True)
        dpt = lax.dot_general(vv, dov, NT, preferred_element_type=F32)
        delta = jnp.sum(pt * dpt, axis=0, keepdims=True)
        dst = (pt * (dpt - delta) * scale).astype(BF16)
        ptb = pt.astype(BF16)
        dq_ref[...] = lax.dot_general(dst, kvv, TN, preferred_element_type=F32)
        dk_new = jnp.dot(dst, qv, preferred_element_type=F32)
        dv_new = jnp.dot(ptb, dov, preferred_element_type=F32)
        first = jnp.logical_and(pl.program_id(1) == 0, pl.program_id(2) == 0)

        @pl.when(first)
        def _():
            dk_ref[...] = dk_new
            dv_ref[...] = dv_new

        @pl.when(jnp.logical_not(first))
        def _():
            dk_ref[...] += dk_new
            dv_ref[...] += dv_new

    return pl.pallas_call(
        body, name="attn_bwd", grid=(nkv, q_per_kv, l // tq),
        in_specs=[qs, kv, kv, pl.BlockSpec((tq, LANES), lambda g, r, i: (i, g * q_per_kv + r))],
        out_specs=[qs, kv, kv],
        out_shape=[jax.ShapeDtypeStruct((nq, l, LANES), F32), jax.ShapeDtypeStruct((nkv, s_len, LANES), F32),
                   jax.ShapeDtypeStruct((nkv, s_len, LANES), F32)],
        compiler_params=_params(3),
    )(q, k, v, do)


def _zoh(a_re, a_im, log_dt):
    dt = jnp.exp(log_dt)[:, None]
    mag = jnp.exp(a_re * dt)
    lb_re = mag * jnp.cos(a_im * dt)
    lb_im = mag * jnp.sin(a_im * dt)
    den = a_re * a_re + a_im * a_im
    coef_re = ((lb_re - 1.0) * a_re + lb_im * a_im) / den
    coef_im = (lb_im * a_re - (lb_re - 1.0) * a_im) / den
    return lb_re, lb_im, coef_re, coef_im


def _ssm_discretize(a_re, a_im, log_dt, b_re, b_im):
    lb_re, lb_im, cr, ci = _zoh(a_re, a_im, log_dt)
    bt_re = cr[..., None] * b_re - ci[..., None] * b_im
    bt_im = cr[..., None] * b_im + ci[..., None] * b_re
    return lb_re, lb_im, bt_re, bt_im


def _block_diag(m):
    g, a, b = m.shape
    m = m.reshape(g // SLAB_GROUPS, SLAB_GROUPS, a, b)
    eye = jnp.eye(SLAB_GROUPS, dtype=m.dtype)
    return jnp.einsum("sgab,gh->sgahb", m, eye).reshape(g // SLAB_GROUPS, SLAB_GROUPS * a, SLAB_GROUPS * b)


def _block_diag_extract(m, a, b):
    ns = m.shape[0]
    m = m.reshape(ns, SLAB_GROUPS, a, SLAB_GROUPS, b)
    eye = jnp.eye(SLAB_GROUPS, dtype=m.dtype)
    return jnp.einsum("sgahb,gh->sgab", m, eye).reshape(ns * SLAB_GROUPS, a, b)


def _tap_weights(base_re, base_im, pw_re, pw_im):
    parts_re, parts_im = [], []
    for tau in range(SCAN_TAPS):
        pr, pi = pw_re[tau][:, None, :], pw_im[tau][:, None, :]
        parts_re.append(pr * base_re - pi * base_im)
        parts_im.append(pr * base_im + pi * base_re)
    return jnp.concatenate([jnp.concatenate(parts_re, axis=1), jnp.concatenate(parts_im, axis=1)], axis=-1)


def _carry_table(pw_re, pw_im, descending):
    order = [SCAN_TAPS - r for r in range(SCAN_TAPS)] if descending else [r + 1 for r in range(SCAN_TAPS)]
    re = jnp.stack([pw_re[k] for k in order], axis=1)
    im = jnp.stack([pw_im[k] for k in order], axis=1)
    return jnp.concatenate([re, im], axis=-1)


def _scan_chunk(x, w_ref, tab_ref, s_ref, carry_ref, descending, t_rows, sw):
    row8 = lax.broadcasted_iota(jnp.int32, x.shape, 0) % SCAN_TAPS
    pieces = [x.astype(BF16)]
    for tau in range(1, SCAN_TAPS):
        if descending:
            sh = jnp.where(row8 <= SCAN_TAPS - 1 - tau, pltpu.roll(x, t_rows - tau, 0), 0.0)
        else:
            sh = jnp.where(row8 >= tau, pltpu.roll(x, tau, 0), 0.0)
        pieces.append(sh.astype(BF16))
    xa = jnp.concatenate(pieces, axis=1)
    s_ref[...] = jnp.dot(xa, w_ref[...], preferred_element_type=F32)
    tab = tab_ref[...]
    t_re, t_im = tab[:, :sw], tab[:, sw:]
    nb = t_rows // SCAN_TAPS
    edge = 0 if descending else SCAN_TAPS - 1

    def step(b, carry):
        h_re, h_im = carry
        r0 = pl.multiple_of(((nb - 1 - b) if descending else b) * SCAN_TAPS, SCAN_TAPS)
        x_re = s_ref[pl.ds(r0, SCAN_TAPS), :sw] + t_re * h_re - t_im * h_im
        x_im = s_ref[pl.ds(r0, SCAN_TAPS), sw:] + t_re * h_im + t_im * h_re
        s_ref[pl.ds(r0, SCAN_TAPS), :sw] = x_re
        s_ref[pl.ds(r0, SCAN_TAPS), sw:] = x_im
        return x_re[edge:edge + 1, :], x_im[edge:edge + 1, :]

    h_re, h_im = lax.fori_loop(0, nb, step, (carry_ref[0:1, :sw], carry_ref[0:1, sw:]))
    carry_ref[0:1, :sw] = h_re
    carry_ref[0:1, sw:] = h_im


def _ssm_fwd(name, u_src, u_shard, waug, tab, cd, descending, chunk_of, t_rows, rows):
    ns, kdim, sw2 = waug.shape
    sw = sw2 // 2
    width = ns * LANES
    nchunks = rows // t_rows

    def body(u_ref, w_ref, tab_ref, cd_ref, y_ref, h_ref, s_ref, carry_ref):
        @pl.when(pl.program_id(1) == 0)
        def _():
            carry_ref[...] = jnp.zeros_like(carry_ref)

        _scan_chunk(u_ref[...], w_ref, tab_ref, s_ref, carry_ref, descending, t_rows, sw)
        hb = s_ref[...].astype(BF16)
        h_ref[...] = hb
        y_ref[...] = jnp.dot(hb, cd_ref[...], preferred_element_type=F32)

    return pl.pallas_call(
        body, name=name, grid=(ns, nchunks),
        in_specs=[pl.BlockSpec((None, t_rows, LANES), lambda s, i: (u_shard, chunk_of(i), s)),
                  pl.BlockSpec((None, kdim, sw2), lambda s, i: (s, 0, 0)),
                  pl.BlockSpec((None, SCAN_TAPS, sw2), lambda s, i: (s, 0, 0)),
                  pl.BlockSpec((None, sw2, LANES), lambda s, i: (s, 0, 0))],
        out_specs=[pl.BlockSpec((t_rows, LANES), lambda s, i: (chunk_of(i), s)),
                   pl.BlockSpec((None, t_rows, sw2), lambda s, i: (s, chunk_of(i), 0))],
        out_shape=[jax.ShapeDtypeStruct((rows, width), F32), jax.ShapeDtypeStruct((ns, rows, sw2), BF16)],
        scratch_shapes=[pltpu.VMEM((t_rows, sw2), F32), pltpu.VMEM((SUBLANES, sw2), F32)],
        compiler_params=_params(2),
    )(u_src, waug, tab, cd)


def _ssm_bwd(name, dy, u_src, u_shard, states, caug, tab, bdt, descending, chunk_of, t_rows, rows):
    ns, kdim, sw2 = caug.shape
    sw = sw2 // 2
    width = ns * LANES
    nchunks = rows // t_rows

    def body(dy_ref, u_ref, h_ref, w_ref, tab_ref, bdt_ref, du_ref, dbd_ref, dcd_ref, dlam_ref,
             s_ref, carry_ref, gsave_ref):
        first = pl.program_id(1) == 0

        @pl.when(first)
        def _():
            carry_ref[...] = jnp.zeros_like(carry_ref)
            gsave_ref[...] = jnp.zeros_like(gsave_ref)

        dyv = dy_ref[...]
        _scan_chunk(dyv, w_ref, tab_ref, s_ref, carry_ref, descending, t_rows, sw)
        g = s_ref[...]
        gb = g.astype(BF16)
        du_ref[...] = jnp.dot(gb, bdt_ref[...], preferred_element_type=F32)
        dbd = lax.dot_general(u_ref[...].astype(BF16), gb, TN, preferred_element_type=F32)
        hb = h_ref[...]
        dcd = lax.dot_general(hb, dyv.astype(BF16), TN, preferred_element_type=F32)
        hf = hb.astype(F32)
        rowid = lax.broadcasted_iota(jnp.int32, hf.shape, 0)
        if descending:
            hp = jnp.where(rowid == 0, 0.0, pltpu.roll(hf, 1, 0))
            h_edge, g_edge = hf[t_rows - 1:t_rows, :], g[0:1, :]
        else:
            hp = jnp.where(rowid == t_rows - 1, 0.0, pltpu.roll(hf, t_rows - 1, 0))
            h_edge, g_edge = hf[0:1, :], g[t_rows - 1:t_rows, :]
        g_re, g_im, hp_re, hp_im = g[:, :sw], g[:, sw:], hp[:, :sw], hp[:, sw:]
        gs = gsave_ref[0:1, :]
        gs_re, gs_im, he_re, he_im = gs[:, :sw], gs[:, sw:], h_edge[:, :sw], h_edge[:, sw:]
        dl_re = _colsum(g_re * hp_re + g_im * hp_im) + gs_re * he_re + gs_im * he_im
        dl_im = _colsum(g_im * hp_re - g_re * hp_im) + gs_im * he_re - gs_re * he_im
        gsave_ref[0:1, :] = g_edge

        @pl.when(first)
        def _():
            dbd_ref[...] = dbd
            dcd_ref[...] = dcd
            dlam_ref[:, :sw] = dl_re
            dlam_ref[:, sw:] = dl_im

        @pl.when(jnp.logical_not(first))
        def _():
            dbd_ref[...] += dbd
            dcd_ref[...] += dcd
            dlam_ref[:, :sw] += dl_re
            dlam_ref[:, sw:] += dl_im

    return pl.pallas_call(
        body, name=name, grid=(ns, nchunks),
        in_specs=[pl.BlockSpec((t_rows, LANES), lambda s, i: (chunk_of(i), s)),
                  pl.BlockSpec((None, t_rows, LANES), lambda s, i: (u_shard, chunk_of(i), s)),
                  pl.BlockSpec((None, t_rows, sw2), lambda s, i: (s, chunk_of(i), 0)),
                  pl.BlockSpec((None, kdim, sw2), lambda s, i: (s, 0, 0)),
                  pl.BlockSpec((None, SCAN_TAPS, sw2), lambda s, i: (s, 0, 0)),
                  pl.BlockSpec((None, sw2, LANES), lambda s, i: (s, 0, 0))],
        out_specs=[pl.BlockSpec((t_rows, LANES), lambda s, i: (chunk_of(i), s)),
                   pl.BlockSpec((None, LANES, sw2), lambda s, i: (s, 0, 0)),
                   pl.BlockSpec((None, sw2, LANES), lambda s, i: (s, 0, 0)),
                   pl.BlockSpec((None, 1, sw2), lambda s, i: (s, 0, 0))],
        out_shape=[jax.ShapeDtypeStruct((rows, width), F32), jax.ShapeDtypeStruct((ns, LANES, sw2), F32),
                   jax.ShapeDtypeStruct((ns, sw2, LANES), F32), jax.ShapeDtypeStruct((ns, 1, sw2), F32)],
        scratch_shapes=[pltpu.VMEM((t_rows, sw2), F32), pltpu.VMEM((SUBLANES, sw2), F32),
                        pltpu.VMEM((SUBLANES, sw2), F32)],
        compiler_params=_params(2),
    )(dy, u_src, states, caug, tab, bdt)


def _mod_fwd(cs, w_mod, b_cols):
    d, width = w_mod.shape
    tn = _tile(width, 768, LANES)

    def epilogue(accs, ins, outs, pids):
        outs[0][...] = accs[0] + ins[2][...]

    return _matmul(
        "mod_fwd", (width // tn,), [cs, w_mod, b_cols],
        [pl.BlockSpec((16, d), lambda n: (0, 0)), pl.BlockSpec((d, tn), lambda n: (0, n)),
         pl.BlockSpec((1, tn), lambda n: (0, n))],
        [(0, 1, 0, NN)], [jax.ShapeDtypeStruct((16, width), F32)], [pl.BlockSpec((16, tn), lambda n: (0, n))],
        epilogue, prologue={0: lambda v: v * _sigmoid(v)})[0]


def _mod_bwd_adam(cs, dmod_cols, w, m, v):
    d, width = w.shape
    tn = _tile(width, 128, LANES)
    col = pl.BlockSpec((d, tn), lambda n: (0, n))

    def body(cs_ref, dm_ref, w_ref, m_ref, v_ref, g_ref, dl_ref, nm_ref, nv_ref, ds_ref):
        n = pl.program_id(0)
        lat = dm_ref[pl.ds(0, N_DEV, stride=SUBLANES), :]
        ctx = jnp.sum(dm_ref[pl.ds(1, N_DEV, stride=SUBLANES), :], axis=0, keepdims=True)
        row = lax.broadcasted_iota(jnp.int32, lat.shape, 0)
        dm = jnp.concatenate([lat, jnp.where(row == 0, ctx, 0.0)], axis=0).astype(BF16)
        c = cs_ref[...]
        sc = (c * _sigmoid(c)).astype(BF16)
        wv = w_ref[...]
        g = lax.dot_general(sc, dm, TN, preferred_element_type=F32)
        delta, m2, v2 = _adamw(wv, g, m_ref[...], v_ref[...])
        g_ref[...] = g
        dl_ref[...] = delta
        nm_ref[...] = m2
        nv_ref[...] = v2
        part = lax.dot_general(dm, wv.astype(BF16), NT, preferred_element_type=F32)

        @pl.when(n == 0)
        def _():
            ds_ref[...] = part

        @pl.when(n > 0)
        def _():
            ds_ref[...] += part

    shard = jax.ShapeDtypeStruct((d, width), F32)
    return pl.pallas_call(
        body, name="mod_bwd_adam", grid=(width // tn,),
        in_specs=[pl.BlockSpec((16, d), lambda n: (0, 0)), pl.BlockSpec((N_DEV * SUBLANES, tn), lambda n: (0, n)),
                  col, col, col],
        out_specs=[col, col, col, col, pl.BlockSpec((16, d), lambda n: (0, 0))],
        out_shape=[shard, shard, shard, shard, jax.ShapeDtypeStruct((16, d), F32)],
        compiler_params=_params(1),
    )(cs, dmod_cols, w, m, v)


def _sum_adam(name, parts, w, m, v):
    rows, cols = w.shape
    align = 16 if parts.dtype == BF16 else SUBLANES
    tr = _tile(rows, max(align, (4 * 1024 * 1024) // (cols * 44)), align)
    blk = pl.BlockSpec((tr, cols), lambda i: (i, 0))

    def body(p_ref, w_ref, m_ref, v_ref, g_ref, dl_ref, nm_ref, nv_ref):
        g = p_ref[0].astype(F32)
        for s in range(1, N_DEV):
            g = g + p_ref[s].astype(F32)
        delta, m2, v2 = _adamw(w_ref[...], g, m_ref[...], v_ref[...])
        g_ref[...] = g
        dl_ref[...] = delta
        nm_ref[...] = m2
        nv_ref[...] = v2

    out = jax.ShapeDtypeStruct((rows, cols), F32)
    return pl.pallas_call(
        body, name=name, grid=(rows // tr,),
        in_specs=[pl.BlockSpec((N_DEV, tr, cols), lambda i: (0, i, 0)), blk, blk, blk],
        out_specs=[blk, blk, blk, blk], out_shape=[out, out, out, out], compiler_params=_params(1),
    )(parts, w, m, v)


def _bias_adam(dmod_all, w, m, v):
    width = w.shape[-1]
    tn = _tile(width, 2048, LANES)
    blk = pl.BlockSpec((1, tn), lambda n: (0, n))

    def body(p_ref, w_ref, m_ref, v_ref, g_ref, dl_ref, nm_ref, nv_ref):
        g = jnp.sum(p_ref[...], axis=0, keepdims=True)
        delta, m2, v2 = _adamw(w_ref[...], g, m_ref[...], v_ref[...])
        g_ref[...] = g
        dl_ref[...] = delta
        nm_ref[...] = m2
        nv_ref[...] = v2

    out = jax.ShapeDtypeStruct((1, width), F32)
    return pl.pallas_call(
        body, name="bias_adam", grid=(width // tn,),
        in_specs=[pl.BlockSpec((dmod_all.shape[0], tn), lambda n: (0, n)), blk, blk, blk],
        out_specs=[blk, blk, blk, blk], out_shape=[out, out, out, out], compiler_params=_params(1),
    )(dmod_all, w, m, v)


def _pack(arrays, total_rows):
    flat = []
    for a in arrays:
        a = a.reshape(-1).astype(F32)
        flat.append(jnp.pad(a, (0, (-a.shape[0]) % LANES)))
    flat = jnp.concatenate(flat).reshape(-1, LANES)
    return jnp.pad(flat, ((0, total_rows - flat.shape[0]), (0, 0)))


def _unpack(packed, shapes):
    out, row = [], 0
    for shp in shapes:
        size = math.prod(shp)
        nrows = -(-size // LANES)
        out.append(packed[row:row + nrows].reshape(-1)[:size].reshape(shp))
        row += nrows
    return out


def kernel(x, c, ctx, c_ctx, w_mod, b_mod, norm_g, w_ffn1_gate, w_ffn1_up, w_ffn1_down, w_in, q_norm_g, k_norm_g, ssm_a_re, ssm_a_im, ssm_log_dt, ssm_b_re, ssm_b_im, ssm_c_re, ssm_c_im, ssm_d, w_glu, b_glu, w_br_attn, w_br_ssm, w_out, w_ffn2_gate, w_ffn2_up, w_ffn2_down, loss_target, m_c_ctx, m_w_mod, m_b_mod, m_norm_g, m_w_ffn1_gate, m_w_ffn1_up, m_w_ffn1_down, m_w_in, m_q_norm_g, m_k_norm_g, m_ssm_a_re, m_ssm_a_im, m_ssm_log_dt, m_ssm_b_re, m_ssm_b_im, m_ssm_c_re, m_ssm_c_im, m_ssm_d, m_w_glu, m_b_glu, m_w_br_attn, m_w_br_ssm, m_w_out, m_w_ffn2_gate, m_w_ffn2_up, m_w_ffn2_down, v_c_ctx, v_w_mod, v_b_mod, v_norm_g, v_w_ffn1_gate, v_w_ffn1_up, v_w_ffn1_down, v_w_in, v_q_norm_g, v_k_norm_g, v_ssm_a_re, v_ssm_a_im, v_ssm_log_dt, v_ssm_b_re, v_ssm_b_im, v_ssm_c_re, v_ssm_c_im, v_ssm_d, v_w_glu, v_b_glu, v_w_br_attn, v_w_br_ssm, v_w_out, v_w_ffn2_gate, v_w_ffn2_up, v_w_ffn2_down):
    _, L, D = x.shape
    Lc = ctx.shape[1]
    R = L + Lc
    MODW = w_mod.shape[-1]
    INS = w_in.shape[-1]
    KVW = INS // 2
    NQ = D // LANES
    NKV = KVW // LANES
    QPK = NQ // NKV
    G, P, E = ssm_b_re.shape[2:]
    W = G * E
    SW = SLAB_GROUPS * P
    assert E * SLAB_GROUPS == LANES and W == INS and NQ * LANES == D and Lc <= L
    me = 4 * lax.axis_index("x") + 2 * lax.axis_index("y") + lax.axis_index("c")

    x2, ctx2, tgt = x[0], ctx[0], loss_target[0]
    xc0 = jnp.concatenate([x2, ctx2], axis=0)

    def widen(a):
        return jnp.pad(a[0], ((0, 0), (0, D - a.shape[-1])))

    def at_row(a, r, total):
        return jnp.pad(a, ((r, total - r - a.shape[0]), (0, 0)))

    pack_in = (at_row(c, 0, 16) + at_row(widen(norm_g), 1, 16) + at_row(widen(m_norm_g), 4, 16)
               + at_row(widen(v_norm_g), 7, 16))
    (g_in,) = _all_gather("ag_inputs", [pack_in])
    c_all = g_in[:, 0, :]
    dn = D // N_DEV

    def full_norm(k):
        return jnp.transpose(g_in[:, k:k + 3, :dn], (1, 0, 2)).reshape(3, D)

    ng_full, m_ng_full, v_ng_full = full_norm(1), full_norm(4), full_norm(7)
    cs = at_row(c_all, 0, 16) + at_row(c_ctx[None, :], 8, 16)

    b_cols = lax.dynamic_slice_in_dim(b_mod, me * MODW, MODW, axis=1)
    mod_blk = _mod_fwd(cs, w_mod[0], b_cols)
    (mod_g,) = _all_gather("ag_mod", [mod_blk])
    mod_lat = lax.dynamic_index_in_dim(mod_g, me, axis=1, keepdims=False).reshape(-1)
    mod_ctx = mod_g[:, 8, :].reshape(-1)
    sh1, sc1, g1, sh2, sc2, g2, sh3, sc3, g3 = [mod_lat[k * D:(k + 1) * D] for k in range(9)]
    mc0, mc1, mc2, mc3, mc4 = [mod_ctx[k * D:(k + 1) * D] for k in range(5)]

    def grp(a, b):
        return jnp.stack([a, b])[:, None, :]

    gam = [ng_full[k][None, :] for k in range(3)]

    shards = [w_ffn1_gate[0], w_ffn1_up[0], w_ffn1_down[0], w_in[0], w_glu[0], w_br_attn[0], w_br_ssm[0],
              w_out[0], w_ffn2_gate[0], w_ffn2_up[0], w_ffn2_down[0]]
    (wg1, wu1, wd1, win, wglu, wbra, wbrs, wout, wg2, wu2, wd2) = _all_gather(
        "ag_weights", [s.astype(BF16) for s in shards])
    wglu2 = wglu.reshape(W, W)
    wbra2 = wbra.reshape(D, D)
    wout2 = wout.reshape(D, D)

    gate1 = grp(g1, mc2)
    h1 = _norm_mod_fwd("nm1_fwd", xc0, gam[0], grp(sh1, mc0), grp(sc1, mc1), L, Lc)
    a1, b1, s1, f1, xc1 = _ffn_fwd("ffn1", xc0, h1, wg1, wu1, wd1, gate1[:, 0, :], L)

    h2 = _norm_mod_fwd("nm2_fwd", xc1, gam[1], grp(sh2, mc3), grp(sc2, mc4), L, Lc)
    tm = _tile(R, 512, 128)
    tml = _tile(L, 512, 128)

    def store_f32(accs, ins, outs, pids):
        outs[0][...] = accs[0]

    p01 = _matmul(
        "in_proj_kvu", (2, R // tm), [h2, win],
        [pl.BlockSpec((tm, D), lambda j, i: (i, 0)), pl.BlockSpec((None, D, INS), lambda j, i: (j, 0, 0))],
        [(0, 1, 0, NN)], [jax.ShapeDtypeStruct((2, R, INS), F32)],
        [pl.BlockSpec((None, tm, INS), lambda j, i: (j, i, 0))], store_f32)[0]
    p27 = _matmul(
        "in_proj_qg", (6, L // tml), [h2, win],
        [pl.BlockSpec((tml, D), lambda j, i: (i, 0)), pl.BlockSpec((None, D, INS), lambda j, i: (j + 2, 0, 0))],
        [(0, 1, 0, NN)], [jax.ShapeDtypeStruct((6, L, INS), F32)],
        [pl.BlockSpec((None, tml, INS), lambda j, i: (j, i, 0))], store_f32)[0]

    half = LANES // 4
    inv_freq = ROPE_THETA ** (-jnp.arange(half, dtype=F32) / half)
    pos = jnp.arange(L)
    ang_r = (pos // GRID_W).astype(F32)[:, None] * inv_freq
    ang_c = (pos % GRID_W).astype(F32)[:, None] * inv_freq
    cos_l = jnp.concatenate([jnp.cos(ang_r)] * 2 + [jnp.cos(ang_c)] * 2, axis=1)
    sin_l = jnp.concatenate([-jnp.sin(ang_r), jnp.sin(ang_r), -jnp.sin(ang_c), jnp.sin(ang_c)], axis=1)
    cos_all = jnp.concatenate([cos_l, jnp.ones((Lc, LANES), F32)], axis=0)
    sin_all = jnp.concatenate([sin_l, jnp.zeros((Lc, LANES), F32)], axis=0)

    q_rot = _qk_prep("q_prep", p27, 0, NQ, L, q_norm_g, cos_l, sin_l)
    k_rot = _qk_prep("k_prep", p01, 0, NKV, R, k_norm_g, cos_all, sin_all)
    v_hd = _heads_cast("v_heads", p01, NKV, NKV, R)
    attn = _attn_fwd(q_rot, k_rot, v_hd, QPK)

    t_rows = _tile(math.gcd(L, Lc), 256, SUBLANES)
    nl, ncx = L // t_rows, Lc // t_rows
    nch = nl + ncx
    ssm = []
    for d_ in range(2):
        lb_re, lb_im, bt_re, bt_im = _ssm_discretize(
            ssm_a_re[0, d_], ssm_a_im[0, d_], ssm_log_dt[0, d_], ssm_b_re[0, d_], ssm_b_im[0, d_])
        ns = G // SLAB_GROUPS
        lam_re, lam_im = lb_re.reshape(ns, SW), lb_im.reshape(ns, SW)
        pw_re, pw_im = [jnp.ones_like(lam_re)], [jnp.zeros_like(lam_im)]
        for _ in range(SCAN_TAPS):
            pw_re, pw_im = (pw_re + [pw_re[-1] * lam_re - pw_im[-1] * lam_im],
                            pw_im + [pw_re[-1] * lam_im + pw_im[-1] * lam_re])
        cj_im = [-p for p in pw_im]
        bd_re = _block_diag(jnp.transpose(bt_re, (0, 2, 1)))
        bd_im = _block_diag(jnp.transpose(bt_im, (0, 2, 1)))
        ct_re = _block_diag(ssm_c_re[0, d_])
        ct_im = _block_diag(-ssm_c_im[0, d_])
        fwd_desc = d_ == 1
        ssm.append(dict(
            waug=_tap_weights(bd_re, bd_im, pw_re, pw_im).astype(BF16),
            tab=_carry_table(pw_re, pw_im, fwd_desc),
            cd=jnp.concatenate([jnp.transpose(ct_re, (0, 2, 1)), jnp.transpose(ct_im, (0, 2, 1))], axis=1).astype(BF16),
            caug=_tap_weights(ct_re, ct_im, pw_re, cj_im).astype(BF16),
            tabc=_carry_table(pw_re, cj_im, not fwd_desc),
            bdt=jnp.concatenate([jnp.transpose(bd_re, (0, 2, 1)), jnp.transpose(bd_im, (0, 2, 1))], axis=1).astype(BF16),
            fwd_desc=fwd_desc))
    order = [lambda i: (i + nl) % nch, lambda i: nch - 1 - i]
    order_adj = [lambda i: (nch - 1 - i + nl) % nch, lambda i: i]
    y_dir, states = [], []
    for d_ in range(2):
        yd, hd = _ssm_fwd("ssm_fwd%d" % d_, p01, 1, ssm[d_]["waug"], ssm[d_]["tab"], ssm[d_]["cd"],
                          ssm[d_]["fwd_desc"], order[d_], t_rows, R)
        y_dir.append(yd)
        states.append(hd)

    tr = _row_tile(L, 0)
    rowW = pl.BlockSpec((tr, W), lambda i: (i, 0))
    vecW = pl.BlockSpec((1, W), lambda i: (0, 0))
    u_lat = pl.BlockSpec((None, tr, W), lambda i: (1, i, 0))

    def ssm_post(i, u, y0, y1, dvec):
        sv = dvec * u + y0 + y1
        return [sv, _gelu(sv)], []

    (ssm_out, yg), _ = _rowwise(
        "ssm_post", L // tr, [p01, y_dir[0], y_dir[1], ssm_d], [u_lat, rowW, rowW, vecW],
        [jax.ShapeDtypeStruct((L, W), F32), jax.ShapeDtypeStruct((L, W), BF16)], [rowW, rowW], [], ssm_post)

    tnw = _tile(W, 512, 128)

    def glu_epilogue(accs, ins, outs, pids):
        z = accs[0] + ins[3][...]
        outs[0][...] = z
        outs[1][...] = (_gelu(ins[2][...]) * _sigmoid(z)).astype(BF16)

    z_glu, y2 = _matmul(
        "glu", (L // tml, W // tnw), [yg, wglu2, ssm_out, b_glu],
        [pl.BlockSpec((tml, W), lambda i, n: (i, 0)), pl.BlockSpec((W, tnw), lambda i, n: (0, n)),
         pl.BlockSpec((tml, tnw), lambda i, n: (i, n)), pl.BlockSpec((1, tnw), lambda i, n: (0, n))],
        [(0, 1, 0, NN)], [jax.ShapeDtypeStruct((L, W), F32), jax.ShapeDtypeStruct((L, W), BF16)],
        [pl.BlockSpec((tml, tnw), lambda i, n: (i, n))] * 2, glu_epilogue)

    tnd = _tile(D, 512, 128)
    out_ld = pl.BlockSpec((tml, tnd), lambda i, n: (i, n))
    br_a = _matmul(
        "br_attn", (L // tml, D // tnd), [attn, wbra2],
        [pl.BlockSpec((tml, D), lambda i, n: (i, 0)), pl.BlockSpec((D, tnd), lambda i, n: (0, n))],
        [(0, 1, 0, NN)], [jax.ShapeDtypeStruct((L, D), F32)], [out_ld], store_f32)[0]

    cb = wbrs.shape[-1]
    gpb = INS // cb

    def gate_spec(first_shard):
        return pl.BlockSpec((None, tml, cb), lambda i, j: (first_shard + j // gpb, i, j % gpb))

    def merge_epilogue(accs, ins, outs, pids):
        br_s = accs[0]
        outs[0][...] = br_s
        outs[1][...] = (_sigmoid(ins[2][...]) * ins[4][...] + _sigmoid(ins[3][...]) * br_s).astype(BF16)

    col_blk = pl.BlockSpec((tml, cb), lambda i, j: (i, j))
    br_s, merged = _matmul(
        "br_ssm_merge", (L // tml, N_DEV), [y2, wbrs, p27, p27, br_a],
        [pl.BlockSpec((tml, W), lambda i, j: (i, 0)), pl.BlockSpec((None, W, cb), lambda i, j: (j, 0, 0)),
         gate_spec(2), gate_spec(4), col_blk],
        [(0, 1, 0, NN)], [jax.ShapeDtypeStruct((L, D), F32), jax.ShapeDtypeStruct((L, D), BF16)],
        [col_blk, col_blk], merge_epilogue)

    def out_epilogue(accs, ins, outs, pids):
        outs[0][...] = accs[0]
        outs[1][...] = ins[2][...] + ins[3][...] * accs[0]

    g2row = g2[None, :]
    mix, x2_ = _matmul(
        "out_proj", (L // tml, D // tnd), [merged, wout2, xc1, g2row],
        [pl.BlockSpec((tml, D), lambda i, n: (i, 0)), pl.BlockSpec((D, tnd), lambda i, n: (0, n)), out_ld,
         pl.BlockSpec((1, tnd), lambda i, n: (0, n))],
        [(0, 1, 0, NN)], [jax.ShapeDtypeStruct((L, D), F32)] * 2, [out_ld, out_ld], out_epilogue)

    gate3 = grp(g3, g3)
    h3 = _norm_mod_fwd("nm3_fwd", x2_, gam[2], grp(sh3, sh3), grp(sc3, sc3), L, 0)
    a3, b3, s3, f3, x3 = _ffn_fwd("ffn2", x2_, h3, wg2, wu2, wd2, gate3[:, 0, :], L)

    trd = _row_tile(L, 0)
    rowD = pl.BlockSpec((trd, D), lambda i: (i, 0))

    def loss_fn(i, y, t):
        err = y - t
        return [err * (1.0 / D)], [_colsum(err * err)]

    (dx3,), (sq,) = _rowwise("loss", L // trd, [x3, tgt], [rowD, rowD],
                             [jax.ShapeDtypeStruct((L, D), F32)], [rowD], [D], loss_fn)
    loss = lax.psum(0.5 * jnp.sum(sq) / D, ("x", "y", "c"))

    df3, (dg3, _) = _gate_bwd("gate3_bwd", dx3, f3, gate3, 0.5, L, 0)
    dh3, dwg2, dwu2, dwd2 = _ffn_bwd("ffn2b", df3, h3, a3, b3, s3, wg2, wu2, wd2)
    dx2, (dsh3, dsc3, _, _, dgam3) = _norm_mod_bwd("nm3_bwd", x2_, dh3, gam[2], grp(sc3, sc3), L, 0, dres=dx3)

    dmix, (dg2, _) = _gate_bwd("gate2_bwd", dx2, mix, grp(g2, g2), 1.0, L, 0)

    def store_bf16(accs, ins, outs, pids):
        outs[0][...] = accs[0].astype(BF16)

    def dmerged_epilogue(accs, ins, outs, pids):
        dm = accs[0]
        ga, gs = _sigmoid(ins[2][...]), _sigmoid(ins[3][...])
        outs[0][...] = (ga * dm).astype(BF16)
        outs[1][...] = (gs * dm).astype(BF16)
        outs[2][...] = (dm * ins[4][...] * ga * (1.0 - ga)).astype(BF16)
        outs[3][...] = (dm * ins[5][...] * gs * (1.0 - gs)).astype(BF16)

    dgate_spec = pl.BlockSpec((None, tml, cb), lambda i, j: (j // gpb, i, j % gpb))
    d_br_a, d_br_s, dg_a, dg_s = _matmul(
        "dmerged", (L // tml, N_DEV), [dmix, wout2, p27, p27, br_a, br_s],
        [pl.BlockSpec((tml, D), lambda i, j: (i, 0)), pl.BlockSpec((cb, D), lambda i, j: (j, 0)),
         gate_spec(2), gate_spec(4), col_blk, col_blk],
        [(0, 1, 0, NT)],
        [jax.ShapeDtypeStruct((L, D), BF16)] * 2 + [jax.ShapeDtypeStruct((2, L, INS), BF16)] * 2,
        [col_blk, col_blk, dgate_spec, dgate_spec], dmerged_epilogue)

    def wgrad(name, a_mat, b_mat, tmo, tno):
        ka, ma = a_mat.shape
        _, nb_ = b_mat.shape
        return _matmul(
            name, (ma // tmo, nb_ // tno), [a_mat, b_mat],
            [pl.BlockSpec((ka, tmo), lambda m, n: (0, m)), pl.BlockSpec((ka, tno), lambda m, n: (0, n))],
            [(0, 1, 0, TN)], [jax.ShapeDtypeStruct((ma, nb_), BF16)],
            [pl.BlockSpec((tmo, tno), lambda m, n: (m, n))], store_bf16)[0]

    dwout = wgrad("dw_out", merged, dmix, tnd, tnd)
    dwbra = wgrad("dw_br_attn", attn, d_br_a, tnd, tnd)
    d_attn = _matmul(
        "d_attn", (L // tml, D // tnd), [d_br_a, wbra2],
        [pl.BlockSpec((tml, D), lambda i, n: (i, 0)), pl.BlockSpec((tnd, D), lambda i, n: (n, 0))],
        [(0, 1, 0, NT)], [jax.ShapeDtypeStruct((L, D), BF16)], [out_ld], store_bf16)[0]

    dwbrs = _matmul(
        "dw_br_ssm", (N_DEV,), [y2, d_br_s],
        [pl.BlockSpec((L, W), lambda j: (0, 0)), pl.BlockSpec((L, cb), lambda j: (0, j))],
        [(0, 1, 0, TN)], [jax.ShapeDtypeStruct((N_DEV, W, cb), BF16)],
        [pl.BlockSpec((None, W, cb), lambda j: (j, 0, 0))], store_bf16)[0]

    def dy2_epilogue(accs, ins, outs, pids):
        dy2 = accs[0]
        sg = _sigmoid(ins[2][...])
        outs[0][...] = dy2 * sg
        outs[1][...] = (dy2 * _gelu(ins[3][...]) * sg * (1.0 - sg)).astype(BF16)

    wn_blk = pl.BlockSpec((tml, tnw), lambda i, n, k: (i, n))
    dyg1, dz = _matmul(
        "d_y2", (L // tml, W // tnw, N_DEV), [d_br_s, wbrs, z_glu, ssm_out],
        [pl.BlockSpec((tml, cb), lambda i, n, k: (i, k)), pl.BlockSpec((None, tnw, cb), lambda i, n, k: (k, n, 0)),
         wn_blk, wn_blk],
        [(0, 1, 0, NT)], [jax.ShapeDtypeStruct((L, W), F32), jax.ShapeDtypeStruct((L, W), BF16)],
        [wn_blk, wn_blk], dy2_epilogue, acc_shapes=[(tml, tnw)], nk=N_DEV)

    dwglu = wgrad("dw_glu", yg, dz, tnw, tnw)

    def dssm_epilogue(accs, ins, outs, pids):
        outs[0][...] = (accs[0] + ins[2][...]) * _gelu_grad(ins[3][...])

    wn2 = pl.BlockSpec((tml, tnw), lambda i, n: (i, n))
    dssm = _matmul(
        "d_ssm", (L // tml, W // tnw), [dz, wglu2, dyg1, ssm_out],
        [pl.BlockSpec((tml, W), lambda i, n: (i, 0)), pl.BlockSpec((tnw, W), lambda i, n: (n, 0)), wn2, wn2],
        [(0, 1, 0, NT)], [jax.ShapeDtypeStruct((L, W), F32)], [wn2], dssm_epilogue)[0]

    dssm_all = jnp.concatenate([dssm, jnp.zeros((Lc, W), F32)], axis=0)
    du_dir, ssm_grads = [], []
    for d_ in range(2):
        du_d, dbd, dcd, dlam = _ssm_bwd(
            "ssm_bwd%d" % d_, dssm_all, p01, 1, states[d_], ssm[d_]["caug"], ssm[d_]["tabc"], ssm[d_]["bdt"],
            not ssm[d_]["fwd_desc"], order_adj[d_], t_rows, R)
        du_dir.append(du_d)
        ssm_grads.append((dbd, dcd, dlam))

    trr = _row_tile(L, Lc)
    nlt = L // trr
    rowR = pl.BlockSpec((trr, W), lambda i: (i, 0))

    def du_fn(i, du0, du1, dsv, dvec, u):
        lat = (i < nlt).astype(F32)
        return [du0 + du1 + lat * (dvec * dsv)], [lat * _colsum(dsv * u)]

    (du_all,), (d_ssm_d,) = _rowwise(
        "du_combine", R // trr, [du_dir[0], du_dir[1], dssm_all, ssm_d, p01],
        [rowR, rowR, rowR, pl.BlockSpec((1, W), lambda i: (0, 0)), pl.BlockSpec((None, trr, W), lambda i: (1, i, 0))],
        [jax.ShapeDtypeStruct((R, W), BF16)], [rowR], [W], du_fn)

    def dz_sum(i, dzv):
        return [], [_colsum(dzv.astype(F32))]

    _, (d_b_glu,) = _rowwise("db_glu", L // tr, [dz], [rowW], [], [], [W], dz_sum)

    dq_rot, dk_rot, dv_hd = _attn_bwd(q_rot, k_rot, v_hd, d_attn, QPK)
    dq_pre, d_qg = _qk_prep_bwd("q_prep_bwd", dq_rot, p27, 0, NQ, L, q_norm_g, cos_l, sin_l)
    dk_pre, d_kg = _qk_prep_bwd("k_prep_bwd", dk_rot, p01, 0, NKV, R, k_norm_g, cos_all, sin_all)
    dv_pre = _heads_merge("dv_merge", dv_hd)

    def lat_blocks(a):
        return jnp.pad(a, ((0, 0), (0, Lc), (0, 0)))

    dq_blocks = jnp.transpose(dq_pre.reshape(L, 2, INS), (1, 0, 2))
    dp = jnp.concatenate([
        jnp.concatenate([dk_pre, dv_pre], axis=1)[None], du_all[None],
        lat_blocks(dq_blocks), lat_blocks(dg_a), lat_blocks(dg_s)], axis=0)

    def store_list(accs, ins, outs, pids):
        for o, v_ in zip(outs, accs):
            o[...] = v_.astype(o.dtype)

    tmo = _tile(D, 512, 128)
    dwin = _matmul(
        "dw_in", (N_DEV, D // tmo), [h2, dp],
        [pl.BlockSpec((R, tmo), lambda j, m: (0, m)), pl.BlockSpec((None, R, INS), lambda j, m: (j, 0, 0))],
        [(0, 1, 0, TN)], [jax.ShapeDtypeStruct((N_DEV, D, INS), BF16)],
        [pl.BlockSpec((None, tmo, INS), lambda j, m: (j, m, 0))], store_list)[0]
    dh2 = _matmul(
        "d_h2", (R // tm, D // tnd, N_DEV), [dp, win],
        [pl.BlockSpec((None, tm, INS), lambda i, n, k: (k, i, 0)),
         pl.BlockSpec((None, tnd, INS), lambda i, n, k: (k, n, 0))],
        [(0, 1, 0, NT)], [jax.ShapeDtypeStruct((R, D), F32)], [pl.BlockSpec((tm, tnd), lambda i, n, k: (i, n))],
        store_list, acc_shapes=[(tm, tnd)], nk=N_DEV)[0]
    dxc1, (dsh2, dsc2, dmc3, dmc4, dgam2) = _norm_mod_bwd(
        "nm2_bwd", xc1, dh2, gam[1], grp(sc2, mc4), L, Lc, dres=dx2)

    df1, (dg1, dmc2) = _gate_bwd("gate1_bwd", dxc1, f1, gate1, 0.5, L, Lc)
    dh1, dwg1, dwu1, dwd1 = _ffn_bwd("ffn1b", df1, h1, a1, b1, s1, wg1, wu1, wd1)
    dxc0, (dsh1, dsc1, dmc0, dmc1, dgam1) = _norm_mod_bwd(
        "nm1_bwd", xc0, dh1, gam[0], grp(sc1, mc1), L, Lc, dres=dxc1)
    grad_x = dxc0[:L][None]

    dmod_lat = jnp.concatenate([dsh1, dsc1, dg1, dsh2, dsc2, dg2, dsh3, dsc3, dg3], axis=1)
    dmod_ctx = jnp.concatenate([dmc0, dmc1, dmc2, dmc3, dmc4, jnp.zeros((1, 4 * D), F32)], axis=1)
    dmod_pack = at_row(dmod_lat, 0, SUBLANES) + at_row(dmod_ctx, 1, SUBLANES)
    (dmod_g,) = _all_gather("ag_dmod", [dmod_pack])
    dmod_all = dmod_g.reshape(N_DEV * SUBLANES, 9 * D)
    dmod_cols = lax.dynamic_slice_in_dim(dmod_all, me * MODW, MODW, axis=1)
    g_wmod, dl_wmod, nm_wmod, nv_wmod, dsilu = _mod_bwd_adam(cs, dmod_cols, w_mod[0], m_w_mod[0], v_w_mod[0])
    sg_cc = jax.nn.sigmoid(c_ctx)
    d_c_ctx = dsilu[8] * (sg_cc * (1.0 + c_ctx * (1.0 - sg_cc)))
    g_bmod, dl_bmod, nm_bmod, nv_bmod = _bias_adam(dmod_all, b_mod, m_b_mod, v_b_mod)

    d_a_re, d_a_im, d_ldt, d_b_re, d_b_im, d_c_re, d_c_im = [], [], [], [], [], [], []
    for d_ in range(2):
        dbd, dcd, dlam = ssm_grads[d_]
        dbt_re = jnp.transpose(_block_diag_extract(dbd[:, :, :SW], E, P), (0, 2, 1))
        dbt_im = jnp.transpose(_block_diag_extract(dbd[:, :, SW:], E, P), (0, 2, 1))
        dl_re, dl_im = dlam[:, 0, :SW].reshape(G, P), dlam[:, 0, SW:].reshape(G, P)
        prim = (ssm_a_re[0, d_], ssm_a_im[0, d_], ssm_log_dt[0, d_], ssm_b_re[0, d_], ssm_b_im[0, d_])
        _, vjp = jax.vjp(_ssm_discretize, *prim)
        ga_re, ga_im, gl_dt, gb_re, gb_im = vjp((dl_re, dl_im, dbt_re, dbt_im))
        d_a_re.append(ga_re)
        d_a_im.append(ga_im)
        d_ldt.append(gl_dt)
        d_b_re.append(gb_re)
        d_b_im.append(gb_im)
        d_c_re.append(jnp.transpose(_block_diag_extract(dcd[:, :SW, :], P, E), (0, 2, 1)))
        d_c_im.append(-jnp.transpose(_block_diag_extract(dcd[:, SW:, :], P, E), (0, 2, 1)))

    dgam_all = jnp.concatenate([dgam1, dgam2, dgam3], axis=0)
    small_g = [d_c_ctx, d_qg, d_kg, jnp.stack(d_a_re), jnp.stack(d_a_im), jnp.stack(d_ldt), jnp.stack(d_b_re),
               jnp.stack(d_b_im), jnp.stack(d_c_re), jnp.stack(d_c_im), d_ssm_d, d_b_glu, dgam_all]
    small_w = [c_ctx, q_norm_g, k_norm_g, ssm_a_re, ssm_a_im, ssm_log_dt, ssm_b_re, ssm_b_im, ssm_c_re, ssm_c_im,
               ssm_d, b_glu, ng_full]
    small_m = [m_c_ctx, m_q_norm_g, m_k_norm_g, m_ssm_a_re, m_ssm_a_im, m_ssm_log_dt, m_ssm_b_re, m_ssm_b_im,
               m_ssm_c_re, m_ssm_c_im, m_ssm_d, m_b_glu, m_ng_full]
    small_v = [v_c_ctx, v_q_norm_g, v_k_norm_g, v_ssm_a_re, v_ssm_a_im, v_ssm_log_dt, v_ssm_b_re, v_ssm_b_im,
               v_ssm_c_re, v_ssm_c_im, v_ssm_d, v_b_glu, v_ng_full]
    small_shapes = [a.shape for a in small_w]
    n_rows = sum(-(-math.prod(s) // LANES) for s in small_shapes)
    n_rows = -(-n_rows // 256) * 256
    (small_parts,) = _all_gather("ag_small_grads", [_pack(small_g, n_rows)])
    small_out = _sum_adam("small_adam", small_parts, _pack(small_w, n_rows), _pack(small_m, n_rows),
                          _pack(small_v, n_rows))
    sm_g, sm_dl, sm_m, sm_v = [_unpack(o, small_shapes) for o in small_out]

    def my_norm_cols(a):
        return lax.dynamic_slice_in_dim(a, me * dn, dn, axis=1)[None]

    for lst in (sm_g, sm_dl, sm_m, sm_v):
        lst[-1] = my_norm_cols(lst[-1])

    grads = [dwg1, dwu1, dwd1, dwin, dwglu.reshape(N_DEV, W // N_DEV, W), dwbra.reshape(N_DEV, D // N_DEV, D), dwbrs,
             dwout.reshape(N_DEV, D // N_DEV, D), dwg2, dwu2, dwd2]
    landed = _exchange_blocks("rs_grads", grads)
    big_w = [w_ffn1_gate, w_ffn1_up, w_ffn1_down, w_in, w_glu, w_br_attn, w_br_ssm, w_out, w_ffn2_gate, w_ffn2_up,
             w_ffn2_down]
    big_m = [m_w_ffn1_gate, m_w_ffn1_up, m_w_ffn1_down, m_w_in, m_w_glu, m_w_br_attn, m_w_br_ssm, m_w_out,
             m_w_ffn2_gate, m_w_ffn2_up, m_w_ffn2_down]
    big_v = [v_w_ffn1_gate, v_w_ffn1_up, v_w_ffn1_down, v_w_in, v_w_glu, v_w_br_attn, v_w_br_ssm, v_w_out,
             v_w_ffn2_gate, v_w_ffn2_up, v_w_ffn2_down]
    big_names = ["ffn1_gate", "ffn1_up", "ffn1_down", "in", "glu", "br_attn", "br_ssm", "out", "ffn2_gate",
                 "ffn2_up", "ffn2_down"]
    big_out = [[o[None] for o in _sum_adam("adam_" + nm, p, w_[0], m_[0], v_[0])]
               for nm, p, w_, m_, v_ in zip(big_names, landed, big_w, big_m, big_v)]

    def leaf(kind):
        sm = (sm_g, sm_dl, sm_m, sm_v)[kind]
        mod = (g_wmod, dl_wmod, nm_wmod, nv_wmod)[kind][None]
        bmod = (g_bmod, dl_bmod, nm_bmod, nv_bmod)[kind]
        big = [b[kind] for b in big_out]
        (c_ctx_, qg_, kg_, a_re_, a_im_, ldt_, b_re_, b_im_, c_re_, c_im_, sd_, bglu_, ng_) = sm
        return [c_ctx_, mod, bmod, ng_, big[0], big[1], big[2], big[3], qg_, kg_, a_re_, a_im_, ldt_, b_re_, b_im_,
                c_re_, c_im_, sd_, big[4], bglu_, big[5], big[6], big[7], big[8], big[9], big[10]]

    return tuple([loss, grad_x] + leaf(0) + leaf(1) + leaf(2) + leaf(3))
```

```python
import math

import jax
import jax.numpy as jnp
from jax import lax
from jax.experimental import pallas as pl
from jax.experimental.pallas import tpu as pltpu

F32 = jnp.float32
BF16 = jnp.bfloat16

N_DEV = 8
N_CHIPS = 4
LANES = 128
SUBLANES = 8
PACKED_SUBLANES = 16
VMEM_LIMIT = 56 * 1024 * 1024
MM_TILE = 512
MM_TILE_NT = 256
ROW_TILE = 256
HEAD_ROW_TILE = 512
ADAM_BLOCK_BYTES = 4 * 1024 * 1024

NORM_EPS = 1e-6
GRID_W = 64
ROPE_THETA = 10000.0
SCAN_TAPS = SUBLANES
SLAB_GROUPS = 8

ADAM_LR = 0.001
ADAM_B1 = 0.9
ADAM_B2 = 0.999
ADAM_EPS = 1e-08
ADAM_WD = 0.01
ADAM_STEP = 10

NN = (((1,), (0,)), ((), ()))
NT = (((1,), (1,)), ((), ()))
TN = (((0,), (0,)), ((), ()))

MESH = pl.DeviceIdType.MESH
ANY = pl.BlockSpec(memory_space=pl.ANY)


def _tile(n, cap, align):
    best = None
    for t in range(align, min(n, cap) + 1, align):
        if n % t == 0:
            best = t
    return n if best is None else best


def _params(n_grid):
    return pltpu.CompilerParams(dimension_semantics=("arbitrary",) * n_grid, vmem_limit_bytes=VMEM_LIMIT)


def _sigmoid(x):
    return 1.0 / (1.0 + jnp.exp(-x))


GELU_K = math.sqrt(2.0 / math.pi)
GELU_C = 0.044715


def _gelu(x):
    return 0.5 * x * (1.0 + jnp.tanh(GELU_K * (x + GELU_C * x * x * x)))


def _gelu_grad(x):
    t = jnp.tanh(GELU_K * (x + GELU_C * x * x * x))
    return 0.5 * (1.0 + t) + 0.5 * x * (1.0 - t * t) * GELU_K * (1.0 + 3.0 * GELU_C * x * x)


def _adamw(w, g, m, v):
    m2 = ADAM_B1 * m + (1.0 - ADAM_B1) * g
    v2 = ADAM_B2 * v + (1.0 - ADAM_B2) * (g * g)
    m_hat = m2 / (1.0 - ADAM_B1 ** ADAM_STEP)
    v_hat = v2 / (1.0 - ADAM_B2 ** ADAM_STEP)
    delta = -ADAM_LR * (m_hat / (jnp.sqrt(v_hat) + ADAM_EPS) + ADAM_WD * w)
    return delta, m2, v2


def _position():
    return lax.axis_index("x"), lax.axis_index("y"), lax.axis_index("c")


class _Gather:
    def __init__(self, arrays):
        self.arrays = list(arrays)
        n = len(self.arrays)
        self.out_shapes = [jax.ShapeDtypeStruct((N_DEV,) + a.shape, a.dtype) for a in self.arrays]
        self.scratch = [pltpu.SemaphoreType.DMA((n, 7)), pltpu.SemaphoreType.DMA((n, 7)),
                        pltpu.SemaphoreType.DMA((n,))]

    def _plan(self, ins, outs, sems):
        send, recv, local = sems
        x, y, c = _position()
        me, sibling = (x, y, c), (x, y, 1 - c)
        chips = [(1 - x, y), (x, 1 - y), (1 - x, 1 - y)]

        def slot(a, p):
            return outs[a].at[4 * p[0] + 2 * p[1] + p[2]]

        def copy(a, k, block, to, src=None):
            dst = slot(a, block)
            return pltpu.make_async_remote_copy(
                src_ref=dst if src is None else src, dst_ref=dst,
                send_sem=send.at[a, k], recv_sem=recv.at[a, k], device_id=to, device_id_type=MESH)

        mine = [pltpu.make_async_copy(ins[a], slot(a, me), local.at[a]) for a in range(len(ins))]
        return me, sibling, chips, c, copy, mine

    def start(self, ins, outs, sems):
        me, sibling, chips, c, copy, mine = self._plan(ins, outs, sems)
        for cp in mine:
            cp.start()
        for a in range(len(ins)):
            copy(a, 0, me, sibling, src=ins[a]).start()
            for j, chip in enumerate(chips):
                copy(a, 1 + j, me, (*chip, c), src=ins[a]).start()

    def finish(self, ins, outs, sems):
        me, sibling, chips, c, copy, mine = self._plan(ins, outs, sems)
        n = len(ins)
        for j, chip in enumerate(chips):
            for a in range(n):
                copy(a, 1 + j, (*chip, c), me).wait_recv()
                copy(a, 4 + j, (*chip, c), sibling).start()
        for a in range(n):
            copy(a, 0, sibling, me).wait_recv()
        for j, chip in enumerate(chips):
            for a in range(n):
                copy(a, 4 + j, (*chip, 1 - c), me).wait_recv()
        for a in range(n):
            copy(a, 0, me, sibling, src=ins[a]).wait_send()
            for j, chip in enumerate(chips):
                copy(a, 1 + j, me, (*chip, c), src=ins[a]).wait_send()
                copy(a, 4 + j, (*chip, c), sibling).wait_send()
        for cp in mine:
            cp.wait()


class _SiblingSwap:
    def __init__(self, arrays):
        self.arrays = list(arrays)
        n = len(self.arrays)
        self.out_shapes = []
        for a in self.arrays:
            half = jax.ShapeDtypeStruct((N_CHIPS,) + a.shape[1:], a.dtype)
            self.out_shapes += [half, half]
        self.scratch = [pltpu.SemaphoreType.DMA((n, N_CHIPS)), pltpu.SemaphoreType.DMA((n, N_CHIPS)),
                        pltpu.SemaphoreType.DMA((n, N_CHIPS))]

    def _plan(self, ins, outs, sems):
        send, recv, local = sems
        x, y, c = _position()
        remote, keep = [], []
        for a in range(len(ins)):
            for j in range(N_CHIPS):
                remote.append(pltpu.make_async_remote_copy(
                    src_ref=ins[a].at[2 * j + 1 - c], dst_ref=outs[2 * a].at[j],
                    send_sem=send.at[a, j], recv_sem=recv.at[a, j], device_id=(x, y, 1 - c), device_id_type=MESH))
                keep.append(pltpu.make_async_copy(ins[a].at[2 * j + c], outs[2 * a + 1].at[j], local.at[a, j]))
        return remote, keep

    def start(self, ins, outs, sems):
        remote, keep = self._plan(ins, outs, sems)
        for cp in keep + remote:
            cp.start()

    def finish(self, ins, outs, sems):
        remote, keep = self._plan(ins, outs, sems)
        for cp in remote:
            cp.wait_recv()
        for cp in remote:
            cp.wait_send()
        for cp in keep:
            cp.wait()


class _ChipExchange:
    def __init__(self, arrays):
        self.arrays = list(arrays)
        n = len(self.arrays)
        self.out_shapes = [jax.ShapeDtypeStruct(a.shape, a.dtype) for a in self.arrays]
        self.scratch = [pltpu.SemaphoreType.DMA((n, N_CHIPS - 1)), pltpu.SemaphoreType.DMA((n, N_CHIPS - 1)),
                        pltpu.SemaphoreType.DMA((n,))]

    def _plan(self, ins, outs, sems):
        send, recv, local = sems
        x, y, c = _position()
        mine = 2 * x + y
        keep = [pltpu.make_async_copy(ins[a].at[mine], outs[a].at[mine], local.at[a]) for a in range(len(ins))]
        sends, recvs = [], []
        for r in range(1, N_CHIPS):
            px, py = x ^ (r >> 1), y ^ (r & 1)
            peer = 2 * px + py
            for a in range(len(ins)):
                sends.append(pltpu.make_async_remote_copy(
                    src_ref=ins[a].at[peer], dst_ref=outs[a].at[mine],
                    send_sem=send.at[a, r - 1], recv_sem=recv.at[a, r - 1],
                    device_id=(px, py, c), device_id_type=MESH))
                recvs.append(pltpu.make_async_remote_copy(
                    src_ref=ins[a].at[peer], dst_ref=outs[a].at[peer],
                    send_sem=send.at[a, r - 1], recv_sem=recv.at[a, r - 1],
                    device_id=(px, py, c), device_id_type=MESH))
        return keep, sends, recvs

    def start(self, ins, outs, sems):
        keep, sends, _ = self._plan(ins, outs, sems)
        for cp in keep + sends:
            cp.start()

    def finish(self, ins, outs, sems):
        keep, sends, recvs = self._plan(ins, outs, sems)
        for cp in recvs:
            cp.wait_recv()
        for cp in sends:
            cp.wait_send()
        for cp in keep:
            cp.wait()


def _host_call(body, *, name, grid, operands, in_specs, out_shape, out_specs, scratch_shapes=(), comm=None):
    grid = tuple(grid)
    n_in, n_out, n_scr = len(operands), len(out_shape), len(scratch_shapes)
    kwargs = dict(name=name, compiler_params=_params(len(grid)))
    if grid:
        kwargs["grid"] = grid
    if comm is None:
        res = pl.pallas_call(body, in_specs=list(in_specs), out_specs=list(out_specs), out_shape=list(out_shape),
                             scratch_shapes=list(scratch_shapes), **kwargs)(*operands)
        return list(res), []
    nc_in, nc_out = len(comm.arrays), len(comm.out_shapes)

    def hosted(*refs):
        bounds = [0, n_in, n_in + nc_in, n_in + nc_in + n_out, n_in + nc_in + n_out + nc_out,
                  n_in + nc_in + n_out + nc_out + n_scr, len(refs)]
        ins, cins, outs, couts, scr, sems = [refs[a:b] for a, b in zip(bounds[:-1], bounds[1:])]
        if not grid:
            comm.start(cins, couts, sems)
            body(*ins, *outs, *scr)
            comm.finish(cins, couts, sems)
            return
        first, last = None, None
        for ax, size in enumerate(grid):
            pid = pl.program_id(ax)
            f, l = pid == 0, pid == size - 1
            first = f if first is None else jnp.logical_and(first, f)
            last = l if last is None else jnp.logical_and(last, l)

        @pl.when(first)
        def _():
            comm.start(cins, couts, sems)

        body(*ins, *outs, *scr)

        @pl.when(last)
        def _():
            comm.finish(cins, couts, sems)

    res = pl.pallas_call(
        hosted, in_specs=list(in_specs) + [ANY] * nc_in, out_specs=list(out_specs) + [ANY] * nc_out,
        out_shape=list(out_shape) + list(comm.out_shapes), scratch_shapes=list(scratch_shapes) + list(comm.scratch),
        **kwargs)(*operands, *comm.arrays)
    return list(res[:n_out]), list(res[n_out:])


def _exchange_only(name, comm):
    def body():
        pass
    return _host_call(body, name=name, grid=(), operands=[], in_specs=[], out_shape=[], out_specs=[], comm=comm)[1]


def _matmul(name, grid, operands, in_specs, pairs, out_shapes, out_specs, epilogue, acc_shapes=(), nk=1,
            prologue=None, comm=None):
    n_in, n_out = len(operands), len(out_shapes)
    prologue = prologue or {}

    def body(*refs):
        ins, outs, accs = refs[:n_in], refs[n_in:n_in + n_out], refs[n_in + n_out:]
        pids = [pl.program_id(ax) for ax in range(len(grid))]

        def operand(i, blk=None):
            v = ins[i][...] if blk is None else ins[i][blk]
            if i in prologue:
                v = prologue[i](v)
            return v.astype(BF16)

        def products():
            vals = {}
            for pair in pairs:
                ai, bi, ci, dn = pair[:4]
                if len(pair) == 5:
                    p = None
                    for blk in range(pair[4]):
                        q = lax.dot_general(operand(ai, blk), operand(bi, blk), dn, preferred_element_type=F32)
                        p = q if p is None else p + q
                else:
                    p = lax.dot_general(operand(ai), operand(bi), dn, preferred_element_type=F32)
                vals[ci] = p if ci not in vals else vals[ci] + p
            return [vals[ci] for ci in sorted(vals)]

        if nk == 1:
            epilogue(products(), ins, outs, pids)
        else:
            k = pids[-1]
            prods = products()

            @pl.when(k == 0)
            def _():
                for acc, p in zip(accs, prods):
                    acc[...] = p

            @pl.when(k > 0)
            def _():
                for acc, p in zip(accs, prods):
                    acc[...] += p

            @pl.when(k == nk - 1)
            def _():
                epilogue([acc[...] for acc in accs], ins, outs, pids)

    return _host_call(
        body, name=name, grid=grid, operands=operands, in_specs=in_specs, out_shape=out_shapes, out_specs=out_specs,
        scratch_shapes=[pltpu.VMEM(s, F32) for s in acc_shapes] if nk > 1 else [], comm=comm)


def _rowwise(name, n_tiles, operands, in_specs, out_shapes, out_specs, red_widths, fn, comm=None):
    n_in, n_out, n_red = len(operands), len(out_shapes), len(red_widths)

    def body(*refs):
        ins, outs, reds = refs[:n_in], refs[n_in:n_in + n_out], refs[n_in + n_out:]
        i = pl.program_id(0)
        vals, sums = fn(i, *[r[...] for r in ins])
        for o, v in zip(outs, vals):
            o[...] = v.astype(o.dtype)
        if n_red:
            @pl.when(i == 0)
            def _():
                for r, s in zip(reds, sums):
                    r[...] = s

            @pl.when(i > 0)
            def _():
                for r, s in zip(reds, sums):
                    r[...] += s

    red_shapes = [jax.ShapeDtypeStruct((1, w), F32) for w in red_widths]
    red_specs = [pl.BlockSpec((1, w), lambda i: (0, 0)) for w in red_widths]
    res, cres = _host_call(
        body, name=name, grid=(n_tiles,), operands=operands, in_specs=in_specs,
        out_shape=list(out_shapes) + red_shapes, out_specs=list(out_specs) + red_specs, comm=comm)
    return res[:n_out], res[n_out:], cres


def _colsum(v):
    return jnp.sum(v, axis=0, keepdims=True)


def _store_all(accs, ins, outs, pids):
    for o, v in zip(outs, accs):
        o[...] = v.astype(o.dtype)


def _row_tile(rows_a, rows_b):
    return _tile(math.gcd(rows_a, rows_b) if rows_b else rows_a, ROW_TILE, SUBLANES)


def _norm_mod_fwd(name, xs, gamma, shift2, scale2, n_lat, n_ctx):
    rows, d = xs.shape
    tm = _row_tile(n_lat, n_ctx)
    nlt = n_lat // tm
    grp = pl.BlockSpec((None, 1, d), lambda i: (i // nlt, 0, 0))

    def fn(i, x, g, sh, sc):
        xh = x * lax.rsqrt(jnp.mean(x * x, axis=-1, keepdims=True) + NORM_EPS)
        return [(xh * g) * (1.0 + sc) + sh], []

    (h,), _, _ = _rowwise(
        name, rows // tm, [xs, gamma, shift2, scale2],
        [pl.BlockSpec((tm, d), lambda i: (i, 0)), pl.BlockSpec((1, d), lambda i: (0, 0)), grp, grp],
        [jax.ShapeDtypeStruct((rows, d), BF16)], [pl.BlockSpec((tm, d), lambda i: (i, 0))], [], fn)
    return h


def _norm_mod_bwd(name, xs, dh, gamma, scale2, n_lat, n_ctx, dres=None):
    rows, d = xs.shape
    tm = _row_tile(n_lat, n_ctx)
    nlt = n_lat // tm
    grp = pl.BlockSpec((None, 1, d), lambda i: (i // nlt, 0, 0))
    row = pl.BlockSpec((tm, d), lambda i: (i, 0))

    def fn(i, x, dy, g, sc, *res):
        rstd = lax.rsqrt(jnp.mean(x * x, axis=-1, keepdims=True) + NORM_EPS)
        xh = x * rstd
        dsh = _colsum(dy)
        dsc = _colsum(dy * (xh * g))
        dn = dy * (1.0 + sc)
        dgam = _colsum(dn * xh)
        dxh = dn * g
        dx = rstd * (dxh - xh * jnp.mean(dxh * xh, axis=-1, keepdims=True))
        if res:
            dx = dx + jnp.where(i < nlt, res[0], 0.0)
        lat = (i < nlt).astype(F32)
        return [dx], [dsh * lat, dsc * lat, dsh * (1.0 - lat), dsc * (1.0 - lat), dgam]

    operands = [xs, dh, gamma, scale2]
    specs = [row, row, pl.BlockSpec((1, d), lambda i: (0, 0)), grp]
    if dres is not None:
        operands.append(dres)
        specs.append(pl.BlockSpec((tm, d), lambda i: (jnp.minimum(i, nlt - 1), 0)))
    (dx,), sums, _ = _rowwise(name, rows // tm, operands, specs,
                              [jax.ShapeDtypeStruct((rows, d), F32)], [row], [d] * 5, fn)
    return dx, sums


def _gate_bwd(name, dx, f, gate2, coef, n_lat, n_ctx):
    rows, d = dx.shape
    tm = _row_tile(n_lat, n_ctx)
    nlt = n_lat // tm
    row = pl.BlockSpec((tm, d), lambda i: (i, 0))

    def fn(i, dxv, fv, gv):
        dg = _colsum(dxv * fv) * coef
        lat = (i < nlt).astype(F32)
        return [(coef * gv) * dxv], [dg * lat, dg * (1.0 - lat)]

    (df,), sums, _ = _rowwise(
        name, rows // tm, [dx, f, gate2],
        [row, row, pl.BlockSpec((None, 1, d), lambda i: (i // nlt, 0, 0))],
        [jax.ShapeDtypeStruct((rows, d), BF16)], [row], [d, d], fn)
    return df, sums


def _select_rows(i, tm, n_lat, vec2):
    rows = i * tm + lax.broadcasted_iota(jnp.int32, (tm, 1), 0)
    return jnp.where(rows < n_lat, vec2[0:1, :], vec2[1:2, :])


def _ffn_up(tag, h, wg, wu, comm=None):
    rows, d = h.shape
    nb, _, fs = wg.shape
    tm = _tile(rows, MM_TILE, LANES)
    blk = pl.BlockSpec((None, tm, fs), lambda j, i: (j, i, 0))
    wspec = pl.BlockSpec((None, d, fs), lambda j, i: (j, 0, 0))

    def epilogue(accs, ins, outs, pids):
        a, b = accs
        outs[0][...] = a.astype(BF16)
        outs[1][...] = b.astype(BF16)
        outs[2][...] = (a * _sigmoid(a) * b).astype(BF16)

    hid = jax.ShapeDtypeStruct((nb, rows, fs), BF16)
    (a, b, s), cres = _matmul(
        tag + "_up", (nb, rows // tm), [h, wg, wu],
        [pl.BlockSpec((tm, d), lambda j, i: (i, 0)), wspec, wspec],
        [(0, 1, 0, NN), (0, 2, 1, NN)], [hid, hid, hid], [blk, blk, blk], epilogue, comm=comm)
    return a, b, s, cres


def _ffn_down(tag, s, wd, xs, gate2, n_lat, comm=None):
    nb, rows, fs = s.shape
    d = wd.shape[-1]
    tm = _tile(rows, MM_TILE, LANES)
    tn = _tile(d, MM_TILE, LANES)

    def epilogue(accs, ins, outs, pids):
        f = accs[0]
        outs[0][...] = f
        outs[1][...] = ins[2][...] + 0.5 * _select_rows(pids[0], tm, n_lat, ins[3][...]) * f

    out = jax.ShapeDtypeStruct((rows, d), F32)
    ospec = pl.BlockSpec((tm, tn), lambda i, n: (i, n))
    (f, xo), cres = _matmul(
        tag + "_down", (rows // tm, d // tn), [s, wd, xs, gate2],
        [pl.BlockSpec((nb, tm, fs), lambda i, n: (0, i, 0)), pl.BlockSpec((nb, fs, tn), lambda i, n: (0, 0, n)),
         ospec, pl.BlockSpec((2, tn), lambda i, n: (0, n))],
        [(0, 1, 0, NN, nb)], [out, out], [ospec, ospec], epilogue, comm=comm)
    return f, xo, cres


def _ffn_ds(tag, df, wd, a, b):
    rows, d = df.shape
    nb, fs, _ = wd.shape
    tm = _tile(rows, MM_TILE, LANES)
    blk = pl.BlockSpec((None, tm, fs), lambda j, i: (j, i, 0))

    def epilogue(accs, ins, outs, pids):
        ds = accs[0]
        av = ins[2][...].astype(F32)
        bv = ins[3][...].astype(F32)
        sg = _sigmoid(av)
        outs[0][...] = (ds * bv * (sg * (1.0 + av * (1.0 - sg)))).astype(BF16)
        outs[1][...] = (ds * (av * sg)).astype(BF16)

    hid = jax.ShapeDtypeStruct((nb, rows, fs), BF16)
    (da, db), _ = _matmul(
        tag + "_ds", (nb, rows // tm), [df, wd, a, b],
        [pl.BlockSpec((tm, d), lambda j, i: (i, 0)), pl.BlockSpec((None, fs, d), lambda j, i: (j, 0, 0)), blk, blk],
        [(0, 1, 0, NT)], [hid, hid], [blk, blk], epilogue)
    return da, db


def _ffn_dwd(tag, s, df, comm=None):
    nb, rows, fs = s.shape
    d = df.shape[-1]
    tn = _tile(d, MM_TILE, LANES)
    (dwd,), cres = _matmul(
        tag + "_dwd", (nb, d // tn), [s, df],
        [pl.BlockSpec((None, rows, fs), lambda j, n: (j, 0, 0)), pl.BlockSpec((rows, tn), lambda j, n: (0, n))],
        [(0, 1, 0, TN)], [jax.ShapeDtypeStruct((nb, fs, d), BF16)],
        [pl.BlockSpec((None, fs, tn), lambda j, n: (j, 0, n))], _store_all, comm=comm)
    return dwd, cres


def _ffn_dwgu(tag, h, da, db, comm=None):
    rows, d = h.shape
    nb, _, fs = da.shape
    tmo = _tile(d, MM_TILE, LANES)
    full = pl.BlockSpec((None, rows, fs), lambda j, m: (j, 0, 0))
    wshape = jax.ShapeDtypeStruct((nb, d, fs), BF16)
    wblk = pl.BlockSpec((None, tmo, fs), lambda j, m: (j, m, 0))
    (dwg, dwu), cres = _matmul(
        tag + "_dwgu", (nb, d // tmo), [h, da, db],
        [pl.BlockSpec((rows, tmo), lambda j, m: (0, m)), full, full],
        [(0, 1, 0, TN), (0, 2, 1, TN)], [wshape, wshape], [wblk, wblk], _store_all, comm=comm)
    return dwg, dwu, cres


def _ffn_dh(tag, da, db, wg, wu, comm=None):
    nb, rows, fs = da.shape
    d = wg.shape[1]
    tm = _tile(rows, MM_TILE, LANES)
    tn = _tile(d, MM_TILE_NT, LANES)
    aspec = pl.BlockSpec((nb, tm, fs), lambda i, n: (0, i, 0))
    wspec = pl.BlockSpec((nb, tn, fs), lambda i, n: (0, n, 0))
    (dh,), cres = _matmul(
        tag + "_dh", (rows // tm, d // tn), [da, wg, db, wu], [aspec, wspec, aspec, wspec],
        [(0, 1, 0, NT, nb), (2, 3, 0, NT, nb)], [jax.ShapeDtypeStruct((rows, d), F32)],
        [pl.BlockSpec((tm, tn), lambda i, n: (i, n))], _store_all, comm=comm)
    return dh, cres


def _swap_halves(x):
    lane = lax.broadcasted_iota(jnp.int32, x.shape, 1)
    return jnp.where((lane % 64) < 32, pltpu.roll(x, 96, 1), pltpu.roll(x, 32, 1))


def _heads_spec(tq, hb, width, first_block):
    per_shard = width // (hb * LANES)

    def index(k, i):
        blk = first_block + k
        return blk // per_shard, i, blk % per_shard
    return pl.BlockSpec((None, tq, hb * LANES), index)


def _qk_prep(name, src, first_block, hb, n_heads, rows, g, cos_t, sin_t):
    tq = _tile(rows, HEAD_ROW_TILE, SUBLANES)
    tab = pl.BlockSpec((tq, LANES), lambda k, i: (i, 0))

    def body(x_ref, g_ref, c_ref, s_ref, o_ref):
        for h in range(hb):
            x = x_ref[:, h * LANES:(h + 1) * LANES]
            n = x * lax.rsqrt(jnp.mean(x * x, axis=-1, keepdims=True) + NORM_EPS) * g_ref[...]
            o_ref[h] = (n * c_ref[...] + _swap_halves(n) * s_ref[...]).astype(BF16)

    return pl.pallas_call(
        body, name=name, grid=(n_heads // hb, rows // tq),
        in_specs=[_heads_spec(tq, hb, src.shape[-1], first_block), pl.BlockSpec((1, LANES), lambda k, i: (0, 0)),
                  tab, tab],
        out_specs=pl.BlockSpec((hb, tq, LANES), lambda k, i: (k, i, 0)),
        out_shape=jax.ShapeDtypeStruct((n_heads, rows, LANES), BF16), compiler_params=_params(2),
    )(src, g, cos_t, sin_t)


def _qk_prep_bwd(name, dy, src, first_block, hb, n_heads, rows, g, cos_t, sin_t):
    tq = _tile(rows, HEAD_ROW_TILE, SUBLANES)
    tab = pl.BlockSpec((tq, LANES), lambda k, i: (i, 0))

    def body(dy_ref, x_ref, g_ref, c_ref, s_ref, dx_ref, dg_ref):
        g = g_ref[...]
        dg = None
        for h in range(hb):
            x = x_ref[:, h * LANES:(h + 1) * LANES]
            dyv = dy_ref[h]
            rstd = lax.rsqrt(jnp.mean(x * x, axis=-1, keepdims=True) + NORM_EPS)
            xh = x * rstd
            dn = dyv * c_ref[...] + _swap_halves(dyv * s_ref[...])
            dxh = dn * g
            dx = rstd * (dxh - xh * jnp.mean(dxh * xh, axis=-1, keepdims=True))
            dx_ref[:, h * LANES:(h + 1) * LANES] = dx.astype(BF16)
            part = _colsum(dn * xh)
            dg = part if dg is None else dg + part
        first = jnp.logical_and(pl.program_id(0) == 0, pl.program_id(1) == 0)

        @pl.when(first)
        def _():
            dg_ref[...] = dg

        @pl.when(jnp.logical_not(first))
        def _():
            dg_ref[...] += dg

    return pl.pallas_call(
        body, name=name, grid=(n_heads // hb, rows // tq),
        in_specs=[pl.BlockSpec((hb, tq, LANES), lambda k, i: (k, i, 0)),
                  _heads_spec(tq, hb, src.shape[-1], first_block),
                  pl.BlockSpec((1, LANES), lambda k, i: (0, 0)), tab, tab],
        out_specs=[pl.BlockSpec((tq, hb * LANES), lambda k, i: (i, k)),
                   pl.BlockSpec((1, LANES), lambda k, i: (0, 0))],
        out_shape=[jax.ShapeDtypeStruct((rows, n_heads * LANES), BF16), jax.ShapeDtypeStruct((1, LANES), F32)],
        compiler_params=_params(2),
    )(dy, src, g, cos_t, sin_t)


def _heads_cast(name, src, first_block, hb, n_heads, rows):
    tq = _tile(rows, HEAD_ROW_TILE, SUBLANES)

    def body(x_ref, o_ref):
        for h in range(hb):
            o_ref[h] = x_ref[:, h * LANES:(h + 1) * LANES].astype(BF16)

    return pl.pallas_call(
        body, name=name, grid=(n_heads // hb, rows // tq),
        in_specs=[_heads_spec(tq, hb, src.shape[-1], first_block)],
        out_specs=pl.BlockSpec((hb, tq, LANES), lambda k, i: (k, i, 0)),
        out_shape=jax.ShapeDtypeStruct((n_heads, rows, LANES), BF16), compiler_params=_params(2),
    )(src)


def _heads_merge(name, src):
    n_heads, rows, _ = src.shape
    tq = _tile(rows, HEAD_ROW_TILE, SUBLANES)

    def body(x_ref, o_ref):
        for h in range(n_heads):
            o_ref[:, h * LANES:(h + 1) * LANES] = x_ref[h].astype(BF16)

    return pl.pallas_call(
        body, name=name, grid=(rows // tq,),
        in_specs=[pl.BlockSpec((n_heads, tq, LANES), lambda i: (0, i, 0))],
        out_specs=pl.BlockSpec((tq, n_heads * LANES), lambda i: (i, 0)),
        out_shape=jax.ShapeDtypeStruct((rows, n_heads * LANES), BF16), compiler_params=_params(1),
    )(src)


def _attn_fwd(q, k, v, q_per_kv, comm=None):
    nq, l, _ = q.shape
    s_len = k.shape[1]
    tq = _tile(l, ROW_TILE, SUBLANES)
    scale = LANES ** -0.5
    kv = pl.BlockSpec((None, s_len, LANES), lambda h, i: (h // q_per_kv, 0, 0))

    def body(q_ref, k_ref, v_ref, o_ref):
        s = lax.dot_general(q_ref[...], k_ref[...], NT, preferred_element_type=F32) * scale
        p = jnp.exp(s - jnp.max(s, axis=-1, keepdims=True))
        den = jnp.sum(p, axis=-1, keepdims=True)
        o = jnp.dot(p.astype(BF16), v_ref[...], preferred_element_type=F32)
        o_ref[...] = (o / den).astype(BF16)

    (o,), cres = _host_call(
        body, name="attn_fwd", grid=(nq, l // tq), operands=[q, k, v],
        in_specs=[pl.BlockSpec((None, tq, LANES), lambda h, i: (h, i, 0)), kv, kv],
        out_shape=[jax.ShapeDtypeStruct((l, nq * LANES), BF16)],
        out_specs=[pl.BlockSpec((tq, LANES), lambda h, i: (i, h))], comm=comm)
    return o, cres


def _attn_bwd(q, k, v, do, q_per_kv, comm=None):
    nq, l, _ = q.shape
    nkv, s_len, _ = k.shape
    tq = _tile(l, ROW_TILE, SUBLANES)
    scale = LANES ** -0.5
    kv = pl.BlockSpec((None, s_len, LANES), lambda g, r, i: (g, 0, 0))
    qs = pl.BlockSpec((None, tq, LANES), lambda g, r, i: (g * q_per_kv + r, i, 0))

    def body(q_ref, k_ref, v_ref, do_ref, dq_ref, dk_ref, dv_ref):
        qv, kvv, vv, dov = q_ref[...], k_ref[...], v_ref[...], do_ref[...]
        st = lax.dot_general(kvv, qv, NT, preferred_element_type=F32) * scale
        e = jnp.exp(st - jnp.max(st, axis=0, keepdims=True))
        pt = e / jnp.sum(e, axis=0, keepdims=True)
        dpt = lax.dot_general(vv, dov, NT, preferred_element_type=F32)
        delta = jnp.sum(pt * dpt, axis=0, keepdims=True)
        dst = (pt * (dpt - delta) * scale).astype(BF16)
        ptb = pt.astype(BF16)
        dq_ref[...] = lax.dot_general(dst, kvv, TN, preferred_element_type=F32)
        dk_new = jnp.dot(dst, qv, preferred_element_type=F32)
        dv_new = jnp.dot(ptb, dov, preferred_element_type=F32)
        first = jnp.logical_and(pl.program_id(1) == 0, pl.program_id(2) == 0)

        @pl.when(first)
        def _():
            dk_ref[...] = dk_new
            dv_ref[...] = dv_new

        @pl.when(jnp.logical_not(first))
        def _():
            dk_ref[...] += dk_new
            dv_ref[...] += dv_new

    (dq, dk, dv), cres = _host_call(
        body, name="attn_bwd", grid=(nkv, q_per_kv, l // tq), operands=[q, k, v, do],
        in_specs=[qs, kv, kv, pl.BlockSpec((tq, LANES), lambda g, r, i: (i, g * q_per_kv + r))],
        out_specs=[qs, kv, kv],
        out_shape=[jax.ShapeDtypeStruct((nq, l, LANES), F32), jax.ShapeDtypeStruct((nkv, s_len, LANES), F32),
                   jax.ShapeDtypeStruct((nkv, s_len, LANES), F32)], comm=comm)
    return dq, dk, dv, cres


def _zoh(a_re, a_im, log_dt):
    dt = jnp.exp(log_dt)[:, None]
    mag = jnp.exp(a_re * dt)
    lb_re = mag * jnp.cos(a_im * dt)
    lb_im = mag * jnp.sin(a_im * dt)
    den = a_re * a_re + a_im * a_im
    coef_re = ((lb_re - 1.0) * a_re + lb_im * a_im) / den
    coef_im = (lb_im * a_re - (lb_re - 1.0) * a_im) / den
    return lb_re, lb_im, coef_re, coef_im


def _ssm_discretize(a_re, a_im, log_dt, b_re, b_im):
    lb_re, lb_im, cr, ci = _zoh(a_re, a_im, log_dt)
    bt_re = cr[..., None] * b_re - ci[..., None] * b_im
    bt_im = cr[..., None] * b_im + ci[..., None] * b_re
    return lb_re, lb_im, bt_re, bt_im


def _block_diag(m):
    g, a, b = m.shape
    m = m.reshape(g // SLAB_GROUPS, SLAB_GROUPS, a, b)
    eye = jnp.eye(SLAB_GROUPS, dtype=m.dtype)
    return jnp.einsum("sgab,gh->sgahb", m, eye).reshape(g // SLAB_GROUPS, SLAB_GROUPS * a, SLAB_GROUPS * b)


def _block_diag_extract(m, a, b):
    ns = m.shape[0]
    m = m.reshape(ns, SLAB_GROUPS, a, SLAB_GROUPS, b)
    eye = jnp.eye(SLAB_GROUPS, dtype=m.dtype)
    return jnp.einsum("sgahb,gh->sgab", m, eye).reshape(ns * SLAB_GROUPS, a, b)


def _tap_weights(base_re, base_im, pw_re, pw_im):
    parts_re, parts_im = [], []
    for tau in range(SCAN_TAPS):
        pr, pi = pw_re[tau][:, None, :], pw_im[tau][:, None, :]
        parts_re.append(pr * base_re - pi * base_im)
        parts_im.append(pr * base_im + pi * base_re)
    return jnp.concatenate([jnp.concatenate(parts_re, axis=1), jnp.concatenate(parts_im, axis=1)], axis=-1)


def _carry_table(pw_re, pw_im, descending):
    order = [SCAN_TAPS - r for r in range(SCAN_TAPS)] if descending else [r + 1 for r in range(SCAN_TAPS)]
    re = jnp.stack([pw_re[k] for k in order], axis=1)
    im = jnp.stack([pw_im[k] for k in order], axis=1)
    return jnp.concatenate([re, im], axis=-1)


def _scan_chunk(x, w_ref, tab_ref, s_ref, carry_ref, descending, t_rows, sw):
    row8 = lax.broadcasted_iota(jnp.int32, x.shape, 0) % SCAN_TAPS
    pieces = [x.astype(BF16)]
    for tau in range(1, SCAN_TAPS):
        if descending:
            sh = jnp.where(row8 <= SCAN_TAPS - 1 - tau, pltpu.roll(x, t_rows - tau, 0), 0.0)
        else:
            sh = jnp.where(row8 >= tau, pltpu.roll(x, tau, 0), 0.0)
        pieces.append(sh.astype(BF16))
    xa = jnp.concatenate(pieces, axis=1)
    s_ref[...] = jnp.dot(xa, w_ref[...], preferred_element_type=F32)
    tab = tab_ref[...]
    t_re, t_im = tab[:, :sw], tab[:, sw:]
    nb = t_rows // SCAN_TAPS
    edge = 0 if descending else SCAN_TAPS - 1

    def step(b, carry):
        h_re, h_im = carry
        r0 = pl.multiple_of(((nb - 1 - b) if descending else b) * SCAN_TAPS, SCAN_TAPS)
        x_re = s_ref[pl.ds(r0, SCAN_TAPS), :sw] + t_re * h_re - t_im * h_im
        x_im = s_ref[pl.ds(r0, SCAN_TAPS), sw:] + t_re * h_im + t_im * h_re
        s_ref[pl.ds(r0, SCAN_TAPS), :sw] = x_re
        s_ref[pl.ds(r0, SCAN_TAPS), sw:] = x_im
        return x_re[edge:edge + 1, :], x_im[edge:edge + 1, :]

    h_re, h_im = lax.fori_loop(0, nb, step, (carry_ref[0:1, :sw], carry_ref[0:1, sw:]))
    carry_ref[0:1, :sw] = h_re
    carry_ref[0:1, sw:] = h_im


def _ssm_fwd(name, u_src, u_shard, waug, tab, cd, descending, chunk_of, t_rows, rows, comm=None):
    ns, kdim, sw2 = waug.shape
    sw = sw2 // 2
    width = ns * LANES
    nchunks = rows // t_rows

    def body(u_ref, w_ref, tab_ref, cd_ref, y_ref, h_ref, s_ref, carry_ref):
        @pl.when(pl.program_id(1) == 0)
        def _():
            carry_ref[...] = jnp.zeros_like(carry_ref)

        _scan_chunk(u_ref[...], w_ref, tab_ref, s_ref, carry_ref, descending, t_rows, sw)
        hb = s_ref[...].astype(BF16)
        h_ref[...] = hb
        y_ref[...] = jnp.dot(hb, cd_ref[...], preferred_element_type=F32)

    (y, h), cres = _host_call(
        body, name=name, grid=(ns, nchunks), operands=[u_src, waug, tab, cd],
        in_specs=[pl.BlockSpec((None, t_rows, LANES), lambda s, i: (u_shard, chunk_of(i), s)),
                  pl.BlockSpec((None, kdim, sw2), lambda s, i: (s, 0, 0)),
                  pl.BlockSpec((None, SCAN_TAPS, sw2), lambda s, i: (s, 0, 0)),
                  pl.BlockSpec((None, sw2, LANES), lambda s, i: (s, 0, 0))],
        out_specs=[pl.BlockSpec((t_rows, LANES), lambda s, i: (chunk_of(i), s)),
                   pl.BlockSpec((None, t_rows, sw2), lambda s, i: (s, chunk_of(i), 0))],
        out_shape=[jax.ShapeDtypeStruct((rows, width), F32), jax.ShapeDtypeStruct((ns, rows, sw2), BF16)],
        scratch_shapes=[pltpu.VMEM((t_rows, sw2), F32), pltpu.VMEM((SUBLANES, sw2), F32)], comm=comm)
    return y, h, cres


def _ssm_bwd(name, dy, u_src, u_shard, states, caug, tab, bdt, descending, chunk_of, t_rows, rows, comm=None):
    ns, kdim, sw2 = caug.shape
    sw = sw2 // 2
    width = ns * LANES
    nchunks = rows // t_rows

    def body(dy_ref, u_ref, h_ref, w_ref, tab_ref, bdt_ref, du_ref, dbd_ref, dcd_ref, dlam_ref,
             s_ref, carry_ref, gsave_ref):
        first = pl.program_id(1) == 0

        @pl.when(first)
        def _():
            carry_ref[...] = jnp.zeros_like(carry_ref)
            gsave_ref[...] = jnp.zeros_like(gsave_ref)

        dyv = dy_ref[...]
        _scan_chunk(dyv, w_ref, tab_ref, s_ref, carry_ref, descending, t_rows, sw)
        g = s_ref[...]
        gb = g.astype(BF16)
        du_ref[...] = jnp.dot(gb, bdt_ref[...], preferred_element_type=F32)
        dbd = lax.dot_general(u_ref[...].astype(BF16), gb, TN, preferred_element_type=F32)
        hb = h_ref[...]
        dcd = lax.dot_general(hb, dyv.astype(BF16), TN, preferred_element_type=F32)
        hf = hb.astype(F32)
        rowid = lax.broadcasted_iota(jnp.int32, hf.shape, 0)
        if descending:
            hp = jnp.where(rowid == 0, 0.0, pltpu.roll(hf, 1, 0))
            h_edge, g_edge = hf[t_rows - 1:t_rows, :], g[0:1, :]
        else:
            hp = jnp.where(rowid == t_rows - 1, 0.0, pltpu.roll(hf, t_rows - 1, 0))
            h_edge, g_edge = hf[0:1, :], g[t_rows - 1:t_rows, :]
        g_re, g_im, hp_re, hp_im = g[:, :sw], g[:, sw:], hp[:, :sw], hp[:, sw:]
        gs = gsave_ref[0:1, :]
        gs_re, gs_im, he_re, he_im = gs[:, :sw], gs[:, sw:], h_edge[:, :sw], h_edge[:, sw:]
        dl_re = _colsum(g_re * hp_re + g_im * hp_im) + gs_re * he_re + gs_im * he_im
        dl_im = _colsum(g_im * hp_re - g_re * hp_im) + gs_im * he_re - gs_re * he_im
        gsave_ref[0:1, :] = g_edge

        @pl.when(first)
        def _():
            dbd_ref[...] = dbd
            dcd_ref[...] = dcd
            dlam_ref[:, :sw] = dl_re
            dlam_ref[:, sw:] = dl_im

        @pl.when(jnp.logical_not(first))
        def _():
            dbd_ref[...] += dbd
            dcd_ref[...] += dcd
            dlam_ref[:, :sw] += dl_re
            dlam_ref[:, sw:] += dl_im

    (du, dbd, dcd, dlam), cres = _host_call(
        body, name=name, grid=(ns, nchunks), operands=[dy, u_src, states, caug, tab, bdt],
        in_specs=[pl.BlockSpec((t_rows, LANES), lambda s, i: (chunk_of(i), s)),
                  pl.BlockSpec((None, t_rows, LANES), lambda s, i: (u_shard, chunk_of(i), s)),
                  pl.BlockSpec((None, t_rows, sw2), lambda s, i: (s, chunk_of(i), 0)),
                  pl.BlockSpec((None, kdim, sw2), lambda s, i: (s, 0, 0)),
                  pl.BlockSpec((None, SCAN_TAPS, sw2), lambda s, i: (s, 0, 0)),
                  pl.BlockSpec((None, sw2, LANES), lambda s, i: (s, 0, 0))],
        out_specs=[pl.BlockSpec((t_rows, LANES), lambda s, i: (chunk_of(i), s)),
                   pl.BlockSpec((None, LANES, sw2), lambda s, i: (s, 0, 0)),
                   pl.BlockSpec((None, sw2, LANES), lambda s, i: (s, 0, 0)),
                   pl.BlockSpec((None, 1, sw2), lambda s, i: (s, 0, 0))],
        out_shape=[jax.ShapeDtypeStruct((rows, width), F32), jax.ShapeDtypeStruct((ns, LANES, sw2), F32),
                   jax.ShapeDtypeStruct((ns, sw2, LANES), F32), jax.ShapeDtypeStruct((ns, 1, sw2), F32)],
        scratch_shapes=[pltpu.VMEM((t_rows, sw2), F32), pltpu.VMEM((SUBLANES, sw2), F32),
                        pltpu.VMEM((SUBLANES, sw2), F32)], comm=comm)
    return du, dbd, dcd, dlam, cres


def _mod_fwd(cs, w_mod, b_cols):
    d, width = w_mod.shape
    tn = _tile(width, 768, LANES)

    def epilogue(accs, ins, outs, pids):
        outs[0][...] = accs[0] + ins[2][...]

    return _matmul(
        "mod_fwd", (width // tn,), [cs, w_mod, b_cols],
        [pl.BlockSpec((16, d), lambda n: (0, 0)), pl.BlockSpec((d, tn), lambda n: (0, n)),
         pl.BlockSpec((1, tn), lambda n: (0, n))],
        [(0, 1, 0, NN)], [jax.ShapeDtypeStruct((16, width), F32)], [pl.BlockSpec((16, tn), lambda n: (0, n))],
        epilogue, prologue={0: lambda v: v * _sigmoid(v)})[0][0]


def _mod_bwd_adam(cs, dmod_cols, w, m, v, comm=None):
    d, width = w.shape
    tn = _tile(width, LANES, LANES)
    col = pl.BlockSpec((d, tn), lambda n: (0, n))

    def body(cs_ref, dm_ref, w_ref, m_ref, v_ref, g_ref, dl_ref, nm_ref, nv_ref, ds_ref):
        n = pl.program_id(0)
        lat = dm_ref[pl.ds(0, N_DEV, stride=SUBLANES), :]
        ctx = jnp.sum(dm_ref[pl.ds(1, N_DEV, stride=SUBLANES), :], axis=0, keepdims=True)
        row = lax.broadcasted_iota(jnp.int32, lat.shape, 0)
        dm = jnp.concatenate([lat, jnp.where(row == 0, ctx, 0.0)], axis=0).astype(BF16)
        c = cs_ref[...]
        sc = (c * _sigmoid(c)).astype(BF16)
        wv = w_ref[...]
        g = lax.dot_general(sc, dm, TN, preferred_element_type=F32)
        delta, m2, v2 = _adamw(wv, g, m_ref[...], v_ref[...])
        g_ref[...] = g
        dl_ref[...] = delta
        nm_ref[...] = m2
        nv_ref[...] = v2
        part = lax.dot_general(dm, wv.astype(BF16), NT, preferred_element_type=F32)

        @pl.when(n == 0)
        def _():
            ds_ref[...] = part

        @pl.when(n > 0)
        def _():
            ds_ref[...] += part

    shard = jax.ShapeDtypeStruct((d, width), F32)
    return _host_call(
        body, name="mod_bwd_adam", grid=(width // tn,), operands=[cs, dmod_cols, w, m, v],
        in_specs=[pl.BlockSpec((16, d), lambda n: (0, 0)), pl.BlockSpec((N_DEV * SUBLANES, tn), lambda n: (0, n)),
                  col, col, col],
        out_specs=[col, col, col, col, pl.BlockSpec((16, d), lambda n: (0, 0))],
        out_shape=[shard, shard, shard, shard, jax.ShapeDtypeStruct((16, d), F32)], comm=comm)


def _pair_sum(name, own, got):
    nblk, rows, cols = own.shape
    tr = _tile(rows, max(PACKED_SUBLANES, ADAM_BLOCK_BYTES // (cols * 6 * nblk)), PACKED_SUBLANES)
    blk = pl.BlockSpec((nblk, tr, cols), lambda i: (0, i, 0))

    def body(a_ref, b_ref, o_ref):
        o_ref[...] = (a_ref[...].astype(F32) + b_ref[...].astype(F32)).astype(BF16)

    return pl.pallas_call(
        body, name=name, grid=(rows // tr,), in_specs=[blk, blk], out_specs=blk,
        out_shape=jax.ShapeDtypeStruct(own.shape, BF16), compiler_params=_params(1))(own, got)


def _sum_adam(name, parts, w, m, v):
    rows, cols = w.shape
    n_parts = parts.shape[0]
    align = PACKED_SUBLANES if parts.dtype == BF16 else SUBLANES
    tr = _tile(rows, max(align, ADAM_BLOCK_BYTES // (cols * 44)), align)
    blk = pl.BlockSpec((tr, cols), lambda i: (i, 0))

    def body(p_ref, w_ref, m_ref, v_ref, g_ref, dl_ref, nm_ref, nv_ref):
        g = p_ref[0].astype(F32)
        for s in range(1, n_parts):
            g = g + p_ref[s].astype(F32)
        delta, m2, v2 = _adamw(w_ref[...], g, m_ref[...], v_ref[...])
        g_ref[...] = g
        dl_ref[...] = delta
        nm_ref[...] = m2
        nv_ref[...] = v2

    out = jax.ShapeDtypeStruct((rows, cols), F32)
    return pl.pallas_call(
        body, name=name, grid=(rows // tr,),
        in_specs=[pl.BlockSpec((n_parts, tr, cols), lambda i: (0, i, 0)), blk, blk, blk],
        out_specs=[blk, blk, blk, blk], out_shape=[out, out, out, out], compiler_params=_params(1),
    )(parts, w, m, v)


def _bias_adam(dmod_all, w, m, v):
    width = w.shape[-1]
    tn = _tile(width, 2048, LANES)
    blk = pl.BlockSpec((1, tn), lambda n: (0, n))

    def body(p_ref, w_ref, m_ref, v_ref, g_ref, dl_ref, nm_ref, nv_ref):
        g = jnp.sum(p_ref[...], axis=0, keepdims=True)
        delta, m2, v2 = _adamw(w_ref[...], g, m_ref[...], v_ref[...])
        g_ref[...] = g
        dl_ref[...] = delta
        nm_ref[...] = m2
        nv_ref[...] = v2

    out = jax.ShapeDtypeStruct((1, width), F32)
    return pl.pallas_call(
        body, name="bias_adam", grid=(width // tn,),
        in_specs=[pl.BlockSpec((dmod_all.shape[0], tn), lambda n: (0, n)), blk, blk, blk],
        out_specs=[blk, blk, blk, blk], out_shape=[out, out, out, out], compiler_params=_params(1),
    )(dmod_all, w, m, v)


def _pack(arrays, total_rows):
    flat = []
    for a in arrays:
        a = a.reshape(-1).astype(F32)
        flat.append(jnp.pad(a, (0, (-a.shape[0]) % LANES)))
    flat = jnp.concatenate(flat).reshape(-1, LANES)
    return jnp.pad(flat, ((0, total_rows - flat.shape[0]), (0, 0)))


def _unpack(packed, shapes):
    out, row = [], 0
    for shp in shapes:
        size = math.prod(shp)
        nrows = -(-size // LANES)
        out.append(packed[row:row + nrows].reshape(-1)[:size].reshape(shp))
        row += nrows
    return out


def kernel(x, c, ctx, c_ctx, w_mod, b_mod, norm_g, w_ffn1_gate, w_ffn1_up, w_ffn1_down, w_in, q_norm_g, k_norm_g, ssm_a_re, ssm_a_im, ssm_log_dt, ssm_b_re, ssm_b_im, ssm_c_re, ssm_c_im, ssm_d, w_glu, b_glu, w_br_attn, w_br_ssm, w_out, w_ffn2_gate, w_ffn2_up, w_ffn2_down, loss_target, m_c_ctx, m_w_mod, m_b_mod, m_norm_g, m_w_ffn1_gate, m_w_ffn1_up, m_w_ffn1_down, m_w_in, m_q_norm_g, m_k_norm_g, m_ssm_a_re, m_ssm_a_im, m_ssm_log_dt, m_ssm_b_re, m_ssm_b_im, m_ssm_c_re, m_ssm_c_im, m_ssm_d, m_w_glu, m_b_glu, m_w_br_attn, m_w_br_ssm, m_w_out, m_w_ffn2_gate, m_w_ffn2_up, m_w_ffn2_down, v_c_ctx, v_w_mod, v_b_mod, v_norm_g, v_w_ffn1_gate, v_w_ffn1_up, v_w_ffn1_down, v_w_in, v_q_norm_g, v_k_norm_g, v_ssm_a_re, v_ssm_a_im, v_ssm_log_dt, v_ssm_b_re, v_ssm_b_im, v_ssm_c_re, v_ssm_c_im, v_ssm_d, v_w_glu, v_b_glu, v_w_br_attn, v_w_br_ssm, v_w_out, v_w_ffn2_gate, v_w_ffn2_up, v_w_ffn2_down):
    _, L, D = x.shape
    Lc = ctx.shape[1]
    R = L + Lc
    MODW = w_mod.shape[-1]
    INS = w_in.shape[-1]
    KVW = INS // 2
    NQ = D // LANES
    NKV = KVW // LANES
    QPK = NQ // NKV
    HBQ = INS // LANES
    G, P, E = ssm_b_re.shape[2:]
    W = G * E
    SW = SLAB_GROUPS * P
    assert E * SLAB_GROUPS == LANES and W == INS and NQ * LANES == D and Lc <= L
    me = 4 * lax.axis_index("x") + 2 * lax.axis_index("y") + lax.axis_index("c")

    x2, ctx2, tgt = x[0], ctx[0], loss_target[0]
    xc0 = jnp.concatenate([x2, ctx2], axis=0)

    def bf(w):
        return w[0].astype(BF16)

    def widen(a):
        return jnp.pad(a[0], ((0, 0), (0, D - a.shape[-1])))

    def at_row(a, r, total):
        return jnp.pad(a, ((r, total - r - a.shape[0]), (0, 0)))

    pack_in = (at_row(c, 0, 16) + at_row(widen(norm_g), 1, 16) + at_row(widen(m_norm_g), 4, 16)
               + at_row(widen(v_norm_g), 7, 16))
    (g_in,) = _exchange_only("ag_inputs", _Gather([pack_in]))
    c_all = g_in[:, 0, :]
    dn = D // N_DEV

    def full_norm(k):
        return jnp.transpose(g_in[:, k:k + 3, :dn], (1, 0, 2)).reshape(3, D)

    ng_full, m_ng_full, v_ng_full = full_norm(1), full_norm(4), full_norm(7)
    cs = at_row(c_all, 0, 16) + at_row(c_ctx[None, :], 8, 16)

    b_cols = lax.dynamic_slice_in_dim(b_mod, me * MODW, MODW, axis=1)
    mod_blk = _mod_fwd(cs, w_mod[0], b_cols)
    (mod_g,) = _exchange_only("ag_mod", _Gather([mod_blk]))
    mod_lat = lax.dynamic_index_in_dim(mod_g, me, axis=1, keepdims=False).reshape(-1)
    mod_ctx = mod_g[:, 8, :].reshape(-1)
    sh1, sc1, g1, sh2, sc2, g2, sh3, sc3, g3 = [mod_lat[k * D:(k + 1) * D] for k in range(9)]
    mc0, mc1, mc2, mc3, mc4 = [mod_ctx[k * D:(k + 1) * D] for k in range(5)]

    def grp(a, b):
        return jnp.stack([a, b])[:, None, :]

    gam = [ng_full[k][None, :] for k in range(3)]

    wg1, wu1 = _exchange_only("ag_ffn1_gate_up", _Gather([bf(w_ffn1_gate), bf(w_ffn1_up)]))
    gate1 = grp(g1, mc2)
    h1 = _norm_mod_fwd("nm1_fwd", xc0, gam[0], grp(sh1, mc0), grp(sc1, mc1), L, Lc)
    a1, b1, s1, (wd1,) = _ffn_up("ffn1", h1, wg1, wu1, comm=_Gather([bf(w_ffn1_down)]))
    f1, xc1, (win,) = _ffn_down("ffn1", s1, wd1, xc0, gate1[:, 0, :], L, comm=_Gather([bf(w_in)]))

    h2 = _norm_mod_fwd("nm2_fwd", xc1, gam[1], grp(sh2, mc3), grp(sc2, mc4), L, Lc)
    tm = _tile(R, MM_TILE, LANES)
    tml = _tile(L, MM_TILE, LANES)

    (p01,), _ = _matmul(
        "in_proj_kvu", (2, R // tm), [h2, win],
        [pl.BlockSpec((tm, D), lambda j, i: (i, 0)), pl.BlockSpec((None, D, INS), lambda j, i: (j, 0, 0))],
        [(0, 1, 0, NN)], [jax.ShapeDtypeStruct((2, R, INS), F32)],
        [pl.BlockSpec((None, tm, INS), lambda j, i: (j, i, 0))], _store_all)
    (p27,), (wglu, wbra) = _matmul(
        "in_proj_qg", (6, L // tml), [h2, win],
        [pl.BlockSpec((tml, D), lambda j, i: (i, 0)), pl.BlockSpec((None, D, INS), lambda j, i: (j + 2, 0, 0))],
        [(0, 1, 0, NN)], [jax.ShapeDtypeStruct((6, L, INS), F32)],
        [pl.BlockSpec((None, tml, INS), lambda j, i: (j, i, 0))], _store_all,
        comm=_Gather([bf(w_glu), bf(w_br_attn)]))
    wglu2 = wglu.reshape(W, W)
    wbra2 = wbra.reshape(D, D)

    half = LANES // 4
    inv_freq = ROPE_THETA ** (-jnp.arange(half, dtype=F32) / half)
    pos = jnp.arange(L)
    ang_r = (pos // GRID_W).astype(F32)[:, None] * inv_freq
    ang_c = (pos % GRID_W).astype(F32)[:, None] * inv_freq
    cos_l = jnp.concatenate([jnp.cos(ang_r)] * 2 + [jnp.cos(ang_c)] * 2, axis=1)
    sin_l = jnp.concatenate([-jnp.sin(ang_r), jnp.sin(ang_r), -jnp.sin(ang_c), jnp.sin(ang_c)], axis=1)
    cos_all = jnp.concatenate([cos_l, jnp.ones((Lc, LANES), F32)], axis=0)
    sin_all = jnp.concatenate([sin_l, jnp.zeros((Lc, LANES), F32)], axis=0)

    q_rot = _qk_prep("q_prep", p27, 0, HBQ, NQ, L, q_norm_g, cos_l, sin_l)
    k_rot = _qk_prep("k_prep", p01, 0, NKV, NKV, R, k_norm_g, cos_all, sin_all)
    v_hd = _heads_cast("v_heads", p01, 1, NKV, NKV, R)
    attn, (wbrs, wout, wg2) = _attn_fwd(
        q_rot, k_rot, v_hd, QPK, comm=_Gather([bf(w_br_ssm), bf(w_out), bf(w_ffn2_gate)]))
    wout2 = wout.reshape(D, D)

    t_rows = _tile(math.gcd(L, Lc), ROW_TILE, SUBLANES)
    nl, ncx = L // t_rows, Lc // t_rows
    nch = nl + ncx
    ssm = []
    for d_ in range(2):
        lb_re, lb_im, bt_re, bt_im = _ssm_discretize(
            ssm_a_re[0, d_], ssm_a_im[0, d_], ssm_log_dt[0, d_], ssm_b_re[0, d_], ssm_b_im[0, d_])
        ns = G // SLAB_GROUPS
        lam_re, lam_im = lb_re.reshape(ns, SW), lb_im.reshape(ns, SW)
        pw_re, pw_im = [jnp.ones_like(lam_re)], [jnp.zeros_like(lam_im)]
        for _ in range(SCAN_TAPS):
            pw_re, pw_im = (pw_re + [pw_re[-1] * lam_re - pw_im[-1] * lam_im],
                            pw_im + [pw_re[-1] * lam_im + pw_im[-1] * lam_re])
        cj_im = [-p for p in pw_im]
        bd_re = _block_diag(jnp.transpose(bt_re, (0, 2, 1)))
        bd_im = _block_diag(jnp.transpose(bt_im, (0, 2, 1)))
        ct_re = _block_diag(ssm_c_re[0, d_])
        ct_im = _block_diag(-ssm_c_im[0, d_])
        fwd_desc = d_ == 1
        ssm.append(dict(
            waug=_tap_weights(bd_re, bd_im, pw_re, pw_im).astype(BF16),
            tab=_carry_table(pw_re, pw_im, fwd_desc),
            cd=jnp.concatenate([jnp.transpose(ct_re, (0, 2, 1)), jnp.transpose(ct_im, (0, 2, 1))], axis=1).astype(BF16),
            caug=_tap_weights(ct_re, ct_im, pw_re, cj_im).astype(BF16),
            tabc=_carry_table(pw_re, cj_im, not fwd_desc),
            bdt=jnp.concatenate([jnp.transpose(bd_re, (0, 2, 1)), jnp.transpose(bd_im, (0, 2, 1))], axis=1).astype(BF16),
            fwd_desc=fwd_desc))
    order = [lambda i: (i + nl) % nch, lambda i: nch - 1 - i]
    order_adj = [lambda i: (nch - 1 - i + nl) % nch, lambda i: i]
    y0, st0, (wu2,) = _ssm_fwd("ssm_fwd0", p01, 1, ssm[0]["waug"], ssm[0]["tab"], ssm[0]["cd"],
                               ssm[0]["fwd_desc"], order[0], t_rows, R, comm=_Gather([bf(w_ffn2_up)]))
    y1, st1, (wd2,) = _ssm_fwd("ssm_fwd1", p01, 1, ssm[1]["waug"], ssm[1]["tab"], ssm[1]["cd"],
                               ssm[1]["fwd_desc"], order[1], t_rows, R, comm=_Gather([bf(w_ffn2_down)]))
    states = [st0, st1]

    tr = _row_tile(L, 0)
    rowW = pl.BlockSpec((tr, W), lambda i: (i, 0))
    vecW = pl.BlockSpec((1, W), lambda i: (0, 0))
    u_lat = pl.BlockSpec((None, tr, W), lambda i: (1, i, 0))

    def ssm_post(i, u, ya, yb, dvec):
        sv = dvec * u + ya + yb
        return [sv, _gelu(sv)], []

    (ssm_out, yg), _, _ = _rowwise(
        "ssm_post", L // tr, [p01, y0, y1, ssm_d], [u_lat, rowW, rowW, vecW],
        [jax.ShapeDtypeStruct((L, W), F32), jax.ShapeDtypeStruct((L, W), BF16)], [rowW, rowW], [], ssm_post)

    tnw = _tile(W, MM_TILE, LANES)

    def glu_epilogue(accs, ins, outs, pids):
        z = accs[0] + ins[3][...]
        outs[0][...] = z
        outs[1][...] = (_gelu(ins[2][...]) * _sigmoid(z)).astype(BF16)

    (z_glu, y2), _ = _matmul(
        "glu", (L // tml, W // tnw), [yg, wglu2, ssm_out, b_glu],
        [pl.BlockSpec((tml, W), lambda i, n: (i, 0)), pl.BlockSpec((W, tnw), lambda i, n: (0, n)),
         pl.BlockSpec((tml, tnw), lambda i, n: (i, n)), pl.BlockSpec((1, tnw), lambda i, n: (0, n))],
        [(0, 1, 0, NN)], [jax.ShapeDtypeStruct((L, W), F32), jax.ShapeDtypeStruct((L, W), BF16)],
        [pl.BlockSpec((tml, tnw), lambda i, n: (i, n))] * 2, glu_epilogue)

    tnd = _tile(D, MM_TILE, LANES)
    out_ld = pl.BlockSpec((tml, tnd), lambda i, n: (i, n))
    (br_a,), _ = _matmul(
        "br_attn", (L // tml, D // tnd), [attn, wbra2],
        [pl.BlockSpec((tml, D), lambda i, n: (i, 0)), pl.BlockSpec((D, tnd), lambda i, n: (0, n))],
        [(0, 1, 0, NN)], [jax.ShapeDtypeStruct((L, D), F32)], [out_ld], _store_all)

    cb = wbrs.shape[-1]
    gpb = INS // cb

    def gate_spec(first_shard):
        return pl.BlockSpec((None, tml, cb), lambda i, j: (first_shard + j // gpb, i, j % gpb))

    def merge_epilogue(accs, ins, outs, pids):
        br = accs[0]
        outs[0][...] = br
        outs[1][...] = (_sigmoid(ins[2][...]) * ins[4][...] + _sigmoid(ins[3][...]) * br).astype(BF16)

    col_blk = pl.BlockSpec((tml, cb), lambda i, j: (i, j))
    (br_s, merged), _ = _matmul(
        "br_ssm_merge", (L // tml, N_DEV), [y2, wbrs, p27, p27, br_a],
        [pl.BlockSpec((tml, W), lambda i, j: (i, 0)), pl.BlockSpec((None, W, cb), lambda i, j: (j, 0, 0)),
         gate_spec(2), gate_spec(4), col_blk],
        [(0, 1, 0, NN)], [jax.ShapeDtypeStruct((L, D), F32), jax.ShapeDtypeStruct((L, D), BF16)],
        [col_blk, col_blk], merge_epilogue)

    def out_epilogue(accs, ins, outs, pids):
        outs[0][...] = accs[0]
        outs[1][...] = ins[2][...] + ins[3][...] * accs[0]

    g2row = g2[None, :]
    (mix, x2_), _ = _matmul(
        "out_proj", (L // tml, D // tnd), [merged, wout2, xc1, g2row],
        [pl.BlockSpec((tml, D), lambda i, n: (i, 0)), pl.BlockSpec((D, tnd), lambda i, n: (0, n)), out_ld,
         pl.BlockSpec((1, tnd), lambda i, n: (0, n))],
        [(0, 1, 0, NN)], [jax.ShapeDtypeStruct((L, D), F32)] * 2, [out_ld, out_ld], out_epilogue)

    gate3 = grp(g3, g3)
    h3 = _norm_mod_fwd("nm3_fwd", x2_, gam[2], grp(sh3, sh3), grp(sc3, sc3), L, 0)
    a3, b3, s3, _ = _ffn_up("ffn2", h3, wg2, wu2)
    f3, x3, _ = _ffn_down("ffn2", s3, wd2, x2_, gate3[:, 0, :], L)

    trd = _row_tile(L, 0)
    rowD = pl.BlockSpec((trd, D), lambda i: (i, 0))

    def loss_fn(i, yv, t):
        err = yv - t
        return [err * (1.0 / D)], [_colsum(err * err)]

    (dx3,), (sq,), _ = _rowwise("loss", L // trd, [x3, tgt], [rowD, rowD],
                                [jax.ShapeDtypeStruct((L, D), F32)], [rowD], [D], loss_fn)
    loss = lax.psum(0.5 * jnp.sum(sq) / D, ("x", "y", "c"))

    def to_pairs(tag, grads):
        halves = _exchange_only("swap_" + tag, _SiblingSwap(grads))
        return [_pair_sum("pair_%s%d" % (tag, k), halves[2 * k + 1], halves[2 * k]) for k in range(len(grads))]

    df3, (dg3, _) = _gate_bwd("gate3_bwd", dx3, f3, gate3, 0.5, L, 0)
    da3, db3 = _ffn_ds("ffn2b", df3, wd2, a3, b3)
    dwd2, _ = _ffn_dwd("ffn2b", s3, df3)
    (p_wd2,) = to_pairs("wd2", [dwd2])
    dwg2, dwu2, (l_wd2,) = _ffn_dwgu("ffn2b", h3, da3, db3, comm=_ChipExchange([p_wd2]))
    p_wg2, p_wu2 = to_pairs("wgu2", [dwg2, dwu2])
    dh3, (l_wg2,) = _ffn_dh("ffn2b", da3, db3, wg2, wu2, comm=_ChipExchange([p_wg2]))
    dx2, (dsh3, dsc3, _, _, dgam3) = _norm_mod_bwd("nm3_bwd", x2_, dh3, gam[2], grp(sc3, sc3), L, 0, dres=dx3)

    dmix, (dg2, _) = _gate_bwd("gate2_bwd", dx2, mix, grp(g2, g2), 1.0, L, 0)

    def dmerged_epilogue(accs, ins, outs, pids):
        dm = accs[0]
        ga, gs = _sigmoid(ins[2][...]), _sigmoid(ins[3][...])
        outs[0][...] = (ga * dm).astype(BF16)
        outs[1][...] = (gs * dm).astype(BF16)
        outs[2][...] = (dm * ins[4][...] * ga * (1.0 - ga)).astype(BF16)
        outs[3][...] = (dm * ins[5][...] * gs * (1.0 - gs)).astype(BF16)

    dgate_spec = pl.BlockSpec((None, tml, cb), lambda i, j: (j // gpb, i, j % gpb))
    (d_br_a, d_br_s, dg_a, dg_s), _ = _matmul(
        "dmerged", (L // tml, N_DEV), [dmix, wout2, p27, p27, br_a, br_s],
        [pl.BlockSpec((tml, D), lambda i, j: (i, 0)), pl.BlockSpec((cb, D), lambda i, j: (j, 0)),
         gate_spec(2), gate_spec(4), col_blk, col_blk],
        [(0, 1, 0, NT)],
        [jax.ShapeDtypeStruct((L, D), BF16)] * 2 + [jax.ShapeDtypeStruct((2, L, INS), BF16)] * 2,
        [col_blk, col_blk, dgate_spec, dgate_spec], dmerged_epilogue)

    def wgrad(name, a_mat, b_mat, tmo, tno):
        ka, ma = a_mat.shape
        _, nb_ = b_mat.shape
        return _matmul(
            name, (ma // tmo, nb_ // tno), [a_mat, b_mat],
            [pl.BlockSpec((ka, tmo), lambda m, n: (0, m)), pl.BlockSpec((ka, tno), lambda m, n: (0, n))],
            [(0, 1, 0, TN)], [jax.ShapeDtypeStruct((ma, nb_), BF16)],
            [pl.BlockSpec((tmo, tno), lambda m, n: (m, n))], _store_all)[0][0]

    dwout = wgrad("dw_out", merged, dmix, tnd, tnd)
    dwbra = wgrad("dw_br_attn", attn, d_br_a, tnd, tnd)
    (d_attn,), _ = _matmul(
        "d_attn", (L // tml, D // tnd), [d_br_a, wbra2],
        [pl.BlockSpec((tml, D), lambda i, n: (i, 0)), pl.BlockSpec((tnd, D), lambda i, n: (n, 0))],
        [(0, 1, 0, NT)], [jax.ShapeDtypeStruct((L, D), BF16)], [out_ld], _store_all)

    (dwbrs,), _ = _matmul(
        "dw_br_ssm", (N_DEV,), [y2, d_br_s],
        [pl.BlockSpec((L, W), lambda j: (0, 0)), pl.BlockSpec((L, cb), lambda j: (0, j))],
        [(0, 1, 0, TN)], [jax.ShapeDtypeStruct((N_DEV, W, cb), BF16)],
        [pl.BlockSpec((None, W, cb), lambda j: (j, 0, 0))], _store_all)

    def dy2_epilogue(accs, ins, outs, pids):
        dy2 = accs[0]
        sg = _sigmoid(ins[2][...])
        outs[0][...] = dy2 * sg
        outs[1][...] = (dy2 * _gelu(ins[3][...]) * sg * (1.0 - sg)).astype(BF16)

    wn_blk = pl.BlockSpec((tml, tnw), lambda i, n, k: (i, n))
    (dyg1, dz), _ = _matmul(
        "d_y2", (L // tml, W // tnw, N_DEV), [d_br_s, wbrs, z_glu, ssm_out],
        [pl.BlockSpec((tml, cb), lambda i, n, k: (i, k)), pl.BlockSpec((None, tnw, cb), lambda i, n, k: (k, n, 0)),
         wn_blk, wn_blk],
        [(0, 1, 0, NT)], [jax.ShapeDtypeStruct((L, W), F32), jax.ShapeDtypeStruct((L, W), BF16)],
        [wn_blk, wn_blk], dy2_epilogue, acc_shapes=[(tml, tnw)], nk=N_DEV)

    dwglu = wgrad("dw_glu", yg, dz, tnw, tnw)
    p_wout, p_wbra, p_wbrs, p_wglu = to_pairs(
        "mix", [dwout.reshape(N_DEV, D // N_DEV, D), dwbra.reshape(N_DEV, D // N_DEV, D), dwbrs,
                dwglu.reshape(N_DEV, W // N_DEV, W)])

    def dssm_epilogue(accs, ins, outs, pids):
        outs[0][...] = (accs[0] + ins[2][...]) * _gelu_grad(ins[3][...])

    wn2 = pl.BlockSpec((tml, tnw), lambda i, n: (i, n))
    (dssm,), _ = _matmul(
        "d_ssm", (L // tml, W // tnw), [dz, wglu2, dyg1, ssm_out],
        [pl.BlockSpec((tml, W), lambda i, n: (i, 0)), pl.BlockSpec((tnw, W), lambda i, n: (n, 0)), wn2, wn2],
        [(0, 1, 0, NT)], [jax.ShapeDtypeStruct((L, W), F32)], [wn2], dssm_epilogue)

    dssm_all = jnp.concatenate([dssm, jnp.zeros((Lc, W), F32)], axis=0)
    du0, dbd0, dcd0, dlam0, (l_wu2,) = _ssm_bwd(
        "ssm_bwd0", dssm_all, p01, 1, states[0], ssm[0]["caug"], ssm[0]["tabc"], ssm[0]["bdt"],
        not ssm[0]["fwd_desc"], order_adj[0], t_rows, R, comm=_ChipExchange([p_wu2]))
    du1, dbd1, dcd1, dlam1, (l_wout, l_wbra, l_wbrs, l_wglu) = _ssm_bwd(
        "ssm_bwd1", dssm_all, p01, 1, states[1], ssm[1]["caug"], ssm[1]["tabc"], ssm[1]["bdt"],
        not ssm[1]["fwd_desc"], order_adj[1], t_rows, R, comm=_ChipExchange([p_wout, p_wbra, p_wbrs, p_wglu]))
    ssm_grads = [(dbd0, dcd0, dlam0), (dbd1, dcd1, dlam1)]

    trr = _row_tile(L, Lc)
    nlt = L // trr
    rowR = pl.BlockSpec((trr, W), lambda i: (i, 0))

    def du_fn(i, dua, dub, dsv, dvec, u):
        lat = (i < nlt).astype(F32)
        return [dua + dub + lat * (dvec * dsv)], [lat * _colsum(dsv * u)]

    (du_all,), (d_ssm_d,), _ = _rowwise(
        "du_combine", R // trr, [du0, du1, dssm_all, ssm_d, p01],
        [rowR, rowR, rowR, pl.BlockSpec((1, W), lambda i: (0, 0)), pl.BlockSpec((None, trr, W), lambda i: (1, i, 0))],
        [jax.ShapeDtypeStruct((R, W), BF16)], [rowR], [W], du_fn)

    def dz_sum(i, dzv):
        return [], [_colsum(dzv.astype(F32))]

    _, (d_b_glu,), _ = _rowwise("db_glu", L // tr, [dz], [rowW], [], [], [W], dz_sum)

    dq_rot, dk_rot, dv_hd, _ = _attn_bwd(q_rot, k_rot, v_hd, d_attn, QPK)
    dq_pre, d_qg = _qk_prep_bwd("q_prep_bwd", dq_rot, p27, 0, HBQ, NQ, L, q_norm_g, cos_l, sin_l)
    dk_pre, d_kg = _qk_prep_bwd("k_prep_bwd", dk_rot, p01, 0, NKV, NKV, R, k_norm_g, cos_all, sin_all)
    dv_pre = _heads_merge("dv_merge", dv_hd)

    def lat_blocks(a):
        return jnp.pad(a, ((0, 0), (0, Lc), (0, 0)))

    dq_blocks = jnp.transpose(dq_pre.reshape(L, 2, INS), (1, 0, 2))
    dp = jnp.concatenate([
        jnp.concatenate([dk_pre, dv_pre], axis=1)[None], du_all[None],
        lat_blocks(dq_blocks), lat_blocks(dg_a), lat_blocks(dg_s)], axis=0)

    tmo = _tile(D, MM_TILE, LANES)
    (dwin,), _ = _matmul(
        "dw_in", (N_DEV, D // tmo), [h2, dp],
        [pl.BlockSpec((R, tmo), lambda j, m: (0, m)), pl.BlockSpec((None, R, INS), lambda j, m: (j, 0, 0))],
        [(0, 1, 0, TN)], [jax.ShapeDtypeStruct((N_DEV, D, INS), BF16)],
        [pl.BlockSpec((None, tmo, INS), lambda j, m: (j, m, 0))], _store_all)
    (p_win,) = to_pairs("win", [dwin])
    tnh = _tile(D, MM_TILE_NT, LANES)
    (dh2,), (l_win,) = _matmul(
        "d_h2", (R // tm, D // tnh), [dp, win],
        [pl.BlockSpec((N_DEV, tm, INS), lambda i, n: (0, i, 0)),
         pl.BlockSpec((N_DEV, tnh, INS), lambda i, n: (0, n, 0))],
        [(0, 1, 0, NT, N_DEV)], [jax.ShapeDtypeStruct((R, D), F32)], [pl.BlockSpec((tm, tnh), lambda i, n: (i, n))],
        _store_all, comm=_ChipExchange([p_win]))
    dxc1, (dsh2, dsc2, dmc3, dmc4, dgam2) = _norm_mod_bwd(
        "nm2_bwd", xc1, dh2, gam[1], grp(sc2, mc4), L, Lc, dres=dx2)

    df1, (dg1, dmc2) = _gate_bwd("gate1_bwd", dxc1, f1, gate1, 0.5, L, Lc)
    da1, db1 = _ffn_ds("ffn1b", df1, wd1, a1, b1)
    dwd1, _ = _ffn_dwd("ffn1b", s1, df1)
    (p_wd1,) = to_pairs("wd1", [dwd1])
    dwg1, dwu1, (l_wd1,) = _ffn_dwgu("ffn1b", h1, da1, db1, comm=_ChipExchange([p_wd1]))
    p_wg1, p_wu1 = to_pairs("wgu1", [dwg1, dwu1])
    dh1, (l_wg1,) = _ffn_dh("ffn1b", da1, db1, wg1, wu1, comm=_ChipExchange([p_wg1]))
    dxc0, (dsh1, dsc1, dmc0, dmc1, dgam1) = _norm_mod_bwd(
        "nm1_bwd", xc0, dh1, gam[0], grp(sc1, mc1), L, Lc, dres=dxc1)
    grad_x = dxc0[:L][None]

    dmod_lat = jnp.concatenate([dsh1, dsc1, dg1, dsh2, dsc2, dg2, dsh3, dsc3, dg3], axis=1)
    dmod_ctx = jnp.concatenate([dmc0, dmc1, dmc2, dmc3, dmc4, jnp.zeros((1, 4 * D), F32)], axis=1)
    dmod_pack = at_row(dmod_lat, 0, SUBLANES) + at_row(dmod_ctx, 1, SUBLANES)
    (dmod_g,) = _exchange_only("ag_dmod", _Gather([dmod_pack]))
    dmod_all = dmod_g.reshape(N_DEV * SUBLANES, 9 * D)
    dmod_cols = lax.dynamic_slice_in_dim(dmod_all, me * MODW, MODW, axis=1)
    (g_wmod, dl_wmod, nm_wmod, nv_wmod, dsilu), (l_wu1,) = _mod_bwd_adam(
        cs, dmod_cols, w_mod[0], m_w_mod[0], v_w_mod[0], comm=_ChipExchange([p_wu1]))
    sg_cc = jax.nn.sigmoid(c_ctx)
    d_c_ctx = dsilu[8] * (sg_cc * (1.0 + c_ctx * (1.0 - sg_cc)))
    g_bmod, dl_bmod, nm_bmod, nv_bmod = _bias_adam(dmod_all, b_mod, m_b_mod, v_b_mod)

    d_a_re, d_a_im, d_ldt, d_b_re, d_b_im, d_c_re, d_c_im = [], [], [], [], [], [], []
    for d_ in range(2):
        dbd, dcd, dlam = ssm_grads[d_]
        dbt_re = jnp.transpose(_block_diag_extract(dbd[:, :, :SW], E, P), (0, 2, 1))
        dbt_im = jnp.transpose(_block_diag_extract(dbd[:, :, SW:], E, P), (0, 2, 1))
        dl_re, dl_im = dlam[:, 0, :SW].reshape(G, P), dlam[:, 0, SW:].reshape(G, P)
        prim = (ssm_a_re[0, d_], ssm_a_im[0, d_], ssm_log_dt[0, d_], ssm_b_re[0, d_], ssm_b_im[0, d_])
        _, vjp = jax.vjp(_ssm_discretize, *prim)
        ga_re, ga_im, gl_dt, gb_re, gb_im = vjp((dl_re, dl_im, dbt_re, dbt_im))
        d_a_re.append(ga_re)
        d_a_im.append(ga_im)
        d_ldt.append(gl_dt)
        d_b_re.append(gb_re)
        d_b_im.append(gb_im)
        d_c_re.append(jnp.transpose(_block_diag_extract(dcd[:, :SW, :], P, E), (0, 2, 1)))
        d_c_im.append(-jnp.transpose(_block_diag_extract(dcd[:, SW:, :], P, E), (0, 2, 1)))

    dgam_all = jnp.concatenate([dgam1, dgam2, dgam3], axis=0)
    small_g = [d_c_ctx, d_qg, d_kg, jnp.stack(d_a_re), jnp.stack(d_a_im), jnp.stack(d_ldt), jnp.stack(d_b_re),
               jnp.stack(d_b_im), jnp.stack(d_c_re), jnp.stack(d_c_im), d_ssm_d, d_b_glu, dgam_all]
    small_w = [c_ctx, q_norm_g, k_norm_g, ssm_a_re, ssm_a_im, ssm_log_dt, ssm_b_re, ssm_b_im, ssm_c_re, ssm_c_im,
               ssm_d, b_glu, ng_full]
    small_m = [m_c_ctx, m_q_norm_g, m_k_norm_g, m_ssm_a_re, m_ssm_a_im, m_ssm_log_dt, m_ssm_b_re, m_ssm_b_im,
               m_ssm_c_re, m_ssm_c_im, m_ssm_d, m_b_glu, m_ng_full]
    small_v = [v_c_ctx, v_q_norm_g, v_k_norm_g, v_ssm_a_re, v_ssm_a_im, v_ssm_log_dt, v_ssm_b_re, v_ssm_b_im,
               v_ssm_c_re, v_ssm_c_im, v_ssm_d, v_b_glu, v_ng_full]
    small_shapes = [a.shape for a in small_w]
    n_rows = sum(-(-math.prod(s) // LANES) for s in small_shapes)
    n_rows = -(-n_rows // 256) * 256
    (small_parts,) = _exchange_only("ag_small_grads", _Gather([_pack(small_g, n_rows)]))
    small_out = _sum_adam("small_adam", small_parts, _pack(small_w, n_rows), _pack(small_m, n_rows),
                          _pack(small_v, n_rows))
    sm_g, sm_dl, sm_m, sm_v = [_unpack(o, small_shapes) for o in small_out]

    def my_norm_cols(a):
        return lax.dynamic_slice_in_dim(a, me * dn, dn, axis=1)[None]

    for lst in (sm_g, sm_dl, sm_m, sm_v):
        lst[-1] = my_norm_cols(lst[-1])

    landed = [l_wg1, l_wu1, l_wd1, l_win, l_wglu, l_wbra, l_wbrs, l_wout, l_wg2, l_wu2, l_wd2]
    big_w = [w_ffn1_gate, w_ffn1_up, w_ffn1_down, w_in, w_glu, w_br_attn, w_br_ssm, w_out, w_ffn2_gate, w_ffn2_up,
             w_ffn2_down]
    big_m = [m_w_ffn1_gate, m_w_ffn1_up, m_w_ffn1_down, m_w_in, m_w_glu, m_w_br_attn, m_w_br_ssm, m_w_out,
             m_w_ffn2_gate, m_w_ffn2_up, m_w_ffn2_down]
    big_v = [v_w_ffn1_gate, v_w_ffn1_up, v_w_ffn1_down, v_w_in, v_w_glu, v_w_br_attn, v_w_br_ssm, v_w_out,
             v_w_ffn2_gate, v_w_ffn2_up, v_w_ffn2_down]
    big_names = ["ffn1_gate", "ffn1_up", "ffn1_down", "in", "glu", "br_attn", "br_ssm", "out", "ffn2_gate",
                 "ffn2_up", "ffn2_down"]
    big_out = [[o[None] for o in _sum_adam("adam_" + nm, p, w_[0], m_[0], v_[0])]
               for nm, p, w_, m_, v_ in zip(big_names, landed, big_w, big_m, big_v)]

    def leaf(kind):
        sm = (sm_g, sm_dl, sm_m, sm_v)[kind]
        mod = (g_wmod, dl_wmod, nm_wmod, nv_wmod)[kind][None]
        bmod = (g_bmod, dl_bmod, nm_bmod, nv_bmod)[kind]
        big = [b[kind] for b in big_out]
        (c_ctx_, qg_, kg_, a_re_, a_im_, ldt_, b_re_, b_im_, c_re_, c_im_, sd_, bglu_, ng_) = sm
        return [c_ctx_, mod, bmod, ng_, big[0], big[1], big[2], big[3], qg_, kg_, a_re_, a_im_, ldt_, b_re_, b_im_,
                c_re_, c_im_, sd_, big[4], bglu_, big[5], big[6], big[7], big[8], big[9], big[10]]

    return tuple([loss, grad_x] + leaf(0) + leaf(1) + leaf(2) + leaf(3))
```

```python
import math

import jax
import jax.numpy as jnp
from jax import lax
from jax.experimental import pallas as pl
from jax.experimental.pallas import tpu as pltpu

F32 = jnp.float32
BF16 = jnp.bfloat16

N_DEV = 8
N_CHIPS = 4
LANES = 128
SUBLANES = 8
PACKED_SUBLANES = 16
VMEM_LIMIT = 56 * 1024 * 1024
MM_TILE = 512
MM_TILE_NT = 256
ROW_TILE = 256
HEAD_ROW_TILE = 512
ADAM_BLOCK_BYTES = 4 * 1024 * 1024

NORM_EPS = 1e-6
GRID_W = 64
ROPE_THETA = 10000.0
SCAN_TAPS = SUBLANES
SLAB_GROUPS = 8

ADAM_LR = 0.001
ADAM_B1 = 0.9
ADAM_B2 = 0.999
ADAM_EPS = 1e-08
ADAM_WD = 0.01
ADAM_STEP = 10

NN = (((1,), (0,)), ((), ()))
NT = (((1,), (1,)), ((), ()))
TN = (((0,), (0,)), ((), ()))

MESH = pl.DeviceIdType.MESH
ANY = pl.BlockSpec(memory_space=pl.ANY)


def _tile(n, cap, align):
    best = None
    for t in range(align, min(n, cap) + 1, align):
        if n % t == 0:
            best = t
    return n if best is None else best


def _params(n_grid):
    return pltpu.CompilerParams(dimension_semantics=("arbitrary",) * n_grid, vmem_limit_bytes=VMEM_LIMIT)


def _sigmoid(x):
    return 1.0 / (1.0 + jnp.exp(-x))


GELU_K = math.sqrt(2.0 / math.pi)
GELU_C = 0.044715


def _gelu(x):
    return 0.5 * x * (1.0 + jnp.tanh(GELU_K * (x + GELU_C * x * x * x)))


def _gelu_grad(x):
    t = jnp.tanh(GELU_K * (x + GELU_C * x * x * x))
    return 0.5 * (1.0 + t) + 0.5 * x * (1.0 - t * t) * GELU_K * (1.0 + 3.0 * GELU_C * x * x)


def _adamw(w, g, m, v):
    m2 = ADAM_B1 * m + (1.0 - ADAM_B1) * g
    v2 = ADAM_B2 * v + (1.0 - ADAM_B2) * (g * g)
    m_hat = m2 / (1.0 - ADAM_B1 ** ADAM_STEP)
    v_hat = v2 / (1.0 - ADAM_B2 ** ADAM_STEP)
    delta = -ADAM_LR * (m_hat / (jnp.sqrt(v_hat) + ADAM_EPS) + ADAM_WD * w)
    return delta, m2, v2


def _position():
    return lax.axis_index("x"), lax.axis_index("y"), lax.axis_index("c")


class _Gather:
    def __init__(self, arrays):
        self.arrays = list(arrays)
        n = len(self.arrays)
        self.out_shapes = [jax.ShapeDtypeStruct((N_DEV,) + a.shape, a.dtype) for a in self.arrays]
        self.scratch = [pltpu.SemaphoreType.DMA((n, 7)), pltpu.SemaphoreType.DMA((n, 7)),
                        pltpu.SemaphoreType.DMA((n,))]

    def _plan(self, ins, outs, sems):
        send, recv, local = sems
        x, y, c = _position()
        me, sibling = (x, y, c), (x, y, 1 - c)
        chips = [(1 - x, y), (x, 1 - y), (1 - x, 1 - y)]

        def slot(a, p):
            return outs[a].at[4 * p[0] + 2 * p[1] + p[2]]

        def copy(a, k, block, to, src=None):
            dst = slot(a, block)
            return pltpu.make_async_remote_copy(
                src_ref=dst if src is None else src, dst_ref=dst,
                send_sem=send.at[a, k], recv_sem=recv.at[a, k], device_id=to, device_id_type=MESH)

        mine = [pltpu.make_async_copy(ins[a], slot(a, me), local.at[a]) for a in range(len(ins))]
        return me, sibling, chips, c, copy, mine

    def start(self, ins, outs, sems):
        me, sibling, chips, c, copy, mine = self._plan(ins, outs, sems)
        for cp in mine:
            cp.start()
        for a in range(len(ins)):
            copy(a, 0, me, sibling, src=ins[a]).start()
            for j, chip in enumerate(chips):
                copy(a, 1 + j, me, (*chip, c), src=ins[a]).start()

    def finish(self, ins, outs, sems):
        me, sibling, chips, c, copy, mine = self._plan(ins, outs, sems)
        n = len(ins)
        for j, chip in enumerate(chips):
            for a in range(n):
                copy(a, 1 + j, (*chip, c), me).wait_recv()
                copy(a, 4 + j, (*chip, c), sibling).start()
        for a in range(n):
            copy(a, 0, sibling, me).wait_recv()
        for j, chip in enumerate(chips):
            for a in range(n):
                copy(a, 4 + j, (*chip, 1 - c), me).wait_recv()
        for a in range(n):
            copy(a, 0, me, sibling, src=ins[a]).wait_send()
            for j, chip in enumerate(chips):
                copy(a, 1 + j, me, (*chip, c), src=ins[a]).wait_send()
                copy(a, 4 + j, (*chip, c), sibling).wait_send()
        for cp in mine:
            cp.wait()


class _SiblingSwap:
    def __init__(self, arrays):
        self.arrays = list(arrays)
        n = len(self.arrays)
        self.out_shapes = [jax.ShapeDtypeStruct((N_CHIPS,) + a.shape[1:], a.dtype) for a in self.arrays]
        self.scratch = [pltpu.SemaphoreType.DMA((n, N_CHIPS)), pltpu.SemaphoreType.DMA((n, N_CHIPS))]

    def _plan(self, ins, outs, sems):
        send, recv = sems
        x, y, c = _position()
        return [pltpu.make_async_remote_copy(
            src_ref=ins[a].at[2 * j + 1 - c], dst_ref=outs[a].at[j],
            send_sem=send.at[a, j], recv_sem=recv.at[a, j], device_id=(x, y, 1 - c), device_id_type=MESH)
            for a in range(len(ins)) for j in range(N_CHIPS)]

    def start(self, ins, outs, sems):
        for cp in self._plan(ins, outs, sems):
            cp.start()

    def finish(self, ins, outs, sems):
        copies = self._plan(ins, outs, sems)
        for cp in copies:
            cp.wait_recv()
        for cp in copies:
            cp.wait_send()


class _ChipExchange:
    def __init__(self, arrays):
        self.arrays = list(arrays)
        n = len(self.arrays)
        self.out_shapes = [jax.ShapeDtypeStruct((N_CHIPS - 1,) + a.shape[1:], a.dtype) for a in self.arrays]
        self.scratch = [pltpu.SemaphoreType.DMA((n, N_CHIPS - 1)), pltpu.SemaphoreType.DMA((n, N_CHIPS - 1))]

    def _plan(self, ins, outs, sems):
        send, recv = sems
        x, y, c = _position()
        copies = []
        for r in range(1, N_CHIPS):
            px, py = x ^ (r >> 1), y ^ (r & 1)
            for a in range(len(ins)):
                copies.append(pltpu.make_async_remote_copy(
                    src_ref=ins[a].at[2 * px + py], dst_ref=outs[a].at[r - 1],
                    send_sem=send.at[a, r - 1], recv_sem=recv.at[a, r - 1],
                    device_id=(px, py, c), device_id_type=MESH))
        return copies

    def start(self, ins, outs, sems):
        for cp in self._plan(ins, outs, sems):
            cp.start()

    def finish(self, ins, outs, sems):
        copies = self._plan(ins, outs, sems)
        for cp in copies:
            cp.wait_recv()
        for cp in copies:
            cp.wait_send()


class _Both:
    def __init__(self, comms):
        self.comms = list(comms)
        self.arrays = [a for cm in self.comms for a in cm.arrays]
        self.out_shapes = [s for cm in self.comms for s in cm.out_shapes]
        self.scratch = [s for cm in self.comms for s in cm.scratch]

    def _split(self, ins, outs, sems):
        i = o = s = 0
        for cm in self.comms:
            ni, no, nsem = len(cm.arrays), len(cm.out_shapes), len(cm.scratch)
            yield cm, ins[i:i + ni], outs[o:o + no], sems[s:s + nsem]
            i, o, s = i + ni, o + no, s + nsem

    def start(self, ins, outs, sems):
        for cm, i, o, s in self._split(ins, outs, sems):
            cm.start(i, o, s)

    def finish(self, ins, outs, sems):
        for cm, i, o, s in self._split(ins, outs, sems):
            cm.finish(i, o, s)


def _host_call(body, *, name, grid, operands, in_specs, out_shape, out_specs, scratch_shapes=(), comm=None):
    grid = tuple(grid)
    n_in, n_out, n_scr = len(operands), len(out_shape), len(scratch_shapes)
    kwargs = dict(name=name, compiler_params=_params(len(grid)))
    if grid:
        kwargs["grid"] = grid
    if comm is None:
        res = pl.pallas_call(body, in_specs=list(in_specs), out_specs=list(out_specs), out_shape=list(out_shape),
                             scratch_shapes=list(scratch_shapes), **kwargs)(*operands)
        return list(res), []
    nc_in, nc_out = len(comm.arrays), len(comm.out_shapes)

    def hosted(*refs):
        bounds = [0, n_in, n_in + nc_in, n_in + nc_in + n_out, n_in + nc_in + n_out + nc_out,
                  n_in + nc_in + n_out + nc_out + n_scr, len(refs)]
        ins, cins, outs, couts, scr, sems = [refs[a:b] for a, b in zip(bounds[:-1], bounds[1:])]
        if not grid:
            comm.start(cins, couts, sems)
            body(*ins, *outs, *scr)
            comm.finish(cins, couts, sems)
            return
        first, last = None, None
        for ax, size in enumerate(grid):
            pid = pl.program_id(ax)
            f, l = pid == 0, pid == size - 1
            first = f if first is None else jnp.logical_and(first, f)
            last = l if last is None else jnp.logical_and(last, l)

        @pl.when(first)
        def _():
            comm.start(cins, couts, sems)

        body(*ins, *outs, *scr)

        @pl.when(last)
        def _():
            comm.finish(cins, couts, sems)

    res = pl.pallas_call(
        hosted, in_specs=list(in_specs) + [ANY] * nc_in, out_specs=list(out_specs) + [ANY] * nc_out,
        out_shape=list(out_shape) + list(comm.out_shapes), scratch_shapes=list(scratch_shapes) + list(comm.scratch),
        **kwargs)(*operands, *comm.arrays)
    return list(res[:n_out]), list(res[n_out:])


def _exchange_only(name, comm):
    def body():
        pass
    return _host_call(body, name=name, grid=(), operands=[], in_specs=[], out_shape=[], out_specs=[], comm=comm)[1]


def _matmul(name, grid, operands, in_specs, pairs, out_shapes, out_specs, epilogue, acc_shapes=(), nk=1,
            prologue=None, comm=None):
    n_in, n_out = len(operands), len(out_shapes)
    prologue = prologue or {}

    def body(*refs):
        ins, outs, accs = refs[:n_in], refs[n_in:n_in + n_out], refs[n_in + n_out:]
        pids = [pl.program_id(ax) for ax in range(len(grid))]

        def operand(i, blk=None):
            v = ins[i][...] if blk is None else ins[i][blk]
            if i in prologue:
                v = prologue[i](v)
            return v.astype(BF16)

        def products():
            vals = {}
            for pair in pairs:
                ai, bi, ci, dn = pair[:4]
                if len(pair) == 5:
                    p = None
                    for blk in range(pair[4]):
                        q = lax.dot_general(operand(ai, blk), operand(bi, blk), dn, preferred_element_type=F32)
                        p = q if p is None else p + q
                else:
                    p = lax.dot_general(operand(ai), operand(bi), dn, preferred_element_type=F32)
                vals[ci] = p if ci not in vals else vals[ci] + p
            return [vals[ci] for ci in sorted(vals)]

        if nk == 1:
            epilogue(products(), ins, outs, pids)
        else:
            k = pids[-1]
            prods = products()

            @pl.when(k == 0)
            def _():
                for acc, p in zip(accs, prods):
                    acc[...] = p

            @pl.when(k > 0)
            def _():
                for acc, p in zip(accs, prods):
                    acc[...] += p

            @pl.when(k == nk - 1)
            def _():
                epilogue([acc[...] for acc in accs], ins, outs, pids)

    return _host_call(
        body, name=name, grid=grid, operands=operands, in_specs=in_specs, out_shape=out_shapes, out_specs=out_specs,
        scratch_shapes=[pltpu.VMEM(s, F32) for s in acc_shapes] if nk > 1 else [], comm=comm)


def _rowwise(name, n_tiles, operands, in_specs, out_shapes, out_specs, red_widths, fn, comm=None):
    n_in, n_out, n_red = len(operands), len(out_shapes), len(red_widths)

    def body(*refs):
        ins, outs, reds = refs[:n_in], refs[n_in:n_in + n_out], refs[n_in + n_out:]
        i = pl.program_id(0)
        vals, sums = fn(i, *[r[...] for r in ins])
        for o, v in zip(outs, vals):
            o[...] = v.astype(o.dtype)
        if n_red:
            @pl.when(i == 0)
            def _():
                for r, s in zip(reds, sums):
                    r[...] = s

            @pl.when(i > 0)
            def _():
                for r, s in zip(reds, sums):
                    r[...] += s

    red_shapes = [jax.ShapeDtypeStruct((1, w), F32) for w in red_widths]
    red_specs = [pl.BlockSpec((1, w), lambda i: (0, 0)) for w in red_widths]
    res, cres = _host_call(
        body, name=name, grid=(n_tiles,), operands=operands, in_specs=in_specs,
        out_shape=list(out_shapes) + red_shapes, out_specs=list(out_specs) + red_specs, comm=comm)
    return res[:n_out], res[n_out:], cres


def _colsum(v):
    return jnp.sum(v, axis=0, keepdims=True)


def _store_all(accs, ins, outs, pids):
    for o, v in zip(outs, accs):
        o[...] = v.astype(o.dtype)


def _row_tile(rows_a, rows_b):
    return _tile(math.gcd(rows_a, rows_b) if rows_b else rows_a, ROW_TILE, SUBLANES)


def _norm_mod_fwd(name, xs, gamma, shift2, scale2, n_lat, n_ctx):
    rows, d = xs.shape
    tm = _row_tile(n_lat, n_ctx)
    nlt = n_lat // tm
    grp = pl.BlockSpec((None, 1, d), lambda i: (i // nlt, 0, 0))

    def fn(i, x, g, sh, sc):
        xh = x * lax.rsqrt(jnp.mean(x * x, axis=-1, keepdims=True) + NORM_EPS)
        return [(xh * g) * (1.0 + sc) + sh], []

    (h,), _, _ = _rowwise(
        name, rows // tm, [xs, gamma, shift2, scale2],
        [pl.BlockSpec((tm, d), lambda i: (i, 0)), pl.BlockSpec((1, d), lambda i: (0, 0)), grp, grp],
        [jax.ShapeDtypeStruct((rows, d), BF16)], [pl.BlockSpec((tm, d), lambda i: (i, 0))], [], fn)
    return h


def _norm_mod_bwd(name, xs, dh, gamma, scale2, n_lat, n_ctx, dres=None):
    rows, d = xs.shape
    tm = _row_tile(n_lat, n_ctx)
    nlt = n_lat // tm
    grp = pl.BlockSpec((None, 1, d), lambda i: (i // nlt, 0, 0))
    row = pl.BlockSpec((tm, d), lambda i: (i, 0))

    def fn(i, x, dy, g, sc, *res):
        rstd = lax.rsqrt(jnp.mean(x * x, axis=-1, keepdims=True) + NORM_EPS)
        xh = x * rstd
        dsh = _colsum(dy)
        dsc = _colsum(dy * (xh * g))
        dn = dy * (1.0 + sc)
        dgam = _colsum(dn * xh)
        dxh = dn * g
        dx = rstd * (dxh - xh * jnp.mean(dxh * xh, axis=-1, keepdims=True))
        if res:
            dx = dx + jnp.where(i < nlt, res[0], 0.0)
        lat = (i < nlt).astype(F32)
        return [dx], [dsh * lat, dsc * lat, dsh * (1.0 - lat), dsc * (1.0 - lat), dgam]

    operands = [xs, dh, gamma, scale2]
    specs = [row, row, pl.BlockSpec((1, d), lambda i: (0, 0)), grp]
    if dres is not None:
        operands.append(dres)
        specs.append(pl.BlockSpec((tm, d), lambda i: (jnp.minimum(i, nlt - 1), 0)))
    (dx,), sums, _ = _rowwise(name, rows // tm, operands, specs,
                              [jax.ShapeDtypeStruct((rows, d), F32)], [row], [d] * 5, fn)
    return dx, sums


def _gate_bwd(name, dx, f, gate2, coef, n_lat, n_ctx):
    rows, d = dx.shape
    tm = _row_tile(n_lat, n_ctx)
    nlt = n_lat // tm
    row = pl.BlockSpec((tm, d), lambda i: (i, 0))

    def fn(i, dxv, fv, gv):
        dg = _colsum(dxv * fv) * coef
        lat = (i < nlt).astype(F32)
        return [(coef * gv) * dxv], [dg * lat, dg * (1.0 - lat)]

    (df,), sums, _ = _rowwise(
        name, rows // tm, [dx, f, gate2],
        [row, row, pl.BlockSpec((None, 1, d), lambda i: (i // nlt, 0, 0))],
        [jax.ShapeDtypeStruct((rows, d), BF16)], [row], [d, d], fn)
    return df, sums


def _select_rows(i, tm, n_lat, vec2):
    rows = i * tm + lax.broadcasted_iota(jnp.int32, (tm, 1), 0)
    return jnp.where(rows < n_lat, vec2[0:1, :], vec2[1:2, :])


def _ffn_up(tag, h, wg, wu, comm=None):
    rows, d = h.shape
    nb, _, fs = wg.shape
    tm = _tile(rows, MM_TILE, LANES)
    blk = pl.BlockSpec((None, tm, fs), lambda j, i: (j, i, 0))
    wspec = pl.BlockSpec((None, d, fs), lambda j, i: (j, 0, 0))

    def epilogue(accs, ins, outs, pids):
        a, b = accs
        outs[0][...] = a.astype(BF16)
        outs[1][...] = b.astype(BF16)
        outs[2][...] = (a * _sigmoid(a) * b).astype(BF16)

    hid = jax.ShapeDtypeStruct((nb, rows, fs), BF16)
    (a, b, s), cres = _matmul(
        tag + "_up", (nb, rows // tm), [h, wg, wu],
        [pl.BlockSpec((tm, d), lambda j, i: (i, 0)), wspec, wspec],
        [(0, 1, 0, NN), (0, 2, 1, NN)], [hid, hid, hid], [blk, blk, blk], epilogue, comm=comm)
    return a, b, s, cres


def _ffn_down(tag, s, wd, xs, gate2, n_lat, comm=None):
    nb, rows, fs = s.shape
    d = wd.shape[-1]
    tm = _tile(rows, MM_TILE, LANES)
    tn = _tile(d, MM_TILE, LANES)

    def epilogue(accs, ins, outs, pids):
        f = accs[0]
        outs[0][...] = f
        outs[1][...] = ins[2][...] + 0.5 * _select_rows(pids[0], tm, n_lat, ins[3][...]) * f

    out = jax.ShapeDtypeStruct((rows, d), F32)
    ospec = pl.BlockSpec((tm, tn), lambda i, n: (i, n))
    (f, xo), cres = _matmul(
        tag + "_down", (rows // tm, d // tn), [s, wd, xs, gate2],
        [pl.BlockSpec((nb, tm, fs), lambda i, n: (0, i, 0)), pl.BlockSpec((nb, fs, tn), lambda i, n: (0, 0, n)),
         ospec, pl.BlockSpec((2, tn), lambda i, n: (0, n))],
        [(0, 1, 0, NN, nb)], [out, out], [ospec, ospec], epilogue, comm=comm)
    return f, xo, cres


def _ffn_ds(tag, df, wd, a, b, comm=None):
    rows, d = df.shape
    nb, fs, _ = wd.shape
    tm = _tile(rows, MM_TILE, LANES)
    blk = pl.BlockSpec((None, tm, fs), lambda j, i: (j, i, 0))

    def epilogue(accs, ins, outs, pids):
        ds = accs[0]
        av = ins[2][...].astype(F32)
        bv = ins[3][...].astype(F32)
        sg = _sigmoid(av)
        outs[0][...] = (ds * bv * (sg * (1.0 + av * (1.0 - sg)))).astype(BF16)
        outs[1][...] = (ds * (av * sg)).astype(BF16)

    hid = jax.ShapeDtypeStruct((nb, rows, fs), BF16)
    (da, db), cres = _matmul(
        tag + "_ds", (nb, rows // tm), [df, wd, a, b],
        [pl.BlockSpec((tm, d), lambda j, i: (i, 0)), pl.BlockSpec((None, fs, d), lambda j, i: (j, 0, 0)), blk, blk],
        [(0, 1, 0, NT)], [hid, hid], [blk, blk], epilogue, comm=comm)
    return da, db, cres


def _ffn_dwd(tag, s, df, comm=None):
    nb, rows, fs = s.shape
    d = df.shape[-1]
    tn = _tile(d, MM_TILE, LANES)
    (dwd,), cres = _matmul(
        tag + "_dwd", (nb, d // tn), [s, df],
        [pl.BlockSpec((None, rows, fs), lambda j, n: (j, 0, 0)), pl.BlockSpec((rows, tn), lambda j, n: (0, n))],
        [(0, 1, 0, TN)], [jax.ShapeDtypeStruct((nb, fs, d), BF16)],
        [pl.BlockSpec((None, fs, tn), lambda j, n: (j, 0, n))], _store_all, comm=comm)
    return dwd, cres


def _ffn_dwgu(tag, h, da, db, comm=None):
    rows, d = h.shape
    nb, _, fs = da.shape
    tmo = _tile(d, MM_TILE, LANES)
    full = pl.BlockSpec((None, rows, fs), lambda j, m: (j, 0, 0))
    wshape = jax.ShapeDtypeStruct((nb, d, fs), BF16)
    wblk = pl.BlockSpec((None, tmo, fs), lambda j, m: (j, m, 0))
    (dwg, dwu), cres = _matmul(
        tag + "_dwgu", (nb, d // tmo), [h, da, db],
        [pl.BlockSpec((rows, tmo), lambda j, m: (0, m)), full, full],
        [(0, 1, 0, TN), (0, 2, 1, TN)], [wshape, wshape], [wblk, wblk], _store_all, comm=comm)
    return dwg, dwu, cres


def _ffn_dh(tag, da, db, wg, wu, comm=None):
    nb, rows, fs = da.shape
    d = wg.shape[1]
    tm = _tile(rows, MM_TILE, LANES)
    tn = _tile(d, MM_TILE_NT, LANES)
    aspec = pl.BlockSpec((nb, tm, fs), lambda i, n: (0, i, 0))
    wspec = pl.BlockSpec((nb, tn, fs), lambda i, n: (0, n, 0))
    (dh,), cres = _matmul(
        tag + "_dh", (rows // tm, d // tn), [da, wg, db, wu], [aspec, wspec, aspec, wspec],
        [(0, 1, 0, NT, nb), (2, 3, 0, NT, nb)], [jax.ShapeDtypeStruct((rows, d), F32)],
        [pl.BlockSpec((tm, tn), lambda i, n: (i, n))], _store_all, comm=comm)
    return dh, cres


def _swap_halves(x):
    lane = lax.broadcasted_iota(jnp.int32, x.shape, 1)
    return jnp.where((lane % 64) < 32, pltpu.roll(x, 96, 1), pltpu.roll(x, 32, 1))


def _heads_spec(tq, hb, width, first_block):
    per_shard = width // (hb * LANES)

    def index(k, i):
        blk = first_block + k
        return blk // per_shard, i, blk % per_shard
    return pl.BlockSpec((None, tq, hb * LANES), index)


def _qk_prep(name, src, first_block, hb, n_heads, rows, g, cos_t, sin_t):
    tq = _tile(rows, HEAD_ROW_TILE, SUBLANES)
    tab = pl.BlockSpec((tq, LANES), lambda k, i: (i, 0))

    def body(x_ref, g_ref, c_ref, s_ref, o_ref):
        for h in range(hb):
            x = x_ref[:, h * LANES:(h + 1) * LANES]
            n = x * lax.rsqrt(jnp.mean(x * x, axis=-1, keepdims=True) + NORM_EPS) * g_ref[...]
            o_ref[h] = (n * c_ref[...] + _swap_halves(n) * s_ref[...]).astype(BF16)

    return pl.pallas_call(
        body, name=name, grid=(n_heads // hb, rows // tq),
        in_specs=[_heads_spec(tq, hb, src.shape[-1], first_block), pl.BlockSpec((1, LANES), lambda k, i: (0, 0)),
                  tab, tab],
        out_specs=pl.BlockSpec((hb, tq, LANES), lambda k, i: (k, i, 0)),
        out_shape=jax.ShapeDtypeStruct((n_heads, rows, LANES), BF16), compiler_params=_params(2),
    )(src, g, cos_t, sin_t)


def _qk_prep_bwd(name, dy, src, first_block, hb, n_heads, rows, g, cos_t, sin_t):
    tq = _tile(rows, HEAD_ROW_TILE, SUBLANES)
    tab = pl.BlockSpec((tq, LANES), lambda k, i: (i, 0))

    def body(dy_ref, x_ref, g_ref, c_ref, s_ref, dx_ref, dg_ref):
        g = g_ref[...]
        dg = None
        for h in range(hb):
            x = x_ref[:, h * LANES:(h + 1) * LANES]
            dyv = dy_ref[h]
            rstd = lax.rsqrt(jnp.mean(x * x, axis=-1, keepdims=True) + NORM_EPS)
            xh = x * rstd
            dn = dyv * c_ref[...] + _swap_halves(dyv * s_ref[...])
            dxh = dn * g
            dx = rstd * (dxh - xh * jnp.mean(dxh * xh, axis=-1, keepdims=True))
            dx_ref[:, h * LANES:(h + 1) * LANES] = dx.astype(BF16)
            part = _colsum(dn * xh)
            dg = part if dg is None else dg + part
        first = jnp.logical_and(pl.program_id(0) == 0, pl.program_id(1) == 0)

        @pl.when(first)
        def _():
            dg_ref[...] = dg

        @pl.when(jnp.logical_not(first))
        def _():
            dg_ref[...] += dg

    return pl.pallas_call(
        body, name=name, grid=(n_heads // hb, rows // tq),
        in_specs=[pl.BlockSpec((hb, tq, LANES), lambda k, i: (k, i, 0)),
                  _heads_spec(tq, hb, src.shape[-1], first_block),
                  pl.BlockSpec((1, LANES), lambda k, i: (0, 0)), tab, tab],
        out_specs=[pl.BlockSpec((tq, hb * LANES), lambda k, i: (i, k)),
                   pl.BlockSpec((1, LANES), lambda k, i: (0, 0))],
        out_shape=[jax.ShapeDtypeStruct((rows, n_heads * LANES), BF16), jax.ShapeDtypeStruct((1, LANES), F32)],
        compiler_params=_params(2),
    )(dy, src, g, cos_t, sin_t)


def _heads_cast(name, src, first_block, hb, n_heads, rows):
    tq = _tile(rows, HEAD_ROW_TILE, SUBLANES)

    def body(x_ref, o_ref):
        for h in range(hb):
            o_ref[h] = x_ref[:, h * LANES:(h + 1) * LANES].astype(BF16)

    return pl.pallas_call(
        body, name=name, grid=(n_heads // hb, rows // tq),
        in_specs=[_heads_spec(tq, hb, src.shape[-1], first_block)],
        out_specs=pl.BlockSpec((hb, tq, LANES), lambda k, i: (k, i, 0)),
        out_shape=jax.ShapeDtypeStruct((n_heads, rows, LANES), BF16), compiler_params=_params(2),
    )(src)


def _heads_merge(name, src):
    n_heads, rows, _ = src.shape
    tq = _tile(rows, HEAD_ROW_TILE, SUBLANES)

    def body(x_ref, o_ref):
        for h in range(n_heads):
            o_ref[:, h * LANES:(h + 1) * LANES] = x_ref[h].astype(BF16)

    return pl.pallas_call(
        body, name=name, grid=(rows // tq,),
        in_specs=[pl.BlockSpec((n_heads, tq, LANES), lambda i: (0, i, 0))],
        out_specs=pl.BlockSpec((tq, n_heads * LANES), lambda i: (i, 0)),
        out_shape=jax.ShapeDtypeStruct((rows, n_heads * LANES), BF16), compiler_params=_params(1),
    )(src)


def _attn_fwd(q, k, v, q_per_kv, comm=None):
    nq, l, _ = q.shape
    s_len = k.shape[1]
    tq = _tile(l, ROW_TILE, SUBLANES)
    scale = LANES ** -0.5
    kv = pl.BlockSpec((None, s_len, LANES), lambda h, i: (h // q_per_kv, 0, 0))

    def body(q_ref, k_ref, v_ref, o_ref):
        s = lax.dot_general(q_ref[...], k_ref[...], NT, preferred_element_type=F32) * scale
        p = jnp.exp(s - jnp.max(s, axis=-1, keepdims=True))
        den = jnp.sum(p, axis=-1, keepdims=True)
        o = jnp.dot(p.astype(BF16), v_ref[...], preferred_element_type=F32)
        o_ref[...] = (o / den).astype(BF16)

    (o,), cres = _host_call(
        body, name="attn_fwd", grid=(nq, l // tq), operands=[q, k, v],
        in_specs=[pl.BlockSpec((None, tq, LANES), lambda h, i: (h, i, 0)), kv, kv],
        out_shape=[jax.ShapeDtypeStruct((l, nq * LANES), BF16)],
        out_specs=[pl.BlockSpec((tq, LANES), lambda h, i: (i, h))], comm=comm)
    return o, cres


def _attn_bwd(q, k, v, do, q_per_kv, comm=None):
    nq, l, _ = q.shape
    nkv, s_len, _ = k.shape
    tq = _tile(l, ROW_TILE, SUBLANES)
    scale = LANES ** -0.5
    kv = pl.BlockSpec((None, s_len, LANES), lambda g, r, i: (g, 0, 0))
    qs = pl.BlockSpec((None, tq, LANES), lambda g, r, i: (g * q_per_kv + r, i, 0))

    def body(q_ref, k_ref, v_ref, do_ref, dq_ref, dk_ref, dv_ref):
        qv, kvv, vv, dov = q_ref[...], k_ref[...], v_ref[...], do_ref[...]
        st = lax.dot_general(kvv, qv, NT, preferred_element_type=F32) * scale
        e = jnp.exp(st - jnp.max(st, axis=0, keepdims=True))
        pt = e / jnp.sum(e, axis=0, keepdims=True)
        dpt = lax.dot_general(vv, dov, NT, preferred_element_type=F32)
        delta = jnp.sum(pt * dpt, axis=0, keepdims=True)
        dst = (pt * (dpt - delta) * scale).astype(BF16)
        ptb = pt.astype(BF16)
        dq_ref[...] = lax.dot_general(dst, kvv, TN, preferred_element_type=F32)
        dk_new = jnp.dot(dst, qv, preferred_element_type=F32)
        dv_new = jnp.dot(ptb, dov, preferred_element_type=F32)
        first = jnp.logical_and(pl.program_id(1) == 0, pl.program_id(2) == 0)

        @pl.when(first)
        def _():
            dk_ref[...] = dk_new
            dv_ref[...] = dv_new

        @pl.when(jnp.logical_not(first))
        def _():
            dk_ref[...] += dk_new
            dv_ref[...] += dv_new

    (dq, dk, dv), cres = _host_call(
        body, name="attn_bwd", grid=(nkv, q_per_kv, l // tq), operands=[q, k, v, do],
        in_specs=[qs, kv, kv, pl.BlockSpec((tq, LANES), lambda g, r, i: (i, g * q_per_kv + r))],
        out_specs=[qs, kv, kv],
        out_shape=[jax.ShapeDtypeStruct((nq, l, LANES), F32), jax.ShapeDtypeStruct((nkv, s_len, LANES), F32),
                   jax.ShapeDtypeStruct((nkv, s_len, LANES), F32)], comm=comm)
    return dq, dk, dv, cres


def _zoh(a_re, a_im, log_dt):
    dt = jnp.exp(log_dt)[:, None]
    mag = jnp.exp(a_re * dt)
    lb_re = mag * jnp.cos(a_im * dt)
    lb_im = mag * jnp.sin(a_im * dt)
    den = a_re * a_re + a_im * a_im
    coef_re = ((lb_re - 1.0) * a_re + lb_im * a_im) / den
    coef_im = (lb_im * a_re - (lb_re - 1.0) * a_im) / den
    return lb_re, lb_im, coef_re, coef_im


def _ssm_discretize(a_re, a_im, log_dt, b_re, b_im):
    lb_re, lb_im, cr, ci = _zoh(a_re, a_im, log_dt)
    bt_re = cr[..., None] * b_re - ci[..., None] * b_im
    bt_im = cr[..., None] * b_im + ci[..., None] * b_re
    return lb_re, lb_im, bt_re, bt_im


def _block_diag(m):
    g, a, b = m.shape
    m = m.reshape(g // SLAB_GROUPS, SLAB_GROUPS, a, b)
    eye = jnp.eye(SLAB_GROUPS, dtype=m.dtype)
    return jnp.einsum("sgab,gh->sgahb", m, eye).reshape(g // SLAB_GROUPS, SLAB_GROUPS * a, SLAB_GROUPS * b)


def _block_diag_extract(m, a, b):
    ns = m.shape[0]
    m = m.reshape(ns, SLAB_GROUPS, a, SLAB_GROUPS, b)
    eye = jnp.eye(SLAB_GROUPS, dtype=m.dtype)
    return jnp.einsum("sgahb,gh->sgab", m, eye).reshape(ns * SLAB_GROUPS, a, b)


def _tap_weights(base_re, base_im, pw_re, pw_im):
    parts_re, parts_im = [], []
    for tau in range(SCAN_TAPS):
        pr, pi = pw_re[tau][:, None, :], pw_im[tau][:, None, :]
        parts_re.append(pr * base_re - pi * base_im)
        parts_im.append(pr * base_im + pi * base_re)
    return jnp.concatenate([jnp.concatenate(parts_re, axis=1), jnp.concatenate(parts_im, axis=1)], axis=-1)


def _carry_table(pw_re, pw_im, descending):
    order = [SCAN_TAPS - r for r in range(SCAN_TAPS)] if descending else [r + 1 for r in range(SCAN_TAPS)]
    re = jnp.stack([pw_re[k] for k in order], axis=1)
    im = jnp.stack([pw_im[k] for k in order], axis=1)
    return jnp.concatenate([re, im], axis=-1)


def _scan_chunk(x, w_ref, tab_ref, s_ref, carry_ref, descending, t_rows, sw):
    row8 = lax.broadcasted_iota(jnp.int32, x.shape, 0) % SCAN_TAPS
    pieces = [x.astype(BF16)]
    for tau in range(1, SCAN_TAPS):
        if descending:
            sh = jnp.where(row8 <= SCAN_TAPS - 1 - tau, pltpu.roll(x, t_rows - tau, 0), 0.0)
        else:
            sh = jnp.where(row8 >= tau, pltpu.roll(x, tau, 0), 0.0)
        pieces.append(sh.astype(BF16))
    xa = jnp.concatenate(pieces, axis=1)
    s_ref[...] = jnp.dot(xa, w_ref[...], preferred_element_type=F32)
    tab = tab_ref[...]
    t_re, t_im = tab[:, :sw], tab[:, sw:]
    nb = t_rows // SCAN_TAPS
    edge = 0 if descending else SCAN_TAPS - 1

    def step(b, carry):
        h_re, h_im = carry
        r0 = pl.multiple_of(((nb - 1 - b) if descending else b) * SCAN_TAPS, SCAN_TAPS)
        x_re = s_ref[pl.ds(r0, SCAN_TAPS), :sw] + t_re * h_re - t_im * h_im
        x_im = s_ref[pl.ds(r0, SCAN_TAPS), sw:] + t_re * h_im + t_im * h_re
        s_ref[pl.ds(r0, SCAN_TAPS), :sw] = x_re
        s_ref[pl.ds(r0, SCAN_TAPS), sw:] = x_im
        return x_re[edge:edge + 1, :], x_im[edge:edge + 1, :]

    h_re, h_im = lax.fori_loop(0, nb, step, (carry_ref[0:1, :sw], carry_ref[0:1, sw:]))
    carry_ref[0:1, :sw] = h_re
    carry_ref[0:1, sw:] = h_im


def _ssm_fwd(name, u_src, u_shard, waug, tab, cd, descending, chunk_of, t_rows, rows, comm=None):
    ns, kdim, sw2 = waug.shape
    sw = sw2 // 2
    width = ns * LANES
    nchunks = rows // t_rows

    def body(u_ref, w_ref, tab_ref, cd_ref, y_ref, h_ref, s_ref, carry_ref):
        @pl.when(pl.program_id(1) == 0)
        def _():
            carry_ref[...] = jnp.zeros_like(carry_ref)

        _scan_chunk(u_ref[...], w_ref, tab_ref, s_ref, carry_ref, descending, t_rows, sw)
        hb = s_ref[...].astype(BF16)
        h_ref[...] = hb
        y_ref[...] = jnp.dot(hb, cd_ref[...], preferred_element_type=F32)

    (y, h), cres = _host_call(
        body, name=name, grid=(ns, nchunks), operands=[u_src, waug, tab, cd],
        in_specs=[pl.BlockSpec((None, t_rows, LANES), lambda s, i: (u_shard, chunk_of(i), s)),
                  pl.BlockSpec((None, kdim, sw2), lambda s, i: (s, 0, 0)),
                  pl.BlockSpec((None, SCAN_TAPS, sw2), lambda s, i: (s, 0, 0)),
                  pl.BlockSpec((None, sw2, LANES), lambda s, i: (s, 0, 0))],
        out_specs=[pl.BlockSpec((t_rows, LANES), lambda s, i: (chunk_of(i), s)),
                   pl.BlockSpec((None, t_rows, sw2), lambda s, i: (s, chunk_of(i), 0))],
        out_shape=[jax.ShapeDtypeStruct((rows, width), F32), jax.ShapeDtypeStruct((ns, rows, sw2), BF16)],
        scratch_shapes=[pltpu.VMEM((t_rows, sw2), F32), pltpu.VMEM((SUBLANES, sw2), F32)], comm=comm)
    return y, h, cres


def _ssm_bwd(name, dy, u_src, u_shard, states, caug, tab, bdt, descending, chunk_of, t_rows, rows, comm=None):
    ns, kdim, sw2 = caug.shape
    sw = sw2 // 2
    width = ns * LANES
    nchunks = rows // t_rows

    def body(dy_ref, u_ref, h_ref, w_ref, tab_ref, bdt_ref, du_ref, dbd_ref, dcd_ref, dlam_ref,
             s_ref, carry_ref, gsave_ref):
        first = pl.program_id(1) == 0

        @pl.when(first)
        def _():
            carry_ref[...] = jnp.zeros_like(carry_ref)
            gsave_ref[...] = jnp.zeros_like(gsave_ref)

        dyv = dy_ref[...]
        _scan_chunk(dyv, w_ref, tab_ref, s_ref, carry_ref, descending, t_rows, sw)
        g = s_ref[...]
        gb = g.astype(BF16)
        du_ref[...] = jnp.dot(gb, bdt_ref[...], preferred_element_type=F32)
        dbd = lax.dot_general(u_ref[...].astype(BF16), gb, TN, preferred_element_type=F32)
        hb = h_ref[...]
        dcd = lax.dot_general(hb, dyv.astype(BF16), TN, preferred_element_type=F32)
        hf = hb.astype(F32)
        rowid = lax.broadcasted_iota(jnp.int32, hf.shape, 0)
        if descending:
            hp = jnp.where(rowid == 0, 0.0, pltpu.roll(hf, 1, 0))
            h_edge, g_edge = hf[t_rows - 1:t_rows, :], g[0:1, :]
        else:
            hp = jnp.where(rowid == t_rows - 1, 0.0, pltpu.roll(hf, t_rows - 1, 0))
            h_edge, g_edge = hf[0:1, :], g[t_rows - 1:t_rows, :]
        g_re, g_im, hp_re, hp_im = g[:, :sw], g[:, sw:], hp[:, :sw], hp[:, sw:]
        gs = gsave_ref[0:1, :]
        gs_re, gs_im, he_re, he_im = gs[:, :sw], gs[:, sw:], h_edge[:, :sw], h_edge[:, sw:]
        dl_re = _colsum(g_re * hp_re + g_im * hp_im) + gs_re * he_re + gs_im * he_im
        dl_im = _colsum(g_im * hp_re - g_re * hp_im) + gs_im * he_re - gs_re * he_im
        gsave_ref[0:1, :] = g_edge

        @pl.when(first)
        def _():
            dbd_ref[...] = dbd
            dcd_ref[...] = dcd
            dlam_ref[:, :sw] = dl_re
            dlam_ref[:, sw:] = dl_im

        @pl.when(jnp.logical_not(first))
        def _():
            dbd_ref[...] += dbd
            dcd_ref[...] += dcd
            dlam_ref[:, :sw] += dl_re
            dlam_ref[:, sw:] += dl_im

    (du, dbd, dcd, dlam), cres = _host_call(
        body, name=name, grid=(ns, nchunks), operands=[dy, u_src, states, caug, tab, bdt],
        in_specs=[pl.BlockSpec((t_rows, LANES), lambda s, i: (chunk_of(i), s)),
                  pl.BlockSpec((None, t_rows, LANES), lambda s, i: (u_shard, chunk_of(i), s)),
                  pl.BlockSpec((None, t_rows, sw2), lambda s, i: (s, chunk_of(i), 0)),
                  pl.BlockSpec((None, kdim, sw2), lambda s, i: (s, 0, 0)),
                  pl.BlockSpec((None, SCAN_TAPS, sw2), lambda s, i: (s, 0, 0)),
                  pl.BlockSpec((None, sw2, LANES), lambda s, i: (s, 0, 0))],
        out_specs=[pl.BlockSpec((t_rows, LANES), lambda s, i: (chunk_of(i), s)),
                   pl.BlockSpec((None, LANES, sw2), lambda s, i: (s, 0, 0)),
                   pl.BlockSpec((None, sw2, LANES), lambda s, i: (s, 0, 0)),
                   pl.BlockSpec((None, 1, sw2), lambda s, i: (s, 0, 0))],
        out_shape=[jax.ShapeDtypeStruct((rows, width), F32), jax.ShapeDtypeStruct((ns, LANES, sw2), F32),
                   jax.ShapeDtypeStruct((ns, sw2, LANES), F32), jax.ShapeDtypeStruct((ns, 1, sw2), F32)],
        scratch_shapes=[pltpu.VMEM((t_rows, sw2), F32), pltpu.VMEM((SUBLANES, sw2), F32),
                        pltpu.VMEM((SUBLANES, sw2), F32)], comm=comm)
    return du, dbd, dcd, dlam, cres


def _mod_fwd(cs, w_mod, b_cols):
    d, width = w_mod.shape
    tn = _tile(width, 768, LANES)

    def epilogue(accs, ins, outs, pids):
        outs[0][...] = accs[0] + ins[2][...]

    return _matmul(
        "mod_fwd", (width // tn,), [cs, w_mod, b_cols],
        [pl.BlockSpec((16, d), lambda n: (0, 0)), pl.BlockSpec((d, tn), lambda n: (0, n)),
         pl.BlockSpec((1, tn), lambda n: (0, n))],
        [(0, 1, 0, NN)], [jax.ShapeDtypeStruct((16, width), F32)], [pl.BlockSpec((16, tn), lambda n: (0, n))],
        epilogue, prologue={0: lambda v: v * _sigmoid(v)})[0][0]


def _mod_bwd_adam(cs, dmod_cols, w, m, v, comm=None):
    d, width = w.shape
    tn = _tile(width, LANES, LANES)
    col = pl.BlockSpec((d, tn), lambda n: (0, n))

    def body(cs_ref, dm_ref, w_ref, m_ref, v_ref, g_ref, dl_ref, nm_ref, nv_ref, ds_ref):
        n = pl.program_id(0)
        lat = dm_ref[pl.ds(0, N_DEV, stride=SUBLANES), :]
        ctx = jnp.sum(dm_ref[pl.ds(1, N_DEV, stride=SUBLANES), :], axis=0, keepdims=True)
        row = lax.broadcasted_iota(jnp.int32, lat.shape, 0)
        dm = jnp.concatenate([lat, jnp.where(row == 0, ctx, 0.0)], axis=0).astype(BF16)
        c = cs_ref[...]
        sc = (c * _sigmoid(c)).astype(BF16)
        wv = w_ref[...]
        g = lax.dot_general(sc, dm, TN, preferred_element_type=F32)
        delta, m2, v2 = _adamw(wv, g, m_ref[...], v_ref[...])
        g_ref[...] = g
        dl_ref[...] = delta
        nm_ref[...] = m2
        nv_ref[...] = v2
        part = lax.dot_general(dm, wv.astype(BF16), NT, preferred_element_type=F32)

        @pl.when(n == 0)
        def _():
            ds_ref[...] = part

        @pl.when(n > 0)
        def _():
            ds_ref[...] += part

    shard = jax.ShapeDtypeStruct((d, width), F32)
    return _host_call(
        body, name="mod_bwd_adam", grid=(width // tn,), operands=[cs, dmod_cols, w, m, v],
        in_specs=[pl.BlockSpec((16, d), lambda n: (0, 0)), pl.BlockSpec((N_DEV * SUBLANES, tn), lambda n: (0, n)),
                  col, col, col],
        out_specs=[col, col, col, col, pl.BlockSpec((16, d), lambda n: (0, 0))],
        out_shape=[shard, shard, shard, shard, jax.ShapeDtypeStruct((16, d), F32)], comm=comm)


def _pair_sum(name, grads, got, core):
    _, rows, cols = grads.shape
    tr = _tile(rows, max(PACKED_SUBLANES, ADAM_BLOCK_BYTES // (cols * 6 * N_CHIPS)), PACKED_SUBLANES)
    blk = pl.BlockSpec((N_CHIPS, tr, cols), lambda i, cc: (0, i, 0))

    def body(core_ref, a_ref, b_ref, o_ref):
        o_ref[...] = (a_ref[...].astype(F32) + b_ref[...].astype(F32)).astype(BF16)

    grid_spec = pltpu.PrefetchScalarGridSpec(
        num_scalar_prefetch=1, grid=(rows // tr,),
        in_specs=[pl.BlockSpec((N_CHIPS, None, tr, cols), lambda i, cc: (0, cc[0], i, 0)), blk], out_specs=blk)
    return pl.pallas_call(
        body, name=name, grid_spec=grid_spec, out_shape=jax.ShapeDtypeStruct((N_CHIPS, rows, cols), BF16),
        compiler_params=_params(1))(core, grads.reshape(N_CHIPS, 2, rows, cols), got)


def _owner_adam(name, pairs, landed, chip, w, m, v):
    rows, cols = w.shape
    tr = _tile(rows, max(PACKED_SUBLANES, ADAM_BLOCK_BYTES // (cols * 40)), PACKED_SUBLANES)
    blk = pl.BlockSpec((tr, cols), lambda i, ch: (i, 0))

    def body(chip_ref, p_ref, l_ref, w_ref, m_ref, v_ref, g_ref, dl_ref, nm_ref, nv_ref):
        g = p_ref[...].astype(F32)
        for s in range(N_CHIPS - 1):
            g = g + l_ref[s].astype(F32)
        delta, m2, v2 = _adamw(w_ref[...], g, m_ref[...], v_ref[...])
        g_ref[...] = g
        dl_ref[...] = delta
        nm_ref[...] = m2
        nv_ref[...] = v2

    out = jax.ShapeDtypeStruct((rows, cols), F32)
    grid_spec = pltpu.PrefetchScalarGridSpec(
        num_scalar_prefetch=1, grid=(rows // tr,),
        in_specs=[pl.BlockSpec((None, tr, cols), lambda i, ch: (ch[0], i, 0)),
                  pl.BlockSpec((N_CHIPS - 1, tr, cols), lambda i, ch: (0, i, 0)), blk, blk, blk],
        out_specs=[blk, blk, blk, blk])
    return pl.pallas_call(body, name=name, grid_spec=grid_spec, out_shape=[out, out, out, out],
                          compiler_params=_params(1))(chip, pairs, landed, w, m, v)


def _sum_adam(name, parts, w, m, v):
    rows, cols = w.shape
    n_parts = parts.shape[0]
    align = PACKED_SUBLANES if parts.dtype == BF16 else SUBLANES
    tr = _tile(rows, max(align, ADAM_BLOCK_BYTES // (cols * 44)), align)
    blk = pl.BlockSpec((tr, cols), lambda i: (i, 0))

    def body(p_ref, w_ref, m_ref, v_ref, g_ref, dl_ref, nm_ref, nv_ref):
        g = p_ref[0].astype(F32)
        for s in range(1, n_parts):
            g = g + p_ref[s].astype(F32)
        delta, m2, v2 = _adamw(w_ref[...], g, m_ref[...], v_ref[...])
        g_ref[...] = g
        dl_ref[...] = delta
        nm_ref[...] = m2
        nv_ref[...] = v2

    out = jax.ShapeDtypeStruct((rows, cols), F32)
    return pl.pallas_call(
        body, name=name, grid=(rows // tr,),
        in_specs=[pl.BlockSpec((n_parts, tr, cols), lambda i: (0, i, 0)), blk, blk, blk],
        out_specs=[blk, blk, blk, blk], out_shape=[out, out, out, out], compiler_params=_params(1),
    )(parts, w, m, v)


def _bias_adam(dmod_all, w, m, v):
    width = w.shape[-1]
    tn = _tile(width, 2048, LANES)
    blk = pl.BlockSpec((1, tn), lambda n: (0, n))

    def body(p_ref, w_ref, m_ref, v_ref, g_ref, dl_ref, nm_ref, nv_ref):
        g = jnp.sum(p_ref[...], axis=0, keepdims=True)
        delta, m2, v2 = _adamw(w_ref[...], g, m_ref[...], v_ref[...])
        g_ref[...] = g
        dl_ref[...] = delta
        nm_ref[...] = m2
        nv_ref[...] = v2

    out = jax.ShapeDtypeStruct((1, width), F32)
    return pl.pallas_call(
        body, name="bias_adam", grid=(width // tn,),
        in_specs=[pl.BlockSpec((dmod_all.shape[0], tn), lambda n: (0, n)), blk, blk, blk],
        out_specs=[blk, blk, blk, blk], out_shape=[out, out, out, out], compiler_params=_params(1),
    )(dmod_all, w, m, v)


def _pack(arrays, total_rows):
    flat = []
    for a in arrays:
        a = a.reshape(-1).astype(F32)
        flat.append(jnp.pad(a, (0, (-a.shape[0]) % LANES)))
    flat = jnp.concatenate(flat).reshape(-1, LANES)
    return jnp.pad(flat, ((0, total_rows - flat.shape[0]), (0, 0)))


def _unpack(packed, shapes):
    out, row = [], 0
    for shp in shapes:
        size = math.prod(shp)
        nrows = -(-size // LANES)
        out.append(packed[row:row + nrows].reshape(-1)[:size].reshape(shp))
        row += nrows
    return out


def kernel(x, c, ctx, c_ctx, w_mod, b_mod, norm_g, w_ffn1_gate, w_ffn1_up, w_ffn1_down, w_in, q_norm_g, k_norm_g, ssm_a_re, ssm_a_im, ssm_log_dt, ssm_b_re, ssm_b_im, ssm_c_re, ssm_c_im, ssm_d, w_glu, b_glu, w_br_attn, w_br_ssm, w_out, w_ffn2_gate, w_ffn2_up, w_ffn2_down, loss_target, m_c_ctx, m_w_mod, m_b_mod, m_norm_g, m_w_ffn1_gate, m_w_ffn1_up, m_w_ffn1_down, m_w_in, m_q_norm_g, m_k_norm_g, m_ssm_a_re, m_ssm_a_im, m_ssm_log_dt, m_ssm_b_re, m_ssm_b_im, m_ssm_c_re, m_ssm_c_im, m_ssm_d, m_w_glu, m_b_glu, m_w_br_attn, m_w_br_ssm, m_w_out, m_w_ffn2_gate, m_w_ffn2_up, m_w_ffn2_down, v_c_ctx, v_w_mod, v_b_mod, v_norm_g, v_w_ffn1_gate, v_w_ffn1_up, v_w_ffn1_down, v_w_in, v_q_norm_g, v_k_norm_g, v_ssm_a_re, v_ssm_a_im, v_ssm_log_dt, v_ssm_b_re, v_ssm_b_im, v_ssm_c_re, v_ssm_c_im, v_ssm_d, v_w_glu, v_b_glu, v_w_br_attn, v_w_br_ssm, v_w_out, v_w_ffn2_gate, v_w_ffn2_up, v_w_ffn2_down):
    _, L, D = x.shape
    Lc = ctx.shape[1]
    R = L + Lc
    MODW = w_mod.shape[-1]
    INS = w_in.shape[-1]
    KVW = INS // 2
    NQ = D // LANES
    NKV = KVW // LANES
    QPK = NQ // NKV
    HBQ = INS // LANES
    G, P, E = ssm_b_re.shape[2:]
    W = G * E
    SW = SLAB_GROUPS * P
    assert E * SLAB_GROUPS == LANES and W == INS and NQ * LANES == D and Lc <= L
    me = 4 * lax.axis_index("x") + 2 * lax.axis_index("y") + lax.axis_index("c")

    x2, ctx2, tgt = x[0], ctx[0], loss_target[0]
    xc0 = jnp.concatenate([x2, ctx2], axis=0)

    def bf(w):
        return w[0].astype(BF16)

    def widen(a):
        return jnp.pad(a[0], ((0, 0), (0, D - a.shape[-1])))

    def at_row(a, r, total):
        return jnp.pad(a, ((r, total - r - a.shape[0]), (0, 0)))

    pack_in = (at_row(c, 0, 16) + at_row(widen(norm_g), 1, 16) + at_row(widen(m_norm_g), 4, 16)
               + at_row(widen(v_norm_g), 7, 16))
    (g_in,) = _exchange_only("ag_inputs", _Gather([pack_in]))
    c_all = g_in[:, 0, :]
    dn = D // N_DEV

    def full_norm(k):
        return jnp.transpose(g_in[:, k:k + 3, :dn], (1, 0, 2)).reshape(3, D)

    ng_full, m_ng_full, v_ng_full = full_norm(1), full_norm(4), full_norm(7)
    cs = at_row(c_all, 0, 16) + at_row(c_ctx[None, :], 8, 16)

    b_cols = lax.dynamic_slice_in_dim(b_mod, me * MODW, MODW, axis=1)
    mod_blk = _mod_fwd(cs, w_mod[0], b_cols)
    (mod_g,) = _exchange_only("ag_mod", _Gather([mod_blk]))
    mod_lat = lax.dynamic_index_in_dim(mod_g, me, axis=1, keepdims=False).reshape(-1)
    mod_ctx = mod_g[:, 8, :].reshape(-1)
    sh1, sc1, g1, sh2, sc2, g2, sh3, sc3, g3 = [mod_lat[k * D:(k + 1) * D] for k in range(9)]
    mc0, mc1, mc2, mc3, mc4 = [mod_ctx[k * D:(k + 1) * D] for k in range(5)]

    def grp(a, b):
        return jnp.stack([a, b])[:, None, :]

    gam = [ng_full[k][None, :] for k in range(3)]

    wg1, wu1 = _exchange_only("ag_ffn1_gate_up", _Gather([bf(w_ffn1_gate), bf(w_ffn1_up)]))
    gate1 = grp(g1, mc2)
    h1 = _norm_mod_fwd("nm1_fwd", xc0, gam[0], grp(sh1, mc0), grp(sc1, mc1), L, Lc)
    a1, b1, s1, (wd1,) = _ffn_up("ffn1", h1, wg1, wu1, comm=_Gather([bf(w_ffn1_down)]))
    f1, xc1, (win,) = _ffn_down("ffn1", s1, wd1, xc0, gate1[:, 0, :], L, comm=_Gather([bf(w_in)]))

    h2 = _norm_mod_fwd("nm2_fwd", xc1, gam[1], grp(sh2, mc3), grp(sc2, mc4), L, Lc)
    tm = _tile(R, MM_TILE, LANES)
    tml = _tile(L, MM_TILE, LANES)

    (p01,), _ = _matmul(
        "in_proj_kvu", (2, R // tm), [h2, win],
        [pl.BlockSpec((tm, D), lambda j, i: (i, 0)), pl.BlockSpec((None, D, INS), lambda j, i: (j, 0, 0))],
        [(0, 1, 0, NN)], [jax.ShapeDtypeStruct((2, R, INS), F32)],
        [pl.BlockSpec((None, tm, INS), lambda j, i: (j, i, 0))], _store_all)
    (p27,), (wglu, wbra) = _matmul(
        "in_proj_qg", (6, L // tml), [h2, win],
        [pl.BlockSpec((tml, D), lambda j, i: (i, 0)), pl.BlockSpec((None, D, INS), lambda j, i: (j + 2, 0, 0))],
        [(0, 1, 0, NN)], [jax.ShapeDtypeStruct((6, L, INS), F32)],
        [pl.BlockSpec((None, tml, INS), lambda j, i: (j, i, 0))], _store_all,
        comm=_Gather([bf(w_glu), bf(w_br_attn)]))
    wglu2 = wglu.reshape(W, W)
    wbra2 = wbra.reshape(D, D)

    half = LANES // 4
    inv_freq = ROPE_THETA ** (-jnp.arange(half, dtype=F32) / half)
    pos = jnp.arange(L)
    ang_r = (pos // GRID_W).astype(F32)[:, None] * inv_freq
    ang_c = (pos % GRID_W).astype(F32)[:, None] * inv_freq
    cos_l = jnp.concatenate([jnp.cos(ang_r)] * 2 + [jnp.cos(ang_c)] * 2, axis=1)
    sin_l = jnp.concatenate([-jnp.sin(ang_r), jnp.sin(ang_r), -jnp.sin(ang_c), jnp.sin(ang_c)], axis=1)
    cos_all = jnp.concatenate([cos_l, jnp.ones((Lc, LANES), F32)], axis=0)
    sin_all = jnp.concatenate([sin_l, jnp.zeros((Lc, LANES), F32)], axis=0)

    q_rot = _qk_prep("q_prep", p27, 0, HBQ, NQ, L, q_norm_g, cos_l, sin_l)
    k_rot = _qk_prep("k_prep", p01, 0, NKV, NKV, R, k_norm_g, cos_all, sin_all)
    v_hd = _heads_cast("v_heads", p01, 1, NKV, NKV, R)
    attn, (wbrs, wout, wg2) = _attn_fwd(
        q_rot, k_rot, v_hd, QPK, comm=_Gather([bf(w_br_ssm), bf(w_out), bf(w_ffn2_gate)]))
    wout2 = wout.reshape(D, D)

    t_rows = _tile(math.gcd(L, Lc), ROW_TILE, SUBLANES)
    nl, ncx = L // t_rows, Lc // t_rows
    nch = nl + ncx
    ssm = []
    for d_ in range(2):
        lb_re, lb_im, bt_re, bt_im = _ssm_discretize(
            ssm_a_re[0, d_], ssm_a_im[0, d_], ssm_log_dt[0, d_], ssm_b_re[0, d_], ssm_b_im[0, d_])
        ns = G // SLAB_GROUPS
        lam_re, lam_im = lb_re.reshape(ns, SW), lb_im.reshape(ns, SW)
        pw_re, pw_im = [jnp.ones_like(lam_re)], [jnp.zeros_like(lam_im)]
        for _ in range(SCAN_TAPS):
            pw_re, pw_im = (pw_re + [pw_re[-1] * lam_re - pw_im[-1] * lam_im],
                            pw_im + [pw_re[-1] * lam_im + pw_im[-1] * lam_re])
        cj_im = [-p for p in pw_im]
        bd_re = _block_diag(jnp.transpose(bt_re, (0, 2, 1)))
        bd_im = _block_diag(jnp.transpose(bt_im, (0, 2, 1)))
        ct_re = _block_diag(ssm_c_re[0, d_])
        ct_im = _block_diag(-ssm_c_im[0, d_])
        fwd_desc = d_ == 1
        ssm.append(dict(
            waug=_tap_weights(bd_re, bd_im, pw_re, pw_im).astype(BF16),
            tab=_carry_table(pw_re, pw_im, fwd_desc),
            cd=jnp.concatenate([jnp.transpose(ct_re, (0, 2, 1)), jnp.transpose(ct_im, (0, 2, 1))], axis=1).astype(BF16),
            caug=_tap_weights(ct_re, ct_im, pw_re, cj_im).astype(BF16),
            tabc=_carry_table(pw_re, cj_im, not fwd_desc),
            bdt=jnp.concatenate([jnp.transpose(bd_re, (0, 2, 1)), jnp.transpose(bd_im, (0, 2, 1))], axis=1).astype(BF16),
            fwd_desc=fwd_desc))
    order = [lambda i: (i + nl) % nch, lambda i: nch - 1 - i]
    order_adj = [lambda i: (nch - 1 - i + nl) % nch, lambda i: i]
    y0, st0, (wu2,) = _ssm_fwd("ssm_fwd0", p01, 1, ssm[0]["waug"], ssm[0]["tab"], ssm[0]["cd"],
                               ssm[0]["fwd_desc"], order[0], t_rows, R, comm=_Gather([bf(w_ffn2_up)]))
    y1, st1, (wd2,) = _ssm_fwd("ssm_fwd1", p01, 1, ssm[1]["waug"], ssm[1]["tab"], ssm[1]["cd"],
                               ssm[1]["fwd_desc"], order[1], t_rows, R, comm=_Gather([bf(w_ffn2_down)]))
    states = [st0, st1]

    tr = _row_tile(L, 0)
    rowW = pl.BlockSpec((tr, W), lambda i: (i, 0))
    vecW = pl.BlockSpec((1, W), lambda i: (0, 0))
    u_lat = pl.BlockSpec((None, tr, W), lambda i: (1, i, 0))

    def ssm_post(i, u, ya, yb, dvec):
        sv = dvec * u + ya + yb
        return [sv, _gelu(sv)], []

    (ssm_out, yg), _, _ = _rowwise(
        "ssm_post", L // tr, [p01, y0, y1, ssm_d], [u_lat, rowW, rowW, vecW],
        [jax.ShapeDtypeStruct((L, W), F32), jax.ShapeDtypeStruct((L, W), BF16)], [rowW, rowW], [], ssm_post)

    tnw = _tile(W, MM_TILE, LANES)

    def glu_epilogue(accs, ins, outs, pids):
        z = accs[0] + ins[3][...]
        outs[0][...] = z
        outs[1][...] = (_gelu(ins[2][...]) * _sigmoid(z)).astype(BF16)

    (z_glu, y2), _ = _matmul(
        "glu", (L // tml, W // tnw), [yg, wglu2, ssm_out, b_glu],
        [pl.BlockSpec((tml, W), lambda i, n: (i, 0)), pl.BlockSpec((W, tnw), lambda i, n: (0, n)),
         pl.BlockSpec((tml, tnw), lambda i, n: (i, n)), pl.BlockSpec((1, tnw), lambda i, n: (0, n))],
        [(0, 1, 0, NN)], [jax.ShapeDtypeStruct((L, W), F32), jax.ShapeDtypeStruct((L, W), BF16)],
        [pl.BlockSpec((tml, tnw), lambda i, n: (i, n))] * 2, glu_epilogue)

    tnd = _tile(D, MM_TILE, LANES)
    out_ld = pl.BlockSpec((tml, tnd), lambda i, n: (i, n))
    (br_a,), _ = _matmul(
        "br_attn", (L // tml, D // tnd), [attn, wbra2],
        [pl.BlockSpec((tml, D), lambda i, n: (i, 0)), pl.BlockSpec((D, tnd), lambda i, n: (0, n))],
        [(0, 1, 0, NN)], [jax.ShapeDtypeStruct((L, D), F32)], [out_ld], _store_all)

    cb = wbrs.shape[-1]
    gpb = INS // cb

    def gate_spec(first_shard):
        return pl.BlockSpec((None, tml, cb), lambda i, j: (first_shard + j // gpb, i, j % gpb))

    def merge_epilogue(accs, ins, outs, pids):
        br = accs[0]
        outs[0][...] = br
        outs[1][...] = (_sigmoid(ins[2][...]) * ins[4][...] + _sigmoid(ins[3][...]) * br).astype(BF16)

    col_blk = pl.BlockSpec((tml, cb), lambda i, j: (i, j))
    (br_s, merged), _ = _matmul(
        "br_ssm_merge", (L // tml, N_DEV), [y2, wbrs, p27, p27, br_a],
        [pl.BlockSpec((tml, W), lambda i, j: (i, 0)), pl.BlockSpec((None, W, cb), lambda i, j: (j, 0, 0)),
         gate_spec(2), gate_spec(4), col_blk],
        [(0, 1, 0, NN)], [jax.ShapeDtypeStruct((L, D), F32), jax.ShapeDtypeStruct((L, D), BF16)],
        [col_blk, col_blk], merge_epilogue)

    def out_epilogue(accs, ins, outs, pids):
        outs[0][...] = accs[0]
        outs[1][...] = ins[2][...] + ins[3][...] * accs[0]

    g2row = g2[None, :]
    (mix, x2_), _ = _matmul(
        "out_proj", (L // tml, D // tnd), [merged, wout2, xc1, g2row],
        [pl.BlockSpec((tml, D), lambda i, n: (i, 0)), pl.BlockSpec((D, tnd), lambda i, n: (0, n)), out_ld,
         pl.BlockSpec((1, tnd), lambda i, n: (0, n))],
        [(0, 1, 0, NN)], [jax.ShapeDtypeStruct((L, D), F32)] * 2, [out_ld, out_ld], out_epilogue)

    gate3 = grp(g3, g3)
    h3 = _norm_mod_fwd("nm3_fwd", x2_, gam[2], grp(sh3, sh3), grp(sc3, sc3), L, 0)
    a3, b3, s3, _ = _ffn_up("ffn2", h3, wg2, wu2)
    f3, x3, _ = _ffn_down("ffn2", s3, wd2, x2_, gate3[:, 0, :], L)

    trd = _row_tile(L, 0)
    rowD = pl.BlockSpec((trd, D), lambda i: (i, 0))

    def loss_fn(i, yv, t):
        err = yv - t
        return [err * (1.0 / D)], [_colsum(err * err)]

    (dx3,), (sq,), _ = _rowwise("loss", L // trd, [x3, tgt], [rowD, rowD],
                                [jax.ShapeDtypeStruct((L, D), F32)], [rowD], [D], loss_fn)
    loss = lax.psum(0.5 * jnp.sum(sq) / D, ("x", "y", "c"))

    core = lax.axis_index("c").astype(jnp.int32).reshape(1)
    chip = (2 * lax.axis_index("x") + lax.axis_index("y")).astype(jnp.int32).reshape(1)

    def pair_sums(tag, grads, halves):
        return [_pair_sum("pair_%s%d" % (tag, k), g_, h_, core) for k, (g_, h_) in enumerate(zip(grads, halves))]

    df3, (dg3, _) = _gate_bwd("gate3_bwd", dx3, f3, gate3, 0.5, L, 0)
    dwd2, _ = _ffn_dwd("ffn2b", s3, df3)
    da3, db3, half_wd2 = _ffn_ds("ffn2b", df3, wd2, a3, b3, comm=_SiblingSwap([dwd2]))
    (p_wd2,) = pair_sums("wd2", [dwd2], half_wd2)
    dwg2, dwu2, (l_wd2,) = _ffn_dwgu("ffn2b", h3, da3, db3, comm=_ChipExchange([p_wd2]))
    dh3, half_wgu2 = _ffn_dh("ffn2b", da3, db3, wg2, wu2, comm=_SiblingSwap([dwg2, dwu2]))
    p_wg2, p_wu2 = pair_sums("wgu2", [dwg2, dwu2], half_wgu2)
    dx2, (dsh3, dsc3, _, _, dgam3) = _norm_mod_bwd("nm3_bwd", x2_, dh3, gam[2], grp(sc3, sc3), L, 0, dres=dx3)

    dmix, (dg2, _) = _gate_bwd("gate2_bwd", dx2, mix, grp(g2, g2), 1.0, L, 0)

    def dmerged_epilogue(accs, ins, outs, pids):
        dm = accs[0]
        ga, gs = _sigmoid(ins[2][...]), _sigmoid(ins[3][...])
        outs[0][...] = (ga * dm).astype(BF16)
        outs[1][...] = (gs * dm).astype(BF16)
        outs[2][...] = (dm * ins[4][...] * ga * (1.0 - ga)).astype(BF16)
        outs[3][...] = (dm * ins[5][...] * gs * (1.0 - gs)).astype(BF16)

    dgate_spec = pl.BlockSpec((None, tml, cb), lambda i, j: (j // gpb, i, j % gpb))
    (d_br_a, d_br_s, dg_a, dg_s), _ = _matmul(
        "dmerged", (L // tml, N_DEV), [dmix, wout2, p27, p27, br_a, br_s],
        [pl.BlockSpec((tml, D), lambda i, j: (i, 0)), pl.BlockSpec((cb, D), lambda i, j: (j, 0)),
         gate_spec(2), gate_spec(4), col_blk, col_blk],
        [(0, 1, 0, NT)],
        [jax.ShapeDtypeStruct((L, D), BF16)] * 2 + [jax.ShapeDtypeStruct((2, L, INS), BF16)] * 2,
        [col_blk, col_blk, dgate_spec, dgate_spec], dmerged_epilogue)

    def wgrad(name, a_mat, b_mat, tmo, tno):
        ka, ma = a_mat.shape
        _, nb_ = b_mat.shape
        return _matmul(
            name, (ma // tmo, nb_ // tno), [a_mat, b_mat],
            [pl.BlockSpec((ka, tmo), lambda m, n: (0, m)), pl.BlockSpec((ka, tno), lambda m, n: (0, n))],
            [(0, 1, 0, TN)], [jax.ShapeDtypeStruct((ma, nb_), BF16)],
            [pl.BlockSpec((tmo, tno), lambda m, n: (m, n))], _store_all)[0][0]

    dwout = wgrad("dw_out", merged, dmix, tnd, tnd)
    dwbra = wgrad("dw_br_attn", attn, d_br_a, tnd, tnd)
    (d_attn,), _ = _matmul(
        "d_attn", (L // tml, D // tnd), [d_br_a, wbra2],
        [pl.BlockSpec((tml, D), lambda i, n: (i, 0)), pl.BlockSpec((tnd, D), lambda i, n: (n, 0))],
        [(0, 1, 0, NT)], [jax.ShapeDtypeStruct((L, D), BF16)], [out_ld], _store_all)

    (dwbrs,), _ = _matmul(
        "dw_br_ssm", (N_DEV,), [y2, d_br_s],
        [pl.BlockSpec((L, W), lambda j: (0, 0)), pl.BlockSpec((L, cb), lambda j: (0, j))],
        [(0, 1, 0, TN)], [jax.ShapeDtypeStruct((N_DEV, W, cb), BF16)],
        [pl.BlockSpec((None, W, cb), lambda j: (j, 0, 0))], _store_all)

    def dy2_epilogue(accs, ins, outs, pids):
        dy2 = accs[0]
        sg = _sigmoid(ins[2][...])
        outs[0][...] = dy2 * sg
        outs[1][...] = (dy2 * _gelu(ins[3][...]) * sg * (1.0 - sg)).astype(BF16)

    wn_blk = pl.BlockSpec((tml, tnw), lambda i, n, k: (i, n))
    (dyg1, dz), _ = _matmul(
        "d_y2", (L // tml, W // tnw, N_DEV), [d_br_s, wbrs, z_glu, ssm_out],
        [pl.BlockSpec((tml, cb), lambda i, n, k: (i, k)), pl.BlockSpec((None, tnw, cb), lambda i, n, k: (k, n, 0)),
         wn_blk, wn_blk],
        [(0, 1, 0, NT)], [jax.ShapeDtypeStruct((L, W), F32), jax.ShapeDtypeStruct((L, W), BF16)],
        [wn_blk, wn_blk], dy2_epilogue, acc_shapes=[(tml, tnw)], nk=N_DEV)

    dwglu = wgrad("dw_glu", yg, dz, tnw, tnw)
    mix_grads = [dwout.reshape(N_DEV, D // N_DEV, D), dwbra.reshape(N_DEV, D // N_DEV, D), dwbrs,
                 dwglu.reshape(N_DEV, W // N_DEV, W)]

    def dssm_epilogue(accs, ins, outs, pids):
        outs[0][...] = (accs[0] + ins[2][...]) * _gelu_grad(ins[3][...])

    wn2 = pl.BlockSpec((tml, tnw), lambda i, n: (i, n))
    (dssm,), _ = _matmul(
        "d_ssm", (L // tml, W // tnw), [dz, wglu2, dyg1, ssm_out],
        [pl.BlockSpec((tml, W), lambda i, n: (i, 0)), pl.BlockSpec((tnw, W), lambda i, n: (n, 0)), wn2, wn2],
        [(0, 1, 0, NT)], [jax.ShapeDtypeStruct((L, W), F32)], [wn2], dssm_epilogue)

    dssm_all = jnp.concatenate([dssm, jnp.zeros((Lc, W), F32)], axis=0)
    du0, dbd0, dcd0, dlam0, (l_wg2, *half_mix) = _ssm_bwd(
        "ssm_bwd0", dssm_all, p01, 1, states[0], ssm[0]["caug"], ssm[0]["tabc"], ssm[0]["bdt"],
        not ssm[0]["fwd_desc"], order_adj[0], t_rows, R,
        comm=_Both([_ChipExchange([p_wg2]), _SiblingSwap(mix_grads)]))
    p_wout, p_wbra, p_wbrs, p_wglu = pair_sums("mix", mix_grads, half_mix)
    du1, dbd1, dcd1, dlam1, (l_wu2,) = _ssm_bwd(
        "ssm_bwd1", dssm_all, p01, 1, states[1], ssm[1]["caug"], ssm[1]["tabc"], ssm[1]["bdt"],
        not ssm[1]["fwd_desc"], order_adj[1], t_rows, R, comm=_ChipExchange([p_wu2]))
    ssm_grads = [(dbd0, dcd0, dlam0), (dbd1, dcd1, dlam1)]

    trr = _row_tile(L, Lc)
    nlt = L // trr
    rowR = pl.BlockSpec((trr, W), lambda i: (i, 0))

    def du_fn(i, dua, dub, dsv, dvec, u):
        lat = (i < nlt).astype(F32)
        return [dua + dub + lat * (dvec * dsv)], [lat * _colsum(dsv * u)]

    (du_all,), (d_ssm_d,), _ = _rowwise(
        "du_combine", R // trr, [du0, du1, dssm_all, ssm_d, p01],
        [rowR, rowR, rowR, pl.BlockSpec((1, W), lambda i: (0, 0)), pl.BlockSpec((None, trr, W), lambda i: (1, i, 0))],
        [jax.ShapeDtypeStruct((R, W), BF16)], [rowR], [W], du_fn)

    def dz_sum(i, dzv):
        return [], [_colsum(dzv.astype(F32))]

    _, (d_b_glu,), _ = _rowwise("db_glu", L // tr, [dz], [rowW], [], [], [W], dz_sum)

    dq_rot, dk_rot, dv_hd, (l_wout, l_wbra, l_wbrs, l_wglu) = _attn_bwd(
        q_rot, k_rot, v_hd, d_attn, QPK, comm=_ChipExchange([p_wout, p_wbra, p_wbrs, p_wglu]))
    dq_pre, d_qg = _qk_prep_bwd("q_prep_bwd", dq_rot, p27, 0, HBQ, NQ, L, q_norm_g, cos_l, sin_l)
    dk_pre, d_kg = _qk_prep_bwd("k_prep_bwd", dk_rot, p01, 0, NKV, NKV, R, k_norm_g, cos_all, sin_all)
    dv_pre = _heads_merge("dv_merge", dv_hd)

    def lat_blocks(a):
        return jnp.pad(a, ((0, 0), (0, Lc), (0, 0)))

    dq_blocks = jnp.transpose(dq_pre.reshape(L, 2, INS), (1, 0, 2))
    dp = jnp.concatenate([
        jnp.concatenate([dk_pre, dv_pre], axis=1)[None], du_all[None],
        lat_blocks(dq_blocks), lat_blocks(dg_a), lat_blocks(dg_s)], axis=0)

    tmo = _tile(D, MM_TILE, LANES)
    (dwin,), _ = _matmul(
        "dw_in", (N_DEV, D // tmo), [h2, dp],
        [pl.BlockSpec((R, tmo), lambda j, m: (0, m)), pl.BlockSpec((None, R, INS), lambda j, m: (j, 0, 0))],
        [(0, 1, 0, TN)], [jax.ShapeDtypeStruct((N_DEV, D, INS), BF16)],
        [pl.BlockSpec((None, tmo, INS), lambda j, m: (j, m, 0))], _store_all)
    tnh = _tile(D, MM_TILE_NT, LANES)
    (dh2,), half_win = _matmul(
        "d_h2", (R // tm, D // tnh), [dp, win],
        [pl.BlockSpec((N_DEV, tm, INS), lambda i, n: (0, i, 0)),
         pl.BlockSpec((N_DEV, tnh, INS), lambda i, n: (0, n, 0))],
        [(0, 1, 0, NT, N_DEV)], [jax.ShapeDtypeStruct((R, D), F32)], [pl.BlockSpec((tm, tnh), lambda i, n: (i, n))],
        _store_all, comm=_SiblingSwap([dwin]))
    (p_win,) = pair_sums("win", [dwin], half_win)
    dxc1, (dsh2, dsc2, dmc3, dmc4, dgam2) = _norm_mod_bwd(
        "nm2_bwd", xc1, dh2, gam[1], grp(sc2, mc4), L, Lc, dres=dx2)

    df1, (dg1, dmc2) = _gate_bwd("gate1_bwd", dxc1, f1, gate1, 0.5, L, Lc)
    dwd1, _ = _ffn_dwd("ffn1b", s1, df1)
    da1, db1, (l_win, *half_wd1) = _ffn_ds(
        "ffn1b", df1, wd1, a1, b1, comm=_Both([_ChipExchange([p_win]), _SiblingSwap([dwd1])]))
    (p_wd1,) = pair_sums("wd1", [dwd1], half_wd1)
    dwg1, dwu1, (l_wd1,) = _ffn_dwgu("ffn1b", h1, da1, db1, comm=_ChipExchange([p_wd1]))
    dh1, half_wgu1 = _ffn_dh("ffn1b", da1, db1, wg1, wu1, comm=_SiblingSwap([dwg1, dwu1]))
    p_wg1, p_wu1 = pair_sums("wgu1", [dwg1, dwu1], half_wgu1)
    dxc0, (dsh1, dsc1, dmc0, dmc1, dgam1) = _norm_mod_bwd(
        "nm1_bwd", xc0, dh1, gam[0], grp(sc1, mc1), L, Lc, dres=dxc1)
    grad_x = dxc0[:L][None]

    dmod_lat = jnp.concatenate([dsh1, dsc1, dg1, dsh2, dsc2, dg2, dsh3, dsc3, dg3], axis=1)
    dmod_ctx = jnp.concatenate([dmc0, dmc1, dmc2, dmc3, dmc4, jnp.zeros((1, 4 * D), F32)], axis=1)
    dmod_pack = at_row(dmod_lat, 0, SUBLANES) + at_row(dmod_ctx, 1, SUBLANES)
    (dmod_g,) = _exchange_only("ag_dmod", _Gather([dmod_pack]))
    dmod_all = dmod_g.reshape(N_DEV * SUBLANES, 9 * D)
    dmod_cols = lax.dynamic_slice_in_dim(dmod_all, me * MODW, MODW, axis=1)
    (g_wmod, dl_wmod, nm_wmod, nv_wmod, dsilu), (l_wg1,) = _mod_bwd_adam(
        cs, dmod_cols, w_mod[0], m_w_mod[0], v_w_mod[0], comm=_ChipExchange([p_wg1]))
    sg_cc = jax.nn.sigmoid(c_ctx)
    d_c_ctx = dsilu[8] * (sg_cc * (1.0 + c_ctx * (1.0 - sg_cc)))
    g_bmod, dl_bmod, nm_bmod, nv_bmod = _bias_adam(dmod_all, b_mod, m_b_mod, v_b_mod)

    d_a_re, d_a_im, d_ldt, d_b_re, d_b_im, d_c_re, d_c_im = [], [], [], [], [], [], []
    for d_ in range(2):
        dbd, dcd, dlam = ssm_grads[d_]
        dbt_re = jnp.transpose(_block_diag_extract(dbd[:, :, :SW], E, P), (0, 2, 1))
        dbt_im = jnp.transpose(_block_diag_extract(dbd[:, :, SW:], E, P), (0, 2, 1))
        dl_re, dl_im = dlam[:, 0, :SW].reshape(G, P), dlam[:, 0, SW:].reshape(G, P)
        prim = (ssm_a_re[0, d_], ssm_a_im[0, d_], ssm_log_dt[0, d_], ssm_b_re[0, d_], ssm_b_im[0, d_])
        _, vjp = jax.vjp(_ssm_discretize, *prim)
        ga_re, ga_im, gl_dt, gb_re, gb_im = vjp((dl_re, dl_im, dbt_re, dbt_im))
        d_a_re.append(ga_re)
        d_a_im.append(ga_im)
        d_ldt.append(gl_dt)
        d_b_re.append(gb_re)
        d_b_im.append(gb_im)
        d_c_re.append(jnp.transpose(_block_diag_extract(dcd[:, :SW, :], P, E), (0, 2, 1)))
        d_c_im.append(-jnp.transpose(_block_diag_extract(dcd[:, SW:, :], P, E), (0, 2, 1)))

    dgam_all = jnp.concatenate([dgam1, dgam2, dgam3], axis=0)
    small_g = [d_c_ctx, d_qg, d_kg, jnp.stack(d_a_re), jnp.stack(d_a_im), jnp.stack(d_ldt), jnp.stack(d_b_re),
               jnp.stack(d_b_im), jnp.stack(d_c_re), jnp.stack(d_c_im), d_ssm_d, d_b_glu, dgam_all]
    small_w = [c_ctx, q_norm_g, k_norm_g, ssm_a_re, ssm_a_im, ssm_log_dt, ssm_b_re, ssm_b_im, ssm_c_re, ssm_c_im,
               ssm_d, b_glu, ng_full]
    small_m = [m_c_ctx, m_q_norm_g, m_k_norm_g, m_ssm_a_re, m_ssm_a_im, m_ssm_log_dt, m_ssm_b_re, m_ssm_b_im,
               m_ssm_c_re, m_ssm_c_im, m_ssm_d, m_b_glu, m_ng_full]
    small_v = [v_c_ctx, v_q_norm_g, v_k_norm_g, v_ssm_a_re, v_ssm_a_im, v_ssm_log_dt, v_ssm_b_re, v_ssm_b_im,
               v_ssm_c_re, v_ssm_c_im, v_ssm_d, v_b_glu, v_ng_full]
    small_shapes = [a.shape for a in small_w]
    n_rows = sum(-(-math.prod(s) // LANES) for s in small_shapes)
    n_rows = -(-n_rows // 256) * 256
    small_parts, l_wu1 = _exchange_only(
        "ag_small_grads", _Both([_Gather([_pack(small_g, n_rows)]), _ChipExchange([p_wu1])]))
    small_out = _sum_adam("small_adam", small_parts, _pack(small_w, n_rows), _pack(small_m, n_rows),
                          _pack(small_v, n_rows))
    sm_g, sm_dl, sm_m, sm_v = [_unpack(o, small_shapes) for o in small_out]

    def my_norm_cols(a):
        return lax.dynamic_slice_in_dim(a, me * dn, dn, axis=1)[None]

    for lst in (sm_g, sm_dl, sm_m, sm_v):
        lst[-1] = my_norm_cols(lst[-1])

    landed = [l_wg1, l_wu1, l_wd1, l_win, l_wglu, l_wbra, l_wbrs, l_wout, l_wg2, l_wu2, l_wd2]
    pairs = [p_wg1, p_wu1, p_wd1, p_win, p_wglu, p_wbra, p_wbrs, p_wout, p_wg2, p_wu2, p_wd2]
    big_w = [w_ffn1_gate, w_ffn1_up, w_ffn1_down, w_in, w_glu, w_br_attn, w_br_ssm, w_out, w_ffn2_gate, w_ffn2_up,
             w_ffn2_down]
    big_m = [m_w_ffn1_gate, m_w_ffn1_up, m_w_ffn1_down, m_w_in, m_w_glu, m_w_br_attn, m_w_br_ssm, m_w_out,
             m_w_ffn2_gate, m_w_ffn2_up, m_w_ffn2_down]
    big_v = [v_w_ffn1_gate, v_w_ffn1_up, v_w_ffn1_down, v_w_in, v_w_glu, v_w_br_attn, v_w_br_ssm, v_w_out,
             v_w_ffn2_gate, v_w_ffn2_up, v_w_ffn2_down]
    big_names = ["ffn1_gate", "ffn1_up", "ffn1_down", "in", "glu", "br_attn", "br_ssm", "out", "ffn2_gate",
                 "ffn2_up", "ffn2_down"]
    big_out = [[o[None] for o in _owner_adam("adam_" + nm, p, l_, chip, w_[0], m_[0], v_[0])]
               for nm, p, l_, w_, m_, v_ in zip(big_names, pairs, landed, big_w, big_m, big_v)]

    def leaf(kind):
        sm = (sm_g, sm_dl, sm_m, sm_v)[kind]
        mod = (g_wmod, dl_wmod, nm_wmod, nv_wmod)[kind][None]
        bmod = (g_bmod, dl_bmod, nm_bmod, nv_bmod)[kind]
        big = [b[kind] for b in big_out]
        (c_ctx_, qg_, kg_, a_re_, a_im_, ldt_, b_re_, b_im_, c_re_, c_im_, sd_, bglu_, ng_) = sm
        return [c_ctx_, mod, bmod, ng_, big[0], big[1], big[2], big[3], qg_, kg_, a_re_, a_im_, ldt_, b_re_, b_im_,
                c_re_, c_im_, sd_, big[4], bglu_, big[5], big[6], big[7], big[8], big[9], big[10]]

    return tuple([loss, grad_x] + leaf(0) + leaf(1) + leaf(2) + leaf(3))
```

```python
import math

import jax
import jax.numpy as jnp
import numpy as np
from jax import lax
from jax.experimental import pallas as pl
from jax.experimental.pallas import tpu as pltpu

F32 = jnp.float32
BF16 = jnp.bfloat16

N_DEV = 8
N_CHIPS = 4
LANES = 128
SUBLANES = 8
PACKED_SUBLANES = 16
VMEM_LIMIT = 56 * 1024 * 1024
MM_TILE = 512
MM_TILE_NT = 256
ROW_TILE = 256
HEAD_ROW_TILE = 512
ADAM_BLOCK_BYTES = 4 * 1024 * 1024

NORM_EPS = 1e-6
GRID_W = 64
ROPE_THETA = 10000.0
SCAN_TAPS = SUBLANES
SLAB_GROUPS = 8

ADAM_LR = 0.001
ADAM_B1 = 0.9
ADAM_B2 = 0.999
ADAM_EPS = 1e-08
ADAM_WD = 0.01
ADAM_STEP = 10

NN = (((1,), (0,)), ((), ()))
NT = (((1,), (1,)), ((), ()))
TN = (((0,), (0,)), ((), ()))

MESH = pl.DeviceIdType.MESH
ANY = pl.BlockSpec(memory_space=pl.ANY)


def _tile(n, cap, align):
    best = None
    for t in range(align, min(n, cap) + 1, align):
        if n % t == 0:
            best = t
    return n if best is None else best


def _params(n_grid):
    return pltpu.CompilerParams(dimension_semantics=("arbitrary",) * n_grid, vmem_limit_bytes=VMEM_LIMIT)


def _sigmoid(x):
    return 1.0 / (1.0 + jnp.exp(-x))


GELU_K = math.sqrt(2.0 / math.pi)
GELU_C = 0.044715


def _gelu(x):
    return 0.5 * x * (1.0 + jnp.tanh(GELU_K * (x + GELU_C * x * x * x)))


def _gelu_grad(x):
    t = jnp.tanh(GELU_K * (x + GELU_C * x * x * x))
    return 0.5 * (1.0 + t) + 0.5 * x * (1.0 - t * t) * GELU_K * (1.0 + 3.0 * GELU_C * x * x)


def _adamw(w, g, m, v):
    m2 = ADAM_B1 * m + (1.0 - ADAM_B1) * g
    v2 = ADAM_B2 * v + (1.0 - ADAM_B2) * (g * g)
    m_hat = m2 / (1.0 - ADAM_B1 ** ADAM_STEP)
    v_hat = v2 / (1.0 - ADAM_B2 ** ADAM_STEP)
    delta = -ADAM_LR * (m_hat / (jnp.sqrt(v_hat) + ADAM_EPS) + ADAM_WD * w)
    return delta, m2, v2


def _position():
    return lax.axis_index("x"), lax.axis_index("y"), lax.axis_index("c")


class _Gather:
    def __init__(self, arrays):
        self.arrays = list(arrays)
        n = len(self.arrays)
        self.out_shapes = [jax.ShapeDtypeStruct((N_DEV,) + a.shape, a.dtype) for a in self.arrays]
        self.scratch = [pltpu.SemaphoreType.DMA((n, 7)), pltpu.SemaphoreType.DMA((n, 7)),
                        pltpu.SemaphoreType.DMA((n,))]

    def _plan(self, ins, outs, sems):
        send, recv, local = sems
        x, y, c = _position()
        me, sibling = (x, y, c), (x, y, 1 - c)
        chips = [(1 - x, y), (x, 1 - y), (1 - x, 1 - y)]

        def slot(a, p):
            return outs[a].at[4 * p[0] + 2 * p[1] + p[2]]

        def copy(a, k, block, to, src=None):
            dst = slot(a, block)
            return pltpu.make_async_remote_copy(
                src_ref=dst if src is None else src, dst_ref=dst,
                send_sem=send.at[a, k], recv_sem=recv.at[a, k], device_id=to, device_id_type=MESH)

        mine = [pltpu.make_async_copy(ins[a], slot(a, me), local.at[a]) for a in range(len(ins))]
        return me, sibling, chips, c, copy, mine

    def start(self, ins, outs, sems):
        me, sibling, chips, c, copy, mine = self._plan(ins, outs, sems)
        for cp in mine:
            cp.start()
        for a in range(len(ins)):
            copy(a, 0, me, sibling, src=ins[a]).start()
            for j, chip in enumerate(chips):
                copy(a, 1 + j, me, (*chip, c), src=ins[a]).start()

    def finish(self, ins, outs, sems):
        me, sibling, chips, c, copy, mine = self._plan(ins, outs, sems)
        n = len(ins)
        for j, chip in enumerate(chips):
            for a in range(n):
                copy(a, 1 + j, (*chip, c), me).wait_recv()
                copy(a, 4 + j, (*chip, c), sibling).start()
        for a in range(n):
            copy(a, 0, sibling, me).wait_recv()
        for j, chip in enumerate(chips):
            for a in range(n):
                copy(a, 4 + j, (*chip, 1 - c), me).wait_recv()
        for a in range(n):
            copy(a, 0, me, sibling, src=ins[a]).wait_send()
            for j, chip in enumerate(chips):
                copy(a, 1 + j, me, (*chip, c), src=ins[a]).wait_send()
                copy(a, 4 + j, (*chip, c), sibling).wait_send()
        for cp in mine:
            cp.wait()


class _SiblingSwap:
    def __init__(self, arrays):
        self.arrays = list(arrays)
        n = len(self.arrays)
        self.out_shapes = [jax.ShapeDtypeStruct((N_CHIPS,) + a.shape[1:], a.dtype) for a in self.arrays]
        self.scratch = [pltpu.SemaphoreType.DMA((n, N_CHIPS)), pltpu.SemaphoreType.DMA((n, N_CHIPS))]

    def _plan(self, ins, outs, sems):
        send, recv = sems
        x, y, c = _position()
        return [pltpu.make_async_remote_copy(
            src_ref=ins[a].at[2 * j + 1 - c], dst_ref=outs[a].at[j],
            send_sem=send.at[a, j], recv_sem=recv.at[a, j], device_id=(x, y, 1 - c), device_id_type=MESH)
            for a in range(len(ins)) for j in range(N_CHIPS)]

    def start(self, ins, outs, sems):
        for cp in self._plan(ins, outs, sems):
            cp.start()

    def finish(self, ins, outs, sems):
        copies = self._plan(ins, outs, sems)
        for cp in copies:
            cp.wait_recv()
        for cp in copies:
            cp.wait_send()


class _ChipExchange:
    def __init__(self, arrays):
        self.arrays = list(arrays)
        n = len(self.arrays)
        self.out_shapes = [jax.ShapeDtypeStruct((N_CHIPS - 1,) + a.shape[1:], a.dtype) for a in self.arrays]
        self.scratch = [pltpu.SemaphoreType.DMA((n, N_CHIPS - 1)), pltpu.SemaphoreType.DMA((n, N_CHIPS - 1))]

    def _plan(self, ins, outs, sems):
        send, recv = sems
        x, y, c = _position()
        copies = []
        for r in range(1, N_CHIPS):
            px, py = x ^ (r >> 1), y ^ (r & 1)
            for a in range(len(ins)):
                copies.append(pltpu.make_async_remote_copy(
                    src_ref=ins[a].at[2 * px + py], dst_ref=outs[a].at[r - 1],
                    send_sem=send.at[a, r - 1], recv_sem=recv.at[a, r - 1],
                    device_id=(px, py, c), device_id_type=MESH))
        return copies

    def start(self, ins, outs, sems):
        for cp in self._plan(ins, outs, sems):
            cp.start()

    def finish(self, ins, outs, sems):
        copies = self._plan(ins, outs, sems)
        for cp in copies:
            cp.wait_recv()
        for cp in copies:
            cp.wait_send()


class _Both:
    def __init__(self, comms):
        self.comms = list(comms)
        self.arrays = [a for cm in self.comms for a in cm.arrays]
        self.out_shapes = [s for cm in self.comms for s in cm.out_shapes]
        self.scratch = [s for cm in self.comms for s in cm.scratch]

    def _split(self, ins, outs, sems):
        i = o = s = 0
        for cm in self.comms:
            ni, no, nsem = len(cm.arrays), len(cm.out_shapes), len(cm.scratch)
            yield cm, ins[i:i + ni], outs[o:o + no], sems[s:s + nsem]
            i, o, s = i + ni, o + no, s + nsem

    def start(self, ins, outs, sems):
        for cm, i, o, s in self._split(ins, outs, sems):
            cm.start(i, o, s)

    def finish(self, ins, outs, sems):
        for cm, i, o, s in self._split(ins, outs, sems):
            cm.finish(i, o, s)


def _host_call(body, *, name, grid, operands, in_specs, out_shape, out_specs, scratch_shapes=(), comm=None):
    grid = tuple(grid)
    n_in, n_out, n_scr = len(operands), len(out_shape), len(scratch_shapes)
    kwargs = dict(name=name, compiler_params=_params(len(grid)))
    if grid:
        kwargs["grid"] = grid
    if comm is None:
        res = pl.pallas_call(body, in_specs=list(in_specs), out_specs=list(out_specs), out_shape=list(out_shape),
                             scratch_shapes=list(scratch_shapes), **kwargs)(*operands)
        return list(res), []
    nc_in, nc_out = len(comm.arrays), len(comm.out_shapes)

    def hosted(*refs):
        bounds = [0, n_in, n_in + nc_in, n_in + nc_in + n_out, n_in + nc_in + n_out + nc_out,
                  n_in + nc_in + n_out + nc_out + n_scr, len(refs)]
        ins, cins, outs, couts, scr, sems = [refs[a:b] for a, b in zip(bounds[:-1], bounds[1:])]
        if not grid:
            comm.start(cins, couts, sems)
            body(*ins, *outs, *scr)
            comm.finish(cins, couts, sems)
            return
        first, last = None, None
        for ax, size in enumerate(grid):
            pid = pl.program_id(ax)
            f, l = pid == 0, pid == size - 1
            first = f if first is None else jnp.logical_and(first, f)
            last = l if last is None else jnp.logical_and(last, l)

        @pl.when(first)
        def _():
            comm.start(cins, couts, sems)

        body(*ins, *outs, *scr)

        @pl.when(last)
        def _():
            comm.finish(cins, couts, sems)

    res = pl.pallas_call(
        hosted, in_specs=list(in_specs) + [ANY] * nc_in, out_specs=list(out_specs) + [ANY] * nc_out,
        out_shape=list(out_shape) + list(comm.out_shapes), scratch_shapes=list(scratch_shapes) + list(comm.scratch),
        **kwargs)(*operands, *comm.arrays)
    return list(res[:n_out]), list(res[n_out:])


def _exchange_only(name, comm):
    def body():
        pass
    return _host_call(body, name=name, grid=(), operands=[], in_specs=[], out_shape=[], out_specs=[], comm=comm)[1]


def _matmul(name, grid, operands, in_specs, pairs, out_shapes, out_specs, epilogue, acc_shapes=(), nk=1,
            prologue=None, comm=None):
    n_in, n_out = len(operands), len(out_shapes)
    prologue = prologue or {}

    def body(*refs):
        ins, outs, accs = refs[:n_in], refs[n_in:n_in + n_out], refs[n_in + n_out:]
        pids = [pl.program_id(ax) for ax in range(len(grid))]

        def operand(i, blk=None):
            v = ins[i][...] if blk is None else ins[i][blk]
            if i in prologue:
                v = prologue[i](v)
            return v.astype(BF16)

        def products():
            vals = {}
            for pair in pairs:
                ai, bi, ci, dn = pair[:4]
                if len(pair) == 5:
                    p = None
                    for blk in range(pair[4]):
                        q = lax.dot_general(operand(ai, blk), operand(bi, blk), dn, preferred_element_type=F32)
                        p = q if p is None else p + q
                else:
                    p = lax.dot_general(operand(ai), operand(bi), dn, preferred_element_type=F32)
                vals[ci] = p if ci not in vals else vals[ci] + p
            return [vals[ci] for ci in sorted(vals)]

        if nk == 1:
            epilogue(products(), ins, outs, pids)
        else:
            k = pids[-1]
            prods = products()

            @pl.when(k == 0)
            def _():
                for acc, p in zip(accs, prods):
                    acc[...] = p

            @pl.when(k > 0)
            def _():
                for acc, p in zip(accs, prods):
                    acc[...] += p

            @pl.when(k == nk - 1)
            def _():
                epilogue([acc[...] for acc in accs], ins, outs, pids)

    return _host_call(
        body, name=name, grid=grid, operands=operands, in_specs=in_specs, out_shape=out_shapes, out_specs=out_specs,
        scratch_shapes=[pltpu.VMEM(s, F32) for s in acc_shapes] if nk > 1 else [], comm=comm)


def _rowwise(name, n_tiles, operands, in_specs, out_shapes, out_specs, red_widths, fn, comm=None):
    n_in, n_out, n_red = len(operands), len(out_shapes), len(red_widths)

    def body(*refs):
        ins, outs, reds = refs[:n_in], refs[n_in:n_in + n_out], refs[n_in + n_out:]
        i = pl.program_id(0)
        vals, sums = fn(i, *[r[...] for r in ins])
        for o, v in zip(outs, vals):
            o[...] = v.astype(o.dtype)
        if n_red:
            @pl.when(i == 0)
            def _():
                for r, s in zip(reds, sums):
                    r[...] = s

            @pl.when(i > 0)
            def _():
                for r, s in zip(reds, sums):
                    r[...] += s

    red_shapes = [jax.ShapeDtypeStruct((1, w), F32) for w in red_widths]
    red_specs = [pl.BlockSpec((1, w), lambda i: (0, 0)) for w in red_widths]
    res, cres = _host_call(
        body, name=name, grid=(n_tiles,), operands=operands, in_specs=in_specs,
        out_shape=list(out_shapes) + red_shapes, out_specs=list(out_specs) + red_specs, comm=comm)
    return res[:n_out], res[n_out:], cres


def _colsum(v):
    return jnp.sum(v, axis=0, keepdims=True)


def _store_all(accs, ins, outs, pids):
    for o, v in zip(outs, accs):
        o[...] = v.astype(o.dtype)


def _row_tile(rows_a, rows_b):
    return _tile(math.gcd(rows_a, rows_b) if rows_b else rows_a, ROW_TILE, SUBLANES)


def _tab_row(d, nlt, rows2):
    return pl.BlockSpec((None, 1, d), lambda i: (jnp.where(i < nlt, rows2[0], rows2[1]), 0, 0))


def _norm_mod_fwd(name, xs, tab, r_gamma, r_shift, r_scale, n_lat, n_ctx):
    rows, d = xs.shape
    tm = _row_tile(n_lat, n_ctx)
    nlt = n_lat // tm

    def fn(i, x, g, sh, sc):
        xh = x * lax.rsqrt(jnp.mean(x * x, axis=-1, keepdims=True) + NORM_EPS)
        return [(xh * g) * (1.0 + sc) + sh], []

    (h,), _, _ = _rowwise(
        name, rows // tm, [xs, tab, tab, tab],
        [pl.BlockSpec((tm, d), lambda i: (i, 0)), _tab_row(d, nlt, (r_gamma, r_gamma)), _tab_row(d, nlt, r_shift),
         _tab_row(d, nlt, r_scale)],
        [jax.ShapeDtypeStruct((rows, d), BF16)], [pl.BlockSpec((tm, d), lambda i: (i, 0))], [], fn)
    return h


def _norm_mod_bwd(name, xs, dh, tab, r_gamma, r_scale, n_lat, n_ctx, dres=None):
    rows, d = xs.shape
    tm = _row_tile(n_lat, n_ctx)
    nlt = n_lat // tm
    row = pl.BlockSpec((tm, d), lambda i: (i, 0))

    def fn(i, x, dy, g, sc, *res):
        rstd = lax.rsqrt(jnp.mean(x * x, axis=-1, keepdims=True) + NORM_EPS)
        xh = x * rstd
        dsh = _colsum(dy)
        dsc = _colsum(dy * (xh * g))
        dn = dy * (1.0 + sc)
        dgam = _colsum(dn * xh)
        dxh = dn * g
        dx = rstd * (dxh - xh * jnp.mean(dxh * xh, axis=-1, keepdims=True))
        if res:
            dx = dx + jnp.where(i < nlt, res[0], 0.0)
        lat = (i < nlt).astype(F32)
        return [dx], [dsh * lat, dsc * lat, dsh * (1.0 - lat), dsc * (1.0 - lat), dgam]

    operands = [xs, dh, tab, tab]
    specs = [row, row, _tab_row(d, nlt, (r_gamma, r_gamma)), _tab_row(d, nlt, r_scale)]
    if dres is not None:
        operands.append(dres)
        specs.append(pl.BlockSpec((tm, d), lambda i: (jnp.minimum(i, nlt - 1), 0)))
    (dx,), sums, _ = _rowwise(name, rows // tm, operands, specs,
                              [jax.ShapeDtypeStruct((rows, d), F32)], [row], [d] * 5, fn)
    return dx, sums


def _gate_bwd(name, dx, f, tab, r_gate, coef, n_lat, n_ctx):
    rows, d = dx.shape
    tm = _row_tile(n_lat, n_ctx)
    nlt = n_lat // tm
    row = pl.BlockSpec((tm, d), lambda i: (i, 0))

    def fn(i, dxv, fv, gv):
        dg = _colsum(dxv * fv) * coef
        lat = (i < nlt).astype(F32)
        return [(coef * gv) * dxv], [dg * lat, dg * (1.0 - lat)]

    (df,), sums, _ = _rowwise(
        name, rows // tm, [dx, f, tab],
        [row, row, _tab_row(d, nlt, r_gate)],
        [jax.ShapeDtypeStruct((rows, d), BF16)], [row], [d, d], fn)
    return df, sums


def _select_rows(i, tm, n_lat, v_lat, v_ctx):
    rows = i * tm + lax.broadcasted_iota(jnp.int32, (tm, 1), 0)
    return jnp.where(rows < n_lat, v_lat, v_ctx)


def _ffn_up(tag, h, wg, wu, comm=None):
    rows, d = h.shape
    nb, _, fs = wg.shape
    tm = _tile(rows, MM_TILE, LANES)
    blk = pl.BlockSpec((None, tm, fs), lambda j, i: (j, i, 0))
    wspec = pl.BlockSpec((None, d, fs), lambda j, i: (j, 0, 0))

    def epilogue(accs, ins, outs, pids):
        a, b = accs
        outs[0][...] = a.astype(BF16)
        outs[1][...] = b.astype(BF16)
        outs[2][...] = (a * _sigmoid(a) * b).astype(BF16)

    hid = jax.ShapeDtypeStruct((nb, rows, fs), BF16)
    (a, b, s), cres = _matmul(
        tag + "_up", (nb, rows // tm), [h, wg, wu],
        [pl.BlockSpec((tm, d), lambda j, i: (i, 0)), wspec, wspec],
        [(0, 1, 0, NN), (0, 2, 1, NN)], [hid, hid, hid], [blk, blk, blk], epilogue, comm=comm)
    return a, b, s, cres


def _ffn_down(tag, s, wd, xs, tab2, r_gate, n_lat, comm=None):
    nb, rows, fs = s.shape
    d = wd.shape[-1]
    tm = _tile(rows, MM_TILE, LANES)
    tn = _tile(d, MM_TILE, LANES)

    def epilogue(accs, ins, outs, pids):
        f = accs[0]
        g = ins[3][...]
        gate = _select_rows(pids[0], tm, n_lat, g[r_gate[0]:r_gate[0] + 1, :], g[r_gate[1]:r_gate[1] + 1, :])
        outs[0][...] = f
        outs[1][...] = ins[2][...] + 0.5 * gate * f

    out = jax.ShapeDtypeStruct((rows, d), F32)
    ospec = pl.BlockSpec((tm, tn), lambda i, n: (i, n))
    (f, xo), cres = _matmul(
        tag + "_down", (rows // tm, d // tn), [s, wd, xs, tab2],
        [pl.BlockSpec((nb, tm, fs), lambda i, n: (0, i, 0)), pl.BlockSpec((nb, fs, tn), lambda i, n: (0, 0, n)),
         ospec, pl.BlockSpec((tab2.shape[0], tn), lambda i, n: (0, n))],
        [(0, 1, 0, NN, nb)], [out, out], [ospec, ospec], epilogue, comm=comm)
    return f, xo, cres


def _ffn_ds(tag, df, wd, a, b, comm=None):
    rows, d = df.shape
    nb, fs, _ = wd.shape
    tm = _tile(rows, MM_TILE, LANES)
    blk = pl.BlockSpec((None, tm, fs), lambda j, i: (j, i, 0))

    def epilogue(accs, ins, outs, pids):
        ds = accs[0]
        av = ins[2][...].astype(F32)
        bv = ins[3][...].astype(F32)
        sg = _sigmoid(av)
        outs[0][...] = (ds * bv * (sg * (1.0 + av * (1.0 - sg)))).astype(BF16)
        outs[1][...] = (ds * (av * sg)).astype(BF16)

    hid = jax.ShapeDtypeStruct((nb, rows, fs), BF16)
    (da, db), cres = _matmul(
        tag + "_ds", (nb, rows // tm), [df, wd, a, b],
        [pl.BlockSpec((tm, d), lambda j, i: (i, 0)), pl.BlockSpec((None, fs, d), lambda j, i: (j, 0, 0)), blk, blk],
        [(0, 1, 0, NT)], [hid, hid], [blk, blk], epilogue, comm=comm)
    return da, db, cres


def _ffn_dwd(tag, s, df, comm=None):
    nb, rows, fs = s.shape
    d = df.shape[-1]
    tn = _tile(d, MM_TILE, LANES)
    (dwd,), cres = _matmul(
        tag + "_dwd", (nb, d // tn), [s, df],
        [pl.BlockSpec((None, rows, fs), lambda j, n: (j, 0, 0)), pl.BlockSpec((rows, tn), lambda j, n: (0, n))],
        [(0, 1, 0, TN)], [jax.ShapeDtypeStruct((nb, fs, d), BF16)],
        [pl.BlockSpec((None, fs, tn), lambda j, n: (j, 0, n))], _store_all, comm=comm)
    return dwd, cres


def _ffn_dwgu(tag, h, da, db, comm=None):
    rows, d = h.shape
    nb, _, fs = da.shape
    tmo = _tile(d, MM_TILE, LANES)
    full = pl.BlockSpec((None, rows, fs), lambda j, m: (j, 0, 0))
    wshape = jax.ShapeDtypeStruct((nb, d, fs), BF16)
    wblk = pl.BlockSpec((None, tmo, fs), lambda j, m: (j, m, 0))
    (dwg, dwu), cres = _matmul(
        tag + "_dwgu", (nb, d // tmo), [h, da, db],
        [pl.BlockSpec((rows, tmo), lambda j, m: (0, m)), full, full],
        [(0, 1, 0, TN), (0, 2, 1, TN)], [wshape, wshape], [wblk, wblk], _store_all, comm=comm)
    return dwg, dwu, cres


def _ffn_dh(tag, da, db, wg, wu, comm=None):
    nb, rows, fs = da.shape
    d = wg.shape[1]
    tm = _tile(rows, MM_TILE, LANES)
    tn = _tile(d, MM_TILE_NT, LANES)
    aspec = pl.BlockSpec((nb, tm, fs), lambda i, n: (0, i, 0))
    wspec = pl.BlockSpec((nb, tn, fs), lambda i, n: (0, n, 0))
    (dh,), cres = _matmul(
        tag + "_dh", (rows // tm, d // tn), [da, wg, db, wu], [aspec, wspec, aspec, wspec],
        [(0, 1, 0, NT, nb), (2, 3, 0, NT, nb)], [jax.ShapeDtypeStruct((rows, d), F32)],
        [pl.BlockSpec((tm, tn), lambda i, n: (i, n))], _store_all, comm=comm)
    return dh, cres


def _rope_tables(n_lat, n_ctx):
    half = LANES // 4
    inv_freq = (np.float32(ROPE_THETA) ** (-np.arange(half, dtype=np.float32) / np.float32(half))).astype(np.float32)
    pos = np.arange(n_lat)
    ang_r = (pos // GRID_W).astype(np.float32)[:, None] * inv_freq
    ang_c = (pos % GRID_W).astype(np.float32)[:, None] * inv_freq
    cos_l = np.concatenate([np.cos(ang_r)] * 2 + [np.cos(ang_c)] * 2, axis=1)
    sin_l = np.concatenate([-np.sin(ang_r), np.sin(ang_r), -np.sin(ang_c), np.sin(ang_c)], axis=1)
    cos_all = np.concatenate([cos_l, np.ones((n_ctx, LANES), np.float32)], axis=0).astype(np.float32)
    sin_all = np.concatenate([sin_l, np.zeros((n_ctx, LANES), np.float32)], axis=0).astype(np.float32)
    return jnp.asarray(cos_all), jnp.asarray(sin_all)


def _swap_halves(x):
    lane = lax.broadcasted_iota(jnp.int32, x.shape, 1)
    return jnp.where((lane % 64) < 32, pltpu.roll(x, 96, 1), pltpu.roll(x, 32, 1))


def _heads_spec(tq, hb, width, first_block):
    per_shard = width // (hb * LANES)

    def index(k, i):
        blk = first_block + k
        return blk // per_shard, i, blk % per_shard
    return pl.BlockSpec((None, tq, hb * LANES), index)


def _qk_prep(name, src, first_block, hb, n_heads, rows, g, cos_t, sin_t):
    tq = _tile(rows, HEAD_ROW_TILE, SUBLANES)
    tab = pl.BlockSpec((tq, LANES), lambda k, i: (i, 0))

    def body(x_ref, g_ref, c_ref, s_ref, o_ref):
        for h in range(hb):
            x = x_ref[:, h * LANES:(h + 1) * LANES]
            n = x * lax.rsqrt(jnp.mean(x * x, axis=-1, keepdims=True) + NORM_EPS) * g_ref[...]
            o_ref[h] = (n * c_ref[...] + _swap_halves(n) * s_ref[...]).astype(BF16)

    return pl.pallas_call(
        body, name=name, grid=(n_heads // hb, rows // tq),
        in_specs=[_heads_spec(tq, hb, src.shape[-1], first_block), pl.BlockSpec((1, LANES), lambda k, i: (0, 0)),
                  tab, tab],
        out_specs=pl.BlockSpec((hb, tq, LANES), lambda k, i: (k, i, 0)),
        out_shape=jax.ShapeDtypeStruct((n_heads, rows, LANES), BF16), compiler_params=_params(2),
    )(src, g, cos_t, sin_t)


def _qk_prep_bwd(name, dy, src, first_block, hb, n_heads, rows, g, cos_t, sin_t):
    tq = _tile(rows, HEAD_ROW_TILE, SUBLANES)
    tab = pl.BlockSpec((tq, LANES), lambda k, i: (i, 0))

    def body(dy_ref, x_ref, g_ref, c_ref, s_ref, dx_ref, dg_ref):
        g = g_ref[...]
        dg = None
        for h in range(hb):
            x = x_ref[:, h * LANES:(h + 1) * LANES]
            dyv = dy_ref[h]
            rstd = lax.rsqrt(jnp.mean(x * x, axis=-1, keepdims=True) + NORM_EPS)
            xh = x * rstd
            dn = dyv * c_ref[...] + _swap_halves(dyv * s_ref[...])
            dxh = dn * g
            dx = rstd * (dxh - xh * jnp.mean(dxh * xh, axis=-1, keepdims=True))
            dx_ref[:, h * LANES:(h + 1) * LANES] = dx.astype(BF16)
            part = _colsum(dn * xh)
            dg = part if dg is None else dg + part
        first = jnp.logical_and(pl.program_id(0) == 0, pl.program_id(1) == 0)

        @pl.when(first)
        def _():
            dg_ref[...] = dg

        @pl.when(jnp.logical_not(first))
        def _():
            dg_ref[...] += dg

    return pl.pallas_call(
        body, name=name, grid=(n_heads // hb, rows // tq),
        in_specs=[pl.BlockSpec((hb, tq, LANES), lambda k, i: (k, i, 0)),
                  _heads_spec(tq, hb, src.shape[-1], first_block),
                  pl.BlockSpec((1, LANES), lambda k, i: (0, 0)), tab, tab],
        out_specs=[pl.BlockSpec((tq, hb * LANES), lambda k, i: (i, k)),
                   pl.BlockSpec((1, LANES), lambda k, i: (0, 0))],
        out_shape=[jax.ShapeDtypeStruct((rows, n_heads * LANES), BF16), jax.ShapeDtypeStruct((1, LANES), F32)],
        compiler_params=_params(2),
    )(dy, src, g, cos_t, sin_t)


def _heads_cast(name, src, first_block, hb, n_heads, rows):
    tq = _tile(rows, HEAD_ROW_TILE, SUBLANES)

    def body(x_ref, o_ref):
        for h in range(hb):
            o_ref[h] = x_ref[:, h * LANES:(h + 1) * LANES].astype(BF16)

    return pl.pallas_call(
        body, name=name, grid=(n_heads // hb, rows // tq),
        in_specs=[_heads_spec(tq, hb, src.shape[-1], first_block)],
        out_specs=pl.BlockSpec((hb, tq, LANES), lambda k, i: (k, i, 0)),
        out_shape=jax.ShapeDtypeStruct((n_heads, rows, LANES), BF16), compiler_params=_params(2),
    )(src)


def _heads_merge(name, src):
    n_heads, rows, _ = src.shape
    tq = _tile(rows, HEAD_ROW_TILE, SUBLANES)

    def body(x_ref, o_ref):
        for h in range(n_heads):
            o_ref[:, h * LANES:(h + 1) * LANES] = x_ref[h].astype(BF16)

    return pl.pallas_call(
        body, name=name, grid=(rows // tq,),
        in_specs=[pl.BlockSpec((n_heads, tq, LANES), lambda i: (0, i, 0))],
        out_specs=pl.BlockSpec((tq, n_heads * LANES), lambda i: (i, 0)),
        out_shape=jax.ShapeDtypeStruct((rows, n_heads * LANES), BF16), compiler_params=_params(1),
    )(src)


def _attn_fwd(q, k, v, q_per_kv, comm=None):
    nq, l, _ = q.shape
    s_len = k.shape[1]
    tq = _tile(l, ROW_TILE, SUBLANES)
    scale = LANES ** -0.5
    kv = pl.BlockSpec((None, s_len, LANES), lambda h, i: (h // q_per_kv, 0, 0))

    def body(q_ref, k_ref, v_ref, o_ref):
        s = lax.dot_general(q_ref[...], k_ref[...], NT, preferred_element_type=F32) * scale
        p = jnp.exp(s - jnp.max(s, axis=-1, keepdims=True))
        den = jnp.sum(p, axis=-1, keepdims=True)
        o = jnp.dot(p.astype(BF16), v_ref[...], preferred_element_type=F32)
        o_ref[...] = (o / den).astype(BF16)

    (o,), cres = _host_call(
        body, name="attn_fwd", grid=(nq, l // tq), operands=[q, k, v],
        in_specs=[pl.BlockSpec((None, tq, LANES), lambda h, i: (h, i, 0)), kv, kv],
        out_shape=[jax.ShapeDtypeStruct((l, nq * LANES), BF16)],
        out_specs=[pl.BlockSpec((tq, LANES), lambda h, i: (i, h))], comm=comm)
    return o, cres


def _attn_bwd(q, k, v, do, q_per_kv, comm=None):
    nq, l, _ = q.shape
    nkv, s_len, _ = k.shape
    tq = _tile(l, ROW_TILE, SUBLANES)
    scale = LANES ** -0.5
    kv = pl.BlockSpec((None, s_len, LANES), lambda g, r, i: (g, 0, 0))
    qs = pl.BlockSpec((None, tq, LANES), lambda g, r, i: (g * q_per_kv + r, i, 0))

    def body(q_ref, k_ref, v_ref, do_ref, dq_ref, dk_ref, dv_ref):
        qv, kvv, vv, dov = q_ref[...], k_ref[...], v_ref[...], do_ref[...]
        st = lax.dot_general(kvv, qv, NT, preferred_element_type=F32) * scale
        e = jnp.exp(st - jnp.max(st, axis=0, keepdims=True))
        pt = e / jnp.sum(e, axis=0, keepdims=True)
        dpt = lax.dot_general(vv, dov, NT, preferred_element_type=F32)
        delta = jnp.sum(pt * dpt, axis=0, keepdims=True)
        dst = (pt * (dpt - delta) * scale).astype(BF16)
        ptb = pt.astype(BF16)
        dq_ref[...] = lax.dot_general(dst, kvv, TN, preferred_element_type=F32)
        dk_new = jnp.dot(dst, qv, preferred_element_type=F32)
        dv_new = jnp.dot(ptb, dov, preferred_element_type=F32)
        first = jnp.logical_and(pl.program_id(1) == 0, pl.program_id(2) == 0)

        @pl.when(first)
        def _():
            dk_ref[...] = dk_new
            dv_ref[...] = dv_new

        @pl.when(jnp.logical_not(first))
        def _():
            dk_ref[...] += dk_new
            dv_ref[...] += dv_new

    (dq, dk, dv), cres = _host_call(
        body, name="attn_bwd", grid=(nkv, q_per_kv, l // tq), operands=[q, k, v, do],
        in_specs=[qs, kv, kv, pl.BlockSpec((tq, LANES), lambda g, r, i: (i, g * q_per_kv + r))],
        out_specs=[qs, kv, kv],
        out_shape=[jax.ShapeDtypeStruct((nq, l, LANES), F32), jax.ShapeDtypeStruct((nkv, s_len, LANES), F32),
                   jax.ShapeDtypeStruct((nkv, s_len, LANES), F32)], comm=comm)
    return dq, dk, dv, cres


def _zoh(a_re, a_im, log_dt):
    dt = jnp.exp(log_dt)[..., None]
    mag = jnp.exp(a_re * dt)
    lb_re = mag * jnp.cos(a_im * dt)
    lb_im = mag * jnp.sin(a_im * dt)
    den = a_re * a_re + a_im * a_im
    coef_re = ((lb_re - 1.0) * a_re + lb_im * a_im) / den
    coef_im = (lb_im * a_re - (lb_re - 1.0) * a_im) / den
    return lb_re, lb_im, coef_re, coef_im


def _ssm_discretize(a_re, a_im, log_dt, b_re, b_im):
    lb_re, lb_im, cr, ci = _zoh(a_re, a_im, log_dt)
    bt_re = cr[..., None] * b_re - ci[..., None] * b_im
    bt_im = cr[..., None] * b_im + ci[..., None] * b_re
    return lb_re, lb_im, bt_re, bt_im


def _lambda_powers(a_re, a_im, log_dt, ns):
    dt = jnp.exp(log_dt)[..., None]
    k = jnp.arange(SCAN_TAPS + 1, dtype=F32)[:, None, None, None]
    mag, ang = jnp.exp(k * (a_re * dt)), k * (a_im * dt)
    shape = (SCAN_TAPS + 1, 2, ns, -1)
    return (mag * jnp.cos(ang)).reshape(shape), (mag * jnp.sin(ang)).reshape(shape)


def _slab_mask():
    idx = jnp.arange(SLAB_GROUPS)
    return (idx[:, None] == idx[None, :])[None, None, :, None, :, None]


def _block_diag(m):
    d, g, a, b = m.shape
    ns = g // SLAB_GROUPS
    wide = jnp.where(_slab_mask(), m.reshape(d, ns, SLAB_GROUPS, a, 1, b), 0.0)
    return wide.reshape(d, ns, SLAB_GROUPS * a, SLAB_GROUPS * b)


def _block_diag_extract(m, a, b):
    d, ns = m.shape[:2]
    m = m.reshape(d, ns, SLAB_GROUPS, a, SLAB_GROUPS, b)
    return jnp.sum(jnp.where(_slab_mask(), m, 0.0), axis=4).reshape(d, ns * SLAB_GROUPS, a, b)


def _tap_weights(base_re, base_im, pw_re, pw_im):
    pr = jnp.transpose(pw_re[:SCAN_TAPS], (1, 2, 0, 3))[:, :, :, None, :]
    pi = jnp.transpose(pw_im[:SCAN_TAPS], (1, 2, 0, 3))[:, :, :, None, :]
    br, bi = base_re[:, :, None], base_im[:, :, None]
    d, ns, cdim, s = base_re.shape
    re = (pr * br - pi * bi).reshape(d, ns, SCAN_TAPS * cdim, s)
    im = (pr * bi + pi * br).reshape(d, ns, SCAN_TAPS * cdim, s)
    return jnp.concatenate([re, im], axis=-1)


def _carry_tables(pw_re, pw_im, descending):
    def rows(pw):
        asc = pw[1:]
        per_dir = [asc[::-1, d] if descending[d] else asc[:, d] for d in range(2)]
        return jnp.transpose(jnp.stack(per_dir), (0, 2, 1, 3))
    return jnp.concatenate([rows(pw_re), rows(pw_im)], axis=-1)


def _scan_chunk(x, w_ref, tab_ref, s_ref, carry_ref, descending, t_rows, sw):
    row8 = lax.broadcasted_iota(jnp.int32, x.shape, 0) % SCAN_TAPS
    pieces = [x.astype(BF16)]
    for tau in range(1, SCAN_TAPS):
        if descending:
            sh = jnp.where(row8 <= SCAN_TAPS - 1 - tau, pltpu.roll(x, t_rows - tau, 0), 0.0)
        else:
            sh = jnp.where(row8 >= tau, pltpu.roll(x, tau, 0), 0.0)
        pieces.append(sh.astype(BF16))
    xa = jnp.concatenate(pieces, axis=1)
    s_ref[...] = jnp.dot(xa, w_ref[...], preferred_element_type=F32)
    tab = tab_ref[...]
    t_re, t_im = tab[:, :sw], tab[:, sw:]
    nb = t_rows // SCAN_TAPS
    edge = 0 if descending else SCAN_TAPS - 1

    def step(b, carry):
        h_re, h_im = carry
        r0 = pl.multiple_of(((nb - 1 - b) if descending else b) * SCAN_TAPS, SCAN_TAPS)
        x_re = s_ref[pl.ds(r0, SCAN_TAPS), :sw] + t_re * h_re - t_im * h_im
        x_im = s_ref[pl.ds(r0, SCAN_TAPS), sw:] + t_re * h_im + t_im * h_re
        s_ref[pl.ds(r0, SCAN_TAPS), :sw] = x_re
        s_ref[pl.ds(r0, SCAN_TAPS), sw:] = x_im
        return x_re[edge:edge + 1, :], x_im[edge:edge + 1, :]

    h_re, h_im = lax.fori_loop(0, nb, step, (carry_ref[0:1, :sw], carry_ref[0:1, sw:]))
    carry_ref[0:1, :sw] = h_re
    carry_ref[0:1, sw:] = h_im


def _ssm_fwd(name, dr, u_src, u_shard, waug, tab, cd, descending, chunk_of, t_rows, rows, comm=None):
    _, ns, kdim, sw2 = waug.shape
    sw = sw2 // 2
    width = ns * LANES
    nchunks = rows // t_rows

    def body(u_ref, w_ref, tab_ref, cd_ref, y_ref, h_ref, s_ref, carry_ref):
        @pl.when(pl.program_id(1) == 0)
        def _():
            carry_ref[...] = jnp.zeros_like(carry_ref)

        _scan_chunk(u_ref[...], w_ref, tab_ref, s_ref, carry_ref, descending, t_rows, sw)
        hb = s_ref[...].astype(BF16)
        h_ref[...] = hb
        y_ref[...] = jnp.dot(hb, cd_ref[...], preferred_element_type=F32)

    (y, h), cres = _host_call(
        body, name=name, grid=(ns, nchunks), operands=[u_src, waug, tab, cd],
        in_specs=[pl.BlockSpec((None, t_rows, LANES), lambda s, i: (u_shard, chunk_of(i), s)),
                  pl.BlockSpec((None, None, kdim, sw2), lambda s, i: (dr, s, 0, 0)),
                  pl.BlockSpec((None, None, SCAN_TAPS, sw2), lambda s, i: (dr, s, 0, 0)),
                  pl.BlockSpec((None, None, sw2, LANES), lambda s, i: (dr, s, 0, 0))],
        out_specs=[pl.BlockSpec((t_rows, LANES), lambda s, i: (chunk_of(i), s)),
                   pl.BlockSpec((None, t_rows, sw2), lambda s, i: (s, chunk_of(i), 0))],
        out_shape=[jax.ShapeDtypeStruct((rows, width), F32), jax.ShapeDtypeStruct((ns, rows, sw2), BF16)],
        scratch_shapes=[pltpu.VMEM((t_rows, sw2), F32), pltpu.VMEM((SUBLANES, sw2), F32)], comm=comm)
    return y, h, cres


def _ssm_bwd(name, dr, dy, u_src, u_shard, states, caug, tab, bdt, descending, chunk_of, t_rows, rows, comm=None):
    _, ns, kdim, sw2 = caug.shape
    sw = sw2 // 2
    width = ns * LANES
    nchunks = rows // t_rows

    def body(dy_ref, u_ref, h_ref, w_ref, tab_ref, bdt_ref, du_ref, dbd_ref, dcd_ref, dlam_ref,
             s_ref, carry_ref, gsave_ref):
        first = pl.program_id(1) == 0

        @pl.when(first)
        def _():
            carry_ref[...] = jnp.zeros_like(carry_ref)
            gsave_ref[...] = jnp.zeros_like(gsave_ref)

        dyv = dy_ref[...]
        _scan_chunk(dyv, w_ref, tab_ref, s_ref, carry_ref, descending, t_rows, sw)
        g = s_ref[...]
        gb = g.astype(BF16)
        du_ref[...] = jnp.dot(gb, bdt_ref[...], preferred_element_type=F32)
        dbd = lax.dot_general(u_ref[...].astype(BF16), gb, TN, preferred_element_type=F32)
        hb = h_ref[...]
        dcd = lax.dot_general(hb, dyv.astype(BF16), TN, preferred_element_type=F32)
        hf = hb.astype(F32)
        rowid = lax.broadcasted_iota(jnp.int32, hf.shape, 0)
        if descending:
            hp = jnp.where(rowid == 0, 0.0, pltpu.roll(hf, 1, 0))
            h_edge, g_edge = hf[t_rows - 1:t_rows, :], g[0:1, :]
        else:
            hp = jnp.where(rowid == t_rows - 1, 0.0, pltpu.roll(hf, t_rows - 1, 0))
            h_edge, g_edge = hf[0:1, :], g[t_rows - 1:t_rows, :]
        g_re, g_im, hp_re, hp_im = g[:, :sw], g[:, sw:], hp[:, :sw], hp[:, sw:]
        gs = gsave_ref[0:1, :]
        gs_re, gs_im, he_re, he_im = gs[:, :sw], gs[:, sw:], h_edge[:, :sw], h_edge[:, sw:]
        dl_re = _colsum(g_re * hp_re + g_im * hp_im) + gs_re * he_re + gs_im * he_im
        dl_im = _colsum(g_im * hp_re - g_re * hp_im) + gs_im * he_re - gs_re * he_im
        gsave_ref[0:1, :] = g_edge

        @pl.when(first)
        def _():
            dbd_ref[...] = dbd
            dcd_ref[...] = dcd
            dlam_ref[:, :sw] = dl_re
            dlam_ref[:, sw:] = dl_im

        @pl.when(jnp.logical_not(first))
        def _():
            dbd_ref[...] += dbd
            dcd_ref[...] += dcd
            dlam_ref[:, :sw] += dl_re
            dlam_ref[:, sw:] += dl_im

    (du, dbd, dcd, dlam), cres = _host_call(
        body, name=name, grid=(ns, nchunks), operands=[dy, u_src, states, caug, tab, bdt],
        in_specs=[pl.BlockSpec((t_rows, LANES), lambda s, i: (chunk_of(i), s)),
                  pl.BlockSpec((None, t_rows, LANES), lambda s, i: (u_shard, chunk_of(i), s)),
                  pl.BlockSpec((None, t_rows, sw2), lambda s, i: (s, chunk_of(i), 0)),
                  pl.BlockSpec((None, None, kdim, sw2), lambda s, i: (dr, s, 0, 0)),
                  pl.BlockSpec((None, None, SCAN_TAPS, sw2), lambda s, i: (dr, s, 0, 0)),
                  pl.BlockSpec((None, None, sw2, LANES), lambda s, i: (dr, s, 0, 0))],
        out_specs=[pl.BlockSpec((t_rows, LANES), lambda s, i: (chunk_of(i), s)),
                   pl.BlockSpec((None, LANES, sw2), lambda s, i: (s, 0, 0)),
                   pl.BlockSpec((None, sw2, LANES), lambda s, i: (s, 0, 0)),
                   pl.BlockSpec((None, 1, sw2), lambda s, i: (s, 0, 0))],
        out_shape=[jax.ShapeDtypeStruct((rows, width), F32), jax.ShapeDtypeStruct((ns, LANES, sw2), F32),
                   jax.ShapeDtypeStruct((ns, sw2, LANES), F32), jax.ShapeDtypeStruct((ns, 1, sw2), F32)],
        scratch_shapes=[pltpu.VMEM((t_rows, sw2), F32), pltpu.VMEM((SUBLANES, sw2), F32),
                        pltpu.VMEM((SUBLANES, sw2), F32)], comm=comm)
    return du, dbd, dcd, dlam, cres


def _mod_fwd(cs, w_mod, b_cols):
    d, width = w_mod.shape
    tn = _tile(width, 768, LANES)

    def epilogue(accs, ins, outs, pids):
        outs[0][...] = accs[0] + ins[2][...]

    return _matmul(
        "mod_fwd", (width // tn,), [cs, w_mod, b_cols],
        [pl.BlockSpec((16, d), lambda n: (0, 0)), pl.BlockSpec((d, tn), lambda n: (0, n)),
         pl.BlockSpec((1, tn), lambda n: (0, n))],
        [(0, 1, 0, NN)], [jax.ShapeDtypeStruct((16, width), F32)], [pl.BlockSpec((16, tn), lambda n: (0, n))],
        epilogue, prologue={0: lambda v: v * _sigmoid(v)})[0][0]


def _mod_bwd_adam(cs, dmod_cols, w, m, v, comm=None):
    d, width = w.shape
    tn = _tile(width, LANES, LANES)
    col = pl.BlockSpec((d, tn), lambda n: (0, n))

    def body(cs_ref, dm_ref, w_ref, m_ref, v_ref, g_ref, dl_ref, nm_ref, nv_ref, ds_ref):
        n = pl.program_id(0)
        lat = dm_ref[pl.ds(0, N_DEV, stride=SUBLANES), :]
        ctx = jnp.sum(dm_ref[pl.ds(1, N_DEV, stride=SUBLANES), :], axis=0, keepdims=True)
        row = lax.broadcasted_iota(jnp.int32, lat.shape, 0)
        dm = jnp.concatenate([lat, jnp.where(row == 0, ctx, 0.0)], axis=0).astype(BF16)
        c = cs_ref[...]
        sc = (c * _sigmoid(c)).astype(BF16)
        wv = w_ref[...]
        g = lax.dot_general(sc, dm, TN, preferred_element_type=F32)
        delta, m2, v2 = _adamw(wv, g, m_ref[...], v_ref[...])
        g_ref[...] = g
        dl_ref[...] = delta
        nm_ref[...] = m2
        nv_ref[...] = v2
        part = lax.dot_general(dm, wv.astype(BF16), NT, preferred_element_type=F32)

        @pl.when(n == 0)
        def _():
            ds_ref[...] = part

        @pl.when(n > 0)
        def _():
            ds_ref[...] += part

    shard = jax.ShapeDtypeStruct((d, width), F32)
    return _host_call(
        body, name="mod_bwd_adam", grid=(width // tn,), operands=[cs, dmod_cols, w, m, v],
        in_specs=[pl.BlockSpec((16, d), lambda n: (0, 0)), pl.BlockSpec((N_DEV * SUBLANES, tn), lambda n: (0, n)),
                  col, col, col],
        out_specs=[col, col, col, col, pl.BlockSpec((16, d), lambda n: (0, 0))],
        out_shape=[shard, shard, shard, shard, jax.ShapeDtypeStruct((16, d), F32)], comm=comm)


def _pair_sum(name, grads, got, core):
    _, rows, cols = grads.shape
    tr = _tile(rows, max(PACKED_SUBLANES, ADAM_BLOCK_BYTES // (cols * 6 * N_CHIPS)), PACKED_SUBLANES)
    blk = pl.BlockSpec((N_CHIPS, tr, cols), lambda i, cc: (0, i, 0))

    def body(core_ref, a_ref, b_ref, o_ref):
        o_ref[...] = (a_ref[...].astype(F32) + b_ref[...].astype(F32)).astype(BF16)

    grid_spec = pltpu.PrefetchScalarGridSpec(
        num_scalar_prefetch=1, grid=(rows // tr,),
        in_specs=[pl.BlockSpec((N_CHIPS, None, tr, cols), lambda i, cc: (0, cc[0], i, 0)), blk], out_specs=blk)
    return pl.pallas_call(
        body, name=name, grid_spec=grid_spec, out_shape=jax.ShapeDtypeStruct((N_CHIPS, rows, cols), BF16),
        compiler_params=_params(1))(core, grads.reshape(N_CHIPS, 2, rows, cols), got)


def _owner_adam(name, pairs, landed, chip, w, m, v):
    rows, cols = w.shape
    tr = _tile(rows, max(PACKED_SUBLANES, ADAM_BLOCK_BYTES // (cols * 40)), PACKED_SUBLANES)
    blk = pl.BlockSpec((tr, cols), lambda i, ch: (i, 0))

    def body(chip_ref, p_ref, l_ref, w_ref, m_ref, v_ref, g_ref, dl_ref, nm_ref, nv_ref):
        g = p_ref[...].astype(F32)
        for s in range(N_CHIPS - 1):
            g = g + l_ref[s].astype(F32)
        delta, m2, v2 = _adamw(w_ref[...], g, m_ref[...], v_ref[...])
        g_ref[...] = g
        dl_ref[...] = delta
        nm_ref[...] = m2
        nv_ref[...] = v2

    out = jax.ShapeDtypeStruct((rows, cols), F32)
    grid_spec = pltpu.PrefetchScalarGridSpec(
        num_scalar_prefetch=1, grid=(rows // tr,),
        in_specs=[pl.BlockSpec((None, tr, cols), lambda i, ch: (ch[0], i, 0)),
                  pl.BlockSpec((N_CHIPS - 1, tr, cols), lambda i, ch: (0, i, 0)), blk, blk, blk],
        out_specs=[blk, blk, blk, blk])
    return pl.pallas_call(body, name=name, grid_spec=grid_spec, out_shape=[out, out, out, out],
                          compiler_params=_params(1))(chip, pairs, landed, w, m, v)


def _sum_adam(name, parts, w, m, v):
    rows, cols = w.shape
    n_parts = parts.shape[0]
    align = PACKED_SUBLANES if parts.dtype == BF16 else SUBLANES
    tr = _tile(rows, max(align, ADAM_BLOCK_BYTES // (cols * 44)), align)
    blk = pl.BlockSpec((tr, cols), lambda i: (i, 0))

    def body(p_ref, w_ref, m_ref, v_ref, g_ref, dl_ref, nm_ref, nv_ref):
        g = p_ref[0].astype(F32)
        for s in range(1, n_parts):
            g = g + p_ref[s].astype(F32)
        delta, m2, v2 = _adamw(w_ref[...], g, m_ref[...], v_ref[...])
        g_ref[...] = g
        dl_ref[...] = delta
        nm_ref[...] = m2
        nv_ref[...] = v2

    out = jax.ShapeDtypeStruct((rows, cols), F32)
    return pl.pallas_call(
        body, name=name, grid=(rows // tr,),
        in_specs=[pl.BlockSpec((n_parts, tr, cols), lambda i: (0, i, 0)), blk, blk, blk],
        out_specs=[blk, blk, blk, blk], out_shape=[out, out, out, out], compiler_params=_params(1),
    )(parts, w, m, v)


def _bias_adam(dmod_all, w, m, v):
    width = w.shape[-1]
    tn = _tile(width, 2048, LANES)
    blk = pl.BlockSpec((1, tn), lambda n: (0, n))

    def body(p_ref, w_ref, m_ref, v_ref, g_ref, dl_ref, nm_ref, nv_ref):
        g = jnp.sum(p_ref[...], axis=0, keepdims=True)
        delta, m2, v2 = _adamw(w_ref[...], g, m_ref[...], v_ref[...])
        g_ref[...] = g
        dl_ref[...] = delta
        nm_ref[...] = m2
        nv_ref[...] = v2

    out = jax.ShapeDtypeStruct((1, width), F32)
    return pl.pallas_call(
        body, name="bias_adam", grid=(width // tn,),
        in_specs=[pl.BlockSpec((dmod_all.shape[0], tn), lambda n: (0, n)), blk, blk, blk],
        out_specs=[blk, blk, blk, blk], out_shape=[out, out, out, out], compiler_params=_params(1),
    )(dmod_all, w, m, v)


def _pack(arrays, total_rows):
    flat = []
    for a in arrays:
        a = a.reshape(-1).astype(F32)
        flat.append(jnp.pad(a, (0, (-a.shape[0]) % LANES)))
    flat = jnp.concatenate(flat).reshape(-1, LANES)
    return jnp.pad(flat, ((0, total_rows - flat.shape[0]), (0, 0)))


def _unpack(packed, shapes):
    out, row = [], 0
    for shp in shapes:
        size = math.prod(shp)
        nrows = -(-size // LANES)
        out.append(packed[row:row + nrows].reshape(-1)[:size].reshape(shp))
        row += nrows
    return out


def kernel(x, c, ctx, c_ctx, w_mod, b_mod, norm_g, w_ffn1_gate, w_ffn1_up, w_ffn1_down, w_in, q_norm_g, k_norm_g, ssm_a_re, ssm_a_im, ssm_log_dt, ssm_b_re, ssm_b_im, ssm_c_re, ssm_c_im, ssm_d, w_glu, b_glu, w_br_attn, w_br_ssm, w_out, w_ffn2_gate, w_ffn2_up, w_ffn2_down, loss_target, m_c_ctx, m_w_mod, m_b_mod, m_norm_g, m_w_ffn1_gate, m_w_ffn1_up, m_w_ffn1_down, m_w_in, m_q_norm_g, m_k_norm_g, m_ssm_a_re, m_ssm_a_im, m_ssm_log_dt, m_ssm_b_re, m_ssm_b_im, m_ssm_c_re, m_ssm_c_im, m_ssm_d, m_w_glu, m_b_glu, m_w_br_attn, m_w_br_ssm, m_w_out, m_w_ffn2_gate, m_w_ffn2_up, m_w_ffn2_down, v_c_ctx, v_w_mod, v_b_mod, v_norm_g, v_w_ffn1_gate, v_w_ffn1_up, v_w_ffn1_down, v_w_in, v_q_norm_g, v_k_norm_g, v_ssm_a_re, v_ssm_a_im, v_ssm_log_dt, v_ssm_b_re, v_ssm_b_im, v_ssm_c_re, v_ssm_c_im, v_ssm_d, v_w_glu, v_b_glu, v_w_br_attn, v_w_br_ssm, v_w_out, v_w_ffn2_gate, v_w_ffn2_up, v_w_ffn2_down):
    _, L, D = x.shape
    Lc = ctx.shape[1]
    R = L + Lc
    MODW = w_mod.shape[-1]
    INS = w_in.shape[-1]
    KVW = INS // 2
    NQ = D // LANES
    NKV = KVW // LANES
    QPK = NQ // NKV
    HBQ = INS // LANES
    G, P, E = ssm_b_re.shape[2:]
    W = G * E
    SW = SLAB_GROUPS * P
    assert E * SLAB_GROUPS == LANES and W == INS and NQ * LANES == D and Lc <= L
    me = 4 * lax.axis_index("x") + 2 * lax.axis_index("y") + lax.axis_index("c")

    x2, ctx2, tgt = x[0], ctx[0], loss_target[0]
    xc0 = jnp.concatenate([x2, ctx2], axis=0)

    def bf(w):
        return w[0].astype(BF16)

    def widen(a):
        return jnp.pad(a[0], ((0, 0), (0, D - a.shape[-1])))

    def at_row(a, r, total):
        return jnp.pad(a, ((r, total - r - a.shape[0]), (0, 0)))

    pack_in = (at_row(c, 0, 16) + at_row(widen(norm_g), 1, 16) + at_row(widen(m_norm_g), 4, 16)
               + at_row(widen(v_norm_g), 7, 16))
    (g_in,) = _exchange_only("ag_inputs", _Gather([pack_in]))
    c_all = g_in[:, 0, :]
    dn = D // N_DEV

    def full_norm(k):
        return jnp.transpose(g_in[:, k:k + 3, :dn], (1, 0, 2)).reshape(3, D)

    ng_full, m_ng_full, v_ng_full = full_norm(1), full_norm(4), full_norm(7)
    cs = at_row(c_all, 0, 16) + at_row(c_ctx[None, :], 8, 16)

    b_cols = lax.dynamic_slice_in_dim(b_mod, me * MODW, MODW, axis=1)
    mod_blk = _mod_fwd(cs, w_mod[0], b_cols)
    (mod_g,) = _exchange_only("ag_mod", _Gather([mod_blk]))
    mod_lat = lax.dynamic_index_in_dim(mod_g, me, axis=1, keepdims=False).reshape(9, D)
    mod_ctx = mod_g[:, 8, :].reshape(9, D)[:5]
    tab2 = jnp.concatenate([mod_lat, mod_ctx, ng_full, jnp.zeros((7, D), F32)], axis=0)
    tab3 = tab2[:, None, :]
    SH1, SC1, G1, SH2, SC2, G2, SH3, SC3, G3, MC0, MC1, MC2, MC3, MC4, GAM1, GAM2, GAM3 = range(17)

    wg1, wu1 = _exchange_only("ag_ffn1_gate_up", _Gather([bf(w_ffn1_gate), bf(w_ffn1_up)]))
    h1 = _norm_mod_fwd("nm1_fwd", xc0, tab3, GAM1, (SH1, MC0), (SC1, MC1), L, Lc)
    a1, b1, s1, (wd1,) = _ffn_up("ffn1", h1, wg1, wu1, comm=_Gather([bf(w_ffn1_down)]))
    f1, xc1, (win,) = _ffn_down("ffn1", s1, wd1, xc0, tab2, (G1, MC2), L, comm=_Gather([bf(w_in)]))

    h2 = _norm_mod_fwd("nm2_fwd", xc1, tab3, GAM2, (SH2, MC3), (SC2, MC4), L, Lc)
    tm = _tile(R, MM_TILE, LANES)
    tml = _tile(L, MM_TILE, LANES)

    (p01,), _ = _matmul(
        "in_proj_kvu", (2, R // tm), [h2, win],
        [pl.BlockSpec((tm, D), lambda j, i: (i, 0)), pl.BlockSpec((None, D, INS), lambda j, i: (j, 0, 0))],
        [(0, 1, 0, NN)], [jax.ShapeDtypeStruct((2, R, INS), F32)],
        [pl.BlockSpec((None, tm, INS), lambda j, i: (j, i, 0))], _store_all)
    (p27,), (wglu, wbra) = _matmul(
        "in_proj_qg", (6, L // tml), [h2, win],
        [pl.BlockSpec((tml, D), lambda j, i: (i, 0)), pl.BlockSpec((None, D, INS), lambda j, i: (j + 2, 0, 0))],
        [(0, 1, 0, NN)], [jax.ShapeDtypeStruct((6, L, INS), F32)],
        [pl.BlockSpec((None, tml, INS), lambda j, i: (j, i, 0))], _store_all,
        comm=_Gather([bf(w_glu), bf(w_br_attn)]))
    wglu2 = wglu.reshape(W, W)
    wbra2 = wbra.reshape(D, D)

    cos_all, sin_all = _rope_tables(L, Lc)
    cos_l, sin_l = cos_all[:L], sin_all[:L]

    q_rot = _qk_prep("q_prep", p27, 0, HBQ, NQ, L, q_norm_g, cos_l, sin_l)
    k_rot = _qk_prep("k_prep", p01, 0, NKV, NKV, R, k_norm_g, cos_all, sin_all)
    v_hd = _heads_cast("v_heads", p01, 1, NKV, NKV, R)
    attn, (wbrs, wout, wg2) = _attn_fwd(
        q_rot, k_rot, v_hd, QPK, comm=_Gather([bf(w_br_ssm), bf(w_out), bf(w_ffn2_gate)]))
    wout2 = wout.reshape(D, D)

    t_rows = _tile(math.gcd(L, Lc), ROW_TILE, SUBLANES)
    nl, ncx = L // t_rows, Lc // t_rows
    nch = nl + ncx
    ns = G // SLAB_GROUPS
    ssm_prim = (ssm_a_re[0], ssm_a_im[0], ssm_log_dt[0], ssm_b_re[0], ssm_b_im[0])
    _, _, bt_re, bt_im = _ssm_discretize(*ssm_prim)
    pw_re, pw_im = _lambda_powers(ssm_a_re[0], ssm_a_im[0], ssm_log_dt[0], ns)
    bd_re = _block_diag(jnp.swapaxes(bt_re, 2, 3))
    bd_im = _block_diag(jnp.swapaxes(bt_im, 2, 3))
    ct_re = _block_diag(ssm_c_re[0])
    ct_im = _block_diag(-ssm_c_im[0])
    fwd_desc = (False, True)
    adj_desc = (True, False)
    s_waug = _tap_weights(bd_re, bd_im, pw_re, pw_im).astype(BF16)
    s_tab = _carry_tables(pw_re, pw_im, fwd_desc)
    s_cd = jnp.concatenate([jnp.swapaxes(ct_re, 2, 3), jnp.swapaxes(ct_im, 2, 3)], axis=2).astype(BF16)
    s_caug = _tap_weights(ct_re, ct_im, pw_re, -pw_im).astype(BF16)
    s_tabc = _carry_tables(pw_re, -pw_im, adj_desc)
    s_bdt = jnp.concatenate([jnp.swapaxes(bd_re, 2, 3), jnp.swapaxes(bd_im, 2, 3)], axis=2).astype(BF16)
    order = [lambda i: (i + nl) % nch, lambda i: nch - 1 - i]
    order_adj = [lambda i: (nch - 1 - i + nl) % nch, lambda i: i]
    y0, st0, (wu2,) = _ssm_fwd("ssm_fwd0", 0, p01, 1, s_waug, s_tab, s_cd, fwd_desc[0], order[0], t_rows, R,
                               comm=_Gather([bf(w_ffn2_up)]))
    y1, st1, (wd2,) = _ssm_fwd("ssm_fwd1", 1, p01, 1, s_waug, s_tab, s_cd, fwd_desc[1], order[1], t_rows, R,
                               comm=_Gather([bf(w_ffn2_down)]))
    states = [st0, st1]

    tr = _row_tile(L, 0)
    rowW = pl.BlockSpec((tr, W), lambda i: (i, 0))
    vecW = pl.BlockSpec((1, W), lambda i: (0, 0))
    u_lat = pl.BlockSpec((None, tr, W), lambda i: (1, i, 0))

    def ssm_post(i, u, ya, yb, dvec):
        sv = dvec * u + ya + yb
        return [sv, _gelu(sv)], []

    (ssm_out, yg), _, _ = _rowwise(
        "ssm_post", L // tr, [p01, y0, y1, ssm_d], [u_lat, rowW, rowW, vecW],
        [jax.ShapeDtypeStruct((L, W), F32), jax.ShapeDtypeStruct((L, W), BF16)], [rowW, rowW], [], ssm_post)

    tnw = _tile(W, MM_TILE, LANES)

    def glu_epilogue(accs, ins, outs, pids):
        z = accs[0] + ins[3][...]
        outs[0][...] = z
        outs[1][...] = (_gelu(ins[2][...]) * _sigmoid(z)).astype(BF16)

    (z_glu, y2), _ = _matmul(
        "glu", (L // tml, W // tnw), [yg, wglu2, ssm_out, b_glu],
        [pl.BlockSpec((tml, W), lambda i, n: (i, 0)), pl.BlockSpec((W, tnw), lambda i, n: (0, n)),
         pl.BlockSpec((tml, tnw), lambda i, n: (i, n)), pl.BlockSpec((1, tnw), lambda i, n: (0, n))],
        [(0, 1, 0, NN)], [jax.ShapeDtypeStruct((L, W), F32), jax.ShapeDtypeStruct((L, W), BF16)],
        [pl.BlockSpec((tml, tnw), lambda i, n: (i, n))] * 2, glu_epilogue)

    tnd = _tile(D, MM_TILE, LANES)
    out_ld = pl.BlockSpec((tml, tnd), lambda i, n: (i, n))
    (br_a,), _ = _matmul(
        "br_attn", (L // tml, D // tnd), [attn, wbra2],
        [pl.BlockSpec((tml, D), lambda i, n: (i, 0)), pl.BlockSpec((D, tnd), lambda i, n: (0, n))],
        [(0, 1, 0, NN)], [jax.ShapeDtypeStruct((L, D), F32)], [out_ld], _store_all)

    cb = wbrs.shape[-1]
    gpb = INS // cb

    def gate_spec(first_shard):
        return pl.BlockSpec((None, tml, cb), lambda i, j: (first_shard + j // gpb, i, j % gpb))

    def merge_epilogue(accs, ins, outs, pids):
        br = accs[0]
        outs[0][...] = br
        outs[1][...] = (_sigmoid(ins[2][...]) * ins[4][...] + _sigmoid(ins[3][...]) * br).astype(BF16)

    col_blk = pl.BlockSpec((tml, cb), lambda i, j: (i, j))
    (br_s, merged), _ = _matmul(
        "br_ssm_merge", (L // tml, N_DEV), [y2, wbrs, p27, p27, br_a],
        [pl.BlockSpec((tml, W), lambda i, j: (i, 0)), pl.BlockSpec((None, W, cb), lambda i, j: (j, 0, 0)),
         gate_spec(2), gate_spec(4), col_blk],
        [(0, 1, 0, NN)], [jax.ShapeDtypeStruct((L, D), F32), jax.ShapeDtypeStruct((L, D), BF16)],
        [col_blk, col_blk], merge_epilogue)

    def out_epilogue(accs, ins, outs, pids):
        outs[0][...] = accs[0]
        outs[1][...] = ins[2][...] + ins[3][G2:G2 + 1, :] * accs[0]

    (mix, x2_), _ = _matmul(
        "out_proj", (L // tml, D // tnd), [merged, wout2, xc1, tab2],
        [pl.BlockSpec((tml, D), lambda i, n: (i, 0)), pl.BlockSpec((D, tnd), lambda i, n: (0, n)), out_ld,
         pl.BlockSpec((tab2.shape[0], tnd), lambda i, n: (0, n))],
        [(0, 1, 0, NN)], [jax.ShapeDtypeStruct((L, D), F32)] * 2, [out_ld, out_ld], out_epilogue)

    h3 = _norm_mod_fwd("nm3_fwd", x2_, tab3, GAM3, (SH3, SH3), (SC3, SC3), L, 0)
    a3, b3, s3, _ = _ffn_up("ffn2", h3, wg2, wu2)
    f3, x3, _ = _ffn_down("ffn2", s3, wd2, x2_, tab2, (G3, G3), L)

    trd = _row_tile(L, 0)
    rowD = pl.BlockSpec((trd, D), lambda i: (i, 0))

    def loss_fn(i, yv, t):
        err = yv - t
        return [err * (1.0 / D)], [_colsum(err * err)]

    (dx3,), (sq,), _ = _rowwise("loss", L // trd, [x3, tgt], [rowD, rowD],
                                [jax.ShapeDtypeStruct((L, D), F32)], [rowD], [D], loss_fn)
    loss = lax.psum(0.5 * jnp.sum(sq) / D, ("x", "y", "c"))

    core = lax.axis_index("c").astype(jnp.int32).reshape(1)
    chip = (2 * lax.axis_index("x") + lax.axis_index("y")).astype(jnp.int32).reshape(1)

    def pair_sums(tag, grads, halves):
        return [_pair_sum("pair_%s%d" % (tag, k), g_, h_, core) for k, (g_, h_) in enumerate(zip(grads, halves))]

    df3, (dg3, _) = _gate_bwd("gate3_bwd", dx3, f3, tab3, (G3, G3), 0.5, L, 0)
    dwd2, _ = _ffn_dwd("ffn2b", s3, df3)
    da3, db3, half_wd2 = _ffn_ds("ffn2b", df3, wd2, a3, b3, comm=_SiblingSwap([dwd2]))
    (p_wd2,) = pair_sums("wd2", [dwd2], half_wd2)
    dwg2, dwu2, (l_wd2,) = _ffn_dwgu("ffn2b", h3, da3, db3, comm=_ChipExchange([p_wd2]))
    dh3, half_wgu2 = _ffn_dh("ffn2b", da3, db3, wg2, wu2, comm=_SiblingSwap([dwg2, dwu2]))
    p_wg2, p_wu2 = pair_sums("wgu2", [dwg2, dwu2], half_wgu2)
    dx2, (dsh3, dsc3, _, _, dgam3) = _norm_mod_bwd("nm3_bwd", x2_, dh3, tab3, GAM3, (SC3, SC3), L, 0, dres=dx3)

    dmix, (dg2, _) = _gate_bwd("gate2_bwd", dx2, mix, tab3, (G2, G2), 1.0, L, 0)

    def dmerged_epilogue(accs, ins, outs, pids):
        dm = accs[0]
        ga, gs = _sigmoid(ins[2][...]), _sigmoid(ins[3][...])
        outs[0][...] = (ga * dm).astype(BF16)
        outs[1][...] = (gs * dm).astype(BF16)
        outs[2][...] = (dm * ins[4][...] * ga * (1.0 - ga)).astype(BF16)
        outs[3][...] = (dm * ins[5][...] * gs * (1.0 - gs)).astype(BF16)

    dgate_spec = pl.BlockSpec((None, tml, cb), lambda i, j: (j // gpb, i, j % gpb))
    (d_br_a, d_br_s, dg_a, dg_s), _ = _matmul(
        "dmerged", (L // tml, N_DEV), [dmix, wout2, p27, p27, br_a, br_s],
        [pl.BlockSpec((tml, D), lambda i, j: (i, 0)), pl.BlockSpec((cb, D), lambda i, j: (j, 0)),
         gate_spec(2), gate_spec(4), col_blk, col_blk],
        [(0, 1, 0, NT)],
        [jax.ShapeDtypeStruct((L, D), BF16)] * 2 + [jax.ShapeDtypeStruct((2, L, INS), BF16)] * 2,
        [col_blk, col_blk, dgate_spec, dgate_spec], dmerged_epilogue)

    def wgrad(name, a_mat, b_mat, tmo, tno):
        ka, ma = a_mat.shape
        _, nb_ = b_mat.shape
        return _matmul(
            name, (ma // tmo, nb_ // tno), [a_mat, b_mat],
            [pl.BlockSpec((ka, tmo), lambda m, n: (0, m)), pl.BlockSpec((ka, tno), lambda m, n: (0, n))],
            [(0, 1, 0, TN)], [jax.ShapeDtypeStruct((ma, nb_), BF16)],
            [pl.BlockSpec((tmo, tno), lambda m, n: (m, n))], _store_all)[0][0]

    dwout = wgrad("dw_out", merged, dmix, tnd, tnd)
    dwbra = wgrad("dw_br_attn", attn, d_br_a, tnd, tnd)
    (d_attn,), _ = _matmul(
        "d_attn", (L // tml, D // tnd), [d_br_a, wbra2],
        [pl.BlockSpec((tml, D), lambda i, n: (i, 0)), pl.BlockSpec((tnd, D), lambda i, n: (n, 0))],
        [(0, 1, 0, NT)], [jax.ShapeDtypeStruct((L, D), BF16)], [out_ld], _store_all)

    (dwbrs,), _ = _matmul(
        "dw_br_ssm", (N_DEV,), [y2, d_br_s],
        [pl.BlockSpec((L, W), lambda j: (0, 0)), pl.BlockSpec((L, cb), lambda j: (0, j))],
        [(0, 1, 0, TN)], [jax.ShapeDtypeStruct((N_DEV, W, cb), BF16)],
        [pl.BlockSpec((None, W, cb), lambda j: (j, 0, 0))], _store_all)

    def dy2_epilogue(accs, ins, outs, pids):
        dy2 = accs[0]
        sg = _sigmoid(ins[2][...])
        outs[0][...] = dy2 * sg
        outs[1][...] = (dy2 * _gelu(ins[3][...]) * sg * (1.0 - sg)).astype(BF16)

    wn_blk = pl.BlockSpec((tml, tnw), lambda i, n, k: (i, n))
    (dyg1, dz), _ = _matmul(
        "d_y2", (L // tml, W // tnw, N_DEV), [d_br_s, wbrs, z_glu, ssm_out],
        [pl.BlockSpec((tml, cb), lambda i, n, k: (i, k)), pl.BlockSpec((None, tnw, cb), lambda i, n, k: (k, n, 0)),
         wn_blk, wn_blk],
        [(0, 1, 0, NT)], [jax.ShapeDtypeStruct((L, W), F32), jax.ShapeDtypeStruct((L, W), BF16)],
        [wn_blk, wn_blk], dy2_epilogue, acc_shapes=[(tml, tnw)], nk=N_DEV)

    dwglu = wgrad("dw_glu", yg, dz, tnw, tnw)
    mix_grads = [dwout.reshape(N_DEV, D // N_DEV, D), dwbra.reshape(N_DEV, D // N_DEV, D), dwbrs,
                 dwglu.reshape(N_DEV, W // N_DEV, W)]

    def dssm_epilogue(accs, ins, outs, pids):
        outs[0][...] = (accs[0] + ins[2][...]) * _gelu_grad(ins[3][...])

    wn2 = pl.BlockSpec((tml, tnw), lambda i, n: (i, n))
    (dssm,), _ = _matmul(
        "d_ssm", (L // tml, W // tnw), [dz, wglu2, dyg1, ssm_out],
        [pl.BlockSpec((tml, W), lambda i, n: (i, 0)), pl.BlockSpec((tnw, W), lambda i, n: (n, 0)), wn2, wn2],
        [(0, 1, 0, NT)], [jax.ShapeDtypeStruct((L, W), F32)], [wn2], dssm_epilogue)

    dssm_all = jnp.concatenate([dssm, jnp.zeros((Lc, W), F32)], axis=0)
    du0, dbd0, dcd0, dlam0, (l_wg2, *half_mix) = _ssm_bwd(
        "ssm_bwd0", 0, dssm_all, p01, 1, states[0], s_caug, s_tabc, s_bdt, adj_desc[0], order_adj[0], t_rows, R,
        comm=_Both([_ChipExchange([p_wg2]), _SiblingSwap(mix_grads)]))
    p_wout, p_wbra, p_wbrs, p_wglu = pair_sums("mix", mix_grads, half_mix)
    du1, dbd1, dcd1, dlam1, (l_wu2,) = _ssm_bwd(
        "ssm_bwd1", 1, dssm_all, p01, 1, states[1], s_caug, s_tabc, s_bdt, adj_desc[1], order_adj[1], t_rows, R,
        comm=_ChipExchange([p_wu2]))

    trr = _row_tile(L, Lc)
    nlt = L // trr
    rowR = pl.BlockSpec((trr, W), lambda i: (i, 0))

    def du_fn(i, dua, dub, dsv, dvec, u):
        lat = (i < nlt).astype(F32)
        return [dua + dub + lat * (dvec * dsv)], [lat * _colsum(dsv * u)]

    (du_all,), (d_ssm_d,), _ = _rowwise(
        "du_combine", R // trr, [du0, du1, dssm_all, ssm_d, p01],
        [rowR, rowR, rowR, pl.BlockSpec((1, W), lambda i: (0, 0)), pl.BlockSpec((None, trr, W), lambda i: (1, i, 0))],
        [jax.ShapeDtypeStruct((R, W), BF16)], [rowR], [W], du_fn)

    def dz_sum(i, dzv):
        return [], [_colsum(dzv.astype(F32))]

    _, (d_b_glu,), _ = _rowwise("db_glu", L // tr, [dz], [rowW], [], [], [W], dz_sum)

    dq_rot, dk_rot, dv_hd, (l_wout, l_wbra, l_wbrs, l_wglu) = _attn_bwd(
        q_rot, k_rot, v_hd, d_attn, QPK, comm=_ChipExchange([p_wout, p_wbra, p_wbrs, p_wglu]))
    dq_pre, d_qg = _qk_prep_bwd("q_prep_bwd", dq_rot, p27, 0, HBQ, NQ, L, q_norm_g, cos_l, sin_l)
    dk_pre, d_kg = _qk_prep_bwd("k_prep_bwd", dk_rot, p01, 0, NKV, NKV, R, k_norm_g, cos_all, sin_all)
    dv_pre = _heads_merge("dv_merge", dv_hd)

    def lat_blocks(a):
        return jnp.pad(a, ((0, 0), (0, Lc), (0, 0)))

    dq_blocks = jnp.transpose(dq_pre.reshape(L, 2, INS), (1, 0, 2))
    dp = jnp.concatenate([
        jnp.concatenate([dk_pre, dv_pre], axis=1)[None], du_all[None],
        lat_blocks(dq_blocks), lat_blocks(dg_a), lat_blocks(dg_s)], axis=0)

    tmo = _tile(D, MM_TILE, LANES)
    (dwin,), _ = _matmul(
        "dw_in", (N_DEV, D // tmo), [h2, dp],
        [pl.BlockSpec((R, tmo), lambda j, m: (0, m)), pl.BlockSpec((None, R, INS), lambda j, m: (j, 0, 0))],
        [(0, 1, 0, TN)], [jax.ShapeDtypeStruct((N_DEV, D, INS), BF16)],
        [pl.BlockSpec((None, tmo, INS), lambda j, m: (j, m, 0))], _store_all)
    tnh = _tile(D, MM_TILE_NT, LANES)
    (dh2,), half_win = _matmul(
        "d_h2", (R // tm, D // tnh), [dp, win],
        [pl.BlockSpec((N_DEV, tm, INS), lambda i, n: (0, i, 0)),
         pl.BlockSpec((N_DEV, tnh, INS), lambda i, n: (0, n, 0))],
        [(0, 1, 0, NT, N_DEV)], [jax.ShapeDtypeStruct((R, D), F32)], [pl.BlockSpec((tm, tnh), lambda i, n: (i, n))],
        _store_all, comm=_SiblingSwap([dwin]))
    (p_win,) = pair_sums("win", [dwin], half_win)
    dxc1, (dsh2, dsc2, dmc3, dmc4, dgam2) = _norm_mod_bwd(
        "nm2_bwd", xc1, dh2, tab3, GAM2, (SC2, MC4), L, Lc, dres=dx2)

    df1, (dg1, dmc2) = _gate_bwd("gate1_bwd", dxc1, f1, tab3, (G1, MC2), 0.5, L, Lc)
    dwd1, _ = _ffn_dwd("ffn1b", s1, df1)
    da1, db1, (l_win, *half_wd1) = _ffn_ds(
        "ffn1b", df1, wd1, a1, b1, comm=_Both([_ChipExchange([p_win]), _SiblingSwap([dwd1])]))
    (p_wd1,) = pair_sums("wd1", [dwd1], half_wd1)
    dwg1, dwu1, (l_wd1,) = _ffn_dwgu("ffn1b", h1, da1, db1, comm=_ChipExchange([p_wd1]))
    dh1, half_wgu1 = _ffn_dh("ffn1b", da1, db1, wg1, wu1, comm=_SiblingSwap([dwg1, dwu1]))
    p_wg1, p_wu1 = pair_sums("wgu1", [dwg1, dwu1], half_wgu1)
    dxc0, (dsh1, dsc1, dmc0, dmc1, dgam1) = _norm_mod_bwd(
        "nm1_bwd", xc0, dh1, tab3, GAM1, (SC1, MC1), L, Lc, dres=dxc1)
    grad_x = dxc0[:L][None]

    dmod_lat = jnp.concatenate([dsh1, dsc1, dg1, dsh2, dsc2, dg2, dsh3, dsc3, dg3], axis=1)
    dmod_ctx = jnp.concatenate([dmc0, dmc1, dmc2, dmc3, dmc4, jnp.zeros((1, 4 * D), F32)], axis=1)
    dmod_pack = at_row(dmod_lat, 0, SUBLANES) + at_row(dmod_ctx, 1, SUBLANES)
    (dmod_g,) = _exchange_only("ag_dmod", _Gather([dmod_pack]))
    dmod_all = dmod_g.reshape(N_DEV * SUBLANES, 9 * D)
    dmod_cols = lax.dynamic_slice_in_dim(dmod_all, me * MODW, MODW, axis=1)
    (g_wmod, dl_wmod, nm_wmod, nv_wmod, dsilu), (l_wg1,) = _mod_bwd_adam(
        cs, dmod_cols, w_mod[0], m_w_mod[0], v_w_mod[0], comm=_ChipExchange([p_wg1]))
    sg_cc = jax.nn.sigmoid(c_ctx)
    d_c_ctx = dsilu[8] * (sg_cc * (1.0 + c_ctx * (1.0 - sg_cc)))
    g_bmod, dl_bmod, nm_bmod, nv_bmod = _bias_adam(dmod_all, b_mod, m_b_mod, v_b_mod)

    dbd, dcd, dlam = jnp.stack([dbd0, dbd1]), jnp.stack([dcd0, dcd1]), jnp.stack([dlam0, dlam1])
    dbt_re = jnp.swapaxes(_block_diag_extract(dbd[..., :SW], E, P), 2, 3)
    dbt_im = jnp.swapaxes(_block_diag_extract(dbd[..., SW:], E, P), 2, 3)
    dl_re, dl_im = dlam[:, :, 0, :SW].reshape(2, G, P), dlam[:, :, 0, SW:].reshape(2, G, P)
    _, vjp = jax.vjp(_ssm_discretize, *ssm_prim)
    d_a_re, d_a_im, d_ldt, d_b_re, d_b_im = vjp((dl_re, dl_im, dbt_re, dbt_im))
    d_c_re = jnp.swapaxes(_block_diag_extract(dcd[:, :, :SW, :], P, E), 2, 3)
    d_c_im = -jnp.swapaxes(_block_diag_extract(dcd[:, :, SW:, :], P, E), 2, 3)

    dgam_all = jnp.concatenate([dgam1, dgam2, dgam3], axis=0)
    small_g = [d_c_ctx, d_qg, d_kg, d_a_re, d_a_im, d_ldt, d_b_re, d_b_im, d_c_re, d_c_im, d_ssm_d, d_b_glu,
               dgam_all]
    small_w = [c_ctx, q_norm_g, k_norm_g, ssm_a_re, ssm_a_im, ssm_log_dt, ssm_b_re, ssm_b_im, ssm_c_re, ssm_c_im,
               ssm_d, b_glu, ng_full]
    small_m = [m_c_ctx, m_q_norm_g, m_k_norm_g, m_ssm_a_re, m_ssm_a_im, m_ssm_log_dt, m_ssm_b_re, m_ssm_b_im,
               m_ssm_c_re, m_ssm_c_im, m_ssm_d, m_b_glu, m_ng_full]
    small_v = [v_c_ctx, v_q_norm_g, v_k_norm_g, v_ssm_a_re, v_ssm_a_im, v_ssm_log_dt, v_ssm_b_re, v_ssm_b_im,
               v_ssm_c_re, v_ssm_c_im, v_ssm_d, v_b_glu, v_ng_full]
    small_shapes = [a.shape for a in small_w]
    n_rows = sum(-(-math.prod(s) // LANES) for s in small_shapes)
    n_rows = -(-n_rows // 256) * 256
    small_parts, l_wu1 = _exchange_only(
        "ag_small_grads", _Both([_Gather([_pack(small_g, n_rows)]), _ChipExchange([p_wu1])]))
    small_out = _sum_adam("small_adam", small_parts, _pack(small_w, n_rows), _pack(small_m, n_rows),
                          _pack(small_v, n_rows))
    sm_g, sm_dl, sm_m, sm_v = [_unpack(o, small_shapes) for o in small_out]

    def my_norm_cols(a):
        return lax.dynamic_slice_in_dim(a, me * dn, dn, axis=1)[None]

    for lst in (sm_g, sm_dl, sm_m, sm_v):
        lst[-1] = my_norm_cols(lst[-1])

    landed = [l_wg1, l_wu1, l_wd1, l_win, l_wglu, l_wbra, l_wbrs, l_wout, l_wg2, l_wu2, l_wd2]
    pairs = [p_wg1, p_wu1, p_wd1, p_win, p_wglu, p_wbra, p_wbrs, p_wout, p_wg2, p_wu2, p_wd2]
    big_w = [w_ffn1_gate, w_ffn1_up, w_ffn1_down, w_in, w_glu, w_br_attn, w_br_ssm, w_out, w_ffn2_gate, w_ffn2_up,
             w_ffn2_down]
    big_m = [m_w_ffn1_gate, m_w_ffn1_up, m_w_ffn1_down, m_w_in, m_w_glu, m_w_br_attn, m_w_br_ssm, m_w_out,
             m_w_ffn2_gate, m_w_ffn2_up, m_w_ffn2_down]
    big_v = [v_w_ffn1_gate, v_w_ffn1_up, v_w_ffn1_down, v_w_in, v_w_glu, v_w_br_attn, v_w_br_ssm, v_w_out,
             v_w_ffn2_gate, v_w_ffn2_up, v_w_ffn2_down]
    big_names = ["ffn1_gate", "ffn1_up", "ffn1_down", "in", "glu", "br_attn", "br_ssm", "out", "ffn2_gate",
                 "ffn2_up", "ffn2_down"]
    big_out = [[o[None] for o in _owner_adam("adam_" + nm, p, l_, chip, w_[0], m_[0], v_[0])]
               for nm, p, l_, w_, m_, v_ in zip(big_names, pairs, landed, big_w, big_m, big_v)]

    def leaf(kind):
        sm = (sm_g, sm_dl, sm_m, sm_v)[kind]
        mod = (g_wmod, dl_wmod, nm_wmod, nv_wmod)[kind][None]
        bmod = (g_bmod, dl_bmod, nm_bmod, nv_bmod)[kind]
        big = [b[kind] for b in big_out]
        (c_ctx_, qg_, kg_, a_re_, a_im_, ldt_, b_re_, b_im_, c_re_, c_im_, sd_, bglu_, ng_) = sm
        return [c_ctx_, mod, bmod, ng_, big[0], big[1], big[2], big[3], qg_, kg_, a_re_, a_im_, ldt_, b_re_, b_im_,
                c_re_, c_im_, sd_, big[4], bglu_, big[5], big[6], big[7], big[8], big[9], big[10]]

    return tuple([loss, grad_x] + leaf(0) + leaf(1) + leaf(2) + leaf(3))
```

```python
import math

import jax
import jax.numpy as jnp
import numpy as np
from jax import lax
from jax.experimental import pallas as pl
from jax.experimental.pallas import tpu as pltpu

F32 = jnp.float32
BF16 = jnp.bfloat16

N_DEV = 8
N_CHIPS = 4
LANES = 128
SUBLANES = 8
PACKED_SUBLANES = 16
VMEM_LIMIT = 56 * 1024 * 1024
MM_TILE = 512
MM_TILE_NT = 256
ROW_TILE = 256
HEAD_ROW_TILE = 512
ATTN_BWD_HEADS = 2
ADAM_BLOCK_BYTES = 4 * 1024 * 1024
ADAM_GROUP_BLOCK_BYTES = 1024 * 1024

NORM_EPS = 1e-6
GRID_W = 64
ROPE_THETA = 10000.0
SCAN_TAPS = SUBLANES
SLAB_GROUPS = 8

ADAM_LR = 0.001
ADAM_B1 = 0.9
ADAM_B2 = 0.999
ADAM_EPS = 1e-08
ADAM_WD = 0.01
ADAM_STEP = 10

NN = (((1,), (0,)), ((), ()))
NT = (((1,), (1,)), ((), ()))
TN = (((0,), (0,)), ((), ()))

MESH = pl.DeviceIdType.MESH
ANY = pl.BlockSpec(memory_space=pl.ANY)


def _tile(n, cap, align):
    best = None
    for t in range(align, min(n, cap) + 1, align):
        if n % t == 0:
            best = t
    return n if best is None else best


def _params(n_grid):
    return pltpu.CompilerParams(dimension_semantics=("arbitrary",) * n_grid, vmem_limit_bytes=VMEM_LIMIT)


def _sigmoid(x):
    return 1.0 / (1.0 + jnp.exp(-x))


GELU_K = math.sqrt(2.0 / math.pi)
GELU_C = 0.044715


def _gelu(x):
    return 0.5 * x * (1.0 + jnp.tanh(GELU_K * (x + GELU_C * x * x * x)))


def _gelu_grad(x):
    t = jnp.tanh(GELU_K * (x + GELU_C * x * x * x))
    return 0.5 * (1.0 + t) + 0.5 * x * (1.0 - t * t) * GELU_K * (1.0 + 3.0 * GELU_C * x * x)


def _adamw(w, g, m, v):
    m2 = ADAM_B1 * m + (1.0 - ADAM_B1) * g
    v2 = ADAM_B2 * v + (1.0 - ADAM_B2) * (g * g)
    m_hat = m2 / (1.0 - ADAM_B1 ** ADAM_STEP)
    v_hat = v2 / (1.0 - ADAM_B2 ** ADAM_STEP)
    delta = -ADAM_LR * (m_hat / (jnp.sqrt(v_hat) + ADAM_EPS) + ADAM_WD * w)
    return delta, m2, v2


def _position():
    return lax.axis_index("x"), lax.axis_index("y"), lax.axis_index("c")


class _Gather:
    def __init__(self, arrays):
        self.arrays = list(arrays)
        n = len(self.arrays)
        self.out_shapes = [jax.ShapeDtypeStruct((N_DEV,) + a.shape, a.dtype) for a in self.arrays]
        self.scratch = [pltpu.SemaphoreType.DMA((n, 7)), pltpu.SemaphoreType.DMA((n, 7)),
                        pltpu.SemaphoreType.DMA((n,))]

    def _plan(self, ins, outs, sems):
        send, recv, local = sems
        x, y, c = _position()
        me, sibling = (x, y, c), (x, y, 1 - c)
        chips = [(1 - x, y), (x, 1 - y), (1 - x, 1 - y)]

        def slot(a, p):
            return outs[a].at[4 * p[0] + 2 * p[1] + p[2]]

        def copy(a, k, block, to, src=None):
            dst = slot(a, block)
            return pltpu.make_async_remote_copy(
                src_ref=dst if src is None else src, dst_ref=dst,
                send_sem=send.at[a, k], recv_sem=recv.at[a, k], device_id=to, device_id_type=MESH)

        mine = [pltpu.make_async_copy(ins[a], slot(a, me), local.at[a]) for a in range(len(ins))]
        return me, sibling, chips, c, copy, mine

    def start(self, ins, outs, sems):
        me, sibling, chips, c, copy, mine = self._plan(ins, outs, sems)
        for cp in mine:
            cp.start()
        for a in range(len(ins)):
            copy(a, 0, me, sibling, src=ins[a]).start()
            for j, chip in enumerate(chips):
                copy(a, 1 + j, me, (*chip, c), src=ins[a]).start()

    def finish(self, ins, outs, sems):
        me, sibling, chips, c, copy, mine = self._plan(ins, outs, sems)
        n = len(ins)
        for j, chip in enumerate(chips):
            for a in range(n):
                copy(a, 1 + j, (*chip, c), me).wait_recv()
                copy(a, 4 + j, (*chip, c), sibling).start()
        for a in range(n):
            copy(a, 0, sibling, me).wait_recv()
        for j, chip in enumerate(chips):
            for a in range(n):
                copy(a, 4 + j, (*chip, 1 - c), me).wait_recv()
        for a in range(n):
            copy(a, 0, me, sibling, src=ins[a]).wait_send()
            for j, chip in enumerate(chips):
                copy(a, 1 + j, me, (*chip, c), src=ins[a]).wait_send()
                copy(a, 4 + j, (*chip, c), sibling).wait_send()
        for cp in mine:
            cp.wait()


class _SiblingSwap:
    def __init__(self, arrays):
        self.arrays = list(arrays)
        n = len(self.arrays)
        self.out_shapes = [jax.ShapeDtypeStruct((N_CHIPS,) + a.shape[1:], a.dtype) for a in self.arrays]
        self.scratch = [pltpu.SemaphoreType.DMA((n, N_CHIPS)), pltpu.SemaphoreType.DMA((n, N_CHIPS))]

    def _plan(self, ins, outs, sems):
        send, recv = sems
        x, y, c = _position()
        return [pltpu.make_async_remote_copy(
            src_ref=ins[a].at[2 * j + 1 - c], dst_ref=outs[a].at[j],
            send_sem=send.at[a, j], recv_sem=recv.at[a, j], device_id=(x, y, 1 - c), device_id_type=MESH)
            for a in range(len(ins)) for j in range(N_CHIPS)]

    def start(self, ins, outs, sems):
        for cp in self._plan(ins, outs, sems):
            cp.start()

    def finish(self, ins, outs, sems):
        copies = self._plan(ins, outs, sems)
        for cp in copies:
            cp.wait_recv()
        for cp in copies:
            cp.wait_send()


class _ChipExchange:
    def __init__(self, arrays):
        self.arrays = list(arrays)
        n = len(self.arrays)
        self.out_shapes = [jax.ShapeDtypeStruct((N_CHIPS - 1,) + a.shape[1:], a.dtype) for a in self.arrays]
        self.scratch = [pltpu.SemaphoreType.DMA((n, N_CHIPS - 1)), pltpu.SemaphoreType.DMA((n, N_CHIPS - 1))]

    def _plan(self, ins, outs, sems):
        send, recv = sems
        x, y, c = _position()
        copies = []
        for r in range(1, N_CHIPS):
            px, py = x ^ (r >> 1), y ^ (r & 1)
            for a in range(len(ins)):
                copies.append(pltpu.make_async_remote_copy(
                    src_ref=ins[a].at[2 * px + py], dst_ref=outs[a].at[r - 1],
                    send_sem=send.at[a, r - 1], recv_sem=recv.at[a, r - 1],
                    device_id=(px, py, c), device_id_type=MESH))
        return copies

    def start(self, ins, outs, sems):
        for cp in self._plan(ins, outs, sems):
            cp.start()

    def finish(self, ins, outs, sems):
        copies = self._plan(ins, outs, sems)
        for cp in copies:
            cp.wait_recv()
        for cp in copies:
            cp.wait_send()


class _Both:
    def __init__(self, comms):
        self.comms = list(comms)
        self.arrays = [a for cm in self.comms for a in cm.arrays]
        self.out_shapes = [s for cm in self.comms for s in cm.out_shapes]
        self.scratch = [s for cm in self.comms for s in cm.scratch]

    def _split(self, ins, outs, sems):
        i = o = s = 0
        for cm in self.comms:
            ni, no, nsem = len(cm.arrays), len(cm.out_shapes), len(cm.scratch)
            yield cm, ins[i:i + ni], outs[o:o + no], sems[s:s + nsem]
            i, o, s = i + ni, o + no, s + nsem

    def start(self, ins, outs, sems):
        for cm, i, o, s in self._split(ins, outs, sems):
            cm.start(i, o, s)

    def finish(self, ins, outs, sems):
        for cm, i, o, s in self._split(ins, outs, sems):
            cm.finish(i, o, s)


def _host_call(body, *, name, grid, operands, in_specs, out_shape, out_specs, scratch_shapes=(), comm=None,
               prefetch=()):
    grid = tuple(grid)
    n_pre, n_in, n_out, n_scr = len(prefetch), len(operands), len(out_shape), len(scratch_shapes)
    nc_in, nc_out = (len(comm.arrays), len(comm.out_shapes)) if comm else (0, 0)
    all_in = list(in_specs) + [ANY] * nc_in
    all_out = list(out_specs) + [ANY] * nc_out
    all_scr = list(scratch_shapes) + (list(comm.scratch) if comm else [])
    all_shape = list(out_shape) + (list(comm.out_shapes) if comm else [])
    kwargs = dict(name=name, compiler_params=_params(len(grid)), out_shape=all_shape)
    if n_pre:
        kwargs["grid_spec"] = pltpu.PrefetchScalarGridSpec(
            num_scalar_prefetch=n_pre, grid=grid, in_specs=all_in, out_specs=all_out, scratch_shapes=all_scr)
    else:
        kwargs.update(in_specs=all_in, out_specs=all_out, scratch_shapes=all_scr)
        if grid:
            kwargs["grid"] = grid
    args = list(prefetch) + list(operands) + (list(comm.arrays) if comm else [])
    if comm is None:
        return list(pl.pallas_call(body, **kwargs)(*args)), []

    def hosted(*refs):
        bounds = [0, n_pre, n_pre + n_in]
        for n in (nc_in, n_out, nc_out, n_scr):
            bounds.append(bounds[-1] + n)
        bounds.append(len(refs))
        pre, ins, cins, outs, couts, scr, sems = [refs[a:b] for a, b in zip(bounds[:-1], bounds[1:])]
        if not grid:
            comm.start(cins, couts, sems)
            body(*pre, *ins, *outs, *scr)
            comm.finish(cins, couts, sems)
            return
        first, last = None, None
        for ax, size in enumerate(grid):
            pid = pl.program_id(ax)
            f, l = pid == 0, pid == size - 1
            first = f if first is None else jnp.logical_and(first, f)
            last = l if last is None else jnp.logical_and(last, l)

        @pl.when(first)
        def _():
            comm.start(cins, couts, sems)

        body(*pre, *ins, *outs, *scr)

        @pl.when(last)
        def _():
            comm.finish(cins, couts, sems)

    res = pl.pallas_call(hosted, **kwargs)(*args)
    return list(res[:n_out]), list(res[n_out:])


def _exchange_only(name, comm):
    def body():
        pass
    return _host_call(body, name=name, grid=(), operands=[], in_specs=[], out_shape=[], out_specs=[], comm=comm)[1]


def _matmul(name, grid, operands, in_specs, pairs, out_shapes, out_specs, epilogue, acc_shapes=(), nk=1,
            prologue=None, comm=None):
    n_in, n_out = len(operands), len(out_shapes)
    prologue = prologue or {}

    def body(*refs):
        ins, outs, accs = refs[:n_in], refs[n_in:n_in + n_out], refs[n_in + n_out:]
        pids = [pl.program_id(ax) for ax in range(len(grid))]

        def operand(i, blk=None):
            v = ins[i][...] if blk is None else ins[i][blk]
            if i in prologue:
                v = prologue[i](v)
            return v.astype(BF16)

        def products():
            vals = {}
            for pair in pairs:
                ai, bi, ci, dn = pair[:4]
                if len(pair) == 5:
                    p = None
                    for blk in range(pair[4]):
                        q = lax.dot_general(operand(ai, blk), operand(bi, blk), dn, preferred_element_type=F32)
                        p = q if p is None else p + q
                else:
                    p = lax.dot_general(operand(ai), operand(bi), dn, preferred_element_type=F32)
                vals[ci] = p if ci not in vals else vals[ci] + p
            return [vals[ci] for ci in sorted(vals)]

        if nk == 1:
            epilogue(products(), ins, outs, pids)
        else:
            k = pids[-1]
            prods = products()

            @pl.when(k == 0)
            def _():
                for acc, p in zip(accs, prods):
                    acc[...] = p

            @pl.when(k > 0)
            def _():
                for acc, p in zip(accs, prods):
                    acc[...] += p

            @pl.when(k == nk - 1)
            def _():
                epilogue([acc[...] for acc in accs], ins, outs, pids)

    return _host_call(
        body, name=name, grid=grid, operands=operands, in_specs=in_specs, out_shape=out_shapes, out_specs=out_specs,
        scratch_shapes=[pltpu.VMEM(s, F32) for s in acc_shapes] if nk > 1 else [], comm=comm)


def _rowwise(name, n_tiles, operands, in_specs, out_shapes, out_specs, red_widths, fn, comm=None):
    n_in, n_out, n_red = len(operands), len(out_shapes), len(red_widths)

    def body(*refs):
        ins, outs, reds = refs[:n_in], refs[n_in:n_in + n_out], refs[n_in + n_out:]
        i = pl.program_id(0)
        vals, sums = fn(i, *[r[...] for r in ins])
        for o, v in zip(outs, vals):
            o[...] = v.astype(o.dtype)
        if n_red:
            @pl.when(i == 0)
            def _():
                for r, s in zip(reds, sums):
                    r[...] = s

            @pl.when(i > 0)
            def _():
                for r, s in zip(reds, sums):
                    r[...] += s

    red_shapes = [jax.ShapeDtypeStruct((1, w), F32) for w in red_widths]
    red_specs = [pl.BlockSpec((1, w), lambda i: (0, 0)) for w in red_widths]
    res, cres = _host_call(
        body, name=name, grid=(n_tiles,), operands=operands, in_specs=in_specs,
        out_shape=list(out_shapes) + red_shapes, out_specs=list(out_specs) + red_specs, comm=comm)
    return res[:n_out], res[n_out:], cres


def _colsum(v):
    return jnp.sum(v, axis=0, keepdims=True)


def _store_all(accs, ins, outs, pids):
    for o, v in zip(outs, accs):
        o[...] = v.astype(o.dtype)


def _row_tile(rows_a, rows_b):
    return _tile(math.gcd(rows_a, rows_b) if rows_b else rows_a, ROW_TILE, SUBLANES)


def _tab_row(d, nlt, rows2):
    return pl.BlockSpec((None, 1, d), lambda i: (jnp.where(i < nlt, rows2[0], rows2[1]), 0, 0))


def _norm_mod_fwd(name, xs, tab, r_gamma, r_shift, r_scale, n_lat, n_ctx):
    rows, d = xs.shape
    tm = _row_tile(n_lat, n_ctx)
    nlt = n_lat // tm

    def fn(i, x, g, sh, sc):
        xh = x * lax.rsqrt(jnp.mean(x * x, axis=-1, keepdims=True) + NORM_EPS)
        return [(xh * g) * (1.0 + sc) + sh], []

    (h,), _, _ = _rowwise(
        name, rows // tm, [xs, tab, tab, tab],
        [pl.BlockSpec((tm, d), lambda i: (i, 0)), _tab_row(d, nlt, (r_gamma, r_gamma)), _tab_row(d, nlt, r_shift),
         _tab_row(d, nlt, r_scale)],
        [jax.ShapeDtypeStruct((rows, d), BF16)], [pl.BlockSpec((tm, d), lambda i: (i, 0))], [], fn)
    return h


def _norm_mod_bwd(name, xs, dh, tab, r_gamma, r_scale, n_lat, n_ctx, dres=None):
    rows, d = xs.shape
    tm = _row_tile(n_lat, n_ctx)
    nlt = n_lat // tm
    row = pl.BlockSpec((tm, d), lambda i: (i, 0))

    def fn(i, x, dy, g, sc, *res):
        rstd = lax.rsqrt(jnp.mean(x * x, axis=-1, keepdims=True) + NORM_EPS)
        xh = x * rstd
        dsh = _colsum(dy)
        dsc = _colsum(dy * (xh * g))
        dn = dy * (1.0 + sc)
        dgam = _colsum(dn * xh)
        dxh = dn * g
        dx = rstd * (dxh - xh * jnp.mean(dxh * xh, axis=-1, keepdims=True))
        if res:
            dx = dx + jnp.where(i < nlt, res[0], 0.0)
        lat = (i < nlt).astype(F32)
        return [dx], [dsh * lat, dsc * lat, dsh * (1.0 - lat), dsc * (1.0 - lat), dgam]

    operands = [xs, dh, tab, tab]
    specs = [row, row, _tab_row(d, nlt, (r_gamma, r_gamma)), _tab_row(d, nlt, r_scale)]
    if dres is not None:
        operands.append(dres)
        specs.append(pl.BlockSpec((tm, d), lambda i: (jnp.minimum(i, nlt - 1), 0)))
    (dx,), sums, _ = _rowwise(name, rows // tm, operands, specs,
                              [jax.ShapeDtypeStruct((rows, d), F32)], [row], [d] * 5, fn)
    return dx, sums


def _gate_bwd(name, dx, f, tab, r_gate, coef, n_lat, n_ctx):
    rows, d = dx.shape
    tm = _row_tile(n_lat, n_ctx)
    nlt = n_lat // tm
    row = pl.BlockSpec((tm, d), lambda i: (i, 0))

    def fn(i, dxv, fv, gv):
        dg = _colsum(dxv * fv) * coef
        lat = (i < nlt).astype(F32)
        return [(coef * gv) * dxv], [dg * lat, dg * (1.0 - lat)]

    (df,), sums, _ = _rowwise(
        name, rows // tm, [dx, f, tab],
        [row, row, _tab_row(d, nlt, r_gate)],
        [jax.ShapeDtypeStruct((rows, d), BF16)], [row], [d, d], fn)
    return df, sums


def _select_rows(i, tm, n_lat, v_lat, v_ctx):
    rows = i * tm + lax.broadcasted_iota(jnp.int32, (tm, 1), 0)
    return jnp.where(rows < n_lat, v_lat, v_ctx)


def _ffn_up(tag, h, wg, wu, comm=None):
    rows, d = h.shape
    nb, _, fs = wg.shape
    tm = _tile(rows, MM_TILE, LANES)
    blk = pl.BlockSpec((None, tm, fs), lambda j, i: (j, i, 0))
    wspec = pl.BlockSpec((None, d, fs), lambda j, i: (j, 0, 0))

    def epilogue(accs, ins, outs, pids):
        a, b = accs
        outs[0][...] = a.astype(BF16)
        outs[1][...] = b.astype(BF16)
        outs[2][...] = (a * _sigmoid(a) * b).astype(BF16)

    hid = jax.ShapeDtypeStruct((nb, rows, fs), BF16)
    (a, b, s), cres = _matmul(
        tag + "_up", (nb, rows // tm), [h, wg, wu],
        [pl.BlockSpec((tm, d), lambda j, i: (i, 0)), wspec, wspec],
        [(0, 1, 0, NN), (0, 2, 1, NN)], [hid, hid, hid], [blk, blk, blk], epilogue, comm=comm)
    return a, b, s, cres


def _ffn_down(tag, s, wd, xs, tab2, r_gate, n_lat, comm=None):
    nb, rows, fs = s.shape
    d = wd.shape[-1]
    tm = _tile(rows, MM_TILE, LANES)
    tn = _tile(d, MM_TILE, LANES)

    def epilogue(accs, ins, outs, pids):
        f = accs[0]
        g = ins[3][...]
        gate = _select_rows(pids[0], tm, n_lat, g[r_gate[0]:r_gate[0] + 1, :], g[r_gate[1]:r_gate[1] + 1, :])
        outs[0][...] = f
        outs[1][...] = ins[2][...] + 0.5 * gate * f

    out = jax.ShapeDtypeStruct((rows, d), F32)
    ospec = pl.BlockSpec((tm, tn), lambda i, n: (i, n))
    (f, xo), cres = _matmul(
        tag + "_down", (rows // tm, d // tn), [s, wd, xs, tab2],
        [pl.BlockSpec((nb, tm, fs), lambda i, n: (0, i, 0)), pl.BlockSpec((nb, fs, tn), lambda i, n: (0, 0, n)),
         ospec, pl.BlockSpec((tab2.shape[0], tn), lambda i, n: (0, n))],
        [(0, 1, 0, NN, nb)], [out, out], [ospec, ospec], epilogue, comm=comm)
    return f, xo, cres


def _ffn_ds(tag, df, wd, a, b, comm=None):
    rows, d = df.shape
    nb, fs, _ = wd.shape
    tm = _tile(rows, MM_TILE, LANES)
    blk = pl.BlockSpec((None, tm, fs), lambda j, i: (j, i, 0))

    def epilogue(accs, ins, outs, pids):
        ds = accs[0]
        av = ins[2][...].astype(F32)
        bv = ins[3][...].astype(F32)
        sg = _sigmoid(av)
        outs[0][...] = (ds * bv * (sg * (1.0 + av * (1.0 - sg)))).astype(BF16)
        outs[1][...] = (ds * (av * sg)).astype(BF16)

    hid = jax.ShapeDtypeStruct((nb, rows, fs), BF16)
    (da, db), cres = _matmul(
        tag + "_ds", (nb, rows // tm), [df, wd, a, b],
        [pl.BlockSpec((tm, d), lambda j, i: (i, 0)), pl.BlockSpec((None, fs, d), lambda j, i: (j, 0, 0)), blk, blk],
        [(0, 1, 0, NT)], [hid, hid], [blk, blk], epilogue, comm=comm)
    return da, db, cres


def _ffn_dwd(tag, s, df, comm=None):
    nb, rows, fs = s.shape
    d = df.shape[-1]
    tn = _tile(d, MM_TILE, LANES)
    (dwd,), cres = _matmul(
        tag + "_dwd", (nb, d // tn), [s, df],
        [pl.BlockSpec((None, rows, fs), lambda j, n: (j, 0, 0)), pl.BlockSpec((rows, tn), lambda j, n: (0, n))],
        [(0, 1, 0, TN)], [jax.ShapeDtypeStruct((nb, fs, d), BF16)],
        [pl.BlockSpec((None, fs, tn), lambda j, n: (j, 0, n))], _store_all, comm=comm)
    return dwd, cres


def _ffn_dwgu(tag, h, da, db, comm=None):
    rows, d = h.shape
    nb, _, fs = da.shape
    tmo = _tile(d, MM_TILE, LANES)
    full = pl.BlockSpec((None, rows, fs), lambda j, m: (j, 0, 0))
    wshape = jax.ShapeDtypeStruct((nb, d, fs), BF16)
    wblk = pl.BlockSpec((None, tmo, fs), lambda j, m: (j, m, 0))
    (dwg, dwu), cres = _matmul(
        tag + "_dwgu", (nb, d // tmo), [h, da, db],
        [pl.BlockSpec((rows, tmo), lambda j, m: (0, m)), full, full],
        [(0, 1, 0, TN), (0, 2, 1, TN)], [wshape, wshape], [wblk, wblk], _store_all, comm=comm)
    return dwg, dwu, cres


def _ffn_dh(tag, da, db, wg, wu, comm=None):
    nb, rows, fs = da.shape
    d = wg.shape[1]
    tm = _tile(rows, MM_TILE, LANES)
    tn = _tile(d, MM_TILE_NT, LANES)
    aspec = pl.BlockSpec((nb, tm, fs), lambda i, n: (0, i, 0))
    wspec = pl.BlockSpec((nb, tn, fs), lambda i, n: (0, n, 0))
    (dh,), cres = _matmul(
        tag + "_dh", (rows // tm, d // tn), [da, wg, db, wu], [aspec, wspec, aspec, wspec],
        [(0, 1, 0, NT, nb), (2, 3, 0, NT, nb)], [jax.ShapeDtypeStruct((rows, d), F32)],
        [pl.BlockSpec((tm, tn), lambda i, n: (i, n))], _store_all, comm=comm)
    return dh, cres


def _rope_tables(n_lat, n_ctx):
    half = LANES // 4
    inv_freq = (np.float32(ROPE_THETA) ** (-np.arange(half, dtype=np.float32) / np.float32(half))).astype(np.float32)
    pos = np.arange(n_lat)
    ang_r = (pos // GRID_W).astype(np.float32)[:, None] * inv_freq
    ang_c = (pos % GRID_W).astype(np.float32)[:, None] * inv_freq
    cos_l = np.concatenate([np.cos(ang_r)] * 2 + [np.cos(ang_c)] * 2, axis=1)
    sin_l = np.concatenate([-np.sin(ang_r), np.sin(ang_r), -np.sin(ang_c), np.sin(ang_c)], axis=1)
    cos_all = np.concatenate([cos_l, np.ones((n_ctx, LANES), np.float32)], axis=0).astype(np.float32)
    sin_all = np.concatenate([sin_l, np.zeros((n_ctx, LANES), np.float32)], axis=0).astype(np.float32)
    return jnp.asarray(cos_all), jnp.asarray(sin_all)


def _swap_halves(x):
    lane = lax.broadcasted_iota(jnp.int32, x.shape, 1)
    return jnp.where((lane % 64) < 32, pltpu.roll(x, 96, 1), pltpu.roll(x, 32, 1))


def _heads_spec(tq, hb, width, first_block):
    per_shard = width // (hb * LANES)

    def index(k, i):
        blk = first_block + k
        return blk // per_shard, i, blk % per_shard
    return pl.BlockSpec((None, tq, hb * LANES), index)


def _qk_prep(name, src, first_block, hb, n_heads, rows, g, cos_t, sin_t):
    tq = _tile(rows, HEAD_ROW_TILE, SUBLANES)
    tab = pl.BlockSpec((tq, LANES), lambda k, i: (i, 0))

    def body(x_ref, g_ref, c_ref, s_ref, o_ref):
        for h in range(hb):
            x = x_ref[:, h * LANES:(h + 1) * LANES]
            n = x * lax.rsqrt(jnp.mean(x * x, axis=-1, keepdims=True) + NORM_EPS) * g_ref[...]
            o_ref[h] = (n * c_ref[...] + _swap_halves(n) * s_ref[...]).astype(BF16)

    return pl.pallas_call(
        body, name=name, grid=(n_heads // hb, rows // tq),
        in_specs=[_heads_spec(tq, hb, src.shape[-1], first_block), pl.BlockSpec((1, LANES), lambda k, i: (0, 0)),
                  tab, tab],
        out_specs=pl.BlockSpec((hb, tq, LANES), lambda k, i: (k, i, 0)),
        out_shape=jax.ShapeDtypeStruct((n_heads, rows, LANES), BF16), compiler_params=_params(2),
    )(src, g, cos_t, sin_t)


def _qk_prep_bwd(name, dy, src, first_block, hb, n_heads, rows, g, cos_t, sin_t):
    tq = _tile(rows, HEAD_ROW_TILE, SUBLANES)
    tab = pl.BlockSpec((tq, LANES), lambda k, i: (i, 0))

    def body(dy_ref, x_ref, g_ref, c_ref, s_ref, dx_ref, dg_ref):
        g = g_ref[...]
        dg = None
        for h in range(hb):
            x = x_ref[:, h * LANES:(h + 1) * LANES]
            dyv = dy_ref[h]
            rstd = lax.rsqrt(jnp.mean(x * x, axis=-1, keepdims=True) + NORM_EPS)
            xh = x * rstd
            dn = dyv * c_ref[...] + _swap_halves(dyv * s_ref[...])
            dxh = dn * g
            dx = rstd * (dxh - xh * jnp.mean(dxh * xh, axis=-1, keepdims=True))
            dx_ref[:, h * LANES:(h + 1) * LANES] = dx.astype(BF16)
            part = _colsum(dn * xh)
            dg = part if dg is None else dg + part
        first = jnp.logical_and(pl.program_id(0) == 0, pl.program_id(1) == 0)

        @pl.when(first)
        def _():
            dg_ref[...] = dg

        @pl.when(jnp.logical_not(first))
        def _():
            dg_ref[...] += dg

    return pl.pallas_call(
        body, name=name, grid=(n_heads // hb, rows // tq),
        in_specs=[pl.BlockSpec((hb, tq, LANES), lambda k, i: (k, i, 0)),
                  _heads_spec(tq, hb, src.shape[-1], first_block),
                  pl.BlockSpec((1, LANES), lambda k, i: (0, 0)), tab, tab],
        out_specs=[pl.BlockSpec((tq, hb * LANES), lambda k, i: (i, k)),
                   pl.BlockSpec((1, LANES), lambda k, i: (0, 0))],
        out_shape=[jax.ShapeDtypeStruct((rows, n_heads * LANES), BF16), jax.ShapeDtypeStruct((1, LANES), F32)],
        compiler_params=_params(2),
    )(dy, src, g, cos_t, sin_t)


def _heads_cast(name, src, first_block, hb, n_heads, rows):
    tq = _tile(rows, HEAD_ROW_TILE, SUBLANES)

    def body(x_ref, o_ref):
        for h in range(hb):
            o_ref[h] = x_ref[:, h * LANES:(h + 1) * LANES].astype(BF16)

    return pl.pallas_call(
        body, name=name, grid=(n_heads // hb, rows // tq),
        in_specs=[_heads_spec(tq, hb, src.shape[-1], first_block)],
        out_specs=pl.BlockSpec((hb, tq, LANES), lambda k, i: (k, i, 0)),
        out_shape=jax.ShapeDtypeStruct((n_heads, rows, LANES), BF16), compiler_params=_params(2),
    )(src)


def _heads_merge(name, src):
    n_heads, rows, _ = src.shape
    tq = _tile(rows, HEAD_ROW_TILE, SUBLANES)

    def body(x_ref, o_ref):
        for h in range(n_heads):
            o_ref[:, h * LANES:(h + 1) * LANES] = x_ref[h].astype(BF16)

    return pl.pallas_call(
        body, name=name, grid=(rows // tq,),
        in_specs=[pl.BlockSpec((n_heads, tq, LANES), lambda i: (0, i, 0))],
        out_specs=pl.BlockSpec((tq, n_heads * LANES), lambda i: (i, 0)),
        out_shape=jax.ShapeDtypeStruct((rows, n_heads * LANES), BF16), compiler_params=_params(1),
    )(src)


def _attn_fwd(q, k, v, q_per_kv, comm=None):
    nq, l, _ = q.shape
    s_len = k.shape[1]
    tq = _tile(l, ROW_TILE, SUBLANES)
    scale = LANES ** -0.5
    kv = pl.BlockSpec((None, s_len, LANES), lambda h, i: (h // q_per_kv, 0, 0))

    def body(q_ref, k_ref, v_ref, o_ref):
        s = lax.dot_general(q_ref[...], k_ref[...], NT, preferred_element_type=F32) * scale
        p = jnp.exp(s - jnp.max(s, axis=-1, keepdims=True))
        den = jnp.sum(p, axis=-1, keepdims=True)
        o = jnp.dot(p.astype(BF16), v_ref[...], preferred_element_type=F32)
        o_ref[...] = (o / den).astype(BF16)

    (o,), cres = _host_call(
        body, name="attn_fwd", grid=(nq, l // tq), operands=[q, k, v],
        in_specs=[pl.BlockSpec((None, tq, LANES), lambda h, i: (h, i, 0)), kv, kv],
        out_shape=[jax.ShapeDtypeStruct((l, nq * LANES), BF16)],
        out_specs=[pl.BlockSpec((tq, LANES), lambda h, i: (i, h))], comm=comm)
    return o, cres


def _attn_bwd(q, k, v, do, q_per_kv, comm=None):
    nq, l, _ = q.shape
    nkv, s_len, _ = k.shape
    tq = _tile(l, ROW_TILE, SUBLANES)
    scale = LANES ** -0.5
    hp = ATTN_BWD_HEADS if q_per_kv % ATTN_BWD_HEADS == 0 else 1
    kv = pl.BlockSpec((None, s_len, LANES), lambda g, r, i: (g, 0, 0))
    qs = pl.BlockSpec((hp, tq, LANES), lambda g, r, i: (g * (q_per_kv // hp) + r, i, 0))

    def body(q_ref, k_ref, v_ref, do_ref, dq_ref, dk_ref, dv_ref):
        kvv, vv = k_ref[...], v_ref[...]
        dk_new = dv_new = None
        for h in range(hp):
            qv, dov = q_ref[h], do_ref[:, h * LANES:(h + 1) * LANES]
            st = lax.dot_general(kvv, qv, NT, preferred_element_type=F32) * scale
            e = jnp.exp(st - jnp.max(st, axis=0, keepdims=True))
            pt = e / jnp.sum(e, axis=0, keepdims=True)
            dpt = lax.dot_general(vv, dov, NT, preferred_element_type=F32)
            delta = jnp.sum(pt * dpt, axis=0, keepdims=True)
            dst = (pt * (dpt - delta) * scale).astype(BF16)
            dq_ref[h] = lax.dot_general(dst, kvv, TN, preferred_element_type=F32)
            dk_h = jnp.dot(dst, qv, preferred_element_type=F32)
            dv_h = jnp.dot(pt.astype(BF16), dov, preferred_element_type=F32)
            dk_new = dk_h if dk_new is None else dk_new + dk_h
            dv_new = dv_h if dv_new is None else dv_new + dv_h
        first = jnp.logical_and(pl.program_id(1) == 0, pl.program_id(2) == 0)

        @pl.when(first)
        def _():
            dk_ref[...] = dk_new
            dv_ref[...] = dv_new

        @pl.when(jnp.logical_not(first))
        def _():
            dk_ref[...] += dk_new
            dv_ref[...] += dv_new

    (dq, dk, dv), cres = _host_call(
        body, name="attn_bwd", grid=(nkv, q_per_kv // hp, l // tq), operands=[q, k, v, do],
        in_specs=[qs, kv, kv, pl.BlockSpec((tq, hp * LANES), lambda g, r, i: (i, g * (q_per_kv // hp) + r))],
        out_specs=[qs, kv, kv],
        out_shape=[jax.ShapeDtypeStruct((nq, l, LANES), F32), jax.ShapeDtypeStruct((nkv, s_len, LANES), F32),
                   jax.ShapeDtypeStruct((nkv, s_len, LANES), F32)], comm=comm)
    return dq, dk, dv, cres


def _zoh(a_re, a_im, log_dt):
    dt = jnp.exp(log_dt)[..., None]
    mag = jnp.exp(a_re * dt)
    lb_re = mag * jnp.cos(a_im * dt)
    lb_im = mag * jnp.sin(a_im * dt)
    den = a_re * a_re + a_im * a_im
    coef_re = ((lb_re - 1.0) * a_re + lb_im * a_im) / den
    coef_im = (lb_im * a_re - (lb_re - 1.0) * a_im) / den
    return lb_re, lb_im, coef_re, coef_im


def _ssm_discretize(a_re, a_im, log_dt, b_re, b_im):
    lb_re, lb_im, cr, ci = _zoh(a_re, a_im, log_dt)
    bt_re = cr[..., None] * b_re - ci[..., None] * b_im
    bt_im = cr[..., None] * b_im + ci[..., None] * b_re
    return lb_re, lb_im, bt_re, bt_im


def _lambda_powers(a_re, a_im, log_dt, ns):
    dt = jnp.exp(log_dt)[..., None]
    k = jnp.arange(SCAN_TAPS + 1, dtype=F32)[:, None, None, None]
    mag, ang = jnp.exp(k * (a_re * dt)), k * (a_im * dt)
    shape = (SCAN_TAPS + 1, 2, ns, -1)
    return (mag * jnp.cos(ang)).reshape(shape), (mag * jnp.sin(ang)).reshape(shape)


def _slab_mask():
    idx = jnp.arange(SLAB_GROUPS)
    return (idx[:, None] == idx[None, :])[None, None, :, None, :, None]


def _block_diag(m):
    d, g, a, b = m.shape
    ns = g // SLAB_GROUPS
    wide = jnp.where(_slab_mask(), m.reshape(d, ns, SLAB_GROUPS, a, 1, b), 0.0)
    return wide.reshape(d, ns, SLAB_GROUPS * a, SLAB_GROUPS * b)


def _block_diag_extract(m, a, b):
    d, ns = m.shape[:2]
    m = m.reshape(d, ns, SLAB_GROUPS, a, SLAB_GROUPS, b)
    return jnp.sum(jnp.where(_slab_mask(), m, 0.0), axis=4).reshape(d, ns * SLAB_GROUPS, a, b)


def _tap_weights(base_re, base_im, pw_re, pw_im):
    pr = jnp.transpose(pw_re[:SCAN_TAPS], (1, 2, 0, 3))[:, :, :, None, :]
    pi = jnp.transpose(pw_im[:SCAN_TAPS], (1, 2, 0, 3))[:, :, :, None, :]
    br, bi = base_re[:, :, None], base_im[:, :, None]
    d, ns, cdim, s = base_re.shape
    re = (pr * br - pi * bi).reshape(d, ns, SCAN_TAPS * cdim, s)
    im = (pr * bi + pi * br).reshape(d, ns, SCAN_TAPS * cdim, s)
    return jnp.concatenate([re, im], axis=-1)


def _carry_tables(pw_re, pw_im, descending):
    def rows(pw):
        asc = pw[1:]
        per_dir = [asc[::-1, d] if descending[d] else asc[:, d] for d in range(2)]
        return jnp.transpose(jnp.stack(per_dir), (0, 2, 1, 3))
    return jnp.concatenate([rows(pw_re), rows(pw_im)], axis=-1)


def _scan_chunk(x, w_ref, tab_ref, s_ref, carry_ref, descending, t_rows, sw):
    row8 = lax.broadcasted_iota(jnp.int32, x.shape, 0) % SCAN_TAPS
    pieces = [x.astype(BF16)]
    for tau in range(1, SCAN_TAPS):
        if descending:
            sh = jnp.where(row8 <= SCAN_TAPS - 1 - tau, pltpu.roll(x, t_rows - tau, 0), 0.0)
        else:
            sh = jnp.where(row8 >= tau, pltpu.roll(x, tau, 0), 0.0)
        pieces.append(sh.astype(BF16))
    xa = jnp.concatenate(pieces, axis=1)
    s_ref[...] = jnp.dot(xa, w_ref[...], preferred_element_type=F32)
    tab = tab_ref[...]
    t_re, t_im = tab[:, :sw], tab[:, sw:]
    nb = t_rows // SCAN_TAPS
    edge = 0 if descending else SCAN_TAPS - 1

    def step(b, carry):
        h_re, h_im = carry
        r0 = pl.multiple_of(((nb - 1 - b) if descending else b) * SCAN_TAPS, SCAN_TAPS)
        x_re = s_ref[pl.ds(r0, SCAN_TAPS), :sw] + t_re * h_re - t_im * h_im
        x_im = s_ref[pl.ds(r0, SCAN_TAPS), sw:] + t_re * h_im + t_im * h_re
        s_ref[pl.ds(r0, SCAN_TAPS), :sw] = x_re
        s_ref[pl.ds(r0, SCAN_TAPS), sw:] = x_im
        return x_re[edge:edge + 1, :], x_im[edge:edge + 1, :]

    h_re, h_im = lax.fori_loop(0, nb, step, (carry_ref[0:1, :sw], carry_ref[0:1, sw:]))
    carry_ref[0:1, :sw] = h_re
    carry_ref[0:1, sw:] = h_im


def _ssm_fwd(name, dr, u_src, u_shard, waug, tab, cd, descending, chunk_of, t_rows, rows, comm=None):
    _, ns, kdim, sw2 = waug.shape
    sw = sw2 // 2
    width = ns * LANES
    nchunks = rows // t_rows

    def body(u_ref, w_ref, tab_ref, cd_ref, y_ref, h_ref, s_ref, carry_ref):
        @pl.when(pl.program_id(1) == 0)
        def _():
            carry_ref[...] = jnp.zeros_like(carry_ref)

        _scan_chunk(u_ref[...], w_ref, tab_ref, s_ref, carry_ref, descending, t_rows, sw)
        hb = s_ref[...].astype(BF16)
        h_ref[...] = hb
        y_ref[...] = jnp.dot(hb, cd_ref[...], preferred_element_type=F32)

    (y, h), cres = _host_call(
        body, name=name, grid=(ns, nchunks), operands=[u_src, waug, tab, cd],
        in_specs=[pl.BlockSpec((None, t_rows, LANES), lambda s, i: (u_shard, chunk_of(i), s)),
                  pl.BlockSpec((None, None, kdim, sw2), lambda s, i: (dr, s, 0, 0)),
                  pl.BlockSpec((None, None, SCAN_TAPS, sw2), lambda s, i: (dr, s, 0, 0)),
                  pl.BlockSpec((None, None, sw2, LANES), lambda s, i: (dr, s, 0, 0))],
        out_specs=[pl.BlockSpec((t_rows, LANES), lambda s, i: (chunk_of(i), s)),
                   pl.BlockSpec((None, t_rows, sw2), lambda s, i: (s, chunk_of(i), 0))],
        out_shape=[jax.ShapeDtypeStruct((rows, width), F32), jax.ShapeDtypeStruct((ns, rows, sw2), BF16)],
        scratch_shapes=[pltpu.VMEM((t_rows, sw2), F32), pltpu.VMEM((SUBLANES, sw2), F32)], comm=comm)
    return y, h, cres


def _ssm_bwd(name, dr, dy, u_src, u_shard, states, caug, tab, bdt, descending, chunk_of, t_rows, rows, comm=None):
    _, ns, kdim, sw2 = caug.shape
    sw = sw2 // 2
    width = ns * LANES
    nchunks = rows // t_rows

    def body(dy_ref, u_ref, h_ref, w_ref, tab_ref, bdt_ref, du_ref, dbd_ref, dcd_ref, dlam_ref,
             s_ref, carry_ref, gsave_ref):
        first = pl.program_id(1) == 0

        @pl.when(first)
        def _():
            carry_ref[...] = jnp.zeros_like(carry_ref)
            gsave_ref[...] = jnp.zeros_like(gsave_ref)

        dyv = dy_ref[...]
        _scan_chunk(dyv, w_ref, tab_ref, s_ref, carry_ref, descending, t_rows, sw)
        g = s_ref[...]
        gb = g.astype(BF16)
        du_ref[...] = jnp.dot(gb, bdt_ref[...], preferred_element_type=F32)
        dbd = lax.dot_general(u_ref[...].astype(BF16), gb, TN, preferred_element_type=F32)
        hb = h_ref[...]
        dcd = lax.dot_general(hb, dyv.astype(BF16), TN, preferred_element_type=F32)
        hf = hb.astype(F32)
        rowid = lax.broadcasted_iota(jnp.int32, hf.shape, 0)
        if descending:
            hp = jnp.where(rowid == 0, 0.0, pltpu.roll(hf, 1, 0))
            h_edge, g_edge = hf[t_rows - 1:t_rows, :], g[0:1, :]
        else:
            hp = jnp.where(rowid == t_rows - 1, 0.0, pltpu.roll(hf, t_rows - 1, 0))
            h_edge, g_edge = hf[0:1, :], g[t_rows - 1:t_rows, :]
        g_re, g_im, hp_re, hp_im = g[:, :sw], g[:, sw:], hp[:, :sw], hp[:, sw:]
        gs = gsave_ref[0:1, :]
        gs_re, gs_im, he_re, he_im = gs[:, :sw], gs[:, sw:], h_edge[:, :sw], h_edge[:, sw:]
        dl_re = _colsum(g_re * hp_re + g_im * hp_im) + gs_re * he_re + gs_im * he_im
        dl_im = _colsum(g_im * hp_re - g_re * hp_im) + gs_im * he_re - gs_re * he_im
        gsave_ref[0:1, :] = g_edge

        @pl.when(first)
        def _():
            dbd_ref[...] = dbd
            dcd_ref[...] = dcd
            dlam_ref[:, :sw] = dl_re
            dlam_ref[:, sw:] = dl_im

        @pl.when(jnp.logical_not(first))
        def _():
            dbd_ref[...] += dbd
            dcd_ref[...] += dcd
            dlam_ref[:, :sw] += dl_re
            dlam_ref[:, sw:] += dl_im

    (du, dbd, dcd, dlam), cres = _host_call(
        body, name=name, grid=(ns, nchunks), operands=[dy, u_src, states, caug, tab, bdt],
        in_specs=[pl.BlockSpec((t_rows, LANES), lambda s, i: (chunk_of(i), s)),
                  pl.BlockSpec((None, t_rows, LANES), lambda s, i: (u_shard, chunk_of(i), s)),
                  pl.BlockSpec((None, t_rows, sw2), lambda s, i: (s, chunk_of(i), 0)),
                  pl.BlockSpec((None, None, kdim, sw2), lambda s, i: (dr, s, 0, 0)),
                  pl.BlockSpec((None, None, SCAN_TAPS, sw2), lambda s, i: (dr, s, 0, 0)),
                  pl.BlockSpec((None, None, sw2, LANES), lambda s, i: (dr, s, 0, 0))],
        out_specs=[pl.BlockSpec((t_rows, LANES), lambda s, i: (chunk_of(i), s)),
                   pl.BlockSpec((None, LANES, sw2), lambda s, i: (s, 0, 0)),
                   pl.BlockSpec((None, sw2, LANES), lambda s, i: (s, 0, 0)),
                   pl.BlockSpec((None, 1, sw2), lambda s, i: (s, 0, 0))],
        out_shape=[jax.ShapeDtypeStruct((rows, width), F32), jax.ShapeDtypeStruct((ns, LANES, sw2), F32),
                   jax.ShapeDtypeStruct((ns, sw2, LANES), F32), jax.ShapeDtypeStruct((ns, 1, sw2), F32)],
        scratch_shapes=[pltpu.VMEM((t_rows, sw2), F32), pltpu.VMEM((SUBLANES, sw2), F32),
                        pltpu.VMEM((SUBLANES, sw2), F32)], comm=comm)
    return du, dbd, dcd, dlam, cres


def _mod_fwd(cs, w_mod, b_cols):
    d, width = w_mod.shape
    tn = _tile(width, 768, LANES)

    def epilogue(accs, ins, outs, pids):
        outs[0][...] = accs[0] + ins[2][...]

    return _matmul(
        "mod_fwd", (width // tn,), [cs, w_mod, b_cols],
        [pl.BlockSpec((16, d), lambda n: (0, 0)), pl.BlockSpec((d, tn), lambda n: (0, n)),
         pl.BlockSpec((1, tn), lambda n: (0, n))],
        [(0, 1, 0, NN)], [jax.ShapeDtypeStruct((16, width), F32)], [pl.BlockSpec((16, tn), lambda n: (0, n))],
        epilogue, prologue={0: lambda v: v * _sigmoid(v)})[0][0]


def _mod_bwd_adam(cs, dmod_cols, w, m, v, comm=None):
    d, width = w.shape
    tn = _tile(width, LANES, LANES)
    col = pl.BlockSpec((d, tn), lambda n: (0, n))

    def body(cs_ref, dm_ref, w_ref, m_ref, v_ref, g_ref, dl_ref, nm_ref, nv_ref, ds_ref):
        n = pl.program_id(0)
        lat = dm_ref[pl.ds(0, N_DEV, stride=SUBLANES), :]
        ctx = jnp.sum(dm_ref[pl.ds(1, N_DEV, stride=SUBLANES), :], axis=0, keepdims=True)
        row = lax.broadcasted_iota(jnp.int32, lat.shape, 0)
        dm = jnp.concatenate([lat, jnp.where(row == 0, ctx, 0.0)], axis=0).astype(BF16)
        c = cs_ref[...]
        sc = (c * _sigmoid(c)).astype(BF16)
        wv = w_ref[...]
        g = lax.dot_general(sc, dm, TN, preferred_element_type=F32)
        delta, m2, v2 = _adamw(wv, g, m_ref[...], v_ref[...])
        g_ref[...] = g
        dl_ref[...] = delta
        nm_ref[...] = m2
        nv_ref[...] = v2
        part = lax.dot_general(dm, wv.astype(BF16), NT, preferred_element_type=F32)

        @pl.when(n == 0)
        def _():
            ds_ref[...] = part

        @pl.when(n > 0)
        def _():
            ds_ref[...] += part

    shard = jax.ShapeDtypeStruct((d, width), F32)
    return _host_call(
        body, name="mod_bwd_adam", grid=(width // tn,), operands=[cs, dmod_cols, w, m, v],
        in_specs=[pl.BlockSpec((16, d), lambda n: (0, 0)), pl.BlockSpec((N_DEV * SUBLANES, tn), lambda n: (0, n)),
                  col, col, col],
        out_specs=[col, col, col, col, pl.BlockSpec((16, d), lambda n: (0, 0))],
        out_shape=[shard, shard, shard, shard, jax.ShapeDtypeStruct((16, d), F32)], comm=comm)


def _pair_sum(name, grads, got, core):
    _, rows, cols = grads.shape
    tr = _tile(rows, max(PACKED_SUBLANES, ADAM_BLOCK_BYTES // (cols * 6 * N_CHIPS)), PACKED_SUBLANES)
    blk = pl.BlockSpec((N_CHIPS, tr, cols), lambda i, cc: (0, i, 0))

    def body(core_ref, a_ref, b_ref, o_ref):
        o_ref[...] = (a_ref[...].astype(F32) + b_ref[...].astype(F32)).astype(BF16)

    grid_spec = pltpu.PrefetchScalarGridSpec(
        num_scalar_prefetch=1, grid=(rows // tr,),
        in_specs=[pl.BlockSpec((N_CHIPS, None, tr, cols), lambda i, cc: (0, cc[0], i, 0)), blk], out_specs=blk)
    return pl.pallas_call(
        body, name=name, grid_spec=grid_spec, out_shape=jax.ShapeDtypeStruct((N_CHIPS, rows, cols), BF16),
        compiler_params=_params(1))(core, grads.reshape(N_CHIPS, 2, rows, cols), got)


def _owner_adam(name, items, chip, comm=None):
    plan, start = [], 0
    for _, _, w, _, _ in items:
        rows, cols = w.shape
        tr = _tile(rows, max(PACKED_SUBLANES, ADAM_GROUP_BLOCK_BYTES // (cols * 40)), PACKED_SUBLANES)
        plan.append((start, rows // tr, tr, cols))
        start += rows // tr
    operands, in_specs, out_specs, out_shape = [], [], [], []
    for (first, nt, tr, cols), (p, l, w, m, v) in zip(plan, items):
        def tile(s, first=first, nt=nt):
            return jnp.clip(s - first, 0, nt - 1)
        blk = pl.BlockSpec((tr, cols), lambda s, ch, tile=tile: (tile(s), 0))
        operands += [p, l, w, m, v]
        in_specs += [pl.BlockSpec((None, tr, cols), lambda s, ch, tile=tile: (ch[0], tile(s), 0)),
                     pl.BlockSpec((N_CHIPS - 1, tr, cols), lambda s, ch, tile=tile: (0, tile(s), 0)), blk, blk, blk]
        out_specs += [blk] * 4
        out_shape += [jax.ShapeDtypeStruct(w.shape, F32)] * 4
    n = len(items)

    def body(chip_ref, *refs):
        s = pl.program_id(0)
        for k, (first, nt, _, _) in enumerate(plan):
            p_ref, l_ref, w_ref, m_ref, v_ref = refs[5 * k:5 * k + 5]
            g_ref, dl_ref, nm_ref, nv_ref = refs[5 * n + 4 * k:5 * n + 4 * k + 4]

            @pl.when(jnp.logical_and(s >= first, s < first + nt))
            def _(p_ref=p_ref, l_ref=l_ref, w_ref=w_ref, m_ref=m_ref, v_ref=v_ref,
                  g_ref=g_ref, dl_ref=dl_ref, nm_ref=nm_ref, nv_ref=nv_ref):
                g = p_ref[...].astype(F32)
                for r in range(N_CHIPS - 1):
                    g = g + l_ref[r].astype(F32)
                delta, m2, v2 = _adamw(w_ref[...], g, m_ref[...], v_ref[...])
                g_ref[...] = g
                dl_ref[...] = delta
                nm_ref[...] = m2
                nv_ref[...] = v2

    res, cres = _host_call(body, name=name, grid=(start,), operands=operands, in_specs=in_specs,
                           out_shape=out_shape, out_specs=out_specs, comm=comm, prefetch=[chip])
    return [res[4 * k:4 * k + 4] for k in range(n)], cres


def _sum_adam(name, parts, w, m, v):
    rows, cols = w.shape
    n_parts = parts.shape[0]
    align = PACKED_SUBLANES if parts.dtype == BF16 else SUBLANES
    tr = _tile(rows, max(align, ADAM_BLOCK_BYTES // (cols * 44)), align)
    blk = pl.BlockSpec((tr, cols), lambda i: (i, 0))

    def body(p_ref, w_ref, m_ref, v_ref, g_ref, dl_ref, nm_ref, nv_ref):
        g = p_ref[0].astype(F32)
        for s in range(1, n_parts):
            g = g + p_ref[s].astype(F32)
        delta, m2, v2 = _adamw(w_ref[...], g, m_ref[...], v_ref[...])
        g_ref[...] = g
        dl_ref[...] = delta
        nm_ref[...] = m2
        nv_ref[...] = v2

    out = jax.ShapeDtypeStruct((rows, cols), F32)
    return pl.pallas_call(
        body, name=name, grid=(rows // tr,),
        in_specs=[pl.BlockSpec((n_parts, tr, cols), lambda i: (0, i, 0)), blk, blk, blk],
        out_specs=[blk, blk, blk, blk], out_shape=[out, out, out, out], compiler_params=_params(1),
    )(parts, w, m, v)


def _bias_adam(dmod_all, w, m, v):
    width = w.shape[-1]
    tn = _tile(width, 2048, LANES)
    blk = pl.BlockSpec((1, tn), lambda n: (0, n))

    def body(p_ref, w_ref, m_ref, v_ref, g_ref, dl_ref, nm_ref, nv_ref):
        g = jnp.sum(p_ref[...], axis=0, keepdims=True)
        delta, m2, v2 = _adamw(w_ref[...], g, m_ref[...], v_ref[...])
        g_ref[...] = g
        dl_ref[...] = delta
        nm_ref[...] = m2
        nv_ref[...] = v2

    out = jax.ShapeDtypeStruct((1, width), F32)
    return pl.pallas_call(
        body, name="bias_adam", grid=(width // tn,),
        in_specs=[pl.BlockSpec((dmod_all.shape[0], tn), lambda n: (0, n)), blk, blk, blk],
        out_specs=[blk, blk, blk, blk], out_shape=[out, out, out, out], compiler_params=_params(1),
    )(dmod_all, w, m, v)


def _pack(arrays, total_rows):
    flat = []
    for a in arrays:
        a = a.reshape(-1).astype(F32)
        flat.append(jnp.pad(a, (0, (-a.shape[0]) % LANES)))
    flat = jnp.concatenate(flat).reshape(-1, LANES)
    return jnp.pad(flat, ((0, total_rows - flat.shape[0]), (0, 0)))


def _unpack(packed, shapes):
    out, row = [], 0
    for shp in shapes:
        size = math.prod(shp)
        nrows = -(-size // LANES)
        out.append(packed[row:row + nrows].reshape(-1)[:size].reshape(shp))
        row += nrows
    return out


def kernel(x, c, ctx, c_ctx, w_mod, b_mod, norm_g, w_ffn1_gate, w_ffn1_up, w_ffn1_down, w_in, q_norm_g, k_norm_g, ssm_a_re, ssm_a_im, ssm_log_dt, ssm_b_re, ssm_b_im, ssm_c_re, ssm_c_im, ssm_d, w_glu, b_glu, w_br_attn, w_br_ssm, w_out, w_ffn2_gate, w_ffn2_up, w_ffn2_down, loss_target, m_c_ctx, m_w_mod, m_b_mod, m_norm_g, m_w_ffn1_gate, m_w_ffn1_up, m_w_ffn1_down, m_w_in, m_q_norm_g, m_k_norm_g, m_ssm_a_re, m_ssm_a_im, m_ssm_log_dt, m_ssm_b_re, m_ssm_b_im, m_ssm_c_re, m_ssm_c_im, m_ssm_d, m_w_glu, m_b_glu, m_w_br_attn, m_w_br_ssm, m_w_out, m_w_ffn2_gate, m_w_ffn2_up, m_w_ffn2_down, v_c_ctx, v_w_mod, v_b_mod, v_norm_g, v_w_ffn1_gate, v_w_ffn1_up, v_w_ffn1_down, v_w_in, v_q_norm_g, v_k_norm_g, v_ssm_a_re, v_ssm_a_im, v_ssm_log_dt, v_ssm_b_re, v_ssm_b_im, v_ssm_c_re, v_ssm_c_im, v_ssm_d, v_w_glu, v_b_glu, v_w_br_attn, v_w_br_ssm, v_w_out, v_w_ffn2_gate, v_w_ffn2_up, v_w_ffn2_down):
    _, L, D = x.shape
    Lc = ctx.shape[1]
    R = L + Lc
    MODW = w_mod.shape[-1]
    INS = w_in.shape[-1]
    KVW = INS // 2
    NQ = D // LANES
    NKV = KVW // LANES
    QPK = NQ // NKV
    HBQ = INS // LANES
    G, P, E = ssm_b_re.shape[2:]
    W = G * E
    SW = SLAB_GROUPS * P
    assert E * SLAB_GROUPS == LANES and W == INS and NQ * LANES == D and Lc <= L
    me = 4 * lax.axis_index("x") + 2 * lax.axis_index("y") + lax.axis_index("c")

    x2, ctx2, tgt = x[0], ctx[0], loss_target[0]
    xc0 = jnp.concatenate([x2, ctx2], axis=0)

    def bf(w):
        return w[0].astype(BF16)

    def widen(a):
        return jnp.pad(a[0], ((0, 0), (0, D - a.shape[-1])))

    def at_row(a, r, total):
        return jnp.pad(a, ((r, total - r - a.shape[0]), (0, 0)))

    pack_in = (at_row(c, 0, 16) + at_row(widen(norm_g), 1, 16) + at_row(widen(m_norm_g), 4, 16)
               + at_row(widen(v_norm_g), 7, 16))
    (g_in,) = _exchange_only("ag_inputs", _Gather([pack_in]))
    c_all = g_in[:, 0, :]
    dn = D // N_DEV

    def full_norm(k):
        return jnp.transpose(g_in[:, k:k + 3, :dn], (1, 0, 2)).reshape(3, D)

    ng_full, m_ng_full, v_ng_full = full_norm(1), full_norm(4), full_norm(7)
    cs = at_row(c_all, 0, 16) + at_row(c_ctx[None, :], 8, 16)

    b_cols = lax.dynamic_slice_in_dim(b_mod, me * MODW, MODW, axis=1)
    mod_blk = _mod_fwd(cs, w_mod[0], b_cols)
    (mod_g,) = _exchange_only("ag_mod", _Gather([mod_blk]))
    mod_lat = lax.dynamic_index_in_dim(mod_g, me, axis=1, keepdims=False).reshape(9, D)
    mod_ctx = mod_g[:, 8, :].reshape(9, D)[:5]
    tab2 = jnp.concatenate([mod_lat, mod_ctx, ng_full, jnp.zeros((7, D), F32)], axis=0)
    tab3 = tab2[:, None, :]
    SH1, SC1, G1, SH2, SC2, G2, SH3, SC3, G3, MC0, MC1, MC2, MC3, MC4, GAM1, GAM2, GAM3 = range(17)

    wg1, wu1 = _exchange_only("ag_ffn1_gate_up", _Gather([bf(w_ffn1_gate), bf(w_ffn1_up)]))
    h1 = _norm_mod_fwd("nm1_fwd", xc0, tab3, GAM1, (SH1, MC0), (SC1, MC1), L, Lc)
    a1, b1, s1, (wd1,) = _ffn_up("ffn1", h1, wg1, wu1, comm=_Gather([bf(w_ffn1_down)]))
    f1, xc1, (win,) = _ffn_down("ffn1", s1, wd1, xc0, tab2, (G1, MC2), L, comm=_Gather([bf(w_in)]))

    h2 = _norm_mod_fwd("nm2_fwd", xc1, tab3, GAM2, (SH2, MC3), (SC2, MC4), L, Lc)
    tm = _tile(R, MM_TILE, LANES)
    tml = _tile(L, MM_TILE, LANES)

    (p01,), _ = _matmul(
        "in_proj_kvu", (2, R // tm), [h2, win],
        [pl.BlockSpec((tm, D), lambda j, i: (i, 0)), pl.BlockSpec((None, D, INS), lambda j, i: (j, 0, 0))],
        [(0, 1, 0, NN)], [jax.ShapeDtypeStruct((2, R, INS), F32)],
        [pl.BlockSpec((None, tm, INS), lambda j, i: (j, i, 0))], _store_all)
    (p27,), (wglu, wbra) = _matmul(
        "in_proj_qg", (6, L // tml), [h2, win],
        [pl.BlockSpec((tml, D), lambda j, i: (i, 0)), pl.BlockSpec((None, D, INS), lambda j, i: (j + 2, 0, 0))],
        [(0, 1, 0, NN)], [jax.ShapeDtypeStruct((6, L, INS), F32)],
        [pl.BlockSpec((None, tml, INS), lambda j, i: (j, i, 0))], _store_all,
        comm=_Gather([bf(w_glu), bf(w_br_attn)]))
    wglu2 = wglu.reshape(W, W)
    wbra2 = wbra.reshape(D, D)

    cos_all, sin_all = _rope_tables(L, Lc)
    cos_l, sin_l = cos_all[:L], sin_all[:L]

    q_rot = _qk_prep("q_prep", p27, 0, HBQ, NQ, L, q_norm_g, cos_l, sin_l)
    k_rot = _qk_prep("k_prep", p01, 0, NKV, NKV, R, k_norm_g, cos_all, sin_all)
    v_hd = _heads_cast("v_heads", p01, 1, NKV, NKV, R)
    attn, (wbrs, wout, wg2) = _attn_fwd(
        q_rot, k_rot, v_hd, QPK, comm=_Gather([bf(w_br_ssm), bf(w_out), bf(w_ffn2_gate)]))
    wout2 = wout.reshape(D, D)

    t_rows = _tile(math.gcd(L, Lc), ROW_TILE, SUBLANES)
    nl, ncx = L // t_rows, Lc // t_rows
    nch = nl + ncx
    ns = G // SLAB_GROUPS
    ssm_prim = (ssm_a_re[0], ssm_a_im[0], ssm_log_dt[0], ssm_b_re[0], ssm_b_im[0])
    _, _, bt_re, bt_im = _ssm_discretize(*ssm_prim)
    pw_re, pw_im = _lambda_powers(ssm_a_re[0], ssm_a_im[0], ssm_log_dt[0], ns)
    bd_re = _block_diag(jnp.swapaxes(bt_re, 2, 3))
    bd_im = _block_diag(jnp.swapaxes(bt_im, 2, 3))
    ct_re = _block_diag(ssm_c_re[0])
    ct_im = _block_diag(-ssm_c_im[0])
    fwd_desc = (False, True)
    adj_desc = (True, False)
    s_waug = _tap_weights(bd_re, bd_im, pw_re, pw_im).astype(BF16)
    s_tab = _carry_tables(pw_re, pw_im, fwd_desc)
    s_cd = jnp.concatenate([jnp.swapaxes(ct_re, 2, 3), jnp.swapaxes(ct_im, 2, 3)], axis=2).astype(BF16)
    s_caug = _tap_weights(ct_re, ct_im, pw_re, -pw_im).astype(BF16)
    s_tabc = _carry_tables(pw_re, -pw_im, adj_desc)
    s_bdt = jnp.concatenate([jnp.swapaxes(bd_re, 2, 3), jnp.swapaxes(bd_im, 2, 3)], axis=2).astype(BF16)
    order = [lambda i: (i + nl) % nch, lambda i: nch - 1 - i]
    order_adj = [lambda i: (nch - 1 - i + nl) % nch, lambda i: i]
    y0, st0, (wu2,) = _ssm_fwd("ssm_fwd0", 0, p01, 1, s_waug, s_tab, s_cd, fwd_desc[0], order[0], t_rows, R,
                               comm=_Gather([bf(w_ffn2_up)]))
    y1, st1, (wd2,) = _ssm_fwd("ssm_fwd1", 1, p01, 1, s_waug, s_tab, s_cd, fwd_desc[1], order[1], t_rows, R,
                               comm=_Gather([bf(w_ffn2_down)]))
    states = [st0, st1]

    tr = _row_tile(L, 0)
    rowW = pl.BlockSpec((tr, W), lambda i: (i, 0))
    vecW = pl.BlockSpec((1, W), lambda i: (0, 0))
    u_lat = pl.BlockSpec((None, tr, W), lambda i: (1, i, 0))

    def ssm_post(i, u, ya, yb, dvec):
        sv = dvec * u + ya + yb
        return [sv, _gelu(sv)], []

    (ssm_out, yg), _, _ = _rowwise(
        "ssm_post", L // tr, [p01, y0, y1, ssm_d], [u_lat, rowW, rowW, vecW],
        [jax.ShapeDtypeStruct((L, W), F32), jax.ShapeDtypeStruct((L, W), BF16)], [rowW, rowW], [], ssm_post)

    tnw = _tile(W, MM_TILE, LANES)

    def glu_epilogue(accs, ins, outs, pids):
        z = accs[0] + ins[3][...]
        outs[0][...] = z
        outs[1][...] = (_gelu(ins[2][...]) * _sigmoid(z)).astype(BF16)

    (z_glu, y2), _ = _matmul(
        "glu", (L // tml, W // tnw), [yg, wglu2, ssm_out, b_glu],
        [pl.BlockSpec((tml, W), lambda i, n: (i, 0)), pl.BlockSpec((W, tnw), lambda i, n: (0, n)),
         pl.BlockSpec((tml, tnw), lambda i, n: (i, n)), pl.BlockSpec((1, tnw), lambda i, n: (0, n))],
        [(0, 1, 0, NN)], [jax.ShapeDtypeStruct((L, W), F32), jax.ShapeDtypeStruct((L, W), BF16)],
        [pl.BlockSpec((tml, tnw), lambda i, n: (i, n))] * 2, glu_epilogue)

    tnd = _tile(D, MM_TILE, LANES)
    out_ld = pl.BlockSpec((tml, tnd), lambda i, n: (i, n))
    (br_a,), _ = _matmul(
        "br_attn", (L // tml, D // tnd), [attn, wbra2],
        [pl.BlockSpec((tml, D), lambda i, n: (i, 0)), pl.BlockSpec((D, tnd), lambda i, n: (0, n))],
        [(0, 1, 0, NN)], [jax.ShapeDtypeStruct((L, D), F32)], [out_ld], _store_all)

    cb = wbrs.shape[-1]
    gpb = INS // cb

    def gate_spec(first_shard):
        return pl.BlockSpec((None, tml, cb), lambda i, j: (first_shard + j // gpb, i, j % gpb))

    def merge_epilogue(accs, ins, outs, pids):
        br = accs[0]
        outs[0][...] = br
        outs[1][...] = (_sigmoid(ins[2][...]) * ins[4][...] + _sigmoid(ins[3][...]) * br).astype(BF16)

    col_blk = pl.BlockSpec((tml, cb), lambda i, j: (i, j))
    (br_s, merged), _ = _matmul(
        "br_ssm_merge", (L // tml, N_DEV), [y2, wbrs, p27, p27, br_a],
        [pl.BlockSpec((tml, W), lambda i, j: (i, 0)), pl.BlockSpec((None, W, cb), lambda i, j: (j, 0, 0)),
         gate_spec(2), gate_spec(4), col_blk],
        [(0, 1, 0, NN)], [jax.ShapeDtypeStruct((L, D), F32), jax.ShapeDtypeStruct((L, D), BF16)],
        [col_blk, col_blk], merge_epilogue)

    def out_epilogue(accs, ins, outs, pids):
        outs[0][...] = accs[0]
        outs[1][...] = ins[2][...] + ins[3][G2:G2 + 1, :] * accs[0]

    (mix, x2_), _ = _matmul(
        "out_proj", (L // tml, D // tnd), [merged, wout2, xc1, tab2],
        [pl.BlockSpec((tml, D), lambda i, n: (i, 0)), pl.BlockSpec((D, tnd), lambda i, n: (0, n)), out_ld,
         pl.BlockSpec((tab2.shape[0], tnd), lambda i, n: (0, n))],
        [(0, 1, 0, NN)], [jax.ShapeDtypeStruct((L, D), F32)] * 2, [out_ld, out_ld], out_epilogue)

    h3 = _norm_mod_fwd("nm3_fwd", x2_, tab3, GAM3, (SH3, SH3), (SC3, SC3), L, 0)
    a3, b3, s3, _ = _ffn_up("ffn2", h3, wg2, wu2)
    f3, x3, _ = _ffn_down("ffn2", s3, wd2, x2_, tab2, (G3, G3), L)

    trd = _row_tile(L, 0)
    rowD = pl.BlockSpec((trd, D), lambda i: (i, 0))

    def loss_fn(i, yv, t):
        err = yv - t
        return [err * (1.0 / D)], [_colsum(err * err)]

    (dx3,), (sq,), _ = _rowwise("loss", L // trd, [x3, tgt], [rowD, rowD],
                                [jax.ShapeDtypeStruct((L, D), F32)], [rowD], [D], loss_fn)
    loss = lax.psum(0.5 * jnp.sum(sq) / D, ("x", "y", "c"))

    core = lax.axis_index("c").astype(jnp.int32).reshape(1)
    chip = (2 * lax.axis_index("x") + lax.axis_index("y")).astype(jnp.int32).reshape(1)

    def pair_sums(tag, grads, halves):
        return [_pair_sum("pair_%s%d" % (tag, k), g_, h_, core) for k, (g_, h_) in enumerate(zip(grads, halves))]

    df3, (dg3, _) = _gate_bwd("gate3_bwd", dx3, f3, tab3, (G3, G3), 0.5, L, 0)
    dwd2, _ = _ffn_dwd("ffn2b", s3, df3)
    da3, db3, half_wd2 = _ffn_ds("ffn2b", df3, wd2, a3, b3, comm=_SiblingSwap([dwd2]))
    (p_wd2,) = pair_sums("wd2", [dwd2], half_wd2)
    dwg2, dwu2, (l_wd2,) = _ffn_dwgu("ffn2b", h3, da3, db3, comm=_ChipExchange([p_wd2]))
    dh3, half_wgu2 = _ffn_dh("ffn2b", da3, db3, wg2, wu2, comm=_SiblingSwap([dwg2, dwu2]))
    p_wg2, p_wu2 = pair_sums("wgu2", [dwg2, dwu2], half_wgu2)
    dx2, (dsh3, dsc3, _, _, dgam3) = _norm_mod_bwd("nm3_bwd", x2_, dh3, tab3, GAM3, (SC3, SC3), L, 0, dres=dx3)

    dmix, (dg2, _) = _gate_bwd("gate2_bwd", dx2, mix, tab3, (G2, G2), 1.0, L, 0)

    def dmerged_epilogue(accs, ins, outs, pids):
        dm = accs[0]
        ga, gs = _sigmoid(ins[2][...]), _sigmoid(ins[3][...])
        outs[0][...] = (ga * dm).astype(BF16)
        outs[1][...] = (gs * dm).astype(BF16)
        outs[2][...] = (dm * ins[4][...] * ga * (1.0 - ga)).astype(BF16)
        outs[3][...] = (dm * ins[5][...] * gs * (1.0 - gs)).astype(BF16)

    dgate_spec = pl.BlockSpec((None, tml, cb), lambda i, j: (j // gpb, i, j % gpb))
    (d_br_a, d_br_s, dg_a, dg_s), _ = _matmul(
        "dmerged", (L // tml, N_DEV), [dmix, wout2, p27, p27, br_a, br_s],
        [pl.BlockSpec((tml, D), lambda i, j: (i, 0)), pl.BlockSpec((cb, D), lambda i, j: (j, 0)),
         gate_spec(2), gate_spec(4), col_blk, col_blk],
        [(0, 1, 0, NT)],
        [jax.ShapeDtypeStruct((L, D), BF16)] * 2 + [jax.ShapeDtypeStruct((2, L, INS), BF16)] * 2,
        [col_blk, col_blk, dgate_spec, dgate_spec], dmerged_epilogue)

    def wgrad(name, a_mat, b_mat, tmo, tno):
        ka, ma = a_mat.shape
        _, nb_ = b_mat.shape
        return _matmul(
            name, (ma // tmo, nb_ // tno), [a_mat, b_mat],
            [pl.BlockSpec((ka, tmo), lambda m, n: (0, m)), pl.BlockSpec((ka, tno), lambda m, n: (0, n))],
            [(0, 1, 0, TN)], [jax.ShapeDtypeStruct((ma, nb_), BF16)],
            [pl.BlockSpec((tmo, tno), lambda m, n: (m, n))], _store_all)[0][0]

    dwout = wgrad("dw_out", merged, dmix, tnd, tnd)
    dwbra = wgrad("dw_br_attn", attn, d_br_a, tnd, tnd)
    (d_attn,), _ = _matmul(
        "d_attn", (L // tml, D // tnd), [d_br_a, wbra2],
        [pl.BlockSpec((tml, D), lambda i, n: (i, 0)), pl.BlockSpec((tnd, D), lambda i, n: (n, 0))],
        [(0, 1, 0, NT)], [jax.ShapeDtypeStruct((L, D), BF16)], [out_ld], _store_all)

    (dwbrs,), _ = _matmul(
        "dw_br_ssm", (N_DEV,), [y2, d_br_s],
        [pl.BlockSpec((L, W), lambda j: (0, 0)), pl.BlockSpec((L, cb), lambda j: (0, j))],
        [(0, 1, 0, TN)], [jax.ShapeDtypeStruct((N_DEV, W, cb), BF16)],
        [pl.BlockSpec((None, W, cb), lambda j: (j, 0, 0))], _store_all)

    def dy2_epilogue(accs, ins, outs, pids):
        dy2 = accs[0]
        sg = _sigmoid(ins[2][...])
        outs[0][...] = dy2 * sg
        outs[1][...] = (dy2 * _gelu(ins[3][...]) * sg * (1.0 - sg)).astype(BF16)

    wn_blk = pl.BlockSpec((tml, tnw), lambda i, n, k: (i, n))
    (dyg1, dz), _ = _matmul(
        "d_y2", (L // tml, W // tnw, N_DEV), [d_br_s, wbrs, z_glu, ssm_out],
        [pl.BlockSpec((tml, cb), lambda i, n, k: (i, k)), pl.BlockSpec((None, tnw, cb), lambda i, n, k: (k, n, 0)),
         wn_blk, wn_blk],
        [(0, 1, 0, NT)], [jax.ShapeDtypeStruct((L, W), F32), jax.ShapeDtypeStruct((L, W), BF16)],
        [wn_blk, wn_blk], dy2_epilogue, acc_shapes=[(tml, tnw)], nk=N_DEV)

    dwglu = wgrad("dw_glu", yg, dz, tnw, tnw)
    mix_grads = [dwout.reshape(N_DEV, D // N_DEV, D), dwbra.reshape(N_DEV, D // N_DEV, D), dwbrs,
                 dwglu.reshape(N_DEV, W // N_DEV, W)]

    def dssm_epilogue(accs, ins, outs, pids):
        outs[0][...] = (accs[0] + ins[2][...]) * _gelu_grad(ins[3][...])

    wn2 = pl.BlockSpec((tml, tnw), lambda i, n: (i, n))
    (dssm,), _ = _matmul(
        "d_ssm", (L // tml, W // tnw), [dz, wglu2, dyg1, ssm_out],
        [pl.BlockSpec((tml, W), lambda i, n: (i, 0)), pl.BlockSpec((tnw, W), lambda i, n: (n, 0)), wn2, wn2],
        [(0, 1, 0, NT)], [jax.ShapeDtypeStruct((L, W), F32)], [wn2], dssm_epilogue)

    dssm_all = jnp.concatenate([dssm, jnp.zeros((Lc, W), F32)], axis=0)
    du0, dbd0, dcd0, dlam0, (l_wg2, *half_mix) = _ssm_bwd(
        "ssm_bwd0", 0, dssm_all, p01, 1, states[0], s_caug, s_tabc, s_bdt, adj_desc[0], order_adj[0], t_rows, R,
        comm=_Both([_ChipExchange([p_wg2]), _SiblingSwap(mix_grads)]))
    p_wout, p_wbra, p_wbrs, p_wglu = pair_sums("mix", mix_grads, half_mix)
    du1, dbd1, dcd1, dlam1, (l_wu2,) = _ssm_bwd(
        "ssm_bwd1", 1, dssm_all, p01, 1, states[1], s_caug, s_tabc, s_bdt, adj_desc[1], order_adj[1], t_rows, R,
        comm=_ChipExchange([p_wu2]))

    trr = _row_tile(L, Lc)
    nlt = L // trr
    rowR = pl.BlockSpec((trr, W), lambda i: (i, 0))

    def du_fn(i, dua, dub, dsv, dvec, u):
        lat = (i < nlt).astype(F32)
        return [dua + dub + lat * (dvec * dsv)], [lat * _colsum(dsv * u)]

    (du_all,), (d_ssm_d,), _ = _rowwise(
        "du_combine", R // trr, [du0, du1, dssm_all, ssm_d, p01],
        [rowR, rowR, rowR, pl.BlockSpec((1, W), lambda i: (0, 0)), pl.BlockSpec((None, trr, W), lambda i: (1, i, 0))],
        [jax.ShapeDtypeStruct((R, W), BF16)], [rowR], [W], du_fn)

    def dz_sum(i, dzv):
        return [], [_colsum(dzv.astype(F32))]

    _, (d_b_glu,), _ = _rowwise("db_glu", L // tr, [dz], [rowW], [], [], [W], dz_sum)

    dq_rot, dk_rot, dv_hd, (l_wout, l_wbra, l_wbrs, l_wglu) = _attn_bwd(
        q_rot, k_rot, v_hd, d_attn, QPK, comm=_ChipExchange([p_wout, p_wbra, p_wbrs, p_wglu]))
    dq_pre, d_qg = _qk_prep_bwd("q_prep_bwd", dq_rot, p27, 0, HBQ, NQ, L, q_norm_g, cos_l, sin_l)
    dk_pre, d_kg = _qk_prep_bwd("k_prep_bwd", dk_rot, p01, 0, NKV, NKV, R, k_norm_g, cos_all, sin_all)
    dv_pre = _heads_merge("dv_merge", dv_hd)

    def lat_blocks(a):
        return jnp.pad(a, ((0, 0), (0, Lc), (0, 0)))

    dq_blocks = jnp.transpose(dq_pre.reshape(L, 2, INS), (1, 0, 2))
    dp = jnp.concatenate([
        jnp.concatenate([dk_pre, dv_pre], axis=1)[None], du_all[None],
        lat_blocks(dq_blocks), lat_blocks(dg_a), lat_blocks(dg_s)], axis=0)

    tmo = _tile(D, MM_TILE, LANES)
    (dwin,), _ = _matmul(
        "dw_in", (N_DEV, D // tmo), [h2, dp],
        [pl.BlockSpec((R, tmo), lambda j, m: (0, m)), pl.BlockSpec((None, R, INS), lambda j, m: (j, 0, 0))],
        [(0, 1, 0, TN)], [jax.ShapeDtypeStruct((N_DEV, D, INS), BF16)],
        [pl.BlockSpec((None, tmo, INS), lambda j, m: (j, m, 0))], _store_all)
    tnh = _tile(D, MM_TILE_NT, LANES)
    (dh2,), half_win = _matmul(
        "d_h2", (R // tm, D // tnh), [dp, win],
        [pl.BlockSpec((N_DEV, tm, INS), lambda i, n: (0, i, 0)),
         pl.BlockSpec((N_DEV, tnh, INS), lambda i, n: (0, n, 0))],
        [(0, 1, 0, NT, N_DEV)], [jax.ShapeDtypeStruct((R, D), F32)], [pl.BlockSpec((tm, tnh), lambda i, n: (i, n))],
        _store_all, comm=_SiblingSwap([dwin]))
    (p_win,) = pair_sums("win", [dwin], half_win)
    dxc1, (dsh2, dsc2, dmc3, dmc4, dgam2) = _norm_mod_bwd(
        "nm2_bwd", xc1, dh2, tab3, GAM2, (SC2, MC4), L, Lc, dres=dx2)

    df1, (dg1, dmc2) = _gate_bwd("gate1_bwd", dxc1, f1, tab3, (G1, MC2), 0.5, L, Lc)
    dwd1, _ = _ffn_dwd("ffn1b", s1, df1)
    da1, db1, (l_win, *half_wd1) = _ffn_ds(
        "ffn1b", df1, wd1, a1, b1, comm=_Both([_ChipExchange([p_win]), _SiblingSwap([dwd1])]))
    (p_wd1,) = pair_sums("wd1", [dwd1], half_wd1)
    dwg1, dwu1, (l_wd1,) = _ffn_dwgu("ffn1b", h1, da1, db1, comm=_ChipExchange([p_wd1]))
    dh1, half_wgu1 = _ffn_dh("ffn1b", da1, db1, wg1, wu1, comm=_SiblingSwap([dwg1, dwu1]))
    p_wg1, p_wu1 = pair_sums("wgu1", [dwg1, dwu1], half_wgu1)

    def adam_item(p, l_, w_, m_, v_):
        return (p, l_, w_[0], m_[0], v_[0])

    ready = [adam_item(p_wd1, l_wd1, w_ffn1_down, m_w_ffn1_down, v_w_ffn1_down),
             adam_item(p_win, l_win, w_in, m_w_in, v_w_in),
             adam_item(p_wglu, l_wglu, w_glu, m_w_glu, v_w_glu),
             adam_item(p_wbra, l_wbra, w_br_attn, m_w_br_attn, v_w_br_attn),
             adam_item(p_wbrs, l_wbrs, w_br_ssm, m_w_br_ssm, v_w_br_ssm),
             adam_item(p_wout, l_wout, w_out, m_w_out, v_w_out),
             adam_item(p_wg2, l_wg2, w_ffn2_gate, m_w_ffn2_gate, v_w_ffn2_gate),
             adam_item(p_wu2, l_wu2, w_ffn2_up, m_w_ffn2_up, v_w_ffn2_up),
             adam_item(p_wd2, l_wd2, w_ffn2_down, m_w_ffn2_down, v_w_ffn2_down)]
    adam_ready, (l_wg1, l_wu1) = _owner_adam(
        "adam_ready", ready, chip, comm=_Both([_ChipExchange([p_wg1]), _ChipExchange([p_wu1])]))
    dxc0, (dsh1, dsc1, dmc0, dmc1, dgam1) = _norm_mod_bwd(
        "nm1_bwd", xc0, dh1, tab3, GAM1, (SC1, MC1), L, Lc, dres=dxc1)
    grad_x = dxc0[:L][None]

    dmod_lat = jnp.concatenate([dsh1, dsc1, dg1, dsh2, dsc2, dg2, dsh3, dsc3, dg3], axis=1)
    dmod_ctx = jnp.concatenate([dmc0, dmc1, dmc2, dmc3, dmc4, jnp.zeros((1, 4 * D), F32)], axis=1)
    dmod_pack = at_row(dmod_lat, 0, SUBLANES) + at_row(dmod_ctx, 1, SUBLANES)
    (dmod_g,) = _exchange_only("ag_dmod", _Gather([dmod_pack]))
    dmod_all = dmod_g.reshape(N_DEV * SUBLANES, 9 * D)
    dmod_cols = lax.dynamic_slice_in_dim(dmod_all, me * MODW, MODW, axis=1)
    (g_wmod, dl_wmod, nm_wmod, nv_wmod, dsilu), _ = _mod_bwd_adam(
        cs, dmod_cols, w_mod[0], m_w_mod[0], v_w_mod[0])
    sg_cc = jax.nn.sigmoid(c_ctx)
    d_c_ctx = dsilu[8] * (sg_cc * (1.0 + c_ctx * (1.0 - sg_cc)))
    g_bmod, dl_bmod, nm_bmod, nv_bmod = _bias_adam(dmod_all, b_mod, m_b_mod, v_b_mod)

    dbd, dcd, dlam = jnp.stack([dbd0, dbd1]), jnp.stack([dcd0, dcd1]), jnp.stack([dlam0, dlam1])
    dbt_re = jnp.swapaxes(_block_diag_extract(dbd[..., :SW], E, P), 2, 3)
    dbt_im = jnp.swapaxes(_block_diag_extract(dbd[..., SW:], E, P), 2, 3)
    dl_re, dl_im = dlam[:, :, 0, :SW].reshape(2, G, P), dlam[:, :, 0, SW:].reshape(2, G, P)
    _, vjp = jax.vjp(_ssm_discretize, *ssm_prim)
    d_a_re, d_a_im, d_ldt, d_b_re, d_b_im = vjp((dl_re, dl_im, dbt_re, dbt_im))
    d_c_re = jnp.swapaxes(_block_diag_extract(dcd[:, :, :SW, :], P, E), 2, 3)
    d_c_im = -jnp.swapaxes(_block_diag_extract(dcd[:, :, SW:, :], P, E), 2, 3)

    dgam_all = jnp.concatenate([dgam1, dgam2, dgam3], axis=0)
    small_g = [d_c_ctx, d_qg, d_kg, d_a_re, d_a_im, d_ldt, d_b_re, d_b_im, d_c_re, d_c_im, d_ssm_d, d_b_glu,
               dgam_all]
    small_w = [c_ctx, q_norm_g, k_norm_g, ssm_a_re, ssm_a_im, ssm_log_dt, ssm_b_re, ssm_b_im, ssm_c_re, ssm_c_im,
               ssm_d, b_glu, ng_full]
    small_m = [m_c_ctx, m_q_norm_g, m_k_norm_g, m_ssm_a_re, m_ssm_a_im, m_ssm_log_dt, m_ssm_b_re, m_ssm_b_im,
               m_ssm_c_re, m_ssm_c_im, m_ssm_d, m_b_glu, m_ng_full]
    small_v = [v_c_ctx, v_q_norm_g, v_k_norm_g, v_ssm_a_re, v_ssm_a_im, v_ssm_log_dt, v_ssm_b_re, v_ssm_b_im,
               v_ssm_c_re, v_ssm_c_im, v_ssm_d, v_b_glu, v_ng_full]
    small_shapes = [a.shape for a in small_w]
    n_rows = sum(-(-math.prod(s) // LANES) for s in small_shapes)
    n_rows = -(-n_rows // 256) * 256
    (small_parts,) = _exchange_only("ag_small_grads", _Gather([_pack(small_g, n_rows)]))
    small_out = _sum_adam("small_adam", small_parts, _pack(small_w, n_rows), _pack(small_m, n_rows),
                          _pack(small_v, n_rows))
    sm_g, sm_dl, sm_m, sm_v = [_unpack(o, small_shapes) for o in small_out]

    def my_norm_cols(a):
        return lax.dynamic_slice_in_dim(a, me * dn, dn, axis=1)[None]

    for lst in (sm_g, sm_dl, sm_m, sm_v):
        lst[-1] = my_norm_cols(lst[-1])

    adam_last, _ = _owner_adam(
        "adam_last", [adam_item(p_wg1, l_wg1, w_ffn1_gate, m_w_ffn1_gate, v_w_ffn1_gate),
                      adam_item(p_wu1, l_wu1, w_ffn1_up, m_w_ffn1_up, v_w_ffn1_up)], chip)
    big_out = [[o[None] for o in grp_] for grp_ in adam_last + adam_ready]

    def leaf(kind):
        sm = (sm_g, sm_dl, sm_m, sm_v)[kind]
        mod = (g_wmod, dl_wmod, nm_wmod, nv_wmod)[kind][None]
        bmod = (g_bmod, dl_bmod, nm_bmod, nv_bmod)[kind]
        big = [b[kind] for b in big_out]
        (c_ctx_, qg_, kg_, a_re_, a_im_, ldt_, b_re_, b_im_, c_re_, c_im_, sd_, bglu_, ng_) = sm
        return [c_ctx_, mod, bmod, ng_, big[0], big[1], big[2], big[3], qg_, kg_, a_re_, a_im_, ldt_, b_re_, b_im_,
                c_re_, c_im_, sd_, big[4], bglu_, big[5], big[6], big[7], big[8], big[9], big[10]]

    return tuple([loss, grad_x] + leaf(0) + leaf(1) + leaf(2) + leaf(3))
```

```python
import math

import jax
import jax.numpy as jnp
import numpy as np
from jax import lax
from jax.experimental import pallas as pl
from jax.experimental.pallas import tpu as pltpu

F32 = jnp.float32
BF16 = jnp.bfloat16

N_DEV = 8
N_CHIPS = 4
LANES = 128
SUBLANES = 8
PACKED_SUBLANES = 16
VMEM_LIMIT = 56 * 1024 * 1024
MM_TILE = 512
MM_TILE_NT = 256
ROW_TILE = 256
HEAD_ROW_TILE = 512
ATTN_BWD_HEADS = 2
ADAM_BLOCK_BYTES = 4 * 1024 * 1024
ADAM_GROUP_VMEM = 36 * 1024 * 1024

NORM_EPS = 1e-6
GRID_W = 64
ROPE_THETA = 10000.0
SCAN_TAPS = SUBLANES
SLAB_GROUPS = 8

ADAM_LR = 0.001
ADAM_B1 = 0.9
ADAM_B2 = 0.999
ADAM_EPS = 1e-08
ADAM_WD = 0.01
ADAM_STEP = 10

NN = (((1,), (0,)), ((), ()))
NT = (((1,), (1,)), ((), ()))
TN = (((0,), (0,)), ((), ()))

MESH = pl.DeviceIdType.MESH
ANY = pl.BlockSpec(memory_space=pl.ANY)


def _tile(n, cap, align):
    best = None
    for t in range(align, min(n, cap) + 1, align):
        if n % t == 0:
            best = t
    return n if best is None else best


def _params(n_grid):
    return pltpu.CompilerParams(dimension_semantics=("arbitrary",) * n_grid, vmem_limit_bytes=VMEM_LIMIT)


def _sigmoid(x):
    return 1.0 / (1.0 + jnp.exp(-x))


GELU_K = math.sqrt(2.0 / math.pi)
GELU_C = 0.044715


def _gelu(x):
    return 0.5 * x * (1.0 + jnp.tanh(GELU_K * (x + GELU_C * x * x * x)))


def _gelu_grad(x):
    t = jnp.tanh(GELU_K * (x + GELU_C * x * x * x))
    return 0.5 * (1.0 + t) + 0.5 * x * (1.0 - t * t) * GELU_K * (1.0 + 3.0 * GELU_C * x * x)


def _adamw(w, g, m, v):
    m2 = ADAM_B1 * m + (1.0 - ADAM_B1) * g
    v2 = ADAM_B2 * v + (1.0 - ADAM_B2) * (g * g)
    m_hat = m2 / (1.0 - ADAM_B1 ** ADAM_STEP)
    v_hat = v2 / (1.0 - ADAM_B2 ** ADAM_STEP)
    delta = -ADAM_LR * (m_hat / (jnp.sqrt(v_hat) + ADAM_EPS) + ADAM_WD * w)
    return delta, m2, v2


def _position():
    return lax.axis_index("x"), lax.axis_index("y"), lax.axis_index("c")


class _Gather:
    def __init__(self, arrays):
        self.arrays = list(arrays)
        n = len(self.arrays)
        self.out_shapes = [jax.ShapeDtypeStruct((N_DEV,) + a.shape, a.dtype) for a in self.arrays]
        self.scratch = [pltpu.SemaphoreType.DMA((n, 7)), pltpu.SemaphoreType.DMA((n, 7)),
                        pltpu.SemaphoreType.DMA((n,))]

    def _plan(self, ins, outs, sems):
        send, recv, local = sems
        x, y, c = _position()
        me, sibling = (x, y, c), (x, y, 1 - c)
        chips = [(1 - x, y), (x, 1 - y), (1 - x, 1 - y)]

        def slot(a, p):
            return outs[a].at[4 * p[0] + 2 * p[1] + p[2]]

        def copy(a, k, block, to, src=None):
            dst = slot(a, block)
            return pltpu.make_async_remote_copy(
                src_ref=dst if src is None else src, dst_ref=dst,
                send_sem=send.at[a, k], recv_sem=recv.at[a, k], device_id=to, device_id_type=MESH)

        mine = [pltpu.make_async_copy(ins[a], slot(a, me), local.at[a]) for a in range(len(ins))]
        return me, sibling, chips, c, copy, mine

    def start(self, ins, outs, sems):
        me, sibling, chips, c, copy, mine = self._plan(ins, outs, sems)
        for cp in mine:
            cp.start()
        for a in range(len(ins)):
            copy(a, 0, me, sibling, src=ins[a]).start()
            for j, chip in enumerate(chips):
                copy(a, 1 + j, me, (*chip, c), src=ins[a]).start()

    def finish(self, ins, outs, sems):
        me, sibling, chips, c, copy, mine = self._plan(ins, outs, sems)
        n = len(ins)
        for j, chip in enumerate(chips):
            for a in range(n):
                copy(a, 1 + j, (*chip, c), me).wait_recv()
                copy(a, 4 + j, (*chip, c), sibling).start()
        for a in range(n):
            copy(a, 0, sibling, me).wait_recv()
        for j, chip in enumerate(chips):
            for a in range(n):
                copy(a, 4 + j, (*chip, 1 - c), me).wait_recv()
        for a in range(n):
            copy(a, 0, me, sibling, src=ins[a]).wait_send()
            for j, chip in enumerate(chips):
                copy(a, 1 + j, me, (*chip, c), src=ins[a]).wait_send()
                copy(a, 4 + j, (*chip, c), sibling).wait_send()
        for cp in mine:
            cp.wait()


class _SiblingSwap:
    def __init__(self, arrays):
        self.arrays = list(arrays)
        n = len(self.arrays)
        self.out_shapes = [jax.ShapeDtypeStruct((N_CHIPS,) + a.shape[1:], a.dtype) for a in self.arrays]
        self.scratch = [pltpu.SemaphoreType.DMA((n, N_CHIPS)), pltpu.SemaphoreType.DMA((n, N_CHIPS))]

    def _plan(self, ins, outs, sems):
        send, recv = sems
        x, y, c = _position()
        return [pltpu.make_async_remote_copy(
            src_ref=ins[a].at[2 * j + 1 - c], dst_ref=outs[a].at[j],
            send_sem=send.at[a, j], recv_sem=recv.at[a, j], device_id=(x, y, 1 - c), device_id_type=MESH)
            for a in range(len(ins)) for j in range(N_CHIPS)]

    def start(self, ins, outs, sems):
        for cp in self._plan(ins, outs, sems):
            cp.start()

    def finish(self, ins, outs, sems):
        copies = self._plan(ins, outs, sems)
        for cp in copies:
            cp.wait_recv()
        for cp in copies:
            cp.wait_send()


class _ChipExchange:
    def __init__(self, arrays):
        self.arrays = list(arrays)
        n = len(self.arrays)
        self.out_shapes = [jax.ShapeDtypeStruct((N_CHIPS - 1,) + a.shape[1:], a.dtype) for a in self.arrays]
        self.scratch = [pltpu.SemaphoreType.DMA((n, N_CHIPS - 1)), pltpu.SemaphoreType.DMA((n, N_CHIPS - 1))]

    def _plan(self, ins, outs, sems):
        send, recv = sems
        x, y, c = _position()
        copies = []
        for r in range(1, N_CHIPS):
            px, py = x ^ (r >> 1), y ^ (r & 1)
            for a in range(len(ins)):
                copies.append(pltpu.make_async_remote_copy(
                    src_ref=ins[a].at[2 * px + py], dst_ref=outs[a].at[r - 1],
                    send_sem=send.at[a, r - 1], recv_sem=recv.at[a, r - 1],
                    device_id=(px, py, c), device_id_type=MESH))
        return copies

    def start(self, ins, outs, sems):
        for cp in self._plan(ins, outs, sems):
            cp.start()

    def finish(self, ins, outs, sems):
        copies = self._plan(ins, outs, sems)
        for cp in copies:
            cp.wait_recv()
        for cp in copies:
            cp.wait_send()


class _Both:
    def __init__(self, comms):
        self.comms = list(comms)
        self.arrays = [a for cm in self.comms for a in cm.arrays]
        self.out_shapes = [s for cm in self.comms for s in cm.out_shapes]
        self.scratch = [s for cm in self.comms for s in cm.scratch]

    def _split(self, ins, outs, sems):
        i = o = s = 0
        for cm in self.comms:
            ni, no, nsem = len(cm.arrays), len(cm.out_shapes), len(cm.scratch)
            yield cm, ins[i:i + ni], outs[o:o + no], sems[s:s + nsem]
            i, o, s = i + ni, o + no, s + nsem

    def start(self, ins, outs, sems):
        for cm, i, o, s in self._split(ins, outs, sems):
            cm.start(i, o, s)

    def finish(self, ins, outs, sems):
        for cm, i, o, s in self._split(ins, outs, sems):
            cm.finish(i, o, s)


def _host_call(body, *, name, grid, operands, in_specs, out_shape, out_specs, scratch_shapes=(), comm=None,
               prefetch=()):
    grid = tuple(grid)
    n_pre, n_in, n_out, n_scr = len(prefetch), len(operands), len(out_shape), len(scratch_shapes)
    nc_in, nc_out = (len(comm.arrays), len(comm.out_shapes)) if comm else (0, 0)
    all_in = list(in_specs) + [ANY] * nc_in
    all_out = list(out_specs) + [ANY] * nc_out
    all_scr = list(scratch_shapes) + (list(comm.scratch) if comm else [])
    all_shape = list(out_shape) + (list(comm.out_shapes) if comm else [])
    kwargs = dict(name=name, compiler_params=_params(len(grid)), out_shape=all_shape)
    if n_pre:
        kwargs["grid_spec"] = pltpu.PrefetchScalarGridSpec(
            num_scalar_prefetch=n_pre, grid=grid, in_specs=all_in, out_specs=all_out, scratch_shapes=all_scr)
    else:
        kwargs.update(in_specs=all_in, out_specs=all_out, scratch_shapes=all_scr)
        if grid:
            kwargs["grid"] = grid
    args = list(prefetch) + list(operands) + (list(comm.arrays) if comm else [])
    if comm is None:
        return list(pl.pallas_call(body, **kwargs)(*args)), []

    def hosted(*refs):
        bounds = [0, n_pre, n_pre + n_in]
        for n in (nc_in, n_out, nc_out, n_scr):
            bounds.append(bounds[-1] + n)
        bounds.append(len(refs))
        pre, ins, cins, outs, couts, scr, sems = [refs[a:b] for a, b in zip(bounds[:-1], bounds[1:])]
        if not grid:
            comm.start(cins, couts, sems)
            body(*pre, *ins, *outs, *scr)
            comm.finish(cins, couts, sems)
            return
        first, last = None, None
        for ax, size in enumerate(grid):
            pid = pl.program_id(ax)
            f, l = pid == 0, pid == size - 1
            first = f if first is None else jnp.logical_and(first, f)
            last = l if last is None else jnp.logical_and(last, l)

        @pl.when(first)
        def _():
            comm.start(cins, couts, sems)

        body(*pre, *ins, *outs, *scr)

        @pl.when(last)
        def _():
            comm.finish(cins, couts, sems)

    res = pl.pallas_call(hosted, **kwargs)(*args)
    return list(res[:n_out]), list(res[n_out:])


def _exchange_only(name, comm):
    def body():
        pass
    return _host_call(body, name=name, grid=(), operands=[], in_specs=[], out_shape=[], out_specs=[], comm=comm)[1]


def _matmul(name, grid, operands, in_specs, pairs, out_shapes, out_specs, epilogue, acc_shapes=(), nk=1,
            prologue=None, comm=None):
    n_in, n_out = len(operands), len(out_shapes)
    prologue = prologue or {}

    def body(*refs):
        ins, outs, accs = refs[:n_in], refs[n_in:n_in + n_out], refs[n_in + n_out:]
        pids = [pl.program_id(ax) for ax in range(len(grid))]

        def operand(i, blk=None):
            v = ins[i][...] if blk is None else ins[i][blk]
            if i in prologue:
                v = prologue[i](v)
            return v.astype(BF16)

        def products():
            vals = {}
            for pair in pairs:
                ai, bi, ci, dn = pair[:4]
                if len(pair) == 5:
                    p = None
                    for blk in range(pair[4]):
                        q = lax.dot_general(operand(ai, blk), operand(bi, blk), dn, preferred_element_type=F32)
                        p = q if p is None else p + q
                else:
                    p = lax.dot_general(operand(ai), operand(bi), dn, preferred_element_type=F32)
                vals[ci] = p if ci not in vals else vals[ci] + p
            return [vals[ci] for ci in sorted(vals)]

        if nk == 1:
            epilogue(products(), ins, outs, pids)
        else:
            k = pids[-1]
            prods = products()

            @pl.when(k == 0)
            def _():
                for acc, p in zip(accs, prods):
                    acc[...] = p

            @pl.when(k > 0)
            def _():
                for acc, p in zip(accs, prods):
                    acc[...] += p

            @pl.when(k == nk - 1)
            def _():
                epilogue([acc[...] for acc in accs], ins, outs, pids)

    return _host_call(
        body, name=name, grid=grid, operands=operands, in_specs=in_specs, out_shape=out_shapes, out_specs=out_specs,
        scratch_shapes=[pltpu.VMEM(s, F32) for s in acc_shapes] if nk > 1 else [], comm=comm)


def _rowwise(name, n_tiles, operands, in_specs, out_shapes, out_specs, red_widths, fn, comm=None):
    n_in, n_out, n_red = len(operands), len(out_shapes), len(red_widths)

    def body(*refs):
        ins, outs, reds = refs[:n_in], refs[n_in:n_in + n_out], refs[n_in + n_out:]
        i = pl.program_id(0)
        vals, sums = fn(i, *[r[...] for r in ins])
        for o, v in zip(outs, vals):
            o[...] = v.astype(o.dtype)
        if n_red:
            @pl.when(i == 0)
            def _():
                for r, s in zip(reds, sums):
                    r[...] = s

            @pl.when(i > 0)
            def _():
                for r, s in zip(reds, sums):
                    r[...] += s

    red_shapes = [jax.ShapeDtypeStruct((1, w), F32) for w in red_widths]
    red_specs = [pl.BlockSpec((1, w), lambda i: (0, 0)) for w in red_widths]
    res, cres = _host_call(
        body, name=name, grid=(n_tiles,), operands=operands, in_specs=in_specs,
        out_shape=list(out_shapes) + red_shapes, out_specs=list(out_specs) + red_specs, comm=comm)
    return res[:n_out], res[n_out:], cres


def _colsum(v):
    return jnp.sum(v, axis=0, keepdims=True)


def _store_all(accs, ins, outs, pids):
    for o, v in zip(outs, accs):
        o[...] = v.astype(o.dtype)


def _row_tile(rows_a, rows_b):
    return _tile(math.gcd(rows_a, rows_b) if rows_b else rows_a, ROW_TILE, SUBLANES)


def _tab_row(d, nlt, rows2):
    return pl.BlockSpec((None, 1, d), lambda i: (jnp.where(i < nlt, rows2[0], rows2[1]), 0, 0))


def _norm_mod_fwd(name, xs, tab, r_gamma, r_shift, r_scale, n_lat, n_ctx):
    rows, d = xs.shape
    tm = _row_tile(n_lat, n_ctx)
    nlt = n_lat // tm

    def fn(i, x, g, sh, sc):
        xh = x * lax.rsqrt(jnp.mean(x * x, axis=-1, keepdims=True) + NORM_EPS)
        return [(xh * g) * (1.0 + sc) + sh], []

    (h,), _, _ = _rowwise(
        name, rows // tm, [xs, tab, tab, tab],
        [pl.BlockSpec((tm, d), lambda i: (i, 0)), _tab_row(d, nlt, (r_gamma, r_gamma)), _tab_row(d, nlt, r_shift),
         _tab_row(d, nlt, r_scale)],
        [jax.ShapeDtypeStruct((rows, d), BF16)], [pl.BlockSpec((tm, d), lambda i: (i, 0))], [], fn)
    return h


def _norm_mod_bwd(name, xs, dh, tab, r_gamma, r_scale, n_lat, n_ctx, dres=None):
    rows, d = xs.shape
    tm = _row_tile(n_lat, n_ctx)
    nlt = n_lat // tm
    row = pl.BlockSpec((tm, d), lambda i: (i, 0))

    def fn(i, x, dy, g, sc, *res):
        rstd = lax.rsqrt(jnp.mean(x * x, axis=-1, keepdims=True) + NORM_EPS)
        xh = x * rstd
        dsh = _colsum(dy)
        dsc = _colsum(dy * (xh * g))
        dn = dy * (1.0 + sc)
        dgam = _colsum(dn * xh)
        dxh = dn * g
        dx = rstd * (dxh - xh * jnp.mean(dxh * xh, axis=-1, keepdims=True))
        if res:
            dx = dx + jnp.where(i < nlt, res[0], 0.0)
        lat = (i < nlt).astype(F32)
        return [dx], [dsh * lat, dsc * lat, dsh * (1.0 - lat), dsc * (1.0 - lat), dgam]

    operands = [xs, dh, tab, tab]
    specs = [row, row, _tab_row(d, nlt, (r_gamma, r_gamma)), _tab_row(d, nlt, r_scale)]
    if dres is not None:
        operands.append(dres)
        specs.append(pl.BlockSpec((tm, d), lambda i: (jnp.minimum(i, nlt - 1), 0)))
    (dx,), sums, _ = _rowwise(name, rows // tm, operands, specs,
                              [jax.ShapeDtypeStruct((rows, d), F32)], [row], [d] * 5, fn)
    return dx, sums


def _gate_bwd(name, dx, f, tab, r_gate, coef, n_lat, n_ctx):
    rows, d = dx.shape
    tm = _row_tile(n_lat, n_ctx)
    nlt = n_lat // tm
    row = pl.BlockSpec((tm, d), lambda i: (i, 0))

    def fn(i, dxv, fv, gv):
        dg = _colsum(dxv * fv) * coef
        lat = (i < nlt).astype(F32)
        return [(coef * gv) * dxv], [dg * lat, dg * (1.0 - lat)]

    (df,), sums, _ = _rowwise(
        name, rows // tm, [dx, f, tab],
        [row, row, _tab_row(d, nlt, r_gate)],
        [jax.ShapeDtypeStruct((rows, d), BF16)], [row], [d, d], fn)
    return df, sums


def _select_rows(i, tm, n_lat, v_lat, v_ctx):
    rows = i * tm + lax.broadcasted_iota(jnp.int32, (tm, 1), 0)
    return jnp.where(rows < n_lat, v_lat, v_ctx)


def _ffn_up(tag, h, wg, wu, comm=None):
    rows, d = h.shape
    nb, fs, _ = wg.shape
    tm = _tile(rows, MM_TILE, LANES)
    blk = pl.BlockSpec((None, tm, fs), lambda j, i: (j, i, 0))
    wspec = pl.BlockSpec((None, fs, d), lambda j, i: (j, 0, 0))

    def epilogue(accs, ins, outs, pids):
        a, b = accs
        outs[0][...] = a.astype(BF16)
        outs[1][...] = b.astype(BF16)
        outs[2][...] = (a * _sigmoid(a) * b).astype(BF16)

    hid = jax.ShapeDtypeStruct((nb, rows, fs), BF16)
    (a, b, s), cres = _matmul(
        tag + "_up", (nb, rows // tm), [h, wg, wu],
        [pl.BlockSpec((tm, d), lambda j, i: (i, 0)), wspec, wspec],
        [(0, 1, 0, NT), (0, 2, 1, NT)], [hid, hid, hid], [blk, blk, blk], epilogue, comm=comm)
    return a, b, s, cres


def _ffn_down(tag, s, wd, xs, tab2, r_gate, n_lat, comm=None):
    nb, rows, fs = s.shape
    d = wd.shape[-1]
    tm = _tile(rows, MM_TILE, LANES)
    tn = _tile(d, MM_TILE, LANES)

    def epilogue(accs, ins, outs, pids):
        f = accs[0]
        g = ins[3][...]
        gate = _select_rows(pids[0], tm, n_lat, g[r_gate[0]:r_gate[0] + 1, :], g[r_gate[1]:r_gate[1] + 1, :])
        outs[0][...] = f
        outs[1][...] = ins[2][...] + 0.5 * gate * f

    out = jax.ShapeDtypeStruct((rows, d), F32)
    ospec = pl.BlockSpec((tm, tn), lambda i, n: (i, n))
    (f, xo), cres = _matmul(
        tag + "_down", (rows // tm, d // tn), [s, wd, xs, tab2],
        [pl.BlockSpec((nb, tm, fs), lambda i, n: (0, i, 0)), pl.BlockSpec((nb, fs, tn), lambda i, n: (0, 0, n)),
         ospec, pl.BlockSpec((tab2.shape[0], tn), lambda i, n: (0, n))],
        [(0, 1, 0, NN, nb)], [out, out], [ospec, ospec], epilogue, comm=comm)
    return f, xo, cres


def _ffn_ds(tag, df, wd, a, b, comm=None):
    rows, d = df.shape
    nb, fs, _ = wd.shape
    tm = _tile(rows, MM_TILE, LANES)
    blk = pl.BlockSpec((None, tm, fs), lambda j, i: (j, i, 0))

    def epilogue(accs, ins, outs, pids):
        ds = accs[0]
        av = ins[2][...].astype(F32)
        bv = ins[3][...].astype(F32)
        sg = _sigmoid(av)
        outs[0][...] = (ds * bv * (sg * (1.0 + av * (1.0 - sg)))).astype(BF16)
        outs[1][...] = (ds * (av * sg)).astype(BF16)

    hid = jax.ShapeDtypeStruct((nb, rows, fs), BF16)
    (da, db), cres = _matmul(
        tag + "_ds", (nb, rows // tm), [df, wd, a, b],
        [pl.BlockSpec((tm, d), lambda j, i: (i, 0)), pl.BlockSpec((None, fs, d), lambda j, i: (j, 0, 0)), blk, blk],
        [(0, 1, 0, NT)], [hid, hid], [blk, blk], epilogue, comm=comm)
    return da, db, cres


def _ffn_dwd(tag, s, df, comm=None):
    nb, rows, fs = s.shape
    d = df.shape[-1]
    tn = _tile(d, MM_TILE, LANES)
    (dwd,), cres = _matmul(
        tag + "_dwd", (nb, d // tn), [s, df],
        [pl.BlockSpec((None, rows, fs), lambda j, n: (j, 0, 0)), pl.BlockSpec((rows, tn), lambda j, n: (0, n))],
        [(0, 1, 0, TN)], [jax.ShapeDtypeStruct((nb, fs, d), BF16)],
        [pl.BlockSpec((None, fs, tn), lambda j, n: (j, 0, n))], _store_all, comm=comm)
    return dwd, cres


def _ffn_dwgu(tag, h, da, db, comm=None):
    rows, d = h.shape
    nb, _, fs = da.shape
    tno = _tile(d, MM_TILE, LANES)
    full = pl.BlockSpec((None, rows, fs), lambda j, m: (j, 0, 0))
    wshape = jax.ShapeDtypeStruct((nb, fs, d), BF16)
    wblk = pl.BlockSpec((None, fs, tno), lambda j, m: (j, 0, m))
    (dwg, dwu), cres = _matmul(
        tag + "_dwgu", (nb, d // tno), [h, da, db],
        [pl.BlockSpec((rows, tno), lambda j, m: (0, m)), full, full],
        [(1, 0, 0, TN), (2, 0, 1, TN)], [wshape, wshape], [wblk, wblk], _store_all, comm=comm)
    return dwg, dwu, cres


def _ffn_dh(tag, da, db, wg, wu, comm=None):
    nb, rows, fs = da.shape
    d = wg.shape[2]
    tm = _tile(rows, MM_TILE, LANES)
    tn = _tile(d, MM_TILE_NT, LANES)
    aspec = pl.BlockSpec((nb, tm, fs), lambda i, n: (0, i, 0))
    wspec = pl.BlockSpec((nb, fs, tn), lambda i, n: (0, 0, n))
    (dh,), cres = _matmul(
        tag + "_dh", (rows // tm, d // tn), [da, wg, db, wu], [aspec, wspec, aspec, wspec],
        [(0, 1, 0, NN, nb), (2, 3, 0, NN, nb)], [jax.ShapeDtypeStruct((rows, d), F32)],
        [pl.BlockSpec((tm, tn), lambda i, n: (i, n))], _store_all, comm=comm)
    return dh, cres


def _rope_tables(n_lat, n_ctx):
    half = LANES // 4
    inv_freq = (np.float32(ROPE_THETA) ** (-np.arange(half, dtype=np.float32) / np.float32(half))).astype(np.float32)
    pos = np.arange(n_lat)
    ang_r = (pos // GRID_W).astype(np.float32)[:, None] * inv_freq
    ang_c = (pos % GRID_W).astype(np.float32)[:, None] * inv_freq
    cos_l = np.concatenate([np.cos(ang_r)] * 2 + [np.cos(ang_c)] * 2, axis=1)
    sin_l = np.concatenate([-np.sin(ang_r), np.sin(ang_r), -np.sin(ang_c), np.sin(ang_c)], axis=1)
    cos_all = np.concatenate([cos_l, np.ones((n_ctx, LANES), np.float32)], axis=0).astype(np.float32)
    sin_all = np.concatenate([sin_l, np.zeros((n_ctx, LANES), np.float32)], axis=0).astype(np.float32)
    return jnp.asarray(cos_all), jnp.asarray(sin_all)


def _swap_halves(x):
    lane = lax.broadcasted_iota(jnp.int32, x.shape, 1)
    return jnp.where((lane % 64) < 32, pltpu.roll(x, 96, 1), pltpu.roll(x, 32, 1))


def _heads_spec(tq, hb, width, first_block):
    per_shard = width // (hb * LANES)

    def index(k, i):
        blk = first_block + k
        return blk // per_shard, i, blk % per_shard
    return pl.BlockSpec((None, tq, hb * LANES), index)


def _qk_prep(name, src, first_block, hb, n_heads, rows, g, cos_t, sin_t):
    tq = _tile(rows, HEAD_ROW_TILE, SUBLANES)
    tab = pl.BlockSpec((tq, LANES), lambda k, i: (i, 0))

    def body(x_ref, g_ref, c_ref, s_ref, o_ref):
        for h in range(hb):
            x = x_ref[:, h * LANES:(h + 1) * LANES]
            n = x * lax.rsqrt(jnp.mean(x * x, axis=-1, keepdims=True) + NORM_EPS) * g_ref[...]
            o_ref[h] = (n * c_ref[...] + _swap_halves(n) * s_ref[...]).astype(BF16)

    return pl.pallas_call(
        body, name=name, grid=(n_heads // hb, rows // tq),
        in_specs=[_heads_spec(tq, hb, src.shape[-1], first_block), pl.BlockSpec((1, LANES), lambda k, i: (0, 0)),
                  tab, tab],
        out_specs=pl.BlockSpec((hb, tq, LANES), lambda k, i: (k, i, 0)),
        out_shape=jax.ShapeDtypeStruct((n_heads, rows, LANES), BF16), compiler_params=_params(2),
    )(src, g, cos_t, sin_t)


def _qk_prep_bwd(name, dy, src, first_block, hb, n_heads, rows, g, cos_t, sin_t):
    tq = _tile(rows, HEAD_ROW_TILE, SUBLANES)
    tab = pl.BlockSpec((tq, LANES), lambda k, i: (i, 0))

    def body(dy_ref, x_ref, g_ref, c_ref, s_ref, dx_ref, dg_ref):
        g = g_ref[...]
        dg = None
        for h in range(hb):
            x = x_ref[:, h * LANES:(h + 1) * LANES]
            dyv = dy_ref[h]
            rstd = lax.rsqrt(jnp.mean(x * x, axis=-1, keepdims=True) + NORM_EPS)
            xh = x * rstd
            dn = dyv * c_ref[...] + _swap_halves(dyv * s_ref[...])
            dxh = dn * g
            dx = rstd * (dxh - xh * jnp.mean(dxh * xh, axis=-1, keepdims=True))
            dx_ref[:, h * LANES:(h + 1) * LANES] = dx.astype(BF16)
            part = _colsum(dn * xh)
            dg = part if dg is None else dg + part
        first = jnp.logical_and(pl.program_id(0) == 0, pl.program_id(1) == 0)

        @pl.when(first)
        def _():
            dg_ref[...] = dg

        @pl.when(jnp.logical_not(first))
        def _():
            dg_ref[...] += dg

    return pl.pallas_call(
        body, name=name, grid=(n_heads // hb, rows // tq),
        in_specs=[pl.BlockSpec((hb, tq, LANES), lambda k, i: (k, i, 0)),
                  _heads_spec(tq, hb, src.shape[-1], first_block),
                  pl.BlockSpec((1, LANES), lambda k, i: (0, 0)), tab, tab],
        out_specs=[pl.BlockSpec((tq, hb * LANES), lambda k, i: (i, k)),
                   pl.BlockSpec((1, LANES), lambda k, i: (0, 0))],
        out_shape=[jax.ShapeDtypeStruct((rows, n_heads * LANES), BF16), jax.ShapeDtypeStruct((1, LANES), F32)],
        compiler_params=_params(2),
    )(dy, src, g, cos_t, sin_t)


def _heads_cast(name, src, first_block, hb, n_heads, rows):
    tq = _tile(rows, HEAD_ROW_TILE, SUBLANES)

    def body(x_ref, o_ref):
        for h in range(hb):
            o_ref[h] = x_ref[:, h * LANES:(h + 1) * LANES].astype(BF16)

    return pl.pallas_call(
        body, name=name, grid=(n_heads // hb, rows // tq),
        in_specs=[_heads_spec(tq, hb, src.shape[-1], first_block)],
        out_specs=pl.BlockSpec((hb, tq, LANES), lambda k, i: (k, i, 0)),
        out_shape=jax.ShapeDtypeStruct((n_heads, rows, LANES), BF16), compiler_params=_params(2),
    )(src)


def _heads_merge(name, src):
    n_heads, rows, _ = src.shape
    tq = _tile(rows, HEAD_ROW_TILE, SUBLANES)

    def body(x_ref, o_ref):
        for h in range(n_heads):
            o_ref[:, h * LANES:(h + 1) * LANES] = x_ref[h].astype(BF16)

    return pl.pallas_call(
        body, name=name, grid=(rows // tq,),
        in_specs=[pl.BlockSpec((n_heads, tq, LANES), lambda i: (0, i, 0))],
        out_specs=pl.BlockSpec((tq, n_heads * LANES), lambda i: (i, 0)),
        out_shape=jax.ShapeDtypeStruct((rows, n_heads * LANES), BF16), compiler_params=_params(1),
    )(src)


def _attn_fwd(q, k, v, q_per_kv, comm=None):
    nq, l, _ = q.shape
    s_len = k.shape[1]
    tq = _tile(l, ROW_TILE, SUBLANES)
    scale = LANES ** -0.5
    kv = pl.BlockSpec((None, s_len, LANES), lambda h, i: (h // q_per_kv, 0, 0))

    def body(q_ref, k_ref, v_ref, o_ref):
        s = lax.dot_general(q_ref[...], k_ref[...], NT, preferred_element_type=F32) * scale
        p = jnp.exp(s - jnp.max(s, axis=-1, keepdims=True))
        den = jnp.sum(p, axis=-1, keepdims=True)
        o = jnp.dot(p.astype(BF16), v_ref[...], preferred_element_type=F32)
        o_ref[...] = (o / den).astype(BF16)

    (o,), cres = _host_call(
        body, name="attn_fwd", grid=(nq, l // tq), operands=[q, k, v],
        in_specs=[pl.BlockSpec((None, tq, LANES), lambda h, i: (h, i, 0)), kv, kv],
        out_shape=[jax.ShapeDtypeStruct((l, nq * LANES), BF16)],
        out_specs=[pl.BlockSpec((tq, LANES), lambda h, i: (i, h))], comm=comm)
    return o, cres


def _attn_bwd(q, k, v, do, q_per_kv, comm=None):
    nq, l, _ = q.shape
    nkv, s_len, _ = k.shape
    tq = _tile(l, ROW_TILE, SUBLANES)
    scale = LANES ** -0.5
    hp = ATTN_BWD_HEADS if q_per_kv % ATTN_BWD_HEADS == 0 else 1
    kv = pl.BlockSpec((None, s_len, LANES), lambda g, r, i: (g, 0, 0))
    qs = pl.BlockSpec((hp, tq, LANES), lambda g, r, i: (g * (q_per_kv // hp) + r, i, 0))

    def body(q_ref, k_ref, v_ref, do_ref, dq_ref, dk_ref, dv_ref):
        kvv, vv = k_ref[...], v_ref[...]
        dk_new = dv_new = None
        for h in range(hp):
            qv, dov = q_ref[h], do_ref[:, h * LANES:(h + 1) * LANES]
            st = lax.dot_general(kvv, qv, NT, preferred_element_type=F32) * scale
            e = jnp.exp(st - jnp.max(st, axis=0, keepdims=True))
            pt = e / jnp.sum(e, axis=0, keepdims=True)
            dpt = lax.dot_general(vv, dov, NT, preferred_element_type=F32)
            delta = jnp.sum(pt * dpt, axis=0, keepdims=True)
            dst = (pt * (dpt - delta) * scale).astype(BF16)
            dq_ref[h] = lax.dot_general(dst, kvv, TN, preferred_element_type=F32)
            dk_h = jnp.dot(dst, qv, preferred_element_type=F32)
            dv_h = jnp.dot(pt.astype(BF16), dov, preferred_element_type=F32)
            dk_new = dk_h if dk_new is None else dk_new + dk_h
            dv_new = dv_h if dv_new is None else dv_new + dv_h
        first = jnp.logical_and(pl.program_id(1) == 0, pl.program_id(2) == 0)

        @pl.when(first)
        def _():
            dk_ref[...] = dk_new
            dv_ref[...] = dv_new

        @pl.when(jnp.logical_not(first))
        def _():
            dk_ref[...] += dk_new
            dv_ref[...] += dv_new

    (dq, dk, dv), cres = _host_call(
        body, name="attn_bwd", grid=(nkv, q_per_kv // hp, l // tq), operands=[q, k, v, do],
        in_specs=[qs, kv, kv, pl.BlockSpec((tq, hp * LANES), lambda g, r, i: (i, g * (q_per_kv // hp) + r))],
        out_specs=[qs, kv, kv],
        out_shape=[jax.ShapeDtypeStruct((nq, l, LANES), F32), jax.ShapeDtypeStruct((nkv, s_len, LANES), F32),
                   jax.ShapeDtypeStruct((nkv, s_len, LANES), F32)], comm=comm)
    return dq, dk, dv, cres


def _zoh(a_re, a_im, log_dt):
    dt = jnp.exp(log_dt)[..., None]
    mag = jnp.exp(a_re * dt)
    lb_re = mag * jnp.cos(a_im * dt)
    lb_im = mag * jnp.sin(a_im * dt)
    den = a_re * a_re + a_im * a_im
    coef_re = ((lb_re - 1.0) * a_re + lb_im * a_im) / den
    coef_im = (lb_im * a_re - (lb_re - 1.0) * a_im) / den
    return lb_re, lb_im, coef_re, coef_im


def _ssm_discretize(a_re, a_im, log_dt, b_re, b_im):
    lb_re, lb_im, cr, ci = _zoh(a_re, a_im, log_dt)
    bt_re = cr[..., None] * b_re - ci[..., None] * b_im
    bt_im = cr[..., None] * b_im + ci[..., None] * b_re
    return lb_re, lb_im, bt_re, bt_im


def _lambda_powers(a_re, a_im, log_dt, ns):
    dt = jnp.exp(log_dt)[..., None]
    k = jnp.arange(SCAN_TAPS + 1, dtype=F32)[:, None, None, None]
    mag, ang = jnp.exp(k * (a_re * dt)), k * (a_im * dt)
    shape = (SCAN_TAPS + 1, 2, ns, -1)
    return (mag * jnp.cos(ang)).reshape(shape), (mag * jnp.sin(ang)).reshape(shape)


def _slab_mask():
    idx = jnp.arange(SLAB_GROUPS)
    return (idx[:, None] == idx[None, :])[None, None, :, None, :, None]


def _block_diag(m):
    d, g, a, b = m.shape
    ns = g // SLAB_GROUPS
    wide = jnp.where(_slab_mask(), m.reshape(d, ns, SLAB_GROUPS, a, 1, b), 0.0)
    return wide.reshape(d, ns, SLAB_GROUPS * a, SLAB_GROUPS * b)


def _block_diag_extract(m, a, b):
    d, ns = m.shape[:2]
    m = m.reshape(d, ns, SLAB_GROUPS, a, SLAB_GROUPS, b)
    return jnp.sum(jnp.where(_slab_mask(), m, 0.0), axis=4).reshape(d, ns * SLAB_GROUPS, a, b)


def _tap_weights(base_re, base_im, pw_re, pw_im):
    pr = jnp.transpose(pw_re[:SCAN_TAPS], (1, 2, 0, 3))[:, :, :, None, :]
    pi = jnp.transpose(pw_im[:SCAN_TAPS], (1, 2, 0, 3))[:, :, :, None, :]
    br, bi = base_re[:, :, None], base_im[:, :, None]
    d, ns, cdim, s = base_re.shape
    re = (pr * br - pi * bi).reshape(d, ns, SCAN_TAPS * cdim, s)
    im = (pr * bi + pi * br).reshape(d, ns, SCAN_TAPS * cdim, s)
    return jnp.concatenate([re, im], axis=-1)


def _carry_tables(pw_re, pw_im, descending):
    def rows(pw):
        asc = pw[1:]
        per_dir = [asc[::-1, d] if descending[d] else asc[:, d] for d in range(2)]
        return jnp.transpose(jnp.stack(per_dir), (0, 2, 1, 3))
    return jnp.concatenate([rows(pw_re), rows(pw_im)], axis=-1)


def _scan_chunk(x, w_ref, tab_ref, s_ref, carry_ref, descending, t_rows, sw):
    row8 = lax.broadcasted_iota(jnp.int32, x.shape, 0) % SCAN_TAPS
    pieces = [x.astype(BF16)]
    for tau in range(1, SCAN_TAPS):
        if descending:
            sh = jnp.where(row8 <= SCAN_TAPS - 1 - tau, pltpu.roll(x, t_rows - tau, 0), 0.0)
        else:
            sh = jnp.where(row8 >= tau, pltpu.roll(x, tau, 0), 0.0)
        pieces.append(sh.astype(BF16))
    xa = jnp.concatenate(pieces, axis=1)
    s_ref[...] = jnp.dot(xa, w_ref[...], preferred_element_type=F32)
    tab = tab_ref[...]
    t_re, t_im = tab[:, :sw], tab[:, sw:]
    nb = t_rows // SCAN_TAPS
    edge = 0 if descending else SCAN_TAPS - 1

    def step(b, carry):
        h_re, h_im = carry
        r0 = pl.multiple_of(((nb - 1 - b) if descending else b) * SCAN_TAPS, SCAN_TAPS)
        x_re = s_ref[pl.ds(r0, SCAN_TAPS), :sw] + t_re * h_re - t_im * h_im
        x_im = s_ref[pl.ds(r0, SCAN_TAPS), sw:] + t_re * h_im + t_im * h_re
        s_ref[pl.ds(r0, SCAN_TAPS), :sw] = x_re
        s_ref[pl.ds(r0, SCAN_TAPS), sw:] = x_im
        return x_re[edge:edge + 1, :], x_im[edge:edge + 1, :]

    h_re, h_im = lax.fori_loop(0, nb, step, (carry_ref[0:1, :sw], carry_ref[0:1, sw:]))
    carry_ref[0:1, :sw] = h_re
    carry_ref[0:1, sw:] = h_im


def _ssm_fwd(name, dr, u_src, u_shard, waug, tab, cd, descending, chunk_of, t_rows, rows, comm=None):
    _, ns, kdim, sw2 = waug.shape
    sw = sw2 // 2
    width = ns * LANES
    nchunks = rows // t_rows

    def body(u_ref, w_ref, tab_ref, cd_ref, y_ref, h_ref, s_ref, carry_ref):
        @pl.when(pl.program_id(1) == 0)
        def _():
            carry_ref[...] = jnp.zeros_like(carry_ref)

        _scan_chunk(u_ref[...], w_ref, tab_ref, s_ref, carry_ref, descending, t_rows, sw)
        hb = s_ref[...].astype(BF16)
        h_ref[...] = hb
        y_ref[...] = jnp.dot(hb, cd_ref[...], preferred_element_type=F32)

    (y, h), cres = _host_call(
        body, name=name, grid=(ns, nchunks), operands=[u_src, waug, tab, cd],
        in_specs=[pl.BlockSpec((None, t_rows, LANES), lambda s, i: (u_shard, chunk_of(i), s)),
                  pl.BlockSpec((None, None, kdim, sw2), lambda s, i: (dr, s, 0, 0)),
                  pl.BlockSpec((None, None, SCAN_TAPS, sw2), lambda s, i: (dr, s, 0, 0)),
                  pl.BlockSpec((None, None, sw2, LANES), lambda s, i: (dr, s, 0, 0))],
        out_specs=[pl.BlockSpec((t_rows, LANES), lambda s, i: (chunk_of(i), s)),
                   pl.BlockSpec((None, t_rows, sw2), lambda s, i: (s, chunk_of(i), 0))],
        out_shape=[jax.ShapeDtypeStruct((rows, width), F32), jax.ShapeDtypeStruct((ns, rows, sw2), BF16)],
        scratch_shapes=[pltpu.VMEM((t_rows, sw2), F32), pltpu.VMEM((SUBLANES, sw2), F32)], comm=comm)
    return y, h, cres


def _ssm_bwd(name, dr, dy, u_src, u_shard, states, caug, tab, bdt, descending, chunk_of, t_rows, rows, comm=None):
    _, ns, kdim, sw2 = caug.shape
    sw = sw2 // 2
    width = ns * LANES
    nchunks = rows // t_rows

    def body(dy_ref, u_ref, h_ref, w_ref, tab_ref, bdt_ref, du_ref, dbd_ref, dcd_ref, dlam_ref,
             s_ref, carry_ref, gsave_ref):
        first = pl.program_id(1) == 0

        @pl.when(first)
        def _():
            carry_ref[...] = jnp.zeros_like(carry_ref)
            gsave_ref[...] = jnp.zeros_like(gsave_ref)

        dyv = dy_ref[...]
        _scan_chunk(dyv, w_ref, tab_ref, s_ref, carry_ref, descending, t_rows, sw)
        g = s_ref[...]
        gb = g.astype(BF16)
        du_ref[...] = jnp.dot(gb, bdt_ref[...], preferred_element_type=F32)
        dbd = lax.dot_general(u_ref[...].astype(BF16), gb, TN, preferred_element_type=F32)
        hb = h_ref[...]
        dcd = lax.dot_general(hb, dyv.astype(BF16), TN, preferred_element_type=F32)
        hf = hb.astype(F32)
        rowid = lax.broadcasted_iota(jnp.int32, hf.shape, 0)
        if descending:
            hp = jnp.where(rowid == 0, 0.0, pltpu.roll(hf, 1, 0))
            h_edge, g_edge = hf[t_rows - 1:t_rows, :], g[0:1, :]
        else:
            hp = jnp.where(rowid == t_rows - 1, 0.0, pltpu.roll(hf, t_rows - 1, 0))
            h_edge, g_edge = hf[0:1, :], g[t_rows - 1:t_rows, :]
        g_re, g_im, hp_re, hp_im = g[:, :sw], g[:, sw:], hp[:, :sw], hp[:, sw:]
        gs = gsave_ref[0:1, :]
        gs_re, gs_im, he_re, he_im = gs[:, :sw], gs[:, sw:], h_edge[:, :sw], h_edge[:, sw:]
        dl_re = _colsum(g_re * hp_re + g_im * hp_im) + gs_re * he_re + gs_im * he_im
        dl_im = _colsum(g_im * hp_re - g_re * hp_im) + gs_im * he_re - gs_re * he_im
        gsave_ref[0:1, :] = g_edge

        @pl.when(first)
        def _():
            dbd_ref[...] = dbd
            dcd_ref[...] = dcd
            dlam_ref[:, :sw] = dl_re
            dlam_ref[:, sw:] = dl_im

        @pl.when(jnp.logical_not(first))
        def _():
            dbd_ref[...] += dbd
            dcd_ref[...] += dcd
            dlam_ref[:, :sw] += dl_re
            dlam_ref[:, sw:] += dl_im

    (du, dbd, dcd, dlam), cres = _host_call(
        body, name=name, grid=(ns, nchunks), operands=[dy, u_src, states, caug, tab, bdt],
        in_specs=[pl.BlockSpec((t_rows, LANES), lambda s, i: (chunk_of(i), s)),
                  pl.BlockSpec((None, t_rows, LANES), lambda s, i: (u_shard, chunk_of(i), s)),
                  pl.BlockSpec((None, t_rows, sw2), lambda s, i: (s, chunk_of(i), 0)),
                  pl.BlockSpec((None, None, kdim, sw2), lambda s, i: (dr, s, 0, 0)),
                  pl.BlockSpec((None, None, SCAN_TAPS, sw2), lambda s, i: (dr, s, 0, 0)),
                  pl.BlockSpec((None, None, sw2, LANES), lambda s, i: (dr, s, 0, 0))],
        out_specs=[pl.BlockSpec((t_rows, LANES), lambda s, i: (chunk_of(i), s)),
                   pl.BlockSpec((None, LANES, sw2), lambda s, i: (s, 0, 0)),
                   pl.BlockSpec((None, sw2, LANES), lambda s, i: (s, 0, 0)),
                   pl.BlockSpec((None, 1, sw2), lambda s, i: (s, 0, 0))],
        out_shape=[jax.ShapeDtypeStruct((rows, width), F32), jax.ShapeDtypeStruct((ns, LANES, sw2), F32),
                   jax.ShapeDtypeStruct((ns, sw2, LANES), F32), jax.ShapeDtypeStruct((ns, 1, sw2), F32)],
        scratch_shapes=[pltpu.VMEM((t_rows, sw2), F32), pltpu.VMEM((SUBLANES, sw2), F32),
                        pltpu.VMEM((SUBLANES, sw2), F32)], comm=comm)
    return du, dbd, dcd, dlam, cres


def _mod_fwd(cs, w_mod, b_cols):
    d, width = w_mod.shape
    tn = _tile(width, 768, LANES)

    def epilogue(accs, ins, outs, pids):
        outs[0][...] = accs[0] + ins[2][...]

    return _matmul(
        "mod_fwd", (width // tn,), [cs, w_mod, b_cols],
        [pl.BlockSpec((16, d), lambda n: (0, 0)), pl.BlockSpec((d, tn), lambda n: (0, n)),
         pl.BlockSpec((1, tn), lambda n: (0, n))],
        [(0, 1, 0, NN)], [jax.ShapeDtypeStruct((16, width), F32)], [pl.BlockSpec((16, tn), lambda n: (0, n))],
        epilogue, prologue={0: lambda v: v * _sigmoid(v)})[0][0]


def _mod_bwd_adam(cs, dmod_cols, w, m, v, comm=None):
    d, width = w.shape
    tn = _tile(width, LANES, LANES)
    col = pl.BlockSpec((d, tn), lambda n: (0, n))

    def body(cs_ref, dm_ref, w_ref, m_ref, v_ref, g_ref, dl_ref, nm_ref, nv_ref, ds_ref):
        n = pl.program_id(0)
        lat = dm_ref[pl.ds(0, N_DEV, stride=SUBLANES), :]
        ctx = jnp.sum(dm_ref[pl.ds(1, N_DEV, stride=SUBLANES), :], axis=0, keepdims=True)
        row = lax.broadcasted_iota(jnp.int32, lat.shape, 0)
        dm = jnp.concatenate([lat, jnp.where(row == 0, ctx, 0.0)], axis=0).astype(BF16)
        c = cs_ref[...]
        sc = (c * _sigmoid(c)).astype(BF16)
        wv = w_ref[...]
        g = lax.dot_general(sc, dm, TN, preferred_element_type=F32)
        delta, m2, v2 = _adamw(wv, g, m_ref[...], v_ref[...])
        g_ref[...] = g
        dl_ref[...] = delta
        nm_ref[...] = m2
        nv_ref[...] = v2
        part = lax.dot_general(dm, wv.astype(BF16), NT, preferred_element_type=F32)

        @pl.when(n == 0)
        def _():
            ds_ref[...] = part

        @pl.when(n > 0)
        def _():
            ds_ref[...] += part

    shard = jax.ShapeDtypeStruct((d, width), F32)
    return _host_call(
        body, name="mod_bwd_adam", grid=(width // tn,), operands=[cs, dmod_cols, w, m, v],
        in_specs=[pl.BlockSpec((16, d), lambda n: (0, 0)), pl.BlockSpec((N_DEV * SUBLANES, tn), lambda n: (0, n)),
                  col, col, col],
        out_specs=[col, col, col, col, pl.BlockSpec((16, d), lambda n: (0, 0))],
        out_shape=[shard, shard, shard, shard, jax.ShapeDtypeStruct((16, d), F32)], comm=comm)


def _pair_sum(name, grads, got, core):
    _, rows, cols = grads.shape
    tr = _tile(rows, max(PACKED_SUBLANES, ADAM_BLOCK_BYTES // (cols * 6 * N_CHIPS)), PACKED_SUBLANES)
    blk = pl.BlockSpec((N_CHIPS, tr, cols), lambda i, cc: (0, i, 0))

    def body(core_ref, a_ref, b_ref, o_ref):
        o_ref[...] = (a_ref[...].astype(F32) + b_ref[...].astype(F32)).astype(BF16)

    grid_spec = pltpu.PrefetchScalarGridSpec(
        num_scalar_prefetch=1, grid=(rows // tr,),
        in_specs=[pl.BlockSpec((N_CHIPS, None, tr, cols), lambda i, cc: (0, cc[0], i, 0)), blk], out_specs=blk)
    return pl.pallas_call(
        body, name=name, grid_spec=grid_spec, out_shape=jax.ShapeDtypeStruct((N_CHIPS, rows, cols), BF16),
        compiler_params=_params(1))(core, grads.reshape(N_CHIPS, 2, rows, cols), got)


def _owner_adam(name, items, chip, comm=None):
    plan, start = [], 0
    per_element = 2 * (2 * N_CHIPS + 7 * 4)
    block_elements = ADAM_GROUP_VMEM // (per_element * len(items))
    for _, _, w, _, _ in items:
        rows, cols = w.shape
        tr = _tile(rows, max(PACKED_SUBLANES, block_elements // cols), PACKED_SUBLANES)
        plan.append((start, rows // tr, tr, cols))
        start += rows // tr
    operands, in_specs, out_specs, out_shape = [], [], [], []
    for (first, nt, tr, cols), (p, l, w, m, v) in zip(plan, items):
        def tile(s, first=first, nt=nt):
            return jnp.clip(s - first, 0, nt - 1)
        blk = pl.BlockSpec((tr, cols), lambda s, ch, tile=tile: (tile(s), 0))
        operands += [p, l, w, m, v]
        in_specs += [pl.BlockSpec((None, tr, cols), lambda s, ch, tile=tile: (ch[0], tile(s), 0)),
                     pl.BlockSpec((N_CHIPS - 1, tr, cols), lambda s, ch, tile=tile: (0, tile(s), 0)), blk, blk, blk]
        out_specs += [blk] * 4
        out_shape += [jax.ShapeDtypeStruct(w.shape, F32)] * 4
    n = len(items)

    def body(chip_ref, *refs):
        s = pl.program_id(0)
        for k, (first, nt, _, _) in enumerate(plan):
            p_ref, l_ref, w_ref, m_ref, v_ref = refs[5 * k:5 * k + 5]
            g_ref, dl_ref, nm_ref, nv_ref = refs[5 * n + 4 * k:5 * n + 4 * k + 4]

            @pl.when(jnp.logical_and(s >= first, s < first + nt))
            def _(p_ref=p_ref, l_ref=l_ref, w_ref=w_ref, m_ref=m_ref, v_ref=v_ref,
                  g_ref=g_ref, dl_ref=dl_ref, nm_ref=nm_ref, nv_ref=nv_ref):
                g = p_ref[...].astype(F32)
                for r in range(N_CHIPS - 1):
                    g = g + l_ref[r].astype(F32)
                delta, m2, v2 = _adamw(w_ref[...], g, m_ref[...], v_ref[...])
                g_ref[...] = g
                dl_ref[...] = delta
                nm_ref[...] = m2
                nv_ref[...] = v2

    res, cres = _host_call(body, name=name, grid=(start,), operands=operands, in_specs=in_specs,
                           out_shape=out_shape, out_specs=out_specs, comm=comm, prefetch=[chip])
    return [res[4 * k:4 * k + 4] for k in range(n)], cres


def _sum_adam(name, parts, w, m, v):
    rows, cols = w.shape
    n_parts = parts.shape[0]
    align = PACKED_SUBLANES if parts.dtype == BF16 else SUBLANES
    tr = _tile(rows, max(align, ADAM_BLOCK_BYTES // (cols * 44)), align)
    blk = pl.BlockSpec((tr, cols), lambda i: (i, 0))

    def body(p_ref, w_ref, m_ref, v_ref, g_ref, dl_ref, nm_ref, nv_ref):
        g = p_ref[0].astype(F32)
        for s in range(1, n_parts):
            g = g + p_ref[s].astype(F32)
        delta, m2, v2 = _adamw(w_ref[...], g, m_ref[...], v_ref[...])
        g_ref[...] = g
        dl_ref[...] = delta
        nm_ref[...] = m2
        nv_ref[...] = v2

    out = jax.ShapeDtypeStruct((rows, cols), F32)
    return pl.pallas_call(
        body, name=name, grid=(rows // tr,),
        in_specs=[pl.BlockSpec((n_parts, tr, cols), lambda i: (0, i, 0)), blk, blk, blk],
        out_specs=[blk, blk, blk, blk], out_shape=[out, out, out, out], compiler_params=_params(1),
    )(parts, w, m, v)


def _bias_adam(dmod_all, w, m, v):
    width = w.shape[-1]
    tn = _tile(width, 2048, LANES)
    blk = pl.BlockSpec((1, tn), lambda n: (0, n))

    def body(p_ref, w_ref, m_ref, v_ref, g_ref, dl_ref, nm_ref, nv_ref):
        g = jnp.sum(p_ref[...], axis=0, keepdims=True)
        delta, m2, v2 = _adamw(w_ref[...], g, m_ref[...], v_ref[...])
        g_ref[...] = g
        dl_ref[...] = delta
        nm_ref[...] = m2
        nv_ref[...] = v2

    out = jax.ShapeDtypeStruct((1, width), F32)
    return pl.pallas_call(
        body, name="bias_adam", grid=(width // tn,),
        in_specs=[pl.BlockSpec((dmod_all.shape[0], tn), lambda n: (0, n)), blk, blk, blk],
        out_specs=[blk, blk, blk, blk], out_shape=[out, out, out, out], compiler_params=_params(1),
    )(dmod_all, w, m, v)


def _pack(arrays, total_rows):
    flat = []
    for a in arrays:
        a = a.reshape(-1).astype(F32)
        flat.append(jnp.pad(a, (0, (-a.shape[0]) % LANES)))
    flat = jnp.concatenate(flat).reshape(-1, LANES)
    return jnp.pad(flat, ((0, total_rows - flat.shape[0]), (0, 0)))


def _unpack(packed, shapes):
    out, row = [], 0
    for shp in shapes:
        size = math.prod(shp)
        nrows = -(-size // LANES)
        out.append(packed[row:row + nrows].reshape(-1)[:size].reshape(shp))
        row += nrows
    return out


def kernel(x, c, ctx, c_ctx, w_mod, b_mod, norm_g, w_ffn1_gate, w_ffn1_up, w_ffn1_down, w_in, q_norm_g, k_norm_g, ssm_a_re, ssm_a_im, ssm_log_dt, ssm_b_re, ssm_b_im, ssm_c_re, ssm_c_im, ssm_d, w_glu, b_glu, w_br_attn, w_br_ssm, w_out, w_ffn2_gate, w_ffn2_up, w_ffn2_down, loss_target, m_c_ctx, m_w_mod, m_b_mod, m_norm_g, m_w_ffn1_gate, m_w_ffn1_up, m_w_ffn1_down, m_w_in, m_q_norm_g, m_k_norm_g, m_ssm_a_re, m_ssm_a_im, m_ssm_log_dt, m_ssm_b_re, m_ssm_b_im, m_ssm_c_re, m_ssm_c_im, m_ssm_d, m_w_glu, m_b_glu, m_w_br_attn, m_w_br_ssm, m_w_out, m_w_ffn2_gate, m_w_ffn2_up, m_w_ffn2_down, v_c_ctx, v_w_mod, v_b_mod, v_norm_g, v_w_ffn1_gate, v_w_ffn1_up, v_w_ffn1_down, v_w_in, v_q_norm_g, v_k_norm_g, v_ssm_a_re, v_ssm_a_im, v_ssm_log_dt, v_ssm_b_re, v_ssm_b_im, v_ssm_c_re, v_ssm_c_im, v_ssm_d, v_w_glu, v_b_glu, v_w_br_attn, v_w_br_ssm, v_w_out, v_w_ffn2_gate, v_w_ffn2_up, v_w_ffn2_down):
    _, L, D = x.shape
    Lc = ctx.shape[1]
    R = L + Lc
    MODW = w_mod.shape[-1]
    INS = w_in.shape[-1]
    KVW = INS // 2
    NQ = D // LANES
    NKV = KVW // LANES
    QPK = NQ // NKV
    HBQ = INS // LANES
    G, P, E = ssm_b_re.shape[2:]
    W = G * E
    SW = SLAB_GROUPS * P
    assert E * SLAB_GROUPS == LANES and W == INS and NQ * LANES == D and Lc <= L
    me = 4 * lax.axis_index("x") + 2 * lax.axis_index("y") + lax.axis_index("c")

    x2, ctx2, tgt = x[0], ctx[0], loss_target[0]
    xc0 = jnp.concatenate([x2, ctx2], axis=0)

    def bf(w):
        return w[0].astype(BF16)

    def held_t(w):
        return jnp.swapaxes(w[0], 0, 1)

    def bft(w):
        return held_t(w).astype(BF16)

    def widen(a):
        return jnp.pad(a[0], ((0, 0), (0, D - a.shape[-1])))

    def at_row(a, r, total):
        return jnp.pad(a, ((r, total - r - a.shape[0]), (0, 0)))

    pack_in = (at_row(c, 0, 16) + at_row(widen(norm_g), 1, 16) + at_row(widen(m_norm_g), 4, 16)
               + at_row(widen(v_norm_g), 7, 16))
    (g_in,) = _exchange_only("ag_inputs", _Gather([pack_in]))
    c_all = g_in[:, 0, :]
    dn = D // N_DEV

    def full_norm(k):
        return jnp.transpose(g_in[:, k:k + 3, :dn], (1, 0, 2)).reshape(3, D)

    ng_full, m_ng_full, v_ng_full = full_norm(1), full_norm(4), full_norm(7)
    cs = at_row(c_all, 0, 16) + at_row(c_ctx[None, :], 8, 16)

    b_cols = lax.dynamic_slice_in_dim(b_mod, me * MODW, MODW, axis=1)
    mod_blk = _mod_fwd(cs, w_mod[0], b_cols)
    (mod_g,) = _exchange_only("ag_mod", _Gather([mod_blk]))
    mod_lat = lax.dynamic_index_in_dim(mod_g, me, axis=1, keepdims=False).reshape(9, D)
    mod_ctx = mod_g[:, 8, :].reshape(9, D)[:5]
    tab2 = jnp.concatenate([mod_lat, mod_ctx, ng_full, jnp.zeros((7, D), F32)], axis=0)
    tab3 = tab2[:, None, :]
    SH1, SC1, G1, SH2, SC2, G2, SH3, SC3, G3, MC0, MC1, MC2, MC3, MC4, GAM1, GAM2, GAM3 = range(17)

    wg1, wu1 = _exchange_only("ag_ffn1_gate_up", _Gather([bft(w_ffn1_gate), bft(w_ffn1_up)]))
    h1 = _norm_mod_fwd("nm1_fwd", xc0, tab3, GAM1, (SH1, MC0), (SC1, MC1), L, Lc)
    a1, b1, s1, (wd1,) = _ffn_up("ffn1", h1, wg1, wu1, comm=_Gather([bf(w_ffn1_down)]))
    f1, xc1, (win,) = _ffn_down("ffn1", s1, wd1, xc0, tab2, (G1, MC2), L, comm=_Gather([bf(w_in)]))

    h2 = _norm_mod_fwd("nm2_fwd", xc1, tab3, GAM2, (SH2, MC3), (SC2, MC4), L, Lc)
    tm = _tile(R, MM_TILE, LANES)
    tml = _tile(L, MM_TILE, LANES)

    (p01,), _ = _matmul(
        "in_proj_kvu", (2, R // tm), [h2, win],
        [pl.BlockSpec((tm, D), lambda j, i: (i, 0)), pl.BlockSpec((None, D, INS), lambda j, i: (j, 0, 0))],
        [(0, 1, 0, NN)], [jax.ShapeDtypeStruct((2, R, INS), F32)],
        [pl.BlockSpec((None, tm, INS), lambda j, i: (j, i, 0))], _store_all)
    (p27,), (wglu, wbra) = _matmul(
        "in_proj_qg", (6, L // tml), [h2, win],
        [pl.BlockSpec((tml, D), lambda j, i: (i, 0)), pl.BlockSpec((None, D, INS), lambda j, i: (j + 2, 0, 0))],
        [(0, 1, 0, NN)], [jax.ShapeDtypeStruct((6, L, INS), F32)],
        [pl.BlockSpec((None, tml, INS), lambda j, i: (j, i, 0))], _store_all,
        comm=_Gather([bf(w_glu), bf(w_br_attn)]))
    wglu2 = wglu.reshape(W, W)
    wbra2 = wbra.reshape(D, D)

    cos_all, sin_all = _rope_tables(L, Lc)
    cos_l, sin_l = cos_all[:L], sin_all[:L]

    q_rot = _qk_prep("q_prep", p27, 0, HBQ, NQ, L, q_norm_g, cos_l, sin_l)
    k_rot = _qk_prep("k_prep", p01, 0, NKV, NKV, R, k_norm_g, cos_all, sin_all)
    v_hd = _heads_cast("v_heads", p01, 1, NKV, NKV, R)
    attn, (wbrs, wout, wg2) = _attn_fwd(
        q_rot, k_rot, v_hd, QPK, comm=_Gather([bf(w_br_ssm), bf(w_out), bft(w_ffn2_gate)]))
    wout2 = wout.reshape(D, D)

    t_rows = _tile(math.gcd(L, Lc), ROW_TILE, SUBLANES)
    nl, ncx = L // t_rows, Lc // t_rows
    nch = nl + ncx
    ns = G // SLAB_GROUPS
    ssm_prim = (ssm_a_re[0], ssm_a_im[0], ssm_log_dt[0], ssm_b_re[0], ssm_b_im[0])
    _, _, bt_re, bt_im = _ssm_discretize(*ssm_prim)
    pw_re, pw_im = _lambda_powers(ssm_a_re[0], ssm_a_im[0], ssm_log_dt[0], ns)
    bd_re = _block_diag(jnp.swapaxes(bt_re, 2, 3))
    bd_im = _block_diag(jnp.swapaxes(bt_im, 2, 3))
    ct_re = _block_diag(ssm_c_re[0])
    ct_im = _block_diag(-ssm_c_im[0])
    fwd_desc = (False, True)
    adj_desc = (True, False)
    s_waug = _tap_weights(bd_re, bd_im, pw_re, pw_im).astype(BF16)
    s_tab = _carry_tables(pw_re, pw_im, fwd_desc)
    s_cd = jnp.concatenate([jnp.swapaxes(ct_re, 2, 3), jnp.swapaxes(ct_im, 2, 3)], axis=2).astype(BF16)
    s_caug = _tap_weights(ct_re, ct_im, pw_re, -pw_im).astype(BF16)
    s_tabc = _carry_tables(pw_re, -pw_im, adj_desc)
    s_bdt = jnp.concatenate([jnp.swapaxes(bd_re, 2, 3), jnp.swapaxes(bd_im, 2, 3)], axis=2).astype(BF16)
    order = [lambda i: (i + nl) % nch, lambda i: nch - 1 - i]
    order_adj = [lambda i: (nch - 1 - i + nl) % nch, lambda i: i]
    y0, st0, (wu2,) = _ssm_fwd("ssm_fwd0", 0, p01, 1, s_waug, s_tab, s_cd, fwd_desc[0], order[0], t_rows, R,
                               comm=_Gather([bft(w_ffn2_up)]))
    y1, st1, (wd2,) = _ssm_fwd("ssm_fwd1", 1, p01, 1, s_waug, s_tab, s_cd, fwd_desc[1], order[1], t_rows, R,
                               comm=_Gather([bf(w_ffn2_down)]))
    states = [st0, st1]

    tr = _row_tile(L, 0)
    rowW = pl.BlockSpec((tr, W), lambda i: (i, 0))
    vecW = pl.BlockSpec((1, W), lambda i: (0, 0))
    u_lat = pl.BlockSpec((None, tr, W), lambda i: (1, i, 0))

    def ssm_post(i, u, ya, yb, dvec):
        sv = dvec * u + ya + yb
        return [sv, _gelu(sv)], []

    (ssm_out, yg), _, _ = _rowwise(
        "ssm_post", L // tr, [p01, y0, y1, ssm_d], [u_lat, rowW, rowW, vecW],
        [jax.ShapeDtypeStruct((L, W), F32), jax.ShapeDtypeStruct((L, W), BF16)], [rowW, rowW], [], ssm_post)

    tnw = _tile(W, MM_TILE, LANES)

    def glu_epilogue(accs, ins, outs, pids):
        z = accs[0] + ins[3][...]
        outs[0][...] = z
        outs[1][...] = (_gelu(ins[2][...]) * _sigmoid(z)).astype(BF16)

    (z_glu, y2), _ = _matmul(
        "glu", (L // tml, W // tnw), [yg, wglu2, ssm_out, b_glu],
        [pl.BlockSpec((tml, W), lambda i, n: (i, 0)), pl.BlockSpec((W, tnw), lambda i, n: (0, n)),
         pl.BlockSpec((tml, tnw), lambda i, n: (i, n)), pl.BlockSpec((1, tnw), lambda i, n: (0, n))],
        [(0, 1, 0, NN)], [jax.ShapeDtypeStruct((L, W), F32), jax.ShapeDtypeStruct((L, W), BF16)],
        [pl.BlockSpec((tml, tnw), lambda i, n: (i, n))] * 2, glu_epilogue)

    tnd = _tile(D, MM_TILE, LANES)
    out_ld = pl.BlockSpec((tml, tnd), lambda i, n: (i, n))
    (br_a,), _ = _matmul(
        "br_attn", (L // tml, D // tnd), [attn, wbra2],
        [pl.BlockSpec((tml, D), lambda i, n: (i, 0)), pl.BlockSpec((D, tnd), lambda i, n: (0, n))],
        [(0, 1, 0, NN)], [jax.ShapeDtypeStruct((L, D), F32)], [out_ld], _store_all)

    cb = wbrs.shape[-1]
    gpb = INS // cb

    def gate_spec(first_shard):
        return pl.BlockSpec((None, tml, cb), lambda i, j: (first_shard + j // gpb, i, j % gpb))

    def merge_epilogue(accs, ins, outs, pids):
        br = accs[0]
        outs[0][...] = br
        outs[1][...] = (_sigmoid(ins[2][...]) * ins[4][...] + _sigmoid(ins[3][...]) * br).astype(BF16)

    col_blk = pl.BlockSpec((tml, cb), lambda i, j: (i, j))
    (br_s, merged), _ = _matmul(
        "br_ssm_merge", (L // tml, N_DEV), [y2, wbrs, p27, p27, br_a],
        [pl.BlockSpec((tml, W), lambda i, j: (i, 0)), pl.BlockSpec((None, W, cb), lambda i, j: (j, 0, 0)),
         gate_spec(2), gate_spec(4), col_blk],
        [(0, 1, 0, NN)], [jax.ShapeDtypeStruct((L, D), F32), jax.ShapeDtypeStruct((L, D), BF16)],
        [col_blk, col_blk], merge_epilogue)

    def out_epilogue(accs, ins, outs, pids):
        outs[0][...] = accs[0]
        outs[1][...] = ins[2][...] + ins[3][G2:G2 + 1, :] * accs[0]

    (mix, x2_), _ = _matmul(
        "out_proj", (L // tml, D // tnd), [merged, wout2, xc1, tab2],
        [pl.BlockSpec((tml, D), lambda i, n: (i, 0)), pl.BlockSpec((D, tnd), lambda i, n: (0, n)), out_ld,
         pl.BlockSpec((tab2.shape[0], tnd), lambda i, n: (0, n))],
        [(0, 1, 0, NN)], [jax.ShapeDtypeStruct((L, D), F32)] * 2, [out_ld, out_ld], out_epilogue)

    h3 = _norm_mod_fwd("nm3_fwd", x2_, tab3, GAM3, (SH3, SH3), (SC3, SC3), L, 0)
    a3, b3, s3, _ = _ffn_up("ffn2", h3, wg2, wu2)
    f3, x3, _ = _ffn_down("ffn2", s3, wd2, x2_, tab2, (G3, G3), L)

    trd = _row_tile(L, 0)
    rowD = pl.BlockSpec((trd, D), lambda i: (i, 0))

    def loss_fn(i, yv, t):
        err = yv - t
        return [err * (1.0 / D)], [_colsum(err * err)]

    (dx3,), (sq,), _ = _rowwise("loss", L // trd, [x3, tgt], [rowD, rowD],
                                [jax.ShapeDtypeStruct((L, D), F32)], [rowD], [D], loss_fn)
    loss = lax.psum(0.5 * jnp.sum(sq) / D, ("x", "y", "c"))

    core = lax.axis_index("c").astype(jnp.int32).reshape(1)
    chip = (2 * lax.axis_index("x") + lax.axis_index("y")).astype(jnp.int32).reshape(1)

    def pair_sums(tag, grads, halves):
        return [_pair_sum("pair_%s%d" % (tag, k), g_, h_, core) for k, (g_, h_) in enumerate(zip(grads, halves))]

    df3, (dg3, _) = _gate_bwd("gate3_bwd", dx3, f3, tab3, (G3, G3), 0.5, L, 0)
    dwd2, _ = _ffn_dwd("ffn2b", s3, df3)
    da3, db3, half_wd2 = _ffn_ds("ffn2b", df3, wd2, a3, b3, comm=_SiblingSwap([dwd2]))
    (p_wd2,) = pair_sums("wd2", [dwd2], half_wd2)
    dwg2, dwu2, (l_wd2,) = _ffn_dwgu("ffn2b", h3, da3, db3, comm=_ChipExchange([p_wd2]))
    dh3, half_wgu2 = _ffn_dh("ffn2b", da3, db3, wg2, wu2, comm=_SiblingSwap([dwg2, dwu2]))
    p_wg2, p_wu2 = pair_sums("wgu2", [dwg2, dwu2], half_wgu2)
    dx2, (dsh3, dsc3, _, _, dgam3) = _norm_mod_bwd("nm3_bwd", x2_, dh3, tab3, GAM3, (SC3, SC3), L, 0, dres=dx3)

    dmix, (dg2, _) = _gate_bwd("gate2_bwd", dx2, mix, tab3, (G2, G2), 1.0, L, 0)

    def dmerged_epilogue(accs, ins, outs, pids):
        dm = accs[0]
        ga, gs = _sigmoid(ins[2][...]), _sigmoid(ins[3][...])
        outs[0][...] = (ga * dm).astype(BF16)
        outs[1][...] = (gs * dm).astype(BF16)
        outs[2][...] = (dm * ins[4][...] * ga * (1.0 - ga)).astype(BF16)
        outs[3][...] = (dm * ins[5][...] * gs * (1.0 - gs)).astype(BF16)

    dgate_spec = pl.BlockSpec((None, tml, cb), lambda i, j: (j // gpb, i, j % gpb))
    (d_br_a, d_br_s, dg_a, dg_s), _ = _matmul(
        "dmerged", (L // tml, N_DEV), [dmix, wout2, p27, p27, br_a, br_s],
        [pl.BlockSpec((tml, D), lambda i, j: (i, 0)), pl.BlockSpec((cb, D), lambda i, j: (j, 0)),
         gate_spec(2), gate_spec(4), col_blk, col_blk],
        [(0, 1, 0, NT)],
        [jax.ShapeDtypeStruct((L, D), BF16)] * 2 + [jax.ShapeDtypeStruct((2, L, INS), BF16)] * 2,
        [col_blk, col_blk, dgate_spec, dgate_spec], dmerged_epilogue)

    def wgrad(name, a_mat, b_mat, tmo, tno):
        ka, ma = a_mat.shape
        _, nb_ = b_mat.shape
        return _matmul(
            name, (ma // tmo, nb_ // tno), [a_mat, b_mat],
            [pl.BlockSpec((ka, tmo), lambda m, n: (0, m)), pl.BlockSpec((ka, tno), lambda m, n: (0, n))],
            [(0, 1, 0, TN)], [jax.ShapeDtypeStruct((ma, nb_), BF16)],
            [pl.BlockSpec((tmo, tno), lambda m, n: (m, n))], _store_all)[0][0]

    dwout = wgrad("dw_out", merged, dmix, tnd, tnd)
    dwbra = wgrad("dw_br_attn", attn, d_br_a, tnd, tnd)
    (d_attn,), _ = _matmul(
        "d_attn", (L // tml, D // tnd), [d_br_a, wbra2],
        [pl.BlockSpec((tml, D), lambda i, n: (i, 0)), pl.BlockSpec((tnd, D), lambda i, n: (n, 0))],
        [(0, 1, 0, NT)], [jax.ShapeDtypeStruct((L, D), BF16)], [out_ld], _store_all)

    (dwbrs,), _ = _matmul(
        "dw_br_ssm", (N_DEV,), [y2, d_br_s],
        [pl.BlockSpec((L, W), lambda j: (0, 0)), pl.BlockSpec((L, cb), lambda j: (0, j))],
        [(0, 1, 0, TN)], [jax.ShapeDtypeStruct((N_DEV, W, cb), BF16)],
        [pl.BlockSpec((None, W, cb), lambda j: (j, 0, 0))], _store_all)

    def dy2_epilogue(accs, ins, outs, pids):
        dy2 = accs[0]
        sg = _sigmoid(ins[2][...])
        outs[0][...] = dy2 * sg
        outs[1][...] = (dy2 * _gelu(ins[3][...]) * sg * (1.0 - sg)).astype(BF16)

    wn_blk = pl.BlockSpec((tml, tnw), lambda i, n, k: (i, n))
    (dyg1, dz), _ = _matmul(
        "d_y2", (L // tml, W // tnw, N_DEV), [d_br_s, wbrs, z_glu, ssm_out],
        [pl.BlockSpec((tml, cb), lambda i, n, k: (i, k)), pl.BlockSpec((None, tnw, cb), lambda i, n, k: (k, n, 0)),
         wn_blk, wn_blk],
        [(0, 1, 0, NT)], [jax.ShapeDtypeStruct((L, W), F32), jax.ShapeDtypeStruct((L, W), BF16)],
        [wn_blk, wn_blk], dy2_epilogue, acc_shapes=[(tml, tnw)], nk=N_DEV)

    dwglu = wgrad("dw_glu", yg, dz, tnw, tnw)
    mix_grads = [dwout.reshape(N_DEV, D // N_DEV, D), dwbra.reshape(N_DEV, D // N_DEV, D), dwbrs,
                 dwglu.reshape(N_DEV, W // N_DEV, W)]

    def dssm_epilogue(accs, ins, outs, pids):
        outs[0][...] = (accs[0] + ins[2][...]) * _gelu_grad(ins[3][...])

    wn2 = pl.BlockSpec((tml, tnw), lambda i, n: (i, n))
    (dssm,), _ = _matmul(
        "d_ssm", (L // tml, W // tnw), [dz, wglu2, dyg1, ssm_out],
        [pl.BlockSpec((tml, W), lambda i, n: (i, 0)), pl.BlockSpec((tnw, W), lambda i, n: (n, 0)), wn2, wn2],
        [(0, 1, 0, NT)], [jax.ShapeDtypeStruct((L, W), F32)], [wn2], dssm_epilogue)

    dssm_all = jnp.concatenate([dssm, jnp.zeros((Lc, W), F32)], axis=0)
    du0, dbd0, dcd0, dlam0, (l_wg2, *half_mix) = _ssm_bwd(
        "ssm_bwd0", 0, dssm_all, p01, 1, states[0], s_caug, s_tabc, s_bdt, adj_desc[0], order_adj[0], t_rows, R,
        comm=_Both([_ChipExchange([p_wg2]), _SiblingSwap(mix_grads)]))
    p_wout, p_wbra, p_wbrs, p_wglu = pair_sums("mix", mix_grads, half_mix)
    du1, dbd1, dcd1, dlam1, (l_wu2,) = _ssm_bwd(
        "ssm_bwd1", 1, dssm_all, p01, 1, states[1], s_caug, s_tabc, s_bdt, adj_desc[1], order_adj[1], t_rows, R,
        comm=_ChipExchange([p_wu2]))

    trr = _row_tile(L, Lc)
    nlt = L // trr
    rowR = pl.BlockSpec((trr, W), lambda i: (i, 0))

    def du_fn(i, dua, dub, dsv, dvec, u):
        lat = (i < nlt).astype(F32)
        return [dua + dub + lat * (dvec * dsv)], [lat * _colsum(dsv * u)]

    (du_all,), (d_ssm_d,), _ = _rowwise(
        "du_combine", R // trr, [du0, du1, dssm_all, ssm_d, p01],
        [rowR, rowR, rowR, pl.BlockSpec((1, W), lambda i: (0, 0)), pl.BlockSpec((None, trr, W), lambda i: (1, i, 0))],
        [jax.ShapeDtypeStruct((R, W), BF16)], [rowR], [W], du_fn)

    def dz_sum(i, dzv):
        return [], [_colsum(dzv.astype(F32))]

    _, (d_b_glu,), _ = _rowwise("db_glu", L // tr, [dz], [rowW], [], [], [W], dz_sum)

    dq_rot, dk_rot, dv_hd, (l_wout, l_wbra, l_wbrs, l_wglu) = _attn_bwd(
        q_rot, k_rot, v_hd, d_attn, QPK, comm=_ChipExchange([p_wout, p_wbra, p_wbrs, p_wglu]))
    dq_pre, d_qg = _qk_prep_bwd("q_prep_bwd", dq_rot, p27, 0, HBQ, NQ, L, q_norm_g, cos_l, sin_l)
    dk_pre, d_kg = _qk_prep_bwd("k_prep_bwd", dk_rot, p01, 0, NKV, NKV, R, k_norm_g, cos_all, sin_all)
    dv_pre = _heads_merge("dv_merge", dv_hd)

    def lat_blocks(a):
        return jnp.pad(a, ((0, 0), (0, Lc), (0, 0)))

    dq_blocks = jnp.transpose(dq_pre.reshape(L, 2, INS), (1, 0, 2))
    dp = jnp.concatenate([
        jnp.concatenate([dk_pre, dv_pre], axis=1)[None], du_all[None],
        lat_blocks(dq_blocks), lat_blocks(dg_a), lat_blocks(dg_s)], axis=0)

    tmo = _tile(D, MM_TILE, LANES)
    (dwin,), _ = _matmul(
        "dw_in", (N_DEV, D // tmo), [h2, dp],
        [pl.BlockSpec((R, tmo), lambda j, m: (0, m)), pl.BlockSpec((None, R, INS), lambda j, m: (j, 0, 0))],
        [(0, 1, 0, TN)], [jax.ShapeDtypeStruct((N_DEV, D, INS), BF16)],
        [pl.BlockSpec((None, tmo, INS), lambda j, m: (j, m, 0))], _store_all)
    tnh = _tile(D, MM_TILE_NT, LANES)
    (dh2,), half_win = _matmul(
        "d_h2", (R // tm, D // tnh), [dp, win],
        [pl.BlockSpec((N_DEV, tm, INS), lambda i, n: (0, i, 0)),
         pl.BlockSpec((N_DEV, tnh, INS), lambda i, n: (0, n, 0))],
        [(0, 1, 0, NT, N_DEV)], [jax.ShapeDtypeStruct((R, D), F32)], [pl.BlockSpec((tm, tnh), lambda i, n: (i, n))],
        _store_all, comm=_SiblingSwap([dwin]))
    (p_win,) = pair_sums("win", [dwin], half_win)
    dxc1, (dsh2, dsc2, dmc3, dmc4, dgam2) = _norm_mod_bwd(
        "nm2_bwd", xc1, dh2, tab3, GAM2, (SC2, MC4), L, Lc, dres=dx2)

    df1, (dg1, dmc2) = _gate_bwd("gate1_bwd", dxc1, f1, tab3, (G1, MC2), 0.5, L, Lc)
    dwd1, _ = _ffn_dwd("ffn1b", s1, df1)
    da1, db1, (l_win, *half_wd1) = _ffn_ds(
        "ffn1b", df1, wd1, a1, b1, comm=_Both([_ChipExchange([p_win]), _SiblingSwap([dwd1])]))
    (p_wd1,) = pair_sums("wd1", [dwd1], half_wd1)
    dwg1, dwu1, (l_wd1,) = _ffn_dwgu("ffn1b", h1, da1, db1, comm=_ChipExchange([p_wd1]))
    dh1, half_wgu1 = _ffn_dh("ffn1b", da1, db1, wg1, wu1, comm=_SiblingSwap([dwg1, dwu1]))
    p_wg1, p_wu1 = pair_sums("wgu1", [dwg1, dwu1], half_wgu1)

    def adam_item(p, l_, w_, m_, v_):
        return (p, l_, w_[0], m_[0], v_[0])

    def adam_item_t(p, l_, w_, m_, v_):
        return (p, l_, held_t(w_), held_t(m_), held_t(v_))

    ready_a = [adam_item(p_wd1, l_wd1, w_ffn1_down, m_w_ffn1_down, v_w_ffn1_down),
               adam_item(p_win, l_win, w_in, m_w_in, v_w_in),
               adam_item(p_wglu, l_wglu, w_glu, m_w_glu, v_w_glu),
               adam_item(p_wbra, l_wbra, w_br_attn, m_w_br_attn, v_w_br_attn),
               adam_item(p_wbrs, l_wbrs, w_br_ssm, m_w_br_ssm, v_w_br_ssm)]
    ready_b = [adam_item(p_wout, l_wout, w_out, m_w_out, v_w_out),
               adam_item_t(p_wg2, l_wg2, w_ffn2_gate, m_w_ffn2_gate, v_w_ffn2_gate),
               adam_item_t(p_wu2, l_wu2, w_ffn2_up, m_w_ffn2_up, v_w_ffn2_up),
               adam_item(p_wd2, l_wd2, w_ffn2_down, m_w_ffn2_down, v_w_ffn2_down)]
    adam_a, (l_wg1,) = _owner_adam("adam_ready_a", ready_a, chip, comm=_ChipExchange([p_wg1]))
    adam_b, (l_wu1,) = _owner_adam("adam_ready_b", ready_b, chip, comm=_ChipExchange([p_wu1]))
    adam_ready = adam_a + adam_b
    dxc0, (dsh1, dsc1, dmc0, dmc1, dgam1) = _norm_mod_bwd(
        "nm1_bwd", xc0, dh1, tab3, GAM1, (SC1, MC1), L, Lc, dres=dxc1)
    grad_x = dxc0[:L][None]

    dmod_lat = jnp.concatenate([dsh1, dsc1, dg1, dsh2, dsc2, dg2, dsh3, dsc3, dg3], axis=1)
    dmod_ctx = jnp.concatenate([dmc0, dmc1, dmc2, dmc3, dmc4, jnp.zeros((1, 4 * D), F32)], axis=1)
    dmod_pack = at_row(dmod_lat, 0, SUBLANES) + at_row(dmod_ctx, 1, SUBLANES)
    (dmod_g,) = _exchange_only("ag_dmod", _Gather([dmod_pack]))
    dmod_all = dmod_g.reshape(N_DEV * SUBLANES, 9 * D)
    dmod_cols = lax.dynamic_slice_in_dim(dmod_all, me * MODW, MODW, axis=1)
    (g_wmod, dl_wmod, nm_wmod, nv_wmod, dsilu), _ = _mod_bwd_adam(
        cs, dmod_cols, w_mod[0], m_w_mod[0], v_w_mod[0])
    sg_cc = jax.nn.sigmoid(c_ctx)
    d_c_ctx = dsilu[8] * (sg_cc * (1.0 + c_ctx * (1.0 - sg_cc)))
    g_bmod, dl_bmod, nm_bmod, nv_bmod = _bias_adam(dmod_all, b_mod, m_b_mod, v_b_mod)

    dbd, dcd, dlam = jnp.stack([dbd0, dbd1]), jnp.stack([dcd0, dcd1]), jnp.stack([dlam0, dlam1])
    dbt_re = jnp.swapaxes(_block_diag_extract(dbd[..., :SW], E, P), 2, 3)
    dbt_im = jnp.swapaxes(_block_diag_extract(dbd[..., SW:], E, P), 2, 3)
    dl_re, dl_im = dlam[:, :, 0, :SW].reshape(2, G, P), dlam[:, :, 0, SW:].reshape(2, G, P)
    _, vjp = jax.vjp(_ssm_discretize, *ssm_prim)
    d_a_re, d_a_im, d_ldt, d_b_re, d_b_im = vjp((dl_re, dl_im, dbt_re, dbt_im))
    d_c_re = jnp.swapaxes(_block_diag_extract(dcd[:, :, :SW, :], P, E), 2, 3)
    d_c_im = -jnp.swapaxes(_block_diag_extract(dcd[:, :, SW:, :], P, E), 2, 3)

    dgam_all = jnp.concatenate([dgam1, dgam2, dgam3], axis=0)
    small_g = [d_c_ctx, d_qg, d_kg, d_a_re, d_a_im, d_ldt, d_b_re, d_b_im, d_c_re, d_c_im, d_ssm_d, d_b_glu,
               dgam_all]
    small_w = [c_ctx, q_norm_g, k_norm_g, ssm_a_re, ssm_a_im, ssm_log_dt, ssm_b_re, ssm_b_im, ssm_c_re, ssm_c_im,
               ssm_d, b_glu, ng_full]
    small_m = [m_c_ctx, m_q_norm_g, m_k_norm_g, m_ssm_a_re, m_ssm_a_im, m_ssm_log_dt, m_ssm_b_re, m_ssm_b_im,
               m_ssm_c_re, m_ssm_c_im, m_ssm_d, m_b_glu, m_ng_full]
    small_v = [v_c_ctx, v_q_norm_g, v_k_norm_g, v_ssm_a_re, v_ssm_a_im, v_ssm_log_dt, v_ssm_b_re, v_ssm_b_im,
               v_ssm_c_re, v_ssm_c_im, v_ssm_d, v_b_glu, v_ng_full]
    small_shapes = [a.shape for a in small_w]
    n_rows = sum(-(-math.prod(s) // LANES) for s in small_shapes)
    n_rows = -(-n_rows // 256) * 256
    (small_parts,) = _exchange_only("ag_small_grads", _Gather([_pack(small_g, n_rows)]))
    small_out = _sum_adam("small_adam", small_parts, _pack(small_w, n_rows), _pack(small_m, n_rows),
                          _pack(small_v, n_rows))
    sm_g, sm_dl, sm_m, sm_v = [_unpack(o, small_shapes) for o in small_out]

    def my_norm_cols(a):
        return lax.dynamic_slice_in_dim(a, me * dn, dn, axis=1)[None]

    for lst in (sm_g, sm_dl, sm_m, sm_v):
        lst[-1] = my_norm_cols(lst[-1])

    adam_last, _ = _owner_adam(
        "adam_last", [adam_item_t(p_wg1, l_wg1, w_ffn1_gate, m_w_ffn1_gate, v_w_ffn1_gate),
                      adam_item_t(p_wu1, l_wu1, w_ffn1_up, m_w_ffn1_up, v_w_ffn1_up)], chip)
    transposed = (0, 1, 8, 9)
    big_out = [[(jnp.swapaxes(o, 0, 1) if k in transposed else o)[None] for o in grp_]
               for k, grp_ in enumerate(adam_last + adam_ready)]

    def leaf(kind):
        sm = (sm_g, sm_dl, sm_m, sm_v)[kind]
        mod = (g_wmod, dl_wmod, nm_wmod, nv_wmod)[kind][None]
        bmod = (g_bmod, dl_bmod, nm_bmod, nv_bmod)[kind]
        big = [b[kind] for b in big_out]
        (c_ctx_, qg_, kg_, a_re_, a_im_, ldt_, b_re_, b_im_, c_re_, c_im_, sd_, bglu_, ng_) = sm
        return [c_ctx_, mod, bmod, ng_, big[0], big[1], big[2], big[3], qg_, kg_, a_re_, a_im_, ldt_, b_re_, b_im_,
                c_re_, c_im_, sd_, big[4], bglu_, big[5], big[6], big[7], big[8], big[9], big[10]]

    return tuple([loss, grad_x] + leaf(0) + leaf(1) + leaf(2) + leaf(3))
```

```python
import math

import jax
import jax.numpy as jnp
import numpy as np
from jax import lax
from jax.experimental import pallas as pl
from jax.experimental.pallas import tpu as pltpu

F32 = jnp.float32
BF16 = jnp.bfloat16

N_DEV = 8
N_CHIPS = 4
LANES = 128
SUBLANES = 8
PACKED_SUBLANES = 16
VMEM_LIMIT = 56 * 1024 * 1024
MM_TILE = 512
MM_TILE_NT = 256
ROW_TILE = 256
HEAD_ROW_TILE = 512
ATTN_BWD_HEADS = 2
ADAM_BLOCK_BYTES = 4 * 1024 * 1024
ADAM_GROUP_VMEM = 36 * 1024 * 1024

NORM_EPS = 1e-6
GRID_W = 64
ROPE_THETA = 10000.0
SCAN_TAPS = SUBLANES
SLAB_GROUPS = 8

ADAM_LR = 0.001
ADAM_B1 = 0.9
ADAM_B2 = 0.999
ADAM_EPS = 1e-08
ADAM_WD = 0.01
ADAM_STEP = 10

NN = (((1,), (0,)), ((), ()))
NT = (((1,), (1,)), ((), ()))
TN = (((0,), (0,)), ((), ()))

MESH = pl.DeviceIdType.MESH
ANY = pl.BlockSpec(memory_space=pl.ANY)


def _tile(n, cap, align):
    best = None
    for t in range(align, min(n, cap) + 1, align):
        if n % t == 0:
            best = t
    return n if best is None else best


def _params(n_grid):
    return pltpu.CompilerParams(dimension_semantics=("arbitrary",) * n_grid, vmem_limit_bytes=VMEM_LIMIT)


def _sigmoid(x):
    return 1.0 / (1.0 + jnp.exp(-x))


GELU_K = math.sqrt(2.0 / math.pi)
GELU_C = 0.044715


def _gelu(x):
    return 0.5 * x * (1.0 + jnp.tanh(GELU_K * (x + GELU_C * x * x * x)))


def _gelu_grad(x):
    t = jnp.tanh(GELU_K * (x + GELU_C * x * x * x))
    return 0.5 * (1.0 + t) + 0.5 * x * (1.0 - t * t) * GELU_K * (1.0 + 3.0 * GELU_C * x * x)


def _adamw(w, g, m, v):
    m2 = ADAM_B1 * m + (1.0 - ADAM_B1) * g
    v2 = ADAM_B2 * v + (1.0 - ADAM_B2) * (g * g)
    m_hat = m2 / (1.0 - ADAM_B1 ** ADAM_STEP)
    v_hat = v2 / (1.0 - ADAM_B2 ** ADAM_STEP)
    delta = -ADAM_LR * (m_hat / (jnp.sqrt(v_hat) + ADAM_EPS) + ADAM_WD * w)
    return delta, m2, v2


def _position():
    return lax.axis_index("x"), lax.axis_index("y"), lax.axis_index("c")


class _Gather:
    def __init__(self, arrays):
        self.arrays = list(arrays)
        n = len(self.arrays)
        self.out_shapes = [jax.ShapeDtypeStruct((N_DEV,) + a.shape, a.dtype) for a in self.arrays]
        self.scratch = [pltpu.SemaphoreType.DMA((n, 7)), pltpu.SemaphoreType.DMA((n, 7)),
                        pltpu.SemaphoreType.DMA((n,))]

    def _plan(self, ins, outs, sems):
        send, recv, local = sems
        x, y, c = _position()
        me, sibling = (x, y, c), (x, y, 1 - c)
        chips = [(1 - x, y), (x, 1 - y), (1 - x, 1 - y)]

        def slot(a, p):
            return outs[a].at[4 * p[0] + 2 * p[1] + p[2]]

        def copy(a, k, block, to, src=None):
            dst = slot(a, block)
            return pltpu.make_async_remote_copy(
                src_ref=dst if src is None else src, dst_ref=dst,
                send_sem=send.at[a, k], recv_sem=recv.at[a, k], device_id=to, device_id_type=MESH)

        mine = [pltpu.make_async_copy(ins[a], slot(a, me), local.at[a]) for a in range(len(ins))]
        return me, sibling, chips, c, copy, mine

    def start(self, ins, outs, sems):
        me, sibling, chips, c, copy, mine = self._plan(ins, outs, sems)
        for cp in mine:
            cp.start()
        for a in range(len(ins)):
            copy(a, 0, me, sibling, src=ins[a]).start()
            for j, chip in enumerate(chips):
                copy(a, 1 + j, me, (*chip, c), src=ins[a]).start()

    def finish(self, ins, outs, sems):
        me, sibling, chips, c, copy, mine = self._plan(ins, outs, sems)
        n = len(ins)
        for j, chip in enumerate(chips):
            for a in range(n):
                copy(a, 1 + j, (*chip, c), me).wait_recv()
                copy(a, 4 + j, (*chip, c), sibling).start()
        for a in range(n):
            copy(a, 0, sibling, me).wait_recv()
        for j, chip in enumerate(chips):
            for a in range(n):
                copy(a, 4 + j, (*chip, 1 - c), me).wait_recv()
        for a in range(n):
            copy(a, 0, me, sibling, src=ins[a]).wait_send()
            for j, chip in enumerate(chips):
                copy(a, 1 + j, me, (*chip, c), src=ins[a]).wait_send()
                copy(a, 4 + j, (*chip, c), sibling).wait_send()
        for cp in mine:
            cp.wait()


class _SiblingSwap:
    def __init__(self, arrays):
        self.arrays = list(arrays)
        n = len(self.arrays)
        self.out_shapes = [jax.ShapeDtypeStruct((N_CHIPS,) + a.shape[1:], a.dtype) for a in self.arrays]
        self.scratch = [pltpu.SemaphoreType.DMA((n, N_CHIPS)), pltpu.SemaphoreType.DMA((n, N_CHIPS))]

    def _plan(self, ins, outs, sems):
        send, recv = sems
        x, y, c = _position()
        return [pltpu.make_async_remote_copy(
            src_ref=ins[a].at[2 * j + 1 - c], dst_ref=outs[a].at[j],
            send_sem=send.at[a, j], recv_sem=recv.at[a, j], device_id=(x, y, 1 - c), device_id_type=MESH)
            for a in range(len(ins)) for j in range(N_CHIPS)]

    def start(self, ins, outs, sems):
        for cp in self._plan(ins, outs, sems):
            cp.start()

    def finish(self, ins, outs, sems):
        copies = self._plan(ins, outs, sems)
        for cp in copies:
            cp.wait_recv()
        for cp in copies:
            cp.wait_send()


class _ChipExchange:
    def __init__(self, arrays):
        self.arrays = list(arrays)
        n = len(self.arrays)
        self.out_shapes = [jax.ShapeDtypeStruct((N_CHIPS - 1,) + a.shape[1:], a.dtype) for a in self.arrays]
        self.scratch = [pltpu.SemaphoreType.DMA((n, N_CHIPS - 1)), pltpu.SemaphoreType.DMA((n, N_CHIPS - 1))]

    def _plan(self, ins, outs, sems):
        send, recv = sems
        x, y, c = _position()
        copies = []
        for r in range(1, N_CHIPS):
            px, py = x ^ (r >> 1), y ^ (r & 1)
            for a in range(len(ins)):
                copies.append(pltpu.make_async_remote_copy(
                    src_ref=ins[a].at[2 * px + py], dst_ref=outs[a].at[r - 1],
                    send_sem=send.at[a, r - 1], recv_sem=recv.at[a, r - 1],
                    device_id=(px, py, c), device_id_type=MESH))
        return copies

    def start(self, ins, outs, sems):
        for cp in self._plan(ins, outs, sems):
            cp.start()

    def finish(self, ins, outs, sems):
        copies = self._plan(ins, outs, sems)
        for cp in copies:
            cp.wait_recv()
        for cp in copies:
            cp.wait_send()


class _Both:
    def __init__(self, comms):
        self.comms = list(comms)
        self.arrays = [a for cm in self.comms for a in cm.arrays]
        self.out_shapes = [s for cm in self.comms for s in cm.out_shapes]
        self.scratch = [s for cm in self.comms for s in cm.scratch]

    def _split(self, ins, outs, sems):
        i = o = s = 0
        for cm in self.comms:
            ni, no, nsem = len(cm.arrays), len(cm.out_shapes), len(cm.scratch)
            yield cm, ins[i:i + ni], outs[o:o + no], sems[s:s + nsem]
            i, o, s = i + ni, o + no, s + nsem

    def start(self, ins, outs, sems):
        for cm, i, o, s in self._split(ins, outs, sems):
            cm.start(i, o, s)

    def finish(self, ins, outs, sems):
        for cm, i, o, s in self._split(ins, outs, sems):
            cm.finish(i, o, s)


def _host_call(body, *, name, grid, operands, in_specs, out_shape, out_specs, scratch_shapes=(), comm=None,
               prefetch=()):
    grid = tuple(grid)
    n_pre, n_in, n_out, n_scr = len(prefetch), len(operands), len(out_shape), len(scratch_shapes)
    nc_in, nc_out = (len(comm.arrays), len(comm.out_shapes)) if comm else (0, 0)
    all_in = list(in_specs) + [ANY] * nc_in
    all_out = list(out_specs) + [ANY] * nc_out
    all_scr = list(scratch_shapes) + (list(comm.scratch) if comm else [])
    all_shape = list(out_shape) + (list(comm.out_shapes) if comm else [])
    kwargs = dict(name=name, compiler_params=_params(len(grid)), out_shape=all_shape)
    if n_pre:
        kwargs["grid_spec"] = pltpu.PrefetchScalarGridSpec(
            num_scalar_prefetch=n_pre, grid=grid, in_specs=all_in, out_specs=all_out, scratch_shapes=all_scr)
    else:
        kwargs.update(in_specs=all_in, out_specs=all_out, scratch_shapes=all_scr)
        if grid:
            kwargs["grid"] = grid
    args = list(prefetch) + list(operands) + (list(comm.arrays) if comm else [])
    if comm is None:
        return list(pl.pallas_call(body, **kwargs)(*args)), []

    def hosted(*refs):
        bounds = [0, n_pre, n_pre + n_in]
        for n in (nc_in, n_out, nc_out, n_scr):
            bounds.append(bounds[-1] + n)
        bounds.append(len(refs))
        pre, ins, cins, outs, couts, scr, sems = [refs[a:b] for a, b in zip(bounds[:-1], bounds[1:])]
        if not grid:
            comm.start(cins, couts, sems)
            body(*pre, *ins, *outs, *scr)
            comm.finish(cins, couts, sems)
            return
        first, last = None, None
        for ax, size in enumerate(grid):
            pid = pl.program_id(ax)
            f, l = pid == 0, pid == size - 1
            first = f if first is None else jnp.logical_and(first, f)
            last = l if last is None else jnp.logical_and(last, l)

        @pl.when(first)
        def _():
            comm.start(cins, couts, sems)

        body(*pre, *ins, *outs, *scr)

        @pl.when(last)
        def _():
            comm.finish(cins, couts, sems)

    res = pl.pallas_call(hosted, **kwargs)(*args)
    return list(res[:n_out]), list(res[n_out:])


def _exchange_only(name, comm):
    def body():
        pass
    return _host_call(body, name=name, grid=(), operands=[], in_specs=[], out_shape=[], out_specs=[], comm=comm)[1]


def _matmul(name, grid, operands, in_specs, pairs, out_shapes, out_specs, epilogue, acc_shapes=(), nk=1,
            prologue=None, comm=None):
    n_in, n_out = len(operands), len(out_shapes)
    prologue = prologue or {}

    def body(*refs):
        ins, outs, accs = refs[:n_in], refs[n_in:n_in + n_out], refs[n_in + n_out:]
        pids = [pl.program_id(ax) for ax in range(len(grid))]

        def operand(i, blk=None):
            v = ins[i][...] if blk is None else ins[i][blk]
            if i in prologue:
                v = prologue[i](v)
            return v.astype(BF16)

        def products():
            vals = {}
            for pair in pairs:
                ai, bi, ci, dn = pair[:4]
                if len(pair) == 5:
                    p = None
                    for blk in range(pair[4]):
                        q = lax.dot_general(operand(ai, blk), operand(bi, blk), dn, preferred_element_type=F32)
                        p = q if p is None else p + q
                else:
                    p = lax.dot_general(operand(ai), operand(bi), dn, preferred_element_type=F32)
                vals[ci] = p if ci not in vals else vals[ci] + p
            return [vals[ci] for ci in sorted(vals)]

        if nk == 1:
            epilogue(products(), ins, outs, pids)
        else:
            k = pids[-1]
            prods = products()

            @pl.when(k == 0)
            def _():
                for acc, p in zip(accs, prods):
                    acc[...] = p

            @pl.when(k > 0)
            def _():
                for acc, p in zip(accs, prods):
                    acc[...] += p

            @pl.when(k == nk - 1)
            def _():
                epilogue([acc[...] for acc in accs], ins, outs, pids)

    return _host_call(
        body, name=name, grid=grid, operands=operands, in_specs=in_specs, out_shape=out_shapes, out_specs=out_specs,
        scratch_shapes=[pltpu.VMEM(s, F32) for s in acc_shapes] if nk > 1 else [], comm=comm)


def _rowwise(name, n_tiles, operands, in_specs, out_shapes, out_specs, red_widths, fn, comm=None):
    n_in, n_out, n_red = len(operands), len(out_shapes), len(red_widths)

    def body(*refs):
        ins, outs, reds = refs[:n_in], refs[n_in:n_in + n_out], refs[n_in + n_out:]
        i = pl.program_id(0)
        vals, sums = fn(i, *[r[...] for r in ins])
        for o, v in zip(outs, vals):
            o[...] = v.astype(o.dtype)
        if n_red:
            @pl.when(i == 0)
            def _():
                for r, s in zip(reds, sums):
                    r[...] = s

            @pl.when(i > 0)
            def _():
                for r, s in zip(reds, sums):
                    r[...] += s

    red_shapes = [jax.ShapeDtypeStruct((1, w), F32) for w in red_widths]
    red_specs = [pl.BlockSpec((1, w), lambda i: (0, 0)) for w in red_widths]
    res, cres = _host_call(
        body, name=name, grid=(n_tiles,), operands=operands, in_specs=in_specs,
        out_shape=list(out_shapes) + red_shapes, out_specs=list(out_specs) + red_specs, comm=comm)
    return res[:n_out], res[n_out:], cres


def _colsum(v):
    return jnp.sum(v, axis=0, keepdims=True)


def _store_all(accs, ins, outs, pids):
    for o, v in zip(outs, accs):
        o[...] = v.astype(o.dtype)


def _row_tile(rows_a, rows_b):
    return _tile(math.gcd(rows_a, rows_b) if rows_b else rows_a, ROW_TILE, SUBLANES)


def _tab_row(d, nlt, rows2):
    return pl.BlockSpec((None, 1, d), lambda i: (jnp.where(i < nlt, rows2[0], rows2[1]), 0, 0))


def _norm_mod_fwd(name, xs, tab, r_gamma, r_shift, r_scale, n_lat, n_ctx):
    rows, d = xs.shape
    tm = _row_tile(n_lat, n_ctx)
    nlt = n_lat // tm

    def fn(i, x, g, sh, sc):
        xh = x * lax.rsqrt(jnp.mean(x * x, axis=-1, keepdims=True) + NORM_EPS)
        return [(xh * g) * (1.0 + sc) + sh], []

    (h,), _, _ = _rowwise(
        name, rows // tm, [xs, tab, tab, tab],
        [pl.BlockSpec((tm, d), lambda i: (i, 0)), _tab_row(d, nlt, (r_gamma, r_gamma)), _tab_row(d, nlt, r_shift),
         _tab_row(d, nlt, r_scale)],
        [jax.ShapeDtypeStruct((rows, d), BF16)], [pl.BlockSpec((tm, d), lambda i: (i, 0))], [], fn)
    return h


def _norm_mod_bwd(name, xs, dh, tab, r_gamma, r_scale, n_lat, n_ctx, dres=None):
    rows, d = xs.shape
    tm = _row_tile(n_lat, n_ctx)
    nlt = n_lat // tm
    row = pl.BlockSpec((tm, d), lambda i: (i, 0))

    def fn(i, x, dy, g, sc, *res):
        rstd = lax.rsqrt(jnp.mean(x * x, axis=-1, keepdims=True) + NORM_EPS)
        xh = x * rstd
        dsh = _colsum(dy)
        dsc = _colsum(dy * (xh * g))
        dn = dy * (1.0 + sc)
        dgam = _colsum(dn * xh)
        dxh = dn * g
        dx = rstd * (dxh - xh * jnp.mean(dxh * xh, axis=-1, keepdims=True))
        if res:
            dx = dx + jnp.where(i < nlt, res[0], 0.0)
        lat = (i < nlt).astype(F32)
        return [dx], [dsh * lat, dsc * lat, dsh * (1.0 - lat), dsc * (1.0 - lat), dgam]

    operands = [xs, dh, tab, tab]
    specs = [row, row, _tab_row(d, nlt, (r_gamma, r_gamma)), _tab_row(d, nlt, r_scale)]
    if dres is not None:
        operands.append(dres)
        specs.append(pl.BlockSpec((tm, d), lambda i: (jnp.minimum(i, nlt - 1), 0)))
    (dx,), sums, _ = _rowwise(name, rows // tm, operands, specs,
                              [jax.ShapeDtypeStruct((rows, d), F32)], [row], [d] * 5, fn)
    return dx, sums


def _gate_bwd(name, dx, f, tab, r_gate, coef, n_lat, n_ctx):
    rows, d = dx.shape
    tm = _row_tile(n_lat, n_ctx)
    nlt = n_lat // tm
    row = pl.BlockSpec((tm, d), lambda i: (i, 0))

    def fn(i, dxv, fv, gv):
        dg = _colsum(dxv * fv) * coef
        lat = (i < nlt).astype(F32)
        return [(coef * gv) * dxv], [dg * lat, dg * (1.0 - lat)]

    (df,), sums, _ = _rowwise(
        name, rows // tm, [dx, f, tab],
        [row, row, _tab_row(d, nlt, r_gate)],
        [jax.ShapeDtypeStruct((rows, d), BF16)], [row], [d, d], fn)
    return df, sums


def _select_rows(i, tm, n_lat, v_lat, v_ctx):
    rows = i * tm + lax.broadcasted_iota(jnp.int32, (tm, 1), 0)
    return jnp.where(rows < n_lat, v_lat, v_ctx)


def _ffn_up(tag, h, wg, wu, comm=None):
    rows, d = h.shape
    nb, fs, _ = wg.shape
    tm = _tile(rows, MM_TILE, LANES)
    blk = pl.BlockSpec((None, tm, fs), lambda j, i: (j, i, 0))
    wspec = pl.BlockSpec((None, fs, d), lambda j, i: (j, 0, 0))

    def epilogue(accs, ins, outs, pids):
        a, b = accs
        outs[0][...] = a.astype(BF16)
        outs[1][...] = b.astype(BF16)
        outs[2][...] = (a * _sigmoid(a) * b).astype(BF16)

    hid = jax.ShapeDtypeStruct((nb, rows, fs), BF16)
    (a, b, s), cres = _matmul(
        tag + "_up", (nb, rows // tm), [h, wg, wu],
        [pl.BlockSpec((tm, d), lambda j, i: (i, 0)), wspec, wspec],
        [(0, 1, 0, NT), (0, 2, 1, NT)], [hid, hid, hid], [blk, blk, blk], epilogue, comm=comm)
    return a, b, s, cres


def _ffn_down(tag, s, wd, xs, tab2, r_gate, n_lat, comm=None):
    nb, rows, fs = s.shape
    d = wd.shape[-1]
    tm = _tile(rows, MM_TILE, LANES)
    tn = _tile(d, MM_TILE, LANES)

    def epilogue(accs, ins, outs, pids):
        f = accs[0]
        g = ins[3][...]
        gate = _select_rows(pids[0], tm, n_lat, g[r_gate[0]:r_gate[0] + 1, :], g[r_gate[1]:r_gate[1] + 1, :])
        outs[0][...] = f
        outs[1][...] = ins[2][...] + 0.5 * gate * f

    out = jax.ShapeDtypeStruct((rows, d), F32)
    ospec = pl.BlockSpec((tm, tn), lambda i, n: (i, n))
    (f, xo), cres = _matmul(
        tag + "_down", (rows // tm, d // tn), [s, wd, xs, tab2],
        [pl.BlockSpec((nb, tm, fs), lambda i, n: (0, i, 0)), pl.BlockSpec((nb, fs, tn), lambda i, n: (0, 0, n)),
         ospec, pl.BlockSpec((tab2.shape[0], tn), lambda i, n: (0, n))],
        [(0, 1, 0, NN, nb)], [out, out], [ospec, ospec], epilogue, comm=comm)
    return f, xo, cres


def _ffn_ds(tag, df, wd, a, b, comm=None):
    rows, d = df.shape
    nb, fs, _ = wd.shape
    tm = _tile(rows, MM_TILE, LANES)
    blk = pl.BlockSpec((None, tm, fs), lambda j, i: (j, i, 0))

    def epilogue(accs, ins, outs, pids):
        ds = accs[0]
        av = ins[2][...].astype(F32)
        bv = ins[3][...].astype(F32)
        sg = _sigmoid(av)
        outs[0][...] = (ds * bv * (sg * (1.0 + av * (1.0 - sg)))).astype(BF16)
        outs[1][...] = (ds * (av * sg)).astype(BF16)

    hid = jax.ShapeDtypeStruct((nb, rows, fs), BF16)
    (da, db), cres = _matmul(
        tag + "_ds", (nb, rows // tm), [df, wd, a, b],
        [pl.BlockSpec((tm, d), lambda j, i: (i, 0)), pl.BlockSpec((None, fs, d), lambda j, i: (j, 0, 0)), blk, blk],
        [(0, 1, 0, NT)], [hid, hid], [blk, blk], epilogue, comm=comm)
    return da, db, cres


def _ffn_dwd(tag, s, df, comm=None):
    nb, rows, fs = s.shape
    d = df.shape[-1]
    tn = _tile(d, MM_TILE, LANES)
    (dwd,), cres = _matmul(
        tag + "_dwd", (nb, d // tn), [s, df],
        [pl.BlockSpec((None, rows, fs), lambda j, n: (j, 0, 0)), pl.BlockSpec((rows, tn), lambda j, n: (0, n))],
        [(0, 1, 0, TN)], [jax.ShapeDtypeStruct((nb, fs, d), BF16)],
        [pl.BlockSpec((None, fs, tn), lambda j, n: (j, 0, n))], _store_all, comm=comm)
    return dwd, cres


def _ffn_dwgu(tag, h, da, db, comm=None):
    rows, d = h.shape
    nb, _, fs = da.shape
    tno = _tile(d, MM_TILE, LANES)
    full = pl.BlockSpec((None, rows, fs), lambda j, m: (j, 0, 0))
    wshape = jax.ShapeDtypeStruct((nb, fs, d), BF16)
    wblk = pl.BlockSpec((None, fs, tno), lambda j, m: (j, 0, m))
    (dwg, dwu), cres = _matmul(
        tag + "_dwgu", (nb, d // tno), [h, da, db],
        [pl.BlockSpec((rows, tno), lambda j, m: (0, m)), full, full],
        [(1, 0, 0, TN), (2, 0, 1, TN)], [wshape, wshape], [wblk, wblk], _store_all, comm=comm)
    return dwg, dwu, cres


def _ffn_dh(tag, da, db, wg, wu, comm=None):
    nb, rows, fs = da.shape
    d = wg.shape[2]
    tm = _tile(rows, MM_TILE, LANES)
    tn = _tile(d, MM_TILE_NT, LANES)
    aspec = pl.BlockSpec((nb, tm, fs), lambda i, n: (0, i, 0))
    wspec = pl.BlockSpec((nb, fs, tn), lambda i, n: (0, 0, n))
    (dh,), cres = _matmul(
        tag + "_dh", (rows // tm, d // tn), [da, wg, db, wu], [aspec, wspec, aspec, wspec],
        [(0, 1, 0, NN, nb), (2, 3, 0, NN, nb)], [jax.ShapeDtypeStruct((rows, d), F32)],
        [pl.BlockSpec((tm, tn), lambda i, n: (i, n))], _store_all, comm=comm)
    return dh, cres


def _rope_tables(n_lat, n_ctx):
    half = LANES // 4
    inv_freq = (np.float32(ROPE_THETA) ** (-np.arange(half, dtype=np.float32) / np.float32(half))).astype(np.float32)
    pos = np.arange(n_lat)
    ang_r = (pos // GRID_W).astype(np.float32)[:, None] * inv_freq
    ang_c = (pos % GRID_W).astype(np.float32)[:, None] * inv_freq
    cos_l = np.concatenate([np.cos(ang_r)] * 2 + [np.cos(ang_c)] * 2, axis=1)
    sin_l = np.concatenate([-np.sin(ang_r), np.sin(ang_r), -np.sin(ang_c), np.sin(ang_c)], axis=1)
    cos_all = np.concatenate([cos_l, np.ones((n_ctx, LANES), np.float32)], axis=0).astype(np.float32)
    sin_all = np.concatenate([sin_l, np.zeros((n_ctx, LANES), np.float32)], axis=0).astype(np.float32)
    return jnp.asarray(cos_all), jnp.asarray(sin_all)


def _swap_halves(x):
    lane = lax.broadcasted_iota(jnp.int32, x.shape, 1)
    return jnp.where((lane % 64) < 32, pltpu.roll(x, 96, 1), pltpu.roll(x, 32, 1))


def _heads_spec(tq, hb, width, first_block):
    per_shard = width // (hb * LANES)

    def index(k, i):
        blk = first_block + k
        return blk // per_shard, i, blk % per_shard
    return pl.BlockSpec((None, tq, hb * LANES), index)


def _qk_prep(name, src, first_block, hb, n_heads, rows, g, cos_t, sin_t):
    tq = _tile(rows, HEAD_ROW_TILE, SUBLANES)
    tab = pl.BlockSpec((tq, LANES), lambda k, i: (i, 0))

    def body(x_ref, g_ref, c_ref, s_ref, o_ref):
        for h in range(hb):
            x = x_ref[:, h * LANES:(h + 1) * LANES]
            n = x * lax.rsqrt(jnp.mean(x * x, axis=-1, keepdims=True) + NORM_EPS) * g_ref[...]
            o_ref[h] = (n * c_ref[...] + _swap_halves(n) * s_ref[...]).astype(BF16)

    return pl.pallas_call(
        body, name=name, grid=(n_heads // hb, rows // tq),
        in_specs=[_heads_spec(tq, hb, src.shape[-1], first_block), pl.BlockSpec((1, LANES), lambda k, i: (0, 0)),
                  tab, tab],
        out_specs=pl.BlockSpec((hb, tq, LANES), lambda k, i: (k, i, 0)),
        out_shape=jax.ShapeDtypeStruct((n_heads, rows, LANES), BF16), compiler_params=_params(2),
    )(src, g, cos_t, sin_t)


def _qk_prep_bwd(name, dy, src, first_block, hb, n_heads, rows, g, cos_t, sin_t):
    tq = _tile(rows, HEAD_ROW_TILE, SUBLANES)
    tab = pl.BlockSpec((tq, LANES), lambda k, i: (i, 0))

    def body(dy_ref, x_ref, g_ref, c_ref, s_ref, dx_ref, dg_ref):
        g = g_ref[...]
        dg = None
        for h in range(hb):
            x = x_ref[:, h * LANES:(h + 1) * LANES]
            dyv = dy_ref[h]
            rstd = lax.rsqrt(jnp.mean(x * x, axis=-1, keepdims=True) + NORM_EPS)
            xh = x * rstd
            dn = dyv * c_ref[...] + _swap_halves(dyv * s_ref[...])
            dxh = dn * g
            dx = rstd * (dxh - xh * jnp.mean(dxh * xh, axis=-1, keepdims=True))
            dx_ref[:, h * LANES:(h + 1) * LANES] = dx.astype(BF16)
            part = _colsum(dn * xh)
            dg = part if dg is None else dg + part
        first = jnp.logical_and(pl.program_id(0) == 0, pl.program_id(1) == 0)

        @pl.when(first)
        def _():
            dg_ref[...] = dg

        @pl.when(jnp.logical_not(first))
        def _():
            dg_ref[...] += dg

    return pl.pallas_call(
        body, name=name, grid=(n_heads // hb, rows // tq),
        in_specs=[pl.BlockSpec((hb, tq, LANES), lambda k, i: (k, i, 0)),
                  _heads_spec(tq, hb, src.shape[-1], first_block),
                  pl.BlockSpec((1, LANES), lambda k, i: (0, 0)), tab, tab],
        out_specs=[pl.BlockSpec((None, tq, hb * LANES), lambda k, i: (k, i, 0)),
                   pl.BlockSpec((1, LANES), lambda k, i: (0, 0))],
        out_shape=[jax.ShapeDtypeStruct((n_heads // hb, rows, hb * LANES), BF16),
                   jax.ShapeDtypeStruct((1, LANES), F32)],
        compiler_params=_params(2),
    )(dy, src, g, cos_t, sin_t)


def _heads_cast(name, src, first_block, hb, n_heads, rows):
    tq = _tile(rows, HEAD_ROW_TILE, SUBLANES)

    def body(x_ref, o_ref):
        for h in range(hb):
            o_ref[h] = x_ref[:, h * LANES:(h + 1) * LANES].astype(BF16)

    return pl.pallas_call(
        body, name=name, grid=(n_heads // hb, rows // tq),
        in_specs=[_heads_spec(tq, hb, src.shape[-1], first_block)],
        out_specs=pl.BlockSpec((hb, tq, LANES), lambda k, i: (k, i, 0)),
        out_shape=jax.ShapeDtypeStruct((n_heads, rows, LANES), BF16), compiler_params=_params(2),
    )(src)


def _heads_merge(name, src):
    n_heads, rows, _ = src.shape
    tq = _tile(rows, HEAD_ROW_TILE, SUBLANES)

    def body(x_ref, o_ref):
        for h in range(n_heads):
            o_ref[:, h * LANES:(h + 1) * LANES] = x_ref[h].astype(BF16)

    return pl.pallas_call(
        body, name=name, grid=(rows // tq,),
        in_specs=[pl.BlockSpec((n_heads, tq, LANES), lambda i: (0, i, 0))],
        out_specs=pl.BlockSpec((tq, n_heads * LANES), lambda i: (i, 0)),
        out_shape=jax.ShapeDtypeStruct((rows, n_heads * LANES), BF16), compiler_params=_params(1),
    )(src)


def _attn_fwd(q, k, v, q_per_kv, comm=None):
    nq, l, _ = q.shape
    s_len = k.shape[1]
    tq = _tile(l, ROW_TILE, SUBLANES)
    scale = LANES ** -0.5
    kv = pl.BlockSpec((None, s_len, LANES), lambda h, i: (h // q_per_kv, 0, 0))

    def body(q_ref, k_ref, v_ref, o_ref):
        s = lax.dot_general(q_ref[...], k_ref[...], NT, preferred_element_type=F32) * scale
        p = jnp.exp(s - jnp.max(s, axis=-1, keepdims=True))
        den = jnp.sum(p, axis=-1, keepdims=True)
        o = jnp.dot(p.astype(BF16), v_ref[...], preferred_element_type=F32)
        o_ref[...] = (o / den).astype(BF16)

    (o,), cres = _host_call(
        body, name="attn_fwd", grid=(nq, l // tq), operands=[q, k, v],
        in_specs=[pl.BlockSpec((None, tq, LANES), lambda h, i: (h, i, 0)), kv, kv],
        out_shape=[jax.ShapeDtypeStruct((l, nq * LANES), BF16)],
        out_specs=[pl.BlockSpec((tq, LANES), lambda h, i: (i, h))], comm=comm)
    return o, cres


def _attn_bwd(q, k, v, do, q_per_kv, comm=None):
    nq, l, _ = q.shape
    nkv, s_len, _ = k.shape
    tq = _tile(l, ROW_TILE, SUBLANES)
    scale = LANES ** -0.5
    hp = ATTN_BWD_HEADS if q_per_kv % ATTN_BWD_HEADS == 0 else 1
    kv = pl.BlockSpec((None, s_len, LANES), lambda g, r, i: (g, 0, 0))
    qs = pl.BlockSpec((hp, tq, LANES), lambda g, r, i: (g * (q_per_kv // hp) + r, i, 0))

    def body(q_ref, k_ref, v_ref, do_ref, dq_ref, dk_ref, dv_ref):
        kvv, vv = k_ref[...], v_ref[...]
        dk_new = dv_new = None
        for h in range(hp):
            qv, dov = q_ref[h], do_ref[:, h * LANES:(h + 1) * LANES]
            st = lax.dot_general(kvv, qv, NT, preferred_element_type=F32) * scale
            e = jnp.exp(st - jnp.max(st, axis=0, keepdims=True))
            pt = e / jnp.sum(e, axis=0, keepdims=True)
            dpt = lax.dot_general(vv, dov, NT, preferred_element_type=F32)
            delta = jnp.sum(pt * dpt, axis=0, keepdims=True)
            dst = (pt * (dpt - delta) * scale).astype(BF16)
            dq_ref[h] = lax.dot_general(dst, kvv, TN, preferred_element_type=F32)
            dk_h = jnp.dot(dst, qv, preferred_element_type=F32)
            dv_h = jnp.dot(pt.astype(BF16), dov, preferred_element_type=F32)
            dk_new = dk_h if dk_new is None else dk_new + dk_h
            dv_new = dv_h if dv_new is None else dv_new + dv_h
        first = jnp.logical_and(pl.program_id(1) == 0, pl.program_id(2) == 0)

        @pl.when(first)
        def _():
            dk_ref[...] = dk_new
            dv_ref[...] = dv_new

        @pl.when(jnp.logical_not(first))
        def _():
            dk_ref[...] += dk_new
            dv_ref[...] += dv_new

    (dq, dk, dv), cres = _host_call(
        body, name="attn_bwd", grid=(nkv, q_per_kv // hp, l // tq), operands=[q, k, v, do],
        in_specs=[qs, kv, kv, pl.BlockSpec((tq, hp * LANES), lambda g, r, i: (i, g * (q_per_kv // hp) + r))],
        out_specs=[qs, kv, kv],
        out_shape=[jax.ShapeDtypeStruct((nq, l, LANES), F32), jax.ShapeDtypeStruct((nkv, s_len, LANES), F32),
                   jax.ShapeDtypeStruct((nkv, s_len, LANES), F32)], comm=comm)
    return dq, dk, dv, cres


def _zoh(a_re, a_im, log_dt):
    dt = jnp.exp(log_dt)[..., None]
    mag = jnp.exp(a_re * dt)
    lb_re = mag * jnp.cos(a_im * dt)
    lb_im = mag * jnp.sin(a_im * dt)
    den = a_re * a_re + a_im * a_im
    coef_re = ((lb_re - 1.0) * a_re + lb_im * a_im) / den
    coef_im = (lb_im * a_re - (lb_re - 1.0) * a_im) / den
    return lb_re, lb_im, coef_re, coef_im


def _ssm_discretize(a_re, a_im, log_dt, b_re, b_im):
    lb_re, lb_im, cr, ci = _zoh(a_re, a_im, log_dt)
    bt_re = cr[..., None] * b_re - ci[..., None] * b_im
    bt_im = cr[..., None] * b_im + ci[..., None] * b_re
    return lb_re, lb_im, bt_re, bt_im


def _lambda_powers(a_re, a_im, log_dt, ns):
    dt = jnp.exp(log_dt)[..., None]
    k = jnp.arange(SCAN_TAPS + 1, dtype=F32)[:, None, None, None]
    mag, ang = jnp.exp(k * (a_re * dt)), k * (a_im * dt)
    shape = (SCAN_TAPS + 1, 2, ns, -1)
    return (mag * jnp.cos(ang)).reshape(shape), (mag * jnp.sin(ang)).reshape(shape)


def _slab_mask():
    idx = jnp.arange(SLAB_GROUPS)
    return (idx[:, None] == idx[None, :])[None, None, :, None, :, None]


def _block_diag(m):
    d, g, a, b = m.shape
    ns = g // SLAB_GROUPS
    wide = jnp.where(_slab_mask(), m.reshape(d, ns, SLAB_GROUPS, a, 1, b), 0.0)
    return wide.reshape(d, ns, SLAB_GROUPS * a, SLAB_GROUPS * b)


def _block_diag_extract(m, a, b):
    d, ns = m.shape[:2]
    m = m.reshape(d, ns, SLAB_GROUPS, a, SLAB_GROUPS, b)
    return jnp.sum(jnp.where(_slab_mask(), m, 0.0), axis=4).reshape(d, ns * SLAB_GROUPS, a, b)


def _tap_weights(base_re, base_im, pw_re, pw_im):
    pr = jnp.transpose(pw_re[:SCAN_TAPS], (1, 2, 0, 3))[:, :, :, None, :]
    pi = jnp.transpose(pw_im[:SCAN_TAPS], (1, 2, 0, 3))[:, :, :, None, :]
    br, bi = base_re[:, :, None], base_im[:, :, None]
    d, ns, cdim, s = base_re.shape
    re = (pr * br - pi * bi).reshape(d, ns, SCAN_TAPS * cdim, s)
    im = (pr * bi + pi * br).reshape(d, ns, SCAN_TAPS * cdim, s)
    return jnp.concatenate([re, im], axis=-1)


def _carry_tables(pw_re, pw_im, descending):
    def rows(pw):
        asc = pw[1:]
        per_dir = [asc[::-1, d] if descending[d] else asc[:, d] for d in range(2)]
        return jnp.transpose(jnp.stack(per_dir), (0, 2, 1, 3))
    return jnp.concatenate([rows(pw_re), rows(pw_im)], axis=-1)


def _scan_chunk(x, w_ref, tab_ref, s_ref, carry_ref, descending, t_rows, sw):
    row8 = lax.broadcasted_iota(jnp.int32, x.shape, 0) % SCAN_TAPS
    pieces = [x.astype(BF16)]
    for tau in range(1, SCAN_TAPS):
        if descending:
            sh = jnp.where(row8 <= SCAN_TAPS - 1 - tau, pltpu.roll(x, t_rows - tau, 0), 0.0)
        else:
            sh = jnp.where(row8 >= tau, pltpu.roll(x, tau, 0), 0.0)
        pieces.append(sh.astype(BF16))
    xa = jnp.concatenate(pieces, axis=1)
    s_ref[...] = jnp.dot(xa, w_ref[...], preferred_element_type=F32)
    tab = tab_ref[...]
    t_re, t_im = tab[:, :sw], tab[:, sw:]
    nb = t_rows // SCAN_TAPS
    edge = 0 if descending else SCAN_TAPS - 1

    def step(b, carry):
        h_re, h_im = carry
        r0 = pl.multiple_of(((nb - 1 - b) if descending else b) * SCAN_TAPS, SCAN_TAPS)
        x_re = s_ref[pl.ds(r0, SCAN_TAPS), :sw] + t_re * h_re - t_im * h_im
        x_im = s_ref[pl.ds(r0, SCAN_TAPS), sw:] + t_re * h_im + t_im * h_re
        s_ref[pl.ds(r0, SCAN_TAPS), :sw] = x_re
        s_ref[pl.ds(r0, SCAN_TAPS), sw:] = x_im
        return x_re[edge:edge + 1, :], x_im[edge:edge + 1, :]

    h_re, h_im = lax.fori_loop(0, nb, step, (carry_ref[0:1, :sw], carry_ref[0:1, sw:]))
    carry_ref[0:1, :sw] = h_re
    carry_ref[0:1, sw:] = h_im


def _ssm_fwd(name, dr, u_src, u_shard, waug, tab, cd, descending, chunk_of, t_rows, rows, comm=None):
    _, ns, kdim, sw2 = waug.shape
    sw = sw2 // 2
    width = ns * LANES
    nchunks = rows // t_rows

    def body(u_ref, w_ref, tab_ref, cd_ref, y_ref, h_ref, s_ref, carry_ref):
        @pl.when(pl.program_id(1) == 0)
        def _():
            carry_ref[...] = jnp.zeros_like(carry_ref)

        _scan_chunk(u_ref[...], w_ref, tab_ref, s_ref, carry_ref, descending, t_rows, sw)
        hb = s_ref[...].astype(BF16)
        h_ref[...] = hb
        y_ref[...] = jnp.dot(hb, cd_ref[...], preferred_element_type=F32)

    (y, h), cres = _host_call(
        body, name=name, grid=(ns, nchunks), operands=[u_src, waug, tab, cd],
        in_specs=[pl.BlockSpec((None, t_rows, LANES), lambda s, i: (u_shard, chunk_of(i), s)),
                  pl.BlockSpec((None, None, kdim, sw2), lambda s, i: (dr, s, 0, 0)),
                  pl.BlockSpec((None, None, SCAN_TAPS, sw2), lambda s, i: (dr, s, 0, 0)),
                  pl.BlockSpec((None, None, sw2, LANES), lambda s, i: (dr, s, 0, 0))],
        out_specs=[pl.BlockSpec((t_rows, LANES), lambda s, i: (chunk_of(i), s)),
                   pl.BlockSpec((None, t_rows, sw2), lambda s, i: (s, chunk_of(i), 0))],
        out_shape=[jax.ShapeDtypeStruct((rows, width), F32), jax.ShapeDtypeStruct((ns, rows, sw2), BF16)],
        scratch_shapes=[pltpu.VMEM((t_rows, sw2), F32), pltpu.VMEM((SUBLANES, sw2), F32)], comm=comm)
    return y, h, cres


def _ssm_bwd(name, dr, dy, u_src, u_shard, states, caug, tab, bdt, descending, chunk_of, t_rows, rows, comm=None):
    _, ns, kdim, sw2 = caug.shape
    sw = sw2 // 2
    width = ns * LANES
    nchunks = rows // t_rows

    def body(dy_ref, u_ref, h_ref, w_ref, tab_ref, bdt_ref, du_ref, dbd_ref, dcd_ref, dlam_ref,
             s_ref, carry_ref, gsave_ref):
        first = pl.program_id(1) == 0

        @pl.when(first)
        def _():
            carry_ref[...] = jnp.zeros_like(carry_ref)
            gsave_ref[...] = jnp.zeros_like(gsave_ref)

        dyv = dy_ref[...]
        _scan_chunk(dyv, w_ref, tab_ref, s_ref, carry_ref, descending, t_rows, sw)
        g = s_ref[...]
        gb = g.astype(BF16)
        du_ref[...] = jnp.dot(gb, bdt_ref[...], preferred_element_type=F32)
        dbd = lax.dot_general(u_ref[...].astype(BF16), gb, TN, preferred_element_type=F32)
        hb = h_ref[...]
        dcd = lax.dot_general(hb, dyv.astype(BF16), TN, preferred_element_type=F32)
        hf = hb.astype(F32)
        rowid = lax.broadcasted_iota(jnp.int32, hf.shape, 0)
        if descending:
            hp = jnp.where(rowid == 0, 0.0, pltpu.roll(hf, 1, 0))
            h_edge, g_edge = hf[t_rows - 1:t_rows, :], g[0:1, :]
        else:
            hp = jnp.where(rowid == t_rows - 1, 0.0, pltpu.roll(hf, t_rows - 1, 0))
            h_edge, g_edge = hf[0:1, :], g[t_rows - 1:t_rows, :]
        g_re, g_im, hp_re, hp_im = g[:, :sw], g[:, sw:], hp[:, :sw], hp[:, sw:]
        gs = gsave_ref[0:1, :]
        gs_re, gs_im, he_re, he_im = gs[:, :sw], gs[:, sw:], h_edge[:, :sw], h_edge[:, sw:]
        dl_re = _colsum(g_re * hp_re + g_im * hp_im) + gs_re * he_re + gs_im * he_im
        dl_im = _colsum(g_im * hp_re - g_re * hp_im) + gs_im * he_re - gs_re * he_im
        gsave_ref[0:1, :] = g_edge

        @pl.when(first)
        def _():
            dbd_ref[...] = dbd
            dcd_ref[...] = dcd
            dlam_ref[:, :sw] = dl_re
            dlam_ref[:, sw:] = dl_im

        @pl.when(jnp.logical_not(first))
        def _():
            dbd_ref[...] += dbd
            dcd_ref[...] += dcd
            dlam_ref[:, :sw] += dl_re
            dlam_ref[:, sw:] += dl_im

    (du, dbd, dcd, dlam), cres = _host_call(
        body, name=name, grid=(ns, nchunks), operands=[dy, u_src, states, caug, tab, bdt],
        in_specs=[pl.BlockSpec((t_rows, LANES), lambda s, i: (chunk_of(i), s)),
                  pl.BlockSpec((None, t_rows, LANES), lambda s, i: (u_shard, chunk_of(i), s)),
                  pl.BlockSpec((None, t_rows, sw2), lambda s, i: (s, chunk_of(i), 0)),
                  pl.BlockSpec((None, None, kdim, sw2), lambda s, i: (dr, s, 0, 0)),
                  pl.BlockSpec((None, None, SCAN_TAPS, sw2), lambda s, i: (dr, s, 0, 0)),
                  pl.BlockSpec((None, None, sw2, LANES), lambda s, i: (dr, s, 0, 0))],
        out_specs=[pl.BlockSpec((t_rows, LANES), lambda s, i: (chunk_of(i), s)),
                   pl.BlockSpec((None, LANES, sw2), lambda s, i: (s, 0, 0)),
                   pl.BlockSpec((None, sw2, LANES), lambda s, i: (s, 0, 0)),
                   pl.BlockSpec((None, 1, sw2), lambda s, i: (s, 0, 0))],
        out_shape=[jax.ShapeDtypeStruct((rows, width), F32), jax.ShapeDtypeStruct((ns, LANES, sw2), F32),
                   jax.ShapeDtypeStruct((ns, sw2, LANES), F32), jax.ShapeDtypeStruct((ns, 1, sw2), F32)],
        scratch_shapes=[pltpu.VMEM((t_rows, sw2), F32), pltpu.VMEM((SUBLANES, sw2), F32),
                        pltpu.VMEM((SUBLANES, sw2), F32)], comm=comm)
    return du, dbd, dcd, dlam, cres


def _mod_fwd(cs, w_mod, b_cols):
    d, width = w_mod.shape
    tn = _tile(width, 768, LANES)

    def epilogue(accs, ins, outs, pids):
        outs[0][...] = accs[0] + ins[2][...]

    return _matmul(
        "mod_fwd", (width // tn,), [cs, w_mod, b_cols],
        [pl.BlockSpec((16, d), lambda n: (0, 0)), pl.BlockSpec((d, tn), lambda n: (0, n)),
         pl.BlockSpec((1, tn), lambda n: (0, n))],
        [(0, 1, 0, NN)], [jax.ShapeDtypeStruct((16, width), F32)], [pl.BlockSpec((16, tn), lambda n: (0, n))],
        epilogue, prologue={0: lambda v: v * _sigmoid(v)})[0][0]


def _mod_bwd_adam(cs, dmod_cols, w, m, v, comm=None):
    d, width = w.shape
    tn = _tile(width, LANES, LANES)
    col = pl.BlockSpec((d, tn), lambda n: (0, n))

    def body(cs_ref, dm_ref, w_ref, m_ref, v_ref, g_ref, dl_ref, nm_ref, nv_ref, ds_ref):
        n = pl.program_id(0)
        lat = dm_ref[pl.ds(0, N_DEV, stride=SUBLANES), :]
        ctx = jnp.sum(dm_ref[pl.ds(1, N_DEV, stride=SUBLANES), :], axis=0, keepdims=True)
        row = lax.broadcasted_iota(jnp.int32, lat.shape, 0)
        dm = jnp.concatenate([lat, jnp.where(row == 0, ctx, 0.0)], axis=0).astype(BF16)
        c = cs_ref[...]
        sc = (c * _sigmoid(c)).astype(BF16)
        wv = w_ref[...]
        g = lax.dot_general(sc, dm, TN, preferred_element_type=F32)
        delta, m2, v2 = _adamw(wv, g, m_ref[...], v_ref[...])
        g_ref[...] = g
        dl_ref[...] = delta
        nm_ref[...] = m2
        nv_ref[...] = v2
        part = lax.dot_general(dm, wv.astype(BF16), NT, preferred_element_type=F32)

        @pl.when(n == 0)
        def _():
            ds_ref[...] = part

        @pl.when(n > 0)
        def _():
            ds_ref[...] += part

    shard = jax.ShapeDtypeStruct((d, width), F32)
    return _host_call(
        body, name="mod_bwd_adam", grid=(width // tn,), operands=[cs, dmod_cols, w, m, v],
        in_specs=[pl.BlockSpec((16, d), lambda n: (0, 0)), pl.BlockSpec((N_DEV * SUBLANES, tn), lambda n: (0, n)),
                  col, col, col],
        out_specs=[col, col, col, col, pl.BlockSpec((16, d), lambda n: (0, 0))],
        out_shape=[shard, shard, shard, shard, jax.ShapeDtypeStruct((16, d), F32)], comm=comm)


def _pair_sum(name, grads, got, core):
    _, rows, cols = grads.shape
    tr = _tile(rows, max(PACKED_SUBLANES, ADAM_BLOCK_BYTES // (cols * 6 * N_CHIPS)), PACKED_SUBLANES)
    blk = pl.BlockSpec((N_CHIPS, tr, cols), lambda i, cc: (0, i, 0))

    def body(core_ref, a_ref, b_ref, o_ref):
        o_ref[...] = (a_ref[...].astype(F32) + b_ref[...].astype(F32)).astype(BF16)

    grid_spec = pltpu.PrefetchScalarGridSpec(
        num_scalar_prefetch=1, grid=(rows // tr,),
        in_specs=[pl.BlockSpec((N_CHIPS, None, tr, cols), lambda i, cc: (0, cc[0], i, 0)), blk], out_specs=blk)
    return pl.pallas_call(
        body, name=name, grid_spec=grid_spec, out_shape=jax.ShapeDtypeStruct((N_CHIPS, rows, cols), BF16),
        compiler_params=_params(1))(core, grads.reshape(N_CHIPS, 2, rows, cols), got)


def _owner_adam(name, items, chip, comm=None):
    plan, start = [], 0
    per_element = 2 * (2 * N_CHIPS + 7 * 4)
    block_elements = ADAM_GROUP_VMEM // (per_element * len(items))
    for _, _, w, _, _ in items:
        rows, cols = w.shape
        tr = _tile(rows, max(PACKED_SUBLANES, block_elements // cols), PACKED_SUBLANES)
        plan.append((start, rows // tr, tr, cols))
        start += rows // tr
    operands, in_specs, out_specs, out_shape = [], [], [], []
    for (first, nt, tr, cols), (p, l, w, m, v) in zip(plan, items):
        def tile(s, first=first, nt=nt):
            return jnp.clip(s - first, 0, nt - 1)
        blk = pl.BlockSpec((tr, cols), lambda s, ch, tile=tile: (tile(s), 0))
        operands += [p, l, w, m, v]
        in_specs += [pl.BlockSpec((None, tr, cols), lambda s, ch, tile=tile: (ch[0], tile(s), 0)),
                     pl.BlockSpec((N_CHIPS - 1, tr, cols), lambda s, ch, tile=tile: (0, tile(s), 0)), blk, blk, blk]
        out_specs += [blk] * 4
        out_shape += [jax.ShapeDtypeStruct(w.shape, F32)] * 4
    n = len(items)

    def body(chip_ref, *refs):
        s = pl.program_id(0)
        for k, (first, nt, _, _) in enumerate(plan):
            p_ref, l_ref, w_ref, m_ref, v_ref = refs[5 * k:5 * k + 5]
            g_ref, dl_ref, nm_ref, nv_ref = refs[5 * n + 4 * k:5 * n + 4 * k + 4]

            @pl.when(jnp.logical_and(s >= first, s < first + nt))
            def _(p_ref=p_ref, l_ref=l_ref, w_ref=w_ref, m_ref=m_ref, v_ref=v_ref,
                  g_ref=g_ref, dl_ref=dl_ref, nm_ref=nm_ref, nv_ref=nv_ref):
                g = p_ref[...].astype(F32)
                for r in range(N_CHIPS - 1):
                    g = g + l_ref[r].astype(F32)
                delta, m2, v2 = _adamw(w_ref[...], g, m_ref[...], v_ref[...])
                g_ref[...] = g
                dl_ref[...] = delta
                nm_ref[...] = m2
                nv_ref[...] = v2

    res, cres = _host_call(body, name=name, grid=(start,), operands=operands, in_specs=in_specs,
                           out_shape=out_shape, out_specs=out_specs, comm=comm, prefetch=[chip])
    return [res[4 * k:4 * k + 4] for k in range(n)], cres


def _sum_adam(name, parts, w, m, v):
    rows, cols = w.shape
    n_parts = parts.shape[0]
    align = PACKED_SUBLANES if parts.dtype == BF16 else SUBLANES
    tr = _tile(rows, max(align, ADAM_BLOCK_BYTES // (cols * 44)), align)
    blk = pl.BlockSpec((tr, cols), lambda i: (i, 0))

    def body(p_ref, w_ref, m_ref, v_ref, g_ref, dl_ref, nm_ref, nv_ref):
        g = p_ref[0].astype(F32)
        for s in range(1, n_parts):
            g = g + p_ref[s].astype(F32)
        delta, m2, v2 = _adamw(w_ref[...], g, m_ref[...], v_ref[...])
        g_ref[...] = g
        dl_ref[...] = delta
        nm_ref[...] = m2
        nv_ref[...] = v2

    out = jax.ShapeDtypeStruct((rows, cols), F32)
    return pl.pallas_call(
        body, name=name, grid=(rows // tr,),
        in_specs=[pl.BlockSpec((n_parts, tr, cols), lambda i: (0, i, 0)), blk, blk, blk],
        out_specs=[blk, blk, blk, blk], out_shape=[out, out, out, out], compiler_params=_params(1),
    )(parts, w, m, v)


def _bias_adam(dmod_all, w, m, v):
    width = w.shape[-1]
    tn = _tile(width, 2048, LANES)
    blk = pl.BlockSpec((1, tn), lambda n: (0, n))

    def body(p_ref, w_ref, m_ref, v_ref, g_ref, dl_ref, nm_ref, nv_ref):
        g = jnp.sum(p_ref[...], axis=0, keepdims=True)
        delta, m2, v2 = _adamw(w_ref[...], g, m_ref[...], v_ref[...])
        g_ref[...] = g
        dl_ref[...] = delta
        nm_ref[...] = m2
        nv_ref[...] = v2

    out = jax.ShapeDtypeStruct((1, width), F32)
    return pl.pallas_call(
        body, name="bias_adam", grid=(width // tn,),
        in_specs=[pl.BlockSpec((dmod_all.shape[0], tn), lambda n: (0, n)), blk, blk, blk],
        out_specs=[blk, blk, blk, blk], out_shape=[out, out, out, out], compiler_params=_params(1),
    )(dmod_all, w, m, v)


def _pack(arrays, total_rows):
    flat = []
    for a in arrays:
        a = a.reshape(-1).astype(F32)
        flat.append(jnp.pad(a, (0, (-a.shape[0]) % LANES)))
    flat = jnp.concatenate(flat).reshape(-1, LANES)
    return jnp.pad(flat, ((0, total_rows - flat.shape[0]), (0, 0)))


def _unpack(packed, shapes):
    out, row = [], 0
    for shp in shapes:
        size = math.prod(shp)
        nrows = -(-size // LANES)
        out.append(packed[row:row + nrows].reshape(-1)[:size].reshape(shp))
        row += nrows
    return out


def kernel(x, c, ctx, c_ctx, w_mod, b_mod, norm_g, w_ffn1_gate, w_ffn1_up, w_ffn1_down, w_in, q_norm_g, k_norm_g, ssm_a_re, ssm_a_im, ssm_log_dt, ssm_b_re, ssm_b_im, ssm_c_re, ssm_c_im, ssm_d, w_glu, b_glu, w_br_attn, w_br_ssm, w_out, w_ffn2_gate, w_ffn2_up, w_ffn2_down, loss_target, m_c_ctx, m_w_mod, m_b_mod, m_norm_g, m_w_ffn1_gate, m_w_ffn1_up, m_w_ffn1_down, m_w_in, m_q_norm_g, m_k_norm_g, m_ssm_a_re, m_ssm_a_im, m_ssm_log_dt, m_ssm_b_re, m_ssm_b_im, m_ssm_c_re, m_ssm_c_im, m_ssm_d, m_w_glu, m_b_glu, m_w_br_attn, m_w_br_ssm, m_w_out, m_w_ffn2_gate, m_w_ffn2_up, m_w_ffn2_down, v_c_ctx, v_w_mod, v_b_mod, v_norm_g, v_w_ffn1_gate, v_w_ffn1_up, v_w_ffn1_down, v_w_in, v_q_norm_g, v_k_norm_g, v_ssm_a_re, v_ssm_a_im, v_ssm_log_dt, v_ssm_b_re, v_ssm_b_im, v_ssm_c_re, v_ssm_c_im, v_ssm_d, v_w_glu, v_b_glu, v_w_br_attn, v_w_br_ssm, v_w_out, v_w_ffn2_gate, v_w_ffn2_up, v_w_ffn2_down):
    _, L, D = x.shape
    Lc = ctx.shape[1]
    R = L + Lc
    MODW = w_mod.shape[-1]
    INS = w_in.shape[-1]
    KVW = INS // 2
    NQ = D // LANES
    NKV = KVW // LANES
    QPK = NQ // NKV
    HBQ = INS // LANES
    G, P, E = ssm_b_re.shape[2:]
    W = G * E
    SW = SLAB_GROUPS * P
    assert E * SLAB_GROUPS == LANES and W == INS and NQ * LANES == D and Lc <= L
    me = 4 * lax.axis_index("x") + 2 * lax.axis_index("y") + lax.axis_index("c")

    x2, ctx2, tgt = x[0], ctx[0], loss_target[0]
    xc0 = jnp.concatenate([x2, ctx2], axis=0)

    def bf(w):
        return w[0].astype(BF16)

    def held_t(w):
        return jnp.swapaxes(w[0], 0, 1)

    def bft(w):
        return held_t(w).astype(BF16)

    def widen(a):
        return jnp.pad(a[0], ((0, 0), (0, D - a.shape[-1])))

    def at_row(a, r, total):
        return jnp.pad(a, ((r, total - r - a.shape[0]), (0, 0)))

    pack_in = (at_row(c, 0, 16) + at_row(widen(norm_g), 1, 16) + at_row(widen(m_norm_g), 4, 16)
               + at_row(widen(v_norm_g), 7, 16))
    (g_in,) = _exchange_only("ag_inputs", _Gather([pack_in]))
    c_all = g_in[:, 0, :]
    dn = D // N_DEV

    def full_norm(k):
        return jnp.transpose(g_in[:, k:k + 3, :dn], (1, 0, 2)).reshape(3, D)

    ng_full, m_ng_full, v_ng_full = full_norm(1), full_norm(4), full_norm(7)
    cs = at_row(c_all, 0, 16) + at_row(c_ctx[None, :], 8, 16)

    b_cols = lax.dynamic_slice_in_dim(b_mod, me * MODW, MODW, axis=1)
    mod_blk = _mod_fwd(cs, w_mod[0], b_cols)
    (mod_g,) = _exchange_only("ag_mod", _Gather([mod_blk]))
    mod_lat = lax.dynamic_index_in_dim(mod_g, me, axis=1, keepdims=False).reshape(9, D)
    mod_ctx = mod_g[:, 8, :].reshape(9, D)[:5]
    tab2 = jnp.concatenate([mod_lat, mod_ctx, ng_full, jnp.zeros((7, D), F32)], axis=0)
    tab3 = tab2[:, None, :]
    SH1, SC1, G1, SH2, SC2, G2, SH3, SC3, G3, MC0, MC1, MC2, MC3, MC4, GAM1, GAM2, GAM3 = range(17)

    wg1, wu1 = _exchange_only("ag_ffn1_gate_up", _Gather([bft(w_ffn1_gate), bft(w_ffn1_up)]))
    h1 = _norm_mod_fwd("nm1_fwd", xc0, tab3, GAM1, (SH1, MC0), (SC1, MC1), L, Lc)
    a1, b1, s1, (wd1,) = _ffn_up("ffn1", h1, wg1, wu1, comm=_Gather([bf(w_ffn1_down)]))
    f1, xc1, (win,) = _ffn_down("ffn1", s1, wd1, xc0, tab2, (G1, MC2), L, comm=_Gather([bf(w_in)]))

    h2 = _norm_mod_fwd("nm2_fwd", xc1, tab3, GAM2, (SH2, MC3), (SC2, MC4), L, Lc)
    tm = _tile(R, MM_TILE, LANES)
    tml = _tile(L, MM_TILE, LANES)

    (p01,), _ = _matmul(
        "in_proj_kvu", (2, R // tm), [h2, win],
        [pl.BlockSpec((tm, D), lambda j, i: (i, 0)), pl.BlockSpec((None, D, INS), lambda j, i: (j, 0, 0))],
        [(0, 1, 0, NN)], [jax.ShapeDtypeStruct((2, R, INS), F32)],
        [pl.BlockSpec((None, tm, INS), lambda j, i: (j, i, 0))], _store_all)
    (p27,), (wglu, wbra) = _matmul(
        "in_proj_qg", (6, L // tml), [h2, win],
        [pl.BlockSpec((tml, D), lambda j, i: (i, 0)), pl.BlockSpec((None, D, INS), lambda j, i: (j + 2, 0, 0))],
        [(0, 1, 0, NN)], [jax.ShapeDtypeStruct((6, L, INS), F32)],
        [pl.BlockSpec((None, tml, INS), lambda j, i: (j, i, 0))], _store_all,
        comm=_Gather([bf(w_glu), bf(w_br_attn)]))
    wglu2 = wglu.reshape(W, W)
    wbra2 = wbra.reshape(D, D)

    cos_all, sin_all = _rope_tables(L, Lc)
    cos_l, sin_l = cos_all[:L], sin_all[:L]

    q_rot = _qk_prep("q_prep", p27, 0, HBQ, NQ, L, q_norm_g, cos_l, sin_l)
    k_rot = _qk_prep("k_prep", p01, 0, NKV, NKV, R, k_norm_g, cos_all, sin_all)
    v_hd = _heads_cast("v_heads", p01, 1, NKV, NKV, R)
    attn, (wbrs, wout, wg2) = _attn_fwd(
        q_rot, k_rot, v_hd, QPK, comm=_Gather([bf(w_br_ssm), bf(w_out), bft(w_ffn2_gate)]))
    wout2 = wout.reshape(D, D)

    t_rows = _tile(math.gcd(L, Lc), ROW_TILE, SUBLANES)
    nl, ncx = L // t_rows, Lc // t_rows
    nch = nl + ncx
    ns = G // SLAB_GROUPS
    ssm_prim = (ssm_a_re[0], ssm_a_im[0], ssm_log_dt[0], ssm_b_re[0], ssm_b_im[0])
    _, _, bt_re, bt_im = _ssm_discretize(*ssm_prim)
    pw_re, pw_im = _lambda_powers(ssm_a_re[0], ssm_a_im[0], ssm_log_dt[0], ns)
    bd_re = _block_diag(jnp.swapaxes(bt_re, 2, 3))
    bd_im = _block_diag(jnp.swapaxes(bt_im, 2, 3))
    ct_re = _block_diag(ssm_c_re[0])
    ct_im = _block_diag(-ssm_c_im[0])
    fwd_desc = (False, True)
    adj_desc = (True, False)
    s_waug = _tap_weights(bd_re, bd_im, pw_re, pw_im).astype(BF16)
    s_tab = _carry_tables(pw_re, pw_im, fwd_desc)
    s_cd = jnp.concatenate([jnp.swapaxes(ct_re, 2, 3), jnp.swapaxes(ct_im, 2, 3)], axis=2).astype(BF16)
    s_caug = _tap_weights(ct_re, ct_im, pw_re, -pw_im).astype(BF16)
    s_tabc = _carry_tables(pw_re, -pw_im, adj_desc)
    s_bdt = jnp.concatenate([jnp.swapaxes(bd_re, 2, 3), jnp.swapaxes(bd_im, 2, 3)], axis=2).astype(BF16)
    order = [lambda i: (i + nl) % nch, lambda i: nch - 1 - i]
    order_adj = [lambda i: (nch - 1 - i + nl) % nch, lambda i: i]
    y0, st0, (wu2,) = _ssm_fwd("ssm_fwd0", 0, p01, 1, s_waug, s_tab, s_cd, fwd_desc[0], order[0], t_rows, R,
                               comm=_Gather([bft(w_ffn2_up)]))
    y1, st1, (wd2,) = _ssm_fwd("ssm_fwd1", 1, p01, 1, s_waug, s_tab, s_cd, fwd_desc[1], order[1], t_rows, R,
                               comm=_Gather([bf(w_ffn2_down)]))
    states = [st0, st1]

    tr = _row_tile(L, 0)
    rowW = pl.BlockSpec((tr, W), lambda i: (i, 0))
    vecW = pl.BlockSpec((1, W), lambda i: (0, 0))
    u_lat = pl.BlockSpec((None, tr, W), lambda i: (1, i, 0))

    def ssm_post(i, u, ya, yb, dvec):
        sv = dvec * u + ya + yb
        return [sv, _gelu(sv)], []

    (ssm_out, yg), _, _ = _rowwise(
        "ssm_post", L // tr, [p01, y0, y1, ssm_d], [u_lat, rowW, rowW, vecW],
        [jax.ShapeDtypeStruct((L, W), F32), jax.ShapeDtypeStruct((L, W), BF16)], [rowW, rowW], [], ssm_post)

    tnw = _tile(W, MM_TILE, LANES)

    def glu_epilogue(accs, ins, outs, pids):
        z = accs[0] + ins[3][...]
        outs[0][...] = z
        outs[1][...] = (_gelu(ins[2][...]) * _sigmoid(z)).astype(BF16)

    (z_glu, y2), _ = _matmul(
        "glu", (L // tml, W // tnw), [yg, wglu2, ssm_out, b_glu],
        [pl.BlockSpec((tml, W), lambda i, n: (i, 0)), pl.BlockSpec((W, tnw), lambda i, n: (0, n)),
         pl.BlockSpec((tml, tnw), lambda i, n: (i, n)), pl.BlockSpec((1, tnw), lambda i, n: (0, n))],
        [(0, 1, 0, NN)], [jax.ShapeDtypeStruct((L, W), F32), jax.ShapeDtypeStruct((L, W), BF16)],
        [pl.BlockSpec((tml, tnw), lambda i, n: (i, n))] * 2, glu_epilogue)

    tnd = _tile(D, MM_TILE, LANES)
    out_ld = pl.BlockSpec((tml, tnd), lambda i, n: (i, n))
    (br_a,), _ = _matmul(
        "br_attn", (L // tml, D // tnd), [attn, wbra2],
        [pl.BlockSpec((tml, D), lambda i, n: (i, 0)), pl.BlockSpec((D, tnd), lambda i, n: (0, n))],
        [(0, 1, 0, NN)], [jax.ShapeDtypeStruct((L, D), F32)], [out_ld], _store_all)

    cb = wbrs.shape[-1]
    gpb = INS // cb

    def gate_spec(first_shard):
        return pl.BlockSpec((None, tml, cb), lambda i, j: (first_shard + j // gpb, i, j % gpb))

    def merge_epilogue(accs, ins, outs, pids):
        br = accs[0]
        outs[0][...] = br
        outs[1][...] = (_sigmoid(ins[2][...]) * ins[4][...] + _sigmoid(ins[3][...]) * br).astype(BF16)

    col_blk = pl.BlockSpec((tml, cb), lambda i, j: (i, j))
    (br_s, merged), _ = _matmul(
        "br_ssm_merge", (L // tml, N_DEV), [y2, wbrs, p27, p27, br_a],
        [pl.BlockSpec((tml, W), lambda i, j: (i, 0)), pl.BlockSpec((None, W, cb), lambda i, j: (j, 0, 0)),
         gate_spec(2), gate_spec(4), col_blk],
        [(0, 1, 0, NN)], [jax.ShapeDtypeStruct((L, D), F32), jax.ShapeDtypeStruct((L, D), BF16)],
        [col_blk, col_blk], merge_epilogue)

    def out_epilogue(accs, ins, outs, pids):
        outs[0][...] = accs[0]
        outs[1][...] = ins[2][...] + ins[3][G2:G2 + 1, :] * accs[0]

    (mix, x2_), _ = _matmul(
        "out_proj", (L // tml, D // tnd), [merged, wout2, xc1, tab2],
        [pl.BlockSpec((tml, D), lambda i, n: (i, 0)), pl.BlockSpec((D, tnd), lambda i, n: (0, n)), out_ld,
         pl.BlockSpec((tab2.shape[0], tnd), lambda i, n: (0, n))],
        [(0, 1, 0, NN)], [jax.ShapeDtypeStruct((L, D), F32)] * 2, [out_ld, out_ld], out_epilogue)

    h3 = _norm_mod_fwd("nm3_fwd", x2_, tab3, GAM3, (SH3, SH3), (SC3, SC3), L, 0)
    a3, b3, s3, _ = _ffn_up("ffn2", h3, wg2, wu2)
    f3, x3, _ = _ffn_down("ffn2", s3, wd2, x2_, tab2, (G3, G3), L)

    trd = _row_tile(L, 0)
    rowD = pl.BlockSpec((trd, D), lambda i: (i, 0))

    def loss_fn(i, yv, t):
        err = yv - t
        return [err * (1.0 / D)], [_colsum(err * err)]

    (dx3,), (sq,), _ = _rowwise("loss", L // trd, [x3, tgt], [rowD, rowD],
                                [jax.ShapeDtypeStruct((L, D), F32)], [rowD], [D], loss_fn)
    loss = lax.psum(0.5 * jnp.sum(sq) / D, ("x", "y", "c"))

    core = lax.axis_index("c").astype(jnp.int32).reshape(1)
    chip = (2 * lax.axis_index("x") + lax.axis_index("y")).astype(jnp.int32).reshape(1)

    def pair_sums(tag, grads, halves):
        return [_pair_sum("pair_%s%d" % (tag, k), g_, h_, core) for k, (g_, h_) in enumerate(zip(grads, halves))]

    df3, (dg3, _) = _gate_bwd("gate3_bwd", dx3, f3, tab3, (G3, G3), 0.5, L, 0)
    dwd2, _ = _ffn_dwd("ffn2b", s3, df3)
    da3, db3, half_wd2 = _ffn_ds("ffn2b", df3, wd2, a3, b3, comm=_SiblingSwap([dwd2]))
    (p_wd2,) = pair_sums("wd2", [dwd2], half_wd2)
    dwg2, dwu2, (l_wd2,) = _ffn_dwgu("ffn2b", h3, da3, db3, comm=_ChipExchange([p_wd2]))
    dh3, half_wgu2 = _ffn_dh("ffn2b", da3, db3, wg2, wu2, comm=_SiblingSwap([dwg2, dwu2]))
    p_wg2, p_wu2 = pair_sums("wgu2", [dwg2, dwu2], half_wgu2)
    dx2, (dsh3, dsc3, _, _, dgam3) = _norm_mod_bwd("nm3_bwd", x2_, dh3, tab3, GAM3, (SC3, SC3), L, 0, dres=dx3)

    dmix, (dg2, _) = _gate_bwd("gate2_bwd", dx2, mix, tab3, (G2, G2), 1.0, L, 0)

    def dmerged_epilogue(accs, ins, outs, pids):
        dm = accs[0]
        ga, gs = _sigmoid(ins[2][...]), _sigmoid(ins[3][...])
        outs[0][...] = (ga * dm).astype(BF16)
        outs[1][...] = (gs * dm).astype(BF16)
        outs[2][...] = (dm * ins[4][...] * ga * (1.0 - ga)).astype(BF16)
        outs[3][...] = (dm * ins[5][...] * gs * (1.0 - gs)).astype(BF16)

    dgate_spec = pl.BlockSpec((None, tml, cb), lambda i, j: (j // gpb, i, j % gpb))
    (d_br_a, d_br_s, dg_a, dg_s), _ = _matmul(
        "dmerged", (L // tml, N_DEV), [dmix, wout2, p27, p27, br_a, br_s],
        [pl.BlockSpec((tml, D), lambda i, j: (i, 0)), pl.BlockSpec((cb, D), lambda i, j: (j, 0)),
         gate_spec(2), gate_spec(4), col_blk, col_blk],
        [(0, 1, 0, NT)],
        [jax.ShapeDtypeStruct((L, D), BF16)] * 2 + [jax.ShapeDtypeStruct((2, L, INS), BF16)] * 2,
        [col_blk, col_blk, dgate_spec, dgate_spec], dmerged_epilogue)

    def wgrad(name, a_mat, b_mat, tmo, tno):
        ka, ma = a_mat.shape
        _, nb_ = b_mat.shape
        return _matmul(
            name, (ma // tmo, nb_ // tno), [a_mat, b_mat],
            [pl.BlockSpec((ka, tmo), lambda m, n: (0, m)), pl.BlockSpec((ka, tno), lambda m, n: (0, n))],
            [(0, 1, 0, TN)], [jax.ShapeDtypeStruct((ma, nb_), BF16)],
            [pl.BlockSpec((tmo, tno), lambda m, n: (m, n))], _store_all)[0][0]

    dwout = wgrad("dw_out", merged, dmix, tnd, tnd)
    dwbra = wgrad("dw_br_attn", attn, d_br_a, tnd, tnd)
    (d_attn,), _ = _matmul(
        "d_attn", (L // tml, D // tnd), [d_br_a, wbra2],
        [pl.BlockSpec((tml, D), lambda i, n: (i, 0)), pl.BlockSpec((tnd, D), lambda i, n: (n, 0))],
        [(0, 1, 0, NT)], [jax.ShapeDtypeStruct((L, D), BF16)], [out_ld], _store_all)

    (dwbrs,), _ = _matmul(
        "dw_br_ssm", (N_DEV,), [y2, d_br_s],
        [pl.BlockSpec((L, W), lambda j: (0, 0)), pl.BlockSpec((L, cb), lambda j: (0, j))],
        [(0, 1, 0, TN)], [jax.ShapeDtypeStruct((N_DEV, W, cb), BF16)],
        [pl.BlockSpec((None, W, cb), lambda j: (j, 0, 0))], _store_all)

    def dy2_epilogue(accs, ins, outs, pids):
        dy2 = accs[0]
        sg = _sigmoid(ins[2][...])
        outs[0][...] = dy2 * sg
        outs[1][...] = (dy2 * _gelu(ins[3][...]) * sg * (1.0 - sg)).astype(BF16)

    wn_blk = pl.BlockSpec((tml, tnw), lambda i, n, k: (i, n))
    (dyg1, dz), _ = _matmul(
        "d_y2", (L // tml, W // tnw, N_DEV), [d_br_s, wbrs, z_glu, ssm_out],
        [pl.BlockSpec((tml, cb), lambda i, n, k: (i, k)), pl.BlockSpec((None, tnw, cb), lambda i, n, k: (k, n, 0)),
         wn_blk, wn_blk],
        [(0, 1, 0, NT)], [jax.ShapeDtypeStruct((L, W), F32), jax.ShapeDtypeStruct((L, W), BF16)],
        [wn_blk, wn_blk], dy2_epilogue, acc_shapes=[(tml, tnw)], nk=N_DEV)

    dwglu = wgrad("dw_glu", yg, dz, tnw, tnw)
    mix_grads = [dwout.reshape(N_DEV, D // N_DEV, D), dwbra.reshape(N_DEV, D // N_DEV, D), dwbrs,
                 dwglu.reshape(N_DEV, W // N_DEV, W)]

    def dssm_epilogue(accs, ins, outs, pids):
        outs[0][...] = (accs[0] + ins[2][...]) * _gelu_grad(ins[3][...])

    wn2 = pl.BlockSpec((tml, tnw), lambda i, n: (i, n))
    (dssm,), _ = _matmul(
        "d_ssm", (L // tml, W // tnw), [dz, wglu2, dyg1, ssm_out],
        [pl.BlockSpec((tml, W), lambda i, n: (i, 0)), pl.BlockSpec((tnw, W), lambda i, n: (n, 0)), wn2, wn2],
        [(0, 1, 0, NT)], [jax.ShapeDtypeStruct((L, W), F32)], [wn2], dssm_epilogue)

    dssm_all = jnp.concatenate([dssm, jnp.zeros((Lc, W), F32)], axis=0)
    du0, dbd0, dcd0, dlam0, (l_wg2, *half_mix) = _ssm_bwd(
        "ssm_bwd0", 0, dssm_all, p01, 1, states[0], s_caug, s_tabc, s_bdt, adj_desc[0], order_adj[0], t_rows, R,
        comm=_Both([_ChipExchange([p_wg2]), _SiblingSwap(mix_grads)]))
    p_wout, p_wbra, p_wbrs, p_wglu = pair_sums("mix", mix_grads, half_mix)
    du1, dbd1, dcd1, dlam1, (l_wu2,) = _ssm_bwd(
        "ssm_bwd1", 1, dssm_all, p01, 1, states[1], s_caug, s_tabc, s_bdt, adj_desc[1], order_adj[1], t_rows, R,
        comm=_ChipExchange([p_wu2]))

    trr = _row_tile(L, Lc)
    nlt = L // trr
    rowR = pl.BlockSpec((trr, W), lambda i: (i, 0))

    def du_fn(i, dua, dub, dsv, dvec, u):
        lat = (i < nlt).astype(F32)
        return [dua + dub + lat * (dvec * dsv)], [lat * _colsum(dsv * u)]

    (du_all,), (d_ssm_d,), _ = _rowwise(
        "du_combine", R // trr, [du0, du1, dssm_all, ssm_d, p01],
        [rowR, rowR, rowR, pl.BlockSpec((1, W), lambda i: (0, 0)), pl.BlockSpec((None, trr, W), lambda i: (1, i, 0))],
        [jax.ShapeDtypeStruct((R, W), BF16)], [rowR], [W], du_fn)

    def dz_sum(i, dzv):
        return [], [_colsum(dzv.astype(F32))]

    _, (d_b_glu,), _ = _rowwise("db_glu", L // tr, [dz], [rowW], [], [], [W], dz_sum)

    dbd, dcd, dlam = jnp.stack([dbd0, dbd1]), jnp.stack([dcd0, dcd1]), jnp.stack([dlam0, dlam1])
    dbt_re = jnp.swapaxes(_block_diag_extract(dbd[..., :SW], E, P), 2, 3)
    dbt_im = jnp.swapaxes(_block_diag_extract(dbd[..., SW:], E, P), 2, 3)
    dl_re, dl_im = dlam[:, :, 0, :SW].reshape(2, G, P), dlam[:, :, 0, SW:].reshape(2, G, P)
    _, vjp = jax.vjp(_ssm_discretize, *ssm_prim)
    d_a_re, d_a_im, d_ldt, d_b_re, d_b_im = vjp((dl_re, dl_im, dbt_re, dbt_im))
    d_c_re = jnp.swapaxes(_block_diag_extract(dcd[:, :, :SW, :], P, E), 2, 3)
    d_c_im = -jnp.swapaxes(_block_diag_extract(dcd[:, :, SW:, :], P, E), 2, 3)

    early_g = [d_a_re, d_a_im, d_ldt, d_b_re, d_b_im, d_c_re, d_c_im, d_ssm_d, d_b_glu]
    early_w = [ssm_a_re, ssm_a_im, ssm_log_dt, ssm_b_re, ssm_b_im, ssm_c_re, ssm_c_im, ssm_d, b_glu]
    early_m = [m_ssm_a_re, m_ssm_a_im, m_ssm_log_dt, m_ssm_b_re, m_ssm_b_im, m_ssm_c_re, m_ssm_c_im, m_ssm_d, m_b_glu]
    early_v = [v_ssm_a_re, v_ssm_a_im, v_ssm_log_dt, v_ssm_b_re, v_ssm_b_im, v_ssm_c_re, v_ssm_c_im, v_ssm_d, v_b_glu]
    early_shapes = [a.shape for a in early_w]
    early_rows = -(-sum(-(-math.prod(s) // LANES) for s in early_shapes) // 256) * 256

    dq_rot, dk_rot, dv_hd, (l_wout, l_wbra, l_wbrs, l_wglu, early_parts) = _attn_bwd(
        q_rot, k_rot, v_hd, d_attn, QPK,
        comm=_Both([_ChipExchange([p_wout, p_wbra, p_wbrs, p_wglu]), _Gather([_pack(early_g, early_rows)])]))
    dq_pre, d_qg = _qk_prep_bwd("q_prep_bwd", dq_rot, p27, 0, HBQ, NQ, L, q_norm_g, cos_l, sin_l)
    dk_pre, d_kg = _qk_prep_bwd("k_prep_bwd", dk_rot, p01, 0, NKV, NKV, R, k_norm_g, cos_all, sin_all)
    dv_pre = _heads_merge("dv_merge", dv_hd)

    def lat_blocks(a):
        return jnp.pad(a, ((0, 0), (0, Lc), (0, 0)))

    dp = jnp.concatenate([
        jnp.concatenate([dk_pre[0], dv_pre], axis=1)[None], du_all[None],
        lat_blocks(dq_pre), lat_blocks(dg_a), lat_blocks(dg_s)], axis=0)

    tmo = _tile(D, MM_TILE, LANES)
    (dwin,), _ = _matmul(
        "dw_in", (N_DEV, D // tmo), [h2, dp],
        [pl.BlockSpec((R, tmo), lambda j, m: (0, m)), pl.BlockSpec((None, R, INS), lambda j, m: (j, 0, 0))],
        [(0, 1, 0, TN)], [jax.ShapeDtypeStruct((N_DEV, D, INS), BF16)],
        [pl.BlockSpec((None, tmo, INS), lambda j, m: (j, m, 0))], _store_all)
    tnh = _tile(D, MM_TILE_NT, LANES)
    (dh2,), half_win = _matmul(
        "d_h2", (R // tm, D // tnh), [dp, win],
        [pl.BlockSpec((N_DEV, tm, INS), lambda i, n: (0, i, 0)),
         pl.BlockSpec((N_DEV, tnh, INS), lambda i, n: (0, n, 0))],
        [(0, 1, 0, NT, N_DEV)], [jax.ShapeDtypeStruct((R, D), F32)], [pl.BlockSpec((tm, tnh), lambda i, n: (i, n))],
        _store_all, comm=_SiblingSwap([dwin]))
    (p_win,) = pair_sums("win", [dwin], half_win)
    dxc1, (dsh2, dsc2, dmc3, dmc4, dgam2) = _norm_mod_bwd(
        "nm2_bwd", xc1, dh2, tab3, GAM2, (SC2, MC4), L, Lc, dres=dx2)

    df1, (dg1, dmc2) = _gate_bwd("gate1_bwd", dxc1, f1, tab3, (G1, MC2), 0.5, L, Lc)
    dwd1, _ = _ffn_dwd("ffn1b", s1, df1)
    da1, db1, (l_win, *half_wd1) = _ffn_ds(
        "ffn1b", df1, wd1, a1, b1, comm=_Both([_ChipExchange([p_win]), _SiblingSwap([dwd1])]))
    (p_wd1,) = pair_sums("wd1", [dwd1], half_wd1)
    dwg1, dwu1, (l_wd1,) = _ffn_dwgu("ffn1b", h1, da1, db1, comm=_ChipExchange([p_wd1]))
    dh1, half_wgu1 = _ffn_dh("ffn1b", da1, db1, wg1, wu1, comm=_SiblingSwap([dwg1, dwu1]))
    p_wg1, p_wu1 = pair_sums("wgu1", [dwg1, dwu1], half_wgu1)

    def adam_item(p, l_, w_, m_, v_):
        return (p, l_, w_[0], m_[0], v_[0])

    def adam_item_t(p, l_, w_, m_, v_):
        return (p, l_, held_t(w_), held_t(m_), held_t(v_))

    ready_a = [adam_item(p_wd1, l_wd1, w_ffn1_down, m_w_ffn1_down, v_w_ffn1_down),
               adam_item(p_win, l_win, w_in, m_w_in, v_w_in),
               adam_item(p_wglu, l_wglu, w_glu, m_w_glu, v_w_glu),
               adam_item(p_wbra, l_wbra, w_br_attn, m_w_br_attn, v_w_br_attn),
               adam_item(p_wbrs, l_wbrs, w_br_ssm, m_w_br_ssm, v_w_br_ssm)]
    ready_b = [adam_item(p_wout, l_wout, w_out, m_w_out, v_w_out),
               adam_item_t(p_wg2, l_wg2, w_ffn2_gate, m_w_ffn2_gate, v_w_ffn2_gate),
               adam_item_t(p_wu2, l_wu2, w_ffn2_up, m_w_ffn2_up, v_w_ffn2_up),
               adam_item(p_wd2, l_wd2, w_ffn2_down, m_w_ffn2_down, v_w_ffn2_down)]
    adam_a, (l_wg1,) = _owner_adam("adam_ready_a", ready_a, chip, comm=_ChipExchange([p_wg1]))
    adam_b, (l_wu1,) = _owner_adam("adam_ready_b", ready_b, chip, comm=_ChipExchange([p_wu1]))
    adam_ready = adam_a + adam_b
    dxc0, (dsh1, dsc1, dmc0, dmc1, dgam1) = _norm_mod_bwd(
        "nm1_bwd", xc0, dh1, tab3, GAM1, (SC1, MC1), L, Lc, dres=dxc1)
    grad_x = dxc0[:L][None]

    dmod_lat = jnp.concatenate([dsh1, dsc1, dg1, dsh2, dsc2, dg2, dsh3, dsc3, dg3], axis=1)
    dmod_ctx = jnp.concatenate([dmc0, dmc1, dmc2, dmc3, dmc4, jnp.zeros((1, 4 * D), F32)], axis=1)
    dmod_pack = at_row(dmod_lat, 0, SUBLANES) + at_row(dmod_ctx, 1, SUBLANES)
    (dmod_g,) = _exchange_only("ag_dmod", _Gather([dmod_pack]))
    dmod_all = dmod_g.reshape(N_DEV * SUBLANES, 9 * D)
    dmod_cols = lax.dynamic_slice_in_dim(dmod_all, me * MODW, MODW, axis=1)
    (g_wmod, dl_wmod, nm_wmod, nv_wmod, dsilu), _ = _mod_bwd_adam(
        cs, dmod_cols, w_mod[0], m_w_mod[0], v_w_mod[0])
    sg_cc = jax.nn.sigmoid(c_ctx)
    d_c_ctx = dsilu[8] * (sg_cc * (1.0 + c_ctx * (1.0 - sg_cc)))
    g_bmod, dl_bmod, nm_bmod, nv_bmod = _bias_adam(dmod_all, b_mod, m_b_mod, v_b_mod)

    dgam_all = jnp.concatenate([dgam1, dgam2, dgam3], axis=0)
    late_g = [d_c_ctx, d_qg, d_kg, dgam_all]
    late_w = [c_ctx, q_norm_g, k_norm_g, ng_full]
    late_m = [m_c_ctx, m_q_norm_g, m_k_norm_g, m_ng_full]
    late_v = [v_c_ctx, v_q_norm_g, v_k_norm_g, v_ng_full]
    late_shapes = [a.shape for a in late_w]
    late_rows = -(-sum(-(-math.prod(s) // LANES) for s in late_shapes) // SUBLANES) * SUBLANES
    (late_parts,) = _exchange_only("ag_small_grads", _Gather([_pack(late_g, late_rows)]))
    late_out = _sum_adam("small_adam_late", late_parts, _pack(late_w, late_rows), _pack(late_m, late_rows),
                         _pack(late_v, late_rows))
    early_out = _sum_adam("small_adam_s5", early_parts, _pack(early_w, early_rows), _pack(early_m, early_rows),
                          _pack(early_v, early_rows))

    def my_norm_cols(a):
        return lax.dynamic_slice_in_dim(a, me * dn, dn, axis=1)[None]

    small = []
    for lo, eo in zip(late_out, early_out):
        c_ctx_, qg_, kg_, ng_ = _unpack(lo, late_shapes)
        small.append([c_ctx_, qg_, kg_] + _unpack(eo, early_shapes) + [my_norm_cols(ng_)])
    sm_g, sm_dl, sm_m, sm_v = small

    adam_last, _ = _owner_adam(
        "adam_last", [adam_item_t(p_wg1, l_wg1, w_ffn1_gate, m_w_ffn1_gate, v_w_ffn1_gate),
                      adam_item_t(p_wu1, l_wu1, w_ffn1_up, m_w_ffn1_up, v_w_ffn1_up)], chip)
    transposed = (0, 1, 8, 9)
    big_out = [[(jnp.swapaxes(o, 0, 1) if k in transposed else o)[None] for o in grp_]
               for k, grp_ in enumerate(adam_last + adam_ready)]

    def leaf(kind):
        sm = (sm_g, sm_dl, sm_m, sm_v)[kind]
        mod = (g_wmod, dl_wmod, nm_wmod, nv_wmod)[kind][None]
        bmod = (g_bmod, dl_bmod, nm_bmod, nv_bmod)[kind]
        big = [b[kind] for b in big_out]
        (c_ctx_, qg_, kg_, a_re_, a_im_, ldt_, b_re_, b_im_, c_re_, c_im_, sd_, bglu_, ng_) = sm
        return [c_ctx_, mod, bmod, ng_, big[0], big[1], big[2], big[3], qg_, kg_, a_re_, a_im_, ldt_, b_re_, b_im_,
                c_re_, c_im_, sd_, big[4], bglu_, big[5], big[6], big[7], big[8], big[9], big[10]]

    return tuple([loss, grad_x] + leaf(0) + leaf(1) + leaf(2) + leaf(3))
```

```python
import math

import jax
import jax.numpy as jnp
import numpy as np
from jax import lax
from jax.experimental import pallas as pl
from jax.experimental.pallas import tpu as pltpu

F32 = jnp.float32
BF16 = jnp.bfloat16

N_DEV = 8
N_CHIPS = 4
LANES = 128
SUBLANES = 8
PACKED_SUBLANES = 16
VMEM_LIMIT = 56 * 1024 * 1024
MM_TILE = 512
MM_TILE_NT = 256
ROW_TILE = 256
HEAD_ROW_TILE = 512
ATTN_BWD_HEADS = 2
ADAM_BLOCK_BYTES = 4 * 1024 * 1024
ADAM_GROUP_VMEM = 36 * 1024 * 1024

NORM_EPS = 1e-6
GRID_W = 64
ROPE_THETA = 10000.0
SCAN_TAPS = SUBLANES
SLAB_GROUPS = 8

ADAM_LR = 0.001
ADAM_B1 = 0.9
ADAM_B2 = 0.999
ADAM_EPS = 1e-08
ADAM_WD = 0.01
ADAM_STEP = 10

NN = (((1,), (0,)), ((), ()))
NT = (((1,), (1,)), ((), ()))
TN = (((0,), (0,)), ((), ()))

MESH = pl.DeviceIdType.MESH
ANY = pl.BlockSpec(memory_space=pl.ANY)


def _tile(n, cap, align):
    best = None
    for t in range(align, min(n, cap) + 1, align):
        if n % t == 0:
            best = t
    return n if best is None else best


def _params(n_grid):
    return pltpu.CompilerParams(dimension_semantics=("arbitrary",) * n_grid, vmem_limit_bytes=VMEM_LIMIT)


def _sigmoid(x):
    return 1.0 / (1.0 + jnp.exp(-x))


LOG2E = math.log2(math.e)
GELU_K = math.sqrt(2.0 / math.pi)
GELU_C = 0.044715


def _gelu(x):
    return 0.5 * x * (1.0 + jnp.tanh(GELU_K * (x + GELU_C * x * x * x)))


def _gelu_grad(x):
    t = jnp.tanh(GELU_K * (x + GELU_C * x * x * x))
    return 0.5 * (1.0 + t) + 0.5 * x * (1.0 - t * t) * GELU_K * (1.0 + 3.0 * GELU_C * x * x)


def _adamw(w, g, m, v):
    m2 = ADAM_B1 * m + (1.0 - ADAM_B1) * g
    v2 = ADAM_B2 * v + (1.0 - ADAM_B2) * (g * g)
    m_hat = m2 / (1.0 - ADAM_B1 ** ADAM_STEP)
    v_hat = v2 / (1.0 - ADAM_B2 ** ADAM_STEP)
    delta = -ADAM_LR * (m_hat / (jnp.sqrt(v_hat) + ADAM_EPS) + ADAM_WD * w)
    return delta, m2, v2


def _position():
    return lax.axis_index("x"), lax.axis_index("y"), lax.axis_index("c")


class _Gather:
    def __init__(self, arrays):
        self.arrays = list(arrays)
        n = len(self.arrays)
        self.out_shapes = [jax.ShapeDtypeStruct((N_DEV,) + a.shape, a.dtype) for a in self.arrays]
        self.scratch = [pltpu.SemaphoreType.DMA((n, 7)), pltpu.SemaphoreType.DMA((n, 7)),
                        pltpu.SemaphoreType.DMA((n,))]

    def _plan(self, ins, outs, sems):
        send, recv, local = sems
        x, y, c = _position()
        me, sibling = (x, y, c), (x, y, 1 - c)
        chips = [(1 - x, y), (x, 1 - y), (1 - x, 1 - y)]

        def slot(a, p):
            return outs[a].at[4 * p[0] + 2 * p[1] + p[2]]

        def copy(a, k, block, to, src=None):
            dst = slot(a, block)
            return pltpu.make_async_remote_copy(
                src_ref=dst if src is None else src, dst_ref=dst,
                send_sem=send.at[a, k], recv_sem=recv.at[a, k], device_id=to, device_id_type=MESH)

        mine = [pltpu.make_async_copy(ins[a], slot(a, me), local.at[a]) for a in range(len(ins))]
        return me, sibling, chips, c, copy, mine

    def start(self, ins, outs, sems):
        me, sibling, chips, c, copy, mine = self._plan(ins, outs, sems)
        for cp in mine:
            cp.start()
        for a in range(len(ins)):
            copy(a, 0, me, sibling, src=ins[a]).start()
            for j, chip in enumerate(chips):
                copy(a, 1 + j, me, (*chip, c), src=ins[a]).start()

    def finish(self, ins, outs, sems):
        me, sibling, chips, c, copy, mine = self._plan(ins, outs, sems)
        n = len(ins)
        for j, chip in enumerate(chips):
            for a in range(n):
                copy(a, 1 + j, (*chip, c), me).wait_recv()
                copy(a, 4 + j, (*chip, c), sibling).start()
        for a in range(n):
            copy(a, 0, sibling, me).wait_recv()
        for j, chip in enumerate(chips):
            for a in range(n):
                copy(a, 4 + j, (*chip, 1 - c), me).wait_recv()
        for a in range(n):
            copy(a, 0, me, sibling, src=ins[a]).wait_send()
            for j, chip in enumerate(chips):
                copy(a, 1 + j, me, (*chip, c), src=ins[a]).wait_send()
                copy(a, 4 + j, (*chip, c), sibling).wait_send()
        for cp in mine:
            cp.wait()


class _SiblingSwap:
    def __init__(self, arrays):
        self.arrays = list(arrays)
        n = len(self.arrays)
        self.out_shapes = [jax.ShapeDtypeStruct((N_CHIPS,) + a.shape[1:], a.dtype) for a in self.arrays]
        self.scratch = [pltpu.SemaphoreType.DMA((n, N_CHIPS)), pltpu.SemaphoreType.DMA((n, N_CHIPS))]

    def _plan(self, ins, outs, sems):
        send, recv = sems
        x, y, c = _position()
        return [pltpu.make_async_remote_copy(
            src_ref=ins[a].at[2 * j + 1 - c], dst_ref=outs[a].at[j],
            send_sem=send.at[a, j], recv_sem=recv.at[a, j], device_id=(x, y, 1 - c), device_id_type=MESH)
            for a in range(len(ins)) for j in range(N_CHIPS)]

    def start(self, ins, outs, sems):
        for cp in self._plan(ins, outs, sems):
            cp.start()

    def finish(self, ins, outs, sems):
        copies = self._plan(ins, outs, sems)
        for cp in copies:
            cp.wait_recv()
        for cp in copies:
            cp.wait_send()


class _ChipExchange:
    def __init__(self, arrays):
        self.arrays = list(arrays)
        n = len(self.arrays)
        self.out_shapes = [jax.ShapeDtypeStruct((N_CHIPS - 1,) + a.shape[1:], a.dtype) for a in self.arrays]
        self.scratch = [pltpu.SemaphoreType.DMA((n, N_CHIPS - 1)), pltpu.SemaphoreType.DMA((n, N_CHIPS - 1))]

    def _plan(self, ins, outs, sems):
        send, recv = sems
        x, y, c = _position()
        copies = []
        for r in range(1, N_CHIPS):
            px, py = x ^ (r >> 1), y ^ (r & 1)
            for a in range(len(ins)):
                copies.append(pltpu.make_async_remote_copy(
                    src_ref=ins[a].at[2 * px + py], dst_ref=outs[a].at[r - 1],
                    send_sem=send.at[a, r - 1], recv_sem=recv.at[a, r - 1],
                    device_id=(px, py, c), device_id_type=MESH))
        return copies

    def start(self, ins, outs, sems):
        for cp in self._plan(ins, outs, sems):
            cp.start()

    def finish(self, ins, outs, sems):
        copies = self._plan(ins, outs, sems)
        for cp in copies:
            cp.wait_recv()
        for cp in copies:
            cp.wait_send()


class _Both:
    def __init__(self, comms):
        self.comms = list(comms)
        self.arrays = [a for cm in self.comms for a in cm.arrays]
        self.out_shapes = [s for cm in self.comms for s in cm.out_shapes]
        self.scratch = [s for cm in self.comms for s in cm.scratch]

    def _split(self, ins, outs, sems):
        i = o = s = 0
        for cm in self.comms:
            ni, no, nsem = len(cm.arrays), len(cm.out_shapes), len(cm.scratch)
            yield cm, ins[i:i + ni], outs[o:o + no], sems[s:s + nsem]
            i, o, s = i + ni, o + no, s + nsem

    def start(self, ins, outs, sems):
        for cm, i, o, s in self._split(ins, outs, sems):
            cm.start(i, o, s)

    def finish(self, ins, outs, sems):
        for cm, i, o, s in self._split(ins, outs, sems):
            cm.finish(i, o, s)


def _host_call(body, *, name, grid, operands, in_specs, out_shape, out_specs, scratch_shapes=(), comm=None,
               prefetch=()):
    grid = tuple(grid)
    n_pre, n_in, n_out, n_scr = len(prefetch), len(operands), len(out_shape), len(scratch_shapes)
    nc_in, nc_out = (len(comm.arrays), len(comm.out_shapes)) if comm else (0, 0)
    all_in = list(in_specs) + [ANY] * nc_in
    all_out = list(out_specs) + [ANY] * nc_out
    all_scr = list(scratch_shapes) + (list(comm.scratch) if comm else [])
    all_shape = list(out_shape) + (list(comm.out_shapes) if comm else [])
    kwargs = dict(name=name, compiler_params=_params(len(grid)), out_shape=all_shape)
    if n_pre:
        kwargs["grid_spec"] = pltpu.PrefetchScalarGridSpec(
            num_scalar_prefetch=n_pre, grid=grid, in_specs=all_in, out_specs=all_out, scratch_shapes=all_scr)
    else:
        kwargs.update(in_specs=all_in, out_specs=all_out, scratch_shapes=all_scr)
        if grid:
            kwargs["grid"] = grid
    args = list(prefetch) + list(operands) + (list(comm.arrays) if comm else [])
    if comm is None:
        return list(pl.pallas_call(body, **kwargs)(*args)), []

    def hosted(*refs):
        bounds = [0, n_pre, n_pre + n_in]
        for n in (nc_in, n_out, nc_out, n_scr):
            bounds.append(bounds[-1] + n)
        bounds.append(len(refs))
        pre, ins, cins, outs, couts, scr, sems = [refs[a:b] for a, b in zip(bounds[:-1], bounds[1:])]
        if not grid:
            comm.start(cins, couts, sems)
            body(*pre, *ins, *outs, *scr)
            comm.finish(cins, couts, sems)
            return
        first, last = None, None
        for ax, size in enumerate(grid):
            pid = pl.program_id(ax)
            f, l = pid == 0, pid == size - 1
            first = f if first is None else jnp.logical_and(first, f)
            last = l if last is None else jnp.logical_and(last, l)

        @pl.when(first)
        def _():
            comm.start(cins, couts, sems)

        body(*pre, *ins, *outs, *scr)

        @pl.when(last)
        def _():
            comm.finish(cins, couts, sems)

    res = pl.pallas_call(hosted, **kwargs)(*args)
    return list(res[:n_out]), list(res[n_out:])


def _exchange_only(name, comm):
    def body():
        pass
    return _host_call(body, name=name, grid=(), operands=[], in_specs=[], out_shape=[], out_specs=[], comm=comm)[1]


def _matmul(name, grid, operands, in_specs, pairs, out_shapes, out_specs, epilogue, acc_shapes=(), nk=1,
            prologue=None, comm=None):
    n_in, n_out = len(operands), len(out_shapes)
    prologue = prologue or {}

    def body(*refs):
        ins, outs, accs = refs[:n_in], refs[n_in:n_in + n_out], refs[n_in + n_out:]
        pids = [pl.program_id(ax) for ax in range(len(grid))]

        def operand(i, blk=None):
            v = ins[i][...] if blk is None else ins[i][blk]
            if i in prologue:
                v = prologue[i](v)
            return v.astype(BF16)

        def products():
            vals = {}
            for pair in pairs:
                ai, bi, ci, dn = pair[:4]
                if len(pair) == 5:
                    p = None
                    for blk in range(pair[4]):
                        q = lax.dot_general(operand(ai, blk), operand(bi, blk), dn, preferred_element_type=F32)
                        p = q if p is None else p + q
                else:
                    p = lax.dot_general(operand(ai), operand(bi), dn, preferred_element_type=F32)
                vals[ci] = p if ci not in vals else vals[ci] + p
            return [vals[ci] for ci in sorted(vals)]

        if nk == 1:
            epilogue(products(), ins, outs, pids)
        else:
            k = pids[-1]
            prods = products()

            @pl.when(k == 0)
            def _():
                for acc, p in zip(accs, prods):
                    acc[...] = p

            @pl.when(k > 0)
            def _():
                for acc, p in zip(accs, prods):
                    acc[...] += p

            @pl.when(k == nk - 1)
            def _():
                epilogue([acc[...] for acc in accs], ins, outs, pids)

    return _host_call(
        body, name=name, grid=grid, operands=operands, in_specs=in_specs, out_shape=out_shapes, out_specs=out_specs,
        scratch_shapes=[pltpu.VMEM(s, F32) for s in acc_shapes] if nk > 1 else [], comm=comm)


def _rowwise(name, n_tiles, operands, in_specs, out_shapes, out_specs, red_widths, fn, comm=None):
    n_in, n_out, n_red = len(operands), len(out_shapes), len(red_widths)

    def body(*refs):
        ins, outs, reds = refs[:n_in], refs[n_in:n_in + n_out], refs[n_in + n_out:]
        i = pl.program_id(0)
        vals, sums = fn(i, *[r[...] for r in ins])
        for o, v in zip(outs, vals):
            o[...] = v.astype(o.dtype)
        if n_red:
            @pl.when(i == 0)
            def _():
                for r, s in zip(reds, sums):
                    r[...] = s

            @pl.when(i > 0)
            def _():
                for r, s in zip(reds, sums):
                    r[...] += s

    red_shapes = [jax.ShapeDtypeStruct((1, w), F32) for w in red_widths]
    red_specs = [pl.BlockSpec((1, w), lambda i: (0, 0)) for w in red_widths]
    res, cres = _host_call(
        body, name=name, grid=(n_tiles,), operands=operands, in_specs=in_specs,
        out_shape=list(out_shapes) + red_shapes, out_specs=list(out_specs) + red_specs, comm=comm)
    return res[:n_out], res[n_out:], cres


def _colsum(v):
    return jnp.sum(v, axis=0, keepdims=True)


def _store_all(accs, ins, outs, pids):
    for o, v in zip(outs, accs):
        o[...] = v.astype(o.dtype)


def _row_tile(rows_a, rows_b):
    return _tile(math.gcd(rows_a, rows_b) if rows_b else rows_a, ROW_TILE, SUBLANES)


def _tab_row(d, nlt, rows2):
    return pl.BlockSpec((None, 1, d), lambda i: (jnp.where(i < nlt, rows2[0], rows2[1]), 0, 0))


def _norm_mod_fwd(name, xs, tab, r_gamma, r_shift, r_scale, n_lat, n_ctx):
    rows, d = xs.shape
    tm = _row_tile(n_lat, n_ctx)
    nlt = n_lat // tm

    def fn(i, x, g, sh, sc):
        xh = x * lax.rsqrt(jnp.mean(x * x, axis=-1, keepdims=True) + NORM_EPS)
        return [(xh * g) * (1.0 + sc) + sh], []

    (h,), _, _ = _rowwise(
        name, rows // tm, [xs, tab, tab, tab],
        [pl.BlockSpec((tm, d), lambda i: (i, 0)), _tab_row(d, nlt, (r_gamma, r_gamma)), _tab_row(d, nlt, r_shift),
         _tab_row(d, nlt, r_scale)],
        [jax.ShapeDtypeStruct((rows, d), BF16)], [pl.BlockSpec((tm, d), lambda i: (i, 0))], [], fn)
    return h


def _norm_mod_bwd(name, xs, dh, tab, r_gamma, r_scale, n_lat, n_ctx, dres=None):
    rows, d = xs.shape
    tm = _row_tile(n_lat, n_ctx)
    nlt = n_lat // tm
    row = pl.BlockSpec((tm, d), lambda i: (i, 0))

    def fn(i, x, dy, g, sc, *res):
        rstd = lax.rsqrt(jnp.mean(x * x, axis=-1, keepdims=True) + NORM_EPS)
        xh = x * rstd
        dsh = _colsum(dy)
        dsc = _colsum(dy * (xh * g))
        dn = dy * (1.0 + sc)
        dgam = _colsum(dn * xh)
        dxh = dn * g
        dx = rstd * (dxh - xh * jnp.mean(dxh * xh, axis=-1, keepdims=True))
        if res:
            dx = dx + jnp.where(i < nlt, res[0], 0.0)
        lat = (i < nlt).astype(F32)
        return [dx], [dsh * lat, dsc * lat, dsh * (1.0 - lat), dsc * (1.0 - lat), dgam]

    operands = [xs, dh, tab, tab]
    specs = [row, row, _tab_row(d, nlt, (r_gamma, r_gamma)), _tab_row(d, nlt, r_scale)]
    if dres is not None:
        operands.append(dres)
        specs.append(pl.BlockSpec((tm, d), lambda i: (jnp.minimum(i, nlt - 1), 0)))
    (dx,), sums, _ = _rowwise(name, rows // tm, operands, specs,
                              [jax.ShapeDtypeStruct((rows, d), F32)], [row], [d] * 5, fn)
    return dx, sums


def _gate_bwd(name, dx, f, tab, r_gate, coef, n_lat, n_ctx):
    rows, d = dx.shape
    tm = _row_tile(n_lat, n_ctx)
    nlt = n_lat // tm
    row = pl.BlockSpec((tm, d), lambda i: (i, 0))

    def fn(i, dxv, fv, gv):
        dg = _colsum(dxv * fv) * coef
        lat = (i < nlt).astype(F32)
        return [(coef * gv) * dxv], [dg * lat, dg * (1.0 - lat)]

    (df,), sums, _ = _rowwise(
        name, rows // tm, [dx, f, tab],
        [row, row, _tab_row(d, nlt, r_gate)],
        [jax.ShapeDtypeStruct((rows, d), BF16)], [row], [d, d], fn)
    return df, sums


def _select_rows(i, tm, n_lat, v_lat, v_ctx):
    rows = i * tm + lax.broadcasted_iota(jnp.int32, (tm, 1), 0)
    return jnp.where(rows < n_lat, v_lat, v_ctx)


def _ffn_up(tag, h, wg, wu, comm=None):
    rows, d = h.shape
    nb, fs, _ = wg.shape
    tm = _tile(rows, MM_TILE, LANES)
    blk = pl.BlockSpec((None, tm, fs), lambda j, i: (j, i, 0))
    wspec = pl.BlockSpec((None, fs, d), lambda j, i: (j, 0, 0))

    def epilogue(accs, ins, outs, pids):
        a, b = accs
        outs[0][...] = a.astype(BF16)
        outs[1][...] = b.astype(BF16)
        outs[2][...] = (a * _sigmoid(a) * b).astype(BF16)

    hid = jax.ShapeDtypeStruct((nb, rows, fs), BF16)
    (a, b, s), cres = _matmul(
        tag + "_up", (nb, rows // tm), [h, wg, wu],
        [pl.BlockSpec((tm, d), lambda j, i: (i, 0)), wspec, wspec],
        [(0, 1, 0, NT), (0, 2, 1, NT)], [hid, hid, hid], [blk, blk, blk], epilogue, comm=comm)
    return a, b, s, cres


def _ffn_down(tag, s, wd, xs, tab2, r_gate, n_lat, comm=None):
    nb, rows, fs = s.shape
    d = wd.shape[-1]
    tm = _tile(rows, MM_TILE, LANES)
    tn = _tile(d, MM_TILE, LANES)

    def epilogue(accs, ins, outs, pids):
        f = accs[0]
        g = ins[3][...]
        gate = _select_rows(pids[0], tm, n_lat, g[r_gate[0]:r_gate[0] + 1, :], g[r_gate[1]:r_gate[1] + 1, :])
        outs[0][...] = f
        outs[1][...] = ins[2][...] + 0.5 * gate * f

    out = jax.ShapeDtypeStruct((rows, d), F32)
    ospec = pl.BlockSpec((tm, tn), lambda i, n: (i, n))
    (f, xo), cres = _matmul(
        tag + "_down", (rows // tm, d // tn), [s, wd, xs, tab2],
        [pl.BlockSpec((nb, tm, fs), lambda i, n: (0, i, 0)), pl.BlockSpec((nb, fs, tn), lambda i, n: (0, 0, n)),
         ospec, pl.BlockSpec((tab2.shape[0], tn), lambda i, n: (0, n))],
        [(0, 1, 0, NN, nb)], [out, out], [ospec, ospec], epilogue, comm=comm)
    return f, xo, cres


def _ffn_ds(tag, df, wd, a, b, comm=None):
    rows, d = df.shape
    nb, fs, _ = wd.shape
    tm = _tile(rows, MM_TILE, LANES)
    blk = pl.BlockSpec((None, tm, fs), lambda j, i: (j, i, 0))

    def epilogue(accs, ins, outs, pids):
        ds = accs[0]
        av = ins[2][...].astype(F32)
        bv = ins[3][...].astype(F32)
        sg = _sigmoid(av)
        outs[0][...] = (ds * bv * (sg * (1.0 + av * (1.0 - sg)))).astype(BF16)
        outs[1][...] = (ds * (av * sg)).astype(BF16)

    hid = jax.ShapeDtypeStruct((nb, rows, fs), BF16)
    (da, db), cres = _matmul(
        tag + "_ds", (nb, rows // tm), [df, wd, a, b],
        [pl.BlockSpec((tm, d), lambda j, i: (i, 0)), pl.BlockSpec((None, fs, d), lambda j, i: (j, 0, 0)), blk, blk],
        [(0, 1, 0, NT)], [hid, hid], [blk, blk], epilogue, comm=comm)
    return da, db, cres


def _ffn_dwd(tag, s, df, comm=None):
    nb, rows, fs = s.shape
    d = df.shape[-1]
    tn = _tile(d, MM_TILE, LANES)
    (dwd,), cres = _matmul(
        tag + "_dwd", (nb, d // tn), [s, df],
        [pl.BlockSpec((None, rows, fs), lambda j, n: (j, 0, 0)), pl.BlockSpec((rows, tn), lambda j, n: (0, n))],
        [(0, 1, 0, TN)], [jax.ShapeDtypeStruct((nb, fs, d), BF16)],
        [pl.BlockSpec((None, fs, tn), lambda j, n: (j, 0, n))], _store_all, comm=comm)
    return dwd, cres


def _ffn_dwgu(tag, h, da, db, comm=None):
    rows, d = h.shape
    nb, _, fs = da.shape
    tno = _tile(d, MM_TILE, LANES)
    full = pl.BlockSpec((None, rows, fs), lambda j, m: (j, 0, 0))
    wshape = jax.ShapeDtypeStruct((nb, fs, d), BF16)
    wblk = pl.BlockSpec((None, fs, tno), lambda j, m: (j, 0, m))
    (dwg, dwu), cres = _matmul(
        tag + "_dwgu", (nb, d // tno), [h, da, db],
        [pl.BlockSpec((rows, tno), lambda j, m: (0, m)), full, full],
        [(1, 0, 0, TN), (2, 0, 1, TN)], [wshape, wshape], [wblk, wblk], _store_all, comm=comm)
    return dwg, dwu, cres


def _ffn_dh(tag, da, db, wg, wu, comm=None):
    nb, rows, fs = da.shape
    d = wg.shape[2]
    tm = _tile(rows, MM_TILE, LANES)
    tn = _tile(d, MM_TILE_NT, LANES)
    aspec = pl.BlockSpec((nb, tm, fs), lambda i, n: (0, i, 0))
    wspec = pl.BlockSpec((nb, fs, tn), lambda i, n: (0, 0, n))
    (dh,), cres = _matmul(
        tag + "_dh", (rows // tm, d // tn), [da, wg, db, wu], [aspec, wspec, aspec, wspec],
        [(0, 1, 0, NN, nb), (2, 3, 0, NN, nb)], [jax.ShapeDtypeStruct((rows, d), F32)],
        [pl.BlockSpec((tm, tn), lambda i, n: (i, n))], _store_all, comm=comm)
    return dh, cres


def _rope_tables(n_lat, n_ctx):
    half = LANES // 4
    inv_freq = (np.float32(ROPE_THETA) ** (-np.arange(half, dtype=np.float32) / np.float32(half))).astype(np.float32)
    pos = np.arange(n_lat)
    ang_r = (pos // GRID_W).astype(np.float32)[:, None] * inv_freq
    ang_c = (pos % GRID_W).astype(np.float32)[:, None] * inv_freq
    cos_l = np.concatenate([np.cos(ang_r)] * 2 + [np.cos(ang_c)] * 2, axis=1)
    sin_l = np.concatenate([-np.sin(ang_r), np.sin(ang_r), -np.sin(ang_c), np.sin(ang_c)], axis=1)
    cos_all = np.concatenate([cos_l, np.ones((n_ctx, LANES), np.float32)], axis=0).astype(np.float32)
    sin_all = np.concatenate([sin_l, np.zeros((n_ctx, LANES), np.float32)], axis=0).astype(np.float32)
    return jnp.asarray(cos_all), jnp.asarray(sin_all)


def _swap_halves(x):
    lane = lax.broadcasted_iota(jnp.int32, x.shape, 1)
    return jnp.where((lane % 64) < 32, pltpu.roll(x, 96, 1), pltpu.roll(x, 32, 1))


def _heads_spec(tq, hb, width, first_block):
    per_shard = width // (hb * LANES)

    def index(k, i):
        blk = first_block + k
        return blk // per_shard, i, blk % per_shard
    return pl.BlockSpec((None, tq, hb * LANES), index)


def _qk_prep(name, src, first_block, hb, n_heads, rows, g, cos_t, sin_t):
    tq = _tile(rows, HEAD_ROW_TILE, SUBLANES)
    tab = pl.BlockSpec((tq, LANES), lambda k, i: (i, 0))

    def body(x_ref, g_ref, c_ref, s_ref, o_ref):
        for h in range(hb):
            x = x_ref[:, h * LANES:(h + 1) * LANES]
            n = x * lax.rsqrt(jnp.mean(x * x, axis=-1, keepdims=True) + NORM_EPS) * g_ref[...]
            o_ref[h] = (n * c_ref[...] + _swap_halves(n) * s_ref[...]).astype(BF16)

    return pl.pallas_call(
        body, name=name, grid=(n_heads // hb, rows // tq),
        in_specs=[_heads_spec(tq, hb, src.shape[-1], first_block), pl.BlockSpec((1, LANES), lambda k, i: (0, 0)),
                  tab, tab],
        out_specs=pl.BlockSpec((hb, tq, LANES), lambda k, i: (k, i, 0)),
        out_shape=jax.ShapeDtypeStruct((n_heads, rows, LANES), BF16), compiler_params=_params(2),
    )(src, g, cos_t, sin_t)


def _qk_prep_bwd(name, dy, src, first_block, hb, n_heads, rows, g, cos_t, sin_t):
    tq = _tile(rows, HEAD_ROW_TILE, SUBLANES)
    tab = pl.BlockSpec((tq, LANES), lambda k, i: (i, 0))

    def body(dy_ref, x_ref, g_ref, c_ref, s_ref, dx_ref, dg_ref):
        g = g_ref[...]
        dg = None
        for h in range(hb):
            x = x_ref[:, h * LANES:(h + 1) * LANES]
            dyv = dy_ref[h]
            rstd = lax.rsqrt(jnp.mean(x * x, axis=-1, keepdims=True) + NORM_EPS)
            xh = x * rstd
            dn = dyv * c_ref[...] + _swap_halves(dyv * s_ref[...])
            dxh = dn * g
            dx = rstd * (dxh - xh * jnp.mean(dxh * xh, axis=-1, keepdims=True))
            dx_ref[:, h * LANES:(h + 1) * LANES] = dx.astype(BF16)
            part = _colsum(dn * xh)
            dg = part if dg is None else dg + part
        first = jnp.logical_and(pl.program_id(0) == 0, pl.program_id(1) == 0)

        @pl.when(first)
        def _():
            dg_ref[...] = dg

        @pl.when(jnp.logical_not(first))
        def _():
            dg_ref[...] += dg

    return pl.pallas_call(
        body, name=name, grid=(n_heads // hb, rows // tq),
        in_specs=[pl.BlockSpec((hb, tq, LANES), lambda k, i: (k, i, 0)),
                  _heads_spec(tq, hb, src.shape[-1], first_block),
                  pl.BlockSpec((1, LANES), lambda k, i: (0, 0)), tab, tab],
        out_specs=[pl.BlockSpec((None, tq, hb * LANES), lambda k, i: (k, i, 0)),
                   pl.BlockSpec((1, LANES), lambda k, i: (0, 0))],
        out_shape=[jax.ShapeDtypeStruct((n_heads // hb, rows, hb * LANES), BF16),
                   jax.ShapeDtypeStruct((1, LANES), F32)],
        compiler_params=_params(2),
    )(dy, src, g, cos_t, sin_t)


def _heads_cast(name, src, first_block, hb, n_heads, rows):
    tq = _tile(rows, HEAD_ROW_TILE, SUBLANES)

    def body(x_ref, o_ref):
        for h in range(hb):
            o_ref[h] = x_ref[:, h * LANES:(h + 1) * LANES].astype(BF16)

    return pl.pallas_call(
        body, name=name, grid=(n_heads // hb, rows // tq),
        in_specs=[_heads_spec(tq, hb, src.shape[-1], first_block)],
        out_specs=pl.BlockSpec((hb, tq, LANES), lambda k, i: (k, i, 0)),
        out_shape=jax.ShapeDtypeStruct((n_heads, rows, LANES), BF16), compiler_params=_params(2),
    )(src)


def _heads_merge(name, src):
    n_heads, rows, _ = src.shape
    tq = _tile(rows, HEAD_ROW_TILE, SUBLANES)

    def body(x_ref, o_ref):
        for h in range(n_heads):
            o_ref[:, h * LANES:(h + 1) * LANES] = x_ref[h].astype(BF16)

    return pl.pallas_call(
        body, name=name, grid=(rows // tq,),
        in_specs=[pl.BlockSpec((n_heads, tq, LANES), lambda i: (0, i, 0))],
        out_specs=pl.BlockSpec((tq, n_heads * LANES), lambda i: (i, 0)),
        out_shape=jax.ShapeDtypeStruct((rows, n_heads * LANES), BF16), compiler_params=_params(1),
    )(src)


def _attn_fwd(q, k, v, q_per_kv, comm=None):
    nq, l, _ = q.shape
    s_len = k.shape[1]
    tq = _tile(l, ROW_TILE, SUBLANES)
    scale = LANES ** -0.5
    kv = pl.BlockSpec((None, s_len, LANES), lambda h, i: (h // q_per_kv, 0, 0))

    def body(q_ref, k_ref, v_ref, o_ref):
        s = lax.dot_general(q_ref[...], k_ref[...], NT, preferred_element_type=F32)
        p = jnp.exp2((s - jnp.max(s, axis=-1, keepdims=True)) * (scale * LOG2E))
        den = jnp.sum(p, axis=-1, keepdims=True)
        o = jnp.dot(p.astype(BF16), v_ref[...], preferred_element_type=F32)
        o_ref[...] = (o * (1.0 / den)).astype(BF16)

    (o,), cres = _host_call(
        body, name="attn_fwd", grid=(nq, l // tq), operands=[q, k, v],
        in_specs=[pl.BlockSpec((None, tq, LANES), lambda h, i: (h, i, 0)), kv, kv],
        out_shape=[jax.ShapeDtypeStruct((l, nq * LANES), BF16)],
        out_specs=[pl.BlockSpec((tq, LANES), lambda h, i: (i, h))], comm=comm)
    return o, cres


def _attn_bwd(q, k, v, do, q_per_kv, comm=None):
    nq, l, _ = q.shape
    nkv, s_len, _ = k.shape
    tq = _tile(l, ROW_TILE, SUBLANES)
    scale = LANES ** -0.5
    hp = ATTN_BWD_HEADS if q_per_kv % ATTN_BWD_HEADS == 0 else 1
    kv = pl.BlockSpec((None, s_len, LANES), lambda g, r, i: (g, 0, 0))
    qs = pl.BlockSpec((hp, tq, LANES), lambda g, r, i: (g * (q_per_kv // hp) + r, i, 0))

    def body(q_ref, k_ref, v_ref, do_ref, dq_ref, dk_ref, dv_ref):
        kvv, vv = k_ref[...], v_ref[...]
        dk_new = dv_new = None
        for h in range(hp):
            qv, dov = q_ref[h], do_ref[:, h * LANES:(h + 1) * LANES]
            st = lax.dot_general(kvv, qv, NT, preferred_element_type=F32)
            e = jnp.exp2((st - jnp.max(st, axis=0, keepdims=True)) * (scale * LOG2E))
            pt = e * (1.0 / jnp.sum(e, axis=0, keepdims=True))
            dpt = lax.dot_general(vv, dov, NT, preferred_element_type=F32)
            delta = jnp.sum(pt * dpt, axis=0, keepdims=True)
            dst = (pt * (dpt - delta)).astype(BF16)
            dq_ref[h] = lax.dot_general(dst, kvv, TN, preferred_element_type=F32) * scale
            dk_h = jnp.dot(dst, qv, preferred_element_type=F32) * scale
            dv_h = jnp.dot(pt.astype(BF16), dov, preferred_element_type=F32)
            dk_new = dk_h if dk_new is None else dk_new + dk_h
            dv_new = dv_h if dv_new is None else dv_new + dv_h
        first = jnp.logical_and(pl.program_id(1) == 0, pl.program_id(2) == 0)

        @pl.when(first)
        def _():
            dk_ref[...] = dk_new
            dv_ref[...] = dv_new

        @pl.when(jnp.logical_not(first))
        def _():
            dk_ref[...] += dk_new
            dv_ref[...] += dv_new

    (dq, dk, dv), cres = _host_call(
        body, name="attn_bwd", grid=(nkv, q_per_kv // hp, l // tq), operands=[q, k, v, do],
        in_specs=[qs, kv, kv, pl.BlockSpec((tq, hp * LANES), lambda g, r, i: (i, g * (q_per_kv // hp) + r))],
        out_specs=[qs, kv, kv],
        out_shape=[jax.ShapeDtypeStruct((nq, l, LANES), F32), jax.ShapeDtypeStruct((nkv, s_len, LANES), F32),
                   jax.ShapeDtypeStruct((nkv, s_len, LANES), F32)], comm=comm)
    return dq, dk, dv, cres


def _zoh(a_re, a_im, log_dt):
    dt = jnp.exp(log_dt)[..., None]
    mag = jnp.exp(a_re * dt)
    lb_re = mag * jnp.cos(a_im * dt)
    lb_im = mag * jnp.sin(a_im * dt)
    den = a_re * a_re + a_im * a_im
    coef_re = ((lb_re - 1.0) * a_re + lb_im * a_im) / den
    coef_im = (lb_im * a_re - (lb_re - 1.0) * a_im) / den
    return lb_re, lb_im, coef_re, coef_im


def _ssm_discretize(a_re, a_im, log_dt, b_re, b_im):
    lb_re, lb_im, cr, ci = _zoh(a_re, a_im, log_dt)
    bt_re = cr[..., None] * b_re - ci[..., None] * b_im
    bt_im = cr[..., None] * b_im + ci[..., None] * b_re
    return lb_re, lb_im, bt_re, bt_im


def _lambda_powers(a_re, a_im, log_dt, ns):
    dt = jnp.exp(log_dt)[..., None]
    k = jnp.arange(SCAN_TAPS + 1, dtype=F32)[:, None, None, None]
    mag, ang = jnp.exp(k * (a_re * dt)), k * (a_im * dt)
    shape = (SCAN_TAPS + 1, 2, ns, -1)
    return (mag * jnp.cos(ang)).reshape(shape), (mag * jnp.sin(ang)).reshape(shape)


def _slab_mask():
    idx = jnp.arange(SLAB_GROUPS)
    return (idx[:, None] == idx[None, :])[None, None, :, None, :, None]


def _block_diag(m):
    d, g, a, b = m.shape
    ns = g // SLAB_GROUPS
    wide = jnp.where(_slab_mask(), m.reshape(d, ns, SLAB_GROUPS, a, 1, b), 0.0)
    return wide.reshape(d, ns, SLAB_GROUPS * a, SLAB_GROUPS * b)


def _block_diag_extract(m, a, b):
    d, ns = m.shape[:2]
    m = m.reshape(d, ns, SLAB_GROUPS, a, SLAB_GROUPS, b)
    return jnp.sum(jnp.where(_slab_mask(), m, 0.0), axis=4).reshape(d, ns * SLAB_GROUPS, a, b)


def _build_tap_weights(w_ref, base_ref, pw_ref, conj, sw):
    b_re, b_im = base_ref[:, :sw], base_ref[:, sw:]
    for tau in range(SCAN_TAPS):
        p_re, p_im = pw_ref[tau:tau + 1, :sw], pw_ref[tau:tau + 1, sw:]
        if conj:
            p_im = -p_im
        w_ref[tau * LANES:(tau + 1) * LANES, :sw] = (p_re * b_re - p_im * b_im).astype(BF16)
        w_ref[tau * LANES:(tau + 1) * LANES, sw:] = (p_re * b_im + p_im * b_re).astype(BF16)


def _carry_tables(pw_re, pw_im, descending):
    def rows(pw):
        asc = pw[1:]
        per_dir = [asc[::-1, d] if descending[d] else asc[:, d] for d in range(2)]
        return jnp.transpose(jnp.stack(per_dir), (0, 2, 1, 3))
    return jnp.concatenate([rows(pw_re), rows(pw_im)], axis=-1)


def _scan_chunk(x, w_ref, tab_ref, s_ref, carry_ref, descending, t_rows, sw):
    row8 = lax.broadcasted_iota(jnp.int32, x.shape, 0) % SCAN_TAPS
    pieces = [x.astype(BF16)]
    for tau in range(1, SCAN_TAPS):
        if descending:
            sh = jnp.where(row8 <= SCAN_TAPS - 1 - tau, pltpu.roll(x, t_rows - tau, 0), 0.0)
        else:
            sh = jnp.where(row8 >= tau, pltpu.roll(x, tau, 0), 0.0)
        pieces.append(sh.astype(BF16))
    xa = jnp.concatenate(pieces, axis=1)
    s_ref[...] = jnp.dot(xa, w_ref[...], preferred_element_type=F32)
    tab = tab_ref[...]
    t_re, t_im = tab[:, :sw], tab[:, sw:]
    nb = t_rows // SCAN_TAPS
    edge = 0 if descending else SCAN_TAPS - 1

    def step(b, carry):
        h_re, h_im = carry
        r0 = pl.multiple_of(((nb - 1 - b) if descending else b) * SCAN_TAPS, SCAN_TAPS)
        x_re = s_ref[pl.ds(r0, SCAN_TAPS), :sw] + t_re * h_re - t_im * h_im
        x_im = s_ref[pl.ds(r0, SCAN_TAPS), sw:] + t_re * h_im + t_im * h_re
        s_ref[pl.ds(r0, SCAN_TAPS), :sw] = x_re
        s_ref[pl.ds(r0, SCAN_TAPS), sw:] = x_im
        return x_re[edge:edge + 1, :], x_im[edge:edge + 1, :]

    h_re, h_im = lax.fori_loop(0, nb, step, (carry_ref[0:1, :sw], carry_ref[0:1, sw:]))
    carry_ref[0:1, :sw] = h_re
    carry_ref[0:1, sw:] = h_im


def _slab_spec(rows, cols, dr):
    return pl.BlockSpec((None, None, rows, cols), lambda s, i: (dr, s, 0, 0))


def _ssm_fwd(name, dr, u_src, u_shard, bd, pw, tab, ct, descending, chunk_of, t_rows, rows, comm=None):
    _, ns, _, sw2 = bd.shape
    sw = sw2 // 2
    width = ns * LANES
    nchunks = rows // t_rows

    def body(u_ref, bd_ref, pw_ref, tab_ref, ct_ref, y_ref, h_ref, s_ref, carry_ref, w_ref):
        @pl.when(pl.program_id(1) == 0)
        def _():
            carry_ref[...] = jnp.zeros_like(carry_ref)
            _build_tap_weights(w_ref, bd_ref, pw_ref, False, sw)

        _scan_chunk(u_ref[...], w_ref, tab_ref, s_ref, carry_ref, descending, t_rows, sw)
        hb = s_ref[...].astype(BF16)
        h_ref[...] = hb
        y_ref[...] = lax.dot_general(hb, ct_ref[...], NT, preferred_element_type=F32)

    (y, h), cres = _host_call(
        body, name=name, grid=(ns, nchunks), operands=[u_src, bd, pw, tab, ct],
        in_specs=[pl.BlockSpec((None, t_rows, LANES), lambda s, i: (u_shard, chunk_of(i), s)),
                  _slab_spec(LANES, sw2, dr), _slab_spec(2 * SCAN_TAPS, sw2, dr), _slab_spec(SCAN_TAPS, sw2, dr),
                  _slab_spec(LANES, sw2, dr)],
        out_specs=[pl.BlockSpec((t_rows, LANES), lambda s, i: (chunk_of(i), s)),
                   pl.BlockSpec((None, t_rows, sw2), lambda s, i: (s, chunk_of(i), 0))],
        out_shape=[jax.ShapeDtypeStruct((rows, width), F32), jax.ShapeDtypeStruct((ns, rows, sw2), BF16)],
        scratch_shapes=[pltpu.VMEM((t_rows, sw2), F32), pltpu.VMEM((SUBLANES, sw2), F32),
                        pltpu.VMEM((SCAN_TAPS * LANES, sw2), BF16)], comm=comm)
    return y, h, cres


def _ssm_bwd(name, dr, dy, u_src, u_shard, states, ct, pw, tab, bd, descending, chunk_of, t_rows, rows, comm=None):
    _, ns, _, sw2 = ct.shape
    sw = sw2 // 2
    width = ns * LANES
    nchunks = rows // t_rows

    def body(dy_ref, u_ref, h_ref, ct_ref, pw_ref, tab_ref, bd_ref, du_ref, dbd_ref, dcd_ref, dlam_ref,
             s_ref, carry_ref, gsave_ref, w_ref):
        first = pl.program_id(1) == 0

        @pl.when(first)
        def _():
            carry_ref[...] = jnp.zeros_like(carry_ref)
            gsave_ref[...] = jnp.zeros_like(gsave_ref)
            _build_tap_weights(w_ref, ct_ref, pw_ref, True, sw)

        dyv = dy_ref[...]
        _scan_chunk(dyv, w_ref, tab_ref, s_ref, carry_ref, descending, t_rows, sw)
        g = s_ref[...]
        gb = g.astype(BF16)
        du_ref[...] = lax.dot_general(gb, bd_ref[...], NT, preferred_element_type=F32)
        dbd = lax.dot_general(u_ref[...].astype(BF16), gb, TN, preferred_element_type=F32)
        hb = h_ref[...]
        dcd = lax.dot_general(hb, dyv.astype(BF16), TN, preferred_element_type=F32)
        hf = hb.astype(F32)
        rowid = lax.broadcasted_iota(jnp.int32, hf.shape, 0)
        if descending:
            hp = jnp.where(rowid == 0, 0.0, pltpu.roll(hf, 1, 0))
            h_edge, g_edge = hf[t_rows - 1:t_rows, :], g[0:1, :]
        else:
            hp = jnp.where(rowid == t_rows - 1, 0.0, pltpu.roll(hf, t_rows - 1, 0))
            h_edge, g_edge = hf[0:1, :], g[t_rows - 1:t_rows, :]
        g_re, g_im, hp_re, hp_im = g[:, :sw], g[:, sw:], hp[:, :sw], hp[:, sw:]
        gs = gsave_ref[0:1, :]
        gs_re, gs_im, he_re, he_im = gs[:, :sw], gs[:, sw:], h_edge[:, :sw], h_edge[:, sw:]
        dl_re = _colsum(g_re * hp_re + g_im * hp_im) + gs_re * he_re + gs_im * he_im
        dl_im = _colsum(g_im * hp_re - g_re * hp_im) + gs_im * he_re - gs_re * he_im
        gsave_ref[0:1, :] = g_edge

        @pl.when(first)
        def _():
            dbd_ref[...] = dbd
            dcd_ref[...] = dcd
            dlam_ref[:, :sw] = dl_re
            dlam_ref[:, sw:] = dl_im

        @pl.when(jnp.logical_not(first))
        def _():
            dbd_ref[...] += dbd
            dcd_ref[...] += dcd
            dlam_ref[:, :sw] += dl_re
            dlam_ref[:, sw:] += dl_im

    (du, dbd, dcd, dlam), cres = _host_call(
        body, name=name, grid=(ns, nchunks), operands=[dy, u_src, states, ct, pw, tab, bd],
        in_specs=[pl.BlockSpec((t_rows, LANES), lambda s, i: (chunk_of(i), s)),
                  pl.BlockSpec((None, t_rows, LANES), lambda s, i: (u_shard, chunk_of(i), s)),
                  pl.BlockSpec((None, t_rows, sw2), lambda s, i: (s, chunk_of(i), 0)),
                  _slab_spec(LANES, sw2, dr), _slab_spec(2 * SCAN_TAPS, sw2, dr), _slab_spec(SCAN_TAPS, sw2, dr),
                  _slab_spec(LANES, sw2, dr)],
        out_specs=[pl.BlockSpec((t_rows, LANES), lambda s, i: (chunk_of(i), s)),
                   pl.BlockSpec((None, LANES, sw2), lambda s, i: (s, 0, 0)),
                   pl.BlockSpec((None, sw2, LANES), lambda s, i: (s, 0, 0)),
                   pl.BlockSpec((None, 1, sw2), lambda s, i: (s, 0, 0))],
        out_shape=[jax.ShapeDtypeStruct((rows, width), F32), jax.ShapeDtypeStruct((ns, LANES, sw2), F32),
                   jax.ShapeDtypeStruct((ns, sw2, LANES), F32), jax.ShapeDtypeStruct((ns, 1, sw2), F32)],
        scratch_shapes=[pltpu.VMEM((t_rows, sw2), F32), pltpu.VMEM((SUBLANES, sw2), F32),
                        pltpu.VMEM((SUBLANES, sw2), F32), pltpu.VMEM((SCAN_TAPS * LANES, sw2), BF16)], comm=comm)
    return du, dbd, dcd, dlam, cres


def _mod_fwd(cs, w_mod, b_cols):
    d, width = w_mod.shape
    tn = _tile(width, 768, LANES)

    def epilogue(accs, ins, outs, pids):
        outs[0][...] = accs[0] + ins[2][...]

    return _matmul(
        "mod_fwd", (width // tn,), [cs, w_mod, b_cols],
        [pl.BlockSpec((16, d), lambda n: (0, 0)), pl.BlockSpec((d, tn), lambda n: (0, n)),
         pl.BlockSpec((1, tn), lambda n: (0, n))],
        [(0, 1, 0, NN)], [jax.ShapeDtypeStruct((16, width), F32)], [pl.BlockSpec((16, tn), lambda n: (0, n))],
        epilogue, prologue={0: lambda v: v * _sigmoid(v)})[0][0]


def _mod_bwd_adam(cs, dmod_cols, w, m, v, comm=None):
    d, width = w.shape
    tn = _tile(width, LANES, LANES)
    col = pl.BlockSpec((d, tn), lambda n: (0, n))

    def body(cs_ref, dm_ref, w_ref, m_ref, v_ref, g_ref, dl_ref, nm_ref, nv_ref, ds_ref):
        n = pl.program_id(0)
        lat = dm_ref[pl.ds(0, N_DEV, stride=SUBLANES), :]
        ctx = jnp.sum(dm_ref[pl.ds(1, N_DEV, stride=SUBLANES), :], axis=0, keepdims=True)
        row = lax.broadcasted_iota(jnp.int32, lat.shape, 0)
        dm = jnp.concatenate([lat, jnp.where(row == 0, ctx, 0.0)], axis=0).astype(BF16)
        c = cs_ref[...]
        sc = (c * _sigmoid(c)).astype(BF16)
        wv = w_ref[...]
        g = lax.dot_general(sc, dm, TN, preferred_element_type=F32)
        delta, m2, v2 = _adamw(wv, g, m_ref[...], v_ref[...])
        g_ref[...] = g
        dl_ref[...] = delta
        nm_ref[...] = m2
        nv_ref[...] = v2
        part = lax.dot_general(dm, wv.astype(BF16), NT, preferred_element_type=F32)

        @pl.when(n == 0)
        def _():
            ds_ref[...] = part

        @pl.when(n > 0)
        def _():
            ds_ref[...] += part

    shard = jax.ShapeDtypeStruct((d, width), F32)
    return _host_call(
        body, name="mod_bwd_adam", grid=(width // tn,), operands=[cs, dmod_cols, w, m, v],
        in_specs=[pl.BlockSpec((16, d), lambda n: (0, 0)), pl.BlockSpec((N_DEV * SUBLANES, tn), lambda n: (0, n)),
                  col, col, col],
        out_specs=[col, col, col, col, pl.BlockSpec((16, d), lambda n: (0, 0))],
        out_shape=[shard, shard, shard, shard, jax.ShapeDtypeStruct((16, d), F32)], comm=comm)


def _pair_sum(name, grads, got, core):
    _, rows, cols = grads.shape
    tr = _tile(rows, max(PACKED_SUBLANES, ADAM_BLOCK_BYTES // (cols * 6 * N_CHIPS)), PACKED_SUBLANES)
    blk = pl.BlockSpec((N_CHIPS, tr, cols), lambda i, cc: (0, i, 0))

    def body(core_ref, a_ref, b_ref, o_ref):
        o_ref[...] = (a_ref[...].astype(F32) + b_ref[...].astype(F32)).astype(BF16)

    grid_spec = pltpu.PrefetchScalarGridSpec(
        num_scalar_prefetch=1, grid=(rows // tr,),
        in_specs=[pl.BlockSpec((N_CHIPS, None, tr, cols), lambda i, cc: (0, cc[0], i, 0)), blk], out_specs=blk)
    return pl.pallas_call(
        body, name=name, grid_spec=grid_spec, out_shape=jax.ShapeDtypeStruct((N_CHIPS, rows, cols), BF16),
        compiler_params=_params(1))(core, grads.reshape(N_CHIPS, 2, rows, cols), got)


def _owner_adam(name, items, chip, comm=None):
    plan, start = [], 0
    per_element = 2 * (2 * N_CHIPS + 7 * 4)
    block_elements = ADAM_GROUP_VMEM // (per_element * len(items))
    for _, _, w, _, _ in items:
        rows, cols = w.shape
        tr = _tile(rows, max(PACKED_SUBLANES, block_elements // cols), PACKED_SUBLANES)
        plan.append((start, rows // tr, tr, cols))
        start += rows // tr
    operands, in_specs, out_specs, out_shape = [], [], [], []
    for (first, nt, tr, cols), (p, l, w, m, v) in zip(plan, items):
        def tile(s, first=first, nt=nt):
            return jnp.clip(s - first, 0, nt - 1)
        blk = pl.BlockSpec((tr, cols), lambda s, ch, tile=tile: (tile(s), 0))
        operands += [p, l, w, m, v]
        in_specs += [pl.BlockSpec((None, tr, cols), lambda s, ch, tile=tile: (ch[0], tile(s), 0)),
                     pl.BlockSpec((N_CHIPS - 1, tr, cols), lambda s, ch, tile=tile: (0, tile(s), 0)), blk, blk, blk]
        out_specs += [blk] * 4
        out_shape += [jax.ShapeDtypeStruct(w.shape, F32)] * 4
    n = len(items)

    def body(chip_ref, *refs):
        s = pl.program_id(0)
        for k, (first, nt, _, _) in enumerate(plan):
            p_ref, l_ref, w_ref, m_ref, v_ref = refs[5 * k:5 * k + 5]
            g_ref, dl_ref, nm_ref, nv_ref = refs[5 * n + 4 * k:5 * n + 4 * k + 4]

            @pl.when(jnp.logical_and(s >= first, s < first + nt))
            def _(p_ref=p_ref, l_ref=l_ref, w_ref=w_ref, m_ref=m_ref, v_ref=v_ref,
                  g_ref=g_ref, dl_ref=dl_ref, nm_ref=nm_ref, nv_ref=nv_ref):
                g = p_ref[...].astype(F32)
                for r in range(N_CHIPS - 1):
                    g = g + l_ref[r].astype(F32)
                delta, m2, v2 = _adamw(w_ref[...], g, m_ref[...], v_ref[...])
                g_ref[...] = g
                dl_ref[...] = delta
                nm_ref[...] = m2
                nv_ref[...] = v2

    res, cres = _host_call(body, name=name, grid=(start,), operands=operands, in_specs=in_specs,
                           out_shape=out_shape, out_specs=out_specs, comm=comm, prefetch=[chip])
    return [res[4 * k:4 * k + 4] for k in range(n)], cres


def _sum_adam(name, parts, w, m, v):
    rows, cols = w.shape
    n_parts = parts.shape[0]
    align = PACKED_SUBLANES if parts.dtype == BF16 else SUBLANES
    tr = _tile(rows, max(align, ADAM_BLOCK_BYTES // (cols * 44)), align)
    blk = pl.BlockSpec((tr, cols), lambda i: (i, 0))

    def body(p_ref, w_ref, m_ref, v_ref, g_ref, dl_ref, nm_ref, nv_ref):
        g = p_ref[0].astype(F32)
        for s in range(1, n_parts):
            g = g + p_ref[s].astype(F32)
        delta, m2, v2 = _adamw(w_ref[...], g, m_ref[...], v_ref[...])
        g_ref[...] = g
        dl_ref[...] = delta
        nm_ref[...] = m2
        nv_ref[...] = v2

    out = jax.ShapeDtypeStruct((rows, cols), F32)
    return pl.pallas_call(
        body, name=name, grid=(rows // tr,),
        in_specs=[pl.BlockSpec((n_parts, tr, cols), lambda i: (0, i, 0)), blk, blk, blk],
        out_specs=[blk, blk, blk, blk], out_shape=[out, out, out, out], compiler_params=_params(1),
    )(parts, w, m, v)


def _bias_adam(dmod_all, w, m, v):
    width = w.shape[-1]
    tn = _tile(width, 2048, LANES)
    blk = pl.BlockSpec((1, tn), lambda n: (0, n))

    def body(p_ref, w_ref, m_ref, v_ref, g_ref, dl_ref, nm_ref, nv_ref):
        g = jnp.sum(p_ref[...], axis=0, keepdims=True)
        delta, m2, v2 = _adamw(w_ref[...], g, m_ref[...], v_ref[...])
        g_ref[...] = g
        dl_ref[...] = delta
        nm_ref[...] = m2
        nv_ref[...] = v2

    out = jax.ShapeDtypeStruct((1, width), F32)
    return pl.pallas_call(
        body, name="bias_adam", grid=(width // tn,),
        in_specs=[pl.BlockSpec((dmod_all.shape[0], tn), lambda n: (0, n)), blk, blk, blk],
        out_specs=[blk, blk, blk, blk], out_shape=[out, out, out, out], compiler_params=_params(1),
    )(dmod_all, w, m, v)


def _pack(arrays, total_rows):
    flat = []
    for a in arrays:
        a = a.reshape(-1).astype(F32)
        flat.append(jnp.pad(a, (0, (-a.shape[0]) % LANES)))
    flat = jnp.concatenate(flat).reshape(-1, LANES)
    return jnp.pad(flat, ((0, total_rows - flat.shape[0]), (0, 0)))


def _unpack(packed, shapes):
    out, row = [], 0
    for shp in shapes:
        size = math.prod(shp)
        nrows = -(-size // LANES)
        out.append(packed[row:row + nrows].reshape(-1)[:size].reshape(shp))
        row += nrows
    return out


def kernel(x, c, ctx, c_ctx, w_mod, b_mod, norm_g, w_ffn1_gate, w_ffn1_up, w_ffn1_down, w_in, q_norm_g, k_norm_g, ssm_a_re, ssm_a_im, ssm_log_dt, ssm_b_re, ssm_b_im, ssm_c_re, ssm_c_im, ssm_d, w_glu, b_glu, w_br_attn, w_br_ssm, w_out, w_ffn2_gate, w_ffn2_up, w_ffn2_down, loss_target, m_c_ctx, m_w_mod, m_b_mod, m_norm_g, m_w_ffn1_gate, m_w_ffn1_up, m_w_ffn1_down, m_w_in, m_q_norm_g, m_k_norm_g, m_ssm_a_re, m_ssm_a_im, m_ssm_log_dt, m_ssm_b_re, m_ssm_b_im, m_ssm_c_re, m_ssm_c_im, m_ssm_d, m_w_glu, m_b_glu, m_w_br_attn, m_w_br_ssm, m_w_out, m_w_ffn2_gate, m_w_ffn2_up, m_w_ffn2_down, v_c_ctx, v_w_mod, v_b_mod, v_norm_g, v_w_ffn1_gate, v_w_ffn1_up, v_w_ffn1_down, v_w_in, v_q_norm_g, v_k_norm_g, v_ssm_a_re, v_ssm_a_im, v_ssm_log_dt, v_ssm_b_re, v_ssm_b_im, v_ssm_c_re, v_ssm_c_im, v_ssm_d, v_w_glu, v_b_glu, v_w_br_attn, v_w_br_ssm, v_w_out, v_w_ffn2_gate, v_w_ffn2_up, v_w_ffn2_down):
    _, L, D = x.shape
    Lc = ctx.shape[1]
    R = L + Lc
    MODW = w_mod.shape[-1]
    INS = w_in.shape[-1]
    KVW = INS // 2
    NQ = D // LANES
    NKV = KVW // LANES
    QPK = NQ // NKV
    HBQ = INS // LANES
    G, P, E = ssm_b_re.shape[2:]
    W = G * E
    SW = SLAB_GROUPS * P
    assert E * SLAB_GROUPS == LANES and W == INS and NQ * LANES == D and Lc <= L
    me = 4 * lax.axis_index("x") + 2 * lax.axis_index("y") + lax.axis_index("c")

    x2, ctx2, tgt = x[0], ctx[0], loss_target[0]
    xc0 = jnp.concatenate([x2, ctx2], axis=0)

    def bf(w):
        return w[0].astype(BF16)

    def held_t(w):
        return jnp.swapaxes(w[0], 0, 1)

    def bft(w):
        return held_t(w).astype(BF16)

    def widen(a):
        return jnp.pad(a[0], ((0, 0), (0, D - a.shape[-1])))

    def at_row(a, r, total):
        return jnp.pad(a, ((r, total - r - a.shape[0]), (0, 0)))

    pack_in = (at_row(c, 0, 16) + at_row(widen(norm_g), 1, 16) + at_row(widen(m_norm_g), 4, 16)
               + at_row(widen(v_norm_g), 7, 16))
    (g_in,) = _exchange_only("ag_inputs", _Gather([pack_in]))
    c_all = g_in[:, 0, :]
    dn = D // N_DEV

    def full_norm(k):
        return jnp.transpose(g_in[:, k:k + 3, :dn], (1, 0, 2)).reshape(3, D)

    ng_full, m_ng_full, v_ng_full = full_norm(1), full_norm(4), full_norm(7)
    cs = at_row(c_all, 0, 16) + at_row(c_ctx[None, :], 8, 16)

    b_cols = lax.dynamic_slice_in_dim(b_mod, me * MODW, MODW, axis=1)
    mod_blk = _mod_fwd(cs, w_mod[0], b_cols)
    (mod_g,) = _exchange_only("ag_mod", _Gather([mod_blk]))
    mod_lat = lax.dynamic_index_in_dim(mod_g, me, axis=1, keepdims=False).reshape(9, D)
    mod_ctx = mod_g[:, 8, :].reshape(9, D)[:5]
    tab2 = jnp.concatenate([mod_lat, mod_ctx, ng_full, jnp.zeros((7, D), F32)], axis=0)
    tab3 = tab2[:, None, :]
    SH1, SC1, G1, SH2, SC2, G2, SH3, SC3, G3, MC0, MC1, MC2, MC3, MC4, GAM1, GAM2, GAM3 = range(17)

    wg1, wu1 = _exchange_only("ag_ffn1_gate_up", _Gather([bft(w_ffn1_gate), bft(w_ffn1_up)]))
    h1 = _norm_mod_fwd("nm1_fwd", xc0, tab3, GAM1, (SH1, MC0), (SC1, MC1), L, Lc)
    a1, b1, s1, (wd1,) = _ffn_up("ffn1", h1, wg1, wu1, comm=_Gather([bf(w_ffn1_down)]))
    f1, xc1, (win,) = _ffn_down("ffn1", s1, wd1, xc0, tab2, (G1, MC2), L, comm=_Gather([bf(w_in)]))

    h2 = _norm_mod_fwd("nm2_fwd", xc1, tab3, GAM2, (SH2, MC3), (SC2, MC4), L, Lc)
    tm = _tile(R, MM_TILE, LANES)
    tml = _tile(L, MM_TILE, LANES)

    (p01,), _ = _matmul(
        "in_proj_kvu", (2, R // tm), [h2, win],
        [pl.BlockSpec((tm, D), lambda j, i: (i, 0)), pl.BlockSpec((None, D, INS), lambda j, i: (j, 0, 0))],
        [(0, 1, 0, NN)], [jax.ShapeDtypeStruct((2, R, INS), F32)],
        [pl.BlockSpec((None, tm, INS), lambda j, i: (j, i, 0))], _store_all)
    (p27,), (wglu, wbra) = _matmul(
        "in_proj_qg", (6, L // tml), [h2, win],
        [pl.BlockSpec((tml, D), lambda j, i: (i, 0)), pl.BlockSpec((None, D, INS), lambda j, i: (j + 2, 0, 0))],
        [(0, 1, 0, NN)], [jax.ShapeDtypeStruct((6, L, INS), F32)],
        [pl.BlockSpec((None, tml, INS), lambda j, i: (j, i, 0))], _store_all,
        comm=_Gather([bf(w_glu), bf(w_br_attn)]))
    wglu2 = wglu.reshape(W, W)
    wbra2 = wbra.reshape(D, D)

    cos_all, sin_all = _rope_tables(L, Lc)
    cos_l, sin_l = cos_all[:L], sin_all[:L]

    q_rot = _qk_prep("q_prep", p27, 0, HBQ, NQ, L, q_norm_g, cos_l, sin_l)
    k_rot = _qk_prep("k_prep", p01, 0, NKV, NKV, R, k_norm_g, cos_all, sin_all)
    v_hd = _heads_cast("v_heads", p01, 1, NKV, NKV, R)
    attn, (wbrs, wout, wg2) = _attn_fwd(
        q_rot, k_rot, v_hd, QPK, comm=_Gather([bf(w_br_ssm), bf(w_out), bft(w_ffn2_gate)]))
    wout2 = wout.reshape(D, D)

    t_rows = _tile(math.gcd(L, Lc), ROW_TILE, SUBLANES)
    nl, ncx = L // t_rows, Lc // t_rows
    nch = nl + ncx
    ns = G // SLAB_GROUPS
    ssm_prim = (ssm_a_re[0], ssm_a_im[0], ssm_log_dt[0], ssm_b_re[0], ssm_b_im[0])
    _, _, bt_re, bt_im = _ssm_discretize(*ssm_prim)
    pw_re, pw_im = _lambda_powers(ssm_a_re[0], ssm_a_im[0], ssm_log_dt[0], ns)
    bd_re = _block_diag(jnp.swapaxes(bt_re, 2, 3))
    bd_im = _block_diag(jnp.swapaxes(bt_im, 2, 3))
    ct_re = _block_diag(ssm_c_re[0])
    ct_im = _block_diag(-ssm_c_im[0])
    fwd_desc = (False, True)
    adj_desc = (True, False)
    s_bd = jnp.concatenate([bd_re, bd_im], axis=-1)
    s_ct = jnp.concatenate([ct_re, ct_im], axis=-1)
    s_bd16, s_ct16 = s_bd.astype(BF16), s_ct.astype(BF16)
    s_pw = jnp.pad(jnp.transpose(jnp.concatenate([pw_re, pw_im], axis=-1), (1, 2, 0, 3)),
                   ((0, 0), (0, 0), (0, 2 * SCAN_TAPS - SCAN_TAPS - 1), (0, 0)))
    s_tab = _carry_tables(pw_re, pw_im, fwd_desc)
    s_tabc = _carry_tables(pw_re, -pw_im, adj_desc)
    order = [lambda i: (i + nl) % nch, lambda i: nch - 1 - i]
    order_adj = [lambda i: (nch - 1 - i + nl) % nch, lambda i: i]
    y0, st0, (wu2,) = _ssm_fwd("ssm_fwd0", 0, p01, 1, s_bd, s_pw, s_tab, s_ct16, fwd_desc[0], order[0], t_rows, R,
                               comm=_Gather([bft(w_ffn2_up)]))
    y1, st1, (wd2,) = _ssm_fwd("ssm_fwd1", 1, p01, 1, s_bd, s_pw, s_tab, s_ct16, fwd_desc[1], order[1], t_rows, R,
                               comm=_Gather([bf(w_ffn2_down)]))
    states = [st0, st1]

    tr = _row_tile(L, 0)
    rowW = pl.BlockSpec((tr, W), lambda i: (i, 0))
    vecW = pl.BlockSpec((1, W), lambda i: (0, 0))
    u_lat = pl.BlockSpec((None, tr, W), lambda i: (1, i, 0))

    def ssm_post(i, u, ya, yb, dvec):
        sv = dvec * u + ya + yb
        return [sv, _gelu(sv)], []

    (ssm_out, yg), _, _ = _rowwise(
        "ssm_post", L // tr, [p01, y0, y1, ssm_d], [u_lat, rowW, rowW, vecW],
        [jax.ShapeDtypeStruct((L, W), F32), jax.ShapeDtypeStruct((L, W), BF16)], [rowW, rowW], [], ssm_post)

    tnw = _tile(W, MM_TILE, LANES)

    def glu_epilogue(accs, ins, outs, pids):
        z = accs[0] + ins[3][...]
        outs[0][...] = z
        outs[1][...] = (_gelu(ins[2][...]) * _sigmoid(z)).astype(BF16)

    (z_glu, y2), _ = _matmul(
        "glu", (L // tml, W // tnw), [yg, wglu2, ssm_out, b_glu],
        [pl.BlockSpec((tml, W), lambda i, n: (i, 0)), pl.BlockSpec((W, tnw), lambda i, n: (0, n)),
         pl.BlockSpec((tml, tnw), lambda i, n: (i, n)), pl.BlockSpec((1, tnw), lambda i, n: (0, n))],
        [(0, 1, 0, NN)], [jax.ShapeDtypeStruct((L, W), F32), jax.ShapeDtypeStruct((L, W), BF16)],
        [pl.BlockSpec((tml, tnw), lambda i, n: (i, n))] * 2, glu_epilogue)

    tnd = _tile(D, MM_TILE, LANES)
    out_ld = pl.BlockSpec((tml, tnd), lambda i, n: (i, n))
    (br_a,), _ = _matmul(
        "br_attn", (L // tml, D // tnd), [attn, wbra2],
        [pl.BlockSpec((tml, D), lambda i, n: (i, 0)), pl.BlockSpec((D, tnd), lambda i, n: (0, n))],
        [(0, 1, 0, NN)], [jax.ShapeDtypeStruct((L, D), F32)], [out_ld], _store_all)

    cb = wbrs.shape[-1]
    gpb = INS // cb

    def gate_spec(first_shard):
        return pl.BlockSpec((None, tml, cb), lambda i, j: (first_shard + j // gpb, i, j % gpb))

    def merge_epilogue(accs, ins, outs, pids):
        br = accs[0]
        outs[0][...] = br
        outs[1][...] = (_sigmoid(ins[2][...]) * ins[4][...] + _sigmoid(ins[3][...]) * br).astype(BF16)

    col_blk = pl.BlockSpec((tml, cb), lambda i, j: (i, j))
    (br_s, merged), _ = _matmul(
        "br_ssm_merge", (L // tml, N_DEV), [y2, wbrs, p27, p27, br_a],
        [pl.BlockSpec((tml, W), lambda i, j: (i, 0)), pl.BlockSpec((None, W, cb), lambda i, j: (j, 0, 0)),
         gate_spec(2), gate_spec(4), col_blk],
        [(0, 1, 0, NN)], [jax.ShapeDtypeStruct((L, D), F32), jax.ShapeDtypeStruct((L, D), BF16)],
        [col_blk, col_blk], merge_epilogue)

    def out_epilogue(accs, ins, outs, pids):
        outs[0][...] = accs[0]
        outs[1][...] = ins[2][...] + ins[3][G2:G2 + 1, :] * accs[0]

    (mix, x2_), _ = _matmul(
        "out_proj", (L // tml, D // tnd), [merged, wout2, xc1, tab2],
        [pl.BlockSpec((tml, D), lambda i, n: (i, 0)), pl.BlockSpec((D, tnd), lambda i, n: (0, n)), out_ld,
         pl.BlockSpec((tab2.shape[0], tnd), lambda i, n: (0, n))],
        [(0, 1, 0, NN)], [jax.ShapeDtypeStruct((L, D), F32)] * 2, [out_ld, out_ld], out_epilogue)

    h3 = _norm_mod_fwd("nm3_fwd", x2_, tab3, GAM3, (SH3, SH3), (SC3, SC3), L, 0)
    a3, b3, s3, _ = _ffn_up("ffn2", h3, wg2, wu2)
    f3, x3, _ = _ffn_down("ffn2", s3, wd2, x2_, tab2, (G3, G3), L)

    trd = _row_tile(L, 0)
    rowD = pl.BlockSpec((trd, D), lambda i: (i, 0))

    def loss_fn(i, yv, t):
        err = yv - t
        return [err * (1.0 / D)], [_colsum(err * err)]

    (dx3,), (sq,), _ = _rowwise("loss", L // trd, [x3, tgt], [rowD, rowD],
                                [jax.ShapeDtypeStruct((L, D), F32)], [rowD], [D], loss_fn)
    loss = lax.psum(0.5 * jnp.sum(sq) / D, ("x", "y", "c"))

    core = lax.axis_index("c").astype(jnp.int32).reshape(1)
    chip = (2 * lax.axis_index("x") + lax.axis_index("y")).astype(jnp.int32).reshape(1)

    def pair_sums(tag, grads, halves):
        return [_pair_sum("pair_%s%d" % (tag, k), g_, h_, core) for k, (g_, h_) in enumerate(zip(grads, halves))]

    df3, (dg3, _) = _gate_bwd("gate3_bwd", dx3, f3, tab3, (G3, G3), 0.5, L, 0)
    dwd2, _ = _ffn_dwd("ffn2b", s3, df3)
    da3, db3, half_wd2 = _ffn_ds("ffn2b", df3, wd2, a3, b3, comm=_SiblingSwap([dwd2]))
    (p_wd2,) = pair_sums("wd2", [dwd2], half_wd2)
    dwg2, dwu2, (l_wd2,) = _ffn_dwgu("ffn2b", h3, da3, db3, comm=_ChipExchange([p_wd2]))
    dh3, half_wgu2 = _ffn_dh("ffn2b", da3, db3, wg2, wu2, comm=_SiblingSwap([dwg2, dwu2]))
    p_wg2, p_wu2 = pair_sums("wgu2", [dwg2, dwu2], half_wgu2)
    dx2, (dsh3, dsc3, _, _, dgam3) = _norm_mod_bwd("nm3_bwd", x2_, dh3, tab3, GAM3, (SC3, SC3), L, 0, dres=dx3)

    dmix, (dg2, _) = _gate_bwd("gate2_bwd", dx2, mix, tab3, (G2, G2), 1.0, L, 0)

    def dmerged_epilogue(accs, ins, outs, pids):
        dm = accs[0]
        ga, gs = _sigmoid(ins[2][...]), _sigmoid(ins[3][...])
        outs[0][...] = (ga * dm).astype(BF16)
        outs[1][...] = (gs * dm).astype(BF16)
        outs[2][...] = (dm * ins[4][...] * ga * (1.0 - ga)).astype(BF16)
        outs[3][...] = (dm * ins[5][...] * gs * (1.0 - gs)).astype(BF16)

    dgate_spec = pl.BlockSpec((None, tml, cb), lambda i, j: (j // gpb, i, j % gpb))
    (d_br_a, d_br_s, dg_a, dg_s), _ = _matmul(
        "dmerged", (L // tml, N_DEV), [dmix, wout2, p27, p27, br_a, br_s],
        [pl.BlockSpec((tml, D), lambda i, j: (i, 0)), pl.BlockSpec((cb, D), lambda i, j: (j, 0)),
         gate_spec(2), gate_spec(4), col_blk, col_blk],
        [(0, 1, 0, NT)],
        [jax.ShapeDtypeStruct((L, D), BF16)] * 2 + [jax.ShapeDtypeStruct((2, L, INS), BF16)] * 2,
        [col_blk, col_blk, dgate_spec, dgate_spec], dmerged_epilogue)

    def wgrad(name, a_mat, b_mat, tmo, tno):
        ka, ma = a_mat.shape
        _, nb_ = b_mat.shape
        return _matmul(
            name, (ma // tmo, nb_ // tno), [a_mat, b_mat],
            [pl.BlockSpec((ka, tmo), lambda m, n: (0, m)), pl.BlockSpec((ka, tno), lambda m, n: (0, n))],
            [(0, 1, 0, TN)], [jax.ShapeDtypeStruct((ma, nb_), BF16)],
            [pl.BlockSpec((tmo, tno), lambda m, n: (m, n))], _store_all)[0][0]

    dwout = wgrad("dw_out", merged, dmix, tnd, tnd)
    dwbra = wgrad("dw_br_attn", attn, d_br_a, tnd, tnd)
    (d_attn,), _ = _matmul(
        "d_attn", (L // tml, D // tnd), [d_br_a, wbra2],
        [pl.BlockSpec((tml, D), lambda i, n: (i, 0)), pl.BlockSpec((tnd, D), lambda i, n: (n, 0))],
        [(0, 1, 0, NT)], [jax.ShapeDtypeStruct((L, D), BF16)], [out_ld], _store_all)

    (dwbrs,), _ = _matmul(
        "dw_br_ssm", (N_DEV,), [y2, d_br_s],
        [pl.BlockSpec((L, W), lambda j: (0, 0)), pl.BlockSpec((L, cb), lambda j: (0, j))],
        [(0, 1, 0, TN)], [jax.ShapeDtypeStruct((N_DEV, W, cb), BF16)],
        [pl.BlockSpec((None, W, cb), lambda j: (j, 0, 0))], _store_all)

    def dy2_epilogue(accs, ins, outs, pids):
        dy2 = accs[0]
        sg = _sigmoid(ins[2][...])
        outs[0][...] = dy2 * sg
        outs[1][...] = (dy2 * _gelu(ins[3][...]) * sg * (1.0 - sg)).astype(BF16)

    wn_blk = pl.BlockSpec((tml, tnw), lambda i, n, k: (i, n))
    (dyg1, dz), _ = _matmul(
        "d_y2", (L // tml, W // tnw, N_DEV), [d_br_s, wbrs, z_glu, ssm_out],
        [pl.BlockSpec((tml, cb), lambda i, n, k: (i, k)), pl.BlockSpec((None, tnw, cb), lambda i, n, k: (k, n, 0)),
         wn_blk, wn_blk],
        [(0, 1, 0, NT)], [jax.ShapeDtypeStruct((L, W), F32), jax.ShapeDtypeStruct((L, W), BF16)],
        [wn_blk, wn_blk], dy2_epilogue, acc_shapes=[(tml, tnw)], nk=N_DEV)

    dwglu = wgrad("dw_glu", yg, dz, tnw, tnw)
    mix_grads = [dwout.reshape(N_DEV, D // N_DEV, D), dwbra.reshape(N_DEV, D // N_DEV, D), dwbrs,
                 dwglu.reshape(N_DEV, W // N_DEV, W)]

    def dssm_epilogue(accs, ins, outs, pids):
        outs[0][...] = (accs[0] + ins[2][...]) * _gelu_grad(ins[3][...])

    wn2 = pl.BlockSpec((tml, tnw), lambda i, n: (i, n))
    (dssm,), _ = _matmul(
        "d_ssm", (L // tml, W // tnw), [dz, wglu2, dyg1, ssm_out],
        [pl.BlockSpec((tml, W), lambda i, n: (i, 0)), pl.BlockSpec((tnw, W), lambda i, n: (n, 0)), wn2, wn2],
        [(0, 1, 0, NT)], [jax.ShapeDtypeStruct((L, W), F32)], [wn2], dssm_epilogue)

    dssm_all = jnp.concatenate([dssm, jnp.zeros((Lc, W), F32)], axis=0)
    du0, dbd0, dcd0, dlam0, (l_wg2, *half_mix) = _ssm_bwd(
        "ssm_bwd0", 0, dssm_all, p01, 1, states[0], s_ct, s_pw, s_tabc, s_bd16, adj_desc[0], order_adj[0], t_rows, R,
        comm=_Both([_ChipExchange([p_wg2]), _SiblingSwap(mix_grads)]))
    p_wout, p_wbra, p_wbrs, p_wglu = pair_sums("mix", mix_grads, half_mix)
    du1, dbd1, dcd1, dlam1, (l_wu2,) = _ssm_bwd(
        "ssm_bwd1", 1, dssm_all, p01, 1, states[1], s_ct, s_pw, s_tabc, s_bd16, adj_desc[1], order_adj[1], t_rows, R,
        comm=_ChipExchange([p_wu2]))

    trr = _row_tile(L, Lc)
    nlt = L // trr
    rowR = pl.BlockSpec((trr, W), lambda i: (i, 0))

    def du_fn(i, dua, dub, dsv, dvec, u):
        lat = (i < nlt).astype(F32)
        return [dua + dub + lat * (dvec * dsv)], [lat * _colsum(dsv * u)]

    (du_all,), (d_ssm_d,), _ = _rowwise(
        "du_combine", R // trr, [du0, du1, dssm_all, ssm_d, p01],
        [rowR, rowR, rowR, pl.BlockSpec((1, W), lambda i: (0, 0)), pl.BlockSpec((None, trr, W), lambda i: (1, i, 0))],
        [jax.ShapeDtypeStruct((R, W), BF16)], [rowR], [W], du_fn)

    def dz_sum(i, dzv):
        return [], [_colsum(dzv.astype(F32))]

    _, (d_b_glu,), _ = _rowwise("db_glu", L // tr, [dz], [rowW], [], [], [W], dz_sum)

    dbd, dcd, dlam = jnp.stack([dbd0, dbd1]), jnp.stack([dcd0, dcd1]), jnp.stack([dlam0, dlam1])
    dbt_re = jnp.swapaxes(_block_diag_extract(dbd[..., :SW], E, P), 2, 3)
    dbt_im = jnp.swapaxes(_block_diag_extract(dbd[..., SW:], E, P), 2, 3)
    dl_re, dl_im = dlam[:, :, 0, :SW].reshape(2, G, P), dlam[:, :, 0, SW:].reshape(2, G, P)
    _, vjp = jax.vjp(_ssm_discretize, *ssm_prim)
    d_a_re, d_a_im, d_ldt, d_b_re, d_b_im = vjp((dl_re, dl_im, dbt_re, dbt_im))
    d_c_re = jnp.swapaxes(_block_diag_extract(dcd[:, :, :SW, :], P, E), 2, 3)
    d_c_im = -jnp.swapaxes(_block_diag_extract(dcd[:, :, SW:, :], P, E), 2, 3)

    early_g = [d_a_re, d_a_im, d_ldt, d_b_re, d_b_im, d_c_re, d_c_im, d_ssm_d, d_b_glu]
    early_w = [ssm_a_re, ssm_a_im, ssm_log_dt, ssm_b_re, ssm_b_im, ssm_c_re, ssm_c_im, ssm_d, b_glu]
    early_m = [m_ssm_a_re, m_ssm_a_im, m_ssm_log_dt, m_ssm_b_re, m_ssm_b_im, m_ssm_c_re, m_ssm_c_im, m_ssm_d, m_b_glu]
    early_v = [v_ssm_a_re, v_ssm_a_im, v_ssm_log_dt, v_ssm_b_re, v_ssm_b_im, v_ssm_c_re, v_ssm_c_im, v_ssm_d, v_b_glu]
    early_shapes = [a.shape for a in early_w]
    early_rows = -(-sum(-(-math.prod(s) // LANES) for s in early_shapes) // 256) * 256

    dq_rot, dk_rot, dv_hd, (l_wout, l_wbra, l_wbrs, l_wglu, early_parts) = _attn_bwd(
        q_rot, k_rot, v_hd, d_attn, QPK,
        comm=_Both([_ChipExchange([p_wout, p_wbra, p_wbrs, p_wglu]), _Gather([_pack(early_g, early_rows)])]))
    dq_pre, d_qg = _qk_prep_bwd("q_prep_bwd", dq_rot, p27, 0, HBQ, NQ, L, q_norm_g, cos_l, sin_l)
    dk_pre, d_kg = _qk_prep_bwd("k_prep_bwd", dk_rot, p01, 0, NKV, NKV, R, k_norm_g, cos_all, sin_all)
    dv_pre = _heads_merge("dv_merge", dv_hd)

    def lat_blocks(a):
        return jnp.pad(a, ((0, 0), (0, Lc), (0, 0)))

    dp = jnp.concatenate([
        jnp.concatenate([dk_pre[0], dv_pre], axis=1)[None], du_all[None],
        lat_blocks(dq_pre), lat_blocks(dg_a), lat_blocks(dg_s)], axis=0)

    tmo = _tile(D, MM_TILE, LANES)
    (dwin,), _ = _matmul(
        "dw_in", (N_DEV, D // tmo), [h2, dp],
        [pl.BlockSpec((R, tmo), lambda j, m: (0, m)), pl.BlockSpec((None, R, INS), lambda j, m: (j, 0, 0))],
        [(0, 1, 0, TN)], [jax.ShapeDtypeStruct((N_DEV, D, INS), BF16)],
        [pl.BlockSpec((None, tmo, INS), lambda j, m: (j, m, 0))], _store_all)
    tnh = _tile(D, MM_TILE_NT, LANES)
    (dh2,), half_win = _matmul(
        "d_h2", (R // tm, D // tnh), [dp, win],
        [pl.BlockSpec((N_DEV, tm, INS), lambda i, n: (0, i, 0)),
         pl.BlockSpec((N_DEV, tnh, INS), lambda i, n: (0, n, 0))],
        [(0, 1, 0, NT, N_DEV)], [jax.ShapeDtypeStruct((R, D), F32)], [pl.BlockSpec((tm, tnh), lambda i, n: (i, n))],
        _store_all, comm=_SiblingSwap([dwin]))
    (p_win,) = pair_sums("win", [dwin], half_win)
    dxc1, (dsh2, dsc2, dmc3, dmc4, dgam2) = _norm_mod_bwd(
        "nm2_bwd", xc1, dh2, tab3, GAM2, (SC2, MC4), L, Lc, dres=dx2)

    df1, (dg1, dmc2) = _gate_bwd("gate1_bwd", dxc1, f1, tab3, (G1, MC2), 0.5, L, Lc)
    dwd1, _ = _ffn_dwd("ffn1b", s1, df1)
    da1, db1, (l_win, *half_wd1) = _ffn_ds(
        "ffn1b", df1, wd1, a1, b1, comm=_Both([_ChipExchange([p_win]), _SiblingSwap([dwd1])]))
    (p_wd1,) = pair_sums("wd1", [dwd1], half_wd1)
    dwg1, dwu1, (l_wd1,) = _ffn_dwgu("ffn1b", h1, da1, db1, comm=_ChipExchange([p_wd1]))
    dh1, half_wgu1 = _ffn_dh("ffn1b", da1, db1, wg1, wu1, comm=_SiblingSwap([dwg1, dwu1]))
    p_wg1, p_wu1 = pair_sums("wgu1", [dwg1, dwu1], half_wgu1)

    def adam_item(p, l_, w_, m_, v_):
        return (p, l_, w_[0], m_[0], v_[0])

    def adam_item_t(p, l_, w_, m_, v_):
        return (p, l_, held_t(w_), held_t(m_), held_t(v_))

    ready_a = [adam_item(p_wd1, l_wd1, w_ffn1_down, m_w_ffn1_down, v_w_ffn1_down),
               adam_item(p_win, l_win, w_in, m_w_in, v_w_in),
               adam_item(p_wglu, l_wglu, w_glu, m_w_glu, v_w_glu),
               adam_item(p_wbra, l_wbra, w_br_attn, m_w_br_attn, v_w_br_attn),
               adam_item(p_wbrs, l_wbrs, w_br_ssm, m_w_br_ssm, v_w_br_ssm)]
    ready_b = [adam_item(p_wout, l_wout, w_out, m_w_out, v_w_out),
               adam_item_t(p_wg2, l_wg2, w_ffn2_gate, m_w_ffn2_gate, v_w_ffn2_gate),
               adam_item_t(p_wu2, l_wu2, w_ffn2_up, m_w_ffn2_up, v_w_ffn2_up),
               adam_item(p_wd2, l_wd2, w_ffn2_down, m_w_ffn2_down, v_w_ffn2_down)]
    adam_a, (l_wg1,) = _owner_adam("adam_ready_a", ready_a, chip, comm=_ChipExchange([p_wg1]))
    adam_b, (l_wu1,) = _owner_adam("adam_ready_b", ready_b, chip, comm=_ChipExchange([p_wu1]))
    adam_ready = adam_a + adam_b
    dxc0, (dsh1, dsc1, dmc0, dmc1, dgam1) = _norm_mod_bwd(
        "nm1_bwd", xc0, dh1, tab3, GAM1, (SC1, MC1), L, Lc, dres=dxc1)
    grad_x = dxc0[:L][None]

    dmod_lat = jnp.concatenate([dsh1, dsc1, dg1, dsh2, dsc2, dg2, dsh3, dsc3, dg3], axis=1)
    dmod_ctx = jnp.concatenate([dmc0, dmc1, dmc2, dmc3, dmc4, jnp.zeros((1, 4 * D), F32)], axis=1)
    dmod_pack = at_row(dmod_lat, 0, SUBLANES) + at_row(dmod_ctx, 1, SUBLANES)
    (dmod_g,) = _exchange_only("ag_dmod", _Gather([dmod_pack]))
    dmod_all = dmod_g.reshape(N_DEV * SUBLANES, 9 * D)
    dmod_cols = lax.dynamic_slice_in_dim(dmod_all, me * MODW, MODW, axis=1)
    (g_wmod, dl_wmod, nm_wmod, nv_wmod, dsilu), _ = _mod_bwd_adam(
        cs, dmod_cols, w_mod[0], m_w_mod[0], v_w_mod[0])
    sg_cc = jax.nn.sigmoid(c_ctx)
    d_c_ctx = dsilu[8] * (sg_cc * (1.0 + c_ctx * (1.0 - sg_cc)))
    g_bmod, dl_bmod, nm_bmod, nv_bmod = _bias_adam(dmod_all, b_mod, m_b_mod, v_b_mod)

    dgam_all = jnp.concatenate([dgam1, dgam2, dgam3], axis=0)
    late_g = [d_c_ctx, d_qg, d_kg, dgam_all]
    late_w = [c_ctx, q_norm_g, k_norm_g, ng_full]
    late_m = [m_c_ctx, m_q_norm_g, m_k_norm_g, m_ng_full]
    late_v = [v_c_ctx, v_q_norm_g, v_k_norm_g, v_ng_full]
    late_shapes = [a.shape for a in late_w]
    late_rows = -(-sum(-(-math.prod(s) // LANES) for s in late_shapes) // SUBLANES) * SUBLANES
    (late_parts,) = _exchange_only("ag_small_grads", _Gather([_pack(late_g, late_rows)]))
    late_out = _sum_adam("small_adam_late", late_parts, _pack(late_w, late_rows), _pack(late_m, late_rows),
                         _pack(late_v, late_rows))
    early_out = _sum_adam("small_adam_s5", early_parts, _pack(early_w, early_rows), _pack(early_m, early_rows),
                          _pack(early_v, early_rows))

    def my_norm_cols(a):
        return lax.dynamic_slice_in_dim(a, me * dn, dn, axis=1)[None]

    small = []
    for lo, eo in zip(late_out, early_out):
        c_ctx_, qg_, kg_, ng_ = _unpack(lo, late_shapes)
        small.append([c_ctx_, qg_, kg_] + _unpack(eo, early_shapes) + [my_norm_cols(ng_)])
    sm_g, sm_dl, sm_m, sm_v = small

    adam_last, _ = _owner_adam(
        "adam_last", [adam_item_t(p_wg1, l_wg1, w_ffn1_gate, m_w_ffn1_gate, v_w_ffn1_gate),
                      adam_item_t(p_wu1, l_wu1, w_ffn1_up, m_w_ffn1_up, v_w_ffn1_up)], chip)
    transposed = (0, 1, 8, 9)
    big_out = [[(jnp.swapaxes(o, 0, 1) if k in transposed else o)[None] for o in grp_]
               for k, grp_ in enumerate(adam_last + adam_ready)]

    def leaf(kind):
        sm = (sm_g, sm_dl, sm_m, sm_v)[kind]
        mod = (g_wmod, dl_wmod, nm_wmod, nv_wmod)[kind][None]
        bmod = (g_bmod, dl_bmod, nm_bmod, nv_bmod)[kind]
        big = [b[kind] for b in big_out]
        (c_ctx_, qg_, kg_, a_re_, a_im_, ldt_, b_re_, b_im_, c_re_, c_im_, sd_, bglu_, ng_) = sm
        return [c_ctx_, mod, bmod, ng_, big[0], big[1], big[2], big[3], qg_, kg_, a_re_, a_im_, ldt_, b_re_, b_im_,
                c_re_, c_im_, sd_, big[4], bglu_, big[5], big[6], big[7], big[8], big[9], big[10]]

    return tuple([loss, grad_x] + leaf(0) + leaf(1) + leaf(2) + leaf(3))
```

```python
import math

import jax
import jax.numpy as jnp
import numpy as np
from jax import lax
from jax.experimental import pallas as pl
from jax.experimental.pallas import tpu as pltpu

F32 = jnp.float32
BF16 = jnp.bfloat16

N_DEV = 8
N_CHIPS = 4
LANES = 128
SUBLANES = 8
PACKED_SUBLANES = 16
VMEM_LIMIT = 56 * 1024 * 1024
MM_TILE = 512
MM_TILE_NT = 256
ROW_TILE = 256
HEAD_ROW_TILE = 512
ATTN_BWD_HEADS = 2
ADAM_BLOCK_BYTES = 4 * 1024 * 1024
ADAM_GROUP_VMEM = 36 * 1024 * 1024

NORM_EPS = 1e-6
GRID_W = 64
ROPE_THETA = 10000.0
SCAN_TAPS = SUBLANES
SLAB_GROUPS = 8

ADAM_LR = 0.001
ADAM_B1 = 0.9
ADAM_B2 = 0.999
ADAM_EPS = 1e-08
ADAM_WD = 0.01
ADAM_STEP = 10

NN = (((1,), (0,)), ((), ()))
NT = (((1,), (1,)), ((), ()))
TN = (((0,), (0,)), ((), ()))

MESH = pl.DeviceIdType.MESH
ANY = pl.BlockSpec(memory_space=pl.ANY)


def _tile(n, cap, align):
    best = None
    for t in range(align, min(n, cap) + 1, align):
        if n % t == 0:
            best = t
    return n if best is None else best


def _params(n_grid):
    return pltpu.CompilerParams(dimension_semantics=("arbitrary",) * n_grid, vmem_limit_bytes=VMEM_LIMIT)


def _sigmoid(x):
    return 1.0 / (1.0 + jnp.exp(-x))


LOG2E = math.log2(math.e)
GELU_K = math.sqrt(2.0 / math.pi)
GELU_C = 0.044715


def _gelu(x):
    return 0.5 * x * (1.0 + jnp.tanh(GELU_K * (x + GELU_C * x * x * x)))


def _gelu_grad(x):
    t = jnp.tanh(GELU_K * (x + GELU_C * x * x * x))
    return 0.5 * (1.0 + t) + 0.5 * x * (1.0 - t * t) * GELU_K * (1.0 + 3.0 * GELU_C * x * x)


def _adamw(w, g, m, v):
    m2 = ADAM_B1 * m + (1.0 - ADAM_B1) * g
    v2 = ADAM_B2 * v + (1.0 - ADAM_B2) * (g * g)
    m_hat = m2 / (1.0 - ADAM_B1 ** ADAM_STEP)
    v_hat = v2 / (1.0 - ADAM_B2 ** ADAM_STEP)
    delta = -ADAM_LR * (m_hat / (jnp.sqrt(v_hat) + ADAM_EPS) + ADAM_WD * w)
    return delta, m2, v2


def _position():
    return lax.axis_index("x"), lax.axis_index("y"), lax.axis_index("c")


class _Gather:
    def __init__(self, arrays):
        self.arrays = list(arrays)
        n = len(self.arrays)
        self.out_shapes = [jax.ShapeDtypeStruct((N_DEV,) + a.shape, a.dtype) for a in self.arrays]
        self.scratch = [pltpu.SemaphoreType.DMA((n, 7)), pltpu.SemaphoreType.DMA((n, 7)),
                        pltpu.SemaphoreType.DMA((n,))]

    def _plan(self, ins, outs, sems):
        send, recv, local = sems
        x, y, c = _position()
        me, sibling = (x, y, c), (x, y, 1 - c)
        chips = [(1 - x, y), (x, 1 - y), (1 - x, 1 - y)]

        def slot(a, p):
            return outs[a].at[4 * p[0] + 2 * p[1] + p[2]]

        def copy(a, k, block, to, src=None):
            dst = slot(a, block)
            return pltpu.make_async_remote_copy(
                src_ref=dst if src is None else src, dst_ref=dst,
                send_sem=send.at[a, k], recv_sem=recv.at[a, k], device_id=to, device_id_type=MESH)

        mine = [pltpu.make_async_copy(ins[a], slot(a, me), local.at[a]) for a in range(len(ins))]
        return me, sibling, chips, c, copy, mine

    def start(self, ins, outs, sems):
        me, sibling, chips, c, copy, mine = self._plan(ins, outs, sems)
        for cp in mine:
            cp.start()
        for a in range(len(ins)):
            copy(a, 0, me, sibling, src=ins[a]).start()
            for j, chip in enumerate(chips):
                copy(a, 1 + j, me, (*chip, c), src=ins[a]).start()

    def finish(self, ins, outs, sems):
        me, sibling, chips, c, copy, mine = self._plan(ins, outs, sems)
        n = len(ins)
        for j, chip in enumerate(chips):
            for a in range(n):
                copy(a, 1 + j, (*chip, c), me).wait_recv()
                copy(a, 4 + j, (*chip, c), sibling).start()
        for a in range(n):
            copy(a, 0, sibling, me).wait_recv()
        for j, chip in enumerate(chips):
            for a in range(n):
                copy(a, 4 + j, (*chip, 1 - c), me).wait_recv()
        for a in range(n):
            copy(a, 0, me, sibling, src=ins[a]).wait_send()
            for j, chip in enumerate(chips):
                copy(a, 1 + j, me, (*chip, c), src=ins[a]).wait_send()
                copy(a, 4 + j, (*chip, c), sibling).wait_send()
        for cp in mine:
            cp.wait()


class _SiblingSwap:
    def __init__(self, arrays):
        self.arrays = list(arrays)
        n = len(self.arrays)
        self.out_shapes = [jax.ShapeDtypeStruct((N_CHIPS,) + a.shape[1:], a.dtype) for a in self.arrays]
        self.scratch = [pltpu.SemaphoreType.DMA((n, N_CHIPS)), pltpu.SemaphoreType.DMA((n, N_CHIPS))]

    def _plan(self, ins, outs, sems):
        send, recv = sems
        x, y, c = _position()
        return [pltpu.make_async_remote_copy(
            src_ref=ins[a].at[2 * j + 1 - c], dst_ref=outs[a].at[j],
            send_sem=send.at[a, j], recv_sem=recv.at[a, j], device_id=(x, y, 1 - c), device_id_type=MESH)
            for a in range(len(ins)) for j in range(N_CHIPS)]

    def start(self, ins, outs, sems):
        for cp in self._plan(ins, outs, sems):
            cp.start()

    def finish(self, ins, outs, sems):
        copies = self._plan(ins, outs, sems)
        for cp in copies:
            cp.wait_recv()
        for cp in copies:
            cp.wait_send()


class _ChipExchange:
    def __init__(self, arrays):
        self.arrays = list(arrays)
        n = len(self.arrays)
        self.out_shapes = [jax.ShapeDtypeStruct((N_CHIPS - 1,) + a.shape[1:], a.dtype) for a in self.arrays]
        self.scratch = [pltpu.SemaphoreType.DMA((n, N_CHIPS - 1)), pltpu.SemaphoreType.DMA((n, N_CHIPS - 1))]

    def _plan(self, ins, outs, sems):
        send, recv = sems
        x, y, c = _position()
        copies = []
        for r in range(1, N_CHIPS):
            px, py = x ^ (r >> 1), y ^ (r & 1)
            for a in range(len(ins)):
                copies.append(pltpu.make_async_remote_copy(
                    src_ref=ins[a].at[2 * px + py], dst_ref=outs[a].at[r - 1],
                    send_sem=send.at[a, r - 1], recv_sem=recv.at[a, r - 1],
                    device_id=(px, py, c), device_id_type=MESH))
        return copies

    def start(self, ins, outs, sems):
        for cp in self._plan(ins, outs, sems):
            cp.start()

    def finish(self, ins, outs, sems):
        copies = self._plan(ins, outs, sems)
        for cp in copies:
            cp.wait_recv()
        for cp in copies:
            cp.wait_send()


class _Both:
    def __init__(self, comms):
        self.comms = list(comms)
        self.arrays = [a for cm in self.comms for a in cm.arrays]
        self.out_shapes = [s for cm in self.comms for s in cm.out_shapes]
        self.scratch = [s for cm in self.comms for s in cm.scratch]

    def _split(self, ins, outs, sems):
        i = o = s = 0
        for cm in self.comms:
            ni, no, nsem = len(cm.arrays), len(cm.out_shapes), len(cm.scratch)
            yield cm, ins[i:i + ni], outs[o:o + no], sems[s:s + nsem]
            i, o, s = i + ni, o + no, s + nsem

    def start(self, ins, outs, sems):
        for cm, i, o, s in self._split(ins, outs, sems):
            cm.start(i, o, s)

    def finish(self, ins, outs, sems):
        for cm, i, o, s in self._split(ins, outs, sems):
            cm.finish(i, o, s)


def _host_call(body, *, name, grid, operands, in_specs, out_shape, out_specs, scratch_shapes=(), comm=None,
               prefetch=()):
    grid = tuple(grid)
    n_pre, n_in, n_out, n_scr = len(prefetch), len(operands), len(out_shape), len(scratch_shapes)
    nc_in, nc_out = (len(comm.arrays), len(comm.out_shapes)) if comm else (0, 0)
    all_in = list(in_specs) + [ANY] * nc_in
    all_out = list(out_specs) + [ANY] * nc_out
    all_scr = list(scratch_shapes) + (list(comm.scratch) if comm else [])
    all_shape = list(out_shape) + (list(comm.out_shapes) if comm else [])
    kwargs = dict(name=name, compiler_params=_params(len(grid)), out_shape=all_shape)
    if n_pre:
        kwargs["grid_spec"] = pltpu.PrefetchScalarGridSpec(
            num_scalar_prefetch=n_pre, grid=grid, in_specs=all_in, out_specs=all_out, scratch_shapes=all_scr)
    else:
        kwargs.update(in_specs=all_in, out_specs=all_out, scratch_shapes=all_scr)
        if grid:
            kwargs["grid"] = grid
    args = list(prefetch) + list(operands) + (list(comm.arrays) if comm else [])
    if comm is None:
        return list(pl.pallas_call(body, **kwargs)(*args)), []

    def hosted(*refs):
        bounds = [0, n_pre, n_pre + n_in]
        for n in (nc_in, n_out, nc_out, n_scr):
            bounds.append(bounds[-1] + n)
        bounds.append(len(refs))
        pre, ins, cins, outs, couts, scr, sems = [refs[a:b] for a, b in zip(bounds[:-1], bounds[1:])]
        if not grid:
            comm.start(cins, couts, sems)
            body(*pre, *ins, *outs, *scr)
            comm.finish(cins, couts, sems)
            return
        first, last = None, None
        for ax, size in enumerate(grid):
            pid = pl.program_id(ax)
            f, l = pid == 0, pid == size - 1
            first = f if first is None else jnp.logical_and(first, f)
            last = l if last is None else jnp.logical_and(last, l)

        @pl.when(first)
        def _():
            comm.start(cins, couts, sems)

        body(*pre, *ins, *outs, *scr)

        @pl.when(last)
        def _():
            comm.finish(cins, couts, sems)

    res = pl.pallas_call(hosted, **kwargs)(*args)
    return list(res[:n_out]), list(res[n_out:])


def _exchange_only(name, comm):
    def body():
        pass
    return _host_call(body, name=name, grid=(), operands=[], in_specs=[], out_shape=[], out_specs=[], comm=comm)[1]


def _matmul(name, grid, operands, in_specs, pairs, out_shapes, out_specs, epilogue, acc_shapes=(), nk=1,
            prologue=None, comm=None):
    n_in, n_out = len(operands), len(out_shapes)
    prologue = prologue or {}

    def body(*refs):
        ins, outs, accs = refs[:n_in], refs[n_in:n_in + n_out], refs[n_in + n_out:]
        pids = [pl.program_id(ax) for ax in range(len(grid))]

        def operand(i, blk=None):
            v = ins[i][...] if blk is None else ins[i][blk]
            if i in prologue:
                v = prologue[i](v)
            return v.astype(BF16)

        def products():
            vals = {}
            for pair in pairs:
                ai, bi, ci, dn = pair[:4]
                if len(pair) == 5:
                    p = None
                    for blk in range(pair[4]):
                        q = lax.dot_general(operand(ai, blk), operand(bi, blk), dn, preferred_element_type=F32)
                        p = q if p is None else p + q
                else:
                    p = lax.dot_general(operand(ai), operand(bi), dn, preferred_element_type=F32)
                vals[ci] = p if ci not in vals else vals[ci] + p
            return [vals[ci] for ci in sorted(vals)]

        if nk == 1:
            epilogue(products(), ins, outs, pids)
        else:
            k = pids[-1]
            prods = products()

            @pl.when(k == 0)
            def _():
                for acc, p in zip(accs, prods):
                    acc[...] = p

            @pl.when(k > 0)
            def _():
                for acc, p in zip(accs, prods):
                    acc[...] += p

            @pl.when(k == nk - 1)
            def _():
                epilogue([acc[...] for acc in accs], ins, outs, pids)

    return _host_call(
        body, name=name, grid=grid, operands=operands, in_specs=in_specs, out_shape=out_shapes, out_specs=out_specs,
        scratch_shapes=[pltpu.VMEM(s, F32) for s in acc_shapes] if nk > 1 else [], comm=comm)


def _rowwise(name, n_tiles, operands, in_specs, out_shapes, out_specs, red_widths, fn, comm=None):
    n_in, n_out, n_red = len(operands), len(out_shapes), len(red_widths)

    def body(*refs):
        ins, outs, reds = refs[:n_in], refs[n_in:n_in + n_out], refs[n_in + n_out:]
        i = pl.program_id(0)
        vals, sums = fn(i, *[r[...] for r in ins])
        for o, v in zip(outs, vals):
            o[...] = v.astype(o.dtype)
        if n_red:
            @pl.when(i == 0)
            def _():
                for r, s in zip(reds, sums):
                    r[...] = s

            @pl.when(i > 0)
            def _():
                for r, s in zip(reds, sums):
                    r[...] += s

    red_shapes = [jax.ShapeDtypeStruct((1, w), F32) for w in red_widths]
    red_specs = [pl.BlockSpec((1, w), lambda i: (0, 0)) for w in red_widths]
    res, cres = _host_call(
        body, name=name, grid=(n_tiles,), operands=operands, in_specs=in_specs,
        out_shape=list(out_shapes) + red_shapes, out_specs=list(out_specs) + red_specs, comm=comm)
    return res[:n_out], res[n_out:], cres


def _colsum(v):
    return jnp.sum(v, axis=0, keepdims=True)


def _store_all(accs, ins, outs, pids):
    for o, v in zip(outs, accs):
        o[...] = v.astype(o.dtype)


def _row_tile(rows_a, rows_b):
    return _tile(math.gcd(rows_a, rows_b) if rows_b else rows_a, ROW_TILE, SUBLANES)


def _tab_row(d, nlt, rows2):
    return pl.BlockSpec((None, 1, d), lambda i: (jnp.where(i < nlt, rows2[0], rows2[1]), 0, 0))


def _norm_mod_fwd(name, xs, tab, r_gamma, r_shift, r_scale, n_lat, n_ctx):
    rows, d = xs.shape
    tm = _row_tile(n_lat, n_ctx)
    nlt = n_lat // tm

    def fn(i, x, g, sh, sc):
        xh = x * lax.rsqrt(jnp.mean(x * x, axis=-1, keepdims=True) + NORM_EPS)
        return [(xh * g) * (1.0 + sc) + sh], []

    (h,), _, _ = _rowwise(
        name, rows // tm, [xs, tab, tab, tab],
        [pl.BlockSpec((tm, d), lambda i: (i, 0)), _tab_row(d, nlt, (r_gamma, r_gamma)), _tab_row(d, nlt, r_shift),
         _tab_row(d, nlt, r_scale)],
        [jax.ShapeDtypeStruct((rows, d), BF16)], [pl.BlockSpec((tm, d), lambda i: (i, 0))], [], fn)
    return h


def _norm_mod_bwd(name, xs, dh, tab, r_gamma, r_scale, n_lat, n_ctx, dres=None):
    rows, d = xs.shape
    tm = _row_tile(n_lat, n_ctx)
    nlt = n_lat // tm
    row = pl.BlockSpec((tm, d), lambda i: (i, 0))

    def fn(i, x, dy, g, sc, *res):
        rstd = lax.rsqrt(jnp.mean(x * x, axis=-1, keepdims=True) + NORM_EPS)
        xh = x * rstd
        dsh = _colsum(dy)
        dsc = _colsum(dy * (xh * g))
        dn = dy * (1.0 + sc)
        dgam = _colsum(dn * xh)
        dxh = dn * g
        dx = rstd * (dxh - xh * jnp.mean(dxh * xh, axis=-1, keepdims=True))
        if res:
            dx = dx + jnp.where(i < nlt, res[0], 0.0)
        lat = (i < nlt).astype(F32)
        return [dx], [dsh * lat, dsc * lat, dsh * (1.0 - lat), dsc * (1.0 - lat), dgam]

    operands = [xs, dh, tab, tab]
    specs = [row, row, _tab_row(d, nlt, (r_gamma, r_gamma)), _tab_row(d, nlt, r_scale)]
    if dres is not None:
        operands.append(dres)
        specs.append(pl.BlockSpec((tm, d), lambda i: (jnp.minimum(i, nlt - 1), 0)))
    (dx,), sums, _ = _rowwise(name, rows // tm, operands, specs,
                              [jax.ShapeDtypeStruct((rows, d), F32)], [row], [d] * 5, fn)
    return dx, sums


def _gate_bwd(name, dx, f, tab, r_gate, coef, n_lat, n_ctx):
    rows, d = dx.shape
    tm = _row_tile(n_lat, n_ctx)
    nlt = n_lat // tm
    row = pl.BlockSpec((tm, d), lambda i: (i, 0))

    def fn(i, dxv, fv, gv):
        dg = _colsum(dxv * fv) * coef
        lat = (i < nlt).astype(F32)
        return [(coef * gv) * dxv], [dg * lat, dg * (1.0 - lat)]

    (df,), sums, _ = _rowwise(
        name, rows // tm, [dx, f, tab],
        [row, row, _tab_row(d, nlt, r_gate)],
        [jax.ShapeDtypeStruct((rows, d), BF16)], [row], [d, d], fn)
    return df, sums


def _select_rows(i, tm, n_lat, v_lat, v_ctx):
    rows = i * tm + lax.broadcasted_iota(jnp.int32, (tm, 1), 0)
    return jnp.where(rows < n_lat, v_lat, v_ctx)


def _ffn_up(tag, h, wg, wu, comm=None):
    rows, d = h.shape
    nb, fs, _ = wg.shape
    tm = _tile(rows, MM_TILE, LANES)
    blk = pl.BlockSpec((None, tm, fs), lambda j, i: (j, i, 0))
    wspec = pl.BlockSpec((None, fs, d), lambda j, i: (j, 0, 0))

    def epilogue(accs, ins, outs, pids):
        a, b = accs
        outs[0][...] = a.astype(BF16)
        outs[1][...] = b.astype(BF16)
        outs[2][...] = (a * _sigmoid(a) * b).astype(BF16)

    hid = jax.ShapeDtypeStruct((nb, rows, fs), BF16)
    (a, b, s), cres = _matmul(
        tag + "_up", (nb, rows // tm), [h, wg, wu],
        [pl.BlockSpec((tm, d), lambda j, i: (i, 0)), wspec, wspec],
        [(0, 1, 0, NT), (0, 2, 1, NT)], [hid, hid, hid], [blk, blk, blk], epilogue, comm=comm)
    return a, b, s, cres


def _ffn_down(tag, s, wd, xs, tab2, r_gate, n_lat, comm=None):
    nb, rows, fs = s.shape
    d = wd.shape[-1]
    tm = _tile(rows, MM_TILE, LANES)
    tn = _tile(d, MM_TILE, LANES)

    def epilogue(accs, ins, outs, pids):
        f = accs[0]
        g = ins[3][...]
        gate = _select_rows(pids[0], tm, n_lat, g[r_gate[0]:r_gate[0] + 1, :], g[r_gate[1]:r_gate[1] + 1, :])
        outs[0][...] = f
        outs[1][...] = ins[2][...] + 0.5 * gate * f

    out = jax.ShapeDtypeStruct((rows, d), F32)
    ospec = pl.BlockSpec((tm, tn), lambda i, n: (i, n))
    (f, xo), cres = _matmul(
        tag + "_down", (rows // tm, d // tn), [s, wd, xs, tab2],
        [pl.BlockSpec((nb, tm, fs), lambda i, n: (0, i, 0)), pl.BlockSpec((nb, fs, tn), lambda i, n: (0, 0, n)),
         ospec, pl.BlockSpec((tab2.shape[0], tn), lambda i, n: (0, n))],
        [(0, 1, 0, NN, nb)], [out, out], [ospec, ospec], epilogue, comm=comm)
    return f, xo, cres


def _ffn_ds(tag, df, wd, a, b, comm=None):
    rows, d = df.shape
    nb, fs, _ = wd.shape
    tm = _tile(rows, MM_TILE, LANES)
    blk = pl.BlockSpec((None, tm, fs), lambda j, i: (j, i, 0))

    def epilogue(accs, ins, outs, pids):
        ds = accs[0]
        av = ins[2][...].astype(F32)
        bv = ins[3][...].astype(F32)
        sg = _sigmoid(av)
        outs[0][...] = (ds * bv * (sg * (1.0 + av * (1.0 - sg)))).astype(BF16)
        outs[1][...] = (ds * (av * sg)).astype(BF16)

    hid = jax.ShapeDtypeStruct((nb, rows, fs), BF16)
    (da, db), cres = _matmul(
        tag + "_ds", (nb, rows // tm), [df, wd, a, b],
        [pl.BlockSpec((tm, d), lambda j, i: (i, 0)), pl.BlockSpec((None, fs, d), lambda j, i: (j, 0, 0)), blk, blk],
        [(0, 1, 0, NT)], [hid, hid], [blk, blk], epilogue, comm=comm)
    return da, db, cres


def _ffn_dwd(tag, s, df, comm=None):
    nb, rows, fs = s.shape
    d = df.shape[-1]
    tn = _tile(d, MM_TILE, LANES)
    (dwd,), cres = _matmul(
        tag + "_dwd", (nb, d // tn), [s, df],
        [pl.BlockSpec((None, rows, fs), lambda j, n: (j, 0, 0)), pl.BlockSpec((rows, tn), lambda j, n: (0, n))],
        [(0, 1, 0, TN)], [jax.ShapeDtypeStruct((nb, fs, d), BF16)],
        [pl.BlockSpec((None, fs, tn), lambda j, n: (j, 0, n))], _store_all, comm=comm)
    return dwd, cres


def _ffn_dwgu(tag, h, da, db, comm=None):
    rows, d = h.shape
    nb, _, fs = da.shape
    tno = _tile(d, MM_TILE, LANES)
    full = pl.BlockSpec((None, rows, fs), lambda j, m: (j, 0, 0))
    wshape = jax.ShapeDtypeStruct((nb, fs, d), BF16)
    wblk = pl.BlockSpec((None, fs, tno), lambda j, m: (j, 0, m))
    (dwg, dwu), cres = _matmul(
        tag + "_dwgu", (nb, d // tno), [h, da, db],
        [pl.BlockSpec((rows, tno), lambda j, m: (0, m)), full, full],
        [(1, 0, 0, TN), (2, 0, 1, TN)], [wshape, wshape], [wblk, wblk], _store_all, comm=comm)
    return dwg, dwu, cres


def _ffn_dh(tag, da, db, wg, wu, comm=None):
    nb, rows, fs = da.shape
    d = wg.shape[2]
    tm = _tile(rows, MM_TILE, LANES)
    tn = _tile(d, MM_TILE_NT, LANES)
    aspec = pl.BlockSpec((nb, tm, fs), lambda i, n: (0, i, 0))
    wspec = pl.BlockSpec((nb, fs, tn), lambda i, n: (0, 0, n))
    (dh,), cres = _matmul(
        tag + "_dh", (rows // tm, d // tn), [da, wg, db, wu], [aspec, wspec, aspec, wspec],
        [(0, 1, 0, NN, nb), (2, 3, 0, NN, nb)], [jax.ShapeDtypeStruct((rows, d), F32)],
        [pl.BlockSpec((tm, tn), lambda i, n: (i, n))], _store_all, comm=comm)
    return dh, cres


def _rope_tables(n_lat, n_ctx):
    half = LANES // 4
    inv_freq = (np.float32(ROPE_THETA) ** (-np.arange(half, dtype=np.float32) / np.float32(half))).astype(np.float32)
    pos = np.arange(n_lat)
    ang_r = (pos // GRID_W).astype(np.float32)[:, None] * inv_freq
    ang_c = (pos % GRID_W).astype(np.float32)[:, None] * inv_freq
    cos_l = np.concatenate([np.cos(ang_r)] * 2 + [np.cos(ang_c)] * 2, axis=1)
    sin_l = np.concatenate([-np.sin(ang_r), np.sin(ang_r), -np.sin(ang_c), np.sin(ang_c)], axis=1)
    cos_all = np.concatenate([cos_l, np.ones((n_ctx, LANES), np.float32)], axis=0).astype(np.float32)
    sin_all = np.concatenate([sin_l, np.zeros((n_ctx, LANES), np.float32)], axis=0).astype(np.float32)
    return jnp.asarray(cos_all), jnp.asarray(sin_all)


def _swap_halves(x):
    lane = lax.broadcasted_iota(jnp.int32, x.shape, 1)
    return jnp.where((lane % 64) < 32, pltpu.roll(x, 96, 1), pltpu.roll(x, 32, 1))


def _heads_spec(tq, hb, width, first_block):
    per_shard = width // (hb * LANES)

    def index(k, i):
        blk = first_block + k
        return blk // per_shard, i, blk % per_shard
    return pl.BlockSpec((None, tq, hb * LANES), index)


def _qk_prep(name, src, first_block, hb, n_heads, rows, g, cos_t, sin_t):
    tq = _tile(rows, HEAD_ROW_TILE, SUBLANES)
    tab = pl.BlockSpec((tq, LANES), lambda k, i: (i, 0))

    def body(x_ref, g_ref, c_ref, s_ref, o_ref):
        for h in range(hb):
            x = x_ref[:, h * LANES:(h + 1) * LANES]
            n = x * lax.rsqrt(jnp.mean(x * x, axis=-1, keepdims=True) + NORM_EPS) * g_ref[...]
            o_ref[h] = (n * c_ref[...] + _swap_halves(n) * s_ref[...]).astype(BF16)

    return pl.pallas_call(
        body, name=name, grid=(n_heads // hb, rows // tq),
        in_specs=[_heads_spec(tq, hb, src.shape[-1], first_block), pl.BlockSpec((1, LANES), lambda k, i: (0, 0)),
                  tab, tab],
        out_specs=pl.BlockSpec((hb, tq, LANES), lambda k, i: (k, i, 0)),
        out_shape=jax.ShapeDtypeStruct((n_heads, rows, LANES), BF16), compiler_params=_params(2),
    )(src, g, cos_t, sin_t)


def _qk_prep_bwd(name, dy, src, first_block, hb, n_heads, rows, g, cos_t, sin_t):
    tq = _tile(rows, HEAD_ROW_TILE, SUBLANES)
    tab = pl.BlockSpec((tq, LANES), lambda k, i: (i, 0))

    def body(dy_ref, x_ref, g_ref, c_ref, s_ref, dx_ref, dg_ref):
        g = g_ref[...]
        dg = None
        for h in range(hb):
            x = x_ref[:, h * LANES:(h + 1) * LANES]
            dyv = dy_ref[h]
            rstd = lax.rsqrt(jnp.mean(x * x, axis=-1, keepdims=True) + NORM_EPS)
            xh = x * rstd
            dn = dyv * c_ref[...] + _swap_halves(dyv * s_ref[...])
            dxh = dn * g
            dx = rstd * (dxh - xh * jnp.mean(dxh * xh, axis=-1, keepdims=True))
            dx_ref[:, h * LANES:(h + 1) * LANES] = dx.astype(BF16)
            part = _colsum(dn * xh)
            dg = part if dg is None else dg + part
        first = jnp.logical_and(pl.program_id(0) == 0, pl.program_id(1) == 0)

        @pl.when(first)
        def _():
            dg_ref[...] = dg

        @pl.when(jnp.logical_not(first))
        def _():
            dg_ref[...] += dg

    return pl.pallas_call(
        body, name=name, grid=(n_heads // hb, rows // tq),
        in_specs=[pl.BlockSpec((hb, tq, LANES), lambda k, i: (k, i, 0)),
                  _heads_spec(tq, hb, src.shape[-1], first_block),
                  pl.BlockSpec((1, LANES), lambda k, i: (0, 0)), tab, tab],
        out_specs=[pl.BlockSpec((None, tq, hb * LANES), lambda k, i: (k, i, 0)),
                   pl.BlockSpec((1, LANES), lambda k, i: (0, 0))],
        out_shape=[jax.ShapeDtypeStruct((n_heads // hb, rows, hb * LANES), BF16),
                   jax.ShapeDtypeStruct((1, LANES), F32)],
        compiler_params=_params(2),
    )(dy, src, g, cos_t, sin_t)


def _heads_cast(name, src, first_block, hb, n_heads, rows):
    tq = _tile(rows, HEAD_ROW_TILE, SUBLANES)

    def body(x_ref, o_ref):
        for h in range(hb):
            o_ref[h] = x_ref[:, h * LANES:(h + 1) * LANES].astype(BF16)

    return pl.pallas_call(
        body, name=name, grid=(n_heads // hb, rows // tq),
        in_specs=[_heads_spec(tq, hb, src.shape[-1], first_block)],
        out_specs=pl.BlockSpec((hb, tq, LANES), lambda k, i: (k, i, 0)),
        out_shape=jax.ShapeDtypeStruct((n_heads, rows, LANES), BF16), compiler_params=_params(2),
    )(src)


def _heads_merge(name, src):
    n_heads, rows, _ = src.shape
    tq = _tile(rows, HEAD_ROW_TILE, SUBLANES)

    def body(x_ref, o_ref):
        for h in range(n_heads):
            o_ref[:, h * LANES:(h + 1) * LANES] = x_ref[h].astype(BF16)

    return pl.pallas_call(
        body, name=name, grid=(rows // tq,),
        in_specs=[pl.BlockSpec((n_heads, tq, LANES), lambda i: (0, i, 0))],
        out_specs=pl.BlockSpec((tq, n_heads * LANES), lambda i: (i, 0)),
        out_shape=jax.ShapeDtypeStruct((rows, n_heads * LANES), BF16), compiler_params=_params(1),
    )(src)


def _attn_fwd(q, k, v, q_per_kv, comm=None):
    nq, l, _ = q.shape
    s_len = k.shape[1]
    tq = _tile(l, ROW_TILE, SUBLANES)
    scale = LANES ** -0.5
    kv = pl.BlockSpec((None, s_len, LANES), lambda h, i: (h // q_per_kv, 0, 0))

    def body(q_ref, k_ref, v_ref, o_ref):
        s = lax.dot_general(q_ref[...], k_ref[...], NT, preferred_element_type=F32)
        p = jnp.exp2((s - jnp.max(s, axis=-1, keepdims=True)) * (scale * LOG2E))
        den = jnp.sum(p, axis=-1, keepdims=True)
        o = jnp.dot(p.astype(BF16), v_ref[...], preferred_element_type=F32)
        o_ref[...] = (o * (1.0 / den)).astype(BF16)

    (o,), cres = _host_call(
        body, name="attn_fwd", grid=(nq, l // tq), operands=[q, k, v],
        in_specs=[pl.BlockSpec((None, tq, LANES), lambda h, i: (h, i, 0)), kv, kv],
        out_shape=[jax.ShapeDtypeStruct((l, nq * LANES), BF16)],
        out_specs=[pl.BlockSpec((tq, LANES), lambda h, i: (i, h))], comm=comm)
    return o, cres


def _attn_bwd(q, k, v, do, q_per_kv, comm=None):
    nq, l, _ = q.shape
    nkv, s_len, _ = k.shape
    tq = _tile(l, ROW_TILE, SUBLANES)
    scale = LANES ** -0.5
    hp = ATTN_BWD_HEADS if q_per_kv % ATTN_BWD_HEADS == 0 else 1
    kv = pl.BlockSpec((None, s_len, LANES), lambda g, r, i: (g, 0, 0))
    qs = pl.BlockSpec((hp, tq, LANES), lambda g, r, i: (g * (q_per_kv // hp) + r, i, 0))

    def body(q_ref, k_ref, v_ref, do_ref, dq_ref, dk_ref, dv_ref):
        kvv, vv = k_ref[...], v_ref[...]
        dk_new = dv_new = None
        for h in range(hp):
            qv, dov = q_ref[h], do_ref[:, h * LANES:(h + 1) * LANES]
            st = lax.dot_general(kvv, qv, NT, preferred_element_type=F32)
            e = jnp.exp2((st - jnp.max(st, axis=0, keepdims=True)) * (scale * LOG2E))
            pt = e * (1.0 / jnp.sum(e, axis=0, keepdims=True))
            dpt = lax.dot_general(vv, dov, NT, preferred_element_type=F32)
            delta = jnp.sum(pt * dpt, axis=0, keepdims=True)
            dst = (pt * (dpt - delta)).astype(BF16)
            dq_ref[h] = lax.dot_general(dst, kvv, TN, preferred_element_type=F32) * scale
            dk_h = jnp.dot(dst, qv, preferred_element_type=F32) * scale
            dv_h = jnp.dot(pt.astype(BF16), dov, preferred_element_type=F32)
            dk_new = dk_h if dk_new is None else dk_new + dk_h
            dv_new = dv_h if dv_new is None else dv_new + dv_h
        first = jnp.logical_and(pl.program_id(1) == 0, pl.program_id(2) == 0)

        @pl.when(first)
        def _():
            dk_ref[...] = dk_new
            dv_ref[...] = dv_new

        @pl.when(jnp.logical_not(first))
        def _():
            dk_ref[...] += dk_new
            dv_ref[...] += dv_new

    (dq, dk, dv), cres = _host_call(
        body, name="attn_bwd", grid=(nkv, q_per_kv // hp, l // tq), operands=[q, k, v, do],
        in_specs=[qs, kv, kv, pl.BlockSpec((tq, hp * LANES), lambda g, r, i: (i, g * (q_per_kv // hp) + r))],
        out_specs=[qs, kv, kv],
        out_shape=[jax.ShapeDtypeStruct((nq, l, LANES), F32), jax.ShapeDtypeStruct((nkv, s_len, LANES), F32),
                   jax.ShapeDtypeStruct((nkv, s_len, LANES), F32)], comm=comm)
    return dq, dk, dv, cres


def _zoh(a_re, a_im, log_dt):
    dt = jnp.exp(log_dt)[..., None]
    mag = jnp.exp(a_re * dt)
    lb_re = mag * jnp.cos(a_im * dt)
    lb_im = mag * jnp.sin(a_im * dt)
    den = a_re * a_re + a_im * a_im
    coef_re = ((lb_re - 1.0) * a_re + lb_im * a_im) / den
    coef_im = (lb_im * a_re - (lb_re - 1.0) * a_im) / den
    return lb_re, lb_im, coef_re, coef_im


def _ssm_discretize(a_re, a_im, log_dt, b_re, b_im):
    lb_re, lb_im, cr, ci = _zoh(a_re, a_im, log_dt)
    bt_re = cr[..., None] * b_re - ci[..., None] * b_im
    bt_im = cr[..., None] * b_im + ci[..., None] * b_re
    return lb_re, lb_im, bt_re, bt_im


def _lambda_powers(a_re, a_im, log_dt, ns):
    dt = jnp.exp(log_dt)[..., None]
    k = jnp.arange(SCAN_TAPS + 1, dtype=F32)[:, None, None, None]
    mag, ang = jnp.exp(k * (a_re * dt)), k * (a_im * dt)
    shape = (SCAN_TAPS + 1, 2, ns, -1)
    return (mag * jnp.cos(ang)).reshape(shape), (mag * jnp.sin(ang)).reshape(shape)


def _slab_mask():
    idx = jnp.arange(SLAB_GROUPS)
    return (idx[:, None] == idx[None, :])[None, None, :, None, :, None]


def _block_diag(m):
    d, g, a, b = m.shape
    ns = g // SLAB_GROUPS
    wide = jnp.where(_slab_mask(), m.reshape(d, ns, SLAB_GROUPS, a, 1, b), 0.0)
    return wide.reshape(d, ns, SLAB_GROUPS * a, SLAB_GROUPS * b)


def _block_diag_extract(m, a, b):
    d, ns = m.shape[:2]
    m = m.reshape(d, ns, SLAB_GROUPS, a, SLAB_GROUPS, b)
    return jnp.sum(jnp.where(_slab_mask(), m, 0.0), axis=4).reshape(d, ns * SLAB_GROUPS, a, b)


def _build_tap_weights(w_ref, base_ref, pw_ref, conj, sw):
    b_re, b_im = base_ref[:, :sw], base_ref[:, sw:]
    for tau in range(SCAN_TAPS):
        p_re, p_im = pw_ref[tau:tau + 1, :sw], pw_ref[tau:tau + 1, sw:]
        if conj:
            p_im = -p_im
        w_ref[tau * LANES:(tau + 1) * LANES, :sw] = (p_re * b_re - p_im * b_im).astype(BF16)
        w_ref[tau * LANES:(tau + 1) * LANES, sw:] = (p_re * b_im + p_im * b_re).astype(BF16)


def _carry_tables(pw_re, pw_im, descending):
    def rows(pw):
        asc = pw[1:]
        per_dir = [asc[::-1, d] if descending[d] else asc[:, d] for d in range(2)]
        return jnp.transpose(jnp.stack(per_dir), (0, 2, 1, 3))
    return jnp.concatenate([rows(pw_re), rows(pw_im)], axis=-1)


def _scan_chunk(x, w_ref, tab_ref, s_ref, carry_ref, descending, t_rows, sw):
    row8 = lax.broadcasted_iota(jnp.int32, x.shape, 0) % SCAN_TAPS
    pieces = [x.astype(BF16)]
    for tau in range(1, SCAN_TAPS):
        if descending:
            sh = jnp.where(row8 <= SCAN_TAPS - 1 - tau, pltpu.roll(x, t_rows - tau, 0), 0.0)
        else:
            sh = jnp.where(row8 >= tau, pltpu.roll(x, tau, 0), 0.0)
        pieces.append(sh.astype(BF16))
    xa = jnp.concatenate(pieces, axis=1)
    s_ref[...] = jnp.dot(xa, w_ref[...], preferred_element_type=F32)
    tab = tab_ref[...]
    t_re, t_im = tab[:, :sw], tab[:, sw:]
    nb = t_rows // SCAN_TAPS
    edge = 0 if descending else SCAN_TAPS - 1

    def step(b, carry):
        h_re, h_im = carry
        r0 = pl.multiple_of(((nb - 1 - b) if descending else b) * SCAN_TAPS, SCAN_TAPS)
        x_re = s_ref[pl.ds(r0, SCAN_TAPS), :sw] + t_re * h_re - t_im * h_im
        x_im = s_ref[pl.ds(r0, SCAN_TAPS), sw:] + t_re * h_im + t_im * h_re
        s_ref[pl.ds(r0, SCAN_TAPS), :sw] = x_re
        s_ref[pl.ds(r0, SCAN_TAPS), sw:] = x_im
        return x_re[edge:edge + 1, :], x_im[edge:edge + 1, :]

    h_re, h_im = lax.fori_loop(0, nb, step, (carry_ref[0:1, :sw], carry_ref[0:1, sw:]))
    carry_ref[0:1, :sw] = h_re
    carry_ref[0:1, sw:] = h_im


def _slab_spec(rows, cols, dr):
    return pl.BlockSpec((None, None, rows, cols), lambda s, i: (dr, s, 0, 0))


def _ssm_fwd(name, dr, u_src, u_shard, bd, pw, tab, ct, descending, chunk_of, t_rows, rows, comm=None):
    _, ns, _, sw2 = bd.shape
    sw = sw2 // 2
    width = ns * LANES
    nchunks = rows // t_rows

    def body(u_ref, bd_ref, pw_ref, tab_ref, ct_ref, y_ref, h_ref, s_ref, carry_ref, w_ref):
        @pl.when(pl.program_id(1) == 0)
        def _():
            carry_ref[...] = jnp.zeros_like(carry_ref)
            _build_tap_weights(w_ref, bd_ref, pw_ref, False, sw)

        _scan_chunk(u_ref[...], w_ref, tab_ref, s_ref, carry_ref, descending, t_rows, sw)
        hb = s_ref[...].astype(BF16)
        h_ref[...] = hb
        y_ref[...] = lax.dot_general(hb, ct_ref[...], NT, preferred_element_type=F32)

    (y, h), cres = _host_call(
        body, name=name, grid=(ns, nchunks), operands=[u_src, bd, pw, tab, ct],
        in_specs=[pl.BlockSpec((None, t_rows, LANES), lambda s, i: (u_shard, chunk_of(i), s)),
                  _slab_spec(LANES, sw2, dr), _slab_spec(2 * SCAN_TAPS, sw2, dr), _slab_spec(SCAN_TAPS, sw2, dr),
                  _slab_spec(LANES, sw2, dr)],
        out_specs=[pl.BlockSpec((t_rows, LANES), lambda s, i: (chunk_of(i), s)),
                   pl.BlockSpec((None, t_rows, sw2), lambda s, i: (s, chunk_of(i), 0))],
        out_shape=[jax.ShapeDtypeStruct((rows, width), F32), jax.ShapeDtypeStruct((ns, rows, sw2), BF16)],
        scratch_shapes=[pltpu.VMEM((t_rows, sw2), F32), pltpu.VMEM((SUBLANES, sw2), F32),
                        pltpu.VMEM((SCAN_TAPS * LANES, sw2), BF16)], comm=comm)
    return y, h, cres


def _ssm_bwd(name, dr, dy, u_src, u_shard, states, ct, pw, tab, bd, descending, chunk_of, t_rows, rows, comm=None):
    _, ns, _, sw2 = ct.shape
    sw = sw2 // 2
    width = ns * LANES
    nchunks = rows // t_rows

    def body(dy_ref, u_ref, h_ref, ct_ref, pw_ref, tab_ref, bd_ref, du_ref, dbd_ref, dcd_ref, dlam_ref,
             s_ref, carry_ref, gsave_ref, w_ref):
        first = pl.program_id(1) == 0

        @pl.when(first)
        def _():
            carry_ref[...] = jnp.zeros_like(carry_ref)
            gsave_ref[...] = jnp.zeros_like(gsave_ref)
            _build_tap_weights(w_ref, ct_ref, pw_ref, True, sw)

        dyv = dy_ref[...]
        _scan_chunk(dyv, w_ref, tab_ref, s_ref, carry_ref, descending, t_rows, sw)
        g = s_ref[...]
        gb = g.astype(BF16)
        du_ref[...] = lax.dot_general(gb, bd_ref[...], NT, preferred_element_type=F32)
        dbd = lax.dot_general(u_ref[...].astype(BF16), gb, TN, preferred_element_type=F32)
        hb = h_ref[...]
        dcd = lax.dot_general(hb, dyv.astype(BF16), TN, preferred_element_type=F32)
        hf = hb.astype(F32)
        rowid = lax.broadcasted_iota(jnp.int32, hf.shape, 0)
        if descending:
            hp = jnp.where(rowid == 0, 0.0, pltpu.roll(hf, 1, 0))
            h_edge, g_edge = hf[t_rows - 1:t_rows, :], g[0:1, :]
        else:
            hp = jnp.where(rowid == t_rows - 1, 0.0, pltpu.roll(hf, t_rows - 1, 0))
            h_edge, g_edge = hf[0:1, :], g[t_rows - 1:t_rows, :]
        g_re, g_im, hp_re, hp_im = g[:, :sw], g[:, sw:], hp[:, :sw], hp[:, sw:]
        gs = gsave_ref[0:1, :]
        gs_re, gs_im, he_re, he_im = gs[:, :sw], gs[:, sw:], h_edge[:, :sw], h_edge[:, sw:]
        dl_re = _colsum(g_re * hp_re + g_im * hp_im) + gs_re * he_re + gs_im * he_im
        dl_im = _colsum(g_im * hp_re - g_re * hp_im) + gs_im * he_re - gs_re * he_im
        gsave_ref[0:1, :] = g_edge

        @pl.when(first)
        def _():
            dbd_ref[...] = dbd
            dcd_ref[...] = dcd
            dlam_ref[:, :sw] = dl_re
            dlam_ref[:, sw:] = dl_im

        @pl.when(jnp.logical_not(first))
        def _():
            dbd_ref[...] += dbd
            dcd_ref[...] += dcd
            dlam_ref[:, :sw] += dl_re
            dlam_ref[:, sw:] += dl_im

    (du, dbd, dcd, dlam), cres = _host_call(
        body, name=name, grid=(ns, nchunks), operands=[dy, u_src, states, ct, pw, tab, bd],
        in_specs=[pl.BlockSpec((t_rows, LANES), lambda s, i: (chunk_of(i), s)),
                  pl.BlockSpec((None, t_rows, LANES), lambda s, i: (u_shard, chunk_of(i), s)),
                  pl.BlockSpec((None, t_rows, sw2), lambda s, i: (s, chunk_of(i), 0)),
                  _slab_spec(LANES, sw2, dr), _slab_spec(2 * SCAN_TAPS, sw2, dr), _slab_spec(SCAN_TAPS, sw2, dr),
                  _slab_spec(LANES, sw2, dr)],
        out_specs=[pl.BlockSpec((t_rows, LANES), lambda s, i: (chunk_of(i), s)),
                   pl.BlockSpec((None, LANES, sw2), lambda s, i: (s, 0, 0)),
                   pl.BlockSpec((None, sw2, LANES), lambda s, i: (s, 0, 0)),
                   pl.BlockSpec((None, 1, sw2), lambda s, i: (s, 0, 0))],
        out_shape=[jax.ShapeDtypeStruct((rows, width), F32), jax.ShapeDtypeStruct((ns, LANES, sw2), F32),
                   jax.ShapeDtypeStruct((ns, sw2, LANES), F32), jax.ShapeDtypeStruct((ns, 1, sw2), F32)],
        scratch_shapes=[pltpu.VMEM((t_rows, sw2), F32), pltpu.VMEM((SUBLANES, sw2), F32),
                        pltpu.VMEM((SUBLANES, sw2), F32), pltpu.VMEM((SCAN_TAPS * LANES, sw2), BF16)], comm=comm)
    return du, dbd, dcd, dlam, cres


def _mod_fwd(cs, w_mod, b_cols):
    d, width = w_mod.shape
    tn = _tile(width, 768, LANES)

    def epilogue(accs, ins, outs, pids):
        outs[0][...] = accs[0] + ins[2][...]

    return _matmul(
        "mod_fwd", (width // tn,), [cs, w_mod, b_cols],
        [pl.BlockSpec((16, d), lambda n: (0, 0)), pl.BlockSpec((d, tn), lambda n: (0, n)),
         pl.BlockSpec((1, tn), lambda n: (0, n))],
        [(0, 1, 0, NN)], [jax.ShapeDtypeStruct((16, width), F32)], [pl.BlockSpec((16, tn), lambda n: (0, n))],
        epilogue, prologue={0: lambda v: v * _sigmoid(v)})[0][0]


def _mod_bwd_adam(cs, dmod_cols, w, m, v, comm=None):
    d, width = w.shape
    tn = _tile(width, LANES, LANES)
    col = pl.BlockSpec((d, tn), lambda n: (0, n))

    def body(cs_ref, dm_ref, w_ref, m_ref, v_ref, g_ref, dl_ref, nm_ref, nv_ref, ds_ref):
        n = pl.program_id(0)
        lat = dm_ref[pl.ds(0, N_DEV, stride=SUBLANES), :]
        ctx = jnp.sum(dm_ref[pl.ds(1, N_DEV, stride=SUBLANES), :], axis=0, keepdims=True)
        row = lax.broadcasted_iota(jnp.int32, lat.shape, 0)
        dm = jnp.concatenate([lat, jnp.where(row == 0, ctx, 0.0)], axis=0).astype(BF16)
        c = cs_ref[...]
        sc = (c * _sigmoid(c)).astype(BF16)
        wv = w_ref[...]
        g = lax.dot_general(sc, dm, TN, preferred_element_type=F32)
        delta, m2, v2 = _adamw(wv, g, m_ref[...], v_ref[...])
        g_ref[...] = g
        dl_ref[...] = delta
        nm_ref[...] = m2
        nv_ref[...] = v2
        part = lax.dot_general(dm, wv.astype(BF16), NT, preferred_element_type=F32)

        @pl.when(n == 0)
        def _():
            ds_ref[...] = part

        @pl.when(n > 0)
        def _():
            ds_ref[...] += part

    shard = jax.ShapeDtypeStruct((d, width), F32)
    return _host_call(
        body, name="mod_bwd_adam", grid=(width // tn,), operands=[cs, dmod_cols, w, m, v],
        in_specs=[pl.BlockSpec((16, d), lambda n: (0, 0)), pl.BlockSpec((N_DEV * SUBLANES, tn), lambda n: (0, n)),
                  col, col, col],
        out_specs=[col, col, col, col, pl.BlockSpec((16, d), lambda n: (0, 0))],
        out_shape=[shard, shard, shard, shard, jax.ShapeDtypeStruct((16, d), F32)], comm=comm)


def _pair_sum(name, grads, got, core):
    _, rows, cols = grads.shape
    tr = _tile(rows, max(PACKED_SUBLANES, ADAM_BLOCK_BYTES // (cols * 6 * N_CHIPS)), PACKED_SUBLANES)
    blk = pl.BlockSpec((N_CHIPS, tr, cols), lambda i, cc: (0, i, 0))

    def body(core_ref, a_ref, b_ref, o_ref):
        o_ref[...] = (a_ref[...].astype(F32) + b_ref[...].astype(F32)).astype(BF16)

    grid_spec = pltpu.PrefetchScalarGridSpec(
        num_scalar_prefetch=1, grid=(rows // tr,),
        in_specs=[pl.BlockSpec((N_CHIPS, None, tr, cols), lambda i, cc: (0, cc[0], i, 0)), blk], out_specs=blk)
    return pl.pallas_call(
        body, name=name, grid_spec=grid_spec, out_shape=jax.ShapeDtypeStruct((N_CHIPS, rows, cols), BF16),
        compiler_params=_params(1))(core, grads.reshape(N_CHIPS, 2, rows, cols), got)


def _owner_adam(name, items, chip, comm=None):
    plan, start = [], 0
    per_element = 2 * (2 * N_CHIPS + 7 * 4)
    block_elements = ADAM_GROUP_VMEM // (per_element * len(items))
    for _, _, w, _, _ in items:
        rows, cols = w.shape
        tr = _tile(rows, max(PACKED_SUBLANES, block_elements // cols), PACKED_SUBLANES)
        plan.append((start, rows // tr, tr, cols))
        start += rows // tr
    operands, in_specs, out_specs, out_shape = [], [], [], []
    for (first, nt, tr, cols), (p, l, w, m, v) in zip(plan, items):
        def tile(s, first=first, nt=nt):
            return jnp.clip(s - first, 0, nt - 1)
        blk = pl.BlockSpec((tr, cols), lambda s, ch, tile=tile: (tile(s), 0))
        operands += [p, l, w, m, v]
        in_specs += [pl.BlockSpec((None, tr, cols), lambda s, ch, tile=tile: (ch[0], tile(s), 0)),
                     pl.BlockSpec((N_CHIPS - 1, tr, cols), lambda s, ch, tile=tile: (0, tile(s), 0)), blk, blk, blk]
        out_specs += [blk] * 4
        out_shape += [jax.ShapeDtypeStruct(w.shape, F32)] * 4
    n = len(items)

    def body(chip_ref, *refs):
        s = pl.program_id(0)
        for k, (first, nt, _, _) in enumerate(plan):
            p_ref, l_ref, w_ref, m_ref, v_ref = refs[5 * k:5 * k + 5]
            g_ref, dl_ref, nm_ref, nv_ref = refs[5 * n + 4 * k:5 * n + 4 * k + 4]

            @pl.when(jnp.logical_and(s >= first, s < first + nt))
            def _(p_ref=p_ref, l_ref=l_ref, w_ref=w_ref, m_ref=m_ref, v_ref=v_ref,
                  g_ref=g_ref, dl_ref=dl_ref, nm_ref=nm_ref, nv_ref=nv_ref):
                g = p_ref[...].astype(F32)
                for r in range(N_CHIPS - 1):
                    g = g + l_ref[r].astype(F32)
                delta, m2, v2 = _adamw(w_ref[...], g, m_ref[...], v_ref[...])
                g_ref[...] = g
                dl_ref[...] = delta
                nm_ref[...] = m2
                nv_ref[...] = v2

    res, cres = _host_call(body, name=name, grid=(start,), operands=operands, in_specs=in_specs,
                           out_shape=out_shape, out_specs=out_specs, comm=comm, prefetch=[chip])
    return [res[4 * k:4 * k + 4] for k in range(n)], cres


def _sum_adam(name, parts, w, m, v):
    rows, cols = w.shape
    n_parts = parts.shape[0]
    align = PACKED_SUBLANES if parts.dtype == BF16 else SUBLANES
    tr = _tile(rows, max(align, ADAM_BLOCK_BYTES // (cols * 44)), align)
    blk = pl.BlockSpec((tr, cols), lambda i: (i, 0))

    def body(p_ref, w_ref, m_ref, v_ref, g_ref, dl_ref, nm_ref, nv_ref):
        g = p_ref[0].astype(F32)
        for s in range(1, n_parts):
            g = g + p_ref[s].astype(F32)
        delta, m2, v2 = _adamw(w_ref[...], g, m_ref[...], v_ref[...])
        g_ref[...] = g
        dl_ref[...] = delta
        nm_ref[...] = m2
        nv_ref[...] = v2

    out = jax.ShapeDtypeStruct((rows, cols), F32)
    return pl.pallas_call(
        body, name=name, grid=(rows // tr,),
        in_specs=[pl.BlockSpec((n_parts, tr, cols), lambda i: (0, i, 0)), blk, blk, blk],
        out_specs=[blk, blk, blk, blk], out_shape=[out, out, out, out], compiler_params=_params(1),
    )(parts, w, m, v)


def _bias_adam(dmod_all, w, m, v):
    width = w.shape[-1]
    tn = _tile(width, 2048, LANES)
    blk = pl.BlockSpec((1, tn), lambda n: (0, n))

    def body(p_ref, w_ref, m_ref, v_ref, g_ref, dl_ref, nm_ref, nv_ref):
        g = jnp.sum(p_ref[...], axis=0, keepdims=True)
        delta, m2, v2 = _adamw(w_ref[...], g, m_ref[...], v_ref[...])
        g_ref[...] = g
        dl_ref[...] = delta
        nm_ref[...] = m2
        nv_ref[...] = v2

    out = jax.ShapeDtypeStruct((1, width), F32)
    return pl.pallas_call(
        body, name="bias_adam", grid=(width // tn,),
        in_specs=[pl.BlockSpec((dmod_all.shape[0], tn), lambda n: (0, n)), blk, blk, blk],
        out_specs=[blk, blk, blk, blk], out_shape=[out, out, out, out], compiler_params=_params(1),
    )(dmod_all, w, m, v)


def _pack(arrays, total_rows):
    flat = []
    for a in arrays:
        a = a.reshape(-1).astype(F32)
        flat.append(jnp.pad(a, (0, (-a.shape[0]) % LANES)))
    flat = jnp.concatenate(flat).reshape(-1, LANES)
    return jnp.pad(flat, ((0, total_rows - flat.shape[0]), (0, 0)))


def _unpack(packed, shapes):
    out, row = [], 0
    for shp in shapes:
        size = math.prod(shp)
        nrows = -(-size // LANES)
        out.append(packed[row:row + nrows].reshape(-1)[:size].reshape(shp))
        row += nrows
    return out


def kernel(x, c, ctx, c_ctx, w_mod, b_mod, norm_g, w_ffn1_gate, w_ffn1_up, w_ffn1_down, w_in, q_norm_g, k_norm_g, ssm_a_re, ssm_a_im, ssm_log_dt, ssm_b_re, ssm_b_im, ssm_c_re, ssm_c_im, ssm_d, w_glu, b_glu, w_br_attn, w_br_ssm, w_out, w_ffn2_gate, w_ffn2_up, w_ffn2_down, loss_target, m_c_ctx, m_w_mod, m_b_mod, m_norm_g, m_w_ffn1_gate, m_w_ffn1_up, m_w_ffn1_down, m_w_in, m_q_norm_g, m_k_norm_g, m_ssm_a_re, m_ssm_a_im, m_ssm_log_dt, m_ssm_b_re, m_ssm_b_im, m_ssm_c_re, m_ssm_c_im, m_ssm_d, m_w_glu, m_b_glu, m_w_br_attn, m_w_br_ssm, m_w_out, m_w_ffn2_gate, m_w_ffn2_up, m_w_ffn2_down, v_c_ctx, v_w_mod, v_b_mod, v_norm_g, v_w_ffn1_gate, v_w_ffn1_up, v_w_ffn1_down, v_w_in, v_q_norm_g, v_k_norm_g, v_ssm_a_re, v_ssm_a_im, v_ssm_log_dt, v_ssm_b_re, v_ssm_b_im, v_ssm_c_re, v_ssm_c_im, v_ssm_d, v_w_glu, v_b_glu, v_w_br_attn, v_w_br_ssm, v_w_out, v_w_ffn2_gate, v_w_ffn2_up, v_w_ffn2_down):
    _, L, D = x.shape
    Lc = ctx.shape[1]
    R = L + Lc
    MODW = w_mod.shape[-1]
    INS = w_in.shape[-1]
    KVW = INS // 2
    NQ = D // LANES
    NKV = KVW // LANES
    QPK = NQ // NKV
    HBQ = INS // LANES
    G, P, E = ssm_b_re.shape[2:]
    W = G * E
    SW = SLAB_GROUPS * P
    assert E * SLAB_GROUPS == LANES and W == INS and NQ * LANES == D and Lc <= L
    me = 4 * lax.axis_index("x") + 2 * lax.axis_index("y") + lax.axis_index("c")

    x2, ctx2, tgt = x[0], ctx[0], loss_target[0]
    xc0 = jnp.concatenate([x2, ctx2], axis=0)

    def bf(w):
        return w[0].astype(BF16)

    def held_t(w):
        return jnp.swapaxes(w[0], 0, 1)

    def bft(w):
        return held_t(w).astype(BF16)

    def widen(a):
        return jnp.pad(a[0], ((0, 0), (0, D - a.shape[-1])))

    def at_row(a, r, total):
        return jnp.pad(a, ((r, total - r - a.shape[0]), (0, 0)))

    pack_in = (at_row(c, 0, 16) + at_row(widen(norm_g), 1, 16) + at_row(widen(m_norm_g), 4, 16)
               + at_row(widen(v_norm_g), 7, 16))
    (g_in,) = _exchange_only("ag_inputs", _Gather([pack_in]))
    c_all = g_in[:, 0, :]
    dn = D // N_DEV

    def full_norm(k):
        return jnp.transpose(g_in[:, k:k + 3, :dn], (1, 0, 2)).reshape(3, D)

    ng_full, m_ng_full, v_ng_full = full_norm(1), full_norm(4), full_norm(7)
    cs = at_row(c_all, 0, 16) + at_row(c_ctx[None, :], 8, 16)

    b_cols = lax.dynamic_slice_in_dim(b_mod, me * MODW, MODW, axis=1)
    mod_blk = _mod_fwd(cs, w_mod[0], b_cols)
    (mod_g,) = _exchange_only("ag_mod", _Gather([mod_blk]))
    mod_lat = lax.dynamic_index_in_dim(mod_g, me, axis=1, keepdims=False).reshape(9, D)
    mod_ctx = mod_g[:, 8, :].reshape(9, D)[:5]
    tab2 = jnp.concatenate([mod_lat, mod_ctx, ng_full, jnp.zeros((7, D), F32)], axis=0)
    tab3 = tab2[:, None, :]
    SH1, SC1, G1, SH2, SC2, G2, SH3, SC3, G3, MC0, MC1, MC2, MC3, MC4, GAM1, GAM2, GAM3 = range(17)

    wg1, wu1 = _exchange_only("ag_ffn1_gate_up", _Gather([bft(w_ffn1_gate), bft(w_ffn1_up)]))
    h1 = _norm_mod_fwd("nm1_fwd", xc0, tab3, GAM1, (SH1, MC0), (SC1, MC1), L, Lc)
    a1, b1, s1, (wd1,) = _ffn_up("ffn1", h1, wg1, wu1, comm=_Gather([bf(w_ffn1_down)]))
    f1, xc1, (win,) = _ffn_down("ffn1", s1, wd1, xc0, tab2, (G1, MC2), L, comm=_Gather([bf(w_in)]))

    h2 = _norm_mod_fwd("nm2_fwd", xc1, tab3, GAM2, (SH2, MC3), (SC2, MC4), L, Lc)
    tm = _tile(R, MM_TILE, LANES)
    tml = _tile(L, MM_TILE, LANES)

    (p01,), _ = _matmul(
        "in_proj_kvu", (2, R // tm), [h2, win],
        [pl.BlockSpec((tm, D), lambda j, i: (i, 0)), pl.BlockSpec((None, D, INS), lambda j, i: (j, 0, 0))],
        [(0, 1, 0, NN)], [jax.ShapeDtypeStruct((2, R, INS), F32)],
        [pl.BlockSpec((None, tm, INS), lambda j, i: (j, i, 0))], _store_all)
    (p27,), (wglu, wbra) = _matmul(
        "in_proj_qg", (6, L // tml), [h2, win],
        [pl.BlockSpec((tml, D), lambda j, i: (i, 0)), pl.BlockSpec((None, D, INS), lambda j, i: (j + 2, 0, 0))],
        [(0, 1, 0, NN)], [jax.ShapeDtypeStruct((6, L, INS), F32)],
        [pl.BlockSpec((None, tml, INS), lambda j, i: (j, i, 0))], _store_all,
        comm=_Gather([bf(w_glu), bf(w_br_attn)]))
    wglu2 = wglu.reshape(W, W)
    wbra2 = wbra.reshape(D, D)

    cos_all, sin_all = _rope_tables(L, Lc)
    cos_l, sin_l = cos_all[:L], sin_all[:L]

    q_rot = _qk_prep("q_prep", p27, 0, HBQ, NQ, L, q_norm_g, cos_l, sin_l)
    k_rot = _qk_prep("k_prep", p01, 0, NKV, NKV, R, k_norm_g, cos_all, sin_all)
    v_hd = _heads_cast("v_heads", p01, 1, NKV, NKV, R)
    attn, (wbrs, wout) = _attn_fwd(q_rot, k_rot, v_hd, QPK, comm=_Gather([bf(w_br_ssm), bf(w_out)]))
    wout2 = wout.reshape(D, D)

    t_rows = _tile(math.gcd(L, Lc), ROW_TILE, SUBLANES)
    nl, ncx = L // t_rows, Lc // t_rows
    nch = nl + ncx
    ns = G // SLAB_GROUPS
    ssm_prim = (ssm_a_re[0], ssm_a_im[0], ssm_log_dt[0], ssm_b_re[0], ssm_b_im[0])
    _, _, bt_re, bt_im = _ssm_discretize(*ssm_prim)
    pw_re, pw_im = _lambda_powers(ssm_a_re[0], ssm_a_im[0], ssm_log_dt[0], ns)
    bd_re = _block_diag(jnp.swapaxes(bt_re, 2, 3))
    bd_im = _block_diag(jnp.swapaxes(bt_im, 2, 3))
    ct_re = _block_diag(ssm_c_re[0])
    ct_im = _block_diag(-ssm_c_im[0])
    fwd_desc = (False, True)
    adj_desc = (True, False)
    s_bd = jnp.concatenate([bd_re, bd_im], axis=-1)
    s_ct = jnp.concatenate([ct_re, ct_im], axis=-1)
    s_bd16, s_ct16 = s_bd.astype(BF16), s_ct.astype(BF16)
    s_pw = jnp.pad(jnp.transpose(jnp.concatenate([pw_re, pw_im], axis=-1), (1, 2, 0, 3)),
                   ((0, 0), (0, 0), (0, 2 * SCAN_TAPS - SCAN_TAPS - 1), (0, 0)))
    s_tab = _carry_tables(pw_re, pw_im, fwd_desc)
    s_tabc = _carry_tables(pw_re, -pw_im, adj_desc)
    order = [lambda i: (i + nl) % nch, lambda i: nch - 1 - i]
    order_adj = [lambda i: (nch - 1 - i + nl) % nch, lambda i: i]
    y0, st0, (wg2,) = _ssm_fwd("ssm_fwd0", 0, p01, 1, s_bd, s_pw, s_tab, s_ct16, fwd_desc[0], order[0], t_rows, R,
                               comm=_Gather([bft(w_ffn2_gate)]))
    y1, st1, (wu2,) = _ssm_fwd("ssm_fwd1", 1, p01, 1, s_bd, s_pw, s_tab, s_ct16, fwd_desc[1], order[1], t_rows, R,
                               comm=_Gather([bft(w_ffn2_up)]))
    states = [st0, st1]

    tr = _row_tile(L, 0)
    rowW = pl.BlockSpec((tr, W), lambda i: (i, 0))
    vecW = pl.BlockSpec((1, W), lambda i: (0, 0))
    u_lat = pl.BlockSpec((None, tr, W), lambda i: (1, i, 0))

    def ssm_post(i, u, ya, yb, dvec):
        sv = dvec * u + ya + yb
        return [sv, _gelu(sv)], []

    (ssm_out, yg), _, _ = _rowwise(
        "ssm_post", L // tr, [p01, y0, y1, ssm_d], [u_lat, rowW, rowW, vecW],
        [jax.ShapeDtypeStruct((L, W), F32), jax.ShapeDtypeStruct((L, W), BF16)], [rowW, rowW], [], ssm_post)

    tnw = _tile(W, MM_TILE, LANES)

    def glu_epilogue(accs, ins, outs, pids):
        z = accs[0] + ins[3][...]
        outs[0][...] = z
        outs[1][...] = (_gelu(ins[2][...]) * _sigmoid(z)).astype(BF16)

    (z_glu, y2), _ = _matmul(
        "glu", (L // tml, W // tnw), [yg, wglu2, ssm_out, b_glu],
        [pl.BlockSpec((tml, W), lambda i, n: (i, 0)), pl.BlockSpec((W, tnw), lambda i, n: (0, n)),
         pl.BlockSpec((tml, tnw), lambda i, n: (i, n)), pl.BlockSpec((1, tnw), lambda i, n: (0, n))],
        [(0, 1, 0, NN)], [jax.ShapeDtypeStruct((L, W), F32), jax.ShapeDtypeStruct((L, W), BF16)],
        [pl.BlockSpec((tml, tnw), lambda i, n: (i, n))] * 2, glu_epilogue)

    tnd = _tile(D, MM_TILE, LANES)
    out_ld = pl.BlockSpec((tml, tnd), lambda i, n: (i, n))
    (br_a,), _ = _matmul(
        "br_attn", (L // tml, D // tnd), [attn, wbra2],
        [pl.BlockSpec((tml, D), lambda i, n: (i, 0)), pl.BlockSpec((D, tnd), lambda i, n: (0, n))],
        [(0, 1, 0, NN)], [jax.ShapeDtypeStruct((L, D), F32)], [out_ld], _store_all)

    cb = wbrs.shape[-1]
    gpb = INS // cb

    def gate_spec(first_shard):
        return pl.BlockSpec((None, tml, cb), lambda i, j: (first_shard + j // gpb, i, j % gpb))

    def merge_epilogue(accs, ins, outs, pids):
        br = accs[0]
        outs[0][...] = br
        outs[1][...] = (_sigmoid(ins[2][...]) * ins[4][...] + _sigmoid(ins[3][...]) * br).astype(BF16)

    col_blk = pl.BlockSpec((tml, cb), lambda i, j: (i, j))
    (br_s, merged), _ = _matmul(
        "br_ssm_merge", (L // tml, N_DEV), [y2, wbrs, p27, p27, br_a],
        [pl.BlockSpec((tml, W), lambda i, j: (i, 0)), pl.BlockSpec((None, W, cb), lambda i, j: (j, 0, 0)),
         gate_spec(2), gate_spec(4), col_blk],
        [(0, 1, 0, NN)], [jax.ShapeDtypeStruct((L, D), F32), jax.ShapeDtypeStruct((L, D), BF16)],
        [col_blk, col_blk], merge_epilogue)

    def out_epilogue(accs, ins, outs, pids):
        outs[0][...] = accs[0]
        outs[1][...] = ins[2][...] + ins[3][G2:G2 + 1, :] * accs[0]

    (mix, x2_), _ = _matmul(
        "out_proj", (L // tml, D // tnd), [merged, wout2, xc1, tab2],
        [pl.BlockSpec((tml, D), lambda i, n: (i, 0)), pl.BlockSpec((D, tnd), lambda i, n: (0, n)), out_ld,
         pl.BlockSpec((tab2.shape[0], tnd), lambda i, n: (0, n))],
        [(0, 1, 0, NN)], [jax.ShapeDtypeStruct((L, D), F32)] * 2, [out_ld, out_ld], out_epilogue)

    h3 = _norm_mod_fwd("nm3_fwd", x2_, tab3, GAM3, (SH3, SH3), (SC3, SC3), L, 0)
    a3, b3, s3, (wd2,) = _ffn_up("ffn2", h3, wg2, wu2, comm=_Gather([bf(w_ffn2_down)]))
    f3, x3, _ = _ffn_down("ffn2", s3, wd2, x2_, tab2, (G3, G3), L)

    trd = _row_tile(L, 0)
    rowD = pl.BlockSpec((trd, D), lambda i: (i, 0))

    def loss_fn(i, yv, t):
        err = yv - t
        return [err * (1.0 / D)], [_colsum(err * err)]

    (dx3,), (sq,), _ = _rowwise("loss", L // trd, [x3, tgt], [rowD, rowD],
                                [jax.ShapeDtypeStruct((L, D), F32)], [rowD], [D], loss_fn)
    loss = lax.psum(0.5 * jnp.sum(sq) / D, ("x", "y", "c"))

    core = lax.axis_index("c").astype(jnp.int32).reshape(1)
    chip = (2 * lax.axis_index("x") + lax.axis_index("y")).astype(jnp.int32).reshape(1)

    def pair_sums(tag, grads, halves):
        return [_pair_sum("pair_%s%d" % (tag, k), g_, h_, core) for k, (g_, h_) in enumerate(zip(grads, halves))]

    df3, (dg3, _) = _gate_bwd("gate3_bwd", dx3, f3, tab3, (G3, G3), 0.5, L, 0)
    dwd2, _ = _ffn_dwd("ffn2b", s3, df3)
    da3, db3, half_wd2 = _ffn_ds("ffn2b", df3, wd2, a3, b3, comm=_SiblingSwap([dwd2]))
    (p_wd2,) = pair_sums("wd2", [dwd2], half_wd2)
    dwg2, dwu2, (l_wd2,) = _ffn_dwgu("ffn2b", h3, da3, db3, comm=_ChipExchange([p_wd2]))
    dh3, half_wgu2 = _ffn_dh("ffn2b", da3, db3, wg2, wu2, comm=_SiblingSwap([dwg2, dwu2]))
    p_wg2, p_wu2 = pair_sums("wgu2", [dwg2, dwu2], half_wgu2)
    dx2, (dsh3, dsc3, _, _, dgam3) = _norm_mod_bwd("nm3_bwd", x2_, dh3, tab3, GAM3, (SC3, SC3), L, 0, dres=dx3)

    dmix, (dg2, _) = _gate_bwd("gate2_bwd", dx2, mix, tab3, (G2, G2), 1.0, L, 0)

    def dmerged_epilogue(accs, ins, outs, pids):
        dm = accs[0]
        ga, gs = _sigmoid(ins[2][...]), _sigmoid(ins[3][...])
        outs[0][...] = (ga * dm).astype(BF16)
        outs[1][...] = (gs * dm).astype(BF16)
        outs[2][...] = (dm * ins[4][...] * ga * (1.0 - ga)).astype(BF16)
        outs[3][...] = (dm * ins[5][...] * gs * (1.0 - gs)).astype(BF16)

    dgate_spec = pl.BlockSpec((None, tml, cb), lambda i, j: (j // gpb, i, j % gpb))
    (d_br_a, d_br_s, dg_a, dg_s), _ = _matmul(
        "dmerged", (L // tml, N_DEV), [dmix, wout2, p27, p27, br_a, br_s],
        [pl.BlockSpec((tml, D), lambda i, j: (i, 0)), pl.BlockSpec((cb, D), lambda i, j: (j, 0)),
         gate_spec(2), gate_spec(4), col_blk, col_blk],
        [(0, 1, 0, NT)],
        [jax.ShapeDtypeStruct((L, D), BF16)] * 2 + [jax.ShapeDtypeStruct((2, L, INS), BF16)] * 2,
        [col_blk, col_blk, dgate_spec, dgate_spec], dmerged_epilogue)

    def wgrad(name, a_mat, b_mat, tmo, tno):
        ka, ma = a_mat.shape
        _, nb_ = b_mat.shape
        return _matmul(
            name, (ma // tmo, nb_ // tno), [a_mat, b_mat],
            [pl.BlockSpec((ka, tmo), lambda m, n: (0, m)), pl.BlockSpec((ka, tno), lambda m, n: (0, n))],
            [(0, 1, 0, TN)], [jax.ShapeDtypeStruct((ma, nb_), BF16)],
            [pl.BlockSpec((tmo, tno), lambda m, n: (m, n))], _store_all)[0][0]

    dwout = wgrad("dw_out", merged, dmix, tnd, tnd)
    dwbra = wgrad("dw_br_attn", attn, d_br_a, tnd, tnd)
    (d_attn,), _ = _matmul(
        "d_attn", (L // tml, D // tnd), [d_br_a, wbra2],
        [pl.BlockSpec((tml, D), lambda i, n: (i, 0)), pl.BlockSpec((tnd, D), lambda i, n: (n, 0))],
        [(0, 1, 0, NT)], [jax.ShapeDtypeStruct((L, D), BF16)], [out_ld], _store_all)

    (dwbrs,), _ = _matmul(
        "dw_br_ssm", (N_DEV,), [y2, d_br_s],
        [pl.BlockSpec((L, W), lambda j: (0, 0)), pl.BlockSpec((L, cb), lambda j: (0, j))],
        [(0, 1, 0, TN)], [jax.ShapeDtypeStruct((N_DEV, W, cb), BF16)],
        [pl.BlockSpec((None, W, cb), lambda j: (j, 0, 0))], _store_all)

    def dy2_epilogue(accs, ins, outs, pids):
        dy2 = accs[0]
        sg = _sigmoid(ins[2][...])
        outs[0][...] = dy2 * sg
        outs[1][...] = (dy2 * _gelu(ins[3][...]) * sg * (1.0 - sg)).astype(BF16)

    wn_blk = pl.BlockSpec((tml, tnw), lambda i, n, k: (i, n))
    (dyg1, dz), _ = _matmul(
        "d_y2", (L // tml, W // tnw, N_DEV), [d_br_s, wbrs, z_glu, ssm_out],
        [pl.BlockSpec((tml, cb), lambda i, n, k: (i, k)), pl.BlockSpec((None, tnw, cb), lambda i, n, k: (k, n, 0)),
         wn_blk, wn_blk],
        [(0, 1, 0, NT)], [jax.ShapeDtypeStruct((L, W), F32), jax.ShapeDtypeStruct((L, W), BF16)],
        [wn_blk, wn_blk], dy2_epilogue, acc_shapes=[(tml, tnw)], nk=N_DEV)

    dwglu = wgrad("dw_glu", yg, dz, tnw, tnw)
    mix_grads = [dwout.reshape(N_DEV, D // N_DEV, D), dwbra.reshape(N_DEV, D // N_DEV, D), dwbrs,
                 dwglu.reshape(N_DEV, W // N_DEV, W)]

    def dssm_epilogue(accs, ins, outs, pids):
        outs[0][...] = (accs[0] + ins[2][...]) * _gelu_grad(ins[3][...])

    wn2 = pl.BlockSpec((tml, tnw), lambda i, n: (i, n))
    (dssm,), _ = _matmul(
        "d_ssm", (L // tml, W // tnw), [dz, wglu2, dyg1, ssm_out],
        [pl.BlockSpec((tml, W), lambda i, n: (i, 0)), pl.BlockSpec((tnw, W), lambda i, n: (n, 0)), wn2, wn2],
        [(0, 1, 0, NT)], [jax.ShapeDtypeStruct((L, W), F32)], [wn2], dssm_epilogue)

    dssm_all = jnp.concatenate([dssm, jnp.zeros((Lc, W), F32)], axis=0)
    du0, dbd0, dcd0, dlam0, (l_wg2, *half_mix) = _ssm_bwd(
        "ssm_bwd0", 0, dssm_all, p01, 1, states[0], s_ct, s_pw, s_tabc, s_bd16, adj_desc[0], order_adj[0], t_rows, R,
        comm=_Both([_ChipExchange([p_wg2]), _SiblingSwap(mix_grads)]))
    p_wout, p_wbra, p_wbrs, p_wglu = pair_sums("mix", mix_grads, half_mix)
    du1, dbd1, dcd1, dlam1, (l_wu2,) = _ssm_bwd(
        "ssm_bwd1", 1, dssm_all, p01, 1, states[1], s_ct, s_pw, s_tabc, s_bd16, adj_desc[1], order_adj[1], t_rows, R,
        comm=_ChipExchange([p_wu2]))

    trr = _row_tile(L, Lc)
    nlt = L // trr
    rowR = pl.BlockSpec((trr, W), lambda i: (i, 0))

    def du_fn(i, dua, dub, dsv, dvec, u):
        lat = (i < nlt).astype(F32)
        return [dua + dub + lat * (dvec * dsv)], [lat * _colsum(dsv * u)]

    (du_all,), (d_ssm_d,), _ = _rowwise(
        "du_combine", R // trr, [du0, du1, dssm_all, ssm_d, p01],
        [rowR, rowR, rowR, pl.BlockSpec((1, W), lambda i: (0, 0)), pl.BlockSpec((None, trr, W), lambda i: (1, i, 0))],
        [jax.ShapeDtypeStruct((R, W), BF16)], [rowR], [W], du_fn)

    def dz_sum(i, dzv):
        return [], [_colsum(dzv.astype(F32))]

    _, (d_b_glu,), _ = _rowwise("db_glu", L // tr, [dz], [rowW], [], [], [W], dz_sum)

    dbd, dcd, dlam = jnp.stack([dbd0, dbd1]), jnp.stack([dcd0, dcd1]), jnp.stack([dlam0, dlam1])
    dbt_re = jnp.swapaxes(_block_diag_extract(dbd[..., :SW], E, P), 2, 3)
    dbt_im = jnp.swapaxes(_block_diag_extract(dbd[..., SW:], E, P), 2, 3)
    dl_re, dl_im = dlam[:, :, 0, :SW].reshape(2, G, P), dlam[:, :, 0, SW:].reshape(2, G, P)
    _, vjp = jax.vjp(_ssm_discretize, *ssm_prim)
    d_a_re, d_a_im, d_ldt, d_b_re, d_b_im = vjp((dl_re, dl_im, dbt_re, dbt_im))
    d_c_re = jnp.swapaxes(_block_diag_extract(dcd[:, :, :SW, :], P, E), 2, 3)
    d_c_im = -jnp.swapaxes(_block_diag_extract(dcd[:, :, SW:, :], P, E), 2, 3)

    early_g = [d_a_re, d_a_im, d_ldt, d_b_re, d_b_im, d_c_re, d_c_im, d_ssm_d, d_b_glu]
    early_w = [ssm_a_re, ssm_a_im, ssm_log_dt, ssm_b_re, ssm_b_im, ssm_c_re, ssm_c_im, ssm_d, b_glu]
    early_m = [m_ssm_a_re, m_ssm_a_im, m_ssm_log_dt, m_ssm_b_re, m_ssm_b_im, m_ssm_c_re, m_ssm_c_im, m_ssm_d, m_b_glu]
    early_v = [v_ssm_a_re, v_ssm_a_im, v_ssm_log_dt, v_ssm_b_re, v_ssm_b_im, v_ssm_c_re, v_ssm_c_im, v_ssm_d, v_b_glu]
    early_shapes = [a.shape for a in early_w]
    early_rows = -(-sum(-(-math.prod(s) // LANES) for s in early_shapes) // 256) * 256

    dq_rot, dk_rot, dv_hd, (l_wout, l_wbra, l_wbrs, l_wglu, early_parts) = _attn_bwd(
        q_rot, k_rot, v_hd, d_attn, QPK,
        comm=_Both([_ChipExchange([p_wout, p_wbra, p_wbrs, p_wglu]), _Gather([_pack(early_g, early_rows)])]))
    dq_pre, d_qg = _qk_prep_bwd("q_prep_bwd", dq_rot, p27, 0, HBQ, NQ, L, q_norm_g, cos_l, sin_l)
    dk_pre, d_kg = _qk_prep_bwd("k_prep_bwd", dk_rot, p01, 0, NKV, NKV, R, k_norm_g, cos_all, sin_all)
    dv_pre = _heads_merge("dv_merge", dv_hd)

    def lat_blocks(a):
        return jnp.pad(a, ((0, 0), (0, Lc), (0, 0)))

    dp = jnp.concatenate([
        jnp.concatenate([dk_pre[0], dv_pre], axis=1)[None], du_all[None],
        lat_blocks(dq_pre), lat_blocks(dg_a), lat_blocks(dg_s)], axis=0)

    tmo = _tile(D, MM_TILE, LANES)
    (dwin,), _ = _matmul(
        "dw_in", (N_DEV, D // tmo), [h2, dp],
        [pl.BlockSpec((R, tmo), lambda j, m: (0, m)), pl.BlockSpec((None, R, INS), lambda j, m: (j, 0, 0))],
        [(0, 1, 0, TN)], [jax.ShapeDtypeStruct((N_DEV, D, INS), BF16)],
        [pl.BlockSpec((None, tmo, INS), lambda j, m: (j, m, 0))], _store_all)
    tnh = _tile(D, MM_TILE_NT, LANES)
    (dh2,), half_win = _matmul(
        "d_h2", (R // tm, D // tnh), [dp, win],
        [pl.BlockSpec((N_DEV, tm, INS), lambda i, n: (0, i, 0)),
         pl.BlockSpec((N_DEV, tnh, INS), lambda i, n: (0, n, 0))],
        [(0, 1, 0, NT, N_DEV)], [jax.ShapeDtypeStruct((R, D), F32)], [pl.BlockSpec((tm, tnh), lambda i, n: (i, n))],
        _store_all, comm=_SiblingSwap([dwin]))
    (p_win,) = pair_sums("win", [dwin], half_win)
    dxc1, (dsh2, dsc2, dmc3, dmc4, dgam2) = _norm_mod_bwd(
        "nm2_bwd", xc1, dh2, tab3, GAM2, (SC2, MC4), L, Lc, dres=dx2)

    df1, (dg1, dmc2) = _gate_bwd("gate1_bwd", dxc1, f1, tab3, (G1, MC2), 0.5, L, Lc)
    dwd1, _ = _ffn_dwd("ffn1b", s1, df1)
    da1, db1, half_wd1 = _ffn_ds("ffn1b", df1, wd1, a1, b1, comm=_SiblingSwap([dwd1]))
    (p_wd1,) = pair_sums("wd1", [dwd1], half_wd1)
    dwg1, dwu1, (l_wd1,) = _ffn_dwgu("ffn1b", h1, da1, db1, comm=_ChipExchange([p_wd1]))
    dh1, (l_win, *half_wgu1) = _ffn_dh(
        "ffn1b", da1, db1, wg1, wu1, comm=_Both([_ChipExchange([p_win]), _SiblingSwap([dwg1, dwu1])]))
    p_wg1, p_wu1 = pair_sums("wgu1", [dwg1, dwu1], half_wgu1)

    def adam_item(p, l_, w_, m_, v_):
        return (p, l_, w_[0], m_[0], v_[0])

    def adam_item_t(p, l_, w_, m_, v_):
        return (p, l_, held_t(w_), held_t(m_), held_t(v_))

    ready_a = [adam_item(p_wd1, l_wd1, w_ffn1_down, m_w_ffn1_down, v_w_ffn1_down),
               adam_item(p_win, l_win, w_in, m_w_in, v_w_in),
               adam_item(p_wglu, l_wglu, w_glu, m_w_glu, v_w_glu),
               adam_item(p_wbra, l_wbra, w_br_attn, m_w_br_attn, v_w_br_attn),
               adam_item(p_wbrs, l_wbrs, w_br_ssm, m_w_br_ssm, v_w_br_ssm)]
    ready_b = [adam_item(p_wout, l_wout, w_out, m_w_out, v_w_out),
               adam_item_t(p_wg2, l_wg2, w_ffn2_gate, m_w_ffn2_gate, v_w_ffn2_gate),
               adam_item_t(p_wu2, l_wu2, w_ffn2_up, m_w_ffn2_up, v_w_ffn2_up),
               adam_item(p_wd2, l_wd2, w_ffn2_down, m_w_ffn2_down, v_w_ffn2_down)]
    adam_a, (l_wg1,) = _owner_adam("adam_ready_a", ready_a, chip, comm=_ChipExchange([p_wg1]))
    adam_b, (l_wu1,) = _owner_adam("adam_ready_b", ready_b, chip, comm=_ChipExchange([p_wu1]))
    adam_ready = adam_a + adam_b
    dxc0, (dsh1, dsc1, dmc0, dmc1, dgam1) = _norm_mod_bwd(
        "nm1_bwd", xc0, dh1, tab3, GAM1, (SC1, MC1), L, Lc, dres=dxc1)
    grad_x = dxc0[:L][None]

    dmod_lat = jnp.concatenate([dsh1, dsc1, dg1, dsh2, dsc2, dg2, dsh3, dsc3, dg3], axis=1)
    dmod_ctx = jnp.concatenate([dmc0, dmc1, dmc2, dmc3, dmc4, jnp.zeros((1, 4 * D), F32)], axis=1)
    dmod_pack = at_row(dmod_lat, 0, SUBLANES) + at_row(dmod_ctx, 1, SUBLANES)
    (dmod_g,) = _exchange_only("ag_dmod", _Gather([dmod_pack]))
    dmod_all = dmod_g.reshape(N_DEV * SUBLANES, 9 * D)
    dmod_cols = lax.dynamic_slice_in_dim(dmod_all, me * MODW, MODW, axis=1)
    (g_wmod, dl_wmod, nm_wmod, nv_wmod, dsilu), _ = _mod_bwd_adam(
        cs, dmod_cols, w_mod[0], m_w_mod[0], v_w_mod[0])
    sg_cc = jax.nn.sigmoid(c_ctx)
    d_c_ctx = dsilu[8] * (sg_cc * (1.0 + c_ctx * (1.0 - sg_cc)))
    g_bmod, dl_bmod, nm_bmod, nv_bmod = _bias_adam(dmod_all, b_mod, m_b_mod, v_b_mod)

    dgam_all = jnp.concatenate([dgam1, dgam2, dgam3], axis=0)
    late_g = [d_c_ctx, d_qg, d_kg, dgam_all]
    late_w = [c_ctx, q_norm_g, k_norm_g, ng_full]
    late_m = [m_c_ctx, m_q_norm_g, m_k_norm_g, m_ng_full]
    late_v = [v_c_ctx, v_q_norm_g, v_k_norm_g, v_ng_full]
    late_shapes = [a.shape for a in late_w]
    late_rows = -(-sum(-(-math.prod(s) // LANES) for s in late_shapes) // SUBLANES) * SUBLANES
    (late_parts,) = _exchange_only("ag_small_grads", _Gather([_pack(late_g, late_rows)]))
    late_out = _sum_adam("small_adam_late", late_parts, _pack(late_w, late_rows), _pack(late_m, late_rows),
                         _pack(late_v, late_rows))
    early_out = _sum_adam("small_adam_s5", early_parts, _pack(early_w, early_rows), _pack(early_m, early_rows),
                          _pack(early_v, early_rows))

    def my_norm_cols(a):
        return lax.dynamic_slice_in_dim(a, me * dn, dn, axis=1)[None]

    small = []
    for lo, eo in zip(late_out, early_out):
        c_ctx_, qg_, kg_, ng_ = _unpack(lo, late_shapes)
        small.append([c_ctx_, qg_, kg_] + _unpack(eo, early_shapes) + [my_norm_cols(ng_)])
    sm_g, sm_dl, sm_m, sm_v = small

    adam_last, _ = _owner_adam(
        "adam_last", [adam_item_t(p_wg1, l_wg1, w_ffn1_gate, m_w_ffn1_gate, v_w_ffn1_gate),
                      adam_item_t(p_wu1, l_wu1, w_ffn1_up, m_w_ffn1_up, v_w_ffn1_up)], chip)
    transposed = (0, 1, 8, 9)
    big_out = [[(jnp.swapaxes(o, 0, 1) if k in transposed else o)[None] for o in grp_]
               for k, grp_ in enumerate(adam_last + adam_ready)]

    def leaf(kind):
        sm = (sm_g, sm_dl, sm_m, sm_v)[kind]
        mod = (g_wmod, dl_wmod, nm_wmod, nv_wmod)[kind][None]
        bmod = (g_bmod, dl_bmod, nm_bmod, nv_bmod)[kind]
        big = [b[kind] for b in big_out]
        (c_ctx_, qg_, kg_, a_re_, a_im_, ldt_, b_re_, b_im_, c_re_, c_im_, sd_, bglu_, ng_) = sm
        return [c_ctx_, mod, bmod, ng_, big[0], big[1], big[2], big[3], qg_, kg_, a_re_, a_im_, ldt_, b_re_, b_im_,
                c_re_, c_im_, sd_, big[4], bglu_, big[5], big[6], big[7], big[8], big[9], big[10]]

    return tuple([loss, grad_x] + leaf(0) + leaf(1) + leaf(2) + leaf(3))
```

```python
import math

import jax
import jax.numpy as jnp
import numpy as np
from jax import lax
from jax.experimental import pallas as pl
from jax.experimental.pallas import tpu as pltpu

F32 = jnp.float32
BF16 = jnp.bfloat16

N_DEV = 8
N_CHIPS = 4
LANES = 128
SUBLANES = 8
PACKED_SUBLANES = 16
VMEM_LIMIT = 56 * 1024 * 1024
MM_TILE = 512
MM_TILE_NT = 256
ROW_TILE = 256
HEAD_ROW_TILE = 512
ATTN_BWD_HEADS = 4
ATTN_FWD_HEADS = 2
ADAM_BLOCK_BYTES = 4 * 1024 * 1024
ADAM_GROUP_VMEM = 36 * 1024 * 1024

NORM_EPS = 1e-6
GRID_W = 64
ROPE_THETA = 10000.0
SCAN_TAPS = SUBLANES
SLAB_GROUPS = 8

ADAM_LR = 0.001
ADAM_B1 = 0.9
ADAM_B2 = 0.999
ADAM_EPS = 1e-08
ADAM_WD = 0.01
ADAM_STEP = 10

NN = (((1,), (0,)), ((), ()))
NT = (((1,), (1,)), ((), ()))
TN = (((0,), (0,)), ((), ()))

MESH = pl.DeviceIdType.MESH
ANY = pl.BlockSpec(memory_space=pl.ANY)


def _tile(n, cap, align):
    best = None
    for t in range(align, min(n, cap) + 1, align):
        if n % t == 0:
            best = t
    return n if best is None else best


def _params(n_grid):
    return pltpu.CompilerParams(dimension_semantics=("arbitrary",) * n_grid, vmem_limit_bytes=VMEM_LIMIT)


def _sigmoid(x):
    return 1.0 / (1.0 + jnp.exp(-x))


LOG2E = math.log2(math.e)
GELU_K = math.sqrt(2.0 / math.pi)
GELU_C = 0.044715


def _gelu(x):
    return 0.5 * x * (1.0 + jnp.tanh(GELU_K * (x + GELU_C * x * x * x)))


def _gelu_grad(x):
    t = jnp.tanh(GELU_K * (x + GELU_C * x * x * x))
    return 0.5 * (1.0 + t) + 0.5 * x * (1.0 - t * t) * GELU_K * (1.0 + 3.0 * GELU_C * x * x)


def _adamw(w, g, m, v):
    m2 = ADAM_B1 * m + (1.0 - ADAM_B1) * g
    v2 = ADAM_B2 * v + (1.0 - ADAM_B2) * (g * g)
    m_hat = m2 / (1.0 - ADAM_B1 ** ADAM_STEP)
    v_hat = v2 / (1.0 - ADAM_B2 ** ADAM_STEP)
    delta = -ADAM_LR * (m_hat / (jnp.sqrt(v_hat) + ADAM_EPS) + ADAM_WD * w)
    return delta, m2, v2


def _position():
    return lax.axis_index("x"), lax.axis_index("y"), lax.axis_index("c")


class _Gather:
    def __init__(self, arrays):
        self.arrays = list(arrays)
        n = len(self.arrays)
        self.out_shapes = [jax.ShapeDtypeStruct((N_DEV,) + a.shape, a.dtype) for a in self.arrays]
        self.scratch = [pltpu.SemaphoreType.DMA((n, 7)), pltpu.SemaphoreType.DMA((n, 7)),
                        pltpu.SemaphoreType.DMA((n,))]

    def _plan(self, ins, outs, sems):
        send, recv, local = sems
        x, y, c = _position()
        me, sibling = (x, y, c), (x, y, 1 - c)
        chips = [(1 - x, y), (x, 1 - y), (1 - x, 1 - y)]

        def slot(a, p):
            return outs[a].at[4 * p[0] + 2 * p[1] + p[2]]

        def copy(a, k, block, to, src=None):
            dst = slot(a, block)
            return pltpu.make_async_remote_copy(
                src_ref=dst if src is None else src, dst_ref=dst,
                send_sem=send.at[a, k], recv_sem=recv.at[a, k], device_id=to, device_id_type=MESH)

        mine = [pltpu.make_async_copy(ins[a], slot(a, me), local.at[a]) for a in range(len(ins))]
        return me, sibling, chips, c, copy, mine

    def start(self, ins, outs, sems):
        me, sibling, chips, c, copy, mine = self._plan(ins, outs, sems)
        for cp in mine:
            cp.start()
        for a in range(len(ins)):
            copy(a, 0, me, sibling, src=ins[a]).start()
            for j, chip in enumerate(chips):
                copy(a, 1 + j, me, (*chip, c), src=ins[a]).start()

    def finish(self, ins, outs, sems):
        me, sibling, chips, c, copy, mine = self._plan(ins, outs, sems)
        n = len(ins)
        for j, chip in enumerate(chips):
            for a in range(n):
                copy(a, 1 + j, (*chip, c), me).wait_recv()
                copy(a, 4 + j, (*chip, c), sibling).start()
        for a in range(n):
            copy(a, 0, sibling, me).wait_recv()
        for j, chip in enumerate(chips):
            for a in range(n):
                copy(a, 4 + j, (*chip, 1 - c), me).wait_recv()
        for a in range(n):
            copy(a, 0, me, sibling, src=ins[a]).wait_send()
            for j, chip in enumerate(chips):
                copy(a, 1 + j, me, (*chip, c), src=ins[a]).wait_send()
                copy(a, 4 + j, (*chip, c), sibling).wait_send()
        for cp in mine:
            cp.wait()


class _SiblingSwap:
    def __init__(self, arrays):
        self.arrays = list(arrays)
        n = len(self.arrays)
        self.out_shapes = [jax.ShapeDtypeStruct((N_CHIPS,) + a.shape[1:], a.dtype) for a in self.arrays]
        self.scratch = [pltpu.SemaphoreType.DMA((n, N_CHIPS)), pltpu.SemaphoreType.DMA((n, N_CHIPS))]

    def _plan(self, ins, outs, sems):
        send, recv = sems
        x, y, c = _position()
        return [pltpu.make_async_remote_copy(
            src_ref=ins[a].at[2 * j + 1 - c], dst_ref=outs[a].at[j],
            send_sem=send.at[a, j], recv_sem=recv.at[a, j], device_id=(x, y, 1 - c), device_id_type=MESH)
            for a in range(len(ins)) for j in range(N_CHIPS)]

    def start(self, ins, outs, sems):
        for cp in self._plan(ins, outs, sems):
            cp.start()

    def finish(self, ins, outs, sems):
        copies = self._plan(ins, outs, sems)
        for cp in copies:
            cp.wait_recv()
        for cp in copies:
            cp.wait_send()


class _ChipExchange:
    def __init__(self, arrays):
        self.arrays = list(arrays)
        n = len(self.arrays)
        self.out_shapes = [jax.ShapeDtypeStruct((N_CHIPS - 1,) + a.shape[1:], a.dtype) for a in self.arrays]
        self.scratch = [pltpu.SemaphoreType.DMA((n, N_CHIPS - 1)), pltpu.SemaphoreType.DMA((n, N_CHIPS - 1))]

    def _plan(self, ins, outs, sems):
        send, recv = sems
        x, y, c = _position()
        copies = []
        for r in range(1, N_CHIPS):
            px, py = x ^ (r >> 1), y ^ (r & 1)
            for a in range(len(ins)):
                copies.append(pltpu.make_async_remote_copy(
                    src_ref=ins[a].at[2 * px + py], dst_ref=outs[a].at[r - 1],
                    send_sem=send.at[a, r - 1], recv_sem=recv.at[a, r - 1],
                    device_id=(px, py, c), device_id_type=MESH))
        return copies

    def start(self, ins, outs, sems):
        for cp in self._plan(ins, outs, sems):
            cp.start()

    def finish(self, ins, outs, sems):
        copies = self._plan(ins, outs, sems)
        for cp in copies:
            cp.wait_recv()
        for cp in copies:
            cp.wait_send()


class _Both:
    def __init__(self, comms):
        self.comms = list(comms)
        self.arrays = [a for cm in self.comms for a in cm.arrays]
        self.out_shapes = [s for cm in self.comms for s in cm.out_shapes]
        self.scratch = [s for cm in self.comms for s in cm.scratch]

    def _split(self, ins, outs, sems):
        i = o = s = 0
        for cm in self.comms:
            ni, no, nsem = len(cm.arrays), len(cm.out_shapes), len(cm.scratch)
            yield cm, ins[i:i + ni], outs[o:o + no], sems[s:s + nsem]
            i, o, s = i + ni, o + no, s + nsem

    def start(self, ins, outs, sems):
        for cm, i, o, s in self._split(ins, outs, sems):
            cm.start(i, o, s)

    def finish(self, ins, outs, sems):
        for cm, i, o, s in self._split(ins, outs, sems):
            cm.finish(i, o, s)


def _host_call(body, *, name, grid, operands, in_specs, out_shape, out_specs, scratch_shapes=(), comm=None,
               prefetch=()):
    grid = tuple(grid)
    n_pre, n_in, n_out, n_scr = len(prefetch), len(operands), len(out_shape), len(scratch_shapes)
    nc_in, nc_out = (len(comm.arrays), len(comm.out_shapes)) if comm else (0, 0)
    all_in = list(in_specs) + [ANY] * nc_in
    all_out = list(out_specs) + [ANY] * nc_out
    all_scr = list(scratch_shapes) + (list(comm.scratch) if comm else [])
    all_shape = list(out_shape) + (list(comm.out_shapes) if comm else [])
    kwargs = dict(name=name, compiler_params=_params(len(grid)), out_shape=all_shape)
    if n_pre:
        kwargs["grid_spec"] = pltpu.PrefetchScalarGridSpec(
            num_scalar_prefetch=n_pre, grid=grid, in_specs=all_in, out_specs=all_out, scratch_shapes=all_scr)
    else:
        kwargs.update(in_specs=all_in, out_specs=all_out, scratch_shapes=all_scr)
        if grid:
            kwargs["grid"] = grid
    args = list(prefetch) + list(operands) + (list(comm.arrays) if comm else [])
    if comm is None:
        return list(pl.pallas_call(body, **kwargs)(*args)), []

    def hosted(*refs):
        bounds = [0, n_pre, n_pre + n_in]
        for n in (nc_in, n_out, nc_out, n_scr):
            bounds.append(bounds[-1] + n)
        bounds.append(len(refs))
        pre, ins, cins, outs, couts, scr, sems = [refs[a:b] for a, b in zip(bounds[:-1], bounds[1:])]
        if not grid:
            comm.start(cins, couts, sems)
            body(*pre, *ins, *outs, *scr)
            comm.finish(cins, couts, sems)
            return
        first, last = None, None
        for ax, size in enumerate(grid):
            pid = pl.program_id(ax)
            f, l = pid == 0, pid == size - 1
            first = f if first is None else jnp.logical_and(first, f)
            last = l if last is None else jnp.logical_and(last, l)

        @pl.when(first)
        def _():
            comm.start(cins, couts, sems)

        body(*pre, *ins, *outs, *scr)

        @pl.when(last)
        def _():
            comm.finish(cins, couts, sems)

    res = pl.pallas_call(hosted, **kwargs)(*args)
    return list(res[:n_out]), list(res[n_out:])


def _exchange_only(name, comm):
    def body():
        pass
    return _host_call(body, name=name, grid=(), operands=[], in_specs=[], out_shape=[], out_specs=[], comm=comm)[1]


def _matmul(name, grid, operands, in_specs, pairs, out_shapes, out_specs, epilogue, acc_shapes=(), nk=1,
            prologue=None, comm=None):
    n_in, n_out = len(operands), len(out_shapes)
    prologue = prologue or {}

    def body(*refs):
        ins, outs, accs = refs[:n_in], refs[n_in:n_in + n_out], refs[n_in + n_out:]
        pids = [pl.program_id(ax) for ax in range(len(grid))]

        def operand(i, blk=None):
            v = ins[i][...] if blk is None else ins[i][blk]
            if i in prologue:
                v = prologue[i](v)
            return v.astype(BF16)

        def products():
            vals = {}
            for pair in pairs:
                ai, bi, ci, dn = pair[:4]
                if len(pair) == 5:
                    p = None
                    for blk in range(pair[4]):
                        q = lax.dot_general(operand(ai, blk), operand(bi, blk), dn, preferred_element_type=F32)
                        p = q if p is None else p + q
                else:
                    p = lax.dot_general(operand(ai), operand(bi), dn, preferred_element_type=F32)
                vals[ci] = p if ci not in vals else vals[ci] + p
            return [vals[ci] for ci in sorted(vals)]

        if nk == 1:
            epilogue(products(), ins, outs, pids)
        else:
            k = pids[-1]
            prods = products()

            @pl.when(k == 0)
            def _():
                for acc, p in zip(accs, prods):
                    acc[...] = p

            @pl.when(k > 0)
            def _():
                for acc, p in zip(accs, prods):
                    acc[...] += p

            @pl.when(k == nk - 1)
            def _():
                epilogue([acc[...] for acc in accs], ins, outs, pids)

    return _host_call(
        body, name=name, grid=grid, operands=operands, in_specs=in_specs, out_shape=out_shapes, out_specs=out_specs,
        scratch_shapes=[pltpu.VMEM(s, F32) for s in acc_shapes] if nk > 1 else [], comm=comm)


def _rowwise(name, n_tiles, operands, in_specs, out_shapes, out_specs, red_widths, fn, comm=None):
    n_in, n_out, n_red = len(operands), len(out_shapes), len(red_widths)

    def body(*refs):
        ins, outs, reds = refs[:n_in], refs[n_in:n_in + n_out], refs[n_in + n_out:]
        i = pl.program_id(0)
        vals, sums = fn(i, *[r[...] for r in ins])
        for o, v in zip(outs, vals):
            o[...] = v.astype(o.dtype)
        if n_red:
            @pl.when(i == 0)
            def _():
                for r, s in zip(reds, sums):
                    r[...] = s

            @pl.when(i > 0)
            def _():
                for r, s in zip(reds, sums):
                    r[...] += s

    red_shapes = [jax.ShapeDtypeStruct((1, w), F32) for w in red_widths]
    red_specs = [pl.BlockSpec((1, w), lambda i: (0, 0)) for w in red_widths]
    res, cres = _host_call(
        body, name=name, grid=(n_tiles,), operands=operands, in_specs=in_specs,
        out_shape=list(out_shapes) + red_shapes, out_specs=list(out_specs) + red_specs, comm=comm)
    return res[:n_out], res[n_out:], cres


def _colsum(v):
    return jnp.sum(v, axis=0, keepdims=True)


def _store_all(accs, ins, outs, pids):
    for o, v in zip(outs, accs):
        o[...] = v.astype(o.dtype)


def _row_tile(rows_a, rows_b):
    return _tile(math.gcd(rows_a, rows_b) if rows_b else rows_a, ROW_TILE, SUBLANES)


def _tab_row(d, nlt, rows2):
    return pl.BlockSpec((None, 1, d), lambda i: (jnp.where(i < nlt, rows2[0], rows2[1]), 0, 0))


def _norm_mod_fwd(name, xs, tab, r_gamma, r_shift, r_scale, n_lat, n_ctx):
    rows, d = xs.shape
    tm = _row_tile(n_lat, n_ctx)
    nlt = n_lat // tm

    def fn(i, x, g, sh, sc):
        xh = x * lax.rsqrt(jnp.mean(x * x, axis=-1, keepdims=True) + NORM_EPS)
        return [(xh * g) * (1.0 + sc) + sh], []

    (h,), _, _ = _rowwise(
        name, rows // tm, [xs, tab, tab, tab],
        [pl.BlockSpec((tm, d), lambda i: (i, 0)), _tab_row(d, nlt, (r_gamma, r_gamma)), _tab_row(d, nlt, r_shift),
         _tab_row(d, nlt, r_scale)],
        [jax.ShapeDtypeStruct((rows, d), BF16)], [pl.BlockSpec((tm, d), lambda i: (i, 0))], [], fn)
    return h


def _norm_mod_bwd(name, xs, dh, tab, r_gamma, r_scale, n_lat, n_ctx, dres=None):
    rows, d = xs.shape
    tm = _row_tile(n_lat, n_ctx)
    nlt = n_lat // tm
    row = pl.BlockSpec((tm, d), lambda i: (i, 0))

    def fn(i, x, dy, g, sc, *res):
        rstd = lax.rsqrt(jnp.mean(x * x, axis=-1, keepdims=True) + NORM_EPS)
        xh = x * rstd
        dsh = _colsum(dy)
        dsc = _colsum(dy * (xh * g))
        dn = dy * (1.0 + sc)
        dgam = _colsum(dn * xh)
        dxh = dn * g
        dx = rstd * (dxh - xh * jnp.mean(dxh * xh, axis=-1, keepdims=True))
        if res:
            dx = dx + jnp.where(i < nlt, res[0], 0.0)
        lat = (i < nlt).astype(F32)
        return [dx], [dsh * lat, dsc * lat, dsh * (1.0 - lat), dsc * (1.0 - lat), dgam]

    operands = [xs, dh, tab, tab]
    specs = [row, row, _tab_row(d, nlt, (r_gamma, r_gamma)), _tab_row(d, nlt, r_scale)]
    if dres is not None:
        operands.append(dres)
        specs.append(pl.BlockSpec((tm, d), lambda i: (jnp.minimum(i, nlt - 1), 0)))
    (dx,), sums, _ = _rowwise(name, rows // tm, operands, specs,
                              [jax.ShapeDtypeStruct((rows, d), F32)], [row], [d] * 5, fn)
    return dx, sums


def _gate_bwd(name, dx, f, tab, r_gate, coef, n_lat, n_ctx):
    rows, d = dx.shape
    tm = _row_tile(n_lat, n_ctx)
    nlt = n_lat // tm
    row = pl.BlockSpec((tm, d), lambda i: (i, 0))

    def fn(i, dxv, fv, gv):
        dg = _colsum(dxv * fv) * coef
        lat = (i < nlt).astype(F32)
        return [(coef * gv) * dxv], [dg * lat, dg * (1.0 - lat)]

    (df,), sums, _ = _rowwise(
        name, rows // tm, [dx, f, tab],
        [row, row, _tab_row(d, nlt, r_gate)],
        [jax.ShapeDtypeStruct((rows, d), BF16)], [row], [d, d], fn)
    return df, sums


def _select_rows(i, tm, n_lat, v_lat, v_ctx):
    rows = i * tm + lax.broadcasted_iota(jnp.int32, (tm, 1), 0)
    return jnp.where(rows < n_lat, v_lat, v_ctx)


def _ffn_up(tag, h, wg, wu, comm=None):
    rows, d = h.shape
    nb, fs, _ = wg.shape
    tm = _tile(rows, MM_TILE, LANES)
    blk = pl.BlockSpec((None, tm, fs), lambda j, i: (j, i, 0))
    wspec = pl.BlockSpec((None, fs, d), lambda j, i: (j, 0, 0))

    def epilogue(accs, ins, outs, pids):
        a, b = accs
        outs[0][...] = a.astype(BF16)
        outs[1][...] = b.astype(BF16)
        outs[2][...] = (a * _sigmoid(a) * b).astype(BF16)

    hid = jax.ShapeDtypeStruct((nb, rows, fs), BF16)
    (a, b, s), cres = _matmul(
        tag + "_up", (nb, rows // tm), [h, wg, wu],
        [pl.BlockSpec((tm, d), lambda j, i: (i, 0)), wspec, wspec],
        [(0, 1, 0, NT), (0, 2, 1, NT)], [hid, hid, hid], [blk, blk, blk], epilogue, comm=comm)
    return a, b, s, cres


def _ffn_down(tag, s, wd, xs, tab2, r_gate, n_lat, comm=None):
    nb, rows, fs = s.shape
    d = wd.shape[-1]
    tm = _tile(rows, MM_TILE, LANES)
    tn = _tile(d, MM_TILE, LANES)

    def epilogue(accs, ins, outs, pids):
        f = accs[0]
        g = ins[3][...]
        gate = _select_rows(pids[0], tm, n_lat, g[r_gate[0]:r_gate[0] + 1, :], g[r_gate[1]:r_gate[1] + 1, :])
        outs[0][...] = f
        outs[1][...] = ins[2][...] + 0.5 * gate * f

    out = jax.ShapeDtypeStruct((rows, d), F32)
    ospec = pl.BlockSpec((tm, tn), lambda i, n: (i, n))
    (f, xo), cres = _matmul(
        tag + "_down", (rows // tm, d // tn), [s, wd, xs, tab2],
        [pl.BlockSpec((nb, tm, fs), lambda i, n: (0, i, 0)), pl.BlockSpec((nb, fs, tn), lambda i, n: (0, 0, n)),
         ospec, pl.BlockSpec((tab2.shape[0], tn), lambda i, n: (0, n))],
        [(0, 1, 0, NN, nb)], [out, out], [ospec, ospec], epilogue, comm=comm)
    return f, xo, cres


def _ffn_ds(tag, df, wd, a, b, comm=None):
    rows, d = df.shape
    nb, fs, _ = wd.shape
    tm = _tile(rows, MM_TILE, LANES)
    blk = pl.BlockSpec((None, tm, fs), lambda j, i: (j, i, 0))

    def epilogue(accs, ins, outs, pids):
        ds = accs[0]
        av = ins[2][...].astype(F32)
        bv = ins[3][...].astype(F32)
        sg = _sigmoid(av)
        outs[0][...] = (ds * bv * (sg * (1.0 + av * (1.0 - sg)))).astype(BF16)
        outs[1][...] = (ds * (av * sg)).astype(BF16)

    hid = jax.ShapeDtypeStruct((nb, rows, fs), BF16)
    (da, db), cres = _matmul(
        tag + "_ds", (nb, rows // tm), [df, wd, a, b],
        [pl.BlockSpec((tm, d), lambda j, i: (i, 0)), pl.BlockSpec((None, fs, d), lambda j, i: (j, 0, 0)), blk, blk],
        [(0, 1, 0, NT)], [hid, hid], [blk, blk], epilogue, comm=comm)
    return da, db, cres


def _ffn_dwd(tag, s, df, comm=None):
    nb, rows, fs = s.shape
    d = df.shape[-1]
    tn = _tile(d, MM_TILE, LANES)
    (dwd,), cres = _matmul(
        tag + "_dwd", (nb, d // tn), [s, df],
        [pl.BlockSpec((None, rows, fs), lambda j, n: (j, 0, 0)), pl.BlockSpec((rows, tn), lambda j, n: (0, n))],
        [(0, 1, 0, TN)], [jax.ShapeDtypeStruct((nb, fs, d), BF16)],
        [pl.BlockSpec((None, fs, tn), lambda j, n: (j, 0, n))], _store_all, comm=comm)
    return dwd, cres


def _ffn_dwgu(tag, h, da, db, comm=None):
    rows, d = h.shape
    nb, _, fs = da.shape
    tno = _tile(d, MM_TILE, LANES)
    full = pl.BlockSpec((None, rows, fs), lambda j, m: (j, 0, 0))
    wshape = jax.ShapeDtypeStruct((nb, fs, d), BF16)
    wblk = pl.BlockSpec((None, fs, tno), lambda j, m: (j, 0, m))
    (dwg, dwu), cres = _matmul(
        tag + "_dwgu", (nb, d // tno), [h, da, db],
        [pl.BlockSpec((rows, tno), lambda j, m: (0, m)), full, full],
        [(1, 0, 0, TN), (2, 0, 1, TN)], [wshape, wshape], [wblk, wblk], _store_all, comm=comm)
    return dwg, dwu, cres


def _ffn_dh(tag, da, db, wg, wu, comm=None):
    nb, rows, fs = da.shape
    d = wg.shape[2]
    tm = _tile(rows, MM_TILE, LANES)
    tn = _tile(d, MM_TILE_NT, LANES)
    aspec = pl.BlockSpec((nb, tm, fs), lambda i, n: (0, i, 0))
    wspec = pl.BlockSpec((nb, fs, tn), lambda i, n: (0, 0, n))
    (dh,), cres = _matmul(
        tag + "_dh", (rows // tm, d // tn), [da, wg, db, wu], [aspec, wspec, aspec, wspec],
        [(0, 1, 0, NN, nb), (2, 3, 0, NN, nb)], [jax.ShapeDtypeStruct((rows, d), F32)],
        [pl.BlockSpec((tm, tn), lambda i, n: (i, n))], _store_all, comm=comm)
    return dh, cres


def _rope_tables(n_lat, n_ctx):
    half = LANES // 4
    inv_freq = (np.float32(ROPE_THETA) ** (-np.arange(half, dtype=np.float32) / np.float32(half))).astype(np.float32)
    pos = np.arange(n_lat)
    ang_r = (pos // GRID_W).astype(np.float32)[:, None] * inv_freq
    ang_c = (pos % GRID_W).astype(np.float32)[:, None] * inv_freq
    cos_l = np.concatenate([np.cos(ang_r)] * 2 + [np.cos(ang_c)] * 2, axis=1)
    sin_l = np.concatenate([-np.sin(ang_r), np.sin(ang_r), -np.sin(ang_c), np.sin(ang_c)], axis=1)
    cos_all = np.concatenate([cos_l, np.ones((n_ctx, LANES), np.float32)], axis=0).astype(np.float32)
    sin_all = np.concatenate([sin_l, np.zeros((n_ctx, LANES), np.float32)], axis=0).astype(np.float32)
    return jnp.asarray(cos_all), jnp.asarray(sin_all)


def _swap_halves(x):
    lane = lax.broadcasted_iota(jnp.int32, x.shape, 1)
    return jnp.where((lane % 64) < 32, pltpu.roll(x, 96, 1), pltpu.roll(x, 32, 1))


def _heads_spec(tq, hb, width, first_block):
    per_shard = width // (hb * LANES)

    def index(k, i):
        blk = first_block + k
        return blk // per_shard, i, blk % per_shard
    return pl.BlockSpec((None, tq, hb * LANES), index)


def _qk_prep(name, src, first_block, hb, n_heads, rows, g, cos_t, sin_t):
    tq = _tile(rows, HEAD_ROW_TILE, SUBLANES)
    tab = pl.BlockSpec((tq, LANES), lambda k, i: (i, 0))

    def body(x_ref, g_ref, c_ref, s_ref, o_ref):
        for h in range(hb):
            x = x_ref[:, h * LANES:(h + 1) * LANES]
            n = x * lax.rsqrt(jnp.mean(x * x, axis=-1, keepdims=True) + NORM_EPS) * g_ref[...]
            o_ref[h] = (n * c_ref[...] + _swap_halves(n) * s_ref[...]).astype(BF16)

    return pl.pallas_call(
        body, name=name, grid=(n_heads // hb, rows // tq),
        in_specs=[_heads_spec(tq, hb, src.shape[-1], first_block), pl.BlockSpec((1, LANES), lambda k, i: (0, 0)),
                  tab, tab],
        out_specs=pl.BlockSpec((hb, tq, LANES), lambda k, i: (k, i, 0)),
        out_shape=jax.ShapeDtypeStruct((n_heads, rows, LANES), BF16), compiler_params=_params(2),
    )(src, g, cos_t, sin_t)


def _qk_prep_bwd(name, dy, src, first_block, hb, n_heads, rows, g, cos_t, sin_t):
    tq = _tile(rows, HEAD_ROW_TILE, SUBLANES)
    tab = pl.BlockSpec((tq, LANES), lambda k, i: (i, 0))

    def body(dy_ref, x_ref, g_ref, c_ref, s_ref, dx_ref, dg_ref):
        g = g_ref[...]
        dg = None
        for h in range(hb):
            x = x_ref[:, h * LANES:(h + 1) * LANES]
            dyv = dy_ref[h]
            rstd = lax.rsqrt(jnp.mean(x * x, axis=-1, keepdims=True) + NORM_EPS)
            xh = x * rstd
            dn = dyv * c_ref[...] + _swap_halves(dyv * s_ref[...])
            dxh = dn * g
            dx = rstd * (dxh - xh * jnp.mean(dxh * xh, axis=-1, keepdims=True))
            dx_ref[:, h * LANES:(h + 1) * LANES] = dx.astype(BF16)
            part = _colsum(dn * xh)
            dg = part if dg is None else dg + part
        first = jnp.logical_and(pl.program_id(0) == 0, pl.program_id(1) == 0)

        @pl.when(first)
        def _():
            dg_ref[...] = dg

        @pl.when(jnp.logical_not(first))
        def _():
            dg_ref[...] += dg

    return pl.pallas_call(
        body, name=name, grid=(n_heads // hb, rows // tq),
        in_specs=[pl.BlockSpec((hb, tq, LANES), lambda k, i: (k, i, 0)),
                  _heads_spec(tq, hb, src.shape[-1], first_block),
                  pl.BlockSpec((1, LANES), lambda k, i: (0, 0)), tab, tab],
        out_specs=[pl.BlockSpec((None, tq, hb * LANES), lambda k, i: (k, i, 0)),
                   pl.BlockSpec((1, LANES), lambda k, i: (0, 0))],
        out_shape=[jax.ShapeDtypeStruct((n_heads // hb, rows, hb * LANES), BF16),
                   jax.ShapeDtypeStruct((1, LANES), F32)],
        compiler_params=_params(2),
    )(dy, src, g, cos_t, sin_t)


def _heads_cast(name, src, first_block, hb, n_heads, rows):
    tq = _tile(rows, HEAD_ROW_TILE, SUBLANES)

    def body(x_ref, o_ref):
        for h in range(hb):
            o_ref[h] = x_ref[:, h * LANES:(h + 1) * LANES].astype(BF16)

    return pl.pallas_call(
        body, name=name, grid=(n_heads // hb, rows // tq),
        in_specs=[_heads_spec(tq, hb, src.shape[-1], first_block)],
        out_specs=pl.BlockSpec((hb, tq, LANES), lambda k, i: (k, i, 0)),
        out_shape=jax.ShapeDtypeStruct((n_heads, rows, LANES), BF16), compiler_params=_params(2),
    )(src)


def _heads_merge(name, src):
    n_heads, rows, _ = src.shape
    tq = _tile(rows, HEAD_ROW_TILE, SUBLANES)

    def body(x_ref, o_ref):
        for h in range(n_heads):
            o_ref[:, h * LANES:(h + 1) * LANES] = x_ref[h].astype(BF16)

    return pl.pallas_call(
        body, name=name, grid=(rows // tq,),
        in_specs=[pl.BlockSpec((n_heads, tq, LANES), lambda i: (0, i, 0))],
        out_specs=pl.BlockSpec((tq, n_heads * LANES), lambda i: (i, 0)),
        out_shape=jax.ShapeDtypeStruct((rows, n_heads * LANES), BF16), compiler_params=_params(1),
    )(src)


def _attn_fwd(q, k, v, q_per_kv, comm=None):
    nq, l, _ = q.shape
    s_len = k.shape[1]
    tq = _tile(l, ROW_TILE, SUBLANES)
    scale = LANES ** -0.5
    hp = ATTN_FWD_HEADS if q_per_kv % ATTN_FWD_HEADS == 0 else 1
    kv = pl.BlockSpec((None, s_len, LANES), lambda h, i: ((h * hp) // q_per_kv, 0, 0))

    def body(q_ref, k_ref, v_ref, o_ref):
        for h in range(hp):
            s = lax.dot_general(q_ref[h], k_ref[...], NT, preferred_element_type=F32)
            p = jnp.exp2((s - jnp.max(s, axis=-1, keepdims=True)) * (scale * LOG2E))
            den = jnp.sum(p, axis=-1, keepdims=True)
            o = jnp.dot(p.astype(BF16), v_ref[...], preferred_element_type=F32)
            o_ref[:, h * LANES:(h + 1) * LANES] = (o * (1.0 / den)).astype(BF16)

    (o,), cres = _host_call(
        body, name="attn_fwd", grid=(nq // hp, l // tq), operands=[q, k, v],
        in_specs=[pl.BlockSpec((hp, tq, LANES), lambda h, i: (h, i, 0)), kv, kv],
        out_shape=[jax.ShapeDtypeStruct((l, nq * LANES), BF16)],
        out_specs=[pl.BlockSpec((tq, hp * LANES), lambda h, i: (i, h))], comm=comm)
    return o, cres


def _attn_bwd(q, k, v, do, q_per_kv, comm=None):
    nq, l, _ = q.shape
    nkv, s_len, _ = k.shape
    tq = _tile(l, ROW_TILE, SUBLANES)
    scale = LANES ** -0.5
    hp = ATTN_BWD_HEADS if q_per_kv % ATTN_BWD_HEADS == 0 else 1
    kv = pl.BlockSpec((None, s_len, LANES), lambda g, r, i: (g, 0, 0))
    qs = pl.BlockSpec((hp, tq, LANES), lambda g, r, i: (g * (q_per_kv // hp) + r, i, 0))

    def body(q_ref, k_ref, v_ref, do_ref, dq_ref, dk_ref, dv_ref):
        kvv, vv = k_ref[...], v_ref[...]
        dk_new = dv_new = None
        for h in range(hp):
            qv, dov = q_ref[h], do_ref[:, h * LANES:(h + 1) * LANES]
            st = lax.dot_general(kvv, qv, NT, preferred_element_type=F32)
            e = jnp.exp2((st - jnp.max(st, axis=0, keepdims=True)) * (scale * LOG2E))
            pt = e * (1.0 / jnp.sum(e, axis=0, keepdims=True))
            dpt = lax.dot_general(vv, dov, NT, preferred_element_type=F32)
            delta = jnp.sum(pt * dpt, axis=0, keepdims=True)
            dst = (pt * (dpt - delta)).astype(BF16)
            dq_ref[h] = lax.dot_general(dst, kvv, TN, preferred_element_type=F32) * scale
            dk_h = jnp.dot(dst, qv, preferred_element_type=F32) * scale
            dv_h = jnp.dot(pt.astype(BF16), dov, preferred_element_type=F32)
            dk_new = dk_h if dk_new is None else dk_new + dk_h
            dv_new = dv_h if dv_new is None else dv_new + dv_h
        first = jnp.logical_and(pl.program_id(1) == 0, pl.program_id(2) == 0)

        @pl.when(first)
        def _():
            dk_ref[...] = dk_new
            dv_ref[...] = dv_new

        @pl.when(jnp.logical_not(first))
        def _():
            dk_ref[...] += dk_new
            dv_ref[...] += dv_new

    (dq, dk, dv), cres = _host_call(
        body, name="attn_bwd", grid=(nkv, q_per_kv // hp, l // tq), operands=[q, k, v, do],
        in_specs=[qs, kv, kv, pl.BlockSpec((tq, hp * LANES), lambda g, r, i: (i, g * (q_per_kv // hp) + r))],
        out_specs=[qs, kv, kv],
        out_shape=[jax.ShapeDtypeStruct((nq, l, LANES), F32), jax.ShapeDtypeStruct((nkv, s_len, LANES), F32),
                   jax.ShapeDtypeStruct((nkv, s_len, LANES), F32)], comm=comm)
    return dq, dk, dv, cres


def _zoh(a_re, a_im, log_dt):
    dt = jnp.exp(log_dt)[..., None]
    mag = jnp.exp(a_re * dt)
    lb_re = mag * jnp.cos(a_im * dt)
    lb_im = mag * jnp.sin(a_im * dt)
    den = a_re * a_re + a_im * a_im
    coef_re = ((lb_re - 1.0) * a_re + lb_im * a_im) / den
    coef_im = (lb_im * a_re - (lb_re - 1.0) * a_im) / den
    return lb_re, lb_im, coef_re, coef_im


def _ssm_discretize(a_re, a_im, log_dt, b_re, b_im):
    lb_re, lb_im, cr, ci = _zoh(a_re, a_im, log_dt)
    bt_re = cr[..., None] * b_re - ci[..., None] * b_im
    bt_im = cr[..., None] * b_im + ci[..., None] * b_re
    return lb_re, lb_im, bt_re, bt_im


def _lambda_powers(a_re, a_im, log_dt, ns):
    dt = jnp.exp(log_dt)[..., None]
    k = jnp.arange(SCAN_TAPS + 1, dtype=F32)[:, None, None, None]
    mag, ang = jnp.exp(k * (a_re * dt)), k * (a_im * dt)
    shape = (SCAN_TAPS + 1, 2, ns, -1)
    return (mag * jnp.cos(ang)).reshape(shape), (mag * jnp.sin(ang)).reshape(shape)


def _slab_mask():
    idx = jnp.arange(SLAB_GROUPS)
    return (idx[:, None] == idx[None, :])[None, None, :, None, :, None]


def _block_diag(m):
    d, g, a, b = m.shape
    ns = g // SLAB_GROUPS
    wide = jnp.where(_slab_mask(), m.reshape(d, ns, SLAB_GROUPS, a, 1, b), 0.0)
    return wide.reshape(d, ns, SLAB_GROUPS * a, SLAB_GROUPS * b)


def _block_diag_extract(m, a, b):
    d, ns = m.shape[:2]
    m = m.reshape(d, ns, SLAB_GROUPS, a, SLAB_GROUPS, b)
    return jnp.sum(jnp.where(_slab_mask(), m, 0.0), axis=4).reshape(d, ns * SLAB_GROUPS, a, b)


def _build_tap_weights(w_ref, base_ref, pw_ref, conj, sw):
    b_re, b_im = base_ref[:, :sw], base_ref[:, sw:]
    for tau in range(SCAN_TAPS):
        p_re, p_im = pw_ref[tau:tau + 1, :sw], pw_ref[tau:tau + 1, sw:]
        if conj:
            p_im = -p_im
        w_ref[tau * LANES:(tau + 1) * LANES, :sw] = (p_re * b_re - p_im * b_im).astype(BF16)
        w_ref[tau * LANES:(tau + 1) * LANES, sw:] = (p_re * b_im + p_im * b_re).astype(BF16)


def _carry_tables(pw_re, pw_im, descending):
    def rows(pw):
        asc = pw[1:]
        per_dir = [asc[::-1, d] if descending[d] else asc[:, d] for d in range(2)]
        return jnp.transpose(jnp.stack(per_dir), (0, 2, 1, 3))
    return jnp.concatenate([rows(pw_re), rows(pw_im)], axis=-1)


def _scan_chunk(x, w_ref, tab_ref, s_ref, carry_ref, descending, t_rows, sw):
    row8 = lax.broadcasted_iota(jnp.int32, x.shape, 0) % SCAN_TAPS
    pieces = [x.astype(BF16)]
    for tau in range(1, SCAN_TAPS):
        if descending:
            sh = jnp.where(row8 <= SCAN_TAPS - 1 - tau, pltpu.roll(x, t_rows - tau, 0), 0.0)
        else:
            sh = jnp.where(row8 >= tau, pltpu.roll(x, tau, 0), 0.0)
        pieces.append(sh.astype(BF16))
    xa = jnp.concatenate(pieces, axis=1)
    s_ref[...] = jnp.dot(xa, w_ref[...], preferred_element_type=F32)
    tab = tab_ref[...]
    t_re, t_im = tab[:, :sw], tab[:, sw:]
    nb = t_rows // SCAN_TAPS
    edge = 0 if descending else SCAN_TAPS - 1

    def step(b, carry):
        h_re, h_im = carry
        r0 = pl.multiple_of(((nb - 1 - b) if descending else b) * SCAN_TAPS, SCAN_TAPS)
        x_re = s_ref[pl.ds(r0, SCAN_TAPS), :sw] + t_re * h_re - t_im * h_im
        x_im = s_ref[pl.ds(r0, SCAN_TAPS), sw:] + t_re * h_im + t_im * h_re
        s_ref[pl.ds(r0, SCAN_TAPS), :sw] = x_re
        s_ref[pl.ds(r0, SCAN_TAPS), sw:] = x_im
        return x_re[edge:edge + 1, :], x_im[edge:edge + 1, :]

    h_re, h_im = lax.fori_loop(0, nb, step, (carry_ref[0:1, :sw], carry_ref[0:1, sw:]))
    carry_ref[0:1, :sw] = h_re
    carry_ref[0:1, sw:] = h_im


def _slab_spec(rows, cols, dr):
    return pl.BlockSpec((None, None, rows, cols), lambda s, i: (dr, s, 0, 0))


def _ssm_fwd(name, dr, u_src, u_shard, bd, pw, tab, ct, descending, chunk_of, t_rows, rows, comm=None):
    _, ns, _, sw2 = bd.shape
    sw = sw2 // 2
    width = ns * LANES
    nchunks = rows // t_rows

    def body(u_ref, bd_ref, pw_ref, tab_ref, ct_ref, y_ref, h_ref, s_ref, carry_ref, w_ref):
        @pl.when(pl.program_id(1) == 0)
        def _():
            carry_ref[...] = jnp.zeros_like(carry_ref)
            _build_tap_weights(w_ref, bd_ref, pw_ref, False, sw)

        _scan_chunk(u_ref[...], w_ref, tab_ref, s_ref, carry_ref, descending, t_rows, sw)
        hb = s_ref[...].astype(BF16)
        h_ref[...] = hb
        y_ref[...] = lax.dot_general(hb, ct_ref[...], NT, preferred_element_type=F32)

    (y, h), cres = _host_call(
        body, name=name, grid=(ns, nchunks), operands=[u_src, bd, pw, tab, ct],
        in_specs=[pl.BlockSpec((None, t_rows, LANES), lambda s, i: (u_shard, chunk_of(i), s)),
                  _slab_spec(LANES, sw2, dr), _slab_spec(2 * SCAN_TAPS, sw2, dr), _slab_spec(SCAN_TAPS, sw2, dr),
                  _slab_spec(LANES, sw2, dr)],
        out_specs=[pl.BlockSpec((t_rows, LANES), lambda s, i: (chunk_of(i), s)),
                   pl.BlockSpec((None, t_rows, sw2), lambda s, i: (s, chunk_of(i), 0))],
        out_shape=[jax.ShapeDtypeStruct((rows, width), F32), jax.ShapeDtypeStruct((ns, rows, sw2), BF16)],
        scratch_shapes=[pltpu.VMEM((t_rows, sw2), F32), pltpu.VMEM((SUBLANES, sw2), F32),
                        pltpu.VMEM((SCAN_TAPS * LANES, sw2), BF16)], comm=comm)
    return y, h, cres


def _ssm_bwd(name, dr, dy, u_src, u_shard, states, ct, pw, tab, bd, descending, chunk_of, t_rows, rows, comm=None):
    _, ns, _, sw2 = ct.shape
    sw = sw2 // 2
    width = ns * LANES
    nchunks = rows // t_rows

    def body(dy_ref, u_ref, h_ref, ct_ref, pw_ref, tab_ref, bd_ref, du_ref, dbd_ref, dcd_ref, dlam_ref,
             s_ref, carry_ref, gsave_ref, w_ref):
        first = pl.program_id(1) == 0

        @pl.when(first)
        def _():
            carry_ref[...] = jnp.zeros_like(carry_ref)
            gsave_ref[...] = jnp.zeros_like(gsave_ref)
            _build_tap_weights(w_ref, ct_ref, pw_ref, True, sw)

        dyv = dy_ref[...]
        _scan_chunk(dyv, w_ref, tab_ref, s_ref, carry_ref, descending, t_rows, sw)
        g = s_ref[...]
        gb = g.astype(BF16)
        du_ref[...] = lax.dot_general(gb, bd_ref[...], NT, preferred_element_type=F32)
        dbd = lax.dot_general(u_ref[...].astype(BF16), gb, TN, preferred_element_type=F32)
        hb = h_ref[...]
        dcd = lax.dot_general(hb, dyv.astype(BF16), TN, preferred_element_type=F32)
        hf = hb.astype(F32)
        rowid = lax.broadcasted_iota(jnp.int32, hf.shape, 0)
        if descending:
            hp = jnp.where(rowid == 0, 0.0, pltpu.roll(hf, 1, 0))
            h_edge, g_edge = hf[t_rows - 1:t_rows, :], g[0:1, :]
        else:
            hp = jnp.where(rowid == t_rows - 1, 0.0, pltpu.roll(hf, t_rows - 1, 0))
            h_edge, g_edge = hf[0:1, :], g[t_rows - 1:t_rows, :]
        g_re, g_im, hp_re, hp_im = g[:, :sw], g[:, sw:], hp[:, :sw], hp[:, sw:]
        gs = gsave_ref[0:1, :]
        gs_re, gs_im, he_re, he_im = gs[:, :sw], gs[:, sw:], h_edge[:, :sw], h_edge[:, sw:]
        dl_re = _colsum(g_re * hp_re + g_im * hp_im) + gs_re * he_re + gs_im * he_im
        dl_im = _colsum(g_im * hp_re - g_re * hp_im) + gs_im * he_re - gs_re * he_im
        gsave_ref[0:1, :] = g_edge

        @pl.when(first)
        def _():
            dbd_ref[...] = dbd
            dcd_ref[...] = dcd
            dlam_ref[:, :sw] = dl_re
            dlam_ref[:, sw:] = dl_im

        @pl.when(jnp.logical_not(first))
        def _():
            dbd_ref[...] += dbd
            dcd_ref[...] += dcd
            dlam_ref[:, :sw] += dl_re
            dlam_ref[:, sw:] += dl_im

    (du, dbd, dcd, dlam), cres = _host_call(
        body, name=name, grid=(ns, nchunks), operands=[dy, u_src, states, ct, pw, tab, bd],
        in_specs=[pl.BlockSpec((t_rows, LANES), lambda s, i: (chunk_of(i), s)),
                  pl.BlockSpec((None, t_rows, LANES), lambda s, i: (u_shard, chunk_of(i), s)),
                  pl.BlockSpec((None, t_rows, sw2), lambda s, i: (s, chunk_of(i), 0)),
                  _slab_spec(LANES, sw2, dr), _slab_spec(2 * SCAN_TAPS, sw2, dr), _slab_spec(SCAN_TAPS, sw2, dr),
                  _slab_spec(LANES, sw2, dr)],
        out_specs=[pl.BlockSpec((t_rows, LANES), lambda s, i: (chunk_of(i), s)),
                   pl.BlockSpec((None, LANES, sw2), lambda s, i: (s, 0, 0)),
                   pl.BlockSpec((None, sw2, LANES), lambda s, i: (s, 0, 0)),
                   pl.BlockSpec((None, 1, sw2), lambda s, i: (s, 0, 0))],
        out_shape=[jax.ShapeDtypeStruct((rows, width), F32), jax.ShapeDtypeStruct((ns, LANES, sw2), F32),
                   jax.ShapeDtypeStruct((ns, sw2, LANES), F32), jax.ShapeDtypeStruct((ns, 1, sw2), F32)],
        scratch_shapes=[pltpu.VMEM((t_rows, sw2), F32), pltpu.VMEM((SUBLANES, sw2), F32),
                        pltpu.VMEM((SUBLANES, sw2), F32), pltpu.VMEM((SCAN_TAPS * LANES, sw2), BF16)], comm=comm)
    return du, dbd, dcd, dlam, cres


def _mod_fwd(cs, w_mod, b_cols):
    d, width = w_mod.shape
    tn = _tile(width, 768, LANES)

    def epilogue(accs, ins, outs, pids):
        outs[0][...] = accs[0] + ins[2][...]

    return _matmul(
        "mod_fwd", (width // tn,), [cs, w_mod, b_cols],
        [pl.BlockSpec((16, d), lambda n: (0, 0)), pl.BlockSpec((d, tn), lambda n: (0, n)),
         pl.BlockSpec((1, tn), lambda n: (0, n))],
        [(0, 1, 0, NN)], [jax.ShapeDtypeStruct((16, width), F32)], [pl.BlockSpec((16, tn), lambda n: (0, n))],
        epilogue, prologue={0: lambda v: v * _sigmoid(v)})[0][0]


def _mod_bwd_adam(cs, dmod_cols, w, m, v, comm=None):
    d, width = w.shape
    tn = _tile(width, LANES, LANES)
    col = pl.BlockSpec((d, tn), lambda n: (0, n))

    def body(cs_ref, dm_ref, w_ref, m_ref, v_ref, g_ref, dl_ref, nm_ref, nv_ref, ds_ref):
        n = pl.program_id(0)
        lat = dm_ref[pl.ds(0, N_DEV, stride=SUBLANES), :]
        ctx = jnp.sum(dm_ref[pl.ds(1, N_DEV, stride=SUBLANES), :], axis=0, keepdims=True)
        row = lax.broadcasted_iota(jnp.int32, lat.shape, 0)
        dm = jnp.concatenate([lat, jnp.where(row == 0, ctx, 0.0)], axis=0).astype(BF16)
        c = cs_ref[...]
        sc = (c * _sigmoid(c)).astype(BF16)
        wv = w_ref[...]
        g = lax.dot_general(sc, dm, TN, preferred_element_type=F32)
        delta, m2, v2 = _adamw(wv, g, m_ref[...], v_ref[...])
        g_ref[...] = g
        dl_ref[...] = delta
        nm_ref[...] = m2
        nv_ref[...] = v2
        part = lax.dot_general(dm, wv.astype(BF16), NT, preferred_element_type=F32)

        @pl.when(n == 0)
        def _():
            ds_ref[...] = part

        @pl.when(n > 0)
        def _():
            ds_ref[...] += part

    shard = jax.ShapeDtypeStruct((d, width), F32)
    return _host_call(
        body, name="mod_bwd_adam", grid=(width // tn,), operands=[cs, dmod_cols, w, m, v],
        in_specs=[pl.BlockSpec((16, d), lambda n: (0, 0)), pl.BlockSpec((N_DEV * SUBLANES, tn), lambda n: (0, n)),
                  col, col, col],
        out_specs=[col, col, col, col, pl.BlockSpec((16, d), lambda n: (0, 0))],
        out_shape=[shard, shard, shard, shard, jax.ShapeDtypeStruct((16, d), F32)], comm=comm)


def _pair_sum(name, grads, got, core):
    _, rows, cols = grads.shape
    tr = _tile(rows, max(PACKED_SUBLANES, ADAM_BLOCK_BYTES // (cols * 6 * N_CHIPS)), PACKED_SUBLANES)
    blk = pl.BlockSpec((N_CHIPS, tr, cols), lambda i, cc: (0, i, 0))

    def body(core_ref, a_ref, b_ref, o_ref):
        o_ref[...] = (a_ref[...].astype(F32) + b_ref[...].astype(F32)).astype(BF16)

    grid_spec = pltpu.PrefetchScalarGridSpec(
        num_scalar_prefetch=1, grid=(rows // tr,),
        in_specs=[pl.BlockSpec((N_CHIPS, None, tr, cols), lambda i, cc: (0, cc[0], i, 0)), blk], out_specs=blk)
    return pl.pallas_call(
        body, name=name, grid_spec=grid_spec, out_shape=jax.ShapeDtypeStruct((N_CHIPS, rows, cols), BF16),
        compiler_params=_params(1))(core, grads.reshape(N_CHIPS, 2, rows, cols), got)


def _owner_adam(name, items, chip, comm=None):
    plan, start = [], 0
    per_element = 2 * (2 * N_CHIPS + 7 * 4)
    block_elements = ADAM_GROUP_VMEM // (per_element * len(items))
    for _, _, w, _, _ in items:
        rows, cols = w.shape
        tr = _tile(rows, max(PACKED_SUBLANES, block_elements // cols), PACKED_SUBLANES)
        plan.append((start, rows // tr, tr, cols))
        start += rows // tr
    operands, in_specs, out_specs, out_shape = [], [], [], []
    for (first, nt, tr, cols), (p, l, w, m, v) in zip(plan, items):
        def tile(s, first=first, nt=nt):
            return jnp.clip(s - first, 0, nt - 1)
        blk = pl.BlockSpec((tr, cols), lambda s, ch, tile=tile: (tile(s), 0))
        operands += [p, l, w, m, v]
        in_specs += [pl.BlockSpec((None, tr, cols), lambda s, ch, tile=tile: (ch[0], tile(s), 0)),
                     pl.BlockSpec((N_CHIPS - 1, tr, cols), lambda s, ch, tile=tile: (0, tile(s), 0)), blk, blk, blk]
        out_specs += [blk] * 4
        out_shape += [jax.ShapeDtypeStruct(w.shape, F32)] * 4
    n = len(items)

    def body(chip_ref, *refs):
        s = pl.program_id(0)
        for k, (first, nt, _, _) in enumerate(plan):
            p_ref, l_ref, w_ref, m_ref, v_ref = refs[5 * k:5 * k + 5]
            g_ref, dl_ref, nm_ref, nv_ref = refs[5 * n + 4 * k:5 * n + 4 * k + 4]

            @pl.when(jnp.logical_and(s >= first, s < first + nt))
            def _(p_ref=p_ref, l_ref=l_ref, w_ref=w_ref, m_ref=m_ref, v_ref=v_ref,
                  g_ref=g_ref, dl_ref=dl_ref, nm_ref=nm_ref, nv_ref=nv_ref):
                g = p_ref[...].astype(F32)
                for r in range(N_CHIPS - 1):
                    g = g + l_ref[r].astype(F32)
                delta, m2, v2 = _adamw(w_ref[...], g, m_ref[...], v_ref[...])
                g_ref[...] = g
                dl_ref[...] = delta
                nm_ref[...] = m2
                nv_ref[...] = v2

    res, cres = _host_call(body, name=name, grid=(start,), operands=operands, in_specs=in_specs,
                           out_shape=out_shape, out_specs=out_specs, comm=comm, prefetch=[chip])
    return [res[4 * k:4 * k + 4] for k in range(n)], cres


def _sum_adam(name, parts, w, m, v):
    rows, cols = w.shape
    n_parts = parts.shape[0]
    align = PACKED_SUBLANES if parts.dtype == BF16 else SUBLANES
    tr = _tile(rows, max(align, ADAM_BLOCK_BYTES // (cols * 44)), align)
    blk = pl.BlockSpec((tr, cols), lambda i: (i, 0))

    def body(p_ref, w_ref, m_ref, v_ref, g_ref, dl_ref, nm_ref, nv_ref):
        g = p_ref[0].astype(F32)
        for s in range(1, n_parts):
            g = g + p_ref[s].astype(F32)
        delta, m2, v2 = _adamw(w_ref[...], g, m_ref[...], v_ref[...])
        g_ref[...] = g
        dl_ref[...] = delta
        nm_ref[...] = m2
        nv_ref[...] = v2

    out = jax.ShapeDtypeStruct((rows, cols), F32)
    return pl.pallas_call(
        body, name=name, grid=(rows // tr,),
        in_specs=[pl.BlockSpec((n_parts, tr, cols), lambda i: (0, i, 0)), blk, blk, blk],
        out_specs=[blk, blk, blk, blk], out_shape=[out, out, out, out], compiler_params=_params(1),
    )(parts, w, m, v)


def _bias_adam(dmod_all, w, m, v):
    width = w.shape[-1]
    tn = _tile(width, 2048, LANES)
    blk = pl.BlockSpec((1, tn), lambda n: (0, n))

    def body(p_ref, w_ref, m_ref, v_ref, g_ref, dl_ref, nm_ref, nv_ref):
        g = jnp.sum(p_ref[...], axis=0, keepdims=True)
        delta, m2, v2 = _adamw(w_ref[...], g, m_ref[...], v_ref[...])
        g_ref[...] = g
        dl_ref[...] = delta
        nm_ref[...] = m2
        nv_ref[...] = v2

    out = jax.ShapeDtypeStruct((1, width), F32)
    return pl.pallas_call(
        body, name="bias_adam", grid=(width // tn,),
        in_specs=[pl.BlockSpec((dmod_all.shape[0], tn), lambda n: (0, n)), blk, blk, blk],
        out_specs=[blk, blk, blk, blk], out_shape=[out, out, out, out], compiler_params=_params(1),
    )(dmod_all, w, m, v)


def _pack(arrays, total_rows):
    flat = []
    for a in arrays:
        a = a.reshape(-1).astype(F32)
        flat.append(jnp.pad(a, (0, (-a.shape[0]) % LANES)))
    flat = jnp.concatenate(flat).reshape(-1, LANES)
    return jnp.pad(flat, ((0, total_rows - flat.shape[0]), (0, 0)))


def _unpack(packed, shapes):
    out, row = [], 0
    for shp in shapes:
        size = math.prod(shp)
        nrows = -(-size // LANES)
        out.append(packed[row:row + nrows].reshape(-1)[:size].reshape(shp))
        row += nrows
    return out


def kernel(x, c, ctx, c_ctx, w_mod, b_mod, norm_g, w_ffn1_gate, w_ffn1_up, w_ffn1_down, w_in, q_norm_g, k_norm_g, ssm_a_re, ssm_a_im, ssm_log_dt, ssm_b_re, ssm_b_im, ssm_c_re, ssm_c_im, ssm_d, w_glu, b_glu, w_br_attn, w_br_ssm, w_out, w_ffn2_gate, w_ffn2_up, w_ffn2_down, loss_target, m_c_ctx, m_w_mod, m_b_mod, m_norm_g, m_w_ffn1_gate, m_w_ffn1_up, m_w_ffn1_down, m_w_in, m_q_norm_g, m_k_norm_g, m_ssm_a_re, m_ssm_a_im, m_ssm_log_dt, m_ssm_b_re, m_ssm_b_im, m_ssm_c_re, m_ssm_c_im, m_ssm_d, m_w_glu, m_b_glu, m_w_br_attn, m_w_br_ssm, m_w_out, m_w_ffn2_gate, m_w_ffn2_up, m_w_ffn2_down, v_c_ctx, v_w_mod, v_b_mod, v_norm_g, v_w_ffn1_gate, v_w_ffn1_up, v_w_ffn1_down, v_w_in, v_q_norm_g, v_k_norm_g, v_ssm_a_re, v_ssm_a_im, v_ssm_log_dt, v_ssm_b_re, v_ssm_b_im, v_ssm_c_re, v_ssm_c_im, v_ssm_d, v_w_glu, v_b_glu, v_w_br_attn, v_w_br_ssm, v_w_out, v_w_ffn2_gate, v_w_ffn2_up, v_w_ffn2_down):
    _, L, D = x.shape
    Lc = ctx.shape[1]
    R = L + Lc
    MODW = w_mod.shape[-1]
    INS = w_in.shape[-1]
    KVW = INS // 2
    NQ = D // LANES
    NKV = KVW // LANES
    QPK = NQ // NKV
    HBQ = INS // LANES
    G, P, E = ssm_b_re.shape[2:]
    W = G * E
    SW = SLAB_GROUPS * P
    assert E * SLAB_GROUPS == LANES and W == INS and NQ * LANES == D and Lc <= L
    me = 4 * lax.axis_index("x") + 2 * lax.axis_index("y") + lax.axis_index("c")

    x2, ctx2, tgt = x[0], ctx[0], loss_target[0]
    xc0 = jnp.concatenate([x2, ctx2], axis=0)

    def bf(w):
        return w[0].astype(BF16)

    def held_t(w):
        return jnp.swapaxes(w[0], 0, 1)

    def bft(w):
        return held_t(w).astype(BF16)

    def widen(a):
        return jnp.pad(a[0], ((0, 0), (0, D - a.shape[-1])))

    def at_row(a, r, total):
        return jnp.pad(a, ((r, total - r - a.shape[0]), (0, 0)))

    pack_in = (at_row(c, 0, 16) + at_row(widen(norm_g), 1, 16) + at_row(widen(m_norm_g), 4, 16)
               + at_row(widen(v_norm_g), 7, 16))
    (g_in,) = _exchange_only("ag_inputs", _Gather([pack_in]))
    c_all = g_in[:, 0, :]
    dn = D // N_DEV

    def full_norm(k):
        return jnp.transpose(g_in[:, k:k + 3, :dn], (1, 0, 2)).reshape(3, D)

    ng_full, m_ng_full, v_ng_full = full_norm(1), full_norm(4), full_norm(7)
    cs = at_row(c_all, 0, 16) + at_row(c_ctx[None, :], 8, 16)

    b_cols = lax.dynamic_slice_in_dim(b_mod, me * MODW, MODW, axis=1)
    mod_blk = _mod_fwd(cs, w_mod[0], b_cols)
    (mod_g,) = _exchange_only("ag_mod", _Gather([mod_blk]))
    mod_lat = lax.dynamic_index_in_dim(mod_g, me, axis=1, keepdims=False).reshape(9, D)
    mod_ctx = mod_g[:, 8, :].reshape(9, D)[:5]
    tab2 = jnp.concatenate([mod_lat, mod_ctx, ng_full, jnp.zeros((7, D), F32)], axis=0)
    tab3 = tab2[:, None, :]
    SH1, SC1, G1, SH2, SC2, G2, SH3, SC3, G3, MC0, MC1, MC2, MC3, MC4, GAM1, GAM2, GAM3 = range(17)

    wg1, wu1 = _exchange_only("ag_ffn1_gate_up", _Gather([bft(w_ffn1_gate), bft(w_ffn1_up)]))
    h1 = _norm_mod_fwd("nm1_fwd", xc0, tab3, GAM1, (SH1, MC0), (SC1, MC1), L, Lc)
    a1, b1, s1, (wd1,) = _ffn_up("ffn1", h1, wg1, wu1, comm=_Gather([bf(w_ffn1_down)]))
    f1, xc1, (win,) = _ffn_down("ffn1", s1, wd1, xc0, tab2, (G1, MC2), L, comm=_Gather([bf(w_in)]))

    h2 = _norm_mod_fwd("nm2_fwd", xc1, tab3, GAM2, (SH2, MC3), (SC2, MC4), L, Lc)
    tm = _tile(R, MM_TILE, LANES)
    tml = _tile(L, MM_TILE, LANES)

    (p01,), _ = _matmul(
        "in_proj_kvu", (2, R // tm), [h2, win],
        [pl.BlockSpec((tm, D), lambda j, i: (i, 0)), pl.BlockSpec((None, D, INS), lambda j, i: (j, 0, 0))],
        [(0, 1, 0, NN)], [jax.ShapeDtypeStruct((2, R, INS), F32)],
        [pl.BlockSpec((None, tm, INS), lambda j, i: (j, i, 0))], _store_all)
    (p27,), (wglu, wbra) = _matmul(
        "in_proj_qg", (6, L // tml), [h2, win],
        [pl.BlockSpec((tml, D), lambda j, i: (i, 0)), pl.BlockSpec((None, D, INS), lambda j, i: (j + 2, 0, 0))],
        [(0, 1, 0, NN)], [jax.ShapeDtypeStruct((6, L, INS), F32)],
        [pl.BlockSpec((None, tml, INS), lambda j, i: (j, i, 0))], _store_all,
        comm=_Gather([bf(w_glu), bf(w_br_attn)]))
    wglu2 = wglu.reshape(W, W)
    wbra2 = wbra.reshape(D, D)

    cos_all, sin_all = _rope_tables(L, Lc)
    cos_l, sin_l = cos_all[:L], sin_all[:L]

    q_rot = _qk_prep("q_prep", p27, 0, HBQ, NQ, L, q_norm_g, cos_l, sin_l)
    k_rot = _qk_prep("k_prep", p01, 0, NKV, NKV, R, k_norm_g, cos_all, sin_all)
    v_hd = _heads_cast("v_heads", p01, 1, NKV, NKV, R)
    attn, (wbrs, wout) = _attn_fwd(q_rot, k_rot, v_hd, QPK, comm=_Gather([bf(w_br_ssm), bf(w_out)]))
    wout2 = wout.reshape(D, D)

    t_rows = _tile(math.gcd(L, Lc), ROW_TILE, SUBLANES)
    nl, ncx = L // t_rows, Lc // t_rows
    nch = nl + ncx
    ns = G // SLAB_GROUPS
    ssm_prim = (ssm_a_re[0], ssm_a_im[0], ssm_log_dt[0], ssm_b_re[0], ssm_b_im[0])
    _, _, bt_re, bt_im = _ssm_discretize(*ssm_prim)
    pw_re, pw_im = _lambda_powers(ssm_a_re[0], ssm_a_im[0], ssm_log_dt[0], ns)
    bd_re = _block_diag(jnp.swapaxes(bt_re, 2, 3))
    bd_im = _block_diag(jnp.swapaxes(bt_im, 2, 3))
    ct_re = _block_diag(ssm_c_re[0])
    ct_im = _block_diag(-ssm_c_im[0])
    fwd_desc = (False, True)
    adj_desc = (True, False)
    s_bd = jnp.concatenate([bd_re, bd_im], axis=-1)
    s_ct = jnp.concatenate([ct_re, ct_im], axis=-1)
    s_bd16, s_ct16 = s_bd.astype(BF16), s_ct.astype(BF16)
    s_pw = jnp.pad(jnp.transpose(jnp.concatenate([pw_re, pw_im], axis=-1), (1, 2, 0, 3)),
                   ((0, 0), (0, 0), (0, 2 * SCAN_TAPS - SCAN_TAPS - 1), (0, 0)))
    s_tab = _carry_tables(pw_re, pw_im, fwd_desc)
    s_tabc = _carry_tables(pw_re, -pw_im, adj_desc)
    order = [lambda i: (i + nl) % nch, lambda i: nch - 1 - i]
    order_adj = [lambda i: (nch - 1 - i + nl) % nch, lambda i: i]
    y0, st0, (wg2,) = _ssm_fwd("ssm_fwd0", 0, p01, 1, s_bd, s_pw, s_tab, s_ct16, fwd_desc[0], order[0], t_rows, R,
                               comm=_Gather([bft(w_ffn2_gate)]))
    y1, st1, (wu2,) = _ssm_fwd("ssm_fwd1", 1, p01, 1, s_bd, s_pw, s_tab, s_ct16, fwd_desc[1], order[1], t_rows, R,
                               comm=_Gather([bft(w_ffn2_up)]))
    states = [st0, st1]

    tr = _row_tile(L, 0)
    rowW = pl.BlockSpec((tr, W), lambda i: (i, 0))
    vecW = pl.BlockSpec((1, W), lambda i: (0, 0))
    u_lat = pl.BlockSpec((None, tr, W), lambda i: (1, i, 0))

    def ssm_post(i, u, ya, yb, dvec):
        sv = dvec * u + ya + yb
        return [sv, _gelu(sv)], []

    (ssm_out, yg), _, _ = _rowwise(
        "ssm_post", L // tr, [p01, y0, y1, ssm_d], [u_lat, rowW, rowW, vecW],
        [jax.ShapeDtypeStruct((L, W), F32), jax.ShapeDtypeStruct((L, W), BF16)], [rowW, rowW], [], ssm_post)

    tnw = _tile(W, MM_TILE, LANES)

    def glu_epilogue(accs, ins, outs, pids):
        z = accs[0] + ins[3][...]
        outs[0][...] = z
        outs[1][...] = (_gelu(ins[2][...]) * _sigmoid(z)).astype(BF16)

    (z_glu, y2), _ = _matmul(
        "glu", (L // tml, W // tnw), [yg, wglu2, ssm_out, b_glu],
        [pl.BlockSpec((tml, W), lambda i, n: (i, 0)), pl.BlockSpec((W, tnw), lambda i, n: (0, n)),
         pl.BlockSpec((tml, tnw), lambda i, n: (i, n)), pl.BlockSpec((1, tnw), lambda i, n: (0, n))],
        [(0, 1, 0, NN)], [jax.ShapeDtypeStruct((L, W), F32), jax.ShapeDtypeStruct((L, W), BF16)],
        [pl.BlockSpec((tml, tnw), lambda i, n: (i, n))] * 2, glu_epilogue)

    tnd = _tile(D, MM_TILE, LANES)
    out_ld = pl.BlockSpec((tml, tnd), lambda i, n: (i, n))
    (br_a,), _ = _matmul(
        "br_attn", (L // tml, D // tnd), [attn, wbra2],
        [pl.BlockSpec((tml, D), lambda i, n: (i, 0)), pl.BlockSpec((D, tnd), lambda i, n: (0, n))],
        [(0, 1, 0, NN)], [jax.ShapeDtypeStruct((L, D), F32)], [out_ld], _store_all)

    cb = wbrs.shape[-1]
    gpb = INS // cb

    def gate_spec(first_shard):
        return pl.BlockSpec((None, tml, cb), lambda i, j: (first_shard + j // gpb, i, j % gpb))

    def merge_epilogue(accs, ins, outs, pids):
        br = accs[0]
        outs[0][...] = br
        outs[1][...] = (_sigmoid(ins[2][...]) * ins[4][...] + _sigmoid(ins[3][...]) * br).astype(BF16)

    col_blk = pl.BlockSpec((tml, cb), lambda i, j: (i, j))
    (br_s, merged), _ = _matmul(
        "br_ssm_merge", (L // tml, N_DEV), [y2, wbrs, p27, p27, br_a],
        [pl.BlockSpec((tml, W), lambda i, j: (i, 0)), pl.BlockSpec((None, W, cb), lambda i, j: (j, 0, 0)),
         gate_spec(2), gate_spec(4), col_blk],
        [(0, 1, 0, NN)], [jax.ShapeDtypeStruct((L, D), F32), jax.ShapeDtypeStruct((L, D), BF16)],
        [col_blk, col_blk], merge_epilogue)

    def out_epilogue(accs, ins, outs, pids):
        outs[0][...] = accs[0]
        outs[1][...] = ins[2][...] + ins[3][G2:G2 + 1, :] * accs[0]

    (mix, x2_), _ = _matmul(
        "out_proj", (L // tml, D // tnd), [merged, wout2, xc1, tab2],
        [pl.BlockSpec((tml, D), lambda i, n: (i, 0)), pl.BlockSpec((D, tnd), lambda i, n: (0, n)), out_ld,
         pl.BlockSpec((tab2.shape[0], tnd), lambda i, n: (0, n))],
        [(0, 1, 0, NN)], [jax.ShapeDtypeStruct((L, D), F32)] * 2, [out_ld, out_ld], out_epilogue)

    h3 = _norm_mod_fwd("nm3_fwd", x2_, tab3, GAM3, (SH3, SH3), (SC3, SC3), L, 0)
    a3, b3, s3, (wd2,) = _ffn_up("ffn2", h3, wg2, wu2, comm=_Gather([bf(w_ffn2_down)]))
    f3, x3, _ = _ffn_down("ffn2", s3, wd2, x2_, tab2, (G3, G3), L)

    trd = _row_tile(L, 0)
    rowD = pl.BlockSpec((trd, D), lambda i: (i, 0))

    def loss_fn(i, yv, t):
        err = yv - t
        return [err * (1.0 / D)], [_colsum(err * err)]

    (dx3,), (sq,), _ = _rowwise("loss", L // trd, [x3, tgt], [rowD, rowD],
                                [jax.ShapeDtypeStruct((L, D), F32)], [rowD], [D], loss_fn)
    loss = lax.psum(0.5 * jnp.sum(sq) / D, ("x", "y", "c"))

    core = lax.axis_index("c").astype(jnp.int32).reshape(1)
    chip = (2 * lax.axis_index("x") + lax.axis_index("y")).astype(jnp.int32).reshape(1)

    def pair_sums(tag, grads, halves):
        return [_pair_sum("pair_%s%d" % (tag, k), g_, h_, core) for k, (g_, h_) in enumerate(zip(grads, halves))]

    df3, (dg3, _) = _gate_bwd("gate3_bwd", dx3, f3, tab3, (G3, G3), 0.5, L, 0)
    dwd2, _ = _ffn_dwd("ffn2b", s3, df3)
    da3, db3, half_wd2 = _ffn_ds("ffn2b", df3, wd2, a3, b3, comm=_SiblingSwap([dwd2]))
    (p_wd2,) = pair_sums("wd2", [dwd2], half_wd2)
    dwg2, dwu2, (l_wd2,) = _ffn_dwgu("ffn2b", h3, da3, db3, comm=_ChipExchange([p_wd2]))
    dh3, half_wgu2 = _ffn_dh("ffn2b", da3, db3, wg2, wu2, comm=_SiblingSwap([dwg2, dwu2]))
    p_wg2, p_wu2 = pair_sums("wgu2", [dwg2, dwu2], half_wgu2)
    dx2, (dsh3, dsc3, _, _, dgam3) = _norm_mod_bwd("nm3_bwd", x2_, dh3, tab3, GAM3, (SC3, SC3), L, 0, dres=dx3)

    dmix, (dg2, _) = _gate_bwd("gate2_bwd", dx2, mix, tab3, (G2, G2), 1.0, L, 0)

    def dmerged_epilogue(accs, ins, outs, pids):
        dm = accs[0]
        ga, gs = _sigmoid(ins[2][...]), _sigmoid(ins[3][...])
        outs[0][...] = (ga * dm).astype(BF16)
        outs[1][...] = (gs * dm).astype(BF16)
        outs[2][...] = (dm * ins[4][...] * ga * (1.0 - ga)).astype(BF16)
        outs[3][...] = (dm * ins[5][...] * gs * (1.0 - gs)).astype(BF16)

    dgate_spec = pl.BlockSpec((None, tml, cb), lambda i, j: (j // gpb, i, j % gpb))
    (d_br_a, d_br_s, dg_a, dg_s), _ = _matmul(
        "dmerged", (L // tml, N_DEV), [dmix, wout2, p27, p27, br_a, br_s],
        [pl.BlockSpec((tml, D), lambda i, j: (i, 0)), pl.BlockSpec((cb, D), lambda i, j: (j, 0)),
         gate_spec(2), gate_spec(4), col_blk, col_blk],
        [(0, 1, 0, NT)],
        [jax.ShapeDtypeStruct((L, D), BF16)] * 2 + [jax.ShapeDtypeStruct((2, L, INS), BF16)] * 2,
        [col_blk, col_blk, dgate_spec, dgate_spec], dmerged_epilogue)

    def wgrad(name, a_mat, b_mat, tmo, tno):
        ka, ma = a_mat.shape
        _, nb_ = b_mat.shape
        return _matmul(
            name, (ma // tmo, nb_ // tno), [a_mat, b_mat],
            [pl.BlockSpec((ka, tmo), lambda m, n: (0, m)), pl.BlockSpec((ka, tno), lambda m, n: (0, n))],
            [(0, 1, 0, TN)], [jax.ShapeDtypeStruct((ma, nb_), BF16)],
            [pl.BlockSpec((tmo, tno), lambda m, n: (m, n))], _store_all)[0][0]

    dwout = wgrad("dw_out", merged, dmix, tnd, tnd)
    dwbra = wgrad("dw_br_attn", attn, d_br_a, tnd, tnd)
    (d_attn,), _ = _matmul(
        "d_attn", (L // tml, D // tnd), [d_br_a, wbra2],
        [pl.BlockSpec((tml, D), lambda i, n: (i, 0)), pl.BlockSpec((tnd, D), lambda i, n: (n, 0))],
        [(0, 1, 0, NT)], [jax.ShapeDtypeStruct((L, D), BF16)], [out_ld], _store_all)

    (dwbrs,), _ = _matmul(
        "dw_br_ssm", (N_DEV,), [y2, d_br_s],
        [pl.BlockSpec((L, W), lambda j: (0, 0)), pl.BlockSpec((L, cb), lambda j: (0, j))],
        [(0, 1, 0, TN)], [jax.ShapeDtypeStruct((N_DEV, W, cb), BF16)],
        [pl.BlockSpec((None, W, cb), lambda j: (j, 0, 0))], _store_all)

    def dy2_epilogue(accs, ins, outs, pids):
        dy2 = accs[0]
        sg = _sigmoid(ins[2][...])
        outs[0][...] = dy2 * sg
        outs[1][...] = (dy2 * _gelu(ins[3][...]) * sg * (1.0 - sg)).astype(BF16)

    wn_blk = pl.BlockSpec((tml, tnw), lambda i, n, k: (i, n))
    (dyg1, dz), _ = _matmul(
        "d_y2", (L // tml, W // tnw, N_DEV), [d_br_s, wbrs, z_glu, ssm_out],
        [pl.BlockSpec((tml, cb), lambda i, n, k: (i, k)), pl.BlockSpec((None, tnw, cb), lambda i, n, k: (k, n, 0)),
         wn_blk, wn_blk],
        [(0, 1, 0, NT)], [jax.ShapeDtypeStruct((L, W), F32), jax.ShapeDtypeStruct((L, W), BF16)],
        [wn_blk, wn_blk], dy2_epilogue, acc_shapes=[(tml, tnw)], nk=N_DEV)

    dwglu = wgrad("dw_glu", yg, dz, tnw, tnw)
    mix_grads = [dwout.reshape(N_DEV, D // N_DEV, D), dwbra.reshape(N_DEV, D // N_DEV, D), dwbrs,
                 dwglu.reshape(N_DEV, W // N_DEV, W)]

    def dssm_epilogue(accs, ins, outs, pids):
        outs[0][...] = (accs[0] + ins[2][...]) * _gelu_grad(ins[3][...])

    wn2 = pl.BlockSpec((tml, tnw), lambda i, n: (i, n))
    (dssm,), _ = _matmul(
        "d_ssm", (L // tml, W // tnw), [dz, wglu2, dyg1, ssm_out],
        [pl.BlockSpec((tml, W), lambda i, n: (i, 0)), pl.BlockSpec((tnw, W), lambda i, n: (n, 0)), wn2, wn2],
        [(0, 1, 0, NT)], [jax.ShapeDtypeStruct((L, W), F32)], [wn2], dssm_epilogue)

    dssm_all = jnp.concatenate([dssm, jnp.zeros((Lc, W), F32)], axis=0)
    du0, dbd0, dcd0, dlam0, (l_wg2, *half_mix) = _ssm_bwd(
        "ssm_bwd0", 0, dssm_all, p01, 1, states[0], s_ct, s_pw, s_tabc, s_bd16, adj_desc[0], order_adj[0], t_rows, R,
        comm=_Both([_ChipExchange([p_wg2]), _SiblingSwap(mix_grads)]))
    p_wout, p_wbra, p_wbrs, p_wglu = pair_sums("mix", mix_grads, half_mix)
    du1, dbd1, dcd1, dlam1, (l_wu2,) = _ssm_bwd(
        "ssm_bwd1", 1, dssm_all, p01, 1, states[1], s_ct, s_pw, s_tabc, s_bd16, adj_desc[1], order_adj[1], t_rows, R,
        comm=_ChipExchange([p_wu2]))

    trr = _row_tile(L, Lc)
    nlt = L // trr
    rowR = pl.BlockSpec((trr, W), lambda i: (i, 0))

    def du_fn(i, dua, dub, dsv, dvec, u):
        lat = (i < nlt).astype(F32)
        return [dua + dub + lat * (dvec * dsv)], [lat * _colsum(dsv * u)]

    (du_all,), (d_ssm_d,), _ = _rowwise(
        "du_combine", R // trr, [du0, du1, dssm_all, ssm_d, p01],
        [rowR, rowR, rowR, pl.BlockSpec((1, W), lambda i: (0, 0)), pl.BlockSpec((None, trr, W), lambda i: (1, i, 0))],
        [jax.ShapeDtypeStruct((R, W), BF16)], [rowR], [W], du_fn)

    def dz_sum(i, dzv):
        return [], [_colsum(dzv.astype(F32))]

    _, (d_b_glu,), _ = _rowwise("db_glu", L // tr, [dz], [rowW], [], [], [W], dz_sum)

    dbd, dcd, dlam = jnp.stack([dbd0, dbd1]), jnp.stack([dcd0, dcd1]), jnp.stack([dlam0, dlam1])
    dbt_re = jnp.swapaxes(_block_diag_extract(dbd[..., :SW], E, P), 2, 3)
    dbt_im = jnp.swapaxes(_block_diag_extract(dbd[..., SW:], E, P), 2, 3)
    dl_re, dl_im = dlam[:, :, 0, :SW].reshape(2, G, P), dlam[:, :, 0, SW:].reshape(2, G, P)
    _, vjp = jax.vjp(_ssm_discretize, *ssm_prim)
    d_a_re, d_a_im, d_ldt, d_b_re, d_b_im = vjp((dl_re, dl_im, dbt_re, dbt_im))
    d_c_re = jnp.swapaxes(_block_diag_extract(dcd[:, :, :SW, :], P, E), 2, 3)
    d_c_im = -jnp.swapaxes(_block_diag_extract(dcd[:, :, SW:, :], P, E), 2, 3)

    early_g = [d_a_re, d_a_im, d_ldt, d_b_re, d_b_im, d_c_re, d_c_im, d_ssm_d, d_b_glu]
    early_w = [ssm_a_re, ssm_a_im, ssm_log_dt, ssm_b_re, ssm_b_im, ssm_c_re, ssm_c_im, ssm_d, b_glu]
    early_m = [m_ssm_a_re, m_ssm_a_im, m_ssm_log_dt, m_ssm_b_re, m_ssm_b_im, m_ssm_c_re, m_ssm_c_im, m_ssm_d, m_b_glu]
    early_v = [v_ssm_a_re, v_ssm_a_im, v_ssm_log_dt, v_ssm_b_re, v_ssm_b_im, v_ssm_c_re, v_ssm_c_im, v_ssm_d, v_b_glu]
    early_shapes = [a.shape for a in early_w]
    early_rows = -(-sum(-(-math.prod(s) // LANES) for s in early_shapes) // 256) * 256

    dq_rot, dk_rot, dv_hd, (l_wout, l_wbra, l_wbrs, l_wglu, early_parts) = _attn_bwd(
        q_rot, k_rot, v_hd, d_attn, QPK,
        comm=_Both([_ChipExchange([p_wout, p_wbra, p_wbrs, p_wglu]), _Gather([_pack(early_g, early_rows)])]))
    dq_pre, d_qg = _qk_prep_bwd("q_prep_bwd", dq_rot, p27, 0, HBQ, NQ, L, q_norm_g, cos_l, sin_l)
    dk_pre, d_kg = _qk_prep_bwd("k_prep_bwd", dk_rot, p01, 0, NKV, NKV, R, k_norm_g, cos_all, sin_all)
    dv_pre = _heads_merge("dv_merge", dv_hd)

    def lat_blocks(a):
        return jnp.pad(a, ((0, 0), (0, Lc), (0, 0)))

    dp = jnp.concatenate([
        jnp.concatenate([dk_pre[0], dv_pre], axis=1)[None], du_all[None],
        lat_blocks(dq_pre), lat_blocks(dg_a), lat_blocks(dg_s)], axis=0)

    tmo = _tile(D, MM_TILE, LANES)
    (dwin,), _ = _matmul(
        "dw_in", (N_DEV, D // tmo), [h2, dp],
        [pl.BlockSpec((R, tmo), lambda j, m: (0, m)), pl.BlockSpec((None, R, INS), lambda j, m: (j, 0, 0))],
        [(0, 1, 0, TN)], [jax.ShapeDtypeStruct((N_DEV, D, INS), BF16)],
        [pl.BlockSpec((None, tmo, INS), lambda j, m: (j, m, 0))], _store_all)
    tnh = _tile(D, MM_TILE_NT, LANES)
    (dh2,), half_win = _matmul(
        "d_h2", (R // tm, D // tnh), [dp, win],
        [pl.BlockSpec((N_DEV, tm, INS), lambda i, n: (0, i, 0)),
         pl.BlockSpec((N_DEV, tnh, INS), lambda i, n: (0, n, 0))],
        [(0, 1, 0, NT, N_DEV)], [jax.ShapeDtypeStruct((R, D), F32)], [pl.BlockSpec((tm, tnh), lambda i, n: (i, n))],
        _store_all, comm=_SiblingSwap([dwin]))
    (p_win,) = pair_sums("win", [dwin], half_win)
    dxc1, (dsh2, dsc2, dmc3, dmc4, dgam2) = _norm_mod_bwd(
        "nm2_bwd", xc1, dh2, tab3, GAM2, (SC2, MC4), L, Lc, dres=dx2)

    df1, (dg1, dmc2) = _gate_bwd("gate1_bwd", dxc1, f1, tab3, (G1, MC2), 0.5, L, Lc)
    dwd1, _ = _ffn_dwd("ffn1b", s1, df1)
    da1, db1, half_wd1 = _ffn_ds("ffn1b", df1, wd1, a1, b1, comm=_SiblingSwap([dwd1]))
    (p_wd1,) = pair_sums("wd1", [dwd1], half_wd1)
    dwg1, dwu1, (l_wd1,) = _ffn_dwgu("ffn1b", h1, da1, db1, comm=_ChipExchange([p_wd1]))
    dh1, (l_win, *half_wgu1) = _ffn_dh(
        "ffn1b", da1, db1, wg1, wu1, comm=_Both([_ChipExchange([p_win]), _SiblingSwap([dwg1, dwu1])]))
    p_wg1, p_wu1 = pair_sums("wgu1", [dwg1, dwu1], half_wgu1)

    def adam_item(p, l_, w_, m_, v_):
        return (p, l_, w_[0], m_[0], v_[0])

    def adam_item_t(p, l_, w_, m_, v_):
        return (p, l_, held_t(w_), held_t(m_), held_t(v_))

    ready_a = [adam_item(p_wd1, l_wd1, w_ffn1_down, m_w_ffn1_down, v_w_ffn1_down),
               adam_item(p_win, l_win, w_in, m_w_in, v_w_in),
               adam_item(p_wglu, l_wglu, w_glu, m_w_glu, v_w_glu),
               adam_item(p_wbra, l_wbra, w_br_attn, m_w_br_attn, v_w_br_attn),
               adam_item(p_wbrs, l_wbrs, w_br_ssm, m_w_br_ssm, v_w_br_ssm)]
    ready_b = [adam_item(p_wout, l_wout, w_out, m_w_out, v_w_out),
               adam_item_t(p_wg2, l_wg2, w_ffn2_gate, m_w_ffn2_gate, v_w_ffn2_gate),
               adam_item_t(p_wu2, l_wu2, w_ffn2_up, m_w_ffn2_up, v_w_ffn2_up),
               adam_item(p_wd2, l_wd2, w_ffn2_down, m_w_ffn2_down, v_w_ffn2_down)]
    adam_a, (l_wg1,) = _owner_adam("adam_ready_a", ready_a, chip, comm=_ChipExchange([p_wg1]))
    adam_b, (l_wu1,) = _owner_adam("adam_ready_b", ready_b, chip, comm=_ChipExchange([p_wu1]))
    adam_ready = adam_a + adam_b
    dxc0, (dsh1, dsc1, dmc0, dmc1, dgam1) = _norm_mod_bwd(
        "nm1_bwd", xc0, dh1, tab3, GAM1, (SC1, MC1), L, Lc, dres=dxc1)
    grad_x = dxc0[:L][None]

    dmod_lat = jnp.concatenate([dsh1, dsc1, dg1, dsh2, dsc2, dg2, dsh3, dsc3, dg3], axis=1)
    dmod_ctx = jnp.concatenate([dmc0, dmc1, dmc2, dmc3, dmc4, jnp.zeros((1, 4 * D), F32)], axis=1)
    dmod_pack = at_row(dmod_lat, 0, SUBLANES) + at_row(dmod_ctx, 1, SUBLANES)
    (dmod_g,) = _exchange_only("ag_dmod", _Gather([dmod_pack]))
    dmod_all = dmod_g.reshape(N_DEV * SUBLANES, 9 * D)
    dmod_cols = lax.dynamic_slice_in_dim(dmod_all, me * MODW, MODW, axis=1)
    (g_wmod, dl_wmod, nm_wmod, nv_wmod, dsilu), _ = _mod_bwd_adam(
        cs, dmod_cols, w_mod[0], m_w_mod[0], v_w_mod[0])
    sg_cc = jax.nn.sigmoid(c_ctx)
    d_c_ctx = dsilu[8] * (sg_cc * (1.0 + c_ctx * (1.0 - sg_cc)))
    g_bmod, dl_bmod, nm_bmod, nv_bmod = _bias_adam(dmod_all, b_mod, m_b_mod, v_b_mod)

    dgam_all = jnp.concatenate([dgam1, dgam2, dgam3], axis=0)
    late_g = [d_c_ctx, d_qg, d_kg, dgam_all]
    late_w = [c_ctx, q_norm_g, k_norm_g, ng_full]
    late_m = [m_c_ctx, m_q_norm_g, m_k_norm_g, m_ng_full]
    late_v = [v_c_ctx, v_q_norm_g, v_k_norm_g, v_ng_full]
    late_shapes = [a.shape for a in late_w]
    late_rows = -(-sum(-(-math.prod(s) // LANES) for s in late_shapes) // SUBLANES) * SUBLANES
    (late_parts,) = _exchange_only("ag_small_grads", _Gather([_pack(late_g, late_rows)]))
    late_out = _sum_adam("small_adam_late", late_parts, _pack(late_w, late_rows), _pack(late_m, late_rows),
                         _pack(late_v, late_rows))
    early_out = _sum_adam("small_adam_s5", early_parts, _pack(early_w, early_rows), _pack(early_m, early_rows),
                          _pack(early_v, early_rows))

    def my_norm_cols(a):
        return lax.dynamic_slice_in_dim(a, me * dn, dn, axis=1)[None]

    small = []
    for lo, eo in zip(late_out, early_out):
        c_ctx_, qg_, kg_, ng_ = _unpack(lo, late_shapes)
        small.append([c_ctx_, qg_, kg_] + _unpack(eo, early_shapes) + [my_norm_cols(ng_)])
    sm_g, sm_dl, sm_m, sm_v = small

    adam_last, _ = _owner_adam(
        "adam_last", [adam_item_t(p_wg1, l_wg1, w_ffn1_gate, m_w_ffn1_gate, v_w_ffn1_gate),
                      adam_item_t(p_wu1, l_wu1, w_ffn1_up, m_w_ffn1_up, v_w_ffn1_up)], chip)
    transposed = (0, 1, 8, 9)
    big_out = [[(jnp.swapaxes(o, 0, 1) if k in transposed else o)[None] for o in grp_]
               for k, grp_ in enumerate(adam_last + adam_ready)]

    def leaf(kind):
        sm = (sm_g, sm_dl, sm_m, sm_v)[kind]
        mod = (g_wmod, dl_wmod, nm_wmod, nv_wmod)[kind][None]
        bmod = (g_bmod, dl_bmod, nm_bmod, nv_bmod)[kind]
        big = [b[kind] for b in big_out]
        (c_ctx_, qg_, kg_, a_re_, a_im_, ldt_, b_re_, b_im_, c_re_, c_im_, sd_, bglu_, ng_) = sm
        return [c_ctx_, mod, bmod, ng_, big[0], big[1], big[2], big[3], qg_, kg_, a_re_, a_im_, ldt_, b_re_, b_im_,
                c_re_, c_im_, sd_, big[4], bglu_, big[5], big[6], big[7], big[8], big[9], big[10]]

    return tuple([loss, grad_x] + leaf(0) + leaf(1) + leaf(2) + leaf(3))
```

```python
import math

import jax
import jax.numpy as jnp
import numpy as np
from jax import lax
from jax.experimental import pallas as pl
from jax.experimental.pallas import tpu as pltpu

F32 = jnp.float32
BF16 = jnp.bfloat16

N_DEV = 8
N_CHIPS = 4
LANES = 128
SUBLANES = 8
PACKED_SUBLANES = 16
VMEM_LIMIT = 56 * 1024 * 1024
MM_TILE = 512
MM_TILE_NT = 256
ROW_TILE = 256
HEAD_ROW_TILE = 512
ATTN_BWD_HEADS = 4
ATTN_FWD_HEADS = 4
ADAM_BLOCK_BYTES = 4 * 1024 * 1024
ADAM_GROUP_VMEM = 36 * 1024 * 1024

NORM_EPS = 1e-6
GRID_W = 64
ROPE_THETA = 10000.0
SCAN_TAPS = SUBLANES
SLAB_GROUPS = 8

ADAM_LR = 0.001
ADAM_B1 = 0.9
ADAM_B2 = 0.999
ADAM_EPS = 1e-08
ADAM_WD = 0.01
ADAM_STEP = 10

NN = (((1,), (0,)), ((), ()))
NT = (((1,), (1,)), ((), ()))
TN = (((0,), (0,)), ((), ()))

MESH = pl.DeviceIdType.MESH
ANY = pl.BlockSpec(memory_space=pl.ANY)


def _tile(n, cap, align):
    best = None
    for t in range(align, min(n, cap) + 1, align):
        if n % t == 0:
            best = t
    return n if best is None else best


def _params(n_grid):
    return pltpu.CompilerParams(dimension_semantics=("arbitrary",) * n_grid, vmem_limit_bytes=VMEM_LIMIT)


def _sigmoid(x):
    return 1.0 / (1.0 + jnp.exp(-x))


LOG2E = math.log2(math.e)
GELU_K = math.sqrt(2.0 / math.pi)
GELU_C = 0.044715


def _gelu(x):
    return 0.5 * x * (1.0 + jnp.tanh(GELU_K * (x + GELU_C * x * x * x)))


def _gelu_grad(x):
    t = jnp.tanh(GELU_K * (x + GELU_C * x * x * x))
    return 0.5 * (1.0 + t) + 0.5 * x * (1.0 - t * t) * GELU_K * (1.0 + 3.0 * GELU_C * x * x)


def _adamw(w, g, m, v):
    m2 = ADAM_B1 * m + (1.0 - ADAM_B1) * g
    v2 = ADAM_B2 * v + (1.0 - ADAM_B2) * (g * g)
    m_hat = m2 / (1.0 - ADAM_B1 ** ADAM_STEP)
    v_hat = v2 / (1.0 - ADAM_B2 ** ADAM_STEP)
    delta = -ADAM_LR * (m_hat / (jnp.sqrt(v_hat) + ADAM_EPS) + ADAM_WD * w)
    return delta, m2, v2


def _position():
    return lax.axis_index("x"), lax.axis_index("y"), lax.axis_index("c")


class _Gather:
    def __init__(self, arrays):
        self.arrays = list(arrays)
        n = len(self.arrays)
        self.out_shapes = [jax.ShapeDtypeStruct((N_DEV,) + a.shape, a.dtype) for a in self.arrays]
        self.scratch = [pltpu.SemaphoreType.DMA((n, 7)), pltpu.SemaphoreType.DMA((n, 7)),
                        pltpu.SemaphoreType.DMA((n,))]

    def _plan(self, ins, outs, sems):
        send, recv, local = sems
        x, y, c = _position()
        me, sibling = (x, y, c), (x, y, 1 - c)
        chips = [(1 - x, y), (x, 1 - y), (1 - x, 1 - y)]

        def slot(a, p):
            return outs[a].at[4 * p[0] + 2 * p[1] + p[2]]

        def copy(a, k, block, to, src=None):
            dst = slot(a, block)
            return pltpu.make_async_remote_copy(
                src_ref=dst if src is None else src, dst_ref=dst,
                send_sem=send.at[a, k], recv_sem=recv.at[a, k], device_id=to, device_id_type=MESH)

        mine = [pltpu.make_async_copy(ins[a], slot(a, me), local.at[a]) for a in range(len(ins))]
        return me, sibling, chips, c, copy, mine

    def start(self, ins, outs, sems):
        me, sibling, chips, c, copy, mine = self._plan(ins, outs, sems)
        for cp in mine:
            cp.start()
        for a in range(len(ins)):
            copy(a, 0, me, sibling, src=ins[a]).start()
            for j, chip in enumerate(chips):
                copy(a, 1 + j, me, (*chip, c), src=ins[a]).start()

    def finish(self, ins, outs, sems):
        me, sibling, chips, c, copy, mine = self._plan(ins, outs, sems)
        n = len(ins)
        for j, chip in enumerate(chips):
            for a in range(n):
                copy(a, 1 + j, (*chip, c), me).wait_recv()
                copy(a, 4 + j, (*chip, c), sibling).start()
        for a in range(n):
            copy(a, 0, sibling, me).wait_recv()
        for j, chip in enumerate(chips):
            for a in range(n):
                copy(a, 4 + j, (*chip, 1 - c), me).wait_recv()
        for a in range(n):
            copy(a, 0, me, sibling, src=ins[a]).wait_send()
            for j, chip in enumerate(chips):
                copy(a, 1 + j, me, (*chip, c), src=ins[a]).wait_send()
                copy(a, 4 + j, (*chip, c), sibling).wait_send()
        for cp in mine:
            cp.wait()


class _SiblingSwap:
    def __init__(self, arrays):
        self.arrays = list(arrays)
        n = len(self.arrays)
        self.out_shapes = [jax.ShapeDtypeStruct((N_CHIPS,) + a.shape[1:], a.dtype) for a in self.arrays]
        self.scratch = [pltpu.SemaphoreType.DMA((n, N_CHIPS)), pltpu.SemaphoreType.DMA((n, N_CHIPS))]

    def _plan(self, ins, outs, sems):
        send, recv = sems
        x, y, c = _position()
        return [pltpu.make_async_remote_copy(
            src_ref=ins[a].at[2 * j + 1 - c], dst_ref=outs[a].at[j],
            send_sem=send.at[a, j], recv_sem=recv.at[a, j], device_id=(x, y, 1 - c), device_id_type=MESH)
            for a in range(len(ins)) for j in range(N_CHIPS)]

    def start(self, ins, outs, sems):
        for cp in self._plan(ins, outs, sems):
            cp.start()

    def finish(self, ins, outs, sems):
        copies = self._plan(ins, outs, sems)
        for cp in copies:
            cp.wait_recv()
        for cp in copies:
            cp.wait_send()


class _ChipExchange:
    def __init__(self, arrays):
        self.arrays = list(arrays)
        n = len(self.arrays)
        self.out_shapes = [jax.ShapeDtypeStruct((N_CHIPS - 1,) + a.shape[1:], a.dtype) for a in self.arrays]
        self.scratch = [pltpu.SemaphoreType.DMA((n, N_CHIPS - 1)), pltpu.SemaphoreType.DMA((n, N_CHIPS - 1))]

    def _plan(self, ins, outs, sems):
        send, recv = sems
        x, y, c = _position()
        copies = []
        for r in range(1, N_CHIPS):
            px, py = x ^ (r >> 1), y ^ (r & 1)
            for a in range(len(ins)):
                copies.append(pltpu.make_async_remote_copy(
                    src_ref=ins[a].at[2 * px + py], dst_ref=outs[a].at[r - 1],
                    send_sem=send.at[a, r - 1], recv_sem=recv.at[a, r - 1],
                    device_id=(px, py, c), device_id_type=MESH))
        return copies

    def start(self, ins, outs, sems):
        for cp in self._plan(ins, outs, sems):
            cp.start()

    def finish(self, ins, outs, sems):
        copies = self._plan(ins, outs, sems)
        for cp in copies:
            cp.wait_recv()
        for cp in copies:
            cp.wait_send()


class _Both:
    def __init__(self, comms):
        self.comms = list(comms)
        self.arrays = [a for cm in self.comms for a in cm.arrays]
        self.out_shapes = [s for cm in self.comms for s in cm.out_shapes]
        self.scratch = [s for cm in self.comms for s in cm.scratch]

    def _split(self, ins, outs, sems):
        i = o = s = 0
        for cm in self.comms:
            ni, no, nsem = len(cm.arrays), len(cm.out_shapes), len(cm.scratch)
            yield cm, ins[i:i + ni], outs[o:o + no], sems[s:s + nsem]
            i, o, s = i + ni, o + no, s + nsem

    def start(self, ins, outs, sems):
        for cm, i, o, s in self._split(ins, outs, sems):
            cm.start(i, o, s)

    def finish(self, ins, outs, sems):
        for cm, i, o, s in self._split(ins, outs, sems):
            cm.finish(i, o, s)


def _host_call(body, *, name, grid, operands, in_specs, out_shape, out_specs, scratch_shapes=(), comm=None,
               prefetch=()):
    grid = tuple(grid)
    n_pre, n_in, n_out, n_scr = len(prefetch), len(operands), len(out_shape), len(scratch_shapes)
    nc_in, nc_out = (len(comm.arrays), len(comm.out_shapes)) if comm else (0, 0)
    all_in = list(in_specs) + [ANY] * nc_in
    all_out = list(out_specs) + [ANY] * nc_out
    all_scr = list(scratch_shapes) + (list(comm.scratch) if comm else [])
    all_shape = list(out_shape) + (list(comm.out_shapes) if comm else [])
    kwargs = dict(name=name, compiler_params=_params(len(grid)), out_shape=all_shape)
    if n_pre:
        kwargs["grid_spec"] = pltpu.PrefetchScalarGridSpec(
            num_scalar_prefetch=n_pre, grid=grid, in_specs=all_in, out_specs=all_out, scratch_shapes=all_scr)
    else:
        kwargs.update(in_specs=all_in, out_specs=all_out, scratch_shapes=all_scr)
        if grid:
            kwargs["grid"] = grid
    args = list(prefetch) + list(operands) + (list(comm.arrays) if comm else [])
    if comm is None:
        return list(pl.pallas_call(body, **kwargs)(*args)), []

    def hosted(*refs):
        bounds = [0, n_pre, n_pre + n_in]
        for n in (nc_in, n_out, nc_out, n_scr):
            bounds.append(bounds[-1] + n)
        bounds.append(len(refs))
        pre, ins, cins, outs, couts, scr, sems = [refs[a:b] for a, b in zip(bounds[:-1], bounds[1:])]
        if not grid:
            comm.start(cins, couts, sems)
            body(*pre, *ins, *outs, *scr)
            comm.finish(cins, couts, sems)
            return
        first, last = None, None
        for ax, size in enumerate(grid):
            pid = pl.program_id(ax)
            f, l = pid == 0, pid == size - 1
            first = f if first is None else jnp.logical_and(first, f)
            last = l if last is None else jnp.logical_and(last, l)

        @pl.when(first)
        def _():
            comm.start(cins, couts, sems)

        body(*pre, *ins, *outs, *scr)

        @pl.when(last)
        def _():
            comm.finish(cins, couts, sems)

    res = pl.pallas_call(hosted, **kwargs)(*args)
    return list(res[:n_out]), list(res[n_out:])


def _exchange_only(name, comm):
    def body():
        pass
    return _host_call(body, name=name, grid=(), operands=[], in_specs=[], out_shape=[], out_specs=[], comm=comm)[1]


def _matmul(name, grid, operands, in_specs, pairs, out_shapes, out_specs, epilogue, acc_shapes=(), nk=1,
            prologue=None, comm=None):
    n_in, n_out = len(operands), len(out_shapes)
    prologue = prologue or {}

    def body(*refs):
        ins, outs, accs = refs[:n_in], refs[n_in:n_in + n_out], refs[n_in + n_out:]
        pids = [pl.program_id(ax) for ax in range(len(grid))]

        def operand(i, blk=None):
            v = ins[i][...] if blk is None else ins[i][blk]
            if i in prologue:
                v = prologue[i](v)
            return v.astype(BF16)

        def products():
            vals = {}
            for pair in pairs:
                ai, bi, ci, dn = pair[:4]
                if len(pair) == 5:
                    p = None
                    for blk in range(pair[4]):
                        q = lax.dot_general(operand(ai, blk), operand(bi, blk), dn, preferred_element_type=F32)
                        p = q if p is None else p + q
                else:
                    p = lax.dot_general(operand(ai), operand(bi), dn, preferred_element_type=F32)
                vals[ci] = p if ci not in vals else vals[ci] + p
            return [vals[ci] for ci in sorted(vals)]

        if nk == 1:
            epilogue(products(), ins, outs, pids)
        else:
            k = pids[-1]
            prods = products()

            @pl.when(k == 0)
            def _():
                for acc, p in zip(accs, prods):
                    acc[...] = p

            @pl.when(k > 0)
            def _():
                for acc, p in zip(accs, prods):
                    acc[...] += p

            @pl.when(k == nk - 1)
            def _():
                epilogue([acc[...] for acc in accs], ins, outs, pids)

    return _host_call(
        body, name=name, grid=grid, operands=operands, in_specs=in_specs, out_shape=out_shapes, out_specs=out_specs,
        scratch_shapes=[pltpu.VMEM(s, F32) for s in acc_shapes] if nk > 1 else [], comm=comm)


def _rowwise(name, n_tiles, operands, in_specs, out_shapes, out_specs, red_widths, fn, comm=None):
    n_in, n_out, n_red = len(operands), len(out_shapes), len(red_widths)

    def body(*refs):
        ins, outs, reds = refs[:n_in], refs[n_in:n_in + n_out], refs[n_in + n_out:]
        i = pl.program_id(0)
        vals, sums = fn(i, *[r[...] for r in ins])
        for o, v in zip(outs, vals):
            o[...] = v.astype(o.dtype)
        if n_red:
            @pl.when(i == 0)
            def _():
                for r, s in zip(reds, sums):
                    r[...] = s

            @pl.when(i > 0)
            def _():
                for r, s in zip(reds, sums):
                    r[...] += s

    red_shapes = [jax.ShapeDtypeStruct((1, w), F32) for w in red_widths]
    red_specs = [pl.BlockSpec((1, w), lambda i: (0, 0)) for w in red_widths]
    res, cres = _host_call(
        body, name=name, grid=(n_tiles,), operands=operands, in_specs=in_specs,
        out_shape=list(out_shapes) + red_shapes, out_specs=list(out_specs) + red_specs, comm=comm)
    return res[:n_out], res[n_out:], cres


def _colsum(v):
    return jnp.sum(v, axis=0, keepdims=True)


def _store_all(accs, ins, outs, pids):
    for o, v in zip(outs, accs):
        o[...] = v.astype(o.dtype)


def _row_tile(rows_a, rows_b):
    return _tile(math.gcd(rows_a, rows_b) if rows_b else rows_a, ROW_TILE, SUBLANES)


def _tab_row(d, nlt, rows2):
    return pl.BlockSpec((None, 1, d), lambda i: (jnp.where(i < nlt, rows2[0], rows2[1]), 0, 0))


def _norm_mod_fwd(name, xs, tab, r_gamma, r_shift, r_scale, n_lat, n_ctx):
    rows, d = xs.shape
    tm = _row_tile(n_lat, n_ctx)
    nlt = n_lat // tm

    def fn(i, x, g, sh, sc):
        xh = x * lax.rsqrt(jnp.mean(x * x, axis=-1, keepdims=True) + NORM_EPS)
        return [(xh * g) * (1.0 + sc) + sh], []

    (h,), _, _ = _rowwise(
        name, rows // tm, [xs, tab, tab, tab],
        [pl.BlockSpec((tm, d), lambda i: (i, 0)), _tab_row(d, nlt, (r_gamma, r_gamma)), _tab_row(d, nlt, r_shift),
         _tab_row(d, nlt, r_scale)],
        [jax.ShapeDtypeStruct((rows, d), BF16)], [pl.BlockSpec((tm, d), lambda i: (i, 0))], [], fn)
    return h


def _norm_mod_bwd(name, xs, dh, tab, r_gamma, r_scale, n_lat, n_ctx, dres=None):
    rows, d = xs.shape
    tm = _row_tile(n_lat, n_ctx)
    nlt = n_lat // tm
    row = pl.BlockSpec((tm, d), lambda i: (i, 0))

    def fn(i, x, dy, g, sc, *res):
        rstd = lax.rsqrt(jnp.mean(x * x, axis=-1, keepdims=True) + NORM_EPS)
        xh = x * rstd
        dsh = _colsum(dy)
        dsc = _colsum(dy * (xh * g))
        dn = dy * (1.0 + sc)
        dgam = _colsum(dn * xh)
        dxh = dn * g
        dx = rstd * (dxh - xh * jnp.mean(dxh * xh, axis=-1, keepdims=True))
        if res:
            dx = dx + jnp.where(i < nlt, res[0], 0.0)
        lat = (i < nlt).astype(F32)
        return [dx], [dsh * lat, dsc * lat, dsh * (1.0 - lat), dsc * (1.0 - lat), dgam]

    operands = [xs, dh, tab, tab]
    specs = [row, row, _tab_row(d, nlt, (r_gamma, r_gamma)), _tab_row(d, nlt, r_scale)]
    if dres is not None:
        operands.append(dres)
        specs.append(pl.BlockSpec((tm, d), lambda i: (jnp.minimum(i, nlt - 1), 0)))
    (dx,), sums, _ = _rowwise(name, rows // tm, operands, specs,
                              [jax.ShapeDtypeStruct((rows, d), F32)], [row], [d] * 5, fn)
    return dx, sums


def _gate_bwd(name, dx, f, tab, r_gate, coef, n_lat, n_ctx):
    rows, d = dx.shape
    tm = _row_tile(n_lat, n_ctx)
    nlt = n_lat // tm
    row = pl.BlockSpec((tm, d), lambda i: (i, 0))

    def fn(i, dxv, fv, gv):
        dg = _colsum(dxv * fv) * coef
        lat = (i < nlt).astype(F32)
        return [(coef * gv) * dxv], [dg * lat, dg * (1.0 - lat)]

    (df,), sums, _ = _rowwise(
        name, rows // tm, [dx, f, tab],
        [row, row, _tab_row(d, nlt, r_gate)],
        [jax.ShapeDtypeStruct((rows, d), BF16)], [row], [d, d], fn)
    return df, sums


def _select_rows(i, tm, n_lat, v_lat, v_ctx):
    rows = i * tm + lax.broadcasted_iota(jnp.int32, (tm, 1), 0)
    return jnp.where(rows < n_lat, v_lat, v_ctx)


def _ffn_up(tag, h, wg, wu, comm=None):
    rows, d = h.shape
    nb, fs, _ = wg.shape
    tm = _tile(rows, MM_TILE, LANES)
    blk = pl.BlockSpec((None, tm, fs), lambda j, i: (j, i, 0))
    wspec = pl.BlockSpec((None, fs, d), lambda j, i: (j, 0, 0))

    def epilogue(accs, ins, outs, pids):
        a, b = accs
        outs[0][...] = a.astype(BF16)
        outs[1][...] = b.astype(BF16)
        outs[2][...] = (a * _sigmoid(a) * b).astype(BF16)

    hid = jax.ShapeDtypeStruct((nb, rows, fs), BF16)
    (a, b, s), cres = _matmul(
        tag + "_up", (nb, rows // tm), [h, wg, wu],
        [pl.BlockSpec((tm, d), lambda j, i: (i, 0)), wspec, wspec],
        [(0, 1, 0, NT), (0, 2, 1, NT)], [hid, hid, hid], [blk, blk, blk], epilogue, comm=comm)
    return a, b, s, cres


def _ffn_down(tag, s, wd, xs, tab2, r_gate, n_lat, comm=None):
    nb, rows, fs = s.shape
    d = wd.shape[-1]
    tm = _tile(rows, MM_TILE, LANES)
    tn = _tile(d, MM_TILE, LANES)

    def epilogue(accs, ins, outs, pids):
        f = accs[0]
        g = ins[3][...]
        gate = _select_rows(pids[0], tm, n_lat, g[r_gate[0]:r_gate[0] + 1, :], g[r_gate[1]:r_gate[1] + 1, :])
        outs[0][...] = f
        outs[1][...] = ins[2][...] + 0.5 * gate * f

    out = jax.ShapeDtypeStruct((rows, d), F32)
    ospec = pl.BlockSpec((tm, tn), lambda i, n: (i, n))
    (f, xo), cres = _matmul(
        tag + "_down", (rows // tm, d // tn), [s, wd, xs, tab2],
        [pl.BlockSpec((nb, tm, fs), lambda i, n: (0, i, 0)), pl.BlockSpec((nb, fs, tn), lambda i, n: (0, 0, n)),
         ospec, pl.BlockSpec((tab2.shape[0], tn), lambda i, n: (0, n))],
        [(0, 1, 0, NN, nb)], [out, out], [ospec, ospec], epilogue, comm=comm)
    return f, xo, cres


def _ffn_ds(tag, df, wd, a, b, comm=None):
    rows, d = df.shape
    nb, fs, _ = wd.shape
    tm = _tile(rows, MM_TILE, LANES)
    blk = pl.BlockSpec((None, tm, fs), lambda j, i: (j, i, 0))

    def epilogue(accs, ins, outs, pids):
        ds = accs[0]
        av = ins[2][...].astype(F32)
        bv = ins[3][...].astype(F32)
        sg = _sigmoid(av)
        outs[0][...] = (ds * bv * (sg * (1.0 + av * (1.0 - sg)))).astype(BF16)
        outs[1][...] = (ds * (av * sg)).astype(BF16)

    hid = jax.ShapeDtypeStruct((nb, rows, fs), BF16)
    (da, db), cres = _matmul(
        tag + "_ds", (nb, rows // tm), [df, wd, a, b],
        [pl.BlockSpec((tm, d), lambda j, i: (i, 0)), pl.BlockSpec((None, fs, d), lambda j, i: (j, 0, 0)), blk, blk],
        [(0, 1, 0, NT)], [hid, hid], [blk, blk], epilogue, comm=comm)
    return da, db, cres


def _ffn_dwd(tag, s, df, comm=None):
    nb, rows, fs = s.shape
    d = df.shape[-1]
    tn = _tile(d, MM_TILE, LANES)
    (dwd,), cres = _matmul(
        tag + "_dwd", (nb, d // tn), [s, df],
        [pl.BlockSpec((None, rows, fs), lambda j, n: (j, 0, 0)), pl.BlockSpec((rows, tn), lambda j, n: (0, n))],
        [(0, 1, 0, TN)], [jax.ShapeDtypeStruct((nb, fs, d), BF16)],
        [pl.BlockSpec((None, fs, tn), lambda j, n: (j, 0, n))], _store_all, comm=comm)
    return dwd, cres


def _ffn_dwgu(tag, h, da, db, comm=None):
    rows, d = h.shape
    nb, _, fs = da.shape
    tno = _tile(d, MM_TILE, LANES)
    full = pl.BlockSpec((None, rows, fs), lambda j, m: (j, 0, 0))
    wshape = jax.ShapeDtypeStruct((nb, fs, d), BF16)
    wblk = pl.BlockSpec((None, fs, tno), lambda j, m: (j, 0, m))
    (dwg, dwu), cres = _matmul(
        tag + "_dwgu", (nb, d // tno), [h, da, db],
        [pl.BlockSpec((rows, tno), lambda j, m: (0, m)), full, full],
        [(1, 0, 0, TN), (2, 0, 1, TN)], [wshape, wshape], [wblk, wblk], _store_all, comm=comm)
    return dwg, dwu, cres


def _ffn_dh(tag, da, db, wg, wu, comm=None):
    nb, rows, fs = da.shape
    d = wg.shape[2]
    tm = _tile(rows, MM_TILE, LANES)
    tn = _tile(d, MM_TILE_NT, LANES)
    aspec = pl.BlockSpec((nb, tm, fs), lambda i, n: (0, i, 0))
    wspec = pl.BlockSpec((nb, fs, tn), lambda i, n: (0, 0, n))
    (dh,), cres = _matmul(
        tag + "_dh", (rows // tm, d // tn), [da, wg, db, wu], [aspec, wspec, aspec, wspec],
        [(0, 1, 0, NN, nb), (2, 3, 0, NN, nb)], [jax.ShapeDtypeStruct((rows, d), F32)],
        [pl.BlockSpec((tm, tn), lambda i, n: (i, n))], _store_all, comm=comm)
    return dh, cres


def _rope_tables(n_lat, n_ctx):
    half = LANES // 4
    inv_freq = (np.float32(ROPE_THETA) ** (-np.arange(half, dtype=np.float32) / np.float32(half))).astype(np.float32)
    pos = np.arange(n_lat)
    ang_r = (pos // GRID_W).astype(np.float32)[:, None] * inv_freq
    ang_c = (pos % GRID_W).astype(np.float32)[:, None] * inv_freq
    cos_l = np.concatenate([np.cos(ang_r)] * 2 + [np.cos(ang_c)] * 2, axis=1)
    sin_l = np.concatenate([-np.sin(ang_r), np.sin(ang_r), -np.sin(ang_c), np.sin(ang_c)], axis=1)
    cos_all = np.concatenate([cos_l, np.ones((n_ctx, LANES), np.float32)], axis=0).astype(np.float32)
    sin_all = np.concatenate([sin_l, np.zeros((n_ctx, LANES), np.float32)], axis=0).astype(np.float32)
    return jnp.asarray(cos_all), jnp.asarray(sin_all)


def _swap_halves(x):
    lane = lax.broadcasted_iota(jnp.int32, x.shape, 1)
    return jnp.where((lane % 64) < 32, pltpu.roll(x, 96, 1), pltpu.roll(x, 32, 1))


def _heads_spec(tq, hb, width, first_block):
    per_shard = width // (hb * LANES)

    def index(k, i):
        blk = first_block + k
        return blk // per_shard, i, blk % per_shard
    return pl.BlockSpec((None, tq, hb * LANES), index)


def _qk_prep(name, src, first_block, hb, n_heads, rows, g, cos_t, sin_t):
    tq = _tile(rows, HEAD_ROW_TILE, SUBLANES)
    tab = pl.BlockSpec((tq, LANES), lambda k, i: (i, 0))

    def body(x_ref, g_ref, c_ref, s_ref, o_ref):
        for h in range(hb):
            x = x_ref[:, h * LANES:(h + 1) * LANES]
            n = x * lax.rsqrt(jnp.mean(x * x, axis=-1, keepdims=True) + NORM_EPS) * g_ref[...]
            o_ref[h] = (n * c_ref[...] + _swap_halves(n) * s_ref[...]).astype(BF16)

    return pl.pallas_call(
        body, name=name, grid=(n_heads // hb, rows // tq),
        in_specs=[_heads_spec(tq, hb, src.shape[-1], first_block), pl.BlockSpec((1, LANES), lambda k, i: (0, 0)),
                  tab, tab],
        out_specs=pl.BlockSpec((hb, tq, LANES), lambda k, i: (k, i, 0)),
        out_shape=jax.ShapeDtypeStruct((n_heads, rows, LANES), BF16), compiler_params=_params(2),
    )(src, g, cos_t, sin_t)


def _qk_prep_bwd(name, dy, src, first_block, hb, n_heads, rows, g, cos_t, sin_t):
    tq = _tile(rows, HEAD_ROW_TILE, SUBLANES)
    tab = pl.BlockSpec((tq, LANES), lambda k, i: (i, 0))

    def body(dy_ref, x_ref, g_ref, c_ref, s_ref, dx_ref, dg_ref):
        g = g_ref[...]
        dg = None
        for h in range(hb):
            x = x_ref[:, h * LANES:(h + 1) * LANES]
            dyv = dy_ref[h]
            rstd = lax.rsqrt(jnp.mean(x * x, axis=-1, keepdims=True) + NORM_EPS)
            xh = x * rstd
            dn = dyv * c_ref[...] + _swap_halves(dyv * s_ref[...])
            dxh = dn * g
            dx = rstd * (dxh - xh * jnp.mean(dxh * xh, axis=-1, keepdims=True))
            dx_ref[:, h * LANES:(h + 1) * LANES] = dx.astype(BF16)
            part = _colsum(dn * xh)
            dg = part if dg is None else dg + part
        first = jnp.logical_and(pl.program_id(0) == 0, pl.program_id(1) == 0)

        @pl.when(first)
        def _():
            dg_ref[...] = dg

        @pl.when(jnp.logical_not(first))
        def _():
            dg_ref[...] += dg

    return pl.pallas_call(
        body, name=name, grid=(n_heads // hb, rows // tq),
        in_specs=[pl.BlockSpec((hb, tq, LANES), lambda k, i: (k, i, 0)),
                  _heads_spec(tq, hb, src.shape[-1], first_block),
                  pl.BlockSpec((1, LANES), lambda k, i: (0, 0)), tab, tab],
        out_specs=[pl.BlockSpec((None, tq, hb * LANES), lambda k, i: (k, i, 0)),
                   pl.BlockSpec((1, LANES), lambda k, i: (0, 0))],
        out_shape=[jax.ShapeDtypeStruct((n_heads // hb, rows, hb * LANES), BF16),
                   jax.ShapeDtypeStruct((1, LANES), F32)],
        compiler_params=_params(2),
    )(dy, src, g, cos_t, sin_t)


def _heads_cast(name, src, first_block, hb, n_heads, rows):
    tq = _tile(rows, HEAD_ROW_TILE, SUBLANES)

    def body(x_ref, o_ref):
        for h in range(hb):
            o_ref[h] = x_ref[:, h * LANES:(h + 1) * LANES].astype(BF16)

    return pl.pallas_call(
        body, name=name, grid=(n_heads // hb, rows // tq),
        in_specs=[_heads_spec(tq, hb, src.shape[-1], first_block)],
        out_specs=pl.BlockSpec((hb, tq, LANES), lambda k, i: (k, i, 0)),
        out_shape=jax.ShapeDtypeStruct((n_heads, rows, LANES), BF16), compiler_params=_params(2),
    )(src)


def _heads_merge(name, src):
    n_heads, rows, _ = src.shape
    tq = _tile(rows, HEAD_ROW_TILE, SUBLANES)

    def body(x_ref, o_ref):
        for h in range(n_heads):
            o_ref[:, h * LANES:(h + 1) * LANES] = x_ref[h].astype(BF16)

    return pl.pallas_call(
        body, name=name, grid=(rows // tq,),
        in_specs=[pl.BlockSpec((n_heads, tq, LANES), lambda i: (0, i, 0))],
        out_specs=pl.BlockSpec((tq, n_heads * LANES), lambda i: (i, 0)),
        out_shape=jax.ShapeDtypeStruct((rows, n_heads * LANES), BF16), compiler_params=_params(1),
    )(src)


def _attn_fwd(q, k, v, q_per_kv, comm=None):
    nq, l, _ = q.shape
    s_len = k.shape[1]
    tq = _tile(l, ROW_TILE, SUBLANES)
    scale = LANES ** -0.5
    hp = ATTN_FWD_HEADS if q_per_kv % ATTN_FWD_HEADS == 0 else 1
    kv = pl.BlockSpec((None, s_len, LANES), lambda h, i: ((h * hp) // q_per_kv, 0, 0))

    def body(q_ref, k_ref, v_ref, o_ref):
        for h in range(hp):
            s = lax.dot_general(q_ref[h], k_ref[...], NT, preferred_element_type=F32)
            p = jnp.exp2((s - jnp.max(s, axis=-1, keepdims=True)) * (scale * LOG2E))
            den = jnp.sum(p, axis=-1, keepdims=True)
            o = jnp.dot(p.astype(BF16), v_ref[...], preferred_element_type=F32)
            o_ref[:, h * LANES:(h + 1) * LANES] = (o * (1.0 / den)).astype(BF16)

    (o,), cres = _host_call(
        body, name="attn_fwd", grid=(nq // hp, l // tq), operands=[q, k, v],
        in_specs=[pl.BlockSpec((hp, tq, LANES), lambda h, i: (h, i, 0)), kv, kv],
        out_shape=[jax.ShapeDtypeStruct((l, nq * LANES), BF16)],
        out_specs=[pl.BlockSpec((tq, hp * LANES), lambda h, i: (i, h))], comm=comm)
    return o, cres


def _attn_bwd(q, k, v, do, q_per_kv, comm=None):
    nq, l, _ = q.shape
    nkv, s_len, _ = k.shape
    tq = _tile(l, ROW_TILE, SUBLANES)
    scale = LANES ** -0.5
    hp = ATTN_BWD_HEADS if q_per_kv % ATTN_BWD_HEADS == 0 else 1
    kv = pl.BlockSpec((None, s_len, LANES), lambda g, r, i: (g, 0, 0))
    qs = pl.BlockSpec((hp, tq, LANES), lambda g, r, i: (g * (q_per_kv // hp) + r, i, 0))

    def body(q_ref, k_ref, v_ref, do_ref, dq_ref, dk_ref, dv_ref):
        kvv, vv = k_ref[...], v_ref[...]
        dk_new = dv_new = None
        for h in range(hp):
            qv, dov = q_ref[h], do_ref[:, h * LANES:(h + 1) * LANES]
            st = lax.dot_general(kvv, qv, NT, preferred_element_type=F32)
            e = jnp.exp2((st - jnp.max(st, axis=0, keepdims=True)) * (scale * LOG2E))
            pt = e * (1.0 / jnp.sum(e, axis=0, keepdims=True))
            dpt = lax.dot_general(vv, dov, NT, preferred_element_type=F32)
            delta = jnp.sum(pt * dpt, axis=0, keepdims=True)
            dst = (pt * (dpt - delta)).astype(BF16)
            dq_ref[h] = lax.dot_general(dst, kvv, TN, preferred_element_type=F32) * scale
            dk_h = jnp.dot(dst, qv, preferred_element_type=F32) * scale
            dv_h = jnp.dot(pt.astype(BF16), dov, preferred_element_type=F32)
            dk_new = dk_h if dk_new is None else dk_new + dk_h
            dv_new = dv_h if dv_new is None else dv_new + dv_h
        first = jnp.logical_and(pl.program_id(1) == 0, pl.program_id(2) == 0)

        @pl.when(first)
        def _():
            dk_ref[...] = dk_new
            dv_ref[...] = dv_new

        @pl.when(jnp.logical_not(first))
        def _():
            dk_ref[...] += dk_new
            dv_ref[...] += dv_new

    (dq, dk, dv), cres = _host_call(
        body, name="attn_bwd", grid=(nkv, q_per_kv // hp, l // tq), operands=[q, k, v, do],
        in_specs=[qs, kv, kv, pl.BlockSpec((tq, hp * LANES), lambda g, r, i: (i, g * (q_per_kv // hp) + r))],
        out_specs=[qs, kv, kv],
        out_shape=[jax.ShapeDtypeStruct((nq, l, LANES), F32), jax.ShapeDtypeStruct((nkv, s_len, LANES), F32),
                   jax.ShapeDtypeStruct((nkv, s_len, LANES), F32)], comm=comm)
    return dq, dk, dv, cres


def _zoh(a_re, a_im, log_dt):
    dt = jnp.exp(log_dt)[..., None]
    mag = jnp.exp(a_re * dt)
    lb_re = mag * jnp.cos(a_im * dt)
    lb_im = mag * jnp.sin(a_im * dt)
    den = a_re * a_re + a_im * a_im
    coef_re = ((lb_re - 1.0) * a_re + lb_im * a_im) / den
    coef_im = (lb_im * a_re - (lb_re - 1.0) * a_im) / den
    return lb_re, lb_im, coef_re, coef_im


def _ssm_discretize(a_re, a_im, log_dt, b_re, b_im):
    lb_re, lb_im, cr, ci = _zoh(a_re, a_im, log_dt)
    bt_re = cr[..., None] * b_re - ci[..., None] * b_im
    bt_im = cr[..., None] * b_im + ci[..., None] * b_re
    return lb_re, lb_im, bt_re, bt_im


def _lambda_powers(a_re, a_im, log_dt, ns):
    dt = jnp.exp(log_dt)[..., None]
    k = jnp.arange(SCAN_TAPS + 1, dtype=F32)[:, None, None, None]
    mag, ang = jnp.exp(k * (a_re * dt)), k * (a_im * dt)
    shape = (SCAN_TAPS + 1, 2, ns, -1)
    return (mag * jnp.cos(ang)).reshape(shape), (mag * jnp.sin(ang)).reshape(shape)


def _slab_mask():
    idx = jnp.arange(SLAB_GROUPS)
    return (idx[:, None] == idx[None, :])[None, None, :, None, :, None]


def _block_diag(m):
    d, g, a, b = m.shape
    ns = g // SLAB_GROUPS
    wide = jnp.where(_slab_mask(), m.reshape(d, ns, SLAB_GROUPS, a, 1, b), 0.0)
    return wide.reshape(d, ns, SLAB_GROUPS * a, SLAB_GROUPS * b)


def _block_diag_extract(m, a, b):
    d, ns = m.shape[:2]
    m = m.reshape(d, ns, SLAB_GROUPS, a, SLAB_GROUPS, b)
    return jnp.sum(jnp.where(_slab_mask(), m, 0.0), axis=4).reshape(d, ns * SLAB_GROUPS, a, b)


def _build_tap_weights(w_ref, base_ref, pw_ref, conj, sw):
    b_re, b_im = base_ref[:, :sw], base_ref[:, sw:]
    for tau in range(SCAN_TAPS):
        p_re, p_im = pw_ref[tau:tau + 1, :sw], pw_ref[tau:tau + 1, sw:]
        if conj:
            p_im = -p_im
        w_ref[tau * LANES:(tau + 1) * LANES, :sw] = (p_re * b_re - p_im * b_im).astype(BF16)
        w_ref[tau * LANES:(tau + 1) * LANES, sw:] = (p_re * b_im + p_im * b_re).astype(BF16)


def _carry_tables(pw_re, pw_im, descending):
    def rows(pw):
        asc = pw[1:]
        per_dir = [asc[::-1, d] if descending[d] else asc[:, d] for d in range(2)]
        return jnp.transpose(jnp.stack(per_dir), (0, 2, 1, 3))
    return jnp.concatenate([rows(pw_re), rows(pw_im)], axis=-1)


def _scan_chunk(x, w_ref, tab_ref, s_ref, carry_ref, descending, t_rows, sw):
    row8 = lax.broadcasted_iota(jnp.int32, x.shape, 0) % SCAN_TAPS
    pieces = [x.astype(BF16)]
    for tau in range(1, SCAN_TAPS):
        if descending:
            sh = jnp.where(row8 <= SCAN_TAPS - 1 - tau, pltpu.roll(x, t_rows - tau, 0), 0.0)
        else:
            sh = jnp.where(row8 >= tau, pltpu.roll(x, tau, 0), 0.0)
        pieces.append(sh.astype(BF16))
    xa = jnp.concatenate(pieces, axis=1)
    s_ref[...] = jnp.dot(xa, w_ref[...], preferred_element_type=F32)
    tab = tab_ref[...]
    t_re, t_im = tab[:, :sw], tab[:, sw:]
    nb = t_rows // SCAN_TAPS
    edge = 0 if descending else SCAN_TAPS - 1

    def step(b, carry):
        h_re, h_im = carry
        r0 = pl.multiple_of(((nb - 1 - b) if descending else b) * SCAN_TAPS, SCAN_TAPS)
        x_re = s_ref[pl.ds(r0, SCAN_TAPS), :sw] + t_re * h_re - t_im * h_im
        x_im = s_ref[pl.ds(r0, SCAN_TAPS), sw:] + t_re * h_im + t_im * h_re
        s_ref[pl.ds(r0, SCAN_TAPS), :sw] = x_re
        s_ref[pl.ds(r0, SCAN_TAPS), sw:] = x_im
        return x_re[edge:edge + 1, :], x_im[edge:edge + 1, :]

    h_re, h_im = lax.fori_loop(0, nb, step, (carry_ref[0:1, :sw], carry_ref[0:1, sw:]))
    carry_ref[0:1, :sw] = h_re
    carry_ref[0:1, sw:] = h_im


def _slab_spec(rows, cols, dr):
    return pl.BlockSpec((None, None, rows, cols), lambda s, i: (dr, s, 0, 0))


def _ssm_fwd(name, dr, u_src, u_shard, bd, pw, tab, ct, descending, chunk_of, t_rows, rows, comm=None):
    _, ns, _, sw2 = bd.shape
    sw = sw2 // 2
    width = ns * LANES
    nchunks = rows // t_rows

    def body(u_ref, bd_ref, pw_ref, tab_ref, ct_ref, y_ref, h_ref, s_ref, carry_ref, w_ref):
        @pl.when(pl.program_id(1) == 0)
        def _():
            carry_ref[...] = jnp.zeros_like(carry_ref)
            _build_tap_weights(w_ref, bd_ref, pw_ref, False, sw)

        _scan_chunk(u_ref[...], w_ref, tab_ref, s_ref, carry_ref, descending, t_rows, sw)
        hb = s_ref[...].astype(BF16)
        h_ref[...] = hb
        y_ref[...] = lax.dot_general(hb, ct_ref[...], NT, preferred_element_type=F32)

    (y, h), cres = _host_call(
        body, name=name, grid=(ns, nchunks), operands=[u_src, bd, pw, tab, ct],
        in_specs=[pl.BlockSpec((None, t_rows, LANES), lambda s, i: (u_shard, chunk_of(i), s)),
                  _slab_spec(LANES, sw2, dr), _slab_spec(2 * SCAN_TAPS, sw2, dr), _slab_spec(SCAN_TAPS, sw2, dr),
                  _slab_spec(LANES, sw2, dr)],
        out_specs=[pl.BlockSpec((t_rows, LANES), lambda s, i: (chunk_of(i), s)),
                   pl.BlockSpec((None, t_rows, sw2), lambda s, i: (s, chunk_of(i), 0))],
        out_shape=[jax.ShapeDtypeStruct((rows, width), F32), jax.ShapeDtypeStruct((ns, rows, sw2), BF16)],
        scratch_shapes=[pltpu.VMEM((t_rows, sw2), F32), pltpu.VMEM((SUBLANES, sw2), F32),
                        pltpu.VMEM((SCAN_TAPS * LANES, sw2), BF16)], comm=comm)
    return y, h, cres


def _ssm_bwd(name, dr, dy, u_src, u_shard, states, ct, pw, tab, bd, descending, chunk_of, t_rows, rows, comm=None):
    _, ns, _, sw2 = ct.shape
    sw = sw2 // 2
    width = ns * LANES
    nchunks = rows // t_rows

    def body(dy_ref, u_ref, h_ref, ct_ref, pw_ref, tab_ref, bd_ref, du_ref, dbd_ref, dcd_ref, dlam_ref,
             s_ref, carry_ref, gsave_ref, w_ref):
        first = pl.program_id(1) == 0

        @pl.when(first)
        def _():
            carry_ref[...] = jnp.zeros_like(carry_ref)
            gsave_ref[...] = jnp.zeros_like(gsave_ref)
            _build_tap_weights(w_ref, ct_ref, pw_ref, True, sw)

        dyv = dy_ref[...]
        _scan_chunk(dyv, w_ref, tab_ref, s_ref, carry_ref, descending, t_rows, sw)
        g = s_ref[...]
        gb = g.astype(BF16)
        du_ref[...] = lax.dot_general(gb, bd_ref[...], NT, preferred_element_type=F32)
        dbd = lax.dot_general(u_ref[...].astype(BF16), gb, TN, preferred_element_type=F32)
        hb = h_ref[...]
        dcd = lax.dot_general(hb, dyv.astype(BF16), TN, preferred_element_type=F32)
        hf = hb.astype(F32)
        rowid = lax.broadcasted_iota(jnp.int32, hf.shape, 0)
        if descending:
            hp = jnp.where(rowid == 0, 0.0, pltpu.roll(hf, 1, 0))
            h_edge, g_edge = hf[t_rows - 1:t_rows, :], g[0:1, :]
        else:
            hp = jnp.where(rowid == t_rows - 1, 0.0, pltpu.roll(hf, t_rows - 1, 0))
            h_edge, g_edge = hf[0:1, :], g[t_rows - 1:t_rows, :]
        g_re, g_im, hp_re, hp_im = g[:, :sw], g[:, sw:], hp[:, :sw], hp[:, sw:]
        gs = gsave_ref[0:1, :]
        gs_re, gs_im, he_re, he_im = gs[:, :sw], gs[:, sw:], h_edge[:, :sw], h_edge[:, sw:]
        dl_re = _colsum(g_re * hp_re + g_im * hp_im) + gs_re * he_re + gs_im * he_im
        dl_im = _colsum(g_im * hp_re - g_re * hp_im) + gs_im * he_re - gs_re * he_im
        gsave_ref[0:1, :] = g_edge

        @pl.when(first)
        def _():
            dbd_ref[...] = dbd
            dcd_ref[...] = dcd
            dlam_ref[:, :sw] = dl_re
            dlam_ref[:, sw:] = dl_im

        @pl.when(jnp.logical_not(first))
        def _():
            dbd_ref[...] += dbd
            dcd_ref[...] += dcd
            dlam_ref[:, :sw] += dl_re
            dlam_ref[:, sw:] += dl_im

    (du, dbd, dcd, dlam), cres = _host_call(
        body, name=name, grid=(ns, nchunks), operands=[dy, u_src, states, ct, pw, tab, bd],
        in_specs=[pl.BlockSpec((t_rows, LANES), lambda s, i: (chunk_of(i), s)),
                  pl.BlockSpec((None, t_rows, LANES), lambda s, i: (u_shard, chunk_of(i), s)),
                  pl.BlockSpec((None, t_rows, sw2), lambda s, i: (s, chunk_of(i), 0)),
                  _slab_spec(LANES, sw2, dr), _slab_spec(2 * SCAN_TAPS, sw2, dr), _slab_spec(SCAN_TAPS, sw2, dr),
                  _slab_spec(LANES, sw2, dr)],
        out_specs=[pl.BlockSpec((t_rows, LANES), lambda s, i: (chunk_of(i), s)),
                   pl.BlockSpec((None, LANES, sw2), lambda s, i: (s, 0, 0)),
                   pl.BlockSpec((None, sw2, LANES), lambda s, i: (s, 0, 0)),
                   pl.BlockSpec((None, 1, sw2), lambda s, i: (s, 0, 0))],
        out_shape=[jax.ShapeDtypeStruct((rows, width), F32), jax.ShapeDtypeStruct((ns, LANES, sw2), F32),
                   jax.ShapeDtypeStruct((ns, sw2, LANES), F32), jax.ShapeDtypeStruct((ns, 1, sw2), F32)],
        scratch_shapes=[pltpu.VMEM((t_rows, sw2), F32), pltpu.VMEM((SUBLANES, sw2), F32),
                        pltpu.VMEM((SUBLANES, sw2), F32), pltpu.VMEM((SCAN_TAPS * LANES, sw2), BF16)], comm=comm)
    return du, dbd, dcd, dlam, cres


def _mod_fwd(cs, w_mod, b_cols):
    d, width = w_mod.shape
    tn = _tile(width, 768, LANES)

    def epilogue(accs, ins, outs, pids):
        outs[0][...] = accs[0] + ins[2][...]

    return _matmul(
        "mod_fwd", (width // tn,), [cs, w_mod, b_cols],
        [pl.BlockSpec((16, d), lambda n: (0, 0)), pl.BlockSpec((d, tn), lambda n: (0, n)),
         pl.BlockSpec((1, tn), lambda n: (0, n))],
        [(0, 1, 0, NN)], [jax.ShapeDtypeStruct((16, width), F32)], [pl.BlockSpec((16, tn), lambda n: (0, n))],
        epilogue, prologue={0: lambda v: v * _sigmoid(v)})[0][0]


def _mod_bwd_adam(cs, dmod_cols, w, m, v, comm=None):
    d, width = w.shape
    tn = _tile(width, LANES, LANES)
    col = pl.BlockSpec((d, tn), lambda n: (0, n))

    def body(cs_ref, dm_ref, w_ref, m_ref, v_ref, g_ref, dl_ref, nm_ref, nv_ref, ds_ref):
        n = pl.program_id(0)
        lat = dm_ref[pl.ds(0, N_DEV, stride=SUBLANES), :]
        ctx = jnp.sum(dm_ref[pl.ds(1, N_DEV, stride=SUBLANES), :], axis=0, keepdims=True)
        row = lax.broadcasted_iota(jnp.int32, lat.shape, 0)
        dm = jnp.concatenate([lat, jnp.where(row == 0, ctx, 0.0)], axis=0).astype(BF16)
        c = cs_ref[...]
        sc = (c * _sigmoid(c)).astype(BF16)
        wv = w_ref[...]
        g = lax.dot_general(sc, dm, TN, preferred_element_type=F32)
        delta, m2, v2 = _adamw(wv, g, m_ref[...], v_ref[...])
        g_ref[...] = g
        dl_ref[...] = delta
        nm_ref[...] = m2
        nv_ref[...] = v2
        part = lax.dot_general(dm, wv.astype(BF16), NT, preferred_element_type=F32)

        @pl.when(n == 0)
        def _():
            ds_ref[...] = part

        @pl.when(n > 0)
        def _():
            ds_ref[...] += part

    shard = jax.ShapeDtypeStruct((d, width), F32)
    return _host_call(
        body, name="mod_bwd_adam", grid=(width // tn,), operands=[cs, dmod_cols, w, m, v],
        in_specs=[pl.BlockSpec((16, d), lambda n: (0, 0)), pl.BlockSpec((N_DEV * SUBLANES, tn), lambda n: (0, n)),
                  col, col, col],
        out_specs=[col, col, col, col, pl.BlockSpec((16, d), lambda n: (0, 0))],
        out_shape=[shard, shard, shard, shard, jax.ShapeDtypeStruct((16, d), F32)], comm=comm)


def _pair_sum(name, grads, got, core):
    _, rows, cols = grads.shape
    tr = _tile(rows, max(PACKED_SUBLANES, ADAM_BLOCK_BYTES // (cols * 6 * N_CHIPS)), PACKED_SUBLANES)
    blk = pl.BlockSpec((N_CHIPS, tr, cols), lambda i, cc: (0, i, 0))

    def body(core_ref, a_ref, b_ref, o_ref):
        o_ref[...] = (a_ref[...].astype(F32) + b_ref[...].astype(F32)).astype(BF16)

    grid_spec = pltpu.PrefetchScalarGridSpec(
        num_scalar_prefetch=1, grid=(rows // tr,),
        in_specs=[pl.BlockSpec((N_CHIPS, None, tr, cols), lambda i, cc: (0, cc[0], i, 0)), blk], out_specs=blk)
    return pl.pallas_call(
        body, name=name, grid_spec=grid_spec, out_shape=jax.ShapeDtypeStruct((N_CHIPS, rows, cols), BF16),
        compiler_params=_params(1))(core, grads.reshape(N_CHIPS, 2, rows, cols), got)


def _owner_adam(name, items, chip, comm=None):
    plan, start = [], 0
    per_element = 2 * (2 * N_CHIPS + 7 * 4)
    block_elements = ADAM_GROUP_VMEM // (per_element * len(items))
    for _, _, w, _, _ in items:
        rows, cols = w.shape
        tr = _tile(rows, max(PACKED_SUBLANES, block_elements // cols), PACKED_SUBLANES)
        plan.append((start, rows // tr, tr, cols))
        start += rows // tr
    operands, in_specs, out_specs, out_shape = [], [], [], []
    for (first, nt, tr, cols), (p, l, w, m, v) in zip(plan, items):
        def tile(s, first=first, nt=nt):
            return jnp.clip(s - first, 0, nt - 1)
        blk = pl.BlockSpec((tr, cols), lambda s, ch, tile=tile: (tile(s), 0))
        operands += [p, l, w, m, v]
        in_specs += [pl.BlockSpec((None, tr, cols), lambda s, ch, tile=tile: (ch[0], tile(s), 0)),
                     pl.BlockSpec((N_CHIPS - 1, tr, cols), lambda s, ch, tile=tile: (0, tile(s), 0)), blk, blk, blk]
        out_specs += [blk] * 4
        out_shape += [jax.ShapeDtypeStruct(w.shape, F32)] * 4
    n = len(items)

    def body(chip_ref, *refs):
        s = pl.program_id(0)
        for k, (first, nt, _, _) in enumerate(plan):
            p_ref, l_ref, w_ref, m_ref, v_ref = refs[5 * k:5 * k + 5]
            g_ref, dl_ref, nm_ref, nv_ref = refs[5 * n + 4 * k:5 * n + 4 * k + 4]

            @pl.when(jnp.logical_and(s >= first, s < first + nt))
            def _(p_ref=p_ref, l_ref=l_ref, w_ref=w_ref, m_ref=m_ref, v_ref=v_ref,
                  g_ref=g_ref, dl_ref=dl_ref, nm_ref=nm_ref, nv_ref=nv_ref):
                g = p_ref[...].astype(F32)
                for r in range(N_CHIPS - 1):
                    g = g + l_ref[r].astype(F32)
                delta, m2, v2 = _adamw(w_ref[...], g, m_ref[...], v_ref[...])
                g_ref[...] = g
                dl_ref[...] = delta
                nm_ref[...] = m2
                nv_ref[...] = v2

    res, cres = _host_call(body, name=name, grid=(start,), operands=operands, in_specs=in_specs,
                           out_shape=out_shape, out_specs=out_specs, comm=comm, prefetch=[chip])
    return [res[4 * k:4 * k + 4] for k in range(n)], cres


def _sum_adam(name, parts, w, m, v):
    rows, cols = w.shape
    n_parts = parts.shape[0]
    align = PACKED_SUBLANES if parts.dtype == BF16 else SUBLANES
    tr = _tile(rows, max(align, ADAM_BLOCK_BYTES // (cols * 44)), align)
    blk = pl.BlockSpec((tr, cols), lambda i: (i, 0))

    def body(p_ref, w_ref, m_ref, v_ref, g_ref, dl_ref, nm_ref, nv_ref):
        g = p_ref[0].astype(F32)
        for s in range(1, n_parts):
            g = g + p_ref[s].astype(F32)
        delta, m2, v2 = _adamw(w_ref[...], g, m_ref[...], v_ref[...])
        g_ref[...] = g
        dl_ref[...] = delta
        nm_ref[...] = m2
        nv_ref[...] = v2

    out = jax.ShapeDtypeStruct((rows, cols), F32)
    return pl.pallas_call(
        body, name=name, grid=(rows // tr,),
        in_specs=[pl.BlockSpec((n_parts, tr, cols), lambda i: (0, i, 0)), blk, blk, blk],
        out_specs=[blk, blk, blk, blk], out_shape=[out, out, out, out], compiler_params=_params(1),
    )(parts, w, m, v)


def _bias_adam(dmod_all, w, m, v):
    width = w.shape[-1]
    tn = _tile(width, 2048, LANES)
    blk = pl.BlockSpec((1, tn), lambda n: (0, n))

    def body(p_ref, w_ref, m_ref, v_ref, g_ref, dl_ref, nm_ref, nv_ref):
        g = jnp.sum(p_ref[...], axis=0, keepdims=True)
        delta, m2, v2 = _adamw(w_ref[...], g, m_ref[...], v_ref[...])
        g_ref[...] = g
        dl_ref[...] = delta
        nm_ref[...] = m2
        nv_ref[...] = v2

    out = jax.ShapeDtypeStruct((1, width), F32)
    return pl.pallas_call(
        body, name="bias_adam", grid=(width // tn,),
        in_specs=[pl.BlockSpec((dmod_all.shape[0], tn), lambda n: (0, n)), blk, blk, blk],
        out_specs=[blk, blk, blk, blk], out_shape=[out, out, out, out], compiler_params=_params(1),
    )(dmod_all, w, m, v)


def _pack(arrays, total_rows):
    flat = []
    for a in arrays:
        a = a.reshape(-1).astype(F32)
        flat.append(jnp.pad(a, (0, (-a.shape[0]) % LANES)))
    flat = jnp.concatenate(flat).reshape(-1, LANES)
    return jnp.pad(flat, ((0, total_rows - flat.shape[0]), (0, 0)))


def _unpack(packed, shapes):
    out, row = [], 0
    for shp in shapes:
        size = math.prod(shp)
        nrows = -(-size // LANES)
        out.append(packed[row:row + nrows].reshape(-1)[:size].reshape(shp))
        row += nrows
    return out


def kernel(x, c, ctx, c_ctx, w_mod, b_mod, norm_g, w_ffn1_gate, w_ffn1_up, w_ffn1_down, w_in, q_norm_g, k_norm_g, ssm_a_re, ssm_a_im, ssm_log_dt, ssm_b_re, ssm_b_im, ssm_c_re, ssm_c_im, ssm_d, w_glu, b_glu, w_br_attn, w_br_ssm, w_out, w_ffn2_gate, w_ffn2_up, w_ffn2_down, loss_target, m_c_ctx, m_w_mod, m_b_mod, m_norm_g, m_w_ffn1_gate, m_w_ffn1_up, m_w_ffn1_down, m_w_in, m_q_norm_g, m_k_norm_g, m_ssm_a_re, m_ssm_a_im, m_ssm_log_dt, m_ssm_b_re, m_ssm_b_im, m_ssm_c_re, m_ssm_c_im, m_ssm_d, m_w_glu, m_b_glu, m_w_br_attn, m_w_br_ssm, m_w_out, m_w_ffn2_gate, m_w_ffn2_up, m_w_ffn2_down, v_c_ctx, v_w_mod, v_b_mod, v_norm_g, v_w_ffn1_gate, v_w_ffn1_up, v_w_ffn1_down, v_w_in, v_q_norm_g, v_k_norm_g, v_ssm_a_re, v_ssm_a_im, v_ssm_log_dt, v_ssm_b_re, v_ssm_b_im, v_ssm_c_re, v_ssm_c_im, v_ssm_d, v_w_glu, v_b_glu, v_w_br_attn, v_w_br_ssm, v_w_out, v_w_ffn2_gate, v_w_ffn2_up, v_w_ffn2_down):
    _, L, D = x.shape
    Lc = ctx.shape[1]
    R = L + Lc
    MODW = w_mod.shape[-1]
    INS = w_in.shape[-1]
    KVW = INS // 2
    NQ = D // LANES
    NKV = KVW // LANES
    QPK = NQ // NKV
    HBQ = INS // LANES
    G, P, E = ssm_b_re.shape[2:]
    W = G * E
    SW = SLAB_GROUPS * P
    assert E * SLAB_GROUPS == LANES and W == INS and NQ * LANES == D and Lc <= L
    me = 4 * lax.axis_index("x") + 2 * lax.axis_index("y") + lax.axis_index("c")

    x2, ctx2, tgt = x[0], ctx[0], loss_target[0]
    xc0 = jnp.concatenate([x2, ctx2], axis=0)

    def bf(w):
        return w[0].astype(BF16)

    def held_t(w):
        return jnp.swapaxes(w[0], 0, 1)

    def bft(w):
        return held_t(w).astype(BF16)

    def widen(a):
        return jnp.pad(a[0], ((0, 0), (0, D - a.shape[-1])))

    def at_row(a, r, total):
        return jnp.pad(a, ((r, total - r - a.shape[0]), (0, 0)))

    pack_in = (at_row(c, 0, 16) + at_row(widen(norm_g), 1, 16) + at_row(widen(m_norm_g), 4, 16)
               + at_row(widen(v_norm_g), 7, 16))
    (g_in,) = _exchange_only("ag_inputs", _Gather([pack_in]))
    c_all = g_in[:, 0, :]
    dn = D // N_DEV

    def full_norm(k):
        return jnp.transpose(g_in[:, k:k + 3, :dn], (1, 0, 2)).reshape(3, D)

    ng_full, m_ng_full, v_ng_full = full_norm(1), full_norm(4), full_norm(7)
    cs = at_row(c_all, 0, 16) + at_row(c_ctx[None, :], 8, 16)

    b_cols = lax.dynamic_slice_in_dim(b_mod, me * MODW, MODW, axis=1)
    mod_blk = _mod_fwd(cs, w_mod[0], b_cols)
    (mod_g,) = _exchange_only("ag_mod", _Gather([mod_blk]))
    mod_lat = lax.dynamic_index_in_dim(mod_g, me, axis=1, keepdims=False).reshape(9, D)
    mod_ctx = mod_g[:, 8, :].reshape(9, D)[:5]
    tab2 = jnp.concatenate([mod_lat, mod_ctx, ng_full, jnp.zeros((7, D), F32)], axis=0)
    tab3 = tab2[:, None, :]
    SH1, SC1, G1, SH2, SC2, G2, SH3, SC3, G3, MC0, MC1, MC2, MC3, MC4, GAM1, GAM2, GAM3 = range(17)

    wg1, wu1 = _exchange_only("ag_ffn1_gate_up", _Gather([bft(w_ffn1_gate), bft(w_ffn1_up)]))
    h1 = _norm_mod_fwd("nm1_fwd", xc0, tab3, GAM1, (SH1, MC0), (SC1, MC1), L, Lc)
    a1, b1, s1, (wd1,) = _ffn_up("ffn1", h1, wg1, wu1, comm=_Gather([bf(w_ffn1_down)]))
    f1, xc1, (win,) = _ffn_down("ffn1", s1, wd1, xc0, tab2, (G1, MC2), L, comm=_Gather([bf(w_in)]))

    h2 = _norm_mod_fwd("nm2_fwd", xc1, tab3, GAM2, (SH2, MC3), (SC2, MC4), L, Lc)
    tm = _tile(R, MM_TILE, LANES)
    tml = _tile(L, MM_TILE, LANES)

    (p01,), _ = _matmul(
        "in_proj_kvu", (2, R // tm), [h2, win],
        [pl.BlockSpec((tm, D), lambda j, i: (i, 0)), pl.BlockSpec((None, D, INS), lambda j, i: (j, 0, 0))],
        [(0, 1, 0, NN)], [jax.ShapeDtypeStruct((2, R, INS), F32)],
        [pl.BlockSpec((None, tm, INS), lambda j, i: (j, i, 0))], _store_all)
    (p27,), (wglu, wbra) = _matmul(
        "in_proj_qg", (6, L // tml), [h2, win],
        [pl.BlockSpec((tml, D), lambda j, i: (i, 0)), pl.BlockSpec((None, D, INS), lambda j, i: (j + 2, 0, 0))],
        [(0, 1, 0, NN)], [jax.ShapeDtypeStruct((6, L, INS), F32)],
        [pl.BlockSpec((None, tml, INS), lambda j, i: (j, i, 0))], _store_all,
        comm=_Gather([bf(w_glu), bf(w_br_attn)]))
    wglu2 = wglu.reshape(W, W)
    wbra2 = wbra.reshape(D, D)

    cos_all, sin_all = _rope_tables(L, Lc)
    cos_l, sin_l = cos_all[:L], sin_all[:L]

    q_rot = _qk_prep("q_prep", p27, 0, HBQ, NQ, L, q_norm_g, cos_l, sin_l)
    k_rot = _qk_prep("k_prep", p01, 0, NKV, NKV, R, k_norm_g, cos_all, sin_all)
    v_hd = _heads_cast("v_heads", p01, 1, NKV, NKV, R)
    attn, (wbrs, wout) = _attn_fwd(q_rot, k_rot, v_hd, QPK, comm=_Gather([bf(w_br_ssm), bf(w_out)]))
    wout2 = wout.reshape(D, D)

    t_rows = _tile(math.gcd(L, Lc), ROW_TILE, SUBLANES)
    nl, ncx = L // t_rows, Lc // t_rows
    nch = nl + ncx
    ns = G // SLAB_GROUPS
    ssm_prim = (ssm_a_re[0], ssm_a_im[0], ssm_log_dt[0], ssm_b_re[0], ssm_b_im[0])
    _, _, bt_re, bt_im = _ssm_discretize(*ssm_prim)
    pw_re, pw_im = _lambda_powers(ssm_a_re[0], ssm_a_im[0], ssm_log_dt[0], ns)
    bd_re = _block_diag(jnp.swapaxes(bt_re, 2, 3))
    bd_im = _block_diag(jnp.swapaxes(bt_im, 2, 3))
    ct_re = _block_diag(ssm_c_re[0])
    ct_im = _block_diag(-ssm_c_im[0])
    fwd_desc = (False, True)
    adj_desc = (True, False)
    s_bd = jnp.concatenate([bd_re, bd_im], axis=-1)
    s_ct = jnp.concatenate([ct_re, ct_im], axis=-1)
    s_bd16, s_ct16 = s_bd.astype(BF16), s_ct.astype(BF16)
    s_pw = jnp.pad(jnp.transpose(jnp.concatenate([pw_re, pw_im], axis=-1), (1, 2, 0, 3)),
                   ((0, 0), (0, 0), (0, 2 * SCAN_TAPS - SCAN_TAPS - 1), (0, 0)))
    s_tab = _carry_tables(pw_re, pw_im, fwd_desc)
    s_tabc = _carry_tables(pw_re, -pw_im, adj_desc)
    order = [lambda i: (i + nl) % nch, lambda i: nch - 1 - i]
    order_adj = [lambda i: (nch - 1 - i + nl) % nch, lambda i: i]
    y0, st0, (wg2,) = _ssm_fwd("ssm_fwd0", 0, p01, 1, s_bd, s_pw, s_tab, s_ct16, fwd_desc[0], order[0], t_rows, R,
                               comm=_Gather([bft(w_ffn2_gate)]))
    y1, st1, (wu2,) = _ssm_fwd("ssm_fwd1", 1, p01, 1, s_bd, s_pw, s_tab, s_ct16, fwd_desc[1], order[1], t_rows, R,
                               comm=_Gather([bft(w_ffn2_up)]))
    states = [st0, st1]

    tr = _row_tile(L, 0)
    rowW = pl.BlockSpec((tr, W), lambda i: (i, 0))
    vecW = pl.BlockSpec((1, W), lambda i: (0, 0))
    u_lat = pl.BlockSpec((None, tr, W), lambda i: (1, i, 0))

    def ssm_post(i, u, ya, yb, dvec):
        sv = dvec * u + ya + yb
        return [sv, _gelu(sv)], []

    (ssm_out, yg), _, _ = _rowwise(
        "ssm_post", L // tr, [p01, y0, y1, ssm_d], [u_lat, rowW, rowW, vecW],
        [jax.ShapeDtypeStruct((L, W), F32), jax.ShapeDtypeStruct((L, W), BF16)], [rowW, rowW], [], ssm_post)

    tnw = _tile(W, MM_TILE, LANES)

    def glu_epilogue(accs, ins, outs, pids):
        z = accs[0] + ins[3][...]
        outs[0][...] = z
        outs[1][...] = (_gelu(ins[2][...]) * _sigmoid(z)).astype(BF16)

    (z_glu, y2), _ = _matmul(
        "glu", (L // tml, W // tnw), [yg, wglu2, ssm_out, b_glu],
        [pl.BlockSpec((tml, W), lambda i, n: (i, 0)), pl.BlockSpec((W, tnw), lambda i, n: (0, n)),
         pl.BlockSpec((tml, tnw), lambda i, n: (i, n)), pl.BlockSpec((1, tnw), lambda i, n: (0, n))],
        [(0, 1, 0, NN)], [jax.ShapeDtypeStruct((L, W), F32), jax.ShapeDtypeStruct((L, W), BF16)],
        [pl.BlockSpec((tml, tnw), lambda i, n: (i, n))] * 2, glu_epilogue)

    tnd = _tile(D, MM_TILE, LANES)
    out_ld = pl.BlockSpec((tml, tnd), lambda i, n: (i, n))
    (br_a,), _ = _matmul(
        "br_attn", (L // tml, D // tnd), [attn, wbra2],
        [pl.BlockSpec((tml, D), lambda i, n: (i, 0)), pl.BlockSpec((D, tnd), lambda i, n: (0, n))],
        [(0, 1, 0, NN)], [jax.ShapeDtypeStruct((L, D), F32)], [out_ld], _store_all)

    cb = wbrs.shape[-1]
    gpb = INS // cb

    def gate_spec(first_shard):
        return pl.BlockSpec((None, tml, cb), lambda i, j: (first_shard + j // gpb, i, j % gpb))

    def merge_epilogue(accs, ins, outs, pids):
        br = accs[0]
        outs[0][...] = br
        outs[1][...] = (_sigmoid(ins[2][...]) * ins[4][...] + _sigmoid(ins[3][...]) * br).astype(BF16)

    col_blk = pl.BlockSpec((tml, cb), lambda i, j: (i, j))
    (br_s, merged), _ = _matmul(
        "br_ssm_merge", (L // tml, N_DEV), [y2, wbrs, p27, p27, br_a],
        [pl.BlockSpec((tml, W), lambda i, j: (i, 0)), pl.BlockSpec((None, W, cb), lambda i, j: (j, 0, 0)),
         gate_spec(2), gate_spec(4), col_blk],
        [(0, 1, 0, NN)], [jax.ShapeDtypeStruct((L, D), F32), jax.ShapeDtypeStruct((L, D), BF16)],
        [col_blk, col_blk], merge_epilogue)

    def out_epilogue(accs, ins, outs, pids):
        outs[0][...] = accs[0]
        outs[1][...] = ins[2][...] + ins[3][G2:G2 + 1, :] * accs[0]

    (mix, x2_), _ = _matmul(
        "out_proj", (L // tml, D // tnd), [merged, wout2, xc1, tab2],
        [pl.BlockSpec((tml, D), lambda i, n: (i, 0)), pl.BlockSpec((D, tnd), lambda i, n: (0, n)), out_ld,
         pl.BlockSpec((tab2.shape[0], tnd), lambda i, n: (0, n))],
        [(0, 1, 0, NN)], [jax.ShapeDtypeStruct((L, D), F32)] * 2, [out_ld, out_ld], out_epilogue)

    h3 = _norm_mod_fwd("nm3_fwd", x2_, tab3, GAM3, (SH3, SH3), (SC3, SC3), L, 0)
    a3, b3, s3, (wd2,) = _ffn_up("ffn2", h3, wg2, wu2, comm=_Gather([bf(w_ffn2_down)]))
    f3, x3, _ = _ffn_down("ffn2", s3, wd2, x2_, tab2, (G3, G3), L)

    trd = _row_tile(L, 0)
    rowD = pl.BlockSpec((trd, D), lambda i: (i, 0))

    def loss_fn(i, yv, t):
        err = yv - t
        return [err * (1.0 / D)], [_colsum(err * err)]

    (dx3,), (sq,), _ = _rowwise("loss", L // trd, [x3, tgt], [rowD, rowD],
                                [jax.ShapeDtypeStruct((L, D), F32)], [rowD], [D], loss_fn)
    loss = lax.psum(0.5 * jnp.sum(sq) / D, ("x", "y", "c"))

    core = lax.axis_index("c").astype(jnp.int32).reshape(1)
    chip = (2 * lax.axis_index("x") + lax.axis_index("y")).astype(jnp.int32).reshape(1)

    def pair_sums(tag, grads, halves):
        return [_pair_sum("pair_%s%d" % (tag, k), g_, h_, core) for k, (g_, h_) in enumerate(zip(grads, halves))]

    df3, (dg3, _) = _gate_bwd("gate3_bwd", dx3, f3, tab3, (G3, G3), 0.5, L, 0)
    dwd2, _ = _ffn_dwd("ffn2b", s3, df3)
    da3, db3, half_wd2 = _ffn_ds("ffn2b", df3, wd2, a3, b3, comm=_SiblingSwap([dwd2]))
    (p_wd2,) = pair_sums("wd2", [dwd2], half_wd2)
    dwg2, dwu2, (l_wd2,) = _ffn_dwgu("ffn2b", h3, da3, db3, comm=_ChipExchange([p_wd2]))
    dh3, half_wgu2 = _ffn_dh("ffn2b", da3, db3, wg2, wu2, comm=_SiblingSwap([dwg2, dwu2]))
    p_wg2, p_wu2 = pair_sums("wgu2", [dwg2, dwu2], half_wgu2)
    dx2, (dsh3, dsc3, _, _, dgam3) = _norm_mod_bwd("nm3_bwd", x2_, dh3, tab3, GAM3, (SC3, SC3), L, 0, dres=dx3)

    dmix, (dg2, _) = _gate_bwd("gate2_bwd", dx2, mix, tab3, (G2, G2), 1.0, L, 0)

    def dmerged_epilogue(accs, ins, outs, pids):
        dm = accs[0]
        ga, gs = _sigmoid(ins[2][...]), _sigmoid(ins[3][...])
        outs[0][...] = (ga * dm).astype(BF16)
        outs[1][...] = (gs * dm).astype(BF16)
        outs[2][...] = (dm * ins[4][...] * ga * (1.0 - ga)).astype(BF16)
        outs[3][...] = (dm * ins[5][...] * gs * (1.0 - gs)).astype(BF16)

    dgate_spec = pl.BlockSpec((None, tml, cb), lambda i, j: (j // gpb, i, j % gpb))
    (d_br_a, d_br_s, dg_a, dg_s), _ = _matmul(
        "dmerged", (L // tml, N_DEV), [dmix, wout2, p27, p27, br_a, br_s],
        [pl.BlockSpec((tml, D), lambda i, j: (i, 0)), pl.BlockSpec((cb, D), lambda i, j: (j, 0)),
         gate_spec(2), gate_spec(4), col_blk, col_blk],
        [(0, 1, 0, NT)],
        [jax.ShapeDtypeStruct((L, D), BF16)] * 2 + [jax.ShapeDtypeStruct((2, L, INS), BF16)] * 2,
        [col_blk, col_blk, dgate_spec, dgate_spec], dmerged_epilogue)

    def wgrad(name, a_mat, b_mat, tmo, tno):
        ka, ma = a_mat.shape
        _, nb_ = b_mat.shape
        return _matmul(
            name, (ma // tmo, nb_ // tno), [a_mat, b_mat],
            [pl.BlockSpec((ka, tmo), lambda m, n: (0, m)), pl.BlockSpec((ka, tno), lambda m, n: (0, n))],
            [(0, 1, 0, TN)], [jax.ShapeDtypeStruct((ma, nb_), BF16)],
            [pl.BlockSpec((tmo, tno), lambda m, n: (m, n))], _store_all)[0][0]

    dwout = wgrad("dw_out", merged, dmix, tnd, tnd)
    dwbra = wgrad("dw_br_attn", attn, d_br_a, tnd, tnd)
    (d_attn,), _ = _matmul(
        "d_attn", (L // tml, D // tnd), [d_br_a, wbra2],
        [pl.BlockSpec((tml, D), lambda i, n: (i, 0)), pl.BlockSpec((tnd, D), lambda i, n: (n, 0))],
        [(0, 1, 0, NT)], [jax.ShapeDtypeStruct((L, D), BF16)], [out_ld], _store_all)

    (dwbrs,), _ = _matmul(
        "dw_br_ssm", (N_DEV,), [y2, d_br_s],
        [pl.BlockSpec((L, W), lambda j: (0, 0)), pl.BlockSpec((L, cb), lambda j: (0, j))],
        [(0, 1, 0, TN)], [jax.ShapeDtypeStruct((N_DEV, W, cb), BF16)],
        [pl.BlockSpec((None, W, cb), lambda j: (j, 0, 0))], _store_all)

    def dy2_epilogue(accs, ins, outs, pids):
        dy2 = accs[0]
        sg = _sigmoid(ins[2][...])
        outs[0][...] = dy2 * sg
        outs[1][...] = (dy2 * _gelu(ins[3][...]) * sg * (1.0 - sg)).astype(BF16)

    wn_blk = pl.BlockSpec((tml, tnw), lambda i, n, k: (i, n))
    (dyg1, dz), _ = _matmul(
        "d_y2", (L // tml, W // tnw, N_DEV), [d_br_s, wbrs, z_glu, ssm_out],
        [pl.BlockSpec((tml, cb), lambda i, n, k: (i, k)), pl.BlockSpec((None, tnw, cb), lambda i, n, k: (k, n, 0)),
         wn_blk, wn_blk],
        [(0, 1, 0, NT)], [jax.ShapeDtypeStruct((L, W), F32), jax.ShapeDtypeStruct((L, W), BF16)],
        [wn_blk, wn_blk], dy2_epilogue, acc_shapes=[(tml, tnw)], nk=N_DEV)

    dwglu = wgrad("dw_glu", yg, dz, tnw, tnw)
    mix_grads = [dwout.reshape(N_DEV, D // N_DEV, D), dwbra.reshape(N_DEV, D // N_DEV, D), dwbrs,
                 dwglu.reshape(N_DEV, W // N_DEV, W)]

    def dssm_epilogue(accs, ins, outs, pids):
        outs[0][...] = (accs[0] + ins[2][...]) * _gelu_grad(ins[3][...])

    wn2 = pl.BlockSpec((tml, tnw), lambda i, n: (i, n))
    (dssm,), _ = _matmul(
        "d_ssm", (L // tml, W // tnw), [dz, wglu2, dyg1, ssm_out],
        [pl.BlockSpec((tml, W), lambda i, n: (i, 0)), pl.BlockSpec((tnw, W), lambda i, n: (n, 0)), wn2, wn2],
        [(0, 1, 0, NT)], [jax.ShapeDtypeStruct((L, W), F32)], [wn2], dssm_epilogue)

    dssm_all = jnp.concatenate([dssm, jnp.zeros((Lc, W), F32)], axis=0)
    du0, dbd0, dcd0, dlam0, (l_wg2, *half_mix) = _ssm_bwd(
        "ssm_bwd0", 0, dssm_all, p01, 1, states[0], s_ct, s_pw, s_tabc, s_bd16, adj_desc[0], order_adj[0], t_rows, R,
        comm=_Both([_ChipExchange([p_wg2]), _SiblingSwap(mix_grads)]))
    p_wout, p_wbra, p_wbrs, p_wglu = pair_sums("mix", mix_grads, half_mix)
    du1, dbd1, dcd1, dlam1, (l_wu2,) = _ssm_bwd(
        "ssm_bwd1", 1, dssm_all, p01, 1, states[1], s_ct, s_pw, s_tabc, s_bd16, adj_desc[1], order_adj[1], t_rows, R,
        comm=_ChipExchange([p_wu2]))

    trr = _row_tile(L, Lc)
    nlt = L // trr
    rowR = pl.BlockSpec((trr, W), lambda i: (i, 0))

    def du_fn(i, dua, dub, dsv, dvec, u):
        lat = (i < nlt).astype(F32)
        return [dua + dub + lat * (dvec * dsv)], [lat * _colsum(dsv * u)]

    (du_all,), (d_ssm_d,), _ = _rowwise(
        "du_combine", R // trr, [du0, du1, dssm_all, ssm_d, p01],
        [rowR, rowR, rowR, pl.BlockSpec((1, W), lambda i: (0, 0)), pl.BlockSpec((None, trr, W), lambda i: (1, i, 0))],
        [jax.ShapeDtypeStruct((R, W), BF16)], [rowR], [W], du_fn)

    def dz_sum(i, dzv):
        return [], [_colsum(dzv.astype(F32))]

    _, (d_b_glu,), _ = _rowwise("db_glu", L // tr, [dz], [rowW], [], [], [W], dz_sum)

    dbd, dcd, dlam = jnp.stack([dbd0, dbd1]), jnp.stack([dcd0, dcd1]), jnp.stack([dlam0, dlam1])
    dbt_re = jnp.swapaxes(_block_diag_extract(dbd[..., :SW], E, P), 2, 3)
    dbt_im = jnp.swapaxes(_block_diag_extract(dbd[..., SW:], E, P), 2, 3)
    dl_re, dl_im = dlam[:, :, 0, :SW].reshape(2, G, P), dlam[:, :, 0, SW:].reshape(2, G, P)
    _, vjp = jax.vjp(_ssm_discretize, *ssm_prim)
    d_a_re, d_a_im, d_ldt, d_b_re, d_b_im = vjp((dl_re, dl_im, dbt_re, dbt_im))
    d_c_re = jnp.swapaxes(_block_diag_extract(dcd[:, :, :SW, :], P, E), 2, 3)
    d_c_im = -jnp.swapaxes(_block_diag_extract(dcd[:, :, SW:, :], P, E), 2, 3)

    early_g = [d_a_re, d_a_im, d_ldt, d_b_re, d_b_im, d_c_re, d_c_im, d_ssm_d, d_b_glu]
    early_w = [ssm_a_re, ssm_a_im, ssm_log_dt, ssm_b_re, ssm_b_im, ssm_c_re, ssm_c_im, ssm_d, b_glu]
    early_m = [m_ssm_a_re, m_ssm_a_im, m_ssm_log_dt, m_ssm_b_re, m_ssm_b_im, m_ssm_c_re, m_ssm_c_im, m_ssm_d, m_b_glu]
    early_v = [v_ssm_a_re, v_ssm_a_im, v_ssm_log_dt, v_ssm_b_re, v_ssm_b_im, v_ssm_c_re, v_ssm_c_im, v_ssm_d, v_b_glu]
    early_shapes = [a.shape for a in early_w]
    early_rows = -(-sum(-(-math.prod(s) // LANES) for s in early_shapes) // 256) * 256

    dq_rot, dk_rot, dv_hd, (l_wout, l_wbra, l_wbrs, l_wglu, early_parts) = _attn_bwd(
        q_rot, k_rot, v_hd, d_attn, QPK,
        comm=_Both([_ChipExchange([p_wout, p_wbra, p_wbrs, p_wglu]), _Gather([_pack(early_g, early_rows)])]))
    dq_pre, d_qg = _qk_prep_bwd("q_prep_bwd", dq_rot, p27, 0, HBQ, NQ, L, q_norm_g, cos_l, sin_l)
    dk_pre, d_kg = _qk_prep_bwd("k_prep_bwd", dk_rot, p01, 0, NKV, NKV, R, k_norm_g, cos_all, sin_all)
    dv_pre = _heads_merge("dv_merge", dv_hd)

    def lat_blocks(a):
        return jnp.pad(a, ((0, 0), (0, Lc), (0, 0)))

    dp = jnp.concatenate([
        jnp.concatenate([dk_pre[0], dv_pre], axis=1)[None], du_all[None],
        lat_blocks(dq_pre), lat_blocks(dg_a), lat_blocks(dg_s)], axis=0)

    tmo = _tile(D, MM_TILE, LANES)
    (dwin,), _ = _matmul(
        "dw_in", (N_DEV, D // tmo), [h2, dp],
        [pl.BlockSpec((R, tmo), lambda j, m: (0, m)), pl.BlockSpec((None, R, INS), lambda j, m: (j, 0, 0))],
        [(0, 1, 0, TN)], [jax.ShapeDtypeStruct((N_DEV, D, INS), BF16)],
        [pl.BlockSpec((None, tmo, INS), lambda j, m: (j, m, 0))], _store_all)
    tnh = _tile(D, MM_TILE_NT, LANES)
    (dh2,), half_win = _matmul(
        "d_h2", (R // tm, D // tnh), [dp, win],
        [pl.BlockSpec((N_DEV, tm, INS), lambda i, n: (0, i, 0)),
         pl.BlockSpec((N_DEV, tnh, INS), lambda i, n: (0, n, 0))],
        [(0, 1, 0, NT, N_DEV)], [jax.ShapeDtypeStruct((R, D), F32)], [pl.BlockSpec((tm, tnh), lambda i, n: (i, n))],
        _store_all, comm=_SiblingSwap([dwin]))
    (p_win,) = pair_sums("win", [dwin], half_win)
    dxc1, (dsh2, dsc2, dmc3, dmc4, dgam2) = _norm_mod_bwd(
        "nm2_bwd", xc1, dh2, tab3, GAM2, (SC2, MC4), L, Lc, dres=dx2)

    df1, (dg1, dmc2) = _gate_bwd("gate1_bwd", dxc1, f1, tab3, (G1, MC2), 0.5, L, Lc)
    dwd1, _ = _ffn_dwd("ffn1b", s1, df1)
    da1, db1, half_wd1 = _ffn_ds("ffn1b", df1, wd1, a1, b1, comm=_SiblingSwap([dwd1]))
    (p_wd1,) = pair_sums("wd1", [dwd1], half_wd1)
    dwg1, dwu1, (l_wd1,) = _ffn_dwgu("ffn1b", h1, da1, db1, comm=_ChipExchange([p_wd1]))
    dh1, (l_win, *half_wgu1) = _ffn_dh(
        "ffn1b", da1, db1, wg1, wu1, comm=_Both([_ChipExchange([p_win]), _SiblingSwap([dwg1, dwu1])]))
    p_wg1, p_wu1 = pair_sums("wgu1", [dwg1, dwu1], half_wgu1)

    def adam_item(p, l_, w_, m_, v_):
        return (p, l_, w_[0], m_[0], v_[0])

    def adam_item_t(p, l_, w_, m_, v_):
        return (p, l_, held_t(w_), held_t(m_), held_t(v_))

    ready_a = [adam_item(p_wd1, l_wd1, w_ffn1_down, m_w_ffn1_down, v_w_ffn1_down),
               adam_item(p_win, l_win, w_in, m_w_in, v_w_in),
               adam_item(p_wglu, l_wglu, w_glu, m_w_glu, v_w_glu),
               adam_item(p_wbra, l_wbra, w_br_attn, m_w_br_attn, v_w_br_attn),
               adam_item(p_wbrs, l_wbrs, w_br_ssm, m_w_br_ssm, v_w_br_ssm)]
    ready_b = [adam_item(p_wout, l_wout, w_out, m_w_out, v_w_out),
               adam_item_t(p_wg2, l_wg2, w_ffn2_gate, m_w_ffn2_gate, v_w_ffn2_gate),
               adam_item_t(p_wu2, l_wu2, w_ffn2_up, m_w_ffn2_up, v_w_ffn2_up),
               adam_item(p_wd2, l_wd2, w_ffn2_down, m_w_ffn2_down, v_w_ffn2_down)]
    adam_a, (l_wg1,) = _owner_adam("adam_ready_a", ready_a, chip, comm=_ChipExchange([p_wg1]))
    adam_b, (l_wu1,) = _owner_adam("adam_ready_b", ready_b, chip, comm=_ChipExchange([p_wu1]))
    adam_ready = adam_a + adam_b
    dxc0, (dsh1, dsc1, dmc0, dmc1, dgam1) = _norm_mod_bwd(
        "nm1_bwd", xc0, dh1, tab3, GAM1, (SC1, MC1), L, Lc, dres=dxc1)
    grad_x = dxc0[:L][None]

    dmod_lat = jnp.concatenate([dsh1, dsc1, dg1, dsh2, dsc2, dg2, dsh3, dsc3, dg3], axis=1)
    dmod_ctx = jnp.concatenate([dmc0, dmc1, dmc2, dmc3, dmc4, jnp.zeros((1, 4 * D), F32)], axis=1)
    dmod_pack = at_row(dmod_lat, 0, SUBLANES) + at_row(dmod_ctx, 1, SUBLANES)
    (dmod_g,) = _exchange_only("ag_dmod", _Gather([dmod_pack]))
    dmod_all = dmod_g.reshape(N_DEV * SUBLANES, 9 * D)
    dmod_cols = lax.dynamic_slice_in_dim(dmod_all, me * MODW, MODW, axis=1)
    (g_wmod, dl_wmod, nm_wmod, nv_wmod, dsilu), _ = _mod_bwd_adam(
        cs, dmod_cols, w_mod[0], m_w_mod[0], v_w_mod[0])
    sg_cc = jax.nn.sigmoid(c_ctx)
    d_c_ctx = dsilu[8] * (sg_cc * (1.0 + c_ctx * (1.0 - sg_cc)))
    g_bmod, dl_bmod, nm_bmod, nv_bmod = _bias_adam(dmod_all, b_mod, m_b_mod, v_b_mod)

    dgam_all = jnp.concatenate([dgam1, dgam2, dgam3], axis=0)
    late_g = [d_c_ctx, d_qg, d_kg, dgam_all]
    late_w = [c_ctx, q_norm_g, k_norm_g, ng_full]
    late_m = [m_c_ctx, m_q_norm_g, m_k_norm_g, m_ng_full]
    late_v = [v_c_ctx, v_q_norm_g, v_k_norm_g, v_ng_full]
    late_shapes = [a.shape for a in late_w]
    late_rows = -(-sum(-(-math.prod(s) // LANES) for s in late_shapes) // SUBLANES) * SUBLANES
    (late_parts,) = _exchange_only("ag_small_grads", _Gather([_pack(late_g, late_rows)]))
    late_out = _sum_adam("small_adam_late", late_parts, _pack(late_w, late_rows), _pack(late_m, late_rows),
                         _pack(late_v, late_rows))
    early_out = _sum_adam("small_adam_s5", early_parts, _pack(early_w, early_rows), _pack(early_m, early_rows),
                          _pack(early_v, early_rows))

    def my_norm_cols(a):
        return lax.dynamic_slice_in_dim(a, me * dn, dn, axis=1)[None]

    small = []
    for lo, eo in zip(late_out, early_out):
        c_ctx_, qg_, kg_, ng_ = _unpack(lo, late_shapes)
        small.append([c_ctx_, qg_, kg_] + _unpack(eo, early_shapes) + [my_norm_cols(ng_)])
    sm_g, sm_dl, sm_m, sm_v = small

    adam_last, _ = _owner_adam(
        "adam_last", [adam_item_t(p_wg1, l_wg1, w_ffn1_gate, m_w_ffn1_gate, v_w_ffn1_gate),
                      adam_item_t(p_wu1, l_wu1, w_ffn1_up, m_w_ffn1_up, v_w_ffn1_up)], chip)
    transposed = (0, 1, 8, 9)
    big_out = [[(jnp.swapaxes(o, 0, 1) if k in transposed else o)[None] for o in grp_]
               for k, grp_ in enumerate(adam_last + adam_ready)]

    def leaf(kind):
        sm = (sm_g, sm_dl, sm_m, sm_v)[kind]
        mod = (g_wmod, dl_wmod, nm_wmod, nv_wmod)[kind][None]
        bmod = (g_bmod, dl_bmod, nm_bmod, nv_bmod)[kind]
        big = [b[kind] for b in big_out]
        (c_ctx_, qg_, kg_, a_re_, a_im_, ldt_, b_re_, b_im_, c_re_, c_im_, sd_, bglu_, ng_) = sm
        return [c_ctx_, mod, bmod, ng_, big[0], big[1], big[2], big[3], qg_, kg_, a_re_, a_im_, ldt_, b_re_, b_im_,
                c_re_, c_im_, sd_, big[4], bglu_, big[5], big[6], big[7], big[8], big[9], big[10]]

    return tuple([loss, grad_x] + leaf(0) + leaf(1) + leaf(2) + leaf(3))
```

```python
import math

import jax
import jax.numpy as jnp
import numpy as np
from jax import lax
from jax.experimental import pallas as pl
from jax.experimental.pallas import tpu as pltpu

F32 = jnp.float32
BF16 = jnp.bfloat16

N_DEV = 8
N_CHIPS = 4
LANES = 128
SUBLANES = 8
PACKED_SUBLANES = 16
VMEM_LIMIT = 56 * 1024 * 1024
MM_TILE = 512
MM_TILE_NT = 256
ROW_TILE = 256
HEAD_ROW_TILE = 512
ATTN_BWD_HEADS = 4
ATTN_FWD_HEADS = 2
ADAM_BLOCK_BYTES = 4 * 1024 * 1024
ADAM_GROUP_VMEM = 36 * 1024 * 1024

NORM_EPS = 1e-6
GRID_W = 64
ROPE_THETA = 10000.0
SCAN_TAPS = SUBLANES
SLAB_GROUPS = 8

ADAM_LR = 0.001
ADAM_B1 = 0.9
ADAM_B2 = 0.999
ADAM_EPS = 1e-08
ADAM_WD = 0.01
ADAM_STEP = 10

NN = (((1,), (0,)), ((), ()))
NT = (((1,), (1,)), ((), ()))
TN = (((0,), (0,)), ((), ()))

MESH = pl.DeviceIdType.MESH
ANY = pl.BlockSpec(memory_space=pl.ANY)


def _tile(n, cap, align):
    best = None
    for t in range(align, min(n, cap) + 1, align):
        if n % t == 0:
            best = t
    return n if best is None else best


def _params(n_grid):
    return pltpu.CompilerParams(dimension_semantics=("arbitrary",) * n_grid, vmem_limit_bytes=VMEM_LIMIT)


def _sigmoid(x):
    return 1.0 / (1.0 + jnp.exp(-x))


LOG2E = math.log2(math.e)
GELU_K = math.sqrt(2.0 / math.pi)
GELU_C = 0.044715


def _gelu(x):
    return 0.5 * x * (1.0 + jnp.tanh(GELU_K * (x + GELU_C * x * x * x)))


def _gelu_grad(x):
    t = jnp.tanh(GELU_K * (x + GELU_C * x * x * x))
    return 0.5 * (1.0 + t) + 0.5 * x * (1.0 - t * t) * GELU_K * (1.0 + 3.0 * GELU_C * x * x)


def _adamw(w, g, m, v):
    m2 = ADAM_B1 * m + (1.0 - ADAM_B1) * g
    v2 = ADAM_B2 * v + (1.0 - ADAM_B2) * (g * g)
    m_hat = m2 / (1.0 - ADAM_B1 ** ADAM_STEP)
    v_hat = v2 / (1.0 - ADAM_B2 ** ADAM_STEP)
    delta = -ADAM_LR * (m_hat / (jnp.sqrt(v_hat) + ADAM_EPS) + ADAM_WD * w)
    return delta, m2, v2


def _position():
    return lax.axis_index("x"), lax.axis_index("y"), lax.axis_index("c")


class _Gather:
    def __init__(self, arrays):
        self.arrays = list(arrays)
        n = len(self.arrays)
        self.out_shapes = [jax.ShapeDtypeStruct((N_DEV,) + a.shape, a.dtype) for a in self.arrays]
        self.scratch = [pltpu.SemaphoreType.DMA((n, 7)), pltpu.SemaphoreType.DMA((n, 7)),
                        pltpu.SemaphoreType.DMA((n,))]

    def _plan(self, ins, outs, sems):
        send, recv, local = sems
        x, y, c = _position()
        me, sibling = (x, y, c), (x, y, 1 - c)
        chips = [(1 - x, y), (x, 1 - y), (1 - x, 1 - y)]

        def slot(a, p):
            return outs[a].at[4 * p[0] + 2 * p[1] + p[2]]

        def copy(a, k, block, to, src=None):
            dst = slot(a, block)
            return pltpu.make_async_remote_copy(
                src_ref=dst if src is None else src, dst_ref=dst,
                send_sem=send.at[a, k], recv_sem=recv.at[a, k], device_id=to, device_id_type=MESH)

        mine = [pltpu.make_async_copy(ins[a], slot(a, me), local.at[a]) for a in range(len(ins))]
        return me, sibling, chips, c, copy, mine

    def start(self, ins, outs, sems):
        me, sibling, chips, c, copy, mine = self._plan(ins, outs, sems)
        for cp in mine:
            cp.start()
        for a in range(len(ins)):
            copy(a, 0, me, sibling, src=ins[a]).start()
            for j, chip in enumerate(chips):
                copy(a, 1 + j, me, (*chip, c), src=ins[a]).start()

    def finish(self, ins, outs, sems):
        me, sibling, chips, c, copy, mine = self._plan(ins, outs, sems)
        n = len(ins)
        for j, chip in enumerate(chips):
            for a in range(n):
                copy(a, 1 + j, (*chip, c), me).wait_recv()
                copy(a, 4 + j, (*chip, c), sibling).start()
        for a in range(n):
            copy(a, 0, sibling, me).wait_recv()
        for j, chip in enumerate(chips):
            for a in range(n):
                copy(a, 4 + j, (*chip, 1 - c), me).wait_recv()
        for a in range(n):
            copy(a, 0, me, sibling, src=ins[a]).wait_send()
            for j, chip in enumerate(chips):
                copy(a, 1 + j, me, (*chip, c), src=ins[a]).wait_send()
                copy(a, 4 + j, (*chip, c), sibling).wait_send()
        for cp in mine:
            cp.wait()


class _SiblingSwap:
    def __init__(self, arrays):
        self.arrays = list(arrays)
        n = len(self.arrays)
        self.out_shapes = [jax.ShapeDtypeStruct((N_CHIPS,) + a.shape[1:], a.dtype) for a in self.arrays]
        self.scratch = [pltpu.SemaphoreType.DMA((n, N_CHIPS)), pltpu.SemaphoreType.DMA((n, N_CHIPS))]

    def _plan(self, ins, outs, sems):
        send, recv = sems
        x, y, c = _position()
        return [pltpu.make_async_remote_copy(
            src_ref=ins[a].at[2 * j + 1 - c], dst_ref=outs[a].at[j],
            send_sem=send.at[a, j], recv_sem=recv.at[a, j], device_id=(x, y, 1 - c), device_id_type=MESH)
            for a in range(len(ins)) for j in range(N_CHIPS)]

    def start(self, ins, outs, sems):
        for cp in self._plan(ins, outs, sems):
            cp.start()

    def finish(self, ins, outs, sems):
        copies = self._plan(ins, outs, sems)
        for cp in copies:
            cp.wait_recv()
        for cp in copies:
            cp.wait_send()


class _ChipExchange:
    def __init__(self, arrays):
        self.arrays = list(arrays)
        n = len(self.arrays)
        self.out_shapes = [jax.ShapeDtypeStruct((N_CHIPS - 1,) + a.shape[1:], a.dtype) for a in self.arrays]
        self.scratch = [pltpu.SemaphoreType.DMA((n, N_CHIPS - 1)), pltpu.SemaphoreType.DMA((n, N_CHIPS - 1))]

    def _plan(self, ins, outs, sems):
        send, recv = sems
        x, y, c = _position()
        copies = []
        for r in range(1, N_CHIPS):
            px, py = x ^ (r >> 1), y ^ (r & 1)
            for a in range(len(ins)):
                copies.append(pltpu.make_async_remote_copy(
                    src_ref=ins[a].at[2 * px + py], dst_ref=outs[a].at[r - 1],
                    send_sem=send.at[a, r - 1], recv_sem=recv.at[a, r - 1],
                    device_id=(px, py, c), device_id_type=MESH))
        return copies

    def start(self, ins, outs, sems):
        for cp in self._plan(ins, outs, sems):
            cp.start()

    def finish(self, ins, outs, sems):
        copies = self._plan(ins, outs, sems)
        for cp in copies:
            cp.wait_recv()
        for cp in copies:
            cp.wait_send()


class _Both:
    def __init__(self, comms):
        self.comms = list(comms)
        self.arrays = [a for cm in self.comms for a in cm.arrays]
        self.out_shapes = [s for cm in self.comms for s in cm.out_shapes]
        self.scratch = [s for cm in self.comms for s in cm.scratch]

    def _split(self, ins, outs, sems):
        i = o = s = 0
        for cm in self.comms:
            ni, no, nsem = len(cm.arrays), len(cm.out_shapes), len(cm.scratch)
            yield cm, ins[i:i + ni], outs[o:o + no], sems[s:s + nsem]
            i, o, s = i + ni, o + no, s + nsem

    def start(self, ins, outs, sems):
        for cm, i, o, s in self._split(ins, outs, sems):
            cm.start(i, o, s)

    def finish(self, ins, outs, sems):
        for cm, i, o, s in self._split(ins, outs, sems):
            cm.finish(i, o, s)


def _host_call(body, *, name, grid, operands, in_specs, out_shape, out_specs, scratch_shapes=(), comm=None,
               prefetch=()):
    grid = tuple(grid)
    n_pre, n_in, n_out, n_scr = len(prefetch), len(operands), len(out_shape), len(scratch_shapes)
    nc_in, nc_out = (len(comm.arrays), len(comm.out_shapes)) if comm else (0, 0)
    all_in = list(in_specs) + [ANY] * nc_in
    all_out = list(out_specs) + [ANY] * nc_out
    all_scr = list(scratch_shapes) + (list(comm.scratch) if comm else [])
    all_shape = list(out_shape) + (list(comm.out_shapes) if comm else [])
    kwargs = dict(name=name, compiler_params=_params(len(grid)), out_shape=all_shape)
    if n_pre:
        kwargs["grid_spec"] = pltpu.PrefetchScalarGridSpec(
            num_scalar_prefetch=n_pre, grid=grid, in_specs=all_in, out_specs=all_out, scratch_shapes=all_scr)
    else:
        kwargs.update(in_specs=all_in, out_specs=all_out, scratch_shapes=all_scr)
        if grid:
            kwargs["grid"] = grid
    args = list(prefetch) + list(operands) + (list(comm.arrays) if comm else [])
    if comm is None:
        return list(pl.pallas_call(body, **kwargs)(*args)), []

    def hosted(*refs):
        bounds = [0, n_pre, n_pre + n_in]
        for n in (nc_in, n_out, nc_out, n_scr):
            bounds.append(bounds[-1] + n)
        bounds.append(len(refs))
        pre, ins, cins, outs, couts, scr, sems = [refs[a:b] for a, b in zip(bounds[:-1], bounds[1:])]
        if not grid:
            comm.start(cins, couts, sems)
            body(*pre, *ins, *outs, *scr)
            comm.finish(cins, couts, sems)
            return
        first, last = None, None
        for ax, size in enumerate(grid):
            pid = pl.program_id(ax)
            f, l = pid == 0, pid == size - 1
            first = f if first is None else jnp.logical_and(first, f)
            last = l if last is None else jnp.logical_and(last, l)

        @pl.when(first)
        def _():
            comm.start(cins, couts, sems)

        body(*pre, *ins, *outs, *scr)

        @pl.when(last)
        def _():
            comm.finish(cins, couts, sems)

    res = pl.pallas_call(hosted, **kwargs)(*args)
    return list(res[:n_out]), list(res[n_out:])


def _exchange_only(name, comm):
    def body():
        pass
    return _host_call(body, name=name, grid=(), operands=[], in_specs=[], out_shape=[], out_specs=[], comm=comm)[1]


def _matmul(name, grid, operands, in_specs, pairs, out_shapes, out_specs, epilogue, acc_shapes=(), nk=1,
            prologue=None, comm=None):
    n_in, n_out = len(operands), len(out_shapes)
    prologue = prologue or {}

    def body(*refs):
        ins, outs, accs = refs[:n_in], refs[n_in:n_in + n_out], refs[n_in + n_out:]
        pids = [pl.program_id(ax) for ax in range(len(grid))]

        def operand(i, blk=None):
            v = ins[i][...] if blk is None else ins[i][blk]
            if i in prologue:
                v = prologue[i](v)
            return v.astype(BF16)

        def products():
            vals = {}
            for pair in pairs:
                ai, bi, ci, dn = pair[:4]
                if len(pair) == 5:
                    p = None
                    for blk in range(pair[4]):
                        q = lax.dot_general(operand(ai, blk), operand(bi, blk), dn, preferred_element_type=F32)
                        p = q if p is None else p + q
                else:
                    p = lax.dot_general(operand(ai), operand(bi), dn, preferred_element_type=F32)
                vals[ci] = p if ci not in vals else vals[ci] + p
            return [vals[ci] for ci in sorted(vals)]

        if nk == 1:
            epilogue(products(), ins, outs, pids)
        else:
            k = pids[-1]
            prods = products()

            @pl.when(k == 0)
            def _():
                for acc, p in zip(accs, prods):
                    acc[...] = p

            @pl.when(k > 0)
            def _():
                for acc, p in zip(accs, prods):
                    acc[...] += p

            @pl.when(k == nk - 1)
            def _():
                epilogue([acc[...] for acc in accs], ins, outs, pids)

    return _host_call(
        body, name=name, grid=grid, operands=operands, in_specs=in_specs, out_shape=out_shapes, out_specs=out_specs,
        scratch_shapes=[pltpu.VMEM(s, F32) for s in acc_shapes] if nk > 1 else [], comm=comm)


def _rowwise(name, n_tiles, operands, in_specs, out_shapes, out_specs, red_widths, fn, comm=None):
    n_in, n_out, n_red = len(operands), len(out_shapes), len(red_widths)

    def body(*refs):
        ins, outs, reds = refs[:n_in], refs[n_in:n_in + n_out], refs[n_in + n_out:]
        i = pl.program_id(0)
        vals, sums = fn(i, *[r[...] for r in ins])
        for o, v in zip(outs, vals):
            o[...] = v.astype(o.dtype)
        if n_red:
            @pl.when(i == 0)
            def _():
                for r, s in zip(reds, sums):
                    r[...] = s

            @pl.when(i > 0)
            def _():
                for r, s in zip(reds, sums):
                    r[...] += s

    red_shapes = [jax.ShapeDtypeStruct((1, w), F32) for w in red_widths]
    red_specs = [pl.BlockSpec((1, w), lambda i: (0, 0)) for w in red_widths]
    res, cres = _host_call(
        body, name=name, grid=(n_tiles,), operands=operands, in_specs=in_specs,
        out_shape=list(out_shapes) + red_shapes, out_specs=list(out_specs) + red_specs, comm=comm)
    return res[:n_out], res[n_out:], cres


def _colsum(v):
    return jnp.sum(v, axis=0, keepdims=True)


def _store_all(accs, ins, outs, pids):
    for o, v in zip(outs, accs):
        o[...] = v.astype(o.dtype)


def _row_tile(rows_a, rows_b):
    return _tile(math.gcd(rows_a, rows_b) if rows_b else rows_a, ROW_TILE, SUBLANES)


def _tab_row(d, nlt, rows2):
    return pl.BlockSpec((None, 1, d), lambda i: (jnp.where(i < nlt, rows2[0], rows2[1]), 0, 0))


def _norm_mod_fwd(name, xs, tab, r_gamma, r_shift, r_scale, n_lat, n_ctx):
    rows, d = xs.shape
    tm = _row_tile(n_lat, n_ctx)
    nlt = n_lat // tm

    def fn(i, x, g, sh, sc):
        xh = x * lax.rsqrt(jnp.mean(x * x, axis=-1, keepdims=True) + NORM_EPS)
        return [(xh * g) * (1.0 + sc) + sh], []

    (h,), _, _ = _rowwise(
        name, rows // tm, [xs, tab, tab, tab],
        [pl.BlockSpec((tm, d), lambda i: (i, 0)), _tab_row(d, nlt, (r_gamma, r_gamma)), _tab_row(d, nlt, r_shift),
         _tab_row(d, nlt, r_scale)],
        [jax.ShapeDtypeStruct((rows, d), BF16)], [pl.BlockSpec((tm, d), lambda i: (i, 0))], [], fn)
    return h


def _norm_mod_bwd(name, xs, dh, tab, r_gamma, r_scale, n_lat, n_ctx, dres=None):
    rows, d = xs.shape
    tm = _row_tile(n_lat, n_ctx)
    nlt = n_lat // tm
    row = pl.BlockSpec((tm, d), lambda i: (i, 0))

    def fn(i, x, dy, g, sc, *res):
        rstd = lax.rsqrt(jnp.mean(x * x, axis=-1, keepdims=True) + NORM_EPS)
        xh = x * rstd
        dsh = _colsum(dy)
        dsc = _colsum(dy * (xh * g))
        dn = dy * (1.0 + sc)
        dgam = _colsum(dn * xh)
        dxh = dn * g
        dx = rstd * (dxh - xh * jnp.mean(dxh * xh, axis=-1, keepdims=True))
        if res:
            dx = dx + jnp.where(i < nlt, res[0], 0.0)
        lat = (i < nlt).astype(F32)
        return [dx], [dsh * lat, dsc * lat, dsh * (1.0 - lat), dsc * (1.0 - lat), dgam]

    operands = [xs, dh, tab, tab]
    specs = [row, row, _tab_row(d, nlt, (r_gamma, r_gamma)), _tab_row(d, nlt, r_scale)]
    if dres is not None:
        operands.append(dres)
        specs.append(pl.BlockSpec((tm, d), lambda i: (jnp.minimum(i, nlt - 1), 0)))
    (dx,), sums, _ = _rowwise(name, rows // tm, operands, specs,
                              [jax.ShapeDtypeStruct((rows, d), F32)], [row], [d] * 5, fn)
    return dx, sums


def _gate_bwd(name, dx, f, tab, r_gate, coef, n_lat, n_ctx):
    rows, d = dx.shape
    tm = _row_tile(n_lat, n_ctx)
    nlt = n_lat // tm
    row = pl.BlockSpec((tm, d), lambda i: (i, 0))

    def fn(i, dxv, fv, gv):
        dg = _colsum(dxv * fv) * coef
        lat = (i < nlt).astype(F32)
        return [(coef * gv) * dxv], [dg * lat, dg * (1.0 - lat)]

    (df,), sums, _ = _rowwise(
        name, rows // tm, [dx, f, tab],
        [row, row, _tab_row(d, nlt, r_gate)],
        [jax.ShapeDtypeStruct((rows, d), BF16)], [row], [d, d], fn)
    return df, sums


def _select_rows(i, tm, n_lat, v_lat, v_ctx):
    rows = i * tm + lax.broadcasted_iota(jnp.int32, (tm, 1), 0)
    return jnp.where(rows < n_lat, v_lat, v_ctx)


def _ffn_up(tag, h, wg, wu, comm=None):
    rows, d = h.shape
    nb, fs, _ = wg.shape
    tm = _tile(rows, MM_TILE, LANES)
    blk = pl.BlockSpec((None, tm, fs), lambda j, i: (j, i, 0))
    wspec = pl.BlockSpec((None, fs, d), lambda j, i: (j, 0, 0))

    def epilogue(accs, ins, outs, pids):
        a, b = accs
        outs[0][...] = a.astype(BF16)
        outs[1][...] = b.astype(BF16)
        outs[2][...] = (a * _sigmoid(a) * b).astype(BF16)

    hid = jax.ShapeDtypeStruct((nb, rows, fs), BF16)
    (a, b, s), cres = _matmul(
        tag + "_up", (nb, rows // tm), [h, wg, wu],
        [pl.BlockSpec((tm, d), lambda j, i: (i, 0)), wspec, wspec],
        [(0, 1, 0, NT), (0, 2, 1, NT)], [hid, hid, hid], [blk, blk, blk], epilogue, comm=comm)
    return a, b, s, cres


def _ffn_down(tag, s, wd, xs, tab2, r_gate, n_lat, comm=None):
    nb, rows, fs = s.shape
    d = wd.shape[-1]
    tm = _tile(rows, MM_TILE, LANES)
    tn = _tile(d, MM_TILE, LANES)

    def epilogue(accs, ins, outs, pids):
        f = accs[0]
        g = ins[3][...]
        gate = _select_rows(pids[0], tm, n_lat, g[r_gate[0]:r_gate[0] + 1, :], g[r_gate[1]:r_gate[1] + 1, :])
        outs[0][...] = f
        outs[1][...] = ins[2][...] + 0.5 * gate * f

    out = jax.ShapeDtypeStruct((rows, d), F32)
    ospec = pl.BlockSpec((tm, tn), lambda i, n: (i, n))
    (f, xo), cres = _matmul(
        tag + "_down", (rows // tm, d // tn), [s, wd, xs, tab2],
        [pl.BlockSpec((nb, tm, fs), lambda i, n: (0, i, 0)), pl.BlockSpec((nb, fs, tn), lambda i, n: (0, 0, n)),
         ospec, pl.BlockSpec((tab2.shape[0], tn), lambda i, n: (0, n))],
        [(0, 1, 0, NN, nb)], [out, out], [ospec, ospec], epilogue, comm=comm)
    return f, xo, cres


def _ffn_ds(tag, df, wd, a, b, comm=None):
    rows, d = df.shape
    nb, fs, _ = wd.shape
    tm = _tile(rows, MM_TILE, LANES)
    blk = pl.BlockSpec((None, tm, fs), lambda j, i: (j, i, 0))

    def epilogue(accs, ins, outs, pids):
        ds = accs[0]
        av = ins[2][...].astype(F32)
        bv = ins[3][...].astype(F32)
        sg = _sigmoid(av)
        outs[0][...] = (ds * bv * (sg * (1.0 + av * (1.0 - sg)))).astype(BF16)
        outs[1][...] = (ds * (av * sg)).astype(BF16)

    hid = jax.ShapeDtypeStruct((nb, rows, fs), BF16)
    (da, db), cres = _matmul(
        tag + "_ds", (nb, rows // tm), [df, wd, a, b],
        [pl.BlockSpec((tm, d), lambda j, i: (i, 0)), pl.BlockSpec((None, fs, d), lambda j, i: (j, 0, 0)), blk, blk],
        [(0, 1, 0, NT)], [hid, hid], [blk, blk], epilogue, comm=comm)
    return da, db, cres


def _ffn_dwd(tag, s, df, comm=None):
    nb, rows, fs = s.shape
    d = df.shape[-1]
    tn = _tile(d, MM_TILE, LANES)
    (dwd,), cres = _matmul(
        tag + "_dwd", (nb, d // tn), [s, df],
        [pl.BlockSpec((None, rows, fs), lambda j, n: (j, 0, 0)), pl.BlockSpec((rows, tn), lambda j, n: (0, n))],
        [(0, 1, 0, TN)], [jax.ShapeDtypeStruct((nb, fs, d), BF16)],
        [pl.BlockSpec((None, fs, tn), lambda j, n: (j, 0, n))], _store_all, comm=comm)
    return dwd, cres


def _ffn_dwgu(tag, h, da, db, comm=None):
    rows, d = h.shape
    nb, _, fs = da.shape
    tno = _tile(d, MM_TILE, LANES)
    full = pl.BlockSpec((None, rows, fs), lambda j, m: (j, 0, 0))
    wshape = jax.ShapeDtypeStruct((nb, fs, d), BF16)
    wblk = pl.BlockSpec((None, fs, tno), lambda j, m: (j, 0, m))
    (dwg, dwu), cres = _matmul(
        tag + "_dwgu", (nb, d // tno), [h, da, db],
        [pl.BlockSpec((rows, tno), lambda j, m: (0, m)), full, full],
        [(1, 0, 0, TN), (2, 0, 1, TN)], [wshape, wshape], [wblk, wblk], _store_all, comm=comm)
    return dwg, dwu, cres


def _ffn_dh(tag, da, db, wg, wu, comm=None):
    nb, rows, fs = da.shape
    d = wg.shape[2]
    tm = _tile(rows, MM_TILE, LANES)
    tn = _tile(d, MM_TILE_NT, LANES)
    aspec = pl.BlockSpec((nb, tm, fs), lambda i, n: (0, i, 0))
    wspec = pl.BlockSpec((nb, fs, tn), lambda i, n: (0, 0, n))
    (dh,), cres = _matmul(
        tag + "_dh", (rows // tm, d // tn), [da, wg, db, wu], [aspec, wspec, aspec, wspec],
        [(0, 1, 0, NN, nb), (2, 3, 0, NN, nb)], [jax.ShapeDtypeStruct((rows, d), F32)],
        [pl.BlockSpec((tm, tn), lambda i, n: (i, n))], _store_all, comm=comm)
    return dh, cres


def _rope_tables(n_lat, n_ctx):
    half = LANES // 4
    inv_freq = (np.float32(ROPE_THETA) ** (-np.arange(half, dtype=np.float32) / np.float32(half))).astype(np.float32)
    pos = np.arange(n_lat)
    ang_r = (pos // GRID_W).astype(np.float32)[:, None] * inv_freq
    ang_c = (pos % GRID_W).astype(np.float32)[:, None] * inv_freq
    cos_l = np.concatenate([np.cos(ang_r)] * 2 + [np.cos(ang_c)] * 2, axis=1)
    sin_l = np.concatenate([-np.sin(ang_r), np.sin(ang_r), -np.sin(ang_c), np.sin(ang_c)], axis=1)
    cos_all = np.concatenate([cos_l, np.ones((n_ctx, LANES), np.float32)], axis=0).astype(np.float32)
    sin_all = np.concatenate([sin_l, np.zeros((n_ctx, LANES), np.float32)], axis=0).astype(np.float32)
    return jnp.asarray(cos_all), jnp.asarray(sin_all)


def _swap_halves(x):
    lane = lax.broadcasted_iota(jnp.int32, x.shape, 1)
    return jnp.where((lane % 64) < 32, pltpu.roll(x, 96, 1), pltpu.roll(x, 32, 1))


def _heads_spec(tq, hb, width, first_block):
    per_shard = width // (hb * LANES)

    def index(k, i):
        blk = first_block + k
        return blk // per_shard, i, blk % per_shard
    return pl.BlockSpec((None, tq, hb * LANES), index)


def _qk_prep(name, src, first_block, hb, n_heads, rows, g, cos_t, sin_t):
    tq = _tile(rows, HEAD_ROW_TILE, SUBLANES)
    tab = pl.BlockSpec((tq, LANES), lambda k, i: (i, 0))

    def body(x_ref, g_ref, c_ref, s_ref, o_ref):
        for h in range(hb):
            x = x_ref[:, h * LANES:(h + 1) * LANES]
            n = x * lax.rsqrt(jnp.mean(x * x, axis=-1, keepdims=True) + NORM_EPS) * g_ref[...]
            o_ref[h] = (n * c_ref[...] + _swap_halves(n) * s_ref[...]).astype(BF16)

    return pl.pallas_call(
        body, name=name, grid=(n_heads // hb, rows // tq),
        in_specs=[_heads_spec(tq, hb, src.shape[-1], first_block), pl.BlockSpec((1, LANES), lambda k, i: (0, 0)),
                  tab, tab],
        out_specs=pl.BlockSpec((hb, tq, LANES), lambda k, i: (k, i, 0)),
        out_shape=jax.ShapeDtypeStruct((n_heads, rows, LANES), BF16), compiler_params=_params(2),
    )(src, g, cos_t, sin_t)


def _qk_prep_bwd(name, dy, src, first_block, hb, n_heads, rows, g, cos_t, sin_t):
    tq = _tile(rows, HEAD_ROW_TILE, SUBLANES)
    tab = pl.BlockSpec((tq, LANES), lambda k, i: (i, 0))

    def body(dy_ref, x_ref, g_ref, c_ref, s_ref, dx_ref, dg_ref):
        g = g_ref[...]
        dg = None
        for h in range(hb):
            x = x_ref[:, h * LANES:(h + 1) * LANES]
            dyv = dy_ref[h]
            rstd = lax.rsqrt(jnp.mean(x * x, axis=-1, keepdims=True) + NORM_EPS)
            xh = x * rstd
            dn = dyv * c_ref[...] + _swap_halves(dyv * s_ref[...])
            dxh = dn * g
            dx = rstd * (dxh - xh * jnp.mean(dxh * xh, axis=-1, keepdims=True))
            dx_ref[:, h * LANES:(h + 1) * LANES] = dx.astype(BF16)
            part = _colsum(dn * xh)
            dg = part if dg is None else dg + part
        first = jnp.logical_and(pl.program_id(0) == 0, pl.program_id(1) == 0)

        @pl.when(first)
        def _():
            dg_ref[...] = dg

        @pl.when(jnp.logical_not(first))
        def _():
            dg_ref[...] += dg

    return pl.pallas_call(
        body, name=name, grid=(n_heads // hb, rows // tq),
        in_specs=[pl.BlockSpec((hb, tq, LANES), lambda k, i: (k, i, 0)),
                  _heads_spec(tq, hb, src.shape[-1], first_block),
                  pl.BlockSpec((1, LANES), lambda k, i: (0, 0)), tab, tab],
        out_specs=[pl.BlockSpec((None, tq, hb * LANES), lambda k, i: (k, i, 0)),
                   pl.BlockSpec((1, LANES), lambda k, i: (0, 0))],
        out_shape=[jax.ShapeDtypeStruct((n_heads // hb, rows, hb * LANES), BF16),
                   jax.ShapeDtypeStruct((1, LANES), F32)],
        compiler_params=_params(2),
    )(dy, src, g, cos_t, sin_t)


def _heads_cast(name, src, first_block, hb, n_heads, rows):
    tq = _tile(rows, HEAD_ROW_TILE, SUBLANES)

    def body(x_ref, o_ref):
        for h in range(hb):
            o_ref[h] = x_ref[:, h * LANES:(h + 1) * LANES].astype(BF16)

    return pl.pallas_call(
        body, name=name, grid=(n_heads // hb, rows // tq),
        in_specs=[_heads_spec(tq, hb, src.shape[-1], first_block)],
        out_specs=pl.BlockSpec((hb, tq, LANES), lambda k, i: (k, i, 0)),
        out_shape=jax.ShapeDtypeStruct((n_heads, rows, LANES), BF16), compiler_params=_params(2),
    )(src)


def _heads_merge(name, src):
    n_heads, rows, _ = src.shape
    tq = _tile(rows, HEAD_ROW_TILE, SUBLANES)

    def body(x_ref, o_ref):
        for h in range(n_heads):
            o_ref[:, h * LANES:(h + 1) * LANES] = x_ref[h].astype(BF16)

    return pl.pallas_call(
        body, name=name, grid=(rows // tq,),
        in_specs=[pl.BlockSpec((n_heads, tq, LANES), lambda i: (0, i, 0))],
        out_specs=pl.BlockSpec((tq, n_heads * LANES), lambda i: (i, 0)),
        out_shape=jax.ShapeDtypeStruct((rows, n_heads * LANES), BF16), compiler_params=_params(1),
    )(src)


def _attn_fwd(q, k, v, q_per_kv, comm=None):
    nq, l, _ = q.shape
    s_len = k.shape[1]
    tq = _tile(l, ROW_TILE, SUBLANES)
    scale = LANES ** -0.5
    hp = ATTN_FWD_HEADS if q_per_kv % ATTN_FWD_HEADS == 0 else 1
    kv = pl.BlockSpec((None, s_len, LANES), lambda h, i: ((h * hp) // q_per_kv, 0, 0))

    def body(q_ref, k_ref, v_ref, o_ref):
        for h in range(hp):
            s = lax.dot_general(q_ref[h], k_ref[...], NT, preferred_element_type=F32)
            p = jnp.exp2((s - jnp.max(s, axis=-1, keepdims=True)) * (scale * LOG2E))
            den = jnp.sum(p, axis=-1, keepdims=True)
            o = jnp.dot(p.astype(BF16), v_ref[...], preferred_element_type=F32)
            o_ref[:, h * LANES:(h + 1) * LANES] = (o * (1.0 / den)).astype(BF16)

    (o,), cres = _host_call(
        body, name="attn_fwd", grid=(nq // hp, l // tq), operands=[q, k, v],
        in_specs=[pl.BlockSpec((hp, tq, LANES), lambda h, i: (h, i, 0)), kv, kv],
        out_shape=[jax.ShapeDtypeStruct((l, nq * LANES), BF16)],
        out_specs=[pl.BlockSpec((tq, hp * LANES), lambda h, i: (i, h))], comm=comm)
    return o, cres


def _attn_bwd(q, k, v, do, q_per_kv, comm=None):
    nq, l, _ = q.shape
    nkv, s_len, _ = k.shape
    tq = _tile(l, ROW_TILE, SUBLANES)
    scale = LANES ** -0.5
    hp = ATTN_BWD_HEADS if q_per_kv % ATTN_BWD_HEADS == 0 else 1
    kv = pl.BlockSpec((None, s_len, LANES), lambda g, r, i: (g, 0, 0))
    qs = pl.BlockSpec((hp, tq, LANES), lambda g, r, i: (g * (q_per_kv // hp) + r, i, 0))

    def body(q_ref, k_ref, v_ref, do_ref, dq_ref, dk_ref, dv_ref):
        kvv, vv = k_ref[...], v_ref[...]
        dk_new = dv_new = None
        for h in range(hp):
            qv, dov = q_ref[h], do_ref[:, h * LANES:(h + 1) * LANES]
            st = lax.dot_general(kvv, qv, NT, preferred_element_type=F32)
            e = jnp.exp2((st - jnp.max(st, axis=0, keepdims=True)) * (scale * LOG2E))
            pt = e * (1.0 / jnp.sum(e, axis=0, keepdims=True))
            dpt = lax.dot_general(vv, dov, NT, preferred_element_type=F32)
            delta = jnp.sum(pt * dpt, axis=0, keepdims=True)
            dst = (pt * (dpt - delta)).astype(BF16)
            dq_ref[h] = lax.dot_general(dst, kvv, TN, preferred_element_type=F32) * scale
            dk_h = jnp.dot(dst, qv, preferred_element_type=F32) * scale
            dv_h = jnp.dot(pt.astype(BF16), dov, preferred_element_type=F32)
            dk_new = dk_h if dk_new is None else dk_new + dk_h
            dv_new = dv_h if dv_new is None else dv_new + dv_h
        first = jnp.logical_and(pl.program_id(1) == 0, pl.program_id(2) == 0)

        @pl.when(first)
        def _():
            dk_ref[...] = dk_new
            dv_ref[...] = dv_new

        @pl.when(jnp.logical_not(first))
        def _():
            dk_ref[...] += dk_new
            dv_ref[...] += dv_new

    (dq, dk, dv), cres = _host_call(
        body, name="attn_bwd", grid=(nkv, q_per_kv // hp, l // tq), operands=[q, k, v, do],
        in_specs=[qs, kv, kv, pl.BlockSpec((tq, hp * LANES), lambda g, r, i: (i, g * (q_per_kv // hp) + r))],
        out_specs=[qs, kv, kv],
        out_shape=[jax.ShapeDtypeStruct((nq, l, LANES), F32), jax.ShapeDtypeStruct((nkv, s_len, LANES), F32),
                   jax.ShapeDtypeStruct((nkv, s_len, LANES), F32)], comm=comm)
    return dq, dk, dv, cres


def _zoh(a_re, a_im, log_dt):
    dt = jnp.exp(log_dt)[..., None]
    mag = jnp.exp(a_re * dt)
    lb_re = mag * jnp.cos(a_im * dt)
    lb_im = mag * jnp.sin(a_im * dt)
    den = a_re * a_re + a_im * a_im
    coef_re = ((lb_re - 1.0) * a_re + lb_im * a_im) / den
    coef_im = (lb_im * a_re - (lb_re - 1.0) * a_im) / den
    return lb_re, lb_im, coef_re, coef_im


def _ssm_discretize(a_re, a_im, log_dt, b_re, b_im):
    lb_re, lb_im, cr, ci = _zoh(a_re, a_im, log_dt)
    bt_re = cr[..., None] * b_re - ci[..., None] * b_im
    bt_im = cr[..., None] * b_im + ci[..., None] * b_re
    return lb_re, lb_im, bt_re, bt_im


def _lambda_powers(a_re, a_im, log_dt, ns):
    dt = jnp.exp(log_dt)[..., None]
    k = jnp.arange(SCAN_TAPS + 1, dtype=F32)[:, None, None, None]
    mag, ang = jnp.exp(k * (a_re * dt)), k * (a_im * dt)
    shape = (SCAN_TAPS + 1, 2, ns, -1)
    return (mag * jnp.cos(ang)).reshape(shape), (mag * jnp.sin(ang)).reshape(shape)


def _slab_mask():
    idx = jnp.arange(SLAB_GROUPS)
    return (idx[:, None] == idx[None, :])[None, None, :, None, :, None]


def _block_diag(m):
    d, g, a, b = m.shape
    ns = g // SLAB_GROUPS
    wide = jnp.where(_slab_mask(), m.reshape(d, ns, SLAB_GROUPS, a, 1, b), 0.0)
    return wide.reshape(d, ns, SLAB_GROUPS * a, SLAB_GROUPS * b)


def _block_diag_extract(m, a, b):
    d, ns = m.shape[:2]
    m = m.reshape(d, ns, SLAB_GROUPS, a, SLAB_GROUPS, b)
    return jnp.sum(jnp.where(_slab_mask(), m, 0.0), axis=4).reshape(d, ns * SLAB_GROUPS, a, b)


def _build_tap_weights(w_ref, base_ref, pw_ref, conj, sw):
    b_re, b_im = base_ref[:, :sw], base_ref[:, sw:]
    for tau in range(SCAN_TAPS):
        p_re, p_im = pw_ref[tau:tau + 1, :sw], pw_ref[tau:tau + 1, sw:]
        if conj:
            p_im = -p_im
        w_ref[tau * LANES:(tau + 1) * LANES, :sw] = (p_re * b_re - p_im * b_im).astype(BF16)
        w_ref[tau * LANES:(tau + 1) * LANES, sw:] = (p_re * b_im + p_im * b_re).astype(BF16)


def _carry_tables(pw_re, pw_im, descending):
    def rows(pw):
        asc = pw[1:]
        per_dir = [asc[::-1, d] if descending[d] else asc[:, d] for d in range(2)]
        return jnp.transpose(jnp.stack(per_dir), (0, 2, 1, 3))
    return jnp.concatenate([rows(pw_re), rows(pw_im)], axis=-1)


def _scan_chunk(x, w_ref, tab_ref, s_ref, carry_ref, descending, t_rows, sw):
    row8 = lax.broadcasted_iota(jnp.int32, x.shape, 0) % SCAN_TAPS
    pieces = [x.astype(BF16)]
    for tau in range(1, SCAN_TAPS):
        if descending:
            sh = jnp.where(row8 <= SCAN_TAPS - 1 - tau, pltpu.roll(x, t_rows - tau, 0), 0.0)
        else:
            sh = jnp.where(row8 >= tau, pltpu.roll(x, tau, 0), 0.0)
        pieces.append(sh.astype(BF16))
    xa = jnp.concatenate(pieces, axis=1)
    s_ref[...] = jnp.dot(xa, w_ref[...], preferred_element_type=F32)
    tab = tab_ref[...]
    t_re, t_im = tab[:, :sw], tab[:, sw:]
    nb = t_rows // SCAN_TAPS
    edge = 0 if descending else SCAN_TAPS - 1

    def step(b, carry):
        h_re, h_im = carry
        r0 = pl.multiple_of(((nb - 1 - b) if descending else b) * SCAN_TAPS, SCAN_TAPS)
        x_re = s_ref[pl.ds(r0, SCAN_TAPS), :sw] + t_re * h_re - t_im * h_im
        x_im = s_ref[pl.ds(r0, SCAN_TAPS), sw:] + t_re * h_im + t_im * h_re
        s_ref[pl.ds(r0, SCAN_TAPS), :sw] = x_re
        s_ref[pl.ds(r0, SCAN_TAPS), sw:] = x_im
        return x_re[edge:edge + 1, :], x_im[edge:edge + 1, :]

    h_re, h_im = lax.fori_loop(0, nb, step, (carry_ref[0:1, :sw], carry_ref[0:1, sw:]))
    carry_ref[0:1, :sw] = h_re
    carry_ref[0:1, sw:] = h_im


def _slab_spec(rows, cols, dr):
    return pl.BlockSpec((None, None, rows, cols), lambda s, i: (dr, s, 0, 0))


def _ssm_fwd(name, dr, u_src, u_shard, bd, pw, tab, ct, descending, chunk_of, t_rows, rows, comm=None):
    _, ns, _, sw2 = bd.shape
    sw = sw2 // 2
    width = ns * LANES
    nchunks = rows // t_rows

    def body(u_ref, bd_ref, pw_ref, tab_ref, ct_ref, y_ref, h_ref, s_ref, carry_ref, w_ref):
        @pl.when(pl.program_id(1) == 0)
        def _():
            carry_ref[...] = jnp.zeros_like(carry_ref)
            _build_tap_weights(w_ref, bd_ref, pw_ref, False, sw)

        _scan_chunk(u_ref[...], w_ref, tab_ref, s_ref, carry_ref, descending, t_rows, sw)
        hb = s_ref[...].astype(BF16)
        h_ref[...] = hb
        y_ref[...] = lax.dot_general(hb, ct_ref[...], NT, preferred_element_type=F32)

    (y, h), cres = _host_call(
        body, name=name, grid=(ns, nchunks), operands=[u_src, bd, pw, tab, ct],
        in_specs=[pl.BlockSpec((None, t_rows, LANES), lambda s, i: (u_shard, chunk_of(i), s)),
                  _slab_spec(LANES, sw2, dr), _slab_spec(2 * SCAN_TAPS, sw2, dr), _slab_spec(SCAN_TAPS, sw2, dr),
                  _slab_spec(LANES, sw2, dr)],
        out_specs=[pl.BlockSpec((t_rows, LANES), lambda s, i: (chunk_of(i), s)),
                   pl.BlockSpec((None, t_rows, sw2), lambda s, i: (s, chunk_of(i), 0))],
        out_shape=[jax.ShapeDtypeStruct((rows, width), F32), jax.ShapeDtypeStruct((ns, rows, sw2), BF16)],
        scratch_shapes=[pltpu.VMEM((t_rows, sw2), F32), pltpu.VMEM((SUBLANES, sw2), F32),
                        pltpu.VMEM((SCAN_TAPS * LANES, sw2), BF16)], comm=comm)
    return y, h, cres


def _ssm_bwd(name, dr, dy, u_src, u_shard, states, ct, pw, tab, bd, descending, chunk_of, t_rows, rows, comm=None):
    _, ns, _, sw2 = ct.shape
    sw = sw2 // 2
    width = ns * LANES
    nchunks = rows // t_rows

    def body(dy_ref, u_ref, h_ref, ct_ref, pw_ref, tab_ref, bd_ref, du_ref, dbd_ref, dcd_ref, dlam_ref,
             s_ref, carry_ref, gsave_ref, w_ref):
        first = pl.program_id(1) == 0

        @pl.when(first)
        def _():
            carry_ref[...] = jnp.zeros_like(carry_ref)
            gsave_ref[...] = jnp.zeros_like(gsave_ref)
            _build_tap_weights(w_ref, ct_ref, pw_ref, True, sw)

        dyv = dy_ref[...]
        _scan_chunk(dyv, w_ref, tab_ref, s_ref, carry_ref, descending, t_rows, sw)
        g = s_ref[...]
        gb = g.astype(BF16)
        du_ref[...] = lax.dot_general(gb, bd_ref[...], NT, preferred_element_type=F32)
        dbd = lax.dot_general(u_ref[...].astype(BF16), gb, TN, preferred_element_type=F32)
        hb = h_ref[...]
        dcd = lax.dot_general(hb, dyv.astype(BF16), TN, preferred_element_type=F32)
        hf = hb.astype(F32)
        rowid = lax.broadcasted_iota(jnp.int32, hf.shape, 0)
        if descending:
            hp = jnp.where(rowid == 0, 0.0, pltpu.roll(hf, 1, 0))
            h_edge, g_edge = hf[t_rows - 1:t_rows, :], g[0:1, :]
        else:
            hp = jnp.where(rowid == t_rows - 1, 0.0, pltpu.roll(hf, t_rows - 1, 0))
            h_edge, g_edge = hf[0:1, :], g[t_rows - 1:t_rows, :]
        g_re, g_im, hp_re, hp_im = g[:, :sw], g[:, sw:], hp[:, :sw], hp[:, sw:]
        gs = gsave_ref[0:1, :]
        gs_re, gs_im, he_re, he_im = gs[:, :sw], gs[:, sw:], h_edge[:, :sw], h_edge[:, sw:]
        dl_re = _colsum(g_re * hp_re + g_im * hp_im) + gs_re * he_re + gs_im * he_im
        dl_im = _colsum(g_im * hp_re - g_re * hp_im) + gs_im * he_re - gs_re * he_im
        gsave_ref[0:1, :] = g_edge

        @pl.when(first)
        def _():
            dbd_ref[...] = dbd
            dcd_ref[...] = dcd
            dlam_ref[:, :sw] = dl_re
            dlam_ref[:, sw:] = dl_im

        @pl.when(jnp.logical_not(first))
        def _():
            dbd_ref[...] += dbd
            dcd_ref[...] += dcd
            dlam_ref[:, :sw] += dl_re
            dlam_ref[:, sw:] += dl_im

    (du, dbd, dcd, dlam), cres = _host_call(
        body, name=name, grid=(ns, nchunks), operands=[dy, u_src, states, ct, pw, tab, bd],
        in_specs=[pl.BlockSpec((t_rows, LANES), lambda s, i: (chunk_of(i), s)),
                  pl.BlockSpec((None, t_rows, LANES), lambda s, i: (u_shard, chunk_of(i), s)),
                  pl.BlockSpec((None, t_rows, sw2), lambda s, i: (s, chunk_of(i), 0)),
                  _slab_spec(LANES, sw2, dr), _slab_spec(2 * SCAN_TAPS, sw2, dr), _slab_spec(SCAN_TAPS, sw2, dr),
                  _slab_spec(LANES, sw2, dr)],
        out_specs=[pl.BlockSpec((t_rows, LANES), lambda s, i: (chunk_of(i), s)),
                   pl.BlockSpec((None, LANES, sw2), lambda s, i: (s, 0, 0)),
                   pl.BlockSpec((None, sw2, LANES), lambda s, i: (s, 0, 0)),
                   pl.BlockSpec((None, 1, sw2), lambda s, i: (s, 0, 0))],
        out_shape=[jax.ShapeDtypeStruct((rows, width), F32), jax.ShapeDtypeStruct((ns, LANES, sw2), F32),
                   jax.ShapeDtypeStruct((ns, sw2, LANES), F32), jax.ShapeDtypeStruct((ns, 1, sw2), F32)],
        scratch_shapes=[pltpu.VMEM((t_rows, sw2), F32), pltpu.VMEM((SUBLANES, sw2), F32),
                        pltpu.VMEM((SUBLANES, sw2), F32), pltpu.VMEM((SCAN_TAPS * LANES, sw2), BF16)], comm=comm)
    return du, dbd, dcd, dlam, cres


def _mod_fwd(cs, w_mod, b_cols):
    d, width = w_mod.shape
    tn = _tile(width, 768, LANES)

    def epilogue(accs, ins, outs, pids):
        outs[0][...] = accs[0] + ins[2][...]

    return _matmul(
        "mod_fwd", (width // tn,), [cs, w_mod, b_cols],
        [pl.BlockSpec((16, d), lambda n: (0, 0)), pl.BlockSpec((d, tn), lambda n: (0, n)),
         pl.BlockSpec((1, tn), lambda n: (0, n))],
        [(0, 1, 0, NN)], [jax.ShapeDtypeStruct((16, width), F32)], [pl.BlockSpec((16, tn), lambda n: (0, n))],
        epilogue, prologue={0: lambda v: v * _sigmoid(v)})[0][0]


def _mod_bwd_adam(cs, dmod_cols, w, m, v, comm=None):
    d, width = w.shape
    tn = _tile(width, LANES, LANES)
    col = pl.BlockSpec((d, tn), lambda n: (0, n))

    def body(cs_ref, dm_ref, w_ref, m_ref, v_ref, g_ref, dl_ref, nm_ref, nv_ref, ds_ref):
        n = pl.program_id(0)
        lat = dm_ref[pl.ds(0, N_DEV, stride=SUBLANES), :]
        ctx = jnp.sum(dm_ref[pl.ds(1, N_DEV, stride=SUBLANES), :], axis=0, keepdims=True)
        row = lax.broadcasted_iota(jnp.int32, lat.shape, 0)
        dm = jnp.concatenate([lat, jnp.where(row == 0, ctx, 0.0)], axis=0).astype(BF16)
        c = cs_ref[...]
        sc = (c * _sigmoid(c)).astype(BF16)
        wv = w_ref[...]
        g = lax.dot_general(sc, dm, TN, preferred_element_type=F32)
        delta, m2, v2 = _adamw(wv, g, m_ref[...], v_ref[...])
        g_ref[...] = g
        dl_ref[...] = delta
        nm_ref[...] = m2
        nv_ref[...] = v2
        part = lax.dot_general(dm, wv.astype(BF16), NT, preferred_element_type=F32)

        @pl.when(n == 0)
        def _():
            ds_ref[...] = part

        @pl.when(n > 0)
        def _():
            ds_ref[...] += part

    shard = jax.ShapeDtypeStruct((d, width), F32)
    return _host_call(
        body, name="mod_bwd_adam", grid=(width // tn,), operands=[cs, dmod_cols, w, m, v],
        in_specs=[pl.BlockSpec((16, d), lambda n: (0, 0)), pl.BlockSpec((N_DEV * SUBLANES, tn), lambda n: (0, n)),
                  col, col, col],
        out_specs=[col, col, col, col, pl.BlockSpec((16, d), lambda n: (0, 0))],
        out_shape=[shard, shard, shard, shard, jax.ShapeDtypeStruct((16, d), F32)], comm=comm)


def _pair_sum(name, grads, got, core):
    _, rows, cols = grads.shape
    tr = _tile(rows, max(PACKED_SUBLANES, ADAM_BLOCK_BYTES // (cols * 6 * N_CHIPS)), PACKED_SUBLANES)
    blk = pl.BlockSpec((N_CHIPS, tr, cols), lambda i, cc: (0, i, 0))

    def body(core_ref, a_ref, b_ref, o_ref):
        o_ref[...] = (a_ref[...].astype(F32) + b_ref[...].astype(F32)).astype(BF16)

    grid_spec = pltpu.PrefetchScalarGridSpec(
        num_scalar_prefetch=1, grid=(rows // tr,),
        in_specs=[pl.BlockSpec((N_CHIPS, None, tr, cols), lambda i, cc: (0, cc[0], i, 0)), blk], out_specs=blk)
    return pl.pallas_call(
        body, name=name, grid_spec=grid_spec, out_shape=jax.ShapeDtypeStruct((N_CHIPS, rows, cols), BF16),
        compiler_params=_params(1))(core, grads.reshape(N_CHIPS, 2, rows, cols), got)


def _owner_adam(name, items, chip, comm=None):
    plan, start = [], 0
    per_element = 2 * (2 * N_CHIPS + 7 * 4)
    block_elements = ADAM_GROUP_VMEM // (per_element * len(items))
    for _, _, w, _, _ in items:
        rows, cols = w.shape
        tr = _tile(rows, max(PACKED_SUBLANES, block_elements // cols), PACKED_SUBLANES)
        plan.append((start, rows // tr, tr, cols))
        start += rows // tr
    operands, in_specs, out_specs, out_shape = [], [], [], []
    for (first, nt, tr, cols), (p, l, w, m, v) in zip(plan, items):
        def tile(s, first=first, nt=nt):
            return jnp.clip(s - first, 0, nt - 1)
        blk = pl.BlockSpec((tr, cols), lambda s, ch, tile=tile: (tile(s), 0))
        operands += [p, l, w, m, v]
        in_specs += [pl.BlockSpec((None, tr, cols), lambda s, ch, tile=tile: (ch[0], tile(s), 0)),
                     pl.BlockSpec((N_CHIPS - 1, tr, cols), lambda s, ch, tile=tile: (0, tile(s), 0)), blk, blk, blk]
        out_specs += [blk] * 4
        out_shape += [jax.ShapeDtypeStruct(w.shape, F32)] * 4
    n = len(items)

    def body(chip_ref, *refs):
        s = pl.program_id(0)
        for k, (first, nt, _, _) in enumerate(plan):
            p_ref, l_ref, w_ref, m_ref, v_ref = refs[5 * k:5 * k + 5]
            g_ref, dl_ref, nm_ref, nv_ref = refs[5 * n + 4 * k:5 * n + 4 * k + 4]

            @pl.when(jnp.logical_and(s >= first, s < first + nt))
            def _(p_ref=p_ref, l_ref=l_ref, w_ref=w_ref, m_ref=m_ref, v_ref=v_ref,
                  g_ref=g_ref, dl_ref=dl_ref, nm_ref=nm_ref, nv_ref=nv_ref):
                g = p_ref[...].astype(F32)
                for r in range(N_CHIPS - 1):
                    g = g + l_ref[r].astype(F32)
                delta, m2, v2 = _adamw(w_ref[...], g, m_ref[...], v_ref[...])
                g_ref[...] = g
                dl_ref[...] = delta
                nm_ref[...] = m2
                nv_ref[...] = v2

    res, cres = _host_call(body, name=name, grid=(start,), operands=operands, in_specs=in_specs,
                           out_shape=out_shape, out_specs=out_specs, comm=comm, prefetch=[chip])
    return [res[4 * k:4 * k + 4] for k in range(n)], cres


def _sum_adam(name, parts, w, m, v):
    rows, cols = w.shape
    n_parts = parts.shape[0]
    align = PACKED_SUBLANES if parts.dtype == BF16 else SUBLANES
    tr = _tile(rows, max(align, ADAM_BLOCK_BYTES // (cols * 44)), align)
    blk = pl.BlockSpec((tr, cols), lambda i: (i, 0))

    def body(p_ref, w_ref, m_ref, v_ref, g_ref, dl_ref, nm_ref, nv_ref):
        g = p_ref[0].astype(F32)
        for s in range(1, n_parts):
            g = g + p_ref[s].astype(F32)
        delta, m2, v2 = _adamw(w_ref[...], g, m_ref[...], v_ref[...])
        g_ref[...] = g
        dl_ref[...] = delta
        nm_ref[...] = m2
        nv_ref[...] = v2

    out = jax.ShapeDtypeStruct((rows, cols), F32)
    return pl.pallas_call(
        body, name=name, grid=(rows // tr,),
        in_specs=[pl.BlockSpec((n_parts, tr, cols), lambda i: (0, i, 0)), blk, blk, blk],
        out_specs=[blk, blk, blk, blk], out_shape=[out, out, out, out], compiler_params=_params(1),
    )(parts, w, m, v)


def _bias_adam(dmod_all, w, m, v):
    width = w.shape[-1]
    tn = _tile(width, 2048, LANES)
    blk = pl.BlockSpec((1, tn), lambda n: (0, n))

    def body(p_ref, w_ref, m_ref, v_ref, g_ref, dl_ref, nm_ref, nv_ref):
        g = jnp.sum(p_ref[...], axis=0, keepdims=True)
        delta, m2, v2 = _adamw(w_ref[...], g, m_ref[...], v_ref[...])
        g_ref[...] = g
        dl_ref[...] = delta
        nm_ref[...] = m2
        nv_ref[...] = v2

    out = jax.ShapeDtypeStruct((1, width), F32)
    return pl.pallas_call(
        body, name="bias_adam", grid=(width // tn,),
        in_specs=[pl.BlockSpec((dmod_all.shape[0], tn), lambda n: (0, n)), blk, blk, blk],
        out_specs=[blk, blk, blk, blk], out_shape=[out, out, out, out], compiler_params=_params(1),
    )(dmod_all, w, m, v)


def _pack(arrays, total_rows):
    flat = []
    for a in arrays:
        a = a.reshape(-1).astype(F32)
        flat.append(jnp.pad(a, (0, (-a.shape[0]) % LANES)))
    flat = jnp.concatenate(flat).reshape(-1, LANES)
    return jnp.pad(flat, ((0, total_rows - flat.shape[0]), (0, 0)))


def _unpack(packed, shapes):
    out, row = [], 0
    for shp in shapes:
        size = math.prod(shp)
        nrows = -(-size // LANES)
        out.append(packed[row:row + nrows].reshape(-1)[:size].reshape(shp))
        row += nrows
    return out


def kernel(x, c, ctx, c_ctx, w_mod, b_mod, norm_g, w_ffn1_gate, w_ffn1_up, w_ffn1_down, w_in, q_norm_g, k_norm_g, ssm_a_re, ssm_a_im, ssm_log_dt, ssm_b_re, ssm_b_im, ssm_c_re, ssm_c_im, ssm_d, w_glu, b_glu, w_br_attn, w_br_ssm, w_out, w_ffn2_gate, w_ffn2_up, w_ffn2_down, loss_target, m_c_ctx, m_w_mod, m_b_mod, m_norm_g, m_w_ffn1_gate, m_w_ffn1_up, m_w_ffn1_down, m_w_in, m_q_norm_g, m_k_norm_g, m_ssm_a_re, m_ssm_a_im, m_ssm_log_dt, m_ssm_b_re, m_ssm_b_im, m_ssm_c_re, m_ssm_c_im, m_ssm_d, m_w_glu, m_b_glu, m_w_br_attn, m_w_br_ssm, m_w_out, m_w_ffn2_gate, m_w_ffn2_up, m_w_ffn2_down, v_c_ctx, v_w_mod, v_b_mod, v_norm_g, v_w_ffn1_gate, v_w_ffn1_up, v_w_ffn1_down, v_w_in, v_q_norm_g, v_k_norm_g, v_ssm_a_re, v_ssm_a_im, v_ssm_log_dt, v_ssm_b_re, v_ssm_b_im, v_ssm_c_re, v_ssm_c_im, v_ssm_d, v_w_glu, v_b_glu, v_w_br_attn, v_w_br_ssm, v_w_out, v_w_ffn2_gate, v_w_ffn2_up, v_w_ffn2_down):
    _, L, D = x.shape
    Lc = ctx.shape[1]
    R = L + Lc
    MODW = w_mod.shape[-1]
    INS = w_in.shape[-1]
    KVW = INS // 2
    NQ = D // LANES
    NKV = KVW // LANES
    QPK = NQ // NKV
    HBQ = INS // LANES
    G, P, E = ssm_b_re.shape[2:]
    W = G * E
    SW = SLAB_GROUPS * P
    assert E * SLAB_GROUPS == LANES and W == INS and NQ * LANES == D and Lc <= L
    me = 4 * lax.axis_index("x") + 2 * lax.axis_index("y") + lax.axis_index("c")

    x2, ctx2, tgt = x[0], ctx[0], loss_target[0]
    xc0 = jnp.concatenate([x2, ctx2], axis=0)

    def bf(w):
        return w[0].astype(BF16)

    def held_t(w):
        return jnp.swapaxes(w[0], 0, 1)

    def bft(w):
        return held_t(w).astype(BF16)

    def widen(a):
        return jnp.pad(a[0], ((0, 0), (0, D - a.shape[-1])))

    def at_row(a, r, total):
        return jnp.pad(a, ((r, total - r - a.shape[0]), (0, 0)))

    pack_in = (at_row(c, 0, 16) + at_row(widen(norm_g), 1, 16) + at_row(widen(m_norm_g), 4, 16)
               + at_row(widen(v_norm_g), 7, 16))
    (g_in,) = _exchange_only("ag_inputs", _Gather([pack_in]))
    c_all = g_in[:, 0, :]
    dn = D // N_DEV

    def full_norm(k):
        return jnp.transpose(g_in[:, k:k + 3, :dn], (1, 0, 2)).reshape(3, D)

    ng_full, m_ng_full, v_ng_full = full_norm(1), full_norm(4), full_norm(7)
    cs = at_row(c_all, 0, 16) + at_row(c_ctx[None, :], 8, 16)

    b_cols = lax.dynamic_slice_in_dim(b_mod, me * MODW, MODW, axis=1)
    mod_blk = _mod_fwd(cs, w_mod[0], b_cols)
    (mod_g,) = _exchange_only("ag_mod", _Gather([mod_blk]))
    mod_lat = lax.dynamic_index_in_dim(mod_g, me, axis=1, keepdims=False).reshape(9, D)
    mod_ctx = mod_g[:, 8, :].reshape(9, D)[:5]
    tab2 = jnp.concatenate([mod_lat, mod_ctx, ng_full, jnp.zeros((7, D), F32)], axis=0)
    tab3 = tab2[:, None, :]
    SH1, SC1, G1, SH2, SC2, G2, SH3, SC3, G3, MC0, MC1, MC2, MC3, MC4, GAM1, GAM2, GAM3 = range(17)

    wg1, wu1 = _exchange_only("ag_ffn1_gate_up", _Gather([bft(w_ffn1_gate), bft(w_ffn1_up)]))
    h1 = _norm_mod_fwd("nm1_fwd", xc0, tab3, GAM1, (SH1, MC0), (SC1, MC1), L, Lc)
    a1, b1, s1, (wd1,) = _ffn_up("ffn1", h1, wg1, wu1, comm=_Gather([bf(w_ffn1_down)]))
    f1, xc1, (win,) = _ffn_down("ffn1", s1, wd1, xc0, tab2, (G1, MC2), L, comm=_Gather([bf(w_in)]))

    h2 = _norm_mod_fwd("nm2_fwd", xc1, tab3, GAM2, (SH2, MC3), (SC2, MC4), L, Lc)
    tm = _tile(R, MM_TILE, LANES)
    tml = _tile(L, MM_TILE, LANES)

    (p01,), _ = _matmul(
        "in_proj_kvu", (2, R // tm), [h2, win],
        [pl.BlockSpec((tm, D), lambda j, i: (i, 0)), pl.BlockSpec((None, D, INS), lambda j, i: (j, 0, 0))],
        [(0, 1, 0, NN)], [jax.ShapeDtypeStruct((2, R, INS), F32)],
        [pl.BlockSpec((None, tm, INS), lambda j, i: (j, i, 0))], _store_all)
    (p27,), (wglu, wbra) = _matmul(
        "in_proj_qg", (6, L // tml), [h2, win],
        [pl.BlockSpec((tml, D), lambda j, i: (i, 0)), pl.BlockSpec((None, D, INS), lambda j, i: (j + 2, 0, 0))],
        [(0, 1, 0, NN)], [jax.ShapeDtypeStruct((6, L, INS), F32)],
        [pl.BlockSpec((None, tml, INS), lambda j, i: (j, i, 0))], _store_all,
        comm=_Gather([bf(w_glu), bf(w_br_attn)]))
    wglu2 = wglu.reshape(W, W)
    wbra2 = wbra.reshape(D, D)

    cos_all, sin_all = _rope_tables(L, Lc)
    cos_l, sin_l = cos_all[:L], sin_all[:L]

    q_rot = _qk_prep("q_prep", p27, 0, HBQ, NQ, L, q_norm_g, cos_l, sin_l)
    k_rot = _qk_prep("k_prep", p01, 0, NKV, NKV, R, k_norm_g, cos_all, sin_all)
    v_hd = _heads_cast("v_heads", p01, 1, NKV, NKV, R)
    attn, (wbrs, wout) = _attn_fwd(q_rot, k_rot, v_hd, QPK, comm=_Gather([bf(w_br_ssm), bf(w_out)]))
    wout2 = wout.reshape(D, D)

    t_rows = _tile(math.gcd(L, Lc), ROW_TILE, SUBLANES)
    nl, ncx = L // t_rows, Lc // t_rows
    nch = nl + ncx
    ns = G // SLAB_GROUPS
    ssm_prim = (ssm_a_re[0], ssm_a_im[0], ssm_log_dt[0], ssm_b_re[0], ssm_b_im[0])
    _, _, bt_re, bt_im = _ssm_discretize(*ssm_prim)
    pw_re, pw_im = _lambda_powers(ssm_a_re[0], ssm_a_im[0], ssm_log_dt[0], ns)
    bd_re = _block_diag(jnp.swapaxes(bt_re, 2, 3))
    bd_im = _block_diag(jnp.swapaxes(bt_im, 2, 3))
    ct_re = _block_diag(ssm_c_re[0])
    ct_im = _block_diag(-ssm_c_im[0])
    fwd_desc = (False, True)
    adj_desc = (True, False)
    s_bd = jnp.concatenate([bd_re, bd_im], axis=-1)
    s_ct = jnp.concatenate([ct_re, ct_im], axis=-1)
    s_bd16, s_ct16 = s_bd.astype(BF16), s_ct.astype(BF16)
    s_pw = jnp.pad(jnp.transpose(jnp.concatenate([pw_re, pw_im], axis=-1), (1, 2, 0, 3)),
                   ((0, 0), (0, 0), (0, 2 * SCAN_TAPS - SCAN_TAPS - 1), (0, 0)))
    s_tab = _carry_tables(pw_re, pw_im, fwd_desc)
    s_tabc = _carry_tables(pw_re, -pw_im, adj_desc)
    order = [lambda i: (i + nl) % nch, lambda i: nch - 1 - i]
    order_adj = [lambda i: (nch - 1 - i + nl) % nch, lambda i: i]
    y0, st0, (wg2,) = _ssm_fwd("ssm_fwd0", 0, p01, 1, s_bd, s_pw, s_tab, s_ct16, fwd_desc[0], order[0], t_rows, R,
                               comm=_Gather([bft(w_ffn2_gate)]))
    y1, st1, (wu2,) = _ssm_fwd("ssm_fwd1", 1, p01, 1, s_bd, s_pw, s_tab, s_ct16, fwd_desc[1], order[1], t_rows, R,
                               comm=_Gather([bft(w_ffn2_up)]))
    states = [st0, st1]

    tr = _row_tile(L, 0)
    rowW = pl.BlockSpec((tr, W), lambda i: (i, 0))
    vecW = pl.BlockSpec((1, W), lambda i: (0, 0))
    u_lat = pl.BlockSpec((None, tr, W), lambda i: (1, i, 0))

    def ssm_post(i, u, ya, yb, dvec):
        sv = dvec * u + ya + yb
        return [sv, _gelu(sv)], []

    (ssm_out, yg), _, _ = _rowwise(
        "ssm_post", L // tr, [p01, y0, y1, ssm_d], [u_lat, rowW, rowW, vecW],
        [jax.ShapeDtypeStruct((L, W), F32), jax.ShapeDtypeStruct((L, W), BF16)], [rowW, rowW], [], ssm_post)

    tnw = _tile(W, MM_TILE, LANES)

    def glu_epilogue(accs, ins, outs, pids):
        z = accs[0] + ins[3][...]
        outs[0][...] = z
        outs[1][...] = (_gelu(ins[2][...]) * _sigmoid(z)).astype(BF16)

    (z_glu, y2), _ = _matmul(
        "glu", (L // tml, W // tnw), [yg, wglu2, ssm_out, b_glu],
        [pl.BlockSpec((tml, W), lambda i, n: (i, 0)), pl.BlockSpec((W, tnw), lambda i, n: (0, n)),
         pl.BlockSpec((tml, tnw), lambda i, n: (i, n)), pl.BlockSpec((1, tnw), lambda i, n: (0, n))],
        [(0, 1, 0, NN)], [jax.ShapeDtypeStruct((L, W), F32), jax.ShapeDtypeStruct((L, W), BF16)],
        [pl.BlockSpec((tml, tnw), lambda i, n: (i, n))] * 2, glu_epilogue)

    tnd = _tile(D, MM_TILE, LANES)
    out_ld = pl.BlockSpec((tml, tnd), lambda i, n: (i, n))
    (br_a,), _ = _matmul(
        "br_attn", (L // tml, D // tnd), [attn, wbra2],
        [pl.BlockSpec((tml, D), lambda i, n: (i, 0)), pl.BlockSpec((D, tnd), lambda i, n: (0, n))],
        [(0, 1, 0, NN)], [jax.ShapeDtypeStruct((L, D), F32)], [out_ld], _store_all)

    cb = wbrs.shape[-1]
    gpb = INS // cb

    def gate_spec(first_shard):
        return pl.BlockSpec((None, tml, cb), lambda i, j: (first_shard + j // gpb, i, j % gpb))

    def merge_epilogue(accs, ins, outs, pids):
        br = accs[0]
        outs[0][...] = br
        outs[1][...] = (_sigmoid(ins[2][...]) * ins[4][...] + _sigmoid(ins[3][...]) * br).astype(BF16)

    col_blk = pl.BlockSpec((tml, cb), lambda i, j: (i, j))
    (br_s, merged), _ = _matmul(
        "br_ssm_merge", (L // tml, N_DEV), [y2, wbrs, p27, p27, br_a],
        [pl.BlockSpec((tml, W), lambda i, j: (i, 0)), pl.BlockSpec((None, W, cb), lambda i, j: (j, 0, 0)),
         gate_spec(2), gate_spec(4), col_blk],
        [(0, 1, 0, NN)], [jax.ShapeDtypeStruct((L, D), F32), jax.ShapeDtypeStruct((L, D), BF16)],
        [col_blk, col_blk], merge_epilogue)

    def out_epilogue(accs, ins, outs, pids):
        outs[0][...] = accs[0]
        outs[1][...] = ins[2][...] + ins[3][G2:G2 + 1, :] * accs[0]

    (mix, x2_), _ = _matmul(
        "out_proj", (L // tml, D // tnd), [merged, wout2, xc1, tab2],
        [pl.BlockSpec((tml, D), lambda i, n: (i, 0)), pl.BlockSpec((D, tnd), lambda i, n: (0, n)), out_ld,
         pl.BlockSpec((tab2.shape[0], tnd), lambda i, n: (0, n))],
        [(0, 1, 0, NN)], [jax.ShapeDtypeStruct((L, D), F32)] * 2, [out_ld, out_ld], out_epilogue)

    h3 = _norm_mod_fwd("nm3_fwd", x2_, tab3, GAM3, (SH3, SH3), (SC3, SC3), L, 0)
    a3, b3, s3, (wd2,) = _ffn_up("ffn2", h3, wg2, wu2, comm=_Gather([bf(w_ffn2_down)]))
    f3, x3, _ = _ffn_down("ffn2", s3, wd2, x2_, tab2, (G3, G3), L)

    trd = _row_tile(L, 0)
    rowD = pl.BlockSpec((trd, D), lambda i: (i, 0))

    def loss_fn(i, yv, t):
        err = yv - t
        return [err * (1.0 / D)], [_colsum(err * err)]

    (dx3,), (sq,), _ = _rowwise("loss", L // trd, [x3, tgt], [rowD, rowD],
                                [jax.ShapeDtypeStruct((L, D), F32)], [rowD], [D], loss_fn)
    loss = lax.psum(0.5 * jnp.sum(sq) / D, ("x", "y", "c"))

    core = lax.axis_index("c").astype(jnp.int32).reshape(1)
    chip = (2 * lax.axis_index("x") + lax.axis_index("y")).astype(jnp.int32).reshape(1)

    def pair_sums(tag, grads, halves):
        return [_pair_sum("pair_%s%d" % (tag, k), g_, h_, core) for k, (g_, h_) in enumerate(zip(grads, halves))]

    df3, (dg3, _) = _gate_bwd("gate3_bwd", dx3, f3, tab3, (G3, G3), 0.5, L, 0)
    dwd2, _ = _ffn_dwd("ffn2b", s3, df3)
    da3, db3, half_wd2 = _ffn_ds("ffn2b", df3, wd2, a3, b3, comm=_SiblingSwap([dwd2]))
    (p_wd2,) = pair_sums("wd2", [dwd2], half_wd2)
    dwg2, dwu2, (l_wd2,) = _ffn_dwgu("ffn2b", h3, da3, db3, comm=_ChipExchange([p_wd2]))
    dh3, half_wgu2 = _ffn_dh("ffn2b", da3, db3, wg2, wu2, comm=_SiblingSwap([dwg2, dwu2]))
    p_wg2, p_wu2 = pair_sums("wgu2", [dwg2, dwu2], half_wgu2)
    dx2, (dsh3, dsc3, _, _, dgam3) = _norm_mod_bwd("nm3_bwd", x2_, dh3, tab3, GAM3, (SC3, SC3), L, 0, dres=dx3)

    dmix, (dg2, _) = _gate_bwd("gate2_bwd", dx2, mix, tab3, (G2, G2), 1.0, L, 0)

    def dmerged_epilogue(accs, ins, outs, pids):
        dm = accs[0]
        ga, gs = _sigmoid(ins[2][...]), _sigmoid(ins[3][...])
        outs[0][...] = (ga * dm).astype(BF16)
        outs[1][...] = (gs * dm).astype(BF16)
        outs[2][...] = (dm * ins[4][...] * ga * (1.0 - ga)).astype(BF16)
        outs[3][...] = (dm * ins[5][...] * gs * (1.0 - gs)).astype(BF16)

    dgate_spec = pl.BlockSpec((None, tml, cb), lambda i, j: (j // gpb, i, j % gpb))
    (d_br_a, d_br_s, dg_a, dg_s), _ = _matmul(
        "dmerged", (L // tml, N_DEV), [dmix, wout2, p27, p27, br_a, br_s],
        [pl.BlockSpec((tml, D), lambda i, j: (i, 0)), pl.BlockSpec((cb, D), lambda i, j: (j, 0)),
         gate_spec(2), gate_spec(4), col_blk, col_blk],
        [(0, 1, 0, NT)],
        [jax.ShapeDtypeStruct((L, D), BF16)] * 2 + [jax.ShapeDtypeStruct((2, L, INS), BF16)] * 2,
        [col_blk, col_blk, dgate_spec, dgate_spec], dmerged_epilogue)

    def wgrad(name, a_mat, b_mat, tmo, tno):
        ka, ma = a_mat.shape
        _, nb_ = b_mat.shape
        return _matmul(
            name, (ma // tmo, nb_ // tno), [a_mat, b_mat],
            [pl.BlockSpec((ka, tmo), lambda m, n: (0, m)), pl.BlockSpec((ka, tno), lambda m, n: (0, n))],
            [(0, 1, 0, TN)], [jax.ShapeDtypeStruct((ma, nb_), BF16)],
            [pl.BlockSpec((tmo, tno), lambda m, n: (m, n))], _store_all)[0][0]

    dwout = wgrad("dw_out", merged, dmix, tnd, tnd)
    dwbra = wgrad("dw_br_attn", attn, d_br_a, tnd, tnd)
    (d_attn,), _ = _matmul(
        "d_attn", (L // tml, D // tnd), [d_br_a, wbra2],
        [pl.BlockSpec((tml, D), lambda i, n: (i, 0)), pl.BlockSpec((tnd, D), lambda i, n: (n, 0))],
        [(0, 1, 0, NT)], [jax.ShapeDtypeStruct((L, D), BF16)], [out_ld], _store_all)

    (dwbrs,), _ = _matmul(
        "dw_br_ssm", (N_DEV,), [y2, d_br_s],
        [pl.BlockSpec((L, W), lambda j: (0, 0)), pl.BlockSpec((L, cb), lambda j: (0, j))],
        [(0, 1, 0, TN)], [jax.ShapeDtypeStruct((N_DEV, W, cb), BF16)],
        [pl.BlockSpec((None, W, cb), lambda j: (j, 0, 0))], _store_all)

    def dy2_epilogue(accs, ins, outs, pids):
        dy2 = accs[0]
        sg = _sigmoid(ins[2][...])
        outs[0][...] = dy2 * sg
        outs[1][...] = (dy2 * _gelu(ins[3][...]) * sg * (1.0 - sg)).astype(BF16)

    wn_blk = pl.BlockSpec((tml, tnw), lambda i, n, k: (i, n))
    (dyg1, dz), _ = _matmul(
        "d_y2", (L // tml, W // tnw, N_DEV), [d_br_s, wbrs, z_glu, ssm_out],
        [pl.BlockSpec((tml, cb), lambda i, n, k: (i, k)), pl.BlockSpec((None, tnw, cb), lambda i, n, k: (k, n, 0)),
         wn_blk, wn_blk],
        [(0, 1, 0, NT)], [jax.ShapeDtypeStruct((L, W), F32), jax.ShapeDtypeStruct((L, W), BF16)],
        [wn_blk, wn_blk], dy2_epilogue, acc_shapes=[(tml, tnw)], nk=N_DEV)

    dwglu = wgrad("dw_glu", yg, dz, tnw, tnw)
    mix_grads = [dwout.reshape(N_DEV, D // N_DEV, D), dwbra.reshape(N_DEV, D // N_DEV, D), dwbrs,
                 dwglu.reshape(N_DEV, W // N_DEV, W)]

    def dssm_epilogue(accs, ins, outs, pids):
        outs[0][...] = (accs[0] + ins[2][...]) * _gelu_grad(ins[3][...])

    wn2 = pl.BlockSpec((tml, tnw), lambda i, n: (i, n))
    (dssm,), _ = _matmul(
        "d_ssm", (L // tml, W // tnw), [dz, wglu2, dyg1, ssm_out],
        [pl.BlockSpec((tml, W), lambda i, n: (i, 0)), pl.BlockSpec((tnw, W), lambda i, n: (n, 0)), wn2, wn2],
        [(0, 1, 0, NT)], [jax.ShapeDtypeStruct((L, W), F32)], [wn2], dssm_epilogue)

    dssm_all = jnp.concatenate([dssm, jnp.zeros((Lc, W), F32)], axis=0)
    du0, dbd0, dcd0, dlam0, (l_wg2, *half_mix) = _ssm_bwd(
        "ssm_bwd0", 0, dssm_all, p01, 1, states[0], s_ct, s_pw, s_tabc, s_bd16, adj_desc[0], order_adj[0], t_rows, R,
        comm=_Both([_ChipExchange([p_wg2]), _SiblingSwap(mix_grads)]))
    p_wout, p_wbra, p_wbrs, p_wglu = pair_sums("mix", mix_grads, half_mix)
    du1, dbd1, dcd1, dlam1, (l_wu2,) = _ssm_bwd(
        "ssm_bwd1", 1, dssm_all, p01, 1, states[1], s_ct, s_pw, s_tabc, s_bd16, adj_desc[1], order_adj[1], t_rows, R,
        comm=_ChipExchange([p_wu2]))

    trr = _row_tile(L, Lc)
    nlt = L // trr
    rowR = pl.BlockSpec((trr, W), lambda i: (i, 0))

    def du_fn(i, dua, dub, dsv, dvec, u):
        lat = (i < nlt).astype(F32)
        return [dua + dub + lat * (dvec * dsv)], [lat * _colsum(dsv * u)]

    (du_all,), (d_ssm_d,), _ = _rowwise(
        "du_combine", R // trr, [du0, du1, dssm_all, ssm_d, p01],
        [rowR, rowR, rowR, pl.BlockSpec((1, W), lambda i: (0, 0)), pl.BlockSpec((None, trr, W), lambda i: (1, i, 0))],
        [jax.ShapeDtypeStruct((R, W), BF16)], [rowR], [W], du_fn)

    def dz_sum(i, dzv):
        return [], [_colsum(dzv.astype(F32))]

    _, (d_b_glu,), _ = _rowwise("db_glu", L // tr, [dz], [rowW], [], [], [W], dz_sum)

    dbd, dcd, dlam = jnp.stack([dbd0, dbd1]), jnp.stack([dcd0, dcd1]), jnp.stack([dlam0, dlam1])
    dbt_re = jnp.swapaxes(_block_diag_extract(dbd[..., :SW], E, P), 2, 3)
    dbt_im = jnp.swapaxes(_block_diag_extract(dbd[..., SW:], E, P), 2, 3)
    dl_re, dl_im = dlam[:, :, 0, :SW].reshape(2, G, P), dlam[:, :, 0, SW:].reshape(2, G, P)
    _, vjp = jax.vjp(_ssm_discretize, *ssm_prim)
    d_a_re, d_a_im, d_ldt, d_b_re, d_b_im = vjp((dl_re, dl_im, dbt_re, dbt_im))
    d_c_re = jnp.swapaxes(_block_diag_extract(dcd[:, :, :SW, :], P, E), 2, 3)
    d_c_im = -jnp.swapaxes(_block_diag_extract(dcd[:, :, SW:, :], P, E), 2, 3)

    early_g = [d_a_re, d_a_im, d_ldt, d_b_re, d_b_im, d_c_re, d_c_im, d_ssm_d, d_b_glu]
    early_w = [ssm_a_re, ssm_a_im, ssm_log_dt, ssm_b_re, ssm_b_im, ssm_c_re, ssm_c_im, ssm_d, b_glu]
    early_m = [m_ssm_a_re, m_ssm_a_im, m_ssm_log_dt, m_ssm_b_re, m_ssm_b_im, m_ssm_c_re, m_ssm_c_im, m_ssm_d, m_b_glu]
    early_v = [v_ssm_a_re, v_ssm_a_im, v_ssm_log_dt, v_ssm_b_re, v_ssm_b_im, v_ssm_c_re, v_ssm_c_im, v_ssm_d, v_b_glu]
    early_shapes = [a.shape for a in early_w]
    early_rows = -(-sum(-(-math.prod(s) // LANES) for s in early_shapes) // 256) * 256

    dq_rot, dk_rot, dv_hd, (l_wout, l_wbra, l_wbrs, l_wglu, early_parts) = _attn_bwd(
        q_rot, k_rot, v_hd, d_attn, QPK,
        comm=_Both([_ChipExchange([p_wout, p_wbra, p_wbrs, p_wglu]), _Gather([_pack(early_g, early_rows)])]))
    dq_pre, d_qg = _qk_prep_bwd("q_prep_bwd", dq_rot, p27, 0, HBQ, NQ, L, q_norm_g, cos_l, sin_l)
    dk_pre, d_kg = _qk_prep_bwd("k_prep_bwd", dk_rot, p01, 0, NKV, NKV, R, k_norm_g, cos_all, sin_all)
    dv_pre = _heads_merge("dv_merge", dv_hd)

    def lat_blocks(a):
        return jnp.pad(a, ((0, 0), (0, Lc), (0, 0)))

    dp = jnp.concatenate([
        jnp.concatenate([dk_pre[0], dv_pre], axis=1)[None], du_all[None],
        lat_blocks(dq_pre), lat_blocks(dg_a), lat_blocks(dg_s)], axis=0)

    tmo = _tile(D, MM_TILE, LANES)
    (dwin,), _ = _matmul(
        "dw_in", (N_DEV, D // tmo), [h2, dp],
        [pl.BlockSpec((R, tmo), lambda j, m: (0, m)), pl.BlockSpec((None, R, INS), lambda j, m: (j, 0, 0))],
        [(0, 1, 0, TN)], [jax.ShapeDtypeStruct((N_DEV, D, INS), BF16)],
        [pl.BlockSpec((None, tmo, INS), lambda j, m: (j, m, 0))], _store_all)
    tnh = _tile(D, MM_TILE_NT, LANES)
    (dh2,), half_win = _matmul(
        "d_h2", (R // tm, D // tnh), [dp, win],
        [pl.BlockSpec((N_DEV, tm, INS), lambda i, n: (0, i, 0)),
         pl.BlockSpec((N_DEV, tnh, INS), lambda i, n: (0, n, 0))],
        [(0, 1, 0, NT, N_DEV)], [jax.ShapeDtypeStruct((R, D), F32)], [pl.BlockSpec((tm, tnh), lambda i, n: (i, n))],
        _store_all, comm=_SiblingSwap([dwin]))
    (p_win,) = pair_sums("win", [dwin], half_win)
    dxc1, (dsh2, dsc2, dmc3, dmc4, dgam2) = _norm_mod_bwd(
        "nm2_bwd", xc1, dh2, tab3, GAM2, (SC2, MC4), L, Lc, dres=dx2)

    df1, (dg1, dmc2) = _gate_bwd("gate1_bwd", dxc1, f1, tab3, (G1, MC2), 0.5, L, Lc)
    dwd1, _ = _ffn_dwd("ffn1b", s1, df1)
    da1, db1, half_wd1 = _ffn_ds("ffn1b", df1, wd1, a1, b1, comm=_SiblingSwap([dwd1]))
    (p_wd1,) = pair_sums("wd1", [dwd1], half_wd1)
    dwg1, dwu1, (l_wd1,) = _ffn_dwgu("ffn1b", h1, da1, db1, comm=_ChipExchange([p_wd1]))
    dh1, (l_win, *half_wgu1) = _ffn_dh(
        "ffn1b", da1, db1, wg1, wu1, comm=_Both([_ChipExchange([p_win]), _SiblingSwap([dwg1, dwu1])]))
    p_wg1, p_wu1 = pair_sums("wgu1", [dwg1, dwu1], half_wgu1)

    def adam_item(p, l_, w_, m_, v_):
        return (p, l_, w_[0], m_[0], v_[0])

    def adam_item_t(p, l_, w_, m_, v_):
        return (p, l_, held_t(w_), held_t(m_), held_t(v_))

    ready_a = [adam_item(p_wd1, l_wd1, w_ffn1_down, m_w_ffn1_down, v_w_ffn1_down),
               adam_item(p_win, l_win, w_in, m_w_in, v_w_in),
               adam_item(p_wglu, l_wglu, w_glu, m_w_glu, v_w_glu),
               adam_item(p_wbra, l_wbra, w_br_attn, m_w_br_attn, v_w_br_attn),
               adam_item(p_wbrs, l_wbrs, w_br_ssm, m_w_br_ssm, v_w_br_ssm)]
    ready_b = [adam_item(p_wout, l_wout, w_out, m_w_out, v_w_out),
               adam_item_t(p_wg2, l_wg2, w_ffn2_gate, m_w_ffn2_gate, v_w_ffn2_gate),
               adam_item_t(p_wu2, l_wu2, w_ffn2_up, m_w_ffn2_up, v_w_ffn2_up),
               adam_item(p_wd2, l_wd2, w_ffn2_down, m_w_ffn2_down, v_w_ffn2_down)]
    dxc0, (dsh1, dsc1, dmc0, dmc1, dgam1) = _norm_mod_bwd(
        "nm1_bwd", xc0, dh1, tab3, GAM1, (SC1, MC1), L, Lc, dres=dxc1)
    grad_x = dxc0[:L][None]

    dmod_lat = jnp.concatenate([dsh1, dsc1, dg1, dsh2, dsc2, dg2, dsh3, dsc3, dg3], axis=1)
    dmod_ctx = jnp.concatenate([dmc0, dmc1, dmc2, dmc3, dmc4, jnp.zeros((1, 4 * D), F32)], axis=1)
    dmod_pack = at_row(dmod_lat, 0, SUBLANES) + at_row(dmod_ctx, 1, SUBLANES)
    adam_a, (l_wg1, dmod_g) = _owner_adam(
        "adam_ready_a", ready_a, chip, comm=_Both([_ChipExchange([p_wg1]), _Gather([dmod_pack])]))
    dmod_all = dmod_g.reshape(N_DEV * SUBLANES, 9 * D)
    dmod_cols = lax.dynamic_slice_in_dim(dmod_all, me * MODW, MODW, axis=1)
    (g_wmod, dl_wmod, nm_wmod, nv_wmod, dsilu), _ = _mod_bwd_adam(
        cs, dmod_cols, w_mod[0], m_w_mod[0], v_w_mod[0])
    sg_cc = jax.nn.sigmoid(c_ctx)
    d_c_ctx = dsilu[8] * (sg_cc * (1.0 + c_ctx * (1.0 - sg_cc)))
    g_bmod, dl_bmod, nm_bmod, nv_bmod = _bias_adam(dmod_all, b_mod, m_b_mod, v_b_mod)

    dgam_all = jnp.concatenate([dgam1, dgam2, dgam3], axis=0)
    late_g = [d_c_ctx, d_qg, d_kg, dgam_all]
    late_w = [c_ctx, q_norm_g, k_norm_g, ng_full]
    late_m = [m_c_ctx, m_q_norm_g, m_k_norm_g, m_ng_full]
    late_v = [v_c_ctx, v_q_norm_g, v_k_norm_g, v_ng_full]
    late_shapes = [a.shape for a in late_w]
    late_rows = -(-sum(-(-math.prod(s) // LANES) for s in late_shapes) // SUBLANES) * SUBLANES
    adam_b, (l_wu1, late_parts) = _owner_adam(
        "adam_ready_b", ready_b, chip,
        comm=_Both([_ChipExchange([p_wu1]), _Gather([_pack(late_g, late_rows)])]))
    adam_ready = adam_a + adam_b
    late_out =_sum_adam("small_adam_late", late_parts, _pack(late_w, late_rows), _pack(late_m, late_rows),
                         _pack(late_v, late_rows))
    early_out = _sum_adam("small_adam_s5", early_parts, _pack(early_w, early_rows), _pack(early_m, early_rows),
                          _pack(early_v, early_rows))

    def my_norm_cols(a):
        return lax.dynamic_slice_in_dim(a, me * dn, dn, axis=1)[None]

    small = []
    for lo, eo in zip(late_out, early_out):
        c_ctx_, qg_, kg_, ng_ = _unpack(lo, late_shapes)
        small.append([c_ctx_, qg_, kg_] + _unpack(eo, early_shapes) + [my_norm_cols(ng_)])
    sm_g, sm_dl, sm_m, sm_v = small

    adam_last, _ = _owner_adam(
        "adam_last", [adam_item_t(p_wg1, l_wg1, w_ffn1_gate, m_w_ffn1_gate, v_w_ffn1_gate),
                      adam_item_t(p_wu1, l_wu1, w_ffn1_up, m_w_ffn1_up, v_w_ffn1_up)], chip)
    transposed = (0, 1, 8, 9)
    big_out = [[(jnp.swapaxes(o, 0, 1) if k in transposed else o)[None] for o in grp_]
               for k, grp_ in enumerate(adam_last + adam_ready)]

    def leaf(kind):
        sm = (sm_g, sm_dl, sm_m, sm_v)[kind]
        mod = (g_wmod, dl_wmod, nm_wmod, nv_wmod)[kind][None]
        bmod = (g_bmod, dl_bmod, nm_bmod, nv_bmod)[kind]
        big = [b[kind] for b in big_out]
        (c_ctx_, qg_, kg_, a_re_, a_im_, ldt_, b_re_, b_im_, c_re_, c_im_, sd_, bglu_, ng_) = sm
        return [c_ctx_, mod, bmod, ng_, big[0], big[1], big[2], big[3], qg_, kg_, a_re_, a_im_, ldt_, b_re_, b_im_,
                c_re_, c_im_, sd_, big[4], bglu_, big[5], big[6], big[7], big[8], big[9], big[10]]

    return tuple([loss, grad_x] + leaf(0) + leaf(1) + leaf(2) + leaf(3))
```

```python
import math

import jax
import jax.numpy as jnp
import numpy as np
from jax import lax
from jax.experimental import pallas as pl
from jax.experimental.pallas import tpu as pltpu

F32 = jnp.float32
BF16 = jnp.bfloat16

N_DEV = 8
N_CHIPS = 4
LANES = 128
SUBLANES = 8
PACKED_SUBLANES = 16
VMEM_LIMIT = 56 * 1024 * 1024
MM_TILE = 512
MM_TILE_NT = 256
ROW_TILE = 256
ELEMENTWISE_ROW_TILE = 512
HEAD_ROW_TILE = 512
ATTN_BWD_HEADS = 4
ATTN_FWD_HEADS = 2
ADAM_BLOCK_BYTES = 4 * 1024 * 1024
ADAM_GROUP_VMEM = 36 * 1024 * 1024

NORM_EPS = 1e-6
GRID_W = 64
ROPE_THETA = 10000.0
SCAN_TAPS = SUBLANES
SLAB_GROUPS = 8

ADAM_LR = 0.001
ADAM_B1 = 0.9
ADAM_B2 = 0.999
ADAM_EPS = 1e-08
ADAM_WD = 0.01
ADAM_STEP = 10

NN = (((1,), (0,)), ((), ()))
NT = (((1,), (1,)), ((), ()))
TN = (((0,), (0,)), ((), ()))

MESH = pl.DeviceIdType.MESH
ANY = pl.BlockSpec(memory_space=pl.ANY)


def _tile(n, cap, align):
    best = None
    for t in range(align, min(n, cap) + 1, align):
        if n % t == 0:
            best = t
    return n if best is None else best


def _params(n_grid):
    return pltpu.CompilerParams(dimension_semantics=("arbitrary",) * n_grid, vmem_limit_bytes=VMEM_LIMIT)


def _sigmoid(x):
    return 1.0 / (1.0 + jnp.exp(-x))


LOG2E = math.log2(math.e)
GELU_K = math.sqrt(2.0 / math.pi)
GELU_C = 0.044715


def _gelu(x):
    return 0.5 * x * (1.0 + jnp.tanh(GELU_K * (x + GELU_C * x * x * x)))


def _gelu_grad(x):
    t = jnp.tanh(GELU_K * (x + GELU_C * x * x * x))
    return 0.5 * (1.0 + t) + 0.5 * x * (1.0 - t * t) * GELU_K * (1.0 + 3.0 * GELU_C * x * x)


def _adamw(w, g, m, v):
    m2 = ADAM_B1 * m + (1.0 - ADAM_B1) * g
    v2 = ADAM_B2 * v + (1.0 - ADAM_B2) * (g * g)
    m_hat = m2 / (1.0 - ADAM_B1 ** ADAM_STEP)
    v_hat = v2 / (1.0 - ADAM_B2 ** ADAM_STEP)
    delta = -ADAM_LR * (m_hat / (jnp.sqrt(v_hat) + ADAM_EPS) + ADAM_WD * w)
    return delta, m2, v2


def _position():
    return lax.axis_index("x"), lax.axis_index("y"), lax.axis_index("c")


class _Gather:
    def __init__(self, arrays):
        self.arrays = list(arrays)
        n = len(self.arrays)
        self.out_shapes = [jax.ShapeDtypeStruct((N_DEV,) + a.shape, a.dtype) for a in self.arrays]
        self.scratch = [pltpu.SemaphoreType.DMA((n, 7)), pltpu.SemaphoreType.DMA((n, 7)),
                        pltpu.SemaphoreType.DMA((n,))]

    def _plan(self, ins, outs, sems):
        send, recv, local = sems
        x, y, c = _position()
        me, sibling = (x, y, c), (x, y, 1 - c)
        chips = [(1 - x, y), (x, 1 - y), (1 - x, 1 - y)]

        def slot(a, p):
            return outs[a].at[4 * p[0] + 2 * p[1] + p[2]]

        def copy(a, k, block, to, src=None):
            dst = slot(a, block)
            return pltpu.make_async_remote_copy(
                src_ref=dst if src is None else src, dst_ref=dst,
                send_sem=send.at[a, k], recv_sem=recv.at[a, k], device_id=to, device_id_type=MESH)

        mine = [pltpu.make_async_copy(ins[a], slot(a, me), local.at[a]) for a in range(len(ins))]
        return me, sibling, chips, c, copy, mine

    def start(self, ins, outs, sems):
        me, sibling, chips, c, copy, mine = self._plan(ins, outs, sems)
        for cp in mine:
            cp.start()
        for a in range(len(ins)):
            copy(a, 0, me, sibling, src=ins[a]).start()
            for j, chip in enumerate(chips):
                copy(a, 1 + j, me, (*chip, c), src=ins[a]).start()

    def finish(self, ins, outs, sems):
        me, sibling, chips, c, copy, mine = self._plan(ins, outs, sems)
        n = len(ins)
        for j, chip in enumerate(chips):
            for a in range(n):
                copy(a, 1 + j, (*chip, c), me).wait_recv()
                copy(a, 4 + j, (*chip, c), sibling).start()
        for a in range(n):
            copy(a, 0, sibling, me).wait_recv()
        for j, chip in enumerate(chips):
            for a in range(n):
                copy(a, 4 + j, (*chip, 1 - c), me).wait_recv()
        for a in range(n):
            copy(a, 0, me, sibling, src=ins[a]).wait_send()
            for j, chip in enumerate(chips):
                copy(a, 1 + j, me, (*chip, c), src=ins[a]).wait_send()
                copy(a, 4 + j, (*chip, c), sibling).wait_send()
        for cp in mine:
            cp.wait()


class _SiblingSwap:
    def __init__(self, arrays):
        self.arrays = list(arrays)
        n = len(self.arrays)
        self.out_shapes = [jax.ShapeDtypeStruct((N_CHIPS,) + a.shape[1:], a.dtype) for a in self.arrays]
        self.scratch = [pltpu.SemaphoreType.DMA((n, N_CHIPS)), pltpu.SemaphoreType.DMA((n, N_CHIPS))]

    def _plan(self, ins, outs, sems):
        send, recv = sems
        x, y, c = _position()
        return [pltpu.make_async_remote_copy(
            src_ref=ins[a].at[2 * j + 1 - c], dst_ref=outs[a].at[j],
            send_sem=send.at[a, j], recv_sem=recv.at[a, j], device_id=(x, y, 1 - c), device_id_type=MESH)
            for a in range(len(ins)) for j in range(N_CHIPS)]

    def start(self, ins, outs, sems):
        for cp in self._plan(ins, outs, sems):
            cp.start()

    def finish(self, ins, outs, sems):
        copies = self._plan(ins, outs, sems)
        for cp in copies:
            cp.wait_recv()
        for cp in copies:
            cp.wait_send()


class _ChipExchange:
    def __init__(self, arrays):
        self.arrays = list(arrays)
        n = len(self.arrays)
        self.out_shapes = [jax.ShapeDtypeStruct((N_CHIPS - 1,) + a.shape[1:], a.dtype) for a in self.arrays]
        self.scratch = [pltpu.SemaphoreType.DMA((n, N_CHIPS - 1)), pltpu.SemaphoreType.DMA((n, N_CHIPS - 1))]

    def _plan(self, ins, outs, sems):
        send, recv = sems
        x, y, c = _position()
        copies = []
        for r in range(1, N_CHIPS):
            px, py = x ^ (r >> 1), y ^ (r & 1)
            for a in range(len(ins)):
                copies.append(pltpu.make_async_remote_copy(
                    src_ref=ins[a].at[2 * px + py], dst_ref=outs[a].at[r - 1],
                    send_sem=send.at[a, r - 1], recv_sem=recv.at[a, r - 1],
                    device_id=(px, py, c), device_id_type=MESH))
        return copies

    def start(self, ins, outs, sems):
        for cp in self._plan(ins, outs, sems):
            cp.start()

    def finish(self, ins, outs, sems):
        copies = self._plan(ins, outs, sems)
        for cp in copies:
            cp.wait_recv()
        for cp in copies:
            cp.wait_send()


class _Both:
    def __init__(self, comms):
        self.comms = list(comms)
        self.arrays = [a for cm in self.comms for a in cm.arrays]
        self.out_shapes = [s for cm in self.comms for s in cm.out_shapes]
        self.scratch = [s for cm in self.comms for s in cm.scratch]

    def _split(self, ins, outs, sems):
        i = o = s = 0
        for cm in self.comms:
            ni, no, nsem = len(cm.arrays), len(cm.out_shapes), len(cm.scratch)
            yield cm, ins[i:i + ni], outs[o:o + no], sems[s:s + nsem]
            i, o, s = i + ni, o + no, s + nsem

    def start(self, ins, outs, sems):
        for cm, i, o, s in self._split(ins, outs, sems):
            cm.start(i, o, s)

    def finish(self, ins, outs, sems):
        for cm, i, o, s in self._split(ins, outs, sems):
            cm.finish(i, o, s)


def _host_call(body, *, name, grid, operands, in_specs, out_shape, out_specs, scratch_shapes=(), comm=None,
               prefetch=()):
    grid = tuple(grid)
    n_pre, n_in, n_out, n_scr = len(prefetch), len(operands), len(out_shape), len(scratch_shapes)
    nc_in, nc_out = (len(comm.arrays), len(comm.out_shapes)) if comm else (0, 0)
    all_in = list(in_specs) + [ANY] * nc_in
    all_out = list(out_specs) + [ANY] * nc_out
    all_scr = list(scratch_shapes) + (list(comm.scratch) if comm else [])
    all_shape = list(out_shape) + (list(comm.out_shapes) if comm else [])
    kwargs = dict(name=name, compiler_params=_params(len(grid)), out_shape=all_shape)
    if n_pre:
        kwargs["grid_spec"] = pltpu.PrefetchScalarGridSpec(
            num_scalar_prefetch=n_pre, grid=grid, in_specs=all_in, out_specs=all_out, scratch_shapes=all_scr)
    else:
        kwargs.update(in_specs=all_in, out_specs=all_out, scratch_shapes=all_scr)
        if grid:
            kwargs["grid"] = grid
    args = list(prefetch) + list(operands) + (list(comm.arrays) if comm else [])
    if comm is None:
        return list(pl.pallas_call(body, **kwargs)(*args)), []

    def hosted(*refs):
        bounds = [0, n_pre, n_pre + n_in]
        for n in (nc_in, n_out, nc_out, n_scr):
            bounds.append(bounds[-1] + n)
        bounds.append(len(refs))
        pre, ins, cins, outs, couts, scr, sems = [refs[a:b] for a, b in zip(bounds[:-1], bounds[1:])]
        if not grid:
            comm.start(cins, couts, sems)
            body(*pre, *ins, *outs, *scr)
            comm.finish(cins, couts, sems)
            return
        first, last = None, None
        for ax, size in enumerate(grid):
            pid = pl.program_id(ax)
            f, l = pid == 0, pid == size - 1
            first = f if first is None else jnp.logical_and(first, f)
            last = l if last is None else jnp.logical_and(last, l)

        @pl.when(first)
        def _():
            comm.start(cins, couts, sems)

        body(*pre, *ins, *outs, *scr)

        @pl.when(last)
        def _():
            comm.finish(cins, couts, sems)

    res = pl.pallas_call(hosted, **kwargs)(*args)
    return list(res[:n_out]), list(res[n_out:])


def _exchange_only(name, comm):
    def body():
        pass
    return _host_call(body, name=name, grid=(), operands=[], in_specs=[], out_shape=[], out_specs=[], comm=comm)[1]


def _matmul(name, grid, operands, in_specs, pairs, out_shapes, out_specs, epilogue, acc_shapes=(), nk=1,
            prologue=None, comm=None):
    n_in, n_out = len(operands), len(out_shapes)
    prologue = prologue or {}

    def body(*refs):
        ins, outs, accs = refs[:n_in], refs[n_in:n_in + n_out], refs[n_in + n_out:]
        pids = [pl.program_id(ax) for ax in range(len(grid))]

        def operand(i, blk=None):
            v = ins[i][...] if blk is None else ins[i][blk]
            if i in prologue:
                v = prologue[i](v)
            return v.astype(BF16)

        def products():
            vals = {}
            for pair in pairs:
                ai, bi, ci, dn = pair[:4]
                if len(pair) == 5:
                    p = None
                    for blk in range(pair[4]):
                        q = lax.dot_general(operand(ai, blk), operand(bi, blk), dn, preferred_element_type=F32)
                        p = q if p is None else p + q
                else:
                    p = lax.dot_general(operand(ai), operand(bi), dn, preferred_element_type=F32)
                vals[ci] = p if ci not in vals else vals[ci] + p
            return [vals[ci] for ci in sorted(vals)]

        if nk == 1:
            epilogue(products(), ins, outs, pids)
        else:
            k = pids[-1]
            prods = products()

            @pl.when(k == 0)
            def _():
                for acc, p in zip(accs, prods):
                    acc[...] = p

            @pl.when(k > 0)
            def _():
                for acc, p in zip(accs, prods):
                    acc[...] += p

            @pl.when(k == nk - 1)
            def _():
                epilogue([acc[...] for acc in accs], ins, outs, pids)

    return _host_call(
        body, name=name, grid=grid, operands=operands, in_specs=in_specs, out_shape=out_shapes, out_specs=out_specs,
        scratch_shapes=[pltpu.VMEM(s, F32) for s in acc_shapes] if nk > 1 else [], comm=comm)


def _rowwise(name, n_tiles, operands, in_specs, out_shapes, out_specs, red_widths, fn, comm=None):
    n_in, n_out, n_red = len(operands), len(out_shapes), len(red_widths)

    def body(*refs):
        ins, outs, reds = refs[:n_in], refs[n_in:n_in + n_out], refs[n_in + n_out:]
        i = pl.program_id(0)
        vals, sums = fn(i, *[r[...] for r in ins])
        for o, v in zip(outs, vals):
            o[...] = v.astype(o.dtype)
        if n_red:
            @pl.when(i == 0)
            def _():
                for r, s in zip(reds, sums):
                    r[...] = s

            @pl.when(i > 0)
            def _():
                for r, s in zip(reds, sums):
                    r[...] += s

    red_shapes = [jax.ShapeDtypeStruct((1, w), F32) for w in red_widths]
    red_specs = [pl.BlockSpec((1, w), lambda i: (0, 0)) for w in red_widths]
    res, cres = _host_call(
        body, name=name, grid=(n_tiles,), operands=operands, in_specs=in_specs,
        out_shape=list(out_shapes) + red_shapes, out_specs=list(out_specs) + red_specs, comm=comm)
    return res[:n_out], res[n_out:], cres


def _colsum(v):
    return jnp.sum(v, axis=0, keepdims=True)


def _store_all(accs, ins, outs, pids):
    for o, v in zip(outs, accs):
        o[...] = v.astype(o.dtype)


def _row_tile(rows_a, rows_b):
    return _tile(math.gcd(rows_a, rows_b) if rows_b else rows_a, ELEMENTWISE_ROW_TILE, SUBLANES)


def _tab_row(d, nlt, rows2):
    return pl.BlockSpec((None, 1, d), lambda i: (jnp.where(i < nlt, rows2[0], rows2[1]), 0, 0))


def _norm_mod_fwd(name, xs, tab, r_gamma, r_shift, r_scale, n_lat, n_ctx):
    rows, d = xs.shape
    tm = _row_tile(n_lat, n_ctx)
    nlt = n_lat // tm

    def fn(i, x, g, sh, sc):
        xh = x * lax.rsqrt(jnp.mean(x * x, axis=-1, keepdims=True) + NORM_EPS)
        return [(xh * g) * (1.0 + sc) + sh], []

    (h,), _, _ = _rowwise(
        name, rows // tm, [xs, tab, tab, tab],
        [pl.BlockSpec((tm, d), lambda i: (i, 0)), _tab_row(d, nlt, (r_gamma, r_gamma)), _tab_row(d, nlt, r_shift),
         _tab_row(d, nlt, r_scale)],
        [jax.ShapeDtypeStruct((rows, d), BF16)], [pl.BlockSpec((tm, d), lambda i: (i, 0))], [], fn)
    return h


def _norm_mod_bwd(name, xs, dh, tab, r_gamma, r_scale, n_lat, n_ctx, dres=None):
    rows, d = xs.shape
    tm = _row_tile(n_lat, n_ctx)
    nlt = n_lat // tm
    row = pl.BlockSpec((tm, d), lambda i: (i, 0))

    def fn(i, x, dy, g, sc, *res):
        rstd = lax.rsqrt(jnp.mean(x * x, axis=-1, keepdims=True) + NORM_EPS)
        xh = x * rstd
        dsh = _colsum(dy)
        dsc = _colsum(dy * (xh * g))
        dn = dy * (1.0 + sc)
        dgam = _colsum(dn * xh)
        dxh = dn * g
        dx = rstd * (dxh - xh * jnp.mean(dxh * xh, axis=-1, keepdims=True))
        if res:
            dx = dx + jnp.where(i < nlt, res[0], 0.0)
        lat = (i < nlt).astype(F32)
        return [dx], [dsh * lat, dsc * lat, dsh * (1.0 - lat), dsc * (1.0 - lat), dgam]

    operands = [xs, dh, tab, tab]
    specs = [row, row, _tab_row(d, nlt, (r_gamma, r_gamma)), _tab_row(d, nlt, r_scale)]
    if dres is not None:
        operands.append(dres)
        specs.append(pl.BlockSpec((tm, d), lambda i: (jnp.minimum(i, nlt - 1), 0)))
    (dx,), sums, _ = _rowwise(name, rows // tm, operands, specs,
                              [jax.ShapeDtypeStruct((rows, d), F32)], [row], [d] * 5, fn)
    return dx, sums


def _gate_bwd(name, dx, f, tab, r_gate, coef, n_lat, n_ctx):
    rows, d = dx.shape
    tm = _row_tile(n_lat, n_ctx)
    nlt = n_lat // tm
    row = pl.BlockSpec((tm, d), lambda i: (i, 0))

    def fn(i, dxv, fv, gv):
        dg = _colsum(dxv * fv) * coef
        lat = (i < nlt).astype(F32)
        return [(coef * gv) * dxv], [dg * lat, dg * (1.0 - lat)]

    (df,), sums, _ = _rowwise(
        name, rows // tm, [dx, f, tab],
        [row, row, _tab_row(d, nlt, r_gate)],
        [jax.ShapeDtypeStruct((rows, d), BF16)], [row], [d, d], fn)
    return df, sums


def _select_rows(i, tm, n_lat, v_lat, v_ctx):
    rows = i * tm + lax.broadcasted_iota(jnp.int32, (tm, 1), 0)
    return jnp.where(rows < n_lat, v_lat, v_ctx)


def _ffn_up(tag, h, wg, wu, comm=None):
    rows, d = h.shape
    nb, fs, _ = wg.shape
    tm = _tile(rows, MM_TILE, LANES)
    blk = pl.BlockSpec((None, tm, fs), lambda j, i: (j, i, 0))
    wspec = pl.BlockSpec((None, fs, d), lambda j, i: (j, 0, 0))

    def epilogue(accs, ins, outs, pids):
        a, b = accs
        outs[0][...] = a.astype(BF16)
        outs[1][...] = b.astype(BF16)
        outs[2][...] = (a * _sigmoid(a) * b).astype(BF16)

    hid = jax.ShapeDtypeStruct((nb, rows, fs), BF16)
    (a, b, s), cres = _matmul(
        tag + "_up", (nb, rows // tm), [h, wg, wu],
        [pl.BlockSpec((tm, d), lambda j, i: (i, 0)), wspec, wspec],
        [(0, 1, 0, NT), (0, 2, 1, NT)], [hid, hid, hid], [blk, blk, blk], epilogue, comm=comm)
    return a, b, s, cres


def _ffn_down(tag, s, wd, xs, tab2, r_gate, n_lat, comm=None):
    nb, rows, fs = s.shape
    d = wd.shape[-1]
    tm = _tile(rows, MM_TILE, LANES)
    tn = _tile(d, MM_TILE, LANES)

    def epilogue(accs, ins, outs, pids):
        f = accs[0]
        g = ins[3][...]
        gate = _select_rows(pids[0], tm, n_lat, g[r_gate[0]:r_gate[0] + 1, :], g[r_gate[1]:r_gate[1] + 1, :])
        outs[0][...] = f
        outs[1][...] = ins[2][...] + 0.5 * gate * f

    out = jax.ShapeDtypeStruct((rows, d), F32)
    ospec = pl.BlockSpec((tm, tn), lambda i, n: (i, n))
    (f, xo), cres = _matmul(
        tag + "_down", (rows // tm, d // tn), [s, wd, xs, tab2],
        [pl.BlockSpec((nb, tm, fs), lambda i, n: (0, i, 0)), pl.BlockSpec((nb, fs, tn), lambda i, n: (0, 0, n)),
         ospec, pl.BlockSpec((tab2.shape[0], tn), lambda i, n: (0, n))],
        [(0, 1, 0, NN, nb)], [out, out], [ospec, ospec], epilogue, comm=comm)
    return f, xo, cres


def _ffn_ds(tag, df, wd, a, b, comm=None):
    rows, d = df.shape
    nb, fs, _ = wd.shape
    tm = _tile(rows, MM_TILE, LANES)
    blk = pl.BlockSpec((None, tm, fs), lambda j, i: (j, i, 0))

    def epilogue(accs, ins, outs, pids):
        ds = accs[0]
        av = ins[2][...].astype(F32)
        bv = ins[3][...].astype(F32)
        sg = _sigmoid(av)
        outs[0][...] = (ds * bv * (sg * (1.0 + av * (1.0 - sg)))).astype(BF16)
        outs[1][...] = (ds * (av * sg)).astype(BF16)

    hid = jax.ShapeDtypeStruct((nb, rows, fs), BF16)
    (da, db), cres = _matmul(
        tag + "_ds", (nb, rows // tm), [df, wd, a, b],
        [pl.BlockSpec((tm, d), lambda j, i: (i, 0)), pl.BlockSpec((None, fs, d), lambda j, i: (j, 0, 0)), blk, blk],
        [(0, 1, 0, NT)], [hid, hid], [blk, blk], epilogue, comm=comm)
    return da, db, cres


def _ffn_dwd(tag, s, df, comm=None):
    nb, rows, fs = s.shape
    d = df.shape[-1]
    tn = _tile(d, MM_TILE, LANES)
    (dwd,), cres = _matmul(
        tag + "_dwd", (nb, d // tn), [s, df],
        [pl.BlockSpec((None, rows, fs), lambda j, n: (j, 0, 0)), pl.BlockSpec((rows, tn), lambda j, n: (0, n))],
        [(0, 1, 0, TN)], [jax.ShapeDtypeStruct((nb, fs, d), BF16)],
        [pl.BlockSpec((None, fs, tn), lambda j, n: (j, 0, n))], _store_all, comm=comm)
    return dwd, cres


def _ffn_dwgu(tag, h, da, db, comm=None):
    rows, d = h.shape
    nb, _, fs = da.shape
    tno = _tile(d, MM_TILE, LANES)
    full = pl.BlockSpec((None, rows, fs), lambda j, m: (j, 0, 0))
    wshape = jax.ShapeDtypeStruct((nb, fs, d), BF16)
    wblk = pl.BlockSpec((None, fs, tno), lambda j, m: (j, 0, m))
    (dwg, dwu), cres = _matmul(
        tag + "_dwgu", (nb, d // tno), [h, da, db],
        [pl.BlockSpec((rows, tno), lambda j, m: (0, m)), full, full],
        [(1, 0, 0, TN), (2, 0, 1, TN)], [wshape, wshape], [wblk, wblk], _store_all, comm=comm)
    return dwg, dwu, cres


def _ffn_dh(tag, da, db, wg, wu, comm=None):
    nb, rows, fs = da.shape
    d = wg.shape[2]
    tm = _tile(rows, MM_TILE, LANES)
    tn = _tile(d, MM_TILE_NT, LANES)
    aspec = pl.BlockSpec((nb, tm, fs), lambda i, n: (0, i, 0))
    wspec = pl.BlockSpec((nb, fs, tn), lambda i, n: (0, 0, n))
    (dh,), cres = _matmul(
        tag + "_dh", (rows // tm, d // tn), [da, wg, db, wu], [aspec, wspec, aspec, wspec],
        [(0, 1, 0, NN, nb), (2, 3, 0, NN, nb)], [jax.ShapeDtypeStruct((rows, d), F32)],
        [pl.BlockSpec((tm, tn), lambda i, n: (i, n))], _store_all, comm=comm)
    return dh, cres


def _rope_tables(n_lat, n_ctx):
    half = LANES // 4
    inv_freq = (np.float32(ROPE_THETA) ** (-np.arange(half, dtype=np.float32) / np.float32(half))).astype(np.float32)
    pos = np.arange(n_lat)
    ang_r = (pos // GRID_W).astype(np.float32)[:, None] * inv_freq
    ang_c = (pos % GRID_W).astype(np.float32)[:, None] * inv_freq
    cos_l = np.concatenate([np.cos(ang_r)] * 2 + [np.cos(ang_c)] * 2, axis=1)
    sin_l = np.concatenate([-np.sin(ang_r), np.sin(ang_r), -np.sin(ang_c), np.sin(ang_c)], axis=1)
    cos_all = np.concatenate([cos_l, np.ones((n_ctx, LANES), np.float32)], axis=0).astype(np.float32)
    sin_all = np.concatenate([sin_l, np.zeros((n_ctx, LANES), np.float32)], axis=0).astype(np.float32)
    return jnp.asarray(cos_all), jnp.asarray(sin_all)


def _swap_halves(x):
    lane = lax.broadcasted_iota(jnp.int32, x.shape, 1)
    return jnp.where((lane % 64) < 32, pltpu.roll(x, 96, 1), pltpu.roll(x, 32, 1))


def _heads_spec(tq, hb, width, first_block):
    per_shard = width // (hb * LANES)

    def index(k, i):
        blk = first_block + k
        return blk // per_shard, i, blk % per_shard
    return pl.BlockSpec((None, tq, hb * LANES), index)


def _qk_prep(name, src, first_block, hb, n_heads, rows, g, cos_t, sin_t):
    tq = _tile(rows, HEAD_ROW_TILE, SUBLANES)
    tab = pl.BlockSpec((tq, LANES), lambda k, i: (i, 0))

    def body(x_ref, g_ref, c_ref, s_ref, o_ref):
        for h in range(hb):
            x = x_ref[:, h * LANES:(h + 1) * LANES]
            n = x * lax.rsqrt(jnp.mean(x * x, axis=-1, keepdims=True) + NORM_EPS) * g_ref[...]
            o_ref[h] = (n * c_ref[...] + _swap_halves(n) * s_ref[...]).astype(BF16)

    return pl.pallas_call(
        body, name=name, grid=(n_heads // hb, rows // tq),
        in_specs=[_heads_spec(tq, hb, src.shape[-1], first_block), pl.BlockSpec((1, LANES), lambda k, i: (0, 0)),
                  tab, tab],
        out_specs=pl.BlockSpec((hb, tq, LANES), lambda k, i: (k, i, 0)),
        out_shape=jax.ShapeDtypeStruct((n_heads, rows, LANES), BF16), compiler_params=_params(2),
    )(src, g, cos_t, sin_t)


def _qk_prep_bwd(name, dy, src, first_block, hb, n_heads, rows, g, cos_t, sin_t):
    tq = _tile(rows, HEAD_ROW_TILE, SUBLANES)
    tab = pl.BlockSpec((tq, LANES), lambda k, i: (i, 0))

    def body(dy_ref, x_ref, g_ref, c_ref, s_ref, dx_ref, dg_ref):
        g = g_ref[...]
        dg = None
        for h in range(hb):
            x = x_ref[:, h * LANES:(h + 1) * LANES]
            dyv = dy_ref[h]
            rstd = lax.rsqrt(jnp.mean(x * x, axis=-1, keepdims=True) + NORM_EPS)
            xh = x * rstd
            dn = dyv * c_ref[...] + _swap_halves(dyv * s_ref[...])
            dxh = dn * g
            dx = rstd * (dxh - xh * jnp.mean(dxh * xh, axis=-1, keepdims=True))
            dx_ref[:, h * LANES:(h + 1) * LANES] = dx.astype(BF16)
            part = _colsum(dn * xh)
            dg = part if dg is None else dg + part
        first = jnp.logical_and(pl.program_id(0) == 0, pl.program_id(1) == 0)

        @pl.when(first)
        def _():
            dg_ref[...] = dg

        @pl.when(jnp.logical_not(first))
        def _():
            dg_ref[...] += dg

    return pl.pallas_call(
        body, name=name, grid=(n_heads // hb, rows // tq),
        in_specs=[pl.BlockSpec((hb, tq, LANES), lambda k, i: (k, i, 0)),
                  _heads_spec(tq, hb, src.shape[-1], first_block),
                  pl.BlockSpec((1, LANES), lambda k, i: (0, 0)), tab, tab],
        out_specs=[pl.BlockSpec((None, tq, hb * LANES), lambda k, i: (k, i, 0)),
                   pl.BlockSpec((1, LANES), lambda k, i: (0, 0))],
        out_shape=[jax.ShapeDtypeStruct((n_heads // hb, rows, hb * LANES), BF16),
                   jax.ShapeDtypeStruct((1, LANES), F32)],
        compiler_params=_params(2),
    )(dy, src, g, cos_t, sin_t)


def _heads_cast(name, src, first_block, hb, n_heads, rows):
    tq = _tile(rows, HEAD_ROW_TILE, SUBLANES)

    def body(x_ref, o_ref):
        for h in range(hb):
            o_ref[h] = x_ref[:, h * LANES:(h + 1) * LANES].astype(BF16)

    return pl.pallas_call(
        body, name=name, grid=(n_heads // hb, rows // tq),
        in_specs=[_heads_spec(tq, hb, src.shape[-1], first_block)],
        out_specs=pl.BlockSpec((hb, tq, LANES), lambda k, i: (k, i, 0)),
        out_shape=jax.ShapeDtypeStruct((n_heads, rows, LANES), BF16), compiler_params=_params(2),
    )(src)


def _heads_merge(name, src):
    n_heads, rows, _ = src.shape
    tq = _tile(rows, HEAD_ROW_TILE, SUBLANES)

    def body(x_ref, o_ref):
        for h in range(n_heads):
            o_ref[:, h * LANES:(h + 1) * LANES] = x_ref[h].astype(BF16)

    return pl.pallas_call(
        body, name=name, grid=(rows // tq,),
        in_specs=[pl.BlockSpec((n_heads, tq, LANES), lambda i: (0, i, 0))],
        out_specs=pl.BlockSpec((tq, n_heads * LANES), lambda i: (i, 0)),
        out_shape=jax.ShapeDtypeStruct((rows, n_heads * LANES), BF16), compiler_params=_params(1),
    )(src)


def _attn_fwd(q, k, v, q_per_kv, comm=None):
    nq, l, _ = q.shape
    s_len = k.shape[1]
    tq = _tile(l, ROW_TILE, SUBLANES)
    scale = LANES ** -0.5
    hp = ATTN_FWD_HEADS if q_per_kv % ATTN_FWD_HEADS == 0 else 1
    kv = pl.BlockSpec((None, s_len, LANES), lambda h, i: ((h * hp) // q_per_kv, 0, 0))

    def body(q_ref, k_ref, v_ref, o_ref):
        for h in range(hp):
            s = lax.dot_general(q_ref[h], k_ref[...], NT, preferred_element_type=F32)
            p = jnp.exp2((s - jnp.max(s, axis=-1, keepdims=True)) * (scale * LOG2E))
            den = jnp.sum(p, axis=-1, keepdims=True)
            o = jnp.dot(p.astype(BF16), v_ref[...], preferred_element_type=F32)
            o_ref[:, h * LANES:(h + 1) * LANES] = (o * (1.0 / den)).astype(BF16)

    (o,), cres = _host_call(
        body, name="attn_fwd", grid=(nq // hp, l // tq), operands=[q, k, v],
        in_specs=[pl.BlockSpec((hp, tq, LANES), lambda h, i: (h, i, 0)), kv, kv],
        out_shape=[jax.ShapeDtypeStruct((l, nq * LANES), BF16)],
        out_specs=[pl.BlockSpec((tq, hp * LANES), lambda h, i: (i, h))], comm=comm)
    return o, cres


def _attn_bwd(q, k, v, do, q_per_kv, comm=None):
    nq, l, _ = q.shape
    nkv, s_len, _ = k.shape
    tq = _tile(l, ROW_TILE, SUBLANES)
    scale = LANES ** -0.5
    hp = ATTN_BWD_HEADS if q_per_kv % ATTN_BWD_HEADS == 0 else 1
    kv = pl.BlockSpec((None, s_len, LANES), lambda g, r, i: (g, 0, 0))
    qs = pl.BlockSpec((hp, tq, LANES), lambda g, r, i: (g * (q_per_kv // hp) + r, i, 0))

    def body(q_ref, k_ref, v_ref, do_ref, dq_ref, dk_ref, dv_ref):
        kvv, vv = k_ref[...], v_ref[...]
        dk_new = dv_new = None
        for h in range(hp):
            qv, dov = q_ref[h], do_ref[:, h * LANES:(h + 1) * LANES]
            st = lax.dot_general(kvv, qv, NT, preferred_element_type=F32)
            e = jnp.exp2((st - jnp.max(st, axis=0, keepdims=True)) * (scale * LOG2E))
            pt = e * (1.0 / jnp.sum(e, axis=0, keepdims=True))
            dpt = lax.dot_general(vv, dov, NT, preferred_element_type=F32)
            delta = jnp.sum(pt * dpt, axis=0, keepdims=True)
            dst = (pt * (dpt - delta)).astype(BF16)
            dq_ref[h] = lax.dot_general(dst, kvv, TN, preferred_element_type=F32) * scale
            dk_h = jnp.dot(dst, qv, preferred_element_type=F32) * scale
            dv_h = jnp.dot(pt.astype(BF16), dov, preferred_element_type=F32)
            dk_new = dk_h if dk_new is None else dk_new + dk_h
            dv_new = dv_h if dv_new is None else dv_new + dv_h
        first = jnp.logical_and(pl.program_id(1) == 0, pl.program_id(2) == 0)

        @pl.when(first)
        def _():
            dk_ref[...] = dk_new
            dv_ref[...] = dv_new

        @pl.when(jnp.logical_not(first))
        def _():
            dk_ref[...] += dk_new
            dv_ref[...] += dv_new

    (dq, dk, dv), cres = _host_call(
        body, name="attn_bwd", grid=(nkv, q_per_kv // hp, l // tq), operands=[q, k, v, do],
        in_specs=[qs, kv, kv, pl.BlockSpec((tq, hp * LANES), lambda g, r, i: (i, g * (q_per_kv // hp) + r))],
        out_specs=[qs, kv, kv],
        out_shape=[jax.ShapeDtypeStruct((nq, l, LANES), F32), jax.ShapeDtypeStruct((nkv, s_len, LANES), F32),
                   jax.ShapeDtypeStruct((nkv, s_len, LANES), F32)], comm=comm)
    return dq, dk, dv, cres


def _zoh(a_re, a_im, log_dt):
    dt = jnp.exp(log_dt)[..., None]
    mag = jnp.exp(a_re * dt)
    lb_re = mag * jnp.cos(a_im * dt)
    lb_im = mag * jnp.sin(a_im * dt)
    den = a_re * a_re + a_im * a_im
    coef_re = ((lb_re - 1.0) * a_re + lb_im * a_im) / den
    coef_im = (lb_im * a_re - (lb_re - 1.0) * a_im) / den
    return lb_re, lb_im, coef_re, coef_im


def _ssm_discretize(a_re, a_im, log_dt, b_re, b_im):
    lb_re, lb_im, cr, ci = _zoh(a_re, a_im, log_dt)
    bt_re = cr[..., None] * b_re - ci[..., None] * b_im
    bt_im = cr[..., None] * b_im + ci[..., None] * b_re
    return lb_re, lb_im, bt_re, bt_im


def _lambda_powers(a_re, a_im, log_dt, ns):
    dt = jnp.exp(log_dt)[..., None]
    k = jnp.arange(SCAN_TAPS + 1, dtype=F32)[:, None, None, None]
    mag, ang = jnp.exp(k * (a_re * dt)), k * (a_im * dt)
    shape = (SCAN_TAPS + 1, 2, ns, -1)
    return (mag * jnp.cos(ang)).reshape(shape), (mag * jnp.sin(ang)).reshape(shape)


def _slab_mask():
    idx = jnp.arange(SLAB_GROUPS)
    return (idx[:, None] == idx[None, :])[None, None, :, None, :, None]


def _block_diag(m):
    d, g, a, b = m.shape
    ns = g // SLAB_GROUPS
    wide = jnp.where(_slab_mask(), m.reshape(d, ns, SLAB_GROUPS, a, 1, b), 0.0)
    return wide.reshape(d, ns, SLAB_GROUPS * a, SLAB_GROUPS * b)


def _block_diag_extract(m, a, b):
    d, ns = m.shape[:2]
    m = m.reshape(d, ns, SLAB_GROUPS, a, SLAB_GROUPS, b)
    return jnp.sum(jnp.where(_slab_mask(), m, 0.0), axis=4).reshape(d, ns * SLAB_GROUPS, a, b)


def _build_tap_weights(w_ref, base_ref, pw_ref, conj, sw):
    b_re, b_im = base_ref[:, :sw], base_ref[:, sw:]
    for tau in range(SCAN_TAPS):
        p_re, p_im = pw_ref[tau:tau + 1, :sw], pw_ref[tau:tau + 1, sw:]
        if conj:
            p_im = -p_im
        w_ref[tau * LANES:(tau + 1) * LANES, :sw] = (p_re * b_re - p_im * b_im).astype(BF16)
        w_ref[tau * LANES:(tau + 1) * LANES, sw:] = (p_re * b_im + p_im * b_re).astype(BF16)


def _carry_tables(pw_re, pw_im, descending):
    def rows(pw):
        asc = pw[1:]
        per_dir = [asc[::-1, d] if descending[d] else asc[:, d] for d in range(2)]
        return jnp.transpose(jnp.stack(per_dir), (0, 2, 1, 3))
    return jnp.concatenate([rows(pw_re), rows(pw_im)], axis=-1)


def _scan_chunk(x, w_ref, tab_ref, s_ref, carry_ref, descending, t_rows, sw):
    row8 = lax.broadcasted_iota(jnp.int32, x.shape, 0) % SCAN_TAPS
    pieces = [x.astype(BF16)]
    for tau in range(1, SCAN_TAPS):
        if descending:
            sh = jnp.where(row8 <= SCAN_TAPS - 1 - tau, pltpu.roll(x, t_rows - tau, 0), 0.0)
        else:
            sh = jnp.where(row8 >= tau, pltpu.roll(x, tau, 0), 0.0)
        pieces.append(sh.astype(BF16))
    xa = jnp.concatenate(pieces, axis=1)
    s_ref[...] = jnp.dot(xa, w_ref[...], preferred_element_type=F32)
    tab = tab_ref[...]
    t_re, t_im = tab[:, :sw], tab[:, sw:]
    nb = t_rows // SCAN_TAPS
    edge = 0 if descending else SCAN_TAPS - 1

    def step(b, carry):
        h_re, h_im = carry
        r0 = pl.multiple_of(((nb - 1 - b) if descending else b) * SCAN_TAPS, SCAN_TAPS)
        x_re = s_ref[pl.ds(r0, SCAN_TAPS), :sw] + t_re * h_re - t_im * h_im
        x_im = s_ref[pl.ds(r0, SCAN_TAPS), sw:] + t_re * h_im + t_im * h_re
        s_ref[pl.ds(r0, SCAN_TAPS), :sw] = x_re
        s_ref[pl.ds(r0, SCAN_TAPS), sw:] = x_im
        return x_re[edge:edge + 1, :], x_im[edge:edge + 1, :]

    h_re, h_im = lax.fori_loop(0, nb, step, (carry_ref[0:1, :sw], carry_ref[0:1, sw:]))
    carry_ref[0:1, :sw] = h_re
    carry_ref[0:1, sw:] = h_im


def _slab_spec(rows, cols, dr):
    return pl.BlockSpec((None, None, rows, cols), lambda s, i: (dr, s, 0, 0))


def _ssm_fwd(name, dr, u_src, u_shard, bd, pw, tab, ct, descending, chunk_of, t_rows, rows, comm=None):
    _, ns, _, sw2 = bd.shape
    sw = sw2 // 2
    width = ns * LANES
    nchunks = rows // t_rows

    def body(u_ref, bd_ref, pw_ref, tab_ref, ct_ref, y_ref, h_ref, s_ref, carry_ref, w_ref):
        @pl.when(pl.program_id(1) == 0)
        def _():
            carry_ref[...] = jnp.zeros_like(carry_ref)
            _build_tap_weights(w_ref, bd_ref, pw_ref, False, sw)

        _scan_chunk(u_ref[...], w_ref, tab_ref, s_ref, carry_ref, descending, t_rows, sw)
        hb = s_ref[...].astype(BF16)
        h_ref[...] = hb
        y_ref[...] = lax.dot_general(hb, ct_ref[...], NT, preferred_element_type=F32)

    (y, h), cres = _host_call(
        body, name=name, grid=(ns, nchunks), operands=[u_src, bd, pw, tab, ct],
        in_specs=[pl.BlockSpec((None, t_rows, LANES), lambda s, i: (u_shard, chunk_of(i), s)),
                  _slab_spec(LANES, sw2, dr), _slab_spec(2 * SCAN_TAPS, sw2, dr), _slab_spec(SCAN_TAPS, sw2, dr),
                  _slab_spec(LANES, sw2, dr)],
        out_specs=[pl.BlockSpec((t_rows, LANES), lambda s, i: (chunk_of(i), s)),
                   pl.BlockSpec((None, t_rows, sw2), lambda s, i: (s, chunk_of(i), 0))],
        out_shape=[jax.ShapeDtypeStruct((rows, width), F32), jax.ShapeDtypeStruct((ns, rows, sw2), BF16)],
        scratch_shapes=[pltpu.VMEM((t_rows, sw2), F32), pltpu.VMEM((SUBLANES, sw2), F32),
                        pltpu.VMEM((SCAN_TAPS * LANES, sw2), BF16)], comm=comm)
    return y, h, cres


def _ssm_bwd(name, dr, dy, u_src, u_shard, states, ct, pw, tab, bd, descending, chunk_of, t_rows, rows, comm=None):
    _, ns, _, sw2 = ct.shape
    sw = sw2 // 2
    width = ns * LANES
    nchunks = rows // t_rows

    def body(dy_ref, u_ref, h_ref, ct_ref, pw_ref, tab_ref, bd_ref, du_ref, dbd_ref, dcd_ref, dlam_ref,
             s_ref, carry_ref, gsave_ref, w_ref):
        first = pl.program_id(1) == 0

        @pl.when(first)
        def _():
            carry_ref[...] = jnp.zeros_like(carry_ref)
            gsave_ref[...] = jnp.zeros_like(gsave_ref)
            _build_tap_weights(w_ref, ct_ref, pw_ref, True, sw)

        dyv = dy_ref[...]
        _scan_chunk(dyv, w_ref, tab_ref, s_ref, carry_ref, descending, t_rows, sw)
        g = s_ref[...]
        gb = g.astype(BF16)
        du_ref[...] = lax.dot_general(gb, bd_ref[...], NT, preferred_element_type=F32)
        dbd = lax.dot_general(u_ref[...].astype(BF16), gb, TN, preferred_element_type=F32)
        hb = h_ref[...]
        dcd = lax.dot_general(hb, dyv.astype(BF16), TN, preferred_element_type=F32)
        hf = hb.astype(F32)
        rowid = lax.broadcasted_iota(jnp.int32, hf.shape, 0)
        if descending:
            hp = jnp.where(rowid == 0, 0.0, pltpu.roll(hf, 1, 0))
            h_edge, g_edge = hf[t_rows - 1:t_rows, :], g[0:1, :]
        else:
            hp = jnp.where(rowid == t_rows - 1, 0.0, pltpu.roll(hf, t_rows - 1, 0))
            h_edge, g_edge = hf[0:1, :], g[t_rows - 1:t_rows, :]
        g_re, g_im, hp_re, hp_im = g[:, :sw], g[:, sw:], hp[:, :sw], hp[:, sw:]
        gs = gsave_ref[0:1, :]
        gs_re, gs_im, he_re, he_im = gs[:, :sw], gs[:, sw:], h_edge[:, :sw], h_edge[:, sw:]
        dl_re = _colsum(g_re * hp_re + g_im * hp_im) + gs_re * he_re + gs_im * he_im
        dl_im = _colsum(g_im * hp_re - g_re * hp_im) + gs_im * he_re - gs_re * he_im
        gsave_ref[0:1, :] = g_edge

        @pl.when(first)
        def _():
            dbd_ref[...] = dbd
            dcd_ref[...] = dcd
            dlam_ref[:, :sw] = dl_re
            dlam_ref[:, sw:] = dl_im

        @pl.when(jnp.logical_not(first))
        def _():
            dbd_ref[...] += dbd
            dcd_ref[...] += dcd
            dlam_ref[:, :sw] += dl_re
            dlam_ref[:, sw:] += dl_im

    (du, dbd, dcd, dlam), cres = _host_call(
        body, name=name, grid=(ns, nchunks), operands=[dy, u_src, states, ct, pw, tab, bd],
        in_specs=[pl.BlockSpec((t_rows, LANES), lambda s, i: (chunk_of(i), s)),
                  pl.BlockSpec((None, t_rows, LANES), lambda s, i: (u_shard, chunk_of(i), s)),
                  pl.BlockSpec((None, t_rows, sw2), lambda s, i: (s, chunk_of(i), 0)),
                  _slab_spec(LANES, sw2, dr), _slab_spec(2 * SCAN_TAPS, sw2, dr), _slab_spec(SCAN_TAPS, sw2, dr),
                  _slab_spec(LANES, sw2, dr)],
        out_specs=[pl.BlockSpec((t_rows, LANES), lambda s, i: (chunk_of(i), s)),
                   pl.BlockSpec((None, LANES, sw2), lambda s, i: (s, 0, 0)),
                   pl.BlockSpec((None, sw2, LANES), lambda s, i: (s, 0, 0)),
                   pl.BlockSpec((None, 1, sw2), lambda s, i: (s, 0, 0))],
        out_shape=[jax.ShapeDtypeStruct((rows, width), F32), jax.ShapeDtypeStruct((ns, LANES, sw2), F32),
                   jax.ShapeDtypeStruct((ns, sw2, LANES), F32), jax.ShapeDtypeStruct((ns, 1, sw2), F32)],
        scratch_shapes=[pltpu.VMEM((t_rows, sw2), F32), pltpu.VMEM((SUBLANES, sw2), F32),
                        pltpu.VMEM((SUBLANES, sw2), F32), pltpu.VMEM((SCAN_TAPS * LANES, sw2), BF16)], comm=comm)
    return du, dbd, dcd, dlam, cres


def _mod_fwd(cs, w_mod, b_cols):
    d, width = w_mod.shape
    tn = _tile(width, 768, LANES)

    def epilogue(accs, ins, outs, pids):
        outs[0][...] = accs[0] + ins[2][...]

    return _matmul(
        "mod_fwd", (width // tn,), [cs, w_mod, b_cols],
        [pl.BlockSpec((16, d), lambda n: (0, 0)), pl.BlockSpec((d, tn), lambda n: (0, n)),
         pl.BlockSpec((1, tn), lambda n: (0, n))],
        [(0, 1, 0, NN)], [jax.ShapeDtypeStruct((16, width), F32)], [pl.BlockSpec((16, tn), lambda n: (0, n))],
        epilogue, prologue={0: lambda v: v * _sigmoid(v)})[0][0]


def _mod_bwd_adam(cs, dmod_cols, w, m, v, comm=None):
    d, width = w.shape
    tn = _tile(width, LANES, LANES)
    col = pl.BlockSpec((d, tn), lambda n: (0, n))

    def body(cs_ref, dm_ref, w_ref, m_ref, v_ref, g_ref, dl_ref, nm_ref, nv_ref, ds_ref):
        n = pl.program_id(0)
        lat = dm_ref[pl.ds(0, N_DEV, stride=SUBLANES), :]
        ctx = jnp.sum(dm_ref[pl.ds(1, N_DEV, stride=SUBLANES), :], axis=0, keepdims=True)
        row = lax.broadcasted_iota(jnp.int32, lat.shape, 0)
        dm = jnp.concatenate([lat, jnp.where(row == 0, ctx, 0.0)], axis=0).astype(BF16)
        c = cs_ref[...]
        sc = (c * _sigmoid(c)).astype(BF16)
        wv = w_ref[...]
        g = lax.dot_general(sc, dm, TN, preferred_element_type=F32)
        delta, m2, v2 = _adamw(wv, g, m_ref[...], v_ref[...])
        g_ref[...] = g
        dl_ref[...] = delta
        nm_ref[...] = m2
        nv_ref[...] = v2
        part = lax.dot_general(dm, wv.astype(BF16), NT, preferred_element_type=F32)

        @pl.when(n == 0)
        def _():
            ds_ref[...] = part

        @pl.when(n > 0)
        def _():
            ds_ref[...] += part

    shard = jax.ShapeDtypeStruct((d, width), F32)
    return _host_call(
        body, name="mod_bwd_adam", grid=(width // tn,), operands=[cs, dmod_cols, w, m, v],
        in_specs=[pl.BlockSpec((16, d), lambda n: (0, 0)), pl.BlockSpec((N_DEV * SUBLANES, tn), lambda n: (0, n)),
                  col, col, col],
        out_specs=[col, col, col, col, pl.BlockSpec((16, d), lambda n: (0, 0))],
        out_shape=[shard, shard, shard, shard, jax.ShapeDtypeStruct((16, d), F32)], comm=comm)


def _pair_sum(name, grads, got, core):
    _, rows, cols = grads.shape
    tr = _tile(rows, max(PACKED_SUBLANES, ADAM_BLOCK_BYTES // (cols * 6 * N_CHIPS)), PACKED_SUBLANES)
    blk = pl.BlockSpec((N_CHIPS, tr, cols), lambda i, cc: (0, i, 0))

    def body(core_ref, a_ref, b_ref, o_ref):
        o_ref[...] = (a_ref[...].astype(F32) + b_ref[...].astype(F32)).astype(BF16)

    grid_spec = pltpu.PrefetchScalarGridSpec(
        num_scalar_prefetch=1, grid=(rows // tr,),
        in_specs=[pl.BlockSpec((N_CHIPS, None, tr, cols), lambda i, cc: (0, cc[0], i, 0)), blk], out_specs=blk)
    return pl.pallas_call(
        body, name=name, grid_spec=grid_spec, out_shape=jax.ShapeDtypeStruct((N_CHIPS, rows, cols), BF16),
        compiler_params=_params(1))(core, grads.reshape(N_CHIPS, 2, rows, cols), got)


def _owner_adam(name, items, chip, comm=None):
    plan, start = [], 0
    per_element = 2 * (2 * N_CHIPS + 7 * 4)
    block_elements = ADAM_GROUP_VMEM // (per_element * len(items))
    for _, _, w, _, _ in items:
        rows, cols = w.shape
        tr = _tile(rows, max(PACKED_SUBLANES, block_elements // cols), PACKED_SUBLANES)
        plan.append((start, rows // tr, tr, cols))
        start += rows // tr
    operands, in_specs, out_specs, out_shape = [], [], [], []
    for (first, nt, tr, cols), (p, l, w, m, v) in zip(plan, items):
        def tile(s, first=first, nt=nt):
            return jnp.clip(s - first, 0, nt - 1)
        blk = pl.BlockSpec((tr, cols), lambda s, ch, tile=tile: (tile(s), 0))
        operands += [p, l, w, m, v]
        in_specs += [pl.BlockSpec((None, tr, cols), lambda s, ch, tile=tile: (ch[0], tile(s), 0)),
                     pl.BlockSpec((N_CHIPS - 1, tr, cols), lambda s, ch, tile=tile: (0, tile(s), 0)), blk, blk, blk]
        out_specs += [blk] * 4
        out_shape += [jax.ShapeDtypeStruct(w.shape, F32)] * 4
    n = len(items)

    def body(chip_ref, *refs):
        s = pl.program_id(0)
        for k, (first, nt, _, _) in enumerate(plan):
            p_ref, l_ref, w_ref, m_ref, v_ref = refs[5 * k:5 * k + 5]
            g_ref, dl_ref, nm_ref, nv_ref = refs[5 * n + 4 * k:5 * n + 4 * k + 4]

            @pl.when(jnp.logical_and(s >= first, s < first + nt))
            def _(p_ref=p_ref, l_ref=l_ref, w_ref=w_ref, m_ref=m_ref, v_ref=v_ref,
                  g_ref=g_ref, dl_ref=dl_ref, nm_ref=nm_ref, nv_ref=nv_ref):
                g = p_ref[...].astype(F32)
                for r in range(N_CHIPS - 1):
                    g = g + l_ref[r].astype(F32)
                delta, m2, v2 = _adamw(w_ref[...], g, m_ref[...], v_ref[...])
                g_ref[...] = g
                dl_ref[...] = delta
                nm_ref[...] = m2
                nv_ref[...] = v2

    res, cres = _host_call(body, name=name, grid=(start,), operands=operands, in_specs=in_specs,
                           out_shape=out_shape, out_specs=out_specs, comm=comm, prefetch=[chip])
    return [res[4 * k:4 * k + 4] for k in range(n)], cres


def _sum_adam(name, parts, w, m, v):
    rows, cols = w.shape
    n_parts = parts.shape[0]
    align = PACKED_SUBLANES if parts.dtype == BF16 else SUBLANES
    tr = _tile(rows, max(align, ADAM_BLOCK_BYTES // (cols * 44)), align)
    blk = pl.BlockSpec((tr, cols), lambda i: (i, 0))

    def body(p_ref, w_ref, m_ref, v_ref, g_ref, dl_ref, nm_ref, nv_ref):
        g = p_ref[0].astype(F32)
        for s in range(1, n_parts):
            g = g + p_ref[s].astype(F32)
        delta, m2, v2 = _adamw(w_ref[...], g, m_ref[...], v_ref[...])
        g_ref[...] = g
        dl_ref[...] = delta
        nm_ref[...] = m2
        nv_ref[...] = v2

    out = jax.ShapeDtypeStruct((rows, cols), F32)
    return pl.pallas_call(
        body, name=name, grid=(rows // tr,),
        in_specs=[pl.BlockSpec((n_parts, tr, cols), lambda i: (0, i, 0)), blk, blk, blk],
        out_specs=[blk, blk, blk, blk], out_shape=[out, out, out, out], compiler_params=_params(1),
    )(parts, w, m, v)


def _bias_adam(dmod_all, w, m, v):
    width = w.shape[-1]
    tn = _tile(width, 2048, LANES)
    blk = pl.BlockSpec((1, tn), lambda n: (0, n))

    def body(p_ref, w_ref, m_ref, v_ref, g_ref, dl_ref, nm_ref, nv_ref):
        g = jnp.sum(p_ref[...], axis=0, keepdims=True)
        delta, m2, v2 = _adamw(w_ref[...], g, m_ref[...], v_ref[...])
        g_ref[...] = g
        dl_ref[...] = delta
        nm_ref[...] = m2
        nv_ref[...] = v2

    out = jax.ShapeDtypeStruct((1, width), F32)
    return pl.pallas_call(
        body, name="bias_adam", grid=(width // tn,),
        in_specs=[pl.BlockSpec((dmod_all.shape[0], tn), lambda n: (0, n)), blk, blk, blk],
        out_specs=[blk, blk, blk, blk], out_shape=[out, out, out, out], compiler_params=_params(1),
    )(dmod_all, w, m, v)


def _pack(arrays, total_rows):
    flat = []
    for a in arrays:
        a = a.reshape(-1).astype(F32)
        flat.append(jnp.pad(a, (0, (-a.shape[0]) % LANES)))
    flat = jnp.concatenate(flat).reshape(-1, LANES)
    return jnp.pad(flat, ((0, total_rows - flat.shape[0]), (0, 0)))


def _unpack(packed, shapes):
    out, row = [], 0
    for shp in shapes:
        size = math.prod(shp)
        nrows = -(-size // LANES)
        out.append(packed[row:row + nrows].reshape(-1)[:size].reshape(shp))
        row += nrows
    return out


def kernel(x, c, ctx, c_ctx, w_mod, b_mod, norm_g, w_ffn1_gate, w_ffn1_up, w_ffn1_down, w_in, q_norm_g, k_norm_g, ssm_a_re, ssm_a_im, ssm_log_dt, ssm_b_re, ssm_b_im, ssm_c_re, ssm_c_im, ssm_d, w_glu, b_glu, w_br_attn, w_br_ssm, w_out, w_ffn2_gate, w_ffn2_up, w_ffn2_down, loss_target, m_c_ctx, m_w_mod, m_b_mod, m_norm_g, m_w_ffn1_gate, m_w_ffn1_up, m_w_ffn1_down, m_w_in, m_q_norm_g, m_k_norm_g, m_ssm_a_re, m_ssm_a_im, m_ssm_log_dt, m_ssm_b_re, m_ssm_b_im, m_ssm_c_re, m_ssm_c_im, m_ssm_d, m_w_glu, m_b_glu, m_w_br_attn, m_w_br_ssm, m_w_out, m_w_ffn2_gate, m_w_ffn2_up, m_w_ffn2_down, v_c_ctx, v_w_mod, v_b_mod, v_norm_g, v_w_ffn1_gate, v_w_ffn1_up, v_w_ffn1_down, v_w_in, v_q_norm_g, v_k_norm_g, v_ssm_a_re, v_ssm_a_im, v_ssm_log_dt, v_ssm_b_re, v_ssm_b_im, v_ssm_c_re, v_ssm_c_im, v_ssm_d, v_w_glu, v_b_glu, v_w_br_attn, v_w_br_ssm, v_w_out, v_w_ffn2_gate, v_w_ffn2_up, v_w_ffn2_down):
    _, L, D = x.shape
    Lc = ctx.shape[1]
    R = L + Lc
    MODW = w_mod.shape[-1]
    INS = w_in.shape[-1]
    KVW = INS // 2
    NQ = D // LANES
    NKV = KVW // LANES
    QPK = NQ // NKV
    HBQ = INS // LANES
    G, P, E = ssm_b_re.shape[2:]
    W = G * E
    SW = SLAB_GROUPS * P
    assert E * SLAB_GROUPS == LANES and W == INS and NQ * LANES == D and Lc <= L
    me = 4 * lax.axis_index("x") + 2 * lax.axis_index("y") + lax.axis_index("c")

    x2, ctx2, tgt = x[0], ctx[0], loss_target[0]
    xc0 = jnp.concatenate([x2, ctx2], axis=0)

    def bf(w):
        return w[0].astype(BF16)

    def held_t(w):
        return jnp.swapaxes(w[0], 0, 1)

    def bft(w):
        return held_t(w).astype(BF16)

    def widen(a):
        return jnp.pad(a[0], ((0, 0), (0, D - a.shape[-1])))

    def at_row(a, r, total):
        return jnp.pad(a, ((r, total - r - a.shape[0]), (0, 0)))

    pack_in = (at_row(c, 0, 16) + at_row(widen(norm_g), 1, 16) + at_row(widen(m_norm_g), 4, 16)
               + at_row(widen(v_norm_g), 7, 16))
    (g_in,) = _exchange_only("ag_inputs", _Gather([pack_in]))
    c_all = g_in[:, 0, :]
    dn = D // N_DEV

    def full_norm(k):
        return jnp.transpose(g_in[:, k:k + 3, :dn], (1, 0, 2)).reshape(3, D)

    ng_full, m_ng_full, v_ng_full = full_norm(1), full_norm(4), full_norm(7)
    cs = at_row(c_all, 0, 16) + at_row(c_ctx[None, :], 8, 16)

    b_cols = lax.dynamic_slice_in_dim(b_mod, me * MODW, MODW, axis=1)
    mod_blk = _mod_fwd(cs, w_mod[0], b_cols)
    (mod_g,) = _exchange_only("ag_mod", _Gather([mod_blk]))
    mod_lat = lax.dynamic_index_in_dim(mod_g, me, axis=1, keepdims=False).reshape(9, D)
    mod_ctx = mod_g[:, 8, :].reshape(9, D)[:5]
    tab2 = jnp.concatenate([mod_lat, mod_ctx, ng_full, jnp.zeros((7, D), F32)], axis=0)
    tab3 = tab2[:, None, :]
    SH1, SC1, G1, SH2, SC2, G2, SH3, SC3, G3, MC0, MC1, MC2, MC3, MC4, GAM1, GAM2, GAM3 = range(17)

    wg1, wu1 = _exchange_only("ag_ffn1_gate_up", _Gather([bft(w_ffn1_gate), bft(w_ffn1_up)]))
    h1 = _norm_mod_fwd("nm1_fwd", xc0, tab3, GAM1, (SH1, MC0), (SC1, MC1), L, Lc)
    a1, b1, s1, (wd1,) = _ffn_up("ffn1", h1, wg1, wu1, comm=_Gather([bf(w_ffn1_down)]))
    f1, xc1, (win,) = _ffn_down("ffn1", s1, wd1, xc0, tab2, (G1, MC2), L, comm=_Gather([bf(w_in)]))

    h2 = _norm_mod_fwd("nm2_fwd", xc1, tab3, GAM2, (SH2, MC3), (SC2, MC4), L, Lc)
    tm = _tile(R, MM_TILE, LANES)
    tml = _tile(L, MM_TILE, LANES)

    (p01,), _ = _matmul(
        "in_proj_kvu", (2, R // tm), [h2, win],
        [pl.BlockSpec((tm, D), lambda j, i: (i, 0)), pl.BlockSpec((None, D, INS), lambda j, i: (j, 0, 0))],
        [(0, 1, 0, NN)], [jax.ShapeDtypeStruct((2, R, INS), F32)],
        [pl.BlockSpec((None, tm, INS), lambda j, i: (j, i, 0))], _store_all)
    (p27,), (wglu, wbra) = _matmul(
        "in_proj_qg", (6, L // tml), [h2, win],
        [pl.BlockSpec((tml, D), lambda j, i: (i, 0)), pl.BlockSpec((None, D, INS), lambda j, i: (j + 2, 0, 0))],
        [(0, 1, 0, NN)], [jax.ShapeDtypeStruct((6, L, INS), F32)],
        [pl.BlockSpec((None, tml, INS), lambda j, i: (j, i, 0))], _store_all,
        comm=_Gather([bf(w_glu), bf(w_br_attn)]))
    wglu2 = wglu.reshape(W, W)
    wbra2 = wbra.reshape(D, D)

    cos_all, sin_all = _rope_tables(L, Lc)
    cos_l, sin_l = cos_all[:L], sin_all[:L]

    q_rot = _qk_prep("q_prep", p27, 0, HBQ, NQ, L, q_norm_g, cos_l, sin_l)
    k_rot = _qk_prep("k_prep", p01, 0, NKV, NKV, R, k_norm_g, cos_all, sin_all)
    v_hd = _heads_cast("v_heads", p01, 1, NKV, NKV, R)
    attn, (wbrs, wout) = _attn_fwd(q_rot, k_rot, v_hd, QPK, comm=_Gather([bf(w_br_ssm), bf(w_out)]))
    wout2 = wout.reshape(D, D)

    t_rows = _tile(math.gcd(L, Lc), ROW_TILE, SUBLANES)
    nl, ncx = L // t_rows, Lc // t_rows
    nch = nl + ncx
    ns = G // SLAB_GROUPS
    ssm_prim = (ssm_a_re[0], ssm_a_im[0], ssm_log_dt[0], ssm_b_re[0], ssm_b_im[0])
    _, _, bt_re, bt_im = _ssm_discretize(*ssm_prim)
    pw_re, pw_im = _lambda_powers(ssm_a_re[0], ssm_a_im[0], ssm_log_dt[0], ns)
    bd_re = _block_diag(jnp.swapaxes(bt_re, 2, 3))
    bd_im = _block_diag(jnp.swapaxes(bt_im, 2, 3))
    ct_re = _block_diag(ssm_c_re[0])
    ct_im = _block_diag(-ssm_c_im[0])
    fwd_desc = (False, True)
    adj_desc = (True, False)
    s_bd = jnp.concatenate([bd_re, bd_im], axis=-1)
    s_ct = jnp.concatenate([ct_re, ct_im], axis=-1)
    s_bd16, s_ct16 = s_bd.astype(BF16), s_ct.astype(BF16)
    s_pw = jnp.pad(jnp.transpose(jnp.concatenate([pw_re, pw_im], axis=-1), (1, 2, 0, 3)),
                   ((0, 0), (0, 0), (0, 2 * SCAN_TAPS - SCAN_TAPS - 1), (0, 0)))
    s_tab = _carry_tables(pw_re, pw_im, fwd_desc)
    s_tabc = _carry_tables(pw_re, -pw_im, adj_desc)
    order = [lambda i: (i + nl) % nch, lambda i: nch - 1 - i]
    order_adj = [lambda i: (nch - 1 - i + nl) % nch, lambda i: i]
    y0, st0, (wg2,) = _ssm_fwd("ssm_fwd0", 0, p01, 1, s_bd, s_pw, s_tab, s_ct16, fwd_desc[0], order[0], t_rows, R,
                               comm=_Gather([bft(w_ffn2_gate)]))
    y1, st1, (wu2,) = _ssm_fwd("ssm_fwd1", 1, p01, 1, s_bd, s_pw, s_tab, s_ct16, fwd_desc[1], order[1], t_rows, R,
                               comm=_Gather([bft(w_ffn2_up)]))
    states = [st0, st1]

    tr = _row_tile(L, 0)
    rowW = pl.BlockSpec((tr, W), lambda i: (i, 0))
    vecW = pl.BlockSpec((1, W), lambda i: (0, 0))
    u_lat = pl.BlockSpec((None, tr, W), lambda i: (1, i, 0))

    def ssm_post(i, u, ya, yb, dvec):
        sv = dvec * u + ya + yb
        return [sv, _gelu(sv)], []

    (ssm_out, yg), _, _ = _rowwise(
        "ssm_post", L // tr, [p01, y0, y1, ssm_d], [u_lat, rowW, rowW, vecW],
        [jax.ShapeDtypeStruct((L, W), F32), jax.ShapeDtypeStruct((L, W), BF16)], [rowW, rowW], [], ssm_post)

    tnw = _tile(W, MM_TILE, LANES)

    def glu_epilogue(accs, ins, outs, pids):
        z = accs[0] + ins[3][...]
        outs[0][...] = z
        outs[1][...] = (_gelu(ins[2][...]) * _sigmoid(z)).astype(BF16)

    (z_glu, y2), _ = _matmul(
        "glu", (L // tml, W // tnw), [yg, wglu2, ssm_out, b_glu],
        [pl.BlockSpec((tml, W), lambda i, n: (i, 0)), pl.BlockSpec((W, tnw), lambda i, n: (0, n)),
         pl.BlockSpec((tml, tnw), lambda i, n: (i, n)), pl.BlockSpec((1, tnw), lambda i, n: (0, n))],
        [(0, 1, 0, NN)], [jax.ShapeDtypeStruct((L, W), F32), jax.ShapeDtypeStruct((L, W), BF16)],
        [pl.BlockSpec((tml, tnw), lambda i, n: (i, n))] * 2, glu_epilogue)

    tnd = _tile(D, MM_TILE, LANES)
    out_ld = pl.BlockSpec((tml, tnd), lambda i, n: (i, n))
    (br_a,), _ = _matmul(
        "br_attn", (L // tml, D // tnd), [attn, wbra2],
        [pl.BlockSpec((tml, D), lambda i, n: (i, 0)), pl.BlockSpec((D, tnd), lambda i, n: (0, n))],
        [(0, 1, 0, NN)], [jax.ShapeDtypeStruct((L, D), F32)], [out_ld], _store_all)

    cb = wbrs.shape[-1]
    gpb = INS // cb

    def gate_spec(first_shard):
        return pl.BlockSpec((None, tml, cb), lambda i, j: (first_shard + j // gpb, i, j % gpb))

    def merge_epilogue(accs, ins, outs, pids):
        br = accs[0]
        outs[0][...] = br
        outs[1][...] = (_sigmoid(ins[2][...]) * ins[4][...] + _sigmoid(ins[3][...]) * br).astype(BF16)

    col_blk = pl.BlockSpec((tml, cb), lambda i, j: (i, j))
    (br_s, merged), _ = _matmul(
        "br_ssm_merge", (L // tml, N_DEV), [y2, wbrs, p27, p27, br_a],
        [pl.BlockSpec((tml, W), lambda i, j: (i, 0)), pl.BlockSpec((None, W, cb), lambda i, j: (j, 0, 0)),
         gate_spec(2), gate_spec(4), col_blk],
        [(0, 1, 0, NN)], [jax.ShapeDtypeStruct((L, D), F32), jax.ShapeDtypeStruct((L, D), BF16)],
        [col_blk, col_blk], merge_epilogue)

    def out_epilogue(accs, ins, outs, pids):
        outs[0][...] = accs[0]
        outs[1][...] = ins[2][...] + ins[3][G2:G2 + 1, :] * accs[0]

    (mix, x2_), _ = _matmul(
        "out_proj", (L // tml, D // tnd), [merged, wout2, xc1, tab2],
        [pl.BlockSpec((tml, D), lambda i, n: (i, 0)), pl.BlockSpec((D, tnd), lambda i, n: (0, n)), out_ld,
         pl.BlockSpec((tab2.shape[0], tnd), lambda i, n: (0, n))],
        [(0, 1, 0, NN)], [jax.ShapeDtypeStruct((L, D), F32)] * 2, [out_ld, out_ld], out_epilogue)

    h3 = _norm_mod_fwd("nm3_fwd", x2_, tab3, GAM3, (SH3, SH3), (SC3, SC3), L, 0)
    a3, b3, s3, (wd2,) = _ffn_up("ffn2", h3, wg2, wu2, comm=_Gather([bf(w_ffn2_down)]))
    f3, x3, _ = _ffn_down("ffn2", s3, wd2, x2_, tab2, (G3, G3), L)

    trd = _row_tile(L, 0)
    rowD = pl.BlockSpec((trd, D), lambda i: (i, 0))

    def loss_fn(i, yv, t):
        err = yv - t
        return [err * (1.0 / D)], [_colsum(err * err)]

    (dx3,), (sq,), _ = _rowwise("loss", L // trd, [x3, tgt], [rowD, rowD],
                                [jax.ShapeDtypeStruct((L, D), F32)], [rowD], [D], loss_fn)
    loss = lax.psum(0.5 * jnp.sum(sq) / D, ("x", "y", "c"))

    core = lax.axis_index("c").astype(jnp.int32).reshape(1)
    chip = (2 * lax.axis_index("x") + lax.axis_index("y")).astype(jnp.int32).reshape(1)

    def pair_sums(tag, grads, halves):
        return [_pair_sum("pair_%s%d" % (tag, k), g_, h_, core) for k, (g_, h_) in enumerate(zip(grads, halves))]

    df3, (dg3, _) = _gate_bwd("gate3_bwd", dx3, f3, tab3, (G3, G3), 0.5, L, 0)
    dwd2, _ = _ffn_dwd("ffn2b", s3, df3)
    da3, db3, half_wd2 = _ffn_ds("ffn2b", df3, wd2, a3, b3, comm=_SiblingSwap([dwd2]))
    (p_wd2,) = pair_sums("wd2", [dwd2], half_wd2)
    dwg2, dwu2, (l_wd2,) = _ffn_dwgu("ffn2b", h3, da3, db3, comm=_ChipExchange([p_wd2]))
    dh3, half_wgu2 = _ffn_dh("ffn2b", da3, db3, wg2, wu2, comm=_SiblingSwap([dwg2, dwu2]))
    p_wg2, p_wu2 = pair_sums("wgu2", [dwg2, dwu2], half_wgu2)
    dx2, (dsh3, dsc3, _, _, dgam3) = _norm_mod_bwd("nm3_bwd", x2_, dh3, tab3, GAM3, (SC3, SC3), L, 0, dres=dx3)

    dmix, (dg2, _) = _gate_bwd("gate2_bwd", dx2, mix, tab3, (G2, G2), 1.0, L, 0)

    def dmerged_epilogue(accs, ins, outs, pids):
        dm = accs[0]
        ga, gs = _sigmoid(ins[2][...]), _sigmoid(ins[3][...])
        outs[0][...] = (ga * dm).astype(BF16)
        outs[1][...] = (gs * dm).astype(BF16)
        outs[2][...] = (dm * ins[4][...] * ga * (1.0 - ga)).astype(BF16)
        outs[3][...] = (dm * ins[5][...] * gs * (1.0 - gs)).astype(BF16)

    dgate_spec = pl.BlockSpec((None, tml, cb), lambda i, j: (j // gpb, i, j % gpb))
    (d_br_a, d_br_s, dg_a, dg_s), _ = _matmul(
        "dmerged", (L // tml, N_DEV), [dmix, wout2, p27, p27, br_a, br_s],
        [pl.BlockSpec((tml, D), lambda i, j: (i, 0)), pl.BlockSpec((cb, D), lambda i, j: (j, 0)),
         gate_spec(2), gate_spec(4), col_blk, col_blk],
        [(0, 1, 0, NT)],
        [jax.ShapeDtypeStruct((L, D), BF16)] * 2 + [jax.ShapeDtypeStruct((2, L, INS), BF16)] * 2,
        [col_blk, col_blk, dgate_spec, dgate_spec], dmerged_epilogue)

    def wgrad(name, a_mat, b_mat, tmo, tno):
        ka, ma = a_mat.shape
        _, nb_ = b_mat.shape
        return _matmul(
            name, (ma // tmo, nb_ // tno), [a_mat, b_mat],
            [pl.BlockSpec((ka, tmo), lambda m, n: (0, m)), pl.BlockSpec((ka, tno), lambda m, n: (0, n))],
            [(0, 1, 0, TN)], [jax.ShapeDtypeStruct((ma, nb_), BF16)],
            [pl.BlockSpec((tmo, tno), lambda m, n: (m, n))], _store_all)[0][0]

    dwout = wgrad("dw_out", merged, dmix, tnd, tnd)
    dwbra = wgrad("dw_br_attn", attn, d_br_a, tnd, tnd)
    (d_attn,), _ = _matmul(
        "d_attn", (L // tml, D // tnd), [d_br_a, wbra2],
        [pl.BlockSpec((tml, D), lambda i, n: (i, 0)), pl.BlockSpec((tnd, D), lambda i, n: (n, 0))],
        [(0, 1, 0, NT)], [jax.ShapeDtypeStruct((L, D), BF16)], [out_ld], _store_all)

    (dwbrs,), _ = _matmul(
        "dw_br_ssm", (N_DEV,), [y2, d_br_s],
        [pl.BlockSpec((L, W), lambda j: (0, 0)), pl.BlockSpec((L, cb), lambda j: (0, j))],
        [(0, 1, 0, TN)], [jax.ShapeDtypeStruct((N_DEV, W, cb), BF16)],
        [pl.BlockSpec((None, W, cb), lambda j: (j, 0, 0))], _store_all)

    def dy2_epilogue(accs, ins, outs, pids):
        dy2 = accs[0]
        sg = _sigmoid(ins[2][...])
        outs[0][...] = dy2 * sg
        outs[1][...] = (dy2 * _gelu(ins[3][...]) * sg * (1.0 - sg)).astype(BF16)

    wn_blk = pl.BlockSpec((tml, tnw), lambda i, n, k: (i, n))
    (dyg1, dz), _ = _matmul(
        "d_y2", (L // tml, W // tnw, N_DEV), [d_br_s, wbrs, z_glu, ssm_out],
        [pl.BlockSpec((tml, cb), lambda i, n, k: (i, k)), pl.BlockSpec((None, tnw, cb), lambda i, n, k: (k, n, 0)),
         wn_blk, wn_blk],
        [(0, 1, 0, NT)], [jax.ShapeDtypeStruct((L, W), F32), jax.ShapeDtypeStruct((L, W), BF16)],
        [wn_blk, wn_blk], dy2_epilogue, acc_shapes=[(tml, tnw)], nk=N_DEV)

    dwglu = wgrad("dw_glu", yg, dz, tnw, tnw)
    mix_grads = [dwout.reshape(N_DEV, D // N_DEV, D), dwbra.reshape(N_DEV, D // N_DEV, D), dwbrs,
                 dwglu.reshape(N_DEV, W // N_DEV, W)]

    def dssm_epilogue(accs, ins, outs, pids):
        outs[0][...] = (accs[0] + ins[2][...]) * _gelu_grad(ins[3][...])

    wn2 = pl.BlockSpec((tml, tnw), lambda i, n: (i, n))
    (dssm,), _ = _matmul(
        "d_ssm", (L // tml, W // tnw), [dz, wglu2, dyg1, ssm_out],
        [pl.BlockSpec((tml, W), lambda i, n: (i, 0)), pl.BlockSpec((tnw, W), lambda i, n: (n, 0)), wn2, wn2],
        [(0, 1, 0, NT)], [jax.ShapeDtypeStruct((L, W), F32)], [wn2], dssm_epilogue)

    dssm_all = jnp.concatenate([dssm, jnp.zeros((Lc, W), F32)], axis=0)
    du0, dbd0, dcd0, dlam0, (l_wg2, *half_mix) = _ssm_bwd(
        "ssm_bwd0", 0, dssm_all, p01, 1, states[0], s_ct, s_pw, s_tabc, s_bd16, adj_desc[0], order_adj[0], t_rows, R,
        comm=_Both([_ChipExchange([p_wg2]), _SiblingSwap(mix_grads)]))
    p_wout, p_wbra, p_wbrs, p_wglu = pair_sums("mix", mix_grads, half_mix)
    du1, dbd1, dcd1, dlam1, (l_wu2,) = _ssm_bwd(
        "ssm_bwd1", 1, dssm_all, p01, 1, states[1], s_ct, s_pw, s_tabc, s_bd16, adj_desc[1], order_adj[1], t_rows, R,
        comm=_ChipExchange([p_wu2]))

    trr = _row_tile(L, Lc)
    nlt = L // trr
    rowR = pl.BlockSpec((trr, W), lambda i: (i, 0))

    def du_fn(i, dua, dub, dsv, dvec, u):
        lat = (i < nlt).astype(F32)
        return [dua + dub + lat * (dvec * dsv)], [lat * _colsum(dsv * u)]

    (du_all,), (d_ssm_d,), _ = _rowwise(
        "du_combine", R // trr, [du0, du1, dssm_all, ssm_d, p01],
        [rowR, rowR, rowR, pl.BlockSpec((1, W), lambda i: (0, 0)), pl.BlockSpec((None, trr, W), lambda i: (1, i, 0))],
        [jax.ShapeDtypeStruct((R, W), BF16)], [rowR], [W], du_fn)

    def dz_sum(i, dzv):
        return [], [_colsum(dzv.astype(F32))]

    _, (d_b_glu,), _ = _rowwise("db_glu", L // tr, [dz], [rowW], [], [], [W], dz_sum)

    dbd, dcd, dlam = jnp.stack([dbd0, dbd1]), jnp.stack([dcd0, dcd1]), jnp.stack([dlam0, dlam1])
    dbt_re = jnp.swapaxes(_block_diag_extract(dbd[..., :SW], E, P), 2, 3)
    dbt_im = jnp.swapaxes(_block_diag_extract(dbd[..., SW:], E, P), 2, 3)
    dl_re, dl_im = dlam[:, :, 0, :SW].reshape(2, G, P), dlam[:, :, 0, SW:].reshape(2, G, P)
    _, vjp = jax.vjp(_ssm_discretize, *ssm_prim)
    d_a_re, d_a_im, d_ldt, d_b_re, d_b_im = vjp((dl_re, dl_im, dbt_re, dbt_im))
    d_c_re = jnp.swapaxes(_block_diag_extract(dcd[:, :, :SW, :], P, E), 2, 3)
    d_c_im = -jnp.swapaxes(_block_diag_extract(dcd[:, :, SW:, :], P, E), 2, 3)

    early_g = [d_a_re, d_a_im, d_ldt, d_b_re, d_b_im, d_c_re, d_c_im, d_ssm_d, d_b_glu]
    early_w = [ssm_a_re, ssm_a_im, ssm_log_dt, ssm_b_re, ssm_b_im, ssm_c_re, ssm_c_im, ssm_d, b_glu]
    early_m = [m_ssm_a_re, m_ssm_a_im, m_ssm_log_dt, m_ssm_b_re, m_ssm_b_im, m_ssm_c_re, m_ssm_c_im, m_ssm_d, m_b_glu]
    early_v = [v_ssm_a_re, v_ssm_a_im, v_ssm_log_dt, v_ssm_b_re, v_ssm_b_im, v_ssm_c_re, v_ssm_c_im, v_ssm_d, v_b_glu]
    early_shapes = [a.shape for a in early_w]
    early_rows = -(-sum(-(-math.prod(s) // LANES) for s in early_shapes) // 256) * 256

    dq_rot, dk_rot, dv_hd, (l_wout, l_wbra, l_wbrs, l_wglu, early_parts) = _attn_bwd(
        q_rot, k_rot, v_hd, d_attn, QPK,
        comm=_Both([_ChipExchange([p_wout, p_wbra, p_wbrs, p_wglu]), _Gather([_pack(early_g, early_rows)])]))
    dq_pre, d_qg = _qk_prep_bwd("q_prep_bwd", dq_rot, p27, 0, HBQ, NQ, L, q_norm_g, cos_l, sin_l)
    dk_pre, d_kg = _qk_prep_bwd("k_prep_bwd", dk_rot, p01, 0, NKV, NKV, R, k_norm_g, cos_all, sin_all)
    dv_pre = _heads_merge("dv_merge", dv_hd)

    def lat_blocks(a):
        return jnp.pad(a, ((0, 0), (0, Lc), (0, 0)))

    dp = jnp.concatenate([
        jnp.concatenate([dk_pre[0], dv_pre], axis=1)[None], du_all[None],
        lat_blocks(dq_pre), lat_blocks(dg_a), lat_blocks(dg_s)], axis=0)

    tmo = _tile(D, MM_TILE, LANES)
    (dwin,), _ = _matmul(
        "dw_in", (N_DEV, D // tmo), [h2, dp],
        [pl.BlockSpec((R, tmo), lambda j, m: (0, m)), pl.BlockSpec((None, R, INS), lambda j, m: (j, 0, 0))],
        [(0, 1, 0, TN)], [jax.ShapeDtypeStruct((N_DEV, D, INS), BF16)],
        [pl.BlockSpec((None, tmo, INS), lambda j, m: (j, m, 0))], _store_all)
    tnh = _tile(D, MM_TILE_NT, LANES)
    (dh2,), half_win = _matmul(
        "d_h2", (R // tm, D // tnh), [dp, win],
        [pl.BlockSpec((N_DEV, tm, INS), lambda i, n: (0, i, 0)),
         pl.BlockSpec((N_DEV, tnh, INS), lambda i, n: (0, n, 0))],
        [(0, 1, 0, NT, N_DEV)], [jax.ShapeDtypeStruct((R, D), F32)], [pl.BlockSpec((tm, tnh), lambda i, n: (i, n))],
        _store_all, comm=_SiblingSwap([dwin]))
    (p_win,) = pair_sums("win", [dwin], half_win)
    dxc1, (dsh2, dsc2, dmc3, dmc4, dgam2) = _norm_mod_bwd(
        "nm2_bwd", xc1, dh2, tab3, GAM2, (SC2, MC4), L, Lc, dres=dx2)

    df1, (dg1, dmc2) = _gate_bwd("gate1_bwd", dxc1, f1, tab3, (G1, MC2), 0.5, L, Lc)
    dwd1, _ = _ffn_dwd("ffn1b", s1, df1)
    da1, db1, half_wd1 = _ffn_ds("ffn1b", df1, wd1, a1, b1, comm=_SiblingSwap([dwd1]))
    (p_wd1,) = pair_sums("wd1", [dwd1], half_wd1)
    dwg1, dwu1, (l_wd1,) = _ffn_dwgu("ffn1b", h1, da1, db1, comm=_ChipExchange([p_wd1]))
    dh1, (l_win, *half_wgu1) = _ffn_dh(
        "ffn1b", da1, db1, wg1, wu1, comm=_Both([_ChipExchange([p_win]), _SiblingSwap([dwg1, dwu1])]))
    p_wg1, p_wu1 = pair_sums("wgu1", [dwg1, dwu1], half_wgu1)

    def adam_item(p, l_, w_, m_, v_):
        return (p, l_, w_[0], m_[0], v_[0])

    def adam_item_t(p, l_, w_, m_, v_):
        return (p, l_, held_t(w_), held_t(m_), held_t(v_))

    ready_a = [adam_item(p_wd1, l_wd1, w_ffn1_down, m_w_ffn1_down, v_w_ffn1_down),
               adam_item(p_win, l_win, w_in, m_w_in, v_w_in),
               adam_item(p_wglu, l_wglu, w_glu, m_w_glu, v_w_glu),
               adam_item(p_wbra, l_wbra, w_br_attn, m_w_br_attn, v_w_br_attn),
               adam_item(p_wbrs, l_wbrs, w_br_ssm, m_w_br_ssm, v_w_br_ssm)]
    ready_b = [adam_item(p_wout, l_wout, w_out, m_w_out, v_w_out),
               adam_item_t(p_wg2, l_wg2, w_ffn2_gate, m_w_ffn2_gate, v_w_ffn2_gate),
               adam_item_t(p_wu2, l_wu2, w_ffn2_up, m_w_ffn2_up, v_w_ffn2_up),
               adam_item(p_wd2, l_wd2, w_ffn2_down, m_w_ffn2_down, v_w_ffn2_down)]
    dxc0, (dsh1, dsc1, dmc0, dmc1, dgam1) = _norm_mod_bwd(
        "nm1_bwd", xc0, dh1, tab3, GAM1, (SC1, MC1), L, Lc, dres=dxc1)
    grad_x = dxc0[:L][None]

    dmod_lat = jnp.concatenate([dsh1, dsc1, dg1, dsh2, dsc2, dg2, dsh3, dsc3, dg3], axis=1)
    dmod_ctx = jnp.concatenate([dmc0, dmc1, dmc2, dmc3, dmc4, jnp.zeros((1, 4 * D), F32)], axis=1)
    dmod_pack = at_row(dmod_lat, 0, SUBLANES) + at_row(dmod_ctx, 1, SUBLANES)
    adam_a, (l_wg1, dmod_g) = _owner_adam(
        "adam_ready_a", ready_a, chip, comm=_Both([_ChipExchange([p_wg1]), _Gather([dmod_pack])]))
    dmod_all = dmod_g.reshape(N_DEV * SUBLANES, 9 * D)
    dmod_cols = lax.dynamic_slice_in_dim(dmod_all, me * MODW, MODW, axis=1)
    (g_wmod, dl_wmod, nm_wmod, nv_wmod, dsilu), _ = _mod_bwd_adam(
        cs, dmod_cols, w_mod[0], m_w_mod[0], v_w_mod[0])
    sg_cc = jax.nn.sigmoid(c_ctx)
    d_c_ctx = dsilu[8] * (sg_cc * (1.0 + c_ctx * (1.0 - sg_cc)))
    g_bmod, dl_bmod, nm_bmod, nv_bmod = _bias_adam(dmod_all, b_mod, m_b_mod, v_b_mod)

    dgam_all = jnp.concatenate([dgam1, dgam2, dgam3], axis=0)
    late_g = [d_c_ctx, d_qg, d_kg, dgam_all]
    late_w = [c_ctx, q_norm_g, k_norm_g, ng_full]
    late_m = [m_c_ctx, m_q_norm_g, m_k_norm_g, m_ng_full]
    late_v = [v_c_ctx, v_q_norm_g, v_k_norm_g, v_ng_full]
    late_shapes = [a.shape for a in late_w]
    late_rows = -(-sum(-(-math.prod(s) // LANES) for s in late_shapes) // SUBLANES) * SUBLANES
    adam_b, (l_wu1, late_parts) = _owner_adam(
        "adam_ready_b", ready_b, chip,
        comm=_Both([_ChipExchange([p_wu1]), _Gather([_pack(late_g, late_rows)])]))
    adam_ready = adam_a + adam_b
    late_out =_sum_adam("small_adam_late", late_parts, _pack(late_w, late_rows), _pack(late_m, late_rows),
                         _pack(late_v, late_rows))
    early_out = _sum_adam("small_adam_s5", early_parts, _pack(early_w, early_rows), _pack(early_m, early_rows),
                          _pack(early_v, early_rows))

    def my_norm_cols(a):
        return lax.dynamic_slice_in_dim(a, me * dn, dn, axis=1)[None]

    small = []
    for lo, eo in zip(late_out, early_out):
        c_ctx_, qg_, kg_, ng_ = _unpack(lo, late_shapes)
        small.append([c_ctx_, qg_, kg_] + _unpack(eo, early_shapes) + [my_norm_cols(ng_)])
    sm_g, sm_dl, sm_m, sm_v = small

    adam_last, _ = _owner_adam(
        "adam_last", [adam_item_t(p_wg1, l_wg1, w_ffn1_gate, m_w_ffn1_gate, v_w_ffn1_gate),
                      adam_item_t(p_wu1, l_wu1, w_ffn1_up, m_w_ffn1_up, v_w_ffn1_up)], chip)
    transposed = (0, 1, 8, 9)
    big_out = [[(jnp.swapaxes(o, 0, 1) if k in transposed else o)[None] for o in grp_]
               for k, grp_ in enumerate(adam_last + adam_ready)]

    def leaf(kind):
        sm = (sm_g, sm_dl, sm_m, sm_v)[kind]
        mod = (g_wmod, dl_wmod, nm_wmod, nv_wmod)[kind][None]
        bmod = (g_bmod, dl_bmod, nm_bmod, nv_bmod)[kind]
        big = [b[kind] for b in big_out]
        (c_ctx_, qg_, kg_, a_re_, a_im_, ldt_, b_re_, b_im_, c_re_, c_im_, sd_, bglu_, ng_) = sm
        return [c_ctx_, mod, bmod, ng_, big[0], big[1], big[2], big[3], qg_, kg_, a_re_, a_im_, ldt_, b_re_, b_im_,
                c_re_, c_im_, sd_, big[4], bglu_, big[5], big[6], big[7], big[8], big[9], big[10]]

    return tuple([loss, grad_x] + leaf(0) + leaf(1) + leaf(2) + leaf(3))
```
